```python
import math
import jax, jax.numpy as jnp
from jax import lax
import numpy as np

D_MODEL = 1024
BATCH = 8
SEQ = 4096
DEPTH = 2

N_A_LAYERS = DEPTH // 2
N_B_LAYERS = DEPTH - N_A_LAYERS

A_HEADS = 16
A_HEAD_DIM = D_MODEL // A_HEADS
A_QKV_WIDTH = 3 * A_HEADS * A_HEAD_DIM
Q_BLOCK = 128

B_GROUPS = ((128, 1), (512, 4), (2048, 16))
B_N_GROUPS = len(B_GROUPS)
B_HEADS_PER_GROUP = 8
B_HEAD_DIM = 64
B_WINDOW_STEPS = 128
B_Q_WIDTH = B_N_GROUPS * B_HEADS_PER_GROUP * B_HEAD_DIM
B_OUT_WIDTH = B_HEADS_PER_GROUP * B_HEAD_DIM
B_KV_WIDTH = 2 * B_Q_WIDTH
ALIBI_TOTAL_HEADS = B_N_GROUPS * B_HEADS_PER_GROUP

D_FF = 2816
CONV_WIDTH = 3

RMS_EPS = 1e-6

kernel_name = "yoco_fox_dilated_convffn_hybrid"


def rms_norm(x, g):
    xf = x.astype(jnp.float32)
    y = xf * lax.rsqrt(jnp.mean(xf * xf, axis=-1, keepdims=True) + RMS_EPS)
    return (y * g.astype(jnp.float32)).astype(x.dtype)


def conv_ffn(h, w_up, conv_w, conv_b, w_down):
    u = h @ w_up
    c = u.shape[-1]
    u = lax.conv_general_dilated(
        u, conv_w[:, None, :].astype(u.dtype), window_strides=(1,),
        padding=[(CONV_WIDTH - 1, 0)],
        dimension_numbers=("NWC", "WIO", "NWC"),
        feature_group_count=c) + conv_b
    a, gate = jnp.split(u, 2, axis=-1)
    return (jax.nn.silu(gate) * a) @ w_down


def forgetting_attention(q, k, v, log_f):
    S = q.shape[1]
    dh = q.shape[-1]
    scale = dh ** -0.5
    c = jnp.cumsum(log_f.astype(jnp.float32), axis=1).transpose(0, 2, 1)
    outs = []
    for i in range(S // Q_BLOCK):
        q0 = i * Q_BLOCK
        q1 = q0 + Q_BLOCK
        s = jnp.einsum("bqhd,bkhd->bhqk", q[:, q0:q1], k[:, :q1]).astype(jnp.float32) * scale
        s = s + c[:, :, q0:q1, None] - c[:, :, None, :q1]
        causal = jnp.arange(q0, q1)[:, None] >= jnp.arange(q1)[None, :]
        p = jax.nn.softmax(jnp.where(causal, s, -jnp.inf), axis=-1)
        outs.append(jnp.einsum("bhqk,bkhd->bqhd", p.astype(v.dtype), v[:, :q1]))
    return jnp.concatenate(outs, axis=1)


def dilated_branch(q, k, v, dil, slopes):
    B, S, H, Dh = q.shape
    W = B_WINDOW_STEPS
    period = dil * W
    Sp = -(-S // period) * period
    nb = Sp // period
    pad = ((0, 0), (0, Sp - S), (0, 0), (0, 0))

    def blocks(t):
        return jnp.pad(t, pad).reshape(B, nb, W, dil, H, Dh)

    def with_prev(t):
        prev = jnp.pad(t, ((0, 0), (1, 0), (0, 0), (0, 0), (0, 0), (0, 0)))[:, :nb]
        return jnp.concatenate([prev, t], axis=2)

    qb = blocks(q)
    kk = with_prev(blocks(k))
    vv = with_prev(blocks(v))
    s = jnp.einsum("bnirhd,bnjrhd->bnrhij", qb, kk).astype(jnp.float32) * (Dh ** -0.5)
    qi = jnp.arange(W)[:, None]
    kj = jnp.arange(2 * W)[None, :]
    dist = qi + W - kj
    band = (dist >= 0) & (dist <= W)
    first = (jnp.arange(nb)[:, None, None] == 0) & (kj < W)[None]
    valid = band[None] & ~first
    bias = -slopes[:, None, None] * (dil * dist).astype(jnp.float32)
    s = jnp.where(valid[None, :, None, None], s + bias, -jnp.inf)
    m = jnp.max(s, axis=-1, keepdims=True)
    p = jnp.exp(s - m)
    l = jnp.sum(p, axis=-1, keepdims=True)
    o = jnp.einsum("bnrhij,bnjrhd->bnirhd", (p / l).astype(v.dtype), vv)
    lse = (m + jnp.log(l))[..., 0]
    o = o.reshape(B, Sp, H, Dh)[:, :S]
    lse = lse.transpose(0, 1, 4, 2, 3).reshape(B, Sp, H)[:, :S]
    return o, lse


def _fwd_setup_inputs(seed: int = 0) -> dict:
    key = jax.random.key(seed)
    ks = jax.random.split(key, 16)
    f32 = jnp.float32

    def nrm(k, shape, fan_in):
        return jax.random.normal(k, shape, f32) * (fan_in ** -0.5)

    def gain(k, shape):
        return 1.0 + 0.05 * jax.random.normal(k, shape, f32)

    return {
        "x": jax.random.normal(ks[0], (BATCH, SEQ, D_MODEL), f32),
        "a_w_in": nrm(ks[1], (N_A_LAYERS, D_MODEL, A_QKV_WIDTH + A_HEADS), D_MODEL),
        "a_b_f": 2.0 + 0.1 * jax.random.normal(ks[2], (N_A_LAYERS, A_HEADS), f32),
        "a_w_out": nrm(ks[3], (N_A_LAYERS, A_HEADS * A_HEAD_DIM, D_MODEL), A_HEADS * A_HEAD_DIM),
        "b_w_q": nrm(ks[4], (N_B_LAYERS, D_MODEL, B_Q_WIDTH), D_MODEL),
        "b_w_out": nrm(ks[5], (N_B_LAYERS, B_OUT_WIDTH, D_MODEL), B_OUT_WIDTH),
        "kv_norm_g": gain(ks[6], (D_MODEL,)),
        "w_kv": nrm(ks[7], (D_MODEL, B_KV_WIDTH), D_MODEL),
        "mix_norm_g": gain(ks[8], (DEPTH, D_MODEL)),
        "ffn_norm_g": gain(ks[9], (DEPTH, D_MODEL)),
        "ffn_w_up": nrm(ks[10], (DEPTH, D_MODEL, 2 * D_FF), D_MODEL),
        "ffn_conv_w": nrm(ks[11], (DEPTH, CONV_WIDTH, 2 * D_FF), CONV_WIDTH),
        "ffn_conv_b": 0.02 * jax.random.normal(ks[12], (DEPTH, 2 * D_FF), f32),
        "ffn_w_down": nrm(ks[13], (DEPTH, D_FF, D_MODEL), D_FF),
        "final_norm_g": gain(ks[14], (D_MODEL,)),
    }


def _fwd_reference(x, a_w_in, a_b_f, a_w_out, b_w_q, b_w_out, kv_norm_g, w_kv,
              mix_norm_g, ffn_norm_g, ffn_w_up, ffn_conv_w, ffn_conv_b, ffn_w_down,
              final_norm_g):
    B, S, D = x.shape
    slopes = jnp.exp2(-8.0 * jnp.arange(1, ALIBI_TOTAL_HEADS + 1, dtype=jnp.float32)
                      / ALIBI_TOTAL_HEADS).reshape(B_N_GROUPS, B_HEADS_PER_GROUP)
    kv = None
    for layer in range(DEPTH):
        if layer < N_A_LAYERS:
            h = rms_norm(x, mix_norm_g[layer])
            proj = h @ a_w_in[layer]
            qkv = proj[..., :A_QKV_WIDTH].reshape(B, S, 3, A_HEADS, A_HEAD_DIM)
            log_f = jax.nn.log_sigmoid(
                proj[..., A_QKV_WIDTH:].astype(jnp.float32) + a_b_f[layer].astype(jnp.float32))
            o = forgetting_attention(qkv[:, :, 0], qkv[:, :, 1], qkv[:, :, 2], log_f)
            x = x + o.reshape(B, S, A_HEADS * A_HEAD_DIM) @ a_w_out[layer]
        else:
            if kv is None:
                kv = (rms_norm(x, kv_norm_g) @ w_kv).reshape(
                    B, S, 2, B_N_GROUPS, B_HEADS_PER_GROUP, B_HEAD_DIM)
            bl = layer - N_A_LAYERS
            h = rms_norm(x, mix_norm_g[layer])
            q = (h @ b_w_q[bl]).reshape(B, S, B_N_GROUPS, B_HEADS_PER_GROUP, B_HEAD_DIM)
            outs, lses = [], []
            for g, (window, dil) in enumerate(B_GROUPS):
                o_g, lse_g = dilated_branch(q[:, :, g], kv[:, :, 0, g], kv[:, :, 1, g],
                                            dil, slopes[g])
                outs.append(o_g)
                lses.append(lse_g)
            alpha = jax.nn.softmax(jnp.stack(lses, axis=0), axis=0)
            o = jnp.sum(alpha[..., None].astype(outs[0].dtype) * jnp.stack(outs, axis=0), axis=0)
            x = x + o.reshape(B, S, B_OUT_WIDTH) @ b_w_out[bl]
        h = rms_norm(x, ffn_norm_g[layer])
        x = x + conv_ffn(h, ffn_w_up[layer], ffn_conv_w[layer], ffn_conv_b[layer],
                         ffn_w_down[layer])
    return rms_norm(x, final_norm_g)


import jax as _jax
import jax.numpy as _jnp

TWIN_FORMAT = 'train_step'
FWD_PARAMS = ['x', 'a_w_in', 'a_b_f', 'a_w_out', 'b_w_q', 'b_w_out', 'kv_norm_g', 'w_kv', 'mix_norm_g', 'ffn_norm_g', 'ffn_w_up', 'ffn_conv_w', 'ffn_conv_b', 'ffn_w_down', 'final_norm_g']
TWIN_WEIGHTS = ['a_w_in', 'a_b_f', 'a_w_out', 'b_w_q', 'b_w_out', 'kv_norm_g', 'w_kv', 'mix_norm_g', 'ffn_norm_g', 'ffn_w_up', 'ffn_conv_w', 'ffn_conv_b', 'ffn_w_down', 'final_norm_g']
TWIN_DIFF_INPUT = 'x'
TWIN_INPUTS = ['x', 'a_w_in', 'a_b_f', 'a_w_out', 'b_w_q', 'b_w_out', 'kv_norm_g', 'w_kv', 'mix_norm_g', 'ffn_norm_g', 'ffn_w_up', 'ffn_conv_w', 'ffn_conv_b', 'ffn_w_down', 'final_norm_g', 'loss_target', 'm_a_w_in', 'm_a_b_f', 'm_a_w_out', 'm_b_w_q', 'm_b_w_out', 'm_kv_norm_g', 'm_w_kv', 'm_mix_norm_g', 'm_ffn_norm_g', 'm_ffn_w_up', 'm_ffn_conv_w', 'm_ffn_conv_b', 'm_ffn_w_down', 'm_final_norm_g', 'v_a_w_in', 'v_a_b_f', 'v_a_w_out', 'v_b_w_q', 'v_b_w_out', 'v_kv_norm_g', 'v_w_kv', 'v_mix_norm_g', 'v_ffn_norm_g', 'v_ffn_w_up', 'v_ffn_conv_w', 'v_ffn_conv_b', 'v_ffn_w_down', 'v_final_norm_g']
TWIN_OUTPUTS = ['loss', 'grad_x', 'grad_a_w_in', 'grad_a_b_f', 'grad_a_w_out', 'grad_b_w_q', 'grad_b_w_out', 'grad_kv_norm_g', 'grad_w_kv', 'grad_mix_norm_g', 'grad_ffn_norm_g', 'grad_ffn_w_up', 'grad_ffn_conv_w', 'grad_ffn_conv_b', 'grad_ffn_w_down', 'grad_final_norm_g', 'delta_a_w_in', 'delta_a_b_f', 'delta_a_w_out', 'delta_b_w_q', 'delta_b_w_out', 'delta_kv_norm_g', 'delta_w_kv', 'delta_mix_norm_g', 'delta_ffn_norm_g', 'delta_ffn_w_up', 'delta_ffn_conv_w', 'delta_ffn_conv_b', 'delta_ffn_w_down', 'delta_final_norm_g', 'new_m_a_w_in', 'new_m_a_b_f', 'new_m_a_w_out', 'new_m_b_w_q', 'new_m_b_w_out', 'new_m_kv_norm_g', 'new_m_w_kv', 'new_m_mix_norm_g', 'new_m_ffn_norm_g', 'new_m_ffn_w_up', 'new_m_ffn_conv_w', 'new_m_ffn_conv_b', 'new_m_ffn_w_down', 'new_m_final_norm_g', 'new_v_a_w_in', 'new_v_a_b_f', 'new_v_a_w_out', 'new_v_b_w_q', 'new_v_b_w_out', 'new_v_kv_norm_g', 'new_v_w_kv', 'new_v_mix_norm_g', 'new_v_ffn_norm_g', 'new_v_ffn_w_up', 'new_v_ffn_conv_w', 'new_v_ffn_conv_b', 'new_v_ffn_w_down', 'new_v_final_norm_g']
TWIN_LEAF_KINDS = {'loss': 'loss', 'grad_x': 'grad_x', 'grad_a_w_in': 'grad_w', 'grad_a_b_f': 'grad_w', 'grad_a_w_out': 'grad_w', 'grad_b_w_q': 'grad_w', 'grad_b_w_out': 'grad_w', 'grad_kv_norm_g': 'grad_w', 'grad_w_kv': 'grad_w', 'grad_mix_norm_g': 'grad_w', 'grad_ffn_norm_g': 'grad_w', 'grad_ffn_w_up': 'grad_w', 'grad_ffn_conv_w': 'grad_w', 'grad_ffn_conv_b': 'grad_w', 'grad_ffn_w_down': 'grad_w', 'grad_final_norm_g': 'grad_w', 'delta_a_w_in': 'delta_w', 'delta_a_b_f': 'delta_w', 'delta_a_w_out': 'delta_w', 'delta_b_w_q': 'delta_w', 'delta_b_w_out': 'delta_w', 'delta_kv_norm_g': 'delta_w', 'delta_w_kv': 'delta_w', 'delta_mix_norm_g': 'delta_w', 'delta_ffn_norm_g': 'delta_w', 'delta_ffn_w_up': 'delta_w', 'delta_ffn_conv_w': 'delta_w', 'delta_ffn_conv_b': 'delta_w', 'delta_ffn_w_down': 'delta_w', 'delta_final_norm_g': 'delta_w', 'new_m_a_w_in': 'new_m', 'new_m_a_b_f': 'new_m', 'new_m_a_w_out': 'new_m', 'new_m_b_w_q': 'new_m', 'new_m_b_w_out': 'new_m', 'new_m_kv_norm_g': 'new_m', 'new_m_w_kv': 'new_m', 'new_m_mix_norm_g': 'new_m', 'new_m_ffn_norm_g': 'new_m', 'new_m_ffn_w_up': 'new_m', 'new_m_ffn_conv_w': 'new_m', 'new_m_ffn_conv_b': 'new_m', 'new_m_ffn_w_down': 'new_m', 'new_m_final_norm_g': 'new_m', 'new_v_a_w_in': 'new_v', 'new_v_a_b_f': 'new_v', 'new_v_a_w_out': 'new_v', 'new_v_b_w_q': 'new_v', 'new_v_b_w_out': 'new_v', 'new_v_kv_norm_g': 'new_v', 'new_v_w_kv': 'new_v', 'new_v_mix_norm_g': 'new_v', 'new_v_ffn_norm_g': 'new_v', 'new_v_ffn_w_up': 'new_v', 'new_v_ffn_conv_w': 'new_v', 'new_v_ffn_conv_b': 'new_v', 'new_v_ffn_w_down': 'new_v', 'new_v_final_norm_g': 'new_v'}


def _forward(args):
    return _fwd_reference(*[args[k] for k in FWD_PARAMS])


def _output_shape():
    out = _jax.eval_shape(lambda: _forward(_fwd_setup_inputs(0)))
    return out.shape, out.dtype

N_MICROBATCH = 1
ADAM_LR = 0.001
ADAM_B1 = 0.9
ADAM_B2 = 0.999
ADAM_EPS = 1e-08
ADAM_WD = 0.01
ADAM_STEP = 10
PER_EXAMPLE_BATCH_AXIS = {'x': 0, 'loss_target': 0}
SHARED_INPUTS = []
_WEIGHT_DTYPES = {'a_w_in': _jnp.float32, 'a_b_f': _jnp.float32, 'a_w_out': _jnp.float32, 'b_w_q': _jnp.float32, 'b_w_out': _jnp.float32, 'kv_norm_g': _jnp.float32, 'w_kv': _jnp.float32, 'mix_norm_g': _jnp.float32, 'ffn_norm_g': _jnp.float32, 'ffn_w_up': _jnp.float32, 'ffn_conv_w': _jnp.float32, 'ffn_conv_b': _jnp.float32, 'ffn_w_down': _jnp.float32, 'final_norm_g': _jnp.float32}
MOMENT_SCALE = {'a_w_in': 7.511762e-02, 'a_b_f': 9.979137e-01, 'a_w_out': 8.794197e-02, 'b_w_q': 3.377564e-02, 'b_w_out': 5.150226e-02, 'kv_norm_g': 6.713553e-02, 'w_kv': 3.806346e-02, 'mix_norm_g': 9.873292e-02, 'ffn_norm_g': 1.297398e-01, 'ffn_w_up': 5.543601e-02, 'ffn_conv_w': 5.497023e-02, 'ffn_conv_b': 5.442214e-02, 'ffn_w_down': 9.092826e-02, 'final_norm_g': 3.197271e+01}


def _to_microbatches(a, axis):
    t = _jnp.moveaxis(a, axis, 0)
    t = t.reshape((N_MICROBATCH, t.shape[0] // N_MICROBATCH) + t.shape[1:])
    return _jnp.moveaxis(t, 1, axis + 1)


def setup_inputs(seed: int = 0) -> dict:
    inp = _fwd_setup_inputs(seed)
    key = _jax.random.fold_in(_jax.random.key(seed), 7919)
    shape, _ = _output_shape()
    out = dict(inp)
    out["loss_target"] = _jax.random.normal(_jax.random.fold_in(key, 0), shape, _jnp.float32)
    for i, name in enumerate(TWIN_WEIGHTS):
        w = inp[name].astype(_jnp.float32)
        if MOMENT_SCALE is None:
            s = _jnp.sqrt(_jnp.mean(_jnp.square(w)) + 1e-30)
        else:
            s = MOMENT_SCALE[name]
        km, kv = _jax.random.split(_jax.random.fold_in(key, i + 1))
        out[name] = w
        out["m_" + name] = s * _jax.random.normal(km, w.shape, _jnp.float32)
        out["v_" + name] = (s * s) * _jax.random.uniform(kv, w.shape, _jnp.float32, 0.5, 1.5)
    if N_MICROBATCH > 1:
        for name, axis in PER_EXAMPLE_BATCH_AXIS.items():
            out[name] = _to_microbatches(out[name], axis)
    return {'x': out['x'], 'a_w_in': out['a_w_in'], 'a_b_f': out['a_b_f'], 'a_w_out': out['a_w_out'], 'b_w_q': out['b_w_q'], 'b_w_out': out['b_w_out'], 'kv_norm_g': out['kv_norm_g'], 'w_kv': out['w_kv'], 'mix_norm_g': out['mix_norm_g'], 'ffn_norm_g': out['ffn_norm_g'], 'ffn_w_up': out['ffn_w_up'], 'ffn_conv_w': out['ffn_conv_w'], 'ffn_conv_b': out['ffn_conv_b'], 'ffn_w_down': out['ffn_w_down'], 'final_norm_g': out['final_norm_g'], 'loss_target': out['loss_target'], 'm_a_w_in': out['m_a_w_in'], 'm_a_b_f': out['m_a_b_f'], 'm_a_w_out': out['m_a_w_out'], 'm_b_w_q': out['m_b_w_q'], 'm_b_w_out': out['m_b_w_out'], 'm_kv_norm_g': out['m_kv_norm_g'], 'm_w_kv': out['m_w_kv'], 'm_mix_norm_g': out['m_mix_norm_g'], 'm_ffn_norm_g': out['m_ffn_norm_g'], 'm_ffn_w_up': out['m_ffn_w_up'], 'm_ffn_conv_w': out['m_ffn_conv_w'], 'm_ffn_conv_b': out['m_ffn_conv_b'], 'm_ffn_w_down': out['m_ffn_w_down'], 'm_final_norm_g': out['m_final_norm_g'], 'v_a_w_in': out['v_a_w_in'], 'v_a_b_f': out['v_a_b_f'], 'v_a_w_out': out['v_a_w_out'], 'v_b_w_q': out['v_b_w_q'], 'v_b_w_out': out['v_b_w_out'], 'v_kv_norm_g': out['v_kv_norm_g'], 'v_w_kv': out['v_w_kv'], 'v_mix_norm_g': out['v_mix_norm_g'], 'v_ffn_norm_g': out['v_ffn_norm_g'], 'v_ffn_w_up': out['v_ffn_w_up'], 'v_ffn_conv_w': out['v_ffn_conv_w'], 'v_ffn_conv_b': out['v_ffn_conv_b'], 'v_ffn_w_down': out['v_ffn_w_down'], 'v_final_norm_g': out['v_final_norm_g']}


def _loss(weights, diff, rest, loss_target):
    with _jax.named_scope("forward"):
        args = {**rest, TWIN_DIFF_INPUT: diff, **{k: w.astype(_WEIGHT_DTYPES[k]) for k, w in weights.items()}}
        y = _forward(args)
    with _jax.named_scope("loss_head"):
        err = _jnp.square(y.astype(_jnp.float32) - loss_target)
        return 0.5 * _jnp.sum(_jnp.mean(err, axis=-1)) if err.ndim else 0.5 * err


def _adamw(w, g, m, v):
    m = ADAM_B1 * m + (1.0 - ADAM_B1) * g
    v = ADAM_B2 * v + (1.0 - ADAM_B2) * _jnp.square(g)
    m_hat = m / (1.0 - ADAM_B1 ** ADAM_STEP)
    v_hat = v / (1.0 - ADAM_B2 ** ADAM_STEP)
    delta = -ADAM_LR * (m_hat / (_jnp.sqrt(v_hat) + ADAM_EPS) + ADAM_WD * w)
    return delta, m, v


def reference(x, a_w_in, a_b_f, a_w_out, b_w_q, b_w_out, kv_norm_g, w_kv, mix_norm_g, ffn_norm_g, ffn_w_up, ffn_conv_w, ffn_conv_b, ffn_w_down, final_norm_g, loss_target, m_a_w_in, m_a_b_f, m_a_w_out, m_b_w_q, m_b_w_out, m_kv_norm_g, m_w_kv, m_mix_norm_g, m_ffn_norm_g, m_ffn_w_up, m_ffn_conv_w, m_ffn_conv_b, m_ffn_w_down, m_final_norm_g, v_a_w_in, v_a_b_f, v_a_w_out, v_b_w_q, v_b_w_out, v_kv_norm_g, v_w_kv, v_mix_norm_g, v_ffn_norm_g, v_ffn_w_up, v_ffn_conv_w, v_ffn_conv_b, v_ffn_w_down, v_final_norm_g):
    given = dict(x=x, a_w_in=a_w_in, a_b_f=a_b_f, a_w_out=a_w_out, b_w_q=b_w_q, b_w_out=b_w_out, kv_norm_g=kv_norm_g, w_kv=w_kv, mix_norm_g=mix_norm_g, ffn_norm_g=ffn_norm_g, ffn_w_up=ffn_w_up, ffn_conv_w=ffn_conv_w, ffn_conv_b=ffn_conv_b, ffn_w_down=ffn_w_down, final_norm_g=final_norm_g, loss_target=loss_target, m_a_w_in=m_a_w_in, m_a_b_f=m_a_b_f, m_a_w_out=m_a_w_out, m_b_w_q=m_b_w_q, m_b_w_out=m_b_w_out, m_kv_norm_g=m_kv_norm_g, m_w_kv=m_w_kv, m_mix_norm_g=m_mix_norm_g, m_ffn_norm_g=m_ffn_norm_g, m_ffn_w_up=m_ffn_w_up, m_ffn_conv_w=m_ffn_conv_w, m_ffn_conv_b=m_ffn_conv_b, m_ffn_w_down=m_ffn_w_down, m_final_norm_g=m_final_norm_g, v_a_w_in=v_a_w_in, v_a_b_f=v_a_b_f, v_a_w_out=v_a_w_out, v_b_w_q=v_b_w_q, v_b_w_out=v_b_w_out, v_kv_norm_g=v_kv_norm_g, v_w_kv=v_w_kv, v_mix_norm_g=v_mix_norm_g, v_ffn_norm_g=v_ffn_norm_g, v_ffn_w_up=v_ffn_w_up, v_ffn_conv_w=v_ffn_conv_w, v_ffn_conv_b=v_ffn_conv_b, v_ffn_w_down=v_ffn_w_down, v_final_norm_g=v_final_norm_g)
    weights = {n: given[n] for n in TWIN_WEIGHTS}
    shared = {n: given[n] for n in SHARED_INPUTS}
    per_example = {n: given[n] for n in ['x']}
    grad_fn = _jax.value_and_grad(_loss, argnums=(0, 1))

    def one_microbatch(ex, loss_target):
        ex = dict(ex)
        diff = ex.pop(TWIN_DIFF_INPUT)
        return grad_fn(weights, diff, {**shared, **ex}, loss_target)

    if N_MICROBATCH == 1:
        loss, (grad_w, grad_x) = one_microbatch(per_example, given["loss_target"])
    else:
        def body(carry, xs):
            loss_sum, grad_sum = carry
            l_k, (gw_k, gx_k) = one_microbatch(xs[0], xs[1])
            with _jax.named_scope("update"):
                return (loss_sum + l_k, _jax.tree.map(_jnp.add, grad_sum, gw_k)), gx_k

        init = (_jnp.zeros((), _jnp.float32), _jax.tree.map(_jnp.zeros_like, weights))
        (loss, grad_w), grad_x = _jax.lax.scan(body, init, (per_example, given["loss_target"]))
    with _jax.named_scope("update"):
        delta_w, new_m, new_v = {}, {}, {}
        for n in TWIN_WEIGHTS:
            delta_w[n], new_m[n], new_v[n] = _adamw(weights[n], grad_w[n], given["m_" + n], given["v_" + n])
    return (loss, grad_x, *[grad_w[n] for n in TWIN_WEIGHTS], *[delta_w[n] for n in TWIN_WEIGHTS],
            *[new_m[n] for n in TWIN_WEIGHTS], *[new_v[n] for n in TWIN_WEIGHTS])
```

```python
import functools
import math

import jax
import jax.numpy as jnp
from jax import lax
from jax.experimental import pallas as pl
from jax.experimental.pallas import tpu as pltpu

F32 = jnp.float32
BF16 = jnp.bfloat16

S = 4096
D = 1024
N_DEV = 8
A_HEADS = 16
HEAD_DIM = 64
A_QKV = 3072
A_PROJ_PAD = 3200
B_Q = 1536
B_OUT = 512
B_KV = 3072
B_W = 128
B_DILS = (1, 4, 16)
D_FF = 2816
RMS_EPS = 1e-6
SCALE = HEAD_DIM ** -0.5
NEG = -1e30

ADAM_LR = 0.001
ADAM_B1 = 0.9
ADAM_B2 = 0.999
ADAM_EPS = 1e-08
ADAM_WD = 0.01
ADAM_STEP = 10

LANES = 128
VMEM_LIMIT = 56 * 1024 * 1024
MESH = pl.DeviceIdType.MESH
ANY = pl.BlockSpec(memory_space=pl.ANY)

NT_DIMS = (((1,), (1,)), ((), ()))
TN_DIMS = (((0,), (0,)), ((), ()))
NN_DIMS = (((1,), (0,)), ((), ()))


def _params(*sem):
    return pltpu.CompilerParams(dimension_semantics=sem if sem else None, vmem_limit_bytes=VMEM_LIMIT)


def _dot(a, b, dims=NN_DIMS):
    return lax.dot_general(a, b, dims, preferred_element_type=F32)


def _split_dot(x, mat, pieces):
    out = None
    rem = x
    for _ in range(pieces):
        part = rem.astype(BF16)
        rem = rem - part.astype(F32)
        d = _dot(part, mat)
        out = d if out is None else out + d
    return out


def _pick(n, prefs):
    for p in prefs:
        if n % p == 0:
            return p
    return n


def _all_gather(arrays, name):
    n = len(arrays)

    def body(*refs):
        ins = refs[:n]
        outs = refs[n:2 * n]
        send_sems, recv_sems, local_sems = refs[2 * n:]
        x, y, c = lax.axis_index("x"), lax.axis_index("y"), lax.axis_index("c")
        me, sibling = (x, y, c), (x, y, 1 - c)
        chips = [(1 - x, y), (x, 1 - y), (1 - x, 1 - y)]

        def slot(a, px, py, pc):
            return outs[a].at[4 * px + 2 * py + pc]

        def copy(a, k, block, to, src=None):
            return pltpu.make_async_remote_copy(
                src_ref=slot(a, *block) if src is None else src, dst_ref=slot(a, *block),
                send_sem=send_sems.at[a, k], recv_sem=recv_sems.at[a, k],
                device_id=to, device_id_type=MESH)

        mine = [pltpu.make_async_copy(ins[a], slot(a, *me), local_sems.at[a]) for a in range(n)]
        for cp in mine:
            cp.start()
        first = []
        for a in range(n):
            first.append(copy(a, 0, me, sibling, src=ins[a]))
            first += [copy(a, 1 + j, me, (*chip, c), src=ins[a]) for j, chip in enumerate(chips)]
        for cp in first:
            cp.start()
        passed = []
        for j, chip in enumerate(chips):
            for a in range(n):
                copy(a, 1 + j, (*chip, c), me).wait_recv()
                fwd = copy(a, 4 + j, (*chip, c), sibling)
                fwd.start()
                passed.append(fwd)
        for a in range(n):
            copy(a, 0, sibling, me).wait_recv()
            for j, chip in enumerate(chips):
                copy(a, 4 + j, (*chip, 1 - c), me).wait_recv()
        for cp in first + passed:
            cp.wait_send()
        for cp in mine:
            cp.wait()

    return pl.pallas_call(
        body, name=name,
        out_shape=[jax.ShapeDtypeStruct((N_DEV,) + a.shape, a.dtype) for a in arrays],
        in_specs=[ANY] * n, out_specs=[ANY] * n,
        scratch_shapes=[pltpu.SemaphoreType.DMA((n, 7)), pltpu.SemaphoreType.DMA((n, 7)),
                        pltpu.SemaphoreType.DMA((n,))],
    )(*arrays)


def _sibling_exchange(arrays, name):
    n = len(arrays)

    def body(*refs):
        ins = refs[:n]
        outs = refs[n:2 * n]
        send_sems, recv_sems = refs[2 * n:]
        x, y, c = lax.axis_index("x"), lax.axis_index("y"), lax.axis_index("c")
        copies = []
        for a in range(n):
            for chip in range(4):
                copies.append(pltpu.make_async_remote_copy(
                    src_ref=ins[a].at[chip, 1 - c], dst_ref=outs[a].at[chip],
                    send_sem=send_sems.at[a, chip], recv_sem=recv_sems.at[a, chip],
                    device_id=(x, y, 1 - c), device_id_type=MESH))
        for cp in copies:
            cp.start()
        for cp in copies:
            cp.wait()

    return pl.pallas_call(
        body, name=name,
        out_shape=[jax.ShapeDtypeStruct((4,) + a.shape[2:], a.dtype) for a in arrays],
        in_specs=[ANY] * n, out_specs=[ANY] * n,
        scratch_shapes=[pltpu.SemaphoreType.DMA((n, 4)), pltpu.SemaphoreType.DMA((n, 4))],
    )(*arrays)


def _chip_exchange(arrays, name):
    n = len(arrays)

    def body(*refs):
        ins = refs[:n]
        outs = refs[n:2 * n]
        send_sems, recv_sems = refs[2 * n:]
        x, y, c = lax.axis_index("x"), lax.axis_index("y"), lax.axis_index("c")
        chips = [(1 - x, y), (x, 1 - y), (1 - x, 1 - y)]
        copies = []
        for a in range(n):
            for j, (qx, qy) in enumerate(chips):
                copies.append(pltpu.make_async_remote_copy(
                    src_ref=ins[a].at[2 * qx + qy], dst_ref=outs[a].at[j],
                    send_sem=send_sems.at[a, j], recv_sem=recv_sems.at[a, j],
                    device_id=(qx, qy, c), device_id_type=MESH))
        for cp in copies:
            cp.start()
        for cp in copies:
            cp.wait()

    return pl.pallas_call(
        body, name=name,
        out_shape=[jax.ShapeDtypeStruct((3,) + a.shape[1:], a.dtype) for a in arrays],
        in_specs=[ANY] * n, out_specs=[ANY] * n,
        scratch_shapes=[pltpu.SemaphoreType.DMA((n, 3)), pltpu.SemaphoreType.DMA((n, 3))],
    )(*arrays)


def _matmul(a, b, *, mode, out_dtype, name, tm=512, tn=512, tk=None, res=None):
    if mode == "nn":
        (M, K), (K2, N) = a.shape, b.shape
    elif mode == "nt":
        (M, K), (N, K2) = a.shape, b.shape
    else:
        (K, M), (K2, N) = a.shape, b.shape
    assert K == K2, (a.shape, b.shape, mode)
    tm, tn = min(tm, M), min(tn, N)
    tk = K if tk is None else min(tk, K)
    assert M % tm == 0 and N % tn == 0 and K % tk == 0, (M, N, K, tm, tn, tk)
    nk = K // tk
    dims = {"nn": NN_DIMS, "nt": NT_DIMS, "tn": TN_DIMS}[mode]
    if mode == "tn":
        a_spec = pl.BlockSpec((tk, tm), lambda i, j, k: (k, i))
    else:
        a_spec = pl.BlockSpec((tm, tk), lambda i, j, k: (i, k))
    if mode == "nt":
        b_spec = pl.BlockSpec((tn, tk), lambda i, j, k: (j, k))
    else:
        b_spec = pl.BlockSpec((tk, tn), lambda i, j, k: (k, j))
    o_spec = pl.BlockSpec((tm, tn), lambda i, j, k: (i, j))
    has_res = res is not None

    def body(*refs):
        a_ref, b_ref = refs[0], refs[1]
        r_ref = refs[2] if has_res else None
        o_ref = refs[2 + has_res]
        part = _dot(a_ref[...], b_ref[...], dims)

        def finish(total):
            if has_res:
                total = total + r_ref[...]
            o_ref[...] = total.astype(out_dtype)

        if nk == 1:
            finish(part)
        else:
            acc_ref = refs[3 + has_res]
            k = pl.program_id(2)

            @pl.when(k == 0)
            def _():
                acc_ref[...] = part

            @pl.when(k > 0)
            def _():
                acc_ref[...] += part

            @pl.when(k == nk - 1)
            def _():
                finish(acc_ref[...])

    return pl.pallas_call(
        body, name=name, grid=(M // tm, N // tn, nk),
        out_shape=jax.ShapeDtypeStruct((M, N), out_dtype),
        in_specs=[a_spec, b_spec] + ([o_spec] if has_res else []),
        out_specs=o_spec,
        scratch_shapes=[pltpu.VMEM((tm, tn), F32)] if nk > 1 else [],
        compiler_params=_params("parallel", "parallel", "arbitrary"),
    )(*((a, b, res) if has_res else (a, b)))


def _rmsnorm_fwd(x, gains, name, tr=256):
    n = len(gains)

    def body(*refs):
        x_ref = refs[0]
        xv = x_ref[...]
        r = lax.rsqrt(jnp.mean(xv * xv, axis=-1, keepdims=True) + RMS_EPS)
        y = xv * r
        for a in range(n):
            refs[1 + n + a][...] = (y * refs[1 + a][...]).astype(BF16)

    row = pl.BlockSpec((tr, D), lambda i: (i, 0))
    gain = pl.BlockSpec((1, D), lambda i: (0, 0))
    return pl.pallas_call(
        body, name=name, grid=(S // tr,),
        out_shape=[jax.ShapeDtypeStruct((S, D), BF16)] * n,
        in_specs=[row] + [gain] * n, out_specs=[row] * n,
        compiler_params=_params("parallel"),
    )(x, *gains)


def _rmsnorm_bwd(x, dy, g, dres, name, tr=256):
    def body(x_ref, dy_ref, g_ref, dres_ref, dx_ref, dxb_ref, dg_ref):
        xv = x_ref[...]
        dyv = dy_ref[...]
        r = lax.rsqrt(jnp.mean(xv * xv, axis=-1, keepdims=True) + RMS_EPS)
        xhat = xv * r
        dxhat = dyv * g_ref[...]
        mean_term = jnp.mean(dxhat * xhat, axis=-1, keepdims=True)
        dx = r * (dxhat - xhat * mean_term) + dres_ref[...]
        dx_ref[...] = dx
        dxb_ref[...] = dx.astype(BF16)
        part = jnp.sum(dyv * xhat, axis=0, keepdims=True)

        @pl.when(pl.program_id(0) == 0)
        def _():
            dg_ref[...] = part

        @pl.when(pl.program_id(0) > 0)
        def _():
            dg_ref[...] += part

    row = pl.BlockSpec((tr, D), lambda i: (i, 0))
    gain = pl.BlockSpec((1, D), lambda i: (0, 0))
    return pl.pallas_call(
        body, name=name, grid=(S // tr,),
        out_shape=[jax.ShapeDtypeStruct((S, D), F32), jax.ShapeDtypeStruct((S, D), BF16),
                   jax.ShapeDtypeStruct((1, D), F32)],
        in_specs=[row, row, gain, row], out_specs=[row, row, gain],
        compiler_params=_params("arbitrary"),
    )(x, dy, g, dres)


def _final_loss(x, target, g, name, tr=256):
    def body(x_ref, t_ref, g_ref, loss_ref, dx_ref, dxb_ref, dg_ref):
        xv = x_ref[...]
        gv = g_ref[...]
        r = lax.rsqrt(jnp.mean(xv * xv, axis=-1, keepdims=True) + RMS_EPS)
        xhat = xv * r
        err = xhat * gv - t_ref[...]
        row_loss = jnp.mean(err * err, axis=-1, keepdims=True)
        lpart = 0.5 * jnp.sum(row_loss, axis=0, keepdims=True)
        dyv = err / D
        dxhat = dyv * gv
        mean_term = jnp.mean(dxhat * xhat, axis=-1, keepdims=True)
        dx = r * (dxhat - xhat * mean_term)
        dx_ref[...] = dx
        dxb_ref[...] = dx.astype(BF16)
        gpart = jnp.sum(dyv * xhat, axis=0, keepdims=True)

        @pl.when(pl.program_id(0) == 0)
        def _():
            dg_ref[...] = gpart
            loss_ref[...] = jnp.broadcast_to(lpart, loss_ref.shape)

        @pl.when(pl.program_id(0) > 0)
        def _():
            dg_ref[...] += gpart
            loss_ref[...] += jnp.broadcast_to(lpart, loss_ref.shape)

    row = pl.BlockSpec((tr, D), lambda i: (i, 0))
    gain = pl.BlockSpec((1, D), lambda i: (0, 0))
    lspec = pl.BlockSpec((8, LANES), lambda i: (0, 0))
    return pl.pallas_call(
        body, name=name, grid=(S // tr,),
        out_shape=[jax.ShapeDtypeStruct((8, LANES), F32), jax.ShapeDtypeStruct((S, D), F32),
                   jax.ShapeDtypeStruct((S, D), BF16), jax.ShapeDtypeStruct((1, D), F32)],
        in_specs=[row, row, gain], out_specs=[lspec, row, row, gain],
        compiler_params=_params("arbitrary"),
    )(x, target, g)


CONV_TR = 512
CONV_TC = 256
CONV_NJ = D_FF // CONV_TC
HALO = 16


def _causal_taps(cur_ref, prev_ref, first):
    xv = cur_ref[...].astype(F32)
    pv = prev_ref[...].astype(F32)
    p1 = jnp.where(first, 0.0, pv[HALO - 1:HALO, :])
    p2 = jnp.where(first, 0.0, pv[HALO - 2:HALO - 1, :])
    row = lax.broadcasted_iota(jnp.int32, xv.shape, 0)
    xm1 = jnp.where(row == 0, p1, pltpu.roll(xv, 1, 0))
    xm2 = jnp.where(row == 0, p2, jnp.where(row == 1, p1, pltpu.roll(xv, 2, 0)))
    return xv, xm1, xm2


def _conv_specs():
    def prev_row(i):
        return jnp.maximum(i * (CONV_TR // HALO) - 1, 0)
    ua = pl.BlockSpec((CONV_TR, CONV_TC), lambda i, j: (i, j))
    ug = pl.BlockSpec((CONV_TR, CONV_TC), lambda i, j: (i, j + CONV_NJ))
    pa = pl.BlockSpec((HALO, CONV_TC), lambda i, j: (prev_row(i), j))
    pg = pl.BlockSpec((HALO, CONV_TC), lambda i, j: (prev_row(i), j + CONV_NJ))
    wa = pl.BlockSpec((3, CONV_TC), lambda i, j: (0, j))
    wg = pl.BlockSpec((3, CONV_TC), lambda i, j: (0, j + CONV_NJ))
    ba = pl.BlockSpec((1, CONV_TC), lambda i, j: (0, j))
    bg = pl.BlockSpec((1, CONV_TC), lambda i, j: (0, j + CONV_NJ))
    return [ua, pa, ug, pg, wa, wg, ba, bg]


def _convgate_fwd(u, w, b, name):
    def body(ua, pa, ug, pg, wa, wg, ba, bg, o_ref):
        first = pl.program_id(0) == 0
        x0, x1, x2 = _causal_taps(ua, pa, first)
        ac = wa[0:1, :] * x2 + wa[1:2, :] * x1 + wa[2:3, :] * x0 + ba[...]
        x0, x1, x2 = _causal_taps(ug, pg, first)
        gc = wg[0:1, :] * x2 + wg[1:2, :] * x1 + wg[2:3, :] * x0 + bg[...]
        sg = 1.0 / (1.0 + jnp.exp(-gc))
        o_ref[...] = (gc * sg * ac).astype(BF16)

    return pl.pallas_call(
        body, name=name, grid=(S // CONV_TR, CONV_NJ),
        out_shape=jax.ShapeDtypeStruct((S, D_FF), BF16),
        in_specs=_conv_specs(),
        out_specs=pl.BlockSpec((CONV_TR, CONV_TC), lambda i, j: (i, j)),
        compiler_params=_params("parallel", "parallel"),
    )(u, u, u, u, w, w, b, b)


def _convgate_bwd(u, w, b, dact, name):
    def body(ua, pa, ug, pg, wa, wg, ba, bg, d_ref, da_ref, dg_ref, dwa_ref, dwg_ref, dba_ref, dbg_ref):
        i = pl.program_id(1)
        first = i == 0
        a0, a1, a2 = _causal_taps(ua, pa, first)
        ac = wa[0:1, :] * a2 + wa[1:2, :] * a1 + wa[2:3, :] * a0 + ba[...]
        g0, g1, g2 = _causal_taps(ug, pg, first)
        gc = wg[0:1, :] * g2 + wg[1:2, :] * g1 + wg[2:3, :] * g0 + bg[...]
        sg = 1.0 / (1.0 + jnp.exp(-gc))
        dact_v = d_ref[...].astype(F32)
        da = dact_v * (gc * sg)
        dg = dact_v * ac * (sg * (1.0 + gc * (1.0 - sg)))
        da_ref[...] = da.astype(BF16)
        dg_ref[...] = dg.astype(BF16)

        def col(v):
            return jnp.sum(v, axis=0, keepdims=True)

        parts = [col(da * a2), col(da * a1), col(da * a0), col(dg * g2), col(dg * g1), col(dg * g0),
                 col(da), col(dg)]

        @pl.when(first)
        def _():
            for k in range(3):
                dwa_ref[k:k + 1, :] = parts[k]
                dwg_ref[k:k + 1, :] = parts[3 + k]
            dba_ref[...] = parts[6]
            dbg_ref[...] = parts[7]

        @pl.when(i > 0)
        def _():
            for k in range(3):
                dwa_ref[k:k + 1, :] += parts[k]
                dwg_ref[k:k + 1, :] += parts[3 + k]
            dba_ref[...] += parts[6]
            dbg_ref[...] += parts[7]

    def swap(spec):
        return pl.BlockSpec(spec.block_shape, lambda j, i, f=spec.index_map: f(i, j))

    blk = pl.BlockSpec((CONV_TR, CONV_TC), lambda j, i: (i, j))
    w3 = pl.BlockSpec((3, CONV_TC), lambda j, i: (0, j))
    b1 = pl.BlockSpec((1, CONV_TC), lambda j, i: (0, j))
    return pl.pallas_call(
        body, name=name, grid=(CONV_NJ, S // CONV_TR),
        out_shape=[jax.ShapeDtypeStruct((S, D_FF), BF16), jax.ShapeDtypeStruct((S, D_FF), BF16),
                   jax.ShapeDtypeStruct((3, D_FF), F32), jax.ShapeDtypeStruct((3, D_FF), F32),
                   jax.ShapeDtypeStruct((1, D_FF), F32), jax.ShapeDtypeStruct((1, D_FF), F32)],
        in_specs=[swap(s) for s in _conv_specs()] + [blk],
        out_specs=[blk, blk, w3, w3, b1, b1],
        compiler_params=_params("parallel", "arbitrary"),
    )(u, u, u, u, w, w, b, b, dact)


def _conv_input_bwd(d, w, name):
    n_i = S // CONV_TR

    def body(d_ref, n_ref, w_ref, o_ref):
        last = pl.program_id(0) == n_i - 1
        dv = d_ref[...].astype(F32)
        nv = n_ref[...].astype(F32)
        n1 = jnp.where(last, 0.0, nv[0:1, :])
        n2 = jnp.where(last, 0.0, nv[1:2, :])
        row = lax.broadcasted_iota(jnp.int32, dv.shape, 0)
        dp1 = jnp.where(row == CONV_TR - 1, n1, pltpu.roll(dv, CONV_TR - 1, 0))
        dp2 = jnp.where(row == CONV_TR - 1, n2, jnp.where(row == CONV_TR - 2, n1, pltpu.roll(dv, CONV_TR - 2, 0)))
        o_ref[...] = (w_ref[2:3, :] * dv + w_ref[1:2, :] * dp1 + w_ref[0:1, :] * dp2).astype(BF16)

    def next_row(i):
        return jnp.minimum((i + 1) * (CONV_TR // HALO), S // HALO - 1)

    blk = pl.BlockSpec((CONV_TR, CONV_TC), lambda i, j: (i, j))
    return pl.pallas_call(
        body, name=name, grid=(n_i, CONV_NJ),
        out_shape=jax.ShapeDtypeStruct((S, D_FF), BF16),
        in_specs=[blk, pl.BlockSpec((HALO, CONV_TC), lambda i, j: (next_row(i), j)),
                  pl.BlockSpec((3, CONV_TC), lambda i, j: (0, j))],
        out_specs=blk,
        compiler_params=_params("parallel", "parallel"),
    )(d, d, w)


FOX_T = 256
N_PAIRS = A_HEADS // 2


def _lane_masks():
    lane = lax.broadcasted_iota(jnp.int32, (1, LANES), 1)
    return lane, (lane < HEAD_DIM, lane >= HEAD_DIM)


def _fox_prep_fwd(z_t, b, name):
    def body(z_ref, b_ref, c_ref):
        r = lax.broadcasted_iota(jnp.int32, (LANES, LANES), 0)
        cc = lax.broadcasted_iota(jnp.int32, (LANES, LANES), 1)
        upper = (r <= cc).astype(BF16)
        carry = jnp.zeros((A_HEADS, 1), F32)
        for blk in range(S // LANES):
            sl = slice(blk * LANES, (blk + 1) * LANES)
            z = z_ref[:, sl] + b_ref[...]
            lf = jnp.minimum(z, 0.0) - jnp.log(1.0 + jnp.exp(-jnp.abs(z)))
            cs = _split_dot(lf, upper, 3) + carry
            c_ref[:, sl] = cs
            carry = cs[:, LANES - 1:LANES]

    return pl.pallas_call(
        body, name=name, out_shape=jax.ShapeDtypeStruct((A_HEADS, S), F32),
        compiler_params=_params(),
    )(z_t, b)


def _fox_prep_bwd(drow_t, dcol_t, z_t, b, name):
    def body(dr_ref, dc_ref, z_ref, b_ref, dz_ref, db_ref):
        r = lax.broadcasted_iota(jnp.int32, (LANES, LANES), 0)
        cc = lax.broadcasted_iota(jnp.int32, (LANES, LANES), 1)
        lower = (r >= cc).astype(BF16)
        carry = jnp.zeros((A_HEADS, 1), F32)
        db = jnp.zeros((A_HEADS, 1), F32)
        for blk in reversed(range(S // LANES)):
            sl = slice(blk * LANES, (blk + 1) * LANES)
            rc = _split_dot(dr_ref[:, sl] + dc_ref[:, sl], lower, 3) + carry
            carry = rc[:, 0:1]
            z = z_ref[:, sl] + b_ref[...]
            dz = rc / (1.0 + jnp.exp(z))
            dz_ref[:, sl] = dz
            db = db + jnp.sum(dz, axis=1, keepdims=True)
        db_ref[...] = db

    return pl.pallas_call(
        body, name=name,
        out_shape=[jax.ShapeDtypeStruct((A_HEADS, S), F32), jax.ShapeDtypeStruct((A_HEADS, 1), F32)],
        compiler_params=_params(),
    )(drow_t, dcol_t, z_t, b)


def _fox_fwd(qkv, c, c_t2, name):
    t = FOX_T

    def body(q_ref, k_ref, v_ref, c_ref, ct_ref, o_ref, lse_ref):
        hp, qi = pl.program_id(0), pl.program_id(1)
        lane, masks = _lane_masks()
        lane16 = lax.broadcasted_iota(jnp.int32, (1, A_HEADS), 1)
        q = q_ref[...]
        cblk = c_ref[...]
        outs, lses = [], []
        for e in range(2):
            qe = jnp.where(masks[e], q, jnp.zeros_like(q))
            ct = jnp.sum(jnp.where(lane16 == 2 * hp + e, cblk, 0.0), axis=1, keepdims=True)

            def step(j, carry, masked, qe=qe, ct=ct, e=e):
                m, l, acc = carry
                start = pl.multiple_of(j * t, t)
                kb = k_ref[pl.ds(start, t), :]
                vb = v_ref[pl.ds(start, t), :]
                cs = ct_ref[e:e + 1, pl.ds(start, t)]
                s = _dot(qe, kb, NT_DIMS) * SCALE + (ct - cs)
                if masked:
                    rows = lax.broadcasted_iota(jnp.int32, (t, t), 0)
                    cols = lax.broadcasted_iota(jnp.int32, (t, t), 1)
                    s = jnp.where(cols <= rows, s, NEG)
                m_new = jnp.maximum(m, jnp.max(s, axis=1, keepdims=True))
                p = jnp.exp(s - m_new)
                alpha = jnp.exp(m - m_new)
                l = alpha * l + jnp.sum(p, axis=1, keepdims=True)
                acc = alpha * acc + _dot(p.astype(BF16), vb)
                return m_new, l, acc

            init = (jnp.full((t, 1), NEG, F32), jnp.zeros((t, 1), F32), jnp.zeros((t, LANES), F32))
            carry = lax.fori_loop(0, qi, lambda j, cr, step=step: step(j, cr, False), init)
            m, l, acc = step(qi, carry, True)
            outs.append(acc / l)
            lses.append(m + jnp.log(l))
        o_ref[...] = jnp.where(masks[0], outs[0], outs[1]).astype(BF16)
        lse_ref[...] = jnp.where(masks[0], lses[0], lses[1])

    qspec = pl.BlockSpec((t, LANES), lambda h, i: (i, h))
    return pl.pallas_call(
        body, name=name, grid=(N_PAIRS, S // t),
        out_shape=[jax.ShapeDtypeStruct((S, D), BF16), jax.ShapeDtypeStruct((S, D), F32)],
        in_specs=[qspec,
                  pl.BlockSpec((S, LANES), lambda h, i: (0, N_PAIRS + h)),
                  pl.BlockSpec((S, LANES), lambda h, i: (0, 2 * N_PAIRS + h)),
                  pl.BlockSpec((t, A_HEADS), lambda h, i: (i, 0)),
                  pl.BlockSpec((None, 2, S), lambda h, i: (h, 0, 0))],
        out_specs=[qspec, qspec],
        compiler_params=_params("parallel", "parallel"),
    )(qkv, qkv, qkv, c, c_t2)


def _head_rowsum(a, b, name, tr=256):
    C = a.shape[1]

    def body(a_ref, b_ref, o_ref):
        r = lax.broadcasted_iota(jnp.int32, (LANES, LANES), 0) < HEAD_DIM
        cc = lax.broadcasted_iota(jnp.int32, (LANES, LANES), 1) < HEAD_DIM
        same_head = (r == cc).astype(BF16)
        for blk in range(C // LANES):
            sl = slice(blk * LANES, (blk + 1) * LANES)
            prod = a_ref[:, sl].astype(F32) * b_ref[:, sl].astype(F32)
            o_ref[:, sl] = _split_dot(prod, same_head, 2)

    row = pl.BlockSpec((tr, C), lambda i: (i, 0))
    return pl.pallas_call(
        body, name=name, grid=(S // tr,), out_shape=jax.ShapeDtypeStruct((S, C), F32),
        in_specs=[row, row], out_specs=row, compiler_params=_params("parallel"),
    )(a, b)


def _fox_bwd(qkv, do, lse, delta, c, c_t2, name):
    t = FOX_T
    nq = S // t

    def body(q_ref, k_ref, v_ref, do_ref, lse_ref, dl_ref, c_ref, ct_ref, dq_ref, dk_ref, dv_ref, dct_ref, dr_ref):
        hp, kj = pl.program_id(0), pl.program_id(1)

        @pl.when(kj == 0)
        def _():
            dq_ref[...] = jnp.zeros_like(dq_ref)
            dr_ref[...] = jnp.zeros_like(dr_ref)

        lane, masks = _lane_masks()
        lane16 = lax.broadcasted_iota(jnp.int32, (1, A_HEADS), 1)
        k = k_ref[...]
        v = v_ref[...]
        dks, dvs = [], []
        for e in range(2):
            ke = jnp.where(masks[e], k, jnp.zeros_like(k))
            cs = ct_ref[e:e + 1, :]
            lo = e * HEAD_DIM

            def step(i, carry, masked, e=e, ke=ke, cs=cs, lo=lo):
                dk_acc, dv_acc, dc_acc = carry
                r0 = pl.multiple_of(i * t, t)
                qb = q_ref[pl.ds(r0, t), :]
                dob = do_ref[pl.ds(r0, t), :]
                qe = jnp.where(masks[e], qb, jnp.zeros_like(qb))
                doe = jnp.where(masks[e], dob, jnp.zeros_like(dob))
                lse_e = lse_ref[pl.ds(r0, t), lo:lo + 1]
                dl_e = dl_ref[pl.ds(r0, t), lo:lo + 1]
                ct = jnp.sum(jnp.where(lane16 == 2 * hp + e, c_ref[pl.ds(r0, t), :], 0.0), axis=1, keepdims=True)
                s = _dot(qe, k, NT_DIMS) * SCALE + (ct - cs)
                if masked:
                    rows = lax.broadcasted_iota(jnp.int32, (t, t), 0)
                    cols = lax.broadcasted_iota(jnp.int32, (t, t), 1)
                    s = jnp.where(cols <= rows, s, NEG)
                p = jnp.exp(s - lse_e)
                dp = _dot(doe, v, NT_DIMS)
                ds = p * (dp - dl_e)
                pb = p.astype(BF16)
                dsb = ds.astype(BF16)
                dv_acc = dv_acc + _dot(pb, doe, TN_DIMS)
                dk_acc = dk_acc + _dot(dsb, qe, TN_DIMS)
                dq_ref[pl.ds(r0, t), :] += _dot(dsb, ke) * SCALE
                dr_ref[pl.ds(r0, t), :] += jnp.where(masks[e], jnp.sum(ds, axis=1, keepdims=True), 0.0)
                dc_acc = dc_acc + jnp.sum(ds, axis=0, keepdims=True)
                return dk_acc, dv_acc, dc_acc

            init = (jnp.zeros((t, LANES), F32), jnp.zeros((t, LANES), F32), jnp.zeros((1, t), F32))
            carry = step(kj, init, True)
            carry = lax.fori_loop(kj + 1, nq, lambda i, cr, step=step: step(i, cr, False), carry)
            dks.append(carry[0])
            dvs.append(carry[1])
            dct_ref[e:e + 1, :] = -carry[2]
        dk_ref[...] = ((dks[0] + dks[1]) * SCALE).astype(BF16)
        dv_ref[...] = (dvs[0] + dvs[1]).astype(BF16)

    full = lambda off: pl.BlockSpec((S, LANES), lambda h, j, off=off: (0, off + h))
    kv = lambda off: pl.BlockSpec((t, LANES), lambda h, j, off=off: (j, off + h))
    return pl.pallas_call(
        body, name=name, grid=(N_PAIRS, nq),
        out_shape=[jax.ShapeDtypeStruct((S, D), F32), jax.ShapeDtypeStruct((S, D), BF16),
                   jax.ShapeDtypeStruct((S, D), BF16), jax.ShapeDtypeStruct((N_PAIRS, 2, S), F32),
                   jax.ShapeDtypeStruct((S, D), F32)],
        in_specs=[full(0), kv(N_PAIRS), kv(2 * N_PAIRS), full(0), full(0), full(0),
                  pl.BlockSpec((S, A_HEADS), lambda h, j: (0, 0)),
                  pl.BlockSpec((None, 2, t), lambda h, j: (h, 0, j))],
        out_specs=[full(0), kv(0), kv(0), pl.BlockSpec((None, 2, t), lambda h, j: (h, 0, j)), full(0)],
        compiler_params=_params("parallel", "arbitrary"),
    )(qkv, qkv, qkv, do, lse, delta, c, c_t2)


B_PAIRS = 4
B_NB = S // B_W


def _group_consts(g):
    nbs = jnp.where(g == 0, B_NB // B_DILS[0], jnp.where(g == 1, B_NB // B_DILS[1], B_NB // B_DILS[2]))
    dil = jnp.where(g == 0, B_DILS[0], jnp.where(g == 1, B_DILS[1], B_DILS[2]))
    return nbs, dil


def _band(dil):
    qi = lax.broadcasted_iota(jnp.int32, (B_W, B_W), 0)
    kj = lax.broadcasted_iota(jnp.int32, (B_W, B_W), 1)
    dist_c = qi - kj
    dist_p = qi + B_W - kj
    return (dist_c * dil).astype(F32), dist_c >= 0, (dist_p * dil).astype(F32), dist_p <= B_W


def _dil_fwd(qp, kp, vp, slopes, name):
    def body(sl_ref, q_ref, kp_ref, kc_ref, vp_ref, vc_ref, o_ref, lse_ref):
        g, hp, n = pl.program_id(0), pl.program_id(1), pl.program_id(2)
        nbs, dil = _group_consts(g)
        has_prev = (n % nbs) != 0
        lane, masks = _lane_masks()
        bias_c, ok_c, bias_p, ok_p = _band(dil)
        ok_p = jnp.logical_and(ok_p, has_prev)
        q = q_ref[...]
        kc, kpv, vc, vpv = kc_ref[...], kp_ref[...], vc_ref[...], vp_ref[...]
        outs, lses = [], []
        for e in range(2):
            slope = sl_ref[g * 8 + 2 * hp + e]
            qe = jnp.where(masks[e], q, jnp.zeros_like(q))
            sc = jnp.where(ok_c, _dot(qe, kc, NT_DIMS) * SCALE - slope * bias_c, NEG)
            sp = jnp.where(ok_p, _dot(qe, kpv, NT_DIMS) * SCALE - slope * bias_p, NEG)
            m = jnp.maximum(jnp.max(sc, axis=1, keepdims=True), jnp.max(sp, axis=1, keepdims=True))
            pc = jnp.exp(sc - m)
            pp = jnp.exp(sp - m)
            l = jnp.sum(pc, axis=1, keepdims=True) + jnp.sum(pp, axis=1, keepdims=True)
            acc = _dot(pc.astype(BF16), vc) + _dot(pp.astype(BF16), vpv)
            outs.append(acc / l)
            lses.append(m + jnp.log(l))
        o_ref[...] = jnp.where(masks[0], outs[0], outs[1])
        lse_ref[...] = jnp.where(masks[0], lses[0], lses[1])

    cur = pl.BlockSpec((None, B_W, LANES), lambda g, h, n, sl: (g, n, h))
    prev = pl.BlockSpec((None, B_W, LANES), lambda g, h, n, sl: (g, jnp.maximum(n - 1, 0), h))
    return pl.pallas_call(
        body, name=name,
        grid_spec=pltpu.PrefetchScalarGridSpec(
            num_scalar_prefetch=1, grid=(3, B_PAIRS, B_NB),
            in_specs=[cur, prev, cur, prev, cur], out_specs=[cur, cur]),
        out_shape=[jax.ShapeDtypeStruct((3, S, B_OUT), F32), jax.ShapeDtypeStruct((3, S, B_OUT), F32)],
        compiler_params=_params("parallel", "parallel", "parallel"),
    )(slopes, qp, kp, kp, vp, vp)


def _dil_merge(og, lseg, name, tr=256):
    def body(o_ref, l_ref, out_ref, lse_ref):
        l0, l1, l2 = l_ref[0], l_ref[1], l_ref[2]
        m = jnp.maximum(jnp.maximum(l0, l1), l2)
        w0, w1, w2 = jnp.exp(l0 - m), jnp.exp(l1 - m), jnp.exp(l2 - m)
        den = w0 + w1 + w2
        out_ref[...] = ((w0 * o_ref[0] + w1 * o_ref[1] + w2 * o_ref[2]) / den).astype(BF16)
        lse_ref[...] = m + jnp.log(den)

    blk3 = pl.BlockSpec((3, tr, B_OUT), lambda i: (0, i, 0))
    blk = pl.BlockSpec((tr, B_OUT), lambda i: (i, 0))
    return pl.pallas_call(
        body, name=name, grid=(S // tr,),
        out_shape=[jax.ShapeDtypeStruct((S, B_OUT), BF16), jax.ShapeDtypeStruct((S, B_OUT), F32)],
        in_specs=[blk3, blk3], out_specs=[blk, blk], compiler_params=_params("parallel"),
    )(og, lseg)


def _dil_bwd(qp, kp, vp, dop, lsep, dlp, slopes, name):
    def body(sl_ref, qc_ref, qn_ref, kp_ref, kc_ref, vp_ref, vc_ref, doc_ref, don_ref,
             lc_ref, ln_ref, dc_ref, dn_ref, dq_ref, dk_ref, dv_ref):
        g, hp, n = pl.program_id(0), pl.program_id(1), pl.program_id(2)
        nbs, dil = _group_consts(g)
        has_prev = (n % nbs) != 0
        has_next = jnp.logical_and(n + 1 < B_NB, ((n + 1) % nbs) != 0)
        lane, masks = _lane_masks()
        bias_c, ok_c, bias_p, ok_p = _band(dil)
        ok_pp = jnp.logical_and(ok_p, has_prev)
        ok_np = jnp.logical_and(ok_p, has_next)
        qc, qn = qc_ref[...], qn_ref[...]
        kc, kpv, vc, vpv = kc_ref[...], kp_ref[...], vc_ref[...], vp_ref[...]
        doc, don = doc_ref[...], don_ref[...]
        dq = jnp.zeros((B_W, LANES), F32)
        dk = jnp.zeros((B_W, LANES), F32)
        dv = jnp.zeros((B_W, LANES), F32)
        for e in range(2):
            lo = e * HEAD_DIM
            slope = sl_ref[g * 8 + 2 * hp + e]
            qce = jnp.where(masks[e], qc, jnp.zeros_like(qc))
            qne = jnp.where(masks[e], qn, jnp.zeros_like(qn))
            doce = jnp.where(masks[e], doc, jnp.zeros_like(doc))
            done = jnp.where(masks[e], don, jnp.zeros_like(don))
            kce = jnp.where(masks[e], kc, jnp.zeros_like(kc))
            kpe = jnp.where(masks[e], kpv, jnp.zeros_like(kpv))
            lse_c, dl_c = lc_ref[:, lo:lo + 1], dc_ref[:, lo:lo + 1]
            lse_n, dl_n = ln_ref[:, lo:lo + 1], dn_ref[:, lo:lo + 1]
            s = jnp.where(ok_c, _dot(qce, kc, NT_DIMS) * SCALE - slope * bias_c, NEG)
            p = jnp.exp(s - lse_c)
            ds = p * (_dot(doce, vc, NT_DIMS) - dl_c)
            dsb = ds.astype(BF16)
            dq = dq + _dot(dsb, kce)
            dk = dk + _dot(dsb, qce, TN_DIMS)
            dv = dv + _dot(p.astype(BF16), doce, TN_DIMS)
            s = jnp.where(ok_pp, _dot(qce, kpv, NT_DIMS) * SCALE - slope * bias_p, NEG)
            p = jnp.exp(s - lse_c)
            ds = p * (_dot(doce, vpv, NT_DIMS) - dl_c)
            dq = dq + _dot(ds.astype(BF16), kpe)
            s = jnp.where(ok_np, _dot(qne, kc, NT_DIMS) * SCALE - slope * bias_p, NEG)
            p = jnp.exp(s - lse_n)
            ds = p * (_dot(done, vc, NT_DIMS) - dl_n)
            dk = dk + _dot(ds.astype(BF16), qne, TN_DIMS)
            dv = dv + _dot(p.astype(BF16), done, TN_DIMS)
        dq_ref[...] = (dq * SCALE).astype(BF16)
        dk_ref[...] = (dk * SCALE).astype(BF16)
        dv_ref[...] = dv.astype(BF16)

    cur = pl.BlockSpec((None, B_W, LANES), lambda g, h, n, sl: (g, n, h))
    prev = pl.BlockSpec((None, B_W, LANES), lambda g, h, n, sl: (g, jnp.maximum(n - 1, 0), h))
    nxt = pl.BlockSpec((None, B_W, LANES), lambda g, h, n, sl: (g, jnp.minimum(n + 1, B_NB - 1), h))
    return pl.pallas_call(
        body, name=name,
        grid_spec=pltpu.PrefetchScalarGridSpec(
            num_scalar_prefetch=1, grid=(3, B_PAIRS, B_NB),
            in_specs=[cur, nxt, prev, cur, prev, cur, cur, nxt, cur, nxt, cur, nxt],
            out_specs=[cur, cur, cur]),
        out_shape=[jax.ShapeDtypeStruct((3, S, B_OUT), BF16)] * 3,
        compiler_params=_params("parallel", "parallel", "parallel"),
    )(slopes, qp, qp, kp, kp, vp, vp, dop, dop, lsep, lsep, dlp, dlp)


def _rows_block(shape, max_bytes=2 * 1024 * 1024):
    rows, cols = shape
    padded_cols = -(-cols // LANES) * LANES
    for tr in (1024, 512, 256, 128, 64, 32, 16):
        if rows % tr == 0 and tr * padded_cols * 4 <= max_bytes:
            return tr
    return rows


def _pair_add(grads, recv, core, name):
    _, _, R, C = grads.shape
    tr = _rows_block((R, C))

    def body(core_ref, g_ref, r_ref, o_ref):
        o_ref[...] = (g_ref[...].astype(F32) + r_ref[...].astype(F32)).astype(o_ref.dtype)

    return pl.pallas_call(
        body, name=name,
        grid_spec=pltpu.PrefetchScalarGridSpec(
            num_scalar_prefetch=1, grid=(4, R // tr),
            in_specs=[pl.BlockSpec((None, None, tr, C), lambda q, i, cr: (q, cr[0], i, 0)),
                      pl.BlockSpec((None, tr, C), lambda q, i, cr: (q, i, 0))],
            out_specs=pl.BlockSpec((None, tr, C), lambda q, i, cr: (q, i, 0))),
        out_shape=jax.ShapeDtypeStruct((4, R, C), grads.dtype),
        compiler_params=_params("parallel", "parallel"),
    )(core, grads, recv)


def _adam_update(w, m, v, g):
    m_new = ADAM_B1 * m + (1.0 - ADAM_B1) * g
    v_new = ADAM_B2 * v + (1.0 - ADAM_B2) * (g * g)
    m_hat = m_new / (1.0 - ADAM_B1 ** ADAM_STEP)
    v_hat = v_new / (1.0 - ADAM_B2 ** ADAM_STEP)
    delta = -ADAM_LR * (m_hat / (jnp.sqrt(v_hat) + ADAM_EPS) + ADAM_WD * w)
    return delta, m_new, v_new


def _adamw_sharded(w, m, v, chip_sums, recv, chip, name):
    R, C = w.shape
    tr = _rows_block((R, C), max_bytes=1024 * 1024)

    def body(chip_ref, w_ref, m_ref, v_ref, p_ref, r_ref, g_ref, d_ref, mo_ref, vo_ref):
        g = ((p_ref[...].astype(F32) + r_ref[0].astype(F32)) + r_ref[1].astype(F32)) + r_ref[2].astype(F32)
        g_ref[...] = g
        d_ref[...], mo_ref[...], vo_ref[...] = _adam_update(w_ref[...], m_ref[...], v_ref[...], g)

    blk = pl.BlockSpec((tr, C), lambda i, ch: (i, 0))
    out = jax.ShapeDtypeStruct((R, C), F32)
    return pl.pallas_call(
        body, name=name,
        grid_spec=pltpu.PrefetchScalarGridSpec(
            num_scalar_prefetch=1, grid=(R // tr,),
            in_specs=[blk, blk, blk,
                      pl.BlockSpec((None, tr, C), lambda i, ch: (ch[0], i, 0)),
                      pl.BlockSpec((3, tr, C), lambda i, ch: (0, i, 0))],
            out_specs=[blk, blk, blk, blk]),
        out_shape=[out, out, out, out],
        compiler_params=_params("parallel"),
    )(chip, w, m, v, chip_sums, recv)


def _adamw_replicated(w, m, v, parts, name):
    def body(w_ref, m_ref, v_ref, p_ref, g_ref, d_ref, mo_ref, vo_ref):
        g = p_ref[0]
        for dev in range(1, N_DEV):
            g = g + p_ref[dev]
        g_ref[...] = g
        d_ref[...], mo_ref[...], vo_ref[...] = _adam_update(w_ref[...], m_ref[...], v_ref[...], g)

    out = jax.ShapeDtypeStruct(w.shape, F32)
    return pl.pallas_call(body, name=name, out_shape=[out, out, out, out], compiler_params=_params())(w, m, v, parts)


def _cols_from_slots(g):
    return g.transpose(1, 0, 2).reshape(g.shape[1], N_DEV * g.shape[2])


def _cols_to_slots(w):
    k, n = w.shape
    return w.reshape(k, N_DEV, n // N_DEV).transpose(1, 0, 2)


def _permute(t, dil):
    c = t.shape[1]
    return t.reshape(S // dil, dil, c).transpose(1, 0, 2).reshape(S, c)


def _unpermute(t, dil):
    c = t.shape[1]
    return t.reshape(dil, S // dil, c).transpose(1, 0, 2).reshape(S, c)


def _group_permute(t):
    return jnp.stack([_permute(t[:, g * B_OUT:(g + 1) * B_OUT], B_DILS[g]) for g in range(3)])


def _same_permute(t):
    return jnp.stack([_permute(t, d) for d in B_DILS])


def _group_unpermute(t):
    return jnp.stack([_unpermute(t[g], B_DILS[g]) for g in range(3)])


SMALL_ROWS = 144


def _pack_small(a_b_f, kv_g, mix_g, ffn_g, conv_b, fin_g):
    flat = jnp.concatenate([a_b_f.reshape(-1), kv_g.reshape(-1), mix_g.reshape(-1), ffn_g.reshape(-1),
                            conv_b.reshape(-1), fin_g.reshape(-1)])
    return jnp.pad(flat, (0, SMALL_ROWS * LANES - flat.shape[0])).reshape(SMALL_ROWS, LANES)


def _unpack_small(p):
    flat = p.reshape(-1)
    out, off = [], 0
    for shape in ((1, A_HEADS), (D,), (2, D), (2, D), (2, 2 * D_FF), (D,)):
        size = math.prod(shape)
        out.append(flat[off:off + size].reshape(shape))
        off += size
    return out


def _local_step(x0, target, w_in_pad, w_out, w_q, w_bo, w_kvf, w_up, w_down, conv_w,
                a_b_f, kv_norm_g, mix_norm_g, ffn_norm_g, ffn_conv_b, final_norm_g):
    w_qkv, w_f = w_in_pad[:, :A_QKV], w_in_pad[:, A_QKV:]
    conv_b = ffn_conv_b.reshape(2, 1, 2 * D_FF)
    slopes = jnp.exp2(-8.0 * jnp.arange(1, 25, dtype=F32) / 24)

    def gain(g):
        return g.reshape(1, D)

    (h1,) = _rmsnorm_fwd(x0, [gain(mix_norm_g[0])], "norm_mix0")
    qkv = _matmul(h1, w_qkv, mode="nn", out_dtype=BF16, name="proj_qkv")
    z = _matmul(h1, w_f, mode="nn", out_dtype=F32, name="proj_gate")
    z_t = z[:, :A_HEADS].T
    b_f = a_b_f.reshape(A_HEADS, 1)
    c_t = _fox_prep_fwd(z_t, b_f, "fox_prep")
    c_col = c_t.T
    c_t2 = c_t.reshape(N_PAIRS, 2, S)
    o_a, lse_a = _fox_fwd(qkv, c_col, c_t2, "fox_fwd")
    x1 = _matmul(o_a, w_out, mode="nn", out_dtype=F32, name="a_out", res=x0)

    def ffn_fwd(xin, layer):
        (h,) = _rmsnorm_fwd(xin, [gain(ffn_norm_g[layer])], f"norm_ffn{layer}")
        u = _matmul(h, w_up[layer], mode="nn", out_dtype=BF16, name=f"ffn_up{layer}")
        act = _convgate_fwd(u, conv_w[layer], conv_b[layer], f"convgate{layer}")
        xout = _matmul(act, w_down[layer], mode="nn", out_dtype=F32, name=f"ffn_down{layer}", res=xin)
        return h, u, act, xout

    h2, u0, act0, x2 = ffn_fwd(x1, 0)
    hk, h3 = _rmsnorm_fwd(x2, [gain(kv_norm_g), gain(mix_norm_g[1])], "norm_kv_mix1")
    kv = _matmul(hk, w_kvf, mode="nn", out_dtype=BF16, name="proj_kv")
    qb = _matmul(h3, w_q, mode="nn", out_dtype=BF16, name="proj_qb")
    qp, kp, vp = _group_permute(qb), _group_permute(kv[:, :B_Q]), _group_permute(kv[:, B_Q:])
    og_p, lseg_p = _dil_fwd(qp, kp, vp, slopes, "dil_fwd")
    o_b, lse_b = _dil_merge(_group_unpermute(og_p), _group_unpermute(lseg_p), "dil_merge")
    x3 = _matmul(o_b, w_bo, mode="nn", out_dtype=F32, name="b_out", res=x2)
    h4, u1, act1, x4 = ffn_fwd(x3, 1)
    loss_blk, dx4, dx4b, dg_final = _final_loss(x4, target, gain(final_norm_g), "final_loss")

    def ffn_bwd(dx, dxb, xin, h, u, act, layer):
        dact = _matmul(dxb, w_down[layer], mode="nt", out_dtype=BF16, name=f"d_act{layer}", tn=256)
        dw_down = _matmul(act, dxb, mode="tn", out_dtype=BF16, name=f"dw_down{layer}", tm=256, tk=1024)
        da, dg, dwa, dwg, dba, dbg = _convgate_bwd(u, conv_w[layer], conv_b[layer], dact, f"convgate_bwd{layer}")
        du_a = _conv_input_bwd(da, conv_w[layer][:, :D_FF], f"conv_in_bwd_a{layer}")
        du_g = _conv_input_bwd(dg, conv_w[layer][:, D_FF:], f"conv_in_bwd_g{layer}")
        dw_up = jnp.concatenate(
            [_matmul(h, du_a, mode="tn", out_dtype=BF16, name=f"dw_up_a{layer}", tn=256, tk=1024),
             _matmul(h, du_g, mode="tn", out_dtype=BF16, name=f"dw_up_g{layer}", tn=256, tk=1024)], axis=1)
        dh = _matmul(du_a, w_up[layer][:, :D_FF], mode="nt", out_dtype=F32, name=f"dh_ffn_a{layer}")
        dh = _matmul(du_g, w_up[layer][:, D_FF:], mode="nt", out_dtype=F32, name=f"dh_ffn_g{layer}", res=dh)
        dxin, dxinb, dgain = _rmsnorm_bwd(xin, dh, gain(ffn_norm_g[layer]), dx, f"norm_ffn_bwd{layer}")
        dconv_w = jnp.concatenate([dwa, dwg], axis=1)
        dconv_b = jnp.concatenate([dba, dbg], axis=1)
        return dxin, dxinb, dgain, dw_up, dw_down, dconv_w, dconv_b

    dx3, dx3b, dg_ffn1, dw_up1, dw_down1, dconv_w1, dconv_b1 = ffn_bwd(dx4, dx4b, x3, h4, u1, act1, 1)

    do_b = _matmul(dx3b, w_bo, mode="nt", out_dtype=BF16, name="d_ob")
    dw_bo = _matmul(o_b, dx3b, mode="tn", out_dtype=BF16, name="dw_bo", tk=1024)
    dl_b = _head_rowsum(do_b, o_b, "delta_b")
    dqp, dkp, dvp = _dil_bwd(qp, kp, vp, _same_permute(do_b), _same_permute(lse_b), _same_permute(dl_b),
                             slopes, "dil_bwd")

    def natural(tp):
        return jnp.concatenate([_unpermute(tp[g], B_DILS[g]) for g in range(3)], axis=1)

    dqb = natural(dqp)
    dkv = jnp.concatenate([natural(dkp), natural(dvp)], axis=1)
    dw_q = _matmul(h3, dqb, mode="tn", out_dtype=BF16, name="dw_q", tk=1024)
    dw_kv = _matmul(hk, dkv, mode="tn", out_dtype=BF16, name="dw_kv", tk=1024)
    dh3 = _matmul(dqb, w_q, mode="nt", out_dtype=F32, name="dh_mix1")
    dhk = _matmul(dkv, w_kvf, mode="nt", out_dtype=F32, name="dh_kv")
    dx2, _, dg_mix1 = _rmsnorm_bwd(x2, dh3, gain(mix_norm_g[1]), dx3, "norm_mix1_bwd")
    dx2, dx2b, dg_kv = _rmsnorm_bwd(x2, dhk, gain(kv_norm_g), dx2, "norm_kv_bwd")

    dx1, dx1b, dg_ffn0, dw_up0, dw_down0, dconv_w0, dconv_b0 = ffn_bwd(dx2, dx2b, x1, h2, u0, act0, 0)

    do_a = _matmul(dx1b, w_out, mode="nt", out_dtype=BF16, name="d_oa")
    dw_out = _matmul(o_a, dx1b, mode="tn", out_dtype=BF16, name="dw_out", tk=1024)
    dl_a = _head_rowsum(do_a, o_a, "delta_a")
    dq_a, dk_a, dv_a, dc_t2, drow = _fox_bwd(qkv, do_a, lse_a, dl_a, c_col, c_t2, "fox_bwd")
    dz_t, db_f = _fox_prep_bwd(drow[:, ::HEAD_DIM].T, dc_t2.reshape(A_HEADS, S), z_t, b_f, "fox_prep_bwd")
    dz = jnp.pad(dz_t.T, ((0, 0), (0, LANES - A_HEADS))).astype(BF16)
    dproj = jnp.concatenate([dq_a.astype(BF16), dk_a, dv_a, dz], axis=1)
    dw_in = _matmul(h1, dproj, mode="tn", out_dtype=BF16, name="dw_in", tn=640, tk=1024)
    dh1 = _matmul(dproj, w_in_pad, mode="nt", out_dtype=F32, name="dh_mix0")
    grad_x, _, dg_mix0 = _rmsnorm_bwd(x0, dh1, gain(mix_norm_g[0]), dx1, "norm_mix0_bwd")

    dw_up = jnp.stack([dw_up0, dw_up1])
    dw_down = jnp.stack([dw_down0, dw_down1])
    dconv_w = jnp.stack([dconv_w0, dconv_w1])
    dg_mix = jnp.concatenate([dg_mix0, dg_mix1], axis=0)
    dg_ffn = jnp.concatenate([dg_ffn0, dg_ffn1], axis=0)
    dconv_b = jnp.concatenate([dconv_b0, dconv_b1], axis=0)
    big = (dw_in[:, :A_QKV + A_HEADS], dw_out, dw_q, dw_bo, dw_kv, dw_up, dw_down, dconv_w)
    small = (db_f, dg_kv, dg_mix, dg_ffn, dconv_b, dg_final)
    return loss_blk, grad_x, big, small


def kernel(x, a_w_in, a_b_f, a_w_out, b_w_q, b_w_out, kv_norm_g, w_kv, mix_norm_g, ffn_norm_g, ffn_w_up, ffn_conv_w, ffn_conv_b, ffn_w_down, final_norm_g, loss_target, m_a_w_in, m_a_b_f, m_a_w_out, m_b_w_q, m_b_w_out, m_kv_norm_g, m_w_kv, m_mix_norm_g, m_ffn_norm_g, m_ffn_w_up, m_ffn_conv_w, m_ffn_conv_b, m_ffn_w_down, m_final_norm_g, v_a_w_in, v_a_b_f, v_a_w_out, v_b_w_q, v_b_w_out, v_kv_norm_g, v_w_kv, v_mix_norm_g, v_ffn_norm_g, v_ffn_w_up, v_ffn_conv_w, v_ffn_conv_b, v_ffn_w_down, v_final_norm_g):
    xi, yi, ci = lax.axis_index("x"), lax.axis_index("y"), lax.axis_index("c")
    core = jnp.reshape(ci, (1,)).astype(jnp.int32)
    chip = jnp.reshape(2 * xi + yi, (1,)).astype(jnp.int32)

    def shards(a_w_in, a_w_out, b_w_q, b_w_out, w_kv, ffn_w_up, ffn_w_down, ffn_conv_w):
        return [a_w_in[0], a_w_out[0], b_w_q[0], b_w_out[0], w_kv, ffn_w_up.reshape(2 * D, -1),
                ffn_w_down.reshape(-1, D), ffn_conv_w.reshape(6, -1)]

    w_loc = shards(a_w_in, a_w_out, b_w_q, b_w_out, w_kv, ffn_w_up, ffn_w_down, ffn_conv_w)
    m_loc = shards(m_a_w_in, m_a_w_out, m_b_w_q, m_b_w_out, m_w_kv, m_ffn_w_up, m_ffn_w_down, m_ffn_conv_w)
    v_loc = shards(v_a_w_in, v_a_w_out, v_b_w_q, v_b_w_out, v_w_kv, v_ffn_w_up, v_ffn_w_down, v_ffn_conv_w)

    gathered = _all_gather([w.astype(BF16) for w in w_loc[:7]] + [w_loc[7]], "gather_weights")
    w_in = _cols_from_slots(gathered[0])
    w_in_pad = jnp.pad(w_in, ((0, 0), (0, A_PROJ_PAD - w_in.shape[1])))
    w_out = gathered[1].reshape(D, D)
    w_q = _cols_from_slots(gathered[2])
    w_bo = _cols_from_slots(gathered[3])
    w_kvf = _cols_from_slots(gathered[4])
    w_up = gathered[5].reshape(N_DEV, 2, D, -1).transpose(1, 2, 0, 3).reshape(2, D, 2 * D_FF)
    w_down = gathered[6].reshape(N_DEV, 2, -1, D).transpose(1, 0, 2, 3).reshape(2, D_FF, D)
    conv_w = gathered[7].reshape(N_DEV, 2, 3, -1).transpose(1, 2, 0, 3).reshape(2, 3, 2 * D_FF)

    loss_blk, grad_x, big_grads, small_grads = _local_step(
        x[0], loss_target[0], w_in_pad, w_out, w_q, w_bo, w_kvf, w_up, w_down, conv_w,
        a_b_f, kv_norm_g, mix_norm_g, ffn_norm_g, ffn_conv_b, final_norm_g)
    dw_in, dw_out, dw_q, dw_bo, dw_kv, dw_up, dw_down, dconv_w = big_grads

    slots = [
        _cols_to_slots(dw_in),
        dw_out.reshape(N_DEV, D // N_DEV, D),
        _cols_to_slots(dw_q),
        _cols_to_slots(dw_bo),
        _cols_to_slots(dw_kv),
        dw_up.reshape(2, D, N_DEV, -1).transpose(2, 0, 1, 3).reshape(N_DEV, 2 * D, -1),
        dw_down.reshape(2, N_DEV, -1, D).transpose(1, 0, 2, 3).reshape(N_DEV, -1, D),
        dconv_w.reshape(2, 3, N_DEV, -1).transpose(2, 0, 1, 3).reshape(N_DEV, 6, -1),
    ]
    by_chip = [s.reshape((4, 2) + s.shape[1:]) for s in slots]
    from_sibling = _sibling_exchange(by_chip, "grad_sibling_exchange")
    chip_sums = [_pair_add(g, r, core, f"grad_pair_add{k}") for k, (g, r) in enumerate(zip(by_chip, from_sibling))]
    from_chips = _chip_exchange(chip_sums, "grad_chip_exchange")

    big = [_adamw_sharded(w_loc[k], m_loc[k], v_loc[k], chip_sums[k], from_chips[k], chip, f"adamw{k}")
           for k in range(8)]

    small_part = _pack_small(*small_grads)
    (small_parts,) = _all_gather([small_part], "gather_small_grads")
    small = _adamw_replicated(
        _pack_small(a_b_f, kv_norm_g, mix_norm_g, ffn_norm_g, ffn_conv_b, final_norm_g),
        _pack_small(m_a_b_f, m_kv_norm_g, m_mix_norm_g, m_ffn_norm_g, m_ffn_conv_b, m_final_norm_g),
        _pack_small(v_a_b_f, v_kv_norm_g, v_mix_norm_g, v_ffn_norm_g, v_ffn_conv_b, v_final_norm_g),
        small_parts, "adamw_small")

    loss = lax.psum(loss_blk[0, 0], ("x", "y", "c"))

    def assemble(kind):
        b = [r[kind] for r in big]
        s_abf, s_kv, s_mix, s_ffn, s_cb, s_fin = _unpack_small(small[kind])
        return [b[0][None], s_abf, b[1][None], b[2][None], b[3][None], s_kv, b[4], s_mix, s_ffn,
                b[5].reshape(2, D, -1), b[7].reshape(2, 3, -1), s_cb, b[6].reshape(2, -1, D), s_fin]

    return (loss, grad_x[None], *assemble(0), *assemble(1), *assemble(2), *assemble(3))
```

```python
import functools
import math

import jax
import jax.numpy as jnp
from jax import lax
from jax.experimental import pallas as pl
from jax.experimental.pallas import tpu as pltpu

F32 = jnp.float32
BF16 = jnp.bfloat16

S = 4096
D = 1024
N_DEV = 8
A_HEADS = 16
HEAD_DIM = 64
A_QKV = 3072
A_PROJ_PAD = 3200
B_Q = 1536
B_OUT = 512
B_KV = 3072
B_W = 128
B_DILS = (1, 4, 16)
D_FF = 2816
RMS_EPS = 1e-6
SCALE = HEAD_DIM ** -0.5
NEG = -1e30

ADAM_LR = 0.001
ADAM_B1 = 0.9
ADAM_B2 = 0.999
ADAM_EPS = 1e-08
ADAM_WD = 0.01
ADAM_STEP = 10

LANES = 128
VMEM_LIMIT = 56 * 1024 * 1024
MESH = pl.DeviceIdType.MESH
ANY = pl.BlockSpec(memory_space=pl.ANY)

NT_DIMS = (((1,), (1,)), ((), ()))
TN_DIMS = (((0,), (0,)), ((), ()))
NN_DIMS = (((1,), (0,)), ((), ()))


def _params(*sem):
    return pltpu.CompilerParams(dimension_semantics=sem if sem else None, vmem_limit_bytes=VMEM_LIMIT)


def _dot(a, b, dims=NN_DIMS):
    return lax.dot_general(a, b, dims, preferred_element_type=F32)


def _split_dot(x, mat, pieces):
    out = None
    rem = x
    for _ in range(pieces):
        part = rem.astype(BF16)
        rem = rem - part.astype(F32)
        d = _dot(part, mat)
        out = d if out is None else out + d
    return out


def _pick(n, prefs):
    for p in prefs:
        if n % p == 0:
            return p
    return n


def _all_gather(arrays, name):
    n = len(arrays)

    def body(*refs):
        ins = refs[:n]
        outs = refs[n:2 * n]
        send_sems, recv_sems, local_sems = refs[2 * n:]
        x, y, c = lax.axis_index("x"), lax.axis_index("y"), lax.axis_index("c")
        me, sibling = (x, y, c), (x, y, 1 - c)
        chips = [(1 - x, y), (x, 1 - y), (1 - x, 1 - y)]

        def slot(a, px, py, pc):
            return outs[a].at[4 * px + 2 * py + pc]

        def copy(a, k, block, to, src=None):
            return pltpu.make_async_remote_copy(
                src_ref=slot(a, *block) if src is None else src, dst_ref=slot(a, *block),
                send_sem=send_sems.at[a, k], recv_sem=recv_sems.at[a, k],
                device_id=to, device_id_type=MESH)

        mine = [pltpu.make_async_copy(ins[a], slot(a, *me), local_sems.at[a]) for a in range(n)]
        for cp in mine:
            cp.start()
        first = []
        for a in range(n):
            first.append(copy(a, 0, me, sibling, src=ins[a]))
            first += [copy(a, 1 + j, me, (*chip, c), src=ins[a]) for j, chip in enumerate(chips)]
        for cp in first:
            cp.start()
        passed = []
        for j, chip in enumerate(chips):
            for a in range(n):
                copy(a, 1 + j, (*chip, c), me).wait_recv()
                fwd = copy(a, 4 + j, (*chip, c), sibling)
                fwd.start()
                passed.append(fwd)
        for a in range(n):
            copy(a, 0, sibling, me).wait_recv()
            for j, chip in enumerate(chips):
                copy(a, 4 + j, (*chip, 1 - c), me).wait_recv()
        for cp in first + passed:
            cp.wait_send()
        for cp in mine:
            cp.wait()

    return pl.pallas_call(
        body, name=name,
        out_shape=[jax.ShapeDtypeStruct((N_DEV,) + a.shape, a.dtype) for a in arrays],
        in_specs=[ANY] * n, out_specs=[ANY] * n,
        scratch_shapes=[pltpu.SemaphoreType.DMA((n, 7)), pltpu.SemaphoreType.DMA((n, 7)),
                        pltpu.SemaphoreType.DMA((n,))],
    )(*arrays)


def _sibling_exchange(arrays, name):
    n = len(arrays)

    def body(*refs):
        ins = refs[:n]
        outs = refs[n:2 * n]
        send_sems, recv_sems = refs[2 * n:]
        x, y, c = lax.axis_index("x"), lax.axis_index("y"), lax.axis_index("c")
        copies = []
        for a in range(n):
            for chip in range(4):
                copies.append(pltpu.make_async_remote_copy(
                    src_ref=ins[a].at[chip, 1 - c], dst_ref=outs[a].at[chip],
                    send_sem=send_sems.at[a, chip], recv_sem=recv_sems.at[a, chip],
                    device_id=(x, y, 1 - c), device_id_type=MESH))
        for cp in copies:
            cp.start()
        for cp in copies:
            cp.wait()

    return pl.pallas_call(
        body, name=name,
        out_shape=[jax.ShapeDtypeStruct((4,) + a.shape[2:], a.dtype) for a in arrays],
        in_specs=[ANY] * n, out_specs=[ANY] * n,
        scratch_shapes=[pltpu.SemaphoreType.DMA((n, 4)), pltpu.SemaphoreType.DMA((n, 4))],
    )(*arrays)


def _chip_exchange(arrays, name):
    n = len(arrays)

    def body(*refs):
        ins = refs[:n]
        outs = refs[n:2 * n]
        send_sems, recv_sems = refs[2 * n:]
        x, y, c = lax.axis_index("x"), lax.axis_index("y"), lax.axis_index("c")
        chips = [(1 - x, y), (x, 1 - y), (1 - x, 1 - y)]
        copies = []
        for a in range(n):
            for j, (qx, qy) in enumerate(chips):
                copies.append(pltpu.make_async_remote_copy(
                    src_ref=ins[a].at[2 * qx + qy], dst_ref=outs[a].at[j],
                    send_sem=send_sems.at[a, j], recv_sem=recv_sems.at[a, j],
                    device_id=(qx, qy, c), device_id_type=MESH))
        for cp in copies:
            cp.start()
        for cp in copies:
            cp.wait()

    return pl.pallas_call(
        body, name=name,
        out_shape=[jax.ShapeDtypeStruct((3,) + a.shape[1:], a.dtype) for a in arrays],
        in_specs=[ANY] * n, out_specs=[ANY] * n,
        scratch_shapes=[pltpu.SemaphoreType.DMA((n, 3)), pltpu.SemaphoreType.DMA((n, 3))],
    )(*arrays)


MM_ROWS = 512


def _matmul(a, b, *, mode, out_dtype, name, tm, tn, res=None):
    if mode == "nn":
        (M, K), (K2, N) = a.shape, b.shape
    else:
        (M, K), (N, K2) = a.shape, b.shape
    assert K == K2, (a.shape, b.shape, mode)
    tm, tn = min(tm, M), min(tn, N)
    sm = min(tm, MM_ROWS)
    assert M % tm == 0 and N % tn == 0 and tm % sm == 0, (M, N, K, tm, tn)
    dims = NN_DIMS if mode == "nn" else NT_DIMS
    a_spec = pl.BlockSpec((tm, K), lambda i, j: (i, 0))
    if mode == "nt":
        b_spec = pl.BlockSpec((tn, K), lambda i, j: (j, 0))
    else:
        b_spec = pl.BlockSpec((K, tn), lambda i, j: (0, j))
    o_spec = pl.BlockSpec((tm, tn), lambda i, j: (i, j))
    has_res = res is not None

    def body(*refs):
        a_ref, b_ref = refs[0], refs[1]
        r_ref = refs[2] if has_res else None
        o_ref = refs[2 + has_res]

        def chunk(r, carry):
            rows = pl.ds(pl.multiple_of(r * sm, sm), sm)
            total = _dot(a_ref[rows, :], b_ref[...], dims)
            if has_res:
                total = total + r_ref[rows, :]
            o_ref[rows, :] = total.astype(out_dtype)
            return carry

        lax.fori_loop(0, tm // sm, chunk, 0)

    return pl.pallas_call(
        body, name=name, grid=(M // tm, N // tn),
        out_shape=jax.ShapeDtypeStruct((M, N), out_dtype),
        in_specs=[a_spec, b_spec] + ([o_spec] if has_res else []),
        out_specs=o_spec,
        compiler_params=_params("parallel", "parallel"),
    )(*((a, b, res) if has_res else (a, b)))


def _matmul_tn(a, b, *, out_dtype, name, tk=512, sm=256):
    (K, M), (K2, N) = a.shape, b.shape
    assert K == K2 and K % tk == 0 and M % sm == 0, (a.shape, b.shape)
    nk = K // tk

    def body(a_ref, b_ref, o_ref, acc_ref):
        k = pl.program_id(0)

        @pl.when(k == 0)
        def _():
            acc_ref[...] = jnp.zeros_like(acc_ref)

        def chunk(mi, carry):
            cols = pl.ds(pl.multiple_of(mi * sm, sm), sm)
            acc_ref[cols, :] += _dot(a_ref[:, cols].T, b_ref[...])
            return carry

        lax.fori_loop(0, M // sm, chunk, 0)

        @pl.when(k == nk - 1)
        def _():
            def emit(mi, carry):
                rows = pl.ds(pl.multiple_of(mi * sm, sm), sm)
                o_ref[rows, :] = acc_ref[rows, :].astype(out_dtype)
                return carry
            lax.fori_loop(0, M // sm, emit, 0)

    return pl.pallas_call(
        body, name=name, grid=(nk,),
        out_shape=jax.ShapeDtypeStruct((M, N), out_dtype),
        in_specs=[pl.BlockSpec((tk, M), lambda k: (k, 0)), pl.BlockSpec((tk, N), lambda k: (k, 0))],
        out_specs=pl.BlockSpec((M, N), lambda k: (0, 0)),
        scratch_shapes=[pltpu.VMEM((M, N), F32)],
        compiler_params=_params("arbitrary"),
    )(a, b)


def _rmsnorm_fwd(x, gains, name, tr=256):
    n = len(gains)

    def body(*refs):
        x_ref = refs[0]
        xv = x_ref[...]
        r = lax.rsqrt(jnp.mean(xv * xv, axis=-1, keepdims=True) + RMS_EPS)
        y = xv * r
        for a in range(n):
            refs[1 + n + a][...] = (y * refs[1 + a][...]).astype(BF16)

    row = pl.BlockSpec((tr, D), lambda i: (i, 0))
    gain = pl.BlockSpec((1, D), lambda i: (0, 0))
    return pl.pallas_call(
        body, name=name, grid=(S // tr,),
        out_shape=[jax.ShapeDtypeStruct((S, D), BF16)] * n,
        in_specs=[row] + [gain] * n, out_specs=[row] * n,
        compiler_params=_params("parallel"),
    )(x, *gains)


def _rmsnorm_bwd(x, dy, g, dres, name, tr=256):
    def body(x_ref, dy_ref, g_ref, dres_ref, dx_ref, dxb_ref, dg_ref):
        xv = x_ref[...]
        dyv = dy_ref[...]
        r = lax.rsqrt(jnp.mean(xv * xv, axis=-1, keepdims=True) + RMS_EPS)
        xhat = xv * r
        dxhat = dyv * g_ref[...]
        mean_term = jnp.mean(dxhat * xhat, axis=-1, keepdims=True)
        dx = r * (dxhat - xhat * mean_term) + dres_ref[...]
        dx_ref[...] = dx
        dxb_ref[...] = dx.astype(BF16)
        part = jnp.sum(dyv * xhat, axis=0, keepdims=True)

        @pl.when(pl.program_id(0) == 0)
        def _():
            dg_ref[...] = part

        @pl.when(pl.program_id(0) > 0)
        def _():
            dg_ref[...] += part

    row = pl.BlockSpec((tr, D), lambda i: (i, 0))
    gain = pl.BlockSpec((1, D), lambda i: (0, 0))
    return pl.pallas_call(
        body, name=name, grid=(S // tr,),
        out_shape=[jax.ShapeDtypeStruct((S, D), F32), jax.ShapeDtypeStruct((S, D), BF16),
                   jax.ShapeDtypeStruct((1, D), F32)],
        in_specs=[row, row, gain, row], out_specs=[row, row, gain],
        compiler_params=_params("arbitrary"),
    )(x, dy, g, dres)


def _final_loss(x, target, g, name, tr=256):
    def body(x_ref, t_ref, g_ref, loss_ref, dx_ref, dxb_ref, dg_ref):
        xv = x_ref[...]
        gv = g_ref[...]
        r = lax.rsqrt(jnp.mean(xv * xv, axis=-1, keepdims=True) + RMS_EPS)
        xhat = xv * r
        err = xhat * gv - t_ref[...]
        row_loss = jnp.mean(err * err, axis=-1, keepdims=True)
        lpart = 0.5 * jnp.sum(row_loss, axis=0, keepdims=True)
        dyv = err / D
        dxhat = dyv * gv
        mean_term = jnp.mean(dxhat * xhat, axis=-1, keepdims=True)
        dx = r * (dxhat - xhat * mean_term)
        dx_ref[...] = dx
        dxb_ref[...] = dx.astype(BF16)
        gpart = jnp.sum(dyv * xhat, axis=0, keepdims=True)

        @pl.when(pl.program_id(0) == 0)
        def _():
            dg_ref[...] = gpart
            loss_ref[...] = jnp.broadcast_to(lpart, loss_ref.shape)

        @pl.when(pl.program_id(0) > 0)
        def _():
            dg_ref[...] += gpart
            loss_ref[...] += jnp.broadcast_to(lpart, loss_ref.shape)

    row = pl.BlockSpec((tr, D), lambda i: (i, 0))
    gain = pl.BlockSpec((1, D), lambda i: (0, 0))
    lspec = pl.BlockSpec((8, LANES), lambda i: (0, 0))
    return pl.pallas_call(
        body, name=name, grid=(S // tr,),
        out_shape=[jax.ShapeDtypeStruct((8, LANES), F32), jax.ShapeDtypeStruct((S, D), F32),
                   jax.ShapeDtypeStruct((S, D), BF16), jax.ShapeDtypeStruct((1, D), F32)],
        in_specs=[row, row, gain], out_specs=[lspec, row, row, gain],
        compiler_params=_params("arbitrary"),
    )(x, target, g)


CONV_TR = 512
CONV_TC = 256
CONV_NJ = D_FF // CONV_TC
HALO = 16


def _causal_taps(cur_ref, prev_ref, first):
    xv = cur_ref[...].astype(F32)
    pv = prev_ref[...].astype(F32)
    p1 = jnp.where(first, 0.0, pv[HALO - 1:HALO, :])
    p2 = jnp.where(first, 0.0, pv[HALO - 2:HALO - 1, :])
    row = lax.broadcasted_iota(jnp.int32, xv.shape, 0)
    xm1 = jnp.where(row == 0, p1, pltpu.roll(xv, 1, 0))
    xm2 = jnp.where(row == 0, p2, jnp.where(row == 1, p1, pltpu.roll(xv, 2, 0)))
    return xv, xm1, xm2


def _conv_specs():
    def prev_row(i):
        return jnp.maximum(i * (CONV_TR // HALO) - 1, 0)
    ua = pl.BlockSpec((CONV_TR, CONV_TC), lambda i, j: (i, j))
    ug = pl.BlockSpec((CONV_TR, CONV_TC), lambda i, j: (i, j + CONV_NJ))
    pa = pl.BlockSpec((HALO, CONV_TC), lambda i, j: (prev_row(i), j))
    pg = pl.BlockSpec((HALO, CONV_TC), lambda i, j: (prev_row(i), j + CONV_NJ))
    wa = pl.BlockSpec((3, CONV_TC), lambda i, j: (0, j))
    wg = pl.BlockSpec((3, CONV_TC), lambda i, j: (0, j + CONV_NJ))
    ba = pl.BlockSpec((1, CONV_TC), lambda i, j: (0, j))
    bg = pl.BlockSpec((1, CONV_TC), lambda i, j: (0, j + CONV_NJ))
    return [ua, pa, ug, pg, wa, wg, ba, bg]


def _convgate_fwd(u, w, b, name):
    def body(ua, pa, ug, pg, wa, wg, ba, bg, o_ref):
        first = pl.program_id(0) == 0
        x0, x1, x2 = _causal_taps(ua, pa, first)
        ac = wa[0:1, :] * x2 + wa[1:2, :] * x1 + wa[2:3, :] * x0 + ba[...]
        x0, x1, x2 = _causal_taps(ug, pg, first)
        gc = wg[0:1, :] * x2 + wg[1:2, :] * x1 + wg[2:3, :] * x0 + bg[...]
        sg = 1.0 / (1.0 + jnp.exp(-gc))
        o_ref[...] = (gc * sg * ac).astype(BF16)

    return pl.pallas_call(
        body, name=name, grid=(S // CONV_TR, CONV_NJ),
        out_shape=jax.ShapeDtypeStruct((S, D_FF), BF16),
        in_specs=_conv_specs(),
        out_specs=pl.BlockSpec((CONV_TR, CONV_TC), lambda i, j: (i, j)),
        compiler_params=_params("parallel", "parallel"),
    )(u, u, u, u, w, w, b, b)


def _convgate_bwd(u, w, b, dact, name):
    def body(ua, pa, ug, pg, wa, wg, ba, bg, d_ref, da_ref, dg_ref, dwa_ref, dwg_ref, dba_ref, dbg_ref):
        i = pl.program_id(1)
        first = i == 0
        a0, a1, a2 = _causal_taps(ua, pa, first)
        ac = wa[0:1, :] * a2 + wa[1:2, :] * a1 + wa[2:3, :] * a0 + ba[...]
        g0, g1, g2 = _causal_taps(ug, pg, first)
        gc = wg[0:1, :] * g2 + wg[1:2, :] * g1 + wg[2:3, :] * g0 + bg[...]
        sg = 1.0 / (1.0 + jnp.exp(-gc))
        dact_v = d_ref[...].astype(F32)
        da = dact_v * (gc * sg)
        dg = dact_v * ac * (sg * (1.0 + gc * (1.0 - sg)))
        da_ref[...] = da.astype(BF16)
        dg_ref[...] = dg.astype(BF16)

        def col(v):
            return jnp.sum(v, axis=0, keepdims=True)

        parts = [col(da * a2), col(da * a1), col(da * a0), col(dg * g2), col(dg * g1), col(dg * g0),
                 col(da), col(dg)]

        @pl.when(first)
        def _():
            for k in range(3):
                dwa_ref[k:k + 1, :] = parts[k]
                dwg_ref[k:k + 1, :] = parts[3 + k]
            dba_ref[...] = parts[6]
            dbg_ref[...] = parts[7]

        @pl.when(i > 0)
        def _():
            for k in range(3):
                dwa_ref[k:k + 1, :] += parts[k]
                dwg_ref[k:k + 1, :] += parts[3 + k]
            dba_ref[...] += parts[6]
            dbg_ref[...] += parts[7]

    def swap(spec):
        return pl.BlockSpec(spec.block_shape, lambda j, i, f=spec.index_map: f(i, j))

    blk = pl.BlockSpec((CONV_TR, CONV_TC), lambda j, i: (i, j))
    w3 = pl.BlockSpec((3, CONV_TC), lambda j, i: (0, j))
    b1 = pl.BlockSpec((1, CONV_TC), lambda j, i: (0, j))
    return pl.pallas_call(
        body, name=name, grid=(CONV_NJ, S // CONV_TR),
        out_shape=[jax.ShapeDtypeStruct((S, D_FF), BF16), jax.ShapeDtypeStruct((S, D_FF), BF16),
                   jax.ShapeDtypeStruct((3, D_FF), F32), jax.ShapeDtypeStruct((3, D_FF), F32),
                   jax.ShapeDtypeStruct((1, D_FF), F32), jax.ShapeDtypeStruct((1, D_FF), F32)],
        in_specs=[swap(s) for s in _conv_specs()] + [blk],
        out_specs=[blk, blk, w3, w3, b1, b1],
        compiler_params=_params("parallel", "arbitrary"),
    )(u, u, u, u, w, w, b, b, dact)


def _conv_input_bwd(d, w, name):
    n_i = S // CONV_TR

    def body(d_ref, n_ref, w_ref, o_ref):
        last = pl.program_id(0) == n_i - 1
        dv = d_ref[...].astype(F32)
        nv = n_ref[...].astype(F32)
        n1 = jnp.where(last, 0.0, nv[0:1, :])
        n2 = jnp.where(last, 0.0, nv[1:2, :])
        row = lax.broadcasted_iota(jnp.int32, dv.shape, 0)
        dp1 = jnp.where(row == CONV_TR - 1, n1, pltpu.roll(dv, CONV_TR - 1, 0))
        dp2 = jnp.where(row == CONV_TR - 1, n2, jnp.where(row == CONV_TR - 2, n1, pltpu.roll(dv, CONV_TR - 2, 0)))
        o_ref[...] = (w_ref[2:3, :] * dv + w_ref[1:2, :] * dp1 + w_ref[0:1, :] * dp2).astype(BF16)

    def next_row(i):
        return jnp.minimum((i + 1) * (CONV_TR // HALO), S // HALO - 1)

    blk = pl.BlockSpec((CONV_TR, CONV_TC), lambda i, j: (i, j))
    return pl.pallas_call(
        body, name=name, grid=(n_i, CONV_NJ),
        out_shape=jax.ShapeDtypeStruct((S, D_FF), BF16),
        in_specs=[blk, pl.BlockSpec((HALO, CONV_TC), lambda i, j: (next_row(i), j)),
                  pl.BlockSpec((3, CONV_TC), lambda i, j: (0, j))],
        out_specs=blk,
        compiler_params=_params("parallel", "parallel"),
    )(d, d, w)


FOX_T = 256
N_PAIRS = A_HEADS // 2


def _lane_masks():
    lane = lax.broadcasted_iota(jnp.int32, (1, LANES), 1)
    return lane, (lane < HEAD_DIM, lane >= HEAD_DIM)


def _fox_prep_fwd(z_t, b, name):
    def body(z_ref, b_ref, c_ref):
        r = lax.broadcasted_iota(jnp.int32, (LANES, LANES), 0)
        cc = lax.broadcasted_iota(jnp.int32, (LANES, LANES), 1)
        upper = (r <= cc).astype(BF16)
        carry = jnp.zeros((A_HEADS, 1), F32)
        for blk in range(S // LANES):
            sl = slice(blk * LANES, (blk + 1) * LANES)
            z = z_ref[:, sl] + b_ref[...]
            lf = jnp.minimum(z, 0.0) - jnp.log(1.0 + jnp.exp(-jnp.abs(z)))
            cs = _split_dot(lf, upper, 3) + carry
            c_ref[:, sl] = cs
            carry = cs[:, LANES - 1:LANES]

    return pl.pallas_call(
        body, name=name, out_shape=jax.ShapeDtypeStruct((A_HEADS, S), F32),
        compiler_params=_params(),
    )(z_t, b)


def _fox_prep_bwd(drow_t, dcol_t, z_t, b, name):
    def body(dr_ref, dc_ref, z_ref, b_ref, dz_ref, db_ref):
        r = lax.broadcasted_iota(jnp.int32, (LANES, LANES), 0)
        cc = lax.broadcasted_iota(jnp.int32, (LANES, LANES), 1)
        lower = (r >= cc).astype(BF16)
        carry = jnp.zeros((A_HEADS, 1), F32)
        db = jnp.zeros((A_HEADS, 1), F32)
        for blk in reversed(range(S // LANES)):
            sl = slice(blk * LANES, (blk + 1) * LANES)
            rc = _split_dot(dr_ref[:, sl] - dc_ref[:, sl], lower, 3) + carry
            carry = rc[:, 0:1]
            z = z_ref[:, sl] + b_ref[...]
            dz = rc / (1.0 + jnp.exp(z))
            dz_ref[:, sl] = dz
            db = db + jnp.sum(dz, axis=1, keepdims=True)
        db_ref[...] = db

    return pl.pallas_call(
        body, name=name,
        out_shape=[jax.ShapeDtypeStruct((A_HEADS, S), F32), jax.ShapeDtypeStruct((A_HEADS, 1), F32)],
        compiler_params=_params(),
    )(drow_t, dcol_t, z_t, b)


def _fox_fwd(qkv, c_t2, name):
    t = FOX_T

    def body(q_ref, k_ref, v_ref, ct_ref, o_ref, lse_ref):
        qi = pl.program_id(1)
        lane, masks = _lane_masks()
        q = q_ref[...] * SCALE
        qs = [jnp.where(masks[e], q, jnp.zeros_like(q)) for e in range(2)]

        def step(j, carry, masked):
            start = pl.multiple_of(j * t, t)
            kb = k_ref[pl.ds(start, t), :]
            vb = v_ref[pl.ds(start, t), :]
            new = []
            for e in range(2):
                m, acc = carry[e]
                s = _dot(qs[e], kb, NT_DIMS) - ct_ref[e:e + 1, pl.ds(start, t)]
                if masked:
                    rows = lax.broadcasted_iota(jnp.int32, (t, t), 0)
                    cols = lax.broadcasted_iota(jnp.int32, (t, t), 1)
                    s = jnp.where(cols <= rows, s, NEG)
                m_new = jnp.maximum(m, jnp.max(s, axis=1, keepdims=True))
                p = jnp.exp(s - m_new).astype(BF16)
                v_aug = jnp.where(masks[e], vb, jnp.ones_like(vb))
                acc = jnp.exp(m - m_new) * acc + _dot(p, v_aug)
                new.append((m_new, acc))
            return tuple(new)

        init = tuple((jnp.full((t, 1), NEG, F32), jnp.zeros((t, LANES), F32)) for _ in range(2))
        carry = lax.fori_loop(0, qi, lambda j, cr: step(j, cr, False), init)
        (m0, acc0), (m1, acc1) = step(qi, carry, True)
        l0 = acc0[:, HEAD_DIM:HEAD_DIM + 1]
        l1 = acc1[:, 0:1]
        o_ref[...] = jnp.where(masks[0], acc0 / l0, acc1 / l1).astype(BF16)
        lse_ref[...] = jnp.where(masks[0], m0 + jnp.log(l0), m1 + jnp.log(l1))

    qspec = pl.BlockSpec((t, LANES), lambda h, i: (i, h))
    return pl.pallas_call(
        body, name=name, grid=(N_PAIRS, S // t),
        out_shape=[jax.ShapeDtypeStruct((S, D), BF16), jax.ShapeDtypeStruct((S, D), F32)],
        in_specs=[qspec,
                  pl.BlockSpec((S, LANES), lambda h, i: (0, N_PAIRS + h)),
                  pl.BlockSpec((S, LANES), lambda h, i: (0, 2 * N_PAIRS + h)),
                  pl.BlockSpec((None, 2, S), lambda h, i: (h, 0, 0))],
        out_specs=[qspec, qspec],
        compiler_params=_params("parallel", "parallel"),
    )(qkv, qkv, qkv, c_t2)


def _head_rowsum(a, b, name, tr=256):
    C = a.shape[1]

    def body(a_ref, b_ref, o_ref):
        r = lax.broadcasted_iota(jnp.int32, (LANES, LANES), 0) < HEAD_DIM
        cc = lax.broadcasted_iota(jnp.int32, (LANES, LANES), 1) < HEAD_DIM
        same_head = (r == cc).astype(BF16)
        for blk in range(C // LANES):
            sl = slice(blk * LANES, (blk + 1) * LANES)
            prod = a_ref[:, sl].astype(F32) * b_ref[:, sl].astype(F32)
            o_ref[:, sl] = _split_dot(prod, same_head, 2)

    row = pl.BlockSpec((tr, C), lambda i: (i, 0))
    return pl.pallas_call(
        body, name=name, grid=(S // tr,), out_shape=jax.ShapeDtypeStruct((S, C), F32),
        in_specs=[row, row], out_specs=row, compiler_params=_params("parallel"),
    )(a, b)


def _fox_bwd(qkv, do, lse, delta, c_t2, name):
    t = FOX_T
    nq = S // t

    def body(q_ref, k_ref, v_ref, do_ref, lse_ref, dl_ref, ct_ref, dq_ref, dk_ref, dv_ref, dcol_ref, drow_ref):
        kj = pl.program_id(1)

        @pl.when(kj == 0)
        def _():
            dq_ref[...] = jnp.zeros_like(dq_ref)
            drow_ref[...] = jnp.zeros_like(drow_ref)

        lane, masks = _lane_masks()
        k = k_ref[...]
        v = v_ref[...]
        k_aug = [jnp.where(masks[e], k * SCALE, jnp.ones_like(k)) for e in range(2)]
        cs = [ct_ref[e:e + 1, :] for e in range(2)]

        def step(i, carry, masked):
            dk_acc, dv_acc = list(carry[0]), carry[1]
            r0 = pl.multiple_of(i * t, t)
            qb = q_ref[pl.ds(r0, t), :] * SCALE
            dob = do_ref[pl.ds(r0, t), :]
            dq_parts = []
            for e in range(2):
                lo = e * HEAD_DIM
                qe = jnp.where(masks[e], qb, jnp.zeros_like(qb))
                q_aug = jnp.where(masks[e], qb, jnp.ones_like(qb))
                doe = jnp.where(masks[e], dob, jnp.zeros_like(dob))
                s = _dot(qe, k, NT_DIMS) - cs[e]
                if masked:
                    rows = lax.broadcasted_iota(jnp.int32, (t, t), 0)
                    cols = lax.broadcasted_iota(jnp.int32, (t, t), 1)
                    s = jnp.where(cols <= rows, s, NEG)
                p = jnp.exp(s - lse_ref[pl.ds(r0, t), lo:lo + 1])
                ds = (p * (_dot(doe, v, NT_DIMS) - dl_ref[pl.ds(r0, t), lo:lo + 1])).astype(BF16)
                dv_acc = dv_acc + _dot(p.astype(BF16), doe, TN_DIMS)
                dk_acc[e] = dk_acc[e] + _dot(ds, q_aug, TN_DIMS)
                dq_parts.append(_dot(ds, k_aug[e]))
            dq_ref[pl.ds(r0, t), :] += jnp.where(masks[0], dq_parts[0], dq_parts[1])
            drow_ref[pl.ds(r0, t), :] += jnp.where(masks[0], dq_parts[1], dq_parts[0])
            return tuple(dk_acc), dv_acc

        zero = jnp.zeros((t, LANES), F32)
        carry = step(kj, ((zero, zero), zero), True)
        (dk0, dk1), dv = lax.fori_loop(kj + 1, nq, lambda i, cr: step(i, cr, False), carry)
        dk_ref[...] = jnp.where(masks[0], dk0, dk1).astype(BF16)
        dcol_ref[...] = jnp.where(masks[0], dk1, dk0)
        dv_ref[...] = dv.astype(BF16)

    full = lambda off: pl.BlockSpec((S, LANES), lambda h, j, off=off: (0, off + h))
    kv = lambda off: pl.BlockSpec((t, LANES), lambda h, j, off=off: (j, off + h))
    return pl.pallas_call(
        body, name=name, grid=(N_PAIRS, nq),
        out_shape=[jax.ShapeDtypeStruct((S, D), F32), jax.ShapeDtypeStruct((S, D), BF16),
                   jax.ShapeDtypeStruct((S, D), BF16), jax.ShapeDtypeStruct((S, D), F32),
                   jax.ShapeDtypeStruct((S, D), F32)],
        in_specs=[full(0), kv(N_PAIRS), kv(2 * N_PAIRS), full(0), full(0), full(0),
                  pl.BlockSpec((None, 2, t), lambda h, j: (h, 0, j))],
        out_specs=[full(0), kv(0), kv(0), kv(0), full(0)],
        compiler_params=_params("parallel", "arbitrary"),
    )(qkv, qkv, qkv, do, lse, delta, c_t2)


B_PAIRS = 4
B_NB = S // B_W


def _group_consts(g):
    nbs = jnp.where(g == 0, B_NB // B_DILS[0], jnp.where(g == 1, B_NB // B_DILS[1], B_NB // B_DILS[2]))
    dil = jnp.where(g == 0, B_DILS[0], jnp.where(g == 1, B_DILS[1], B_DILS[2]))
    return nbs, dil


def _band(dil):
    qi = lax.broadcasted_iota(jnp.int32, (B_W, B_W), 0)
    kj = lax.broadcasted_iota(jnp.int32, (B_W, B_W), 1)
    dist_c = qi - kj
    dist_p = qi + B_W - kj
    return (dist_c * dil).astype(F32), dist_c >= 0, (dist_p * dil).astype(F32), dist_p <= B_W


def _dil_fwd(qp, kp, vp, slopes, name):
    def body(sl_ref, q_ref, kp_ref, kc_ref, vp_ref, vc_ref, o_ref, lse_ref):
        g, hp, n = pl.program_id(0), pl.program_id(1), pl.program_id(2)
        nbs, dil = _group_consts(g)
        has_prev = (n % nbs) != 0
        lane, masks = _lane_masks()
        bias_c, ok_c, bias_p, ok_p = _band(dil)
        ok_p = jnp.logical_and(ok_p, has_prev)
        q = q_ref[...]
        kc, kpv, vc, vpv = kc_ref[...], kp_ref[...], vc_ref[...], vp_ref[...]
        outs, lses = [], []
        for e in range(2):
            slope = sl_ref[g * 8 + 2 * hp + e]
            qe = jnp.where(masks[e], q, jnp.zeros_like(q))
            sc = jnp.where(ok_c, _dot(qe, kc, NT_DIMS) * SCALE - slope * bias_c, NEG)
            sp = jnp.where(ok_p, _dot(qe, kpv, NT_DIMS) * SCALE - slope * bias_p, NEG)
            m = jnp.maximum(jnp.max(sc, axis=1, keepdims=True), jnp.max(sp, axis=1, keepdims=True))
            pc = jnp.exp(sc - m)
            pp = jnp.exp(sp - m)
            l = jnp.sum(pc, axis=1, keepdims=True) + jnp.sum(pp, axis=1, keepdims=True)
            acc = _dot(pc.astype(BF16), vc) + _dot(pp.astype(BF16), vpv)
            outs.append(acc / l)
            lses.append(m + jnp.log(l))
        o_ref[...] = jnp.where(masks[0], outs[0], outs[1])
        lse_ref[...] = jnp.where(masks[0], lses[0], lses[1])

    cur = pl.BlockSpec((None, B_W, LANES), lambda g, h, n, sl: (g, n, h))
    prev = pl.BlockSpec((None, B_W, LANES), lambda g, h, n, sl: (g, jnp.maximum(n - 1, 0), h))
    return pl.pallas_call(
        body, name=name,
        grid_spec=pltpu.PrefetchScalarGridSpec(
            num_scalar_prefetch=1, grid=(3, B_PAIRS, B_NB),
            in_specs=[cur, prev, cur, prev, cur], out_specs=[cur, cur]),
        out_shape=[jax.ShapeDtypeStruct((3, S, B_OUT), F32), jax.ShapeDtypeStruct((3, S, B_OUT), F32)],
        compiler_params=_params("parallel", "parallel", "parallel"),
    )(slopes, qp, kp, kp, vp, vp)


def _dil_merge(og, lseg, name, tr=256):
    def body(o_ref, l_ref, out_ref, lse_ref):
        l0, l1, l2 = l_ref[0], l_ref[1], l_ref[2]
        m = jnp.maximum(jnp.maximum(l0, l1), l2)
        w0, w1, w2 = jnp.exp(l0 - m), jnp.exp(l1 - m), jnp.exp(l2 - m)
        den = w0 + w1 + w2
        out_ref[...] = ((w0 * o_ref[0] + w1 * o_ref[1] + w2 * o_ref[2]) / den).astype(BF16)
        lse_ref[...] = m + jnp.log(den)

    blk3 = pl.BlockSpec((3, tr, B_OUT), lambda i: (0, i, 0))
    blk = pl.BlockSpec((tr, B_OUT), lambda i: (i, 0))
    return pl.pallas_call(
        body, name=name, grid=(S // tr,),
        out_shape=[jax.ShapeDtypeStruct((S, B_OUT), BF16), jax.ShapeDtypeStruct((S, B_OUT), F32)],
        in_specs=[blk3, blk3], out_specs=[blk, blk], compiler_params=_params("parallel"),
    )(og, lseg)


def _dil_bwd(qp, kp, vp, dop, lsep, dlp, slopes, name):
    def body(sl_ref, qc_ref, qn_ref, kp_ref, kc_ref, vp_ref, vc_ref, doc_ref, don_ref,
             lc_ref, ln_ref, dc_ref, dn_ref, dq_ref, dk_ref, dv_ref):
        g, hp, n = pl.program_id(0), pl.program_id(1), pl.program_id(2)
        nbs, dil = _group_consts(g)
        has_prev = (n % nbs) != 0
        has_next = jnp.logical_and(n + 1 < B_NB, ((n + 1) % nbs) != 0)
        lane, masks = _lane_masks()
        bias_c, ok_c, bias_p, ok_p = _band(dil)
        ok_pp = jnp.logical_and(ok_p, has_prev)
        ok_np = jnp.logical_and(ok_p, has_next)
        qc, qn = qc_ref[...], qn_ref[...]
        kc, kpv, vc, vpv = kc_ref[...], kp_ref[...], vc_ref[...], vp_ref[...]
        doc, don = doc_ref[...], don_ref[...]
        dq = jnp.zeros((B_W, LANES), F32)
        dk = jnp.zeros((B_W, LANES), F32)
        dv = jnp.zeros((B_W, LANES), F32)
        for e in range(2):
            lo = e * HEAD_DIM
            slope = sl_ref[g * 8 + 2 * hp + e]
            qce = jnp.where(masks[e], qc, jnp.zeros_like(qc))
            qne = jnp.where(masks[e], qn, jnp.zeros_like(qn))
            doce = jnp.where(masks[e], doc, jnp.zeros_like(doc))
            done = jnp.where(masks[e], don, jnp.zeros_like(don))
            kce = jnp.where(masks[e], kc, jnp.zeros_like(kc))
            kpe = jnp.where(masks[e], kpv, jnp.zeros_like(kpv))
            lse_c, dl_c = lc_ref[:, lo:lo + 1], dc_ref[:, lo:lo + 1]
            lse_n, dl_n = ln_ref[:, lo:lo + 1], dn_ref[:, lo:lo + 1]
            s = jnp.where(ok_c, _dot(qce, kc, NT_DIMS) * SCALE - slope * bias_c, NEG)
            p = jnp.exp(s - lse_c)
            ds = p * (_dot(doce, vc, NT_DIMS) - dl_c)
            dsb = ds.astype(BF16)
            dq = dq + _dot(dsb, kce)
            dk = dk + _dot(dsb, qce, TN_DIMS)
            dv = dv + _dot(p.astype(BF16), doce, TN_DIMS)
            s = jnp.where(ok_pp, _dot(qce, kpv, NT_DIMS) * SCALE - slope * bias_p, NEG)
            p = jnp.exp(s - lse_c)
            ds = p * (_dot(doce, vpv, NT_DIMS) - dl_c)
            dq = dq + _dot(ds.astype(BF16), kpe)
            s = jnp.where(ok_np, _dot(qne, kc, NT_DIMS) * SCALE - slope * bias_p, NEG)
            p = jnp.exp(s - lse_n)
            ds = p * (_dot(done, vc, NT_DIMS) - dl_n)
            dk = dk + _dot(ds.astype(BF16), qne, TN_DIMS)
            dv = dv + _dot(p.astype(BF16), done, TN_DIMS)
        dq_ref[...] = (dq * SCALE).astype(BF16)
        dk_ref[...] = (dk * SCALE).astype(BF16)
        dv_ref[...] = dv.astype(BF16)

    cur = pl.BlockSpec((None, B_W, LANES), lambda g, h, n, sl: (g, n, h))
    prev = pl.BlockSpec((None, B_W, LANES), lambda g, h, n, sl: (g, jnp.maximum(n - 1, 0), h))
    nxt = pl.BlockSpec((None, B_W, LANES), lambda g, h, n, sl: (g, jnp.minimum(n + 1, B_NB - 1), h))
    return pl.pallas_call(
        body, name=name,
        grid_spec=pltpu.PrefetchScalarGridSpec(
            num_scalar_prefetch=1, grid=(3, B_PAIRS, B_NB),
            in_specs=[cur, nxt, prev, cur, prev, cur, cur, nxt, cur, nxt, cur, nxt],
            out_specs=[cur, cur, cur]),
        out_shape=[jax.ShapeDtypeStruct((3, S, B_OUT), BF16)] * 3,
        compiler_params=_params("parallel", "parallel", "parallel"),
    )(slopes, qp, qp, kp, kp, vp, vp, dop, dop, lsep, lsep, dlp, dlp)


def _rows_block(shape, max_bytes=2 * 1024 * 1024):
    rows, cols = shape
    padded_cols = -(-cols // LANES) * LANES
    for tr in (1024, 512, 256, 128, 64, 32, 16):
        if rows % tr == 0 and tr * padded_cols * 4 <= max_bytes:
            return tr
    return rows


def _pair_add(grads, recv, core, name):
    _, _, R, C = grads.shape
    tr = _rows_block((R, C))

    def body(core_ref, g_ref, r_ref, o_ref):
        o_ref[...] = (g_ref[...].astype(F32) + r_ref[...].astype(F32)).astype(o_ref.dtype)

    return pl.pallas_call(
        body, name=name,
        grid_spec=pltpu.PrefetchScalarGridSpec(
            num_scalar_prefetch=1, grid=(4, R // tr),
            in_specs=[pl.BlockSpec((None, None, tr, C), lambda q, i, cr: (q, cr[0], i, 0)),
                      pl.BlockSpec((None, tr, C), lambda q, i, cr: (q, i, 0))],
            out_specs=pl.BlockSpec((None, tr, C), lambda q, i, cr: (q, i, 0))),
        out_shape=jax.ShapeDtypeStruct((4, R, C), grads.dtype),
        compiler_params=_params("parallel", "parallel"),
    )(core, grads, recv)


def _adam_update(w, m, v, g):
    m_new = ADAM_B1 * m + (1.0 - ADAM_B1) * g
    v_new = ADAM_B2 * v + (1.0 - ADAM_B2) * (g * g)
    m_hat = m_new / (1.0 - ADAM_B1 ** ADAM_STEP)
    v_hat = v_new / (1.0 - ADAM_B2 ** ADAM_STEP)
    delta = -ADAM_LR * (m_hat / (jnp.sqrt(v_hat) + ADAM_EPS) + ADAM_WD * w)
    return delta, m_new, v_new


def _adamw_sharded(w, m, v, chip_sums, recv, chip, name):
    R, C = w.shape
    tr = _rows_block((R, C), max_bytes=1024 * 1024)

    def body(chip_ref, w_ref, m_ref, v_ref, p_ref, r_ref, g_ref, d_ref, mo_ref, vo_ref):
        g = ((p_ref[...].astype(F32) + r_ref[0].astype(F32)) + r_ref[1].astype(F32)) + r_ref[2].astype(F32)
        g_ref[...] = g
        d_ref[...], mo_ref[...], vo_ref[...] = _adam_update(w_ref[...], m_ref[...], v_ref[...], g)

    blk = pl.BlockSpec((tr, C), lambda i, ch: (i, 0))
    out = jax.ShapeDtypeStruct((R, C), F32)
    return pl.pallas_call(
        body, name=name,
        grid_spec=pltpu.PrefetchScalarGridSpec(
            num_scalar_prefetch=1, grid=(R // tr,),
            in_specs=[blk, blk, blk,
                      pl.BlockSpec((None, tr, C), lambda i, ch: (ch[0], i, 0)),
                      pl.BlockSpec((3, tr, C), lambda i, ch: (0, i, 0))],
            out_specs=[blk, blk, blk, blk]),
        out_shape=[out, out, out, out],
        compiler_params=_params("parallel"),
    )(chip, w, m, v, chip_sums, recv)


def _adamw_replicated(w, m, v, parts, name):
    def body(w_ref, m_ref, v_ref, p_ref, g_ref, d_ref, mo_ref, vo_ref):
        g = p_ref[0]
        for dev in range(1, N_DEV):
            g = g + p_ref[dev]
        g_ref[...] = g
        d_ref[...], mo_ref[...], vo_ref[...] = _adam_update(w_ref[...], m_ref[...], v_ref[...], g)

    out = jax.ShapeDtypeStruct(w.shape, F32)
    return pl.pallas_call(body, name=name, out_shape=[out, out, out, out], compiler_params=_params())(w, m, v, parts)


def _cols_from_slots(g):
    return g.transpose(1, 0, 2).reshape(g.shape[1], N_DEV * g.shape[2])


def _cols_to_slots(w):
    k, n = w.shape
    return w.reshape(k, N_DEV, n // N_DEV).transpose(1, 0, 2)


def _permute(t, dil):
    c = t.shape[1]
    return t.reshape(S // dil, dil, c).transpose(1, 0, 2).reshape(S, c)


def _unpermute(t, dil):
    c = t.shape[1]
    return t.reshape(dil, S // dil, c).transpose(1, 0, 2).reshape(S, c)


def _group_permute(t):
    return jnp.stack([_permute(t[:, g * B_OUT:(g + 1) * B_OUT], B_DILS[g]) for g in range(3)])


def _same_permute(t):
    return jnp.stack([_permute(t, d) for d in B_DILS])


def _group_unpermute(t):
    return jnp.stack([_unpermute(t[g], B_DILS[g]) for g in range(3)])


SMALL_ROWS = 144


def _pack_small(a_b_f, kv_g, mix_g, ffn_g, conv_b, fin_g):
    flat = jnp.concatenate([a_b_f.reshape(-1), kv_g.reshape(-1), mix_g.reshape(-1), ffn_g.reshape(-1),
                            conv_b.reshape(-1), fin_g.reshape(-1)])
    return jnp.pad(flat, (0, SMALL_ROWS * LANES - flat.shape[0])).reshape(SMALL_ROWS, LANES)


def _unpack_small(p):
    flat = p.reshape(-1)
    out, off = [], 0
    for shape in ((1, A_HEADS), (D,), (2, D), (2, D), (2, 2 * D_FF), (D,)):
        size = math.prod(shape)
        out.append(flat[off:off + size].reshape(shape))
        off += size
    return out


def _local_step(x0, target, w_in_pad, w_out, w_q, w_bo, w_kvf, w_up, w_down, conv_w,
                a_b_f, kv_norm_g, mix_norm_g, ffn_norm_g, ffn_conv_b, final_norm_g):
    w_qkv, w_f = w_in_pad[:, :A_QKV], w_in_pad[:, A_QKV:]
    conv_b = ffn_conv_b.reshape(2, 1, 2 * D_FF)
    slopes = jnp.exp2(-8.0 * jnp.arange(1, 25, dtype=F32) / 24)

    def gain(g):
        return g.reshape(1, D)

    (h1,) = _rmsnorm_fwd(x0, [gain(mix_norm_g[0])], "norm_mix0")
    qkv = _matmul(h1, w_qkv, mode="nn", out_dtype=BF16, name="proj_qkv", tm=S, tn=512)
    z = _matmul(h1, w_f, mode="nn", out_dtype=F32, name="proj_gate", tm=S, tn=LANES)
    z_t = z[:, :A_HEADS].T
    b_f = a_b_f.reshape(A_HEADS, 1)
    c_t = _fox_prep_fwd(z_t, b_f, "fox_prep")
    c_t2 = c_t.reshape(N_PAIRS, 2, S)
    o_a, lse_a = _fox_fwd(qkv, c_t2, "fox_fwd")
    x1 = _matmul(o_a, w_out, mode="nn", out_dtype=F32, name="a_out", tm=512, tn=D, res=x0)

    def ffn_fwd(xin, layer):
        (h,) = _rmsnorm_fwd(xin, [gain(ffn_norm_g[layer])], f"norm_ffn{layer}")
        u = _matmul(h, w_up[layer], mode="nn", out_dtype=BF16, name=f"ffn_up{layer}", tm=S, tn=512)
        act = _convgate_fwd(u, conv_w[layer], conv_b[layer], f"convgate{layer}")
        xout = _matmul(act, w_down[layer], mode="nn", out_dtype=F32, name=f"ffn_down{layer}", tm=512, tn=D, res=xin)
        return h, u, act, xout

    h2, u0, act0, x2 = ffn_fwd(x1, 0)
    hk, h3 = _rmsnorm_fwd(x2, [gain(kv_norm_g), gain(mix_norm_g[1])], "norm_kv_mix1")
    kv = _matmul(hk, w_kvf, mode="nn", out_dtype=BF16, name="proj_kv", tm=S, tn=512)
    qb = _matmul(h3, w_q, mode="nn", out_dtype=BF16, name="proj_qb", tm=S, tn=512)
    qp, kp, vp = _group_permute(qb), _group_permute(kv[:, :B_Q]), _group_permute(kv[:, B_Q:])
    og_p, lseg_p = _dil_fwd(qp, kp, vp, slopes, "dil_fwd")
    o_b, lse_b = _dil_merge(_group_unpermute(og_p), _group_unpermute(lseg_p), "dil_merge")
    x3 = _matmul(o_b, w_bo, mode="nn", out_dtype=F32, name="b_out", tm=512, tn=D, res=x2)
    h4, u1, act1, x4 = ffn_fwd(x3, 1)
    loss_blk, dx4, dx4b, dg_final = _final_loss(x4, target, gain(final_norm_g), "final_loss")

    def ffn_bwd(dx, dxb, xin, h, u, act, layer):
        dact = _matmul(dxb, w_down[layer], mode="nt", out_dtype=BF16, name=f"d_act{layer}", tm=S, tn=256)
        dw_down = _matmul_tn(act, dxb, out_dtype=BF16, name=f"dw_down{layer}")
        da, dg, dwa, dwg, dba, dbg = _convgate_bwd(u, conv_w[layer], conv_b[layer], dact, f"convgate_bwd{layer}")
        du_a = _conv_input_bwd(da, conv_w[layer][:, :D_FF], f"conv_in_bwd_a{layer}")
        du_g = _conv_input_bwd(dg, conv_w[layer][:, D_FF:], f"conv_in_bwd_g{layer}")
        dw_up = jnp.concatenate(
            [_matmul_tn(h, du_a, out_dtype=BF16, name=f"dw_up_a{layer}"),
             _matmul_tn(h, du_g, out_dtype=BF16, name=f"dw_up_g{layer}")], axis=1)
        dh = _matmul(du_a, w_up[layer][:, :D_FF], mode="nt", out_dtype=F32, name=f"dh_ffn_a{layer}", tm=512, tn=D)
        dh = _matmul(du_g, w_up[layer][:, D_FF:], mode="nt", out_dtype=F32, name=f"dh_ffn_g{layer}", tm=512, tn=D,
                     res=dh)
        dxin, dxinb, dgain = _rmsnorm_bwd(xin, dh, gain(ffn_norm_g[layer]), dx, f"norm_ffn_bwd{layer}")
        dconv_w = jnp.concatenate([dwa, dwg], axis=1)
        dconv_b = jnp.concatenate([dba, dbg], axis=1)
        return dxin, dxinb, dgain, dw_up, dw_down, dconv_w, dconv_b

    dx3, dx3b, dg_ffn1, dw_up1, dw_down1, dconv_w1, dconv_b1 = ffn_bwd(dx4, dx4b, x3, h4, u1, act1, 1)

    do_b = _matmul(dx3b, w_bo, mode="nt", out_dtype=BF16, name="d_ob", tm=1024, tn=B_OUT)
    dw_bo = _matmul_tn(o_b, dx3b, out_dtype=BF16, name="dw_bo")
    dl_b = _head_rowsum(do_b, o_b, "delta_b")
    dqp, dkp, dvp = _dil_bwd(qp, kp, vp, _same_permute(do_b), _same_permute(lse_b), _same_permute(dl_b),
                             slopes, "dil_bwd")

    def natural(tp):
        return jnp.concatenate([_unpermute(tp[g], B_DILS[g]) for g in range(3)], axis=1)

    dqb = natural(dqp)
    dkv = jnp.concatenate([natural(dkp), natural(dvp)], axis=1)
    dw_q = _matmul_tn(h3, dqb, out_dtype=BF16, name="dw_q")
    dw_kv = _matmul_tn(hk, dkv, out_dtype=BF16, name="dw_kv")
    dh3 = _matmul(dqb, w_q, mode="nt", out_dtype=F32, name="dh_mix1", tm=512, tn=D)
    dhk = _matmul(dkv, w_kvf, mode="nt", out_dtype=F32, name="dh_kv", tm=512, tn=D)
    dx2, _, dg_mix1 = _rmsnorm_bwd(x2, dh3, gain(mix_norm_g[1]), dx3, "norm_mix1_bwd")
    dx2, dx2b, dg_kv = _rmsnorm_bwd(x2, dhk, gain(kv_norm_g), dx2, "norm_kv_bwd")

    dx1, dx1b, dg_ffn0, dw_up0, dw_down0, dconv_w0, dconv_b0 = ffn_bwd(dx2, dx2b, x1, h2, u0, act0, 0)

    do_a = _matmul(dx1b, w_out, mode="nt", out_dtype=BF16, name="d_oa", tm=512, tn=D)
    dw_out = _matmul_tn(o_a, dx1b, out_dtype=BF16, name="dw_out")
    dl_a = _head_rowsum(do_a, o_a, "delta_a")
    dq_a, dk_a, dv_a, dcol, drow = _fox_bwd(qkv, do_a, lse_a, dl_a, c_t2, "fox_bwd")

    def head_sums(t):
        return t.reshape(S, N_PAIRS, 2, HEAD_DIM)[:, :, ::-1, 0].reshape(S, A_HEADS).T

    dz_t, db_f = _fox_prep_bwd(head_sums(drow), head_sums(dcol), z_t, b_f, "fox_prep_bwd")
    dz = jnp.pad(dz_t.T, ((0, 0), (0, LANES - A_HEADS))).astype(BF16)
    dproj = jnp.concatenate([dq_a.astype(BF16), dk_a, dv_a, dz], axis=1)
    dw_in = _matmul_tn(h1, dproj, out_dtype=BF16, name="dw_in")
    dh1 = _matmul(dproj, w_in_pad, mode="nt", out_dtype=F32, name="dh_mix0", tm=512, tn=D)
    grad_x, _, dg_mix0 = _rmsnorm_bwd(x0, dh1, gain(mix_norm_g[0]), dx1, "norm_mix0_bwd")

    dw_up = jnp.stack([dw_up0, dw_up1])
    dw_down = jnp.stack([dw_down0, dw_down1])
    dconv_w = jnp.stack([dconv_w0, dconv_w1])
    dg_mix = jnp.concatenate([dg_mix0, dg_mix1], axis=0)
    dg_ffn = jnp.concatenate([dg_ffn0, dg_ffn1], axis=0)
    dconv_b = jnp.concatenate([dconv_b0, dconv_b1], axis=0)
    big = (dw_in[:, :A_QKV + A_HEADS], dw_out, dw_q, dw_bo, dw_kv, dw_up, dw_down, dconv_w)
    small = (db_f, dg_kv, dg_mix, dg_ffn, dconv_b, dg_final)
    return loss_blk, grad_x, big, small


def kernel(x, a_w_in, a_b_f, a_w_out, b_w_q, b_w_out, kv_norm_g, w_kv, mix_norm_g, ffn_norm_g, ffn_w_up, ffn_conv_w, ffn_conv_b, ffn_w_down, final_norm_g, loss_target, m_a_w_in, m_a_b_f, m_a_w_out, m_b_w_q, m_b_w_out, m_kv_norm_g, m_w_kv, m_mix_norm_g, m_ffn_norm_g, m_ffn_w_up, m_ffn_conv_w, m_ffn_conv_b, m_ffn_w_down, m_final_norm_g, v_a_w_in, v_a_b_f, v_a_w_out, v_b_w_q, v_b_w_out, v_kv_norm_g, v_w_kv, v_mix_norm_g, v_ffn_norm_g, v_ffn_w_up, v_ffn_conv_w, v_ffn_conv_b, v_ffn_w_down, v_final_norm_g):
    xi, yi, ci = lax.axis_index("x"), lax.axis_index("y"), lax.axis_index("c")
    core = jnp.reshape(ci, (1,)).astype(jnp.int32)
    chip = jnp.reshape(2 * xi + yi, (1,)).astype(jnp.int32)

    def shards(a_w_in, a_w_out, b_w_q, b_w_out, w_kv, ffn_w_up, ffn_w_down, ffn_conv_w):
        return [a_w_in[0], a_w_out[0], b_w_q[0], b_w_out[0], w_kv, ffn_w_up.reshape(2 * D, -1),
                ffn_w_down.reshape(-1, D), ffn_conv_w.reshape(6, -1)]

    w_loc = shards(a_w_in, a_w_out, b_w_q, b_w_out, w_kv, ffn_w_up, ffn_w_down, ffn_conv_w)
    m_loc = shards(m_a_w_in, m_a_w_out, m_b_w_q, m_b_w_out, m_w_kv, m_ffn_w_up, m_ffn_w_down, m_ffn_conv_w)
    v_loc = shards(v_a_w_in, v_a_w_out, v_b_w_q, v_b_w_out, v_w_kv, v_ffn_w_up, v_ffn_w_down, v_ffn_conv_w)

    gathered = _all_gather([w.astype(BF16) for w in w_loc[:7]] + [w_loc[7]], "gather_weights")
    w_in = _cols_from_slots(gathered[0])
    w_in_pad = jnp.pad(w_in, ((0, 0), (0, A_PROJ_PAD - w_in.shape[1])))
    w_out = gathered[1].reshape(D, D)
    w_q = _cols_from_slots(gathered[2])
    w_bo = _cols_from_slots(gathered[3])
    w_kvf = _cols_from_slots(gathered[4])
    w_up = gathered[5].reshape(N_DEV, 2, D, -1).transpose(1, 2, 0, 3).reshape(2, D, 2 * D_FF)
    w_down = gathered[6].reshape(N_DEV, 2, -1, D).transpose(1, 0, 2, 3).reshape(2, D_FF, D)
    conv_w = gathered[7].reshape(N_DEV, 2, 3, -1).transpose(1, 2, 0, 3).reshape(2, 3, 2 * D_FF)

    loss_blk, grad_x, big_grads, small_grads = _local_step(
        x[0], loss_target[0], w_in_pad, w_out, w_q, w_bo, w_kvf, w_up, w_down, conv_w,
        a_b_f, kv_norm_g, mix_norm_g, ffn_norm_g, ffn_conv_b, final_norm_g)
    dw_in, dw_out, dw_q, dw_bo, dw_kv, dw_up, dw_down, dconv_w = big_grads

    slots = [
        _cols_to_slots(dw_in),
        dw_out.reshape(N_DEV, D // N_DEV, D),
        _cols_to_slots(dw_q),
        _cols_to_slots(dw_bo),
        _cols_to_slots(dw_kv),
        dw_up.reshape(2, D, N_DEV, -1).transpose(2, 0, 1, 3).reshape(N_DEV, 2 * D, -1),
        dw_down.reshape(2, N_DEV, -1, D).transpose(1, 0, 2, 3).reshape(N_DEV, -1, D),
        dconv_w.reshape(2, 3, N_DEV, -1).transpose(2, 0, 1, 3).reshape(N_DEV, 6, -1),
    ]
    by_chip = [s.reshape((4, 2) + s.shape[1:]) for s in slots]
    from_sibling = _sibling_exchange(by_chip, "grad_sibling_exchange")
    chip_sums = [_pair_add(g, r, core, f"grad_pair_add{k}") for k, (g, r) in enumerate(zip(by_chip, from_sibling))]
    from_chips = _chip_exchange(chip_sums, "grad_chip_exchange")

    big = [_adamw_sharded(w_loc[k], m_loc[k], v_loc[k], chip_sums[k], from_chips[k], chip, f"adamw{k}")
           for k in range(8)]

    small_part = _pack_small(*small_grads)
    (small_parts,) = _all_gather([small_part], "gather_small_grads")
    small = _adamw_replicated(
        _pack_small(a_b_f, kv_norm_g, mix_norm_g, ffn_norm_g, ffn_conv_b, final_norm_g),
        _pack_small(m_a_b_f, m_kv_norm_g, m_mix_norm_g, m_ffn_norm_g, m_ffn_conv_b, m_final_norm_g),
        _pack_small(v_a_b_f, v_kv_norm_g, v_mix_norm_g, v_ffn_norm_g, v_ffn_conv_b, v_final_norm_g),
        small_parts, "adamw_small")

    loss = lax.psum(loss_blk[0, 0], ("x", "y", "c"))

    def assemble(kind):
        b = [r[kind] for r in big]
        s_abf, s_kv, s_mix, s_ffn, s_cb, s_fin = _unpack_small(small[kind])
        return [b[0][None], s_abf, b[1][None], b[2][None], b[3][None], s_kv, b[4], s_mix, s_ffn,
                b[5].reshape(2, D, -1), b[7].reshape(2, 3, -1), s_cb, b[6].reshape(2, -1, D), s_fin]

    return (loss, grad_x[None], *assemble(0), *assemble(1), *assemble(2), *assemble(3))
```

```python
import functools
import math

import jax
import jax.numpy as jnp
from jax import lax
from jax.experimental import pallas as pl
from jax.experimental.pallas import tpu as pltpu

F32 = jnp.float32
BF16 = jnp.bfloat16

S = 4096
D = 1024
N_DEV = 8
A_HEADS = 16
HEAD_DIM = 64
A_QKV = 3072
A_PROJ_PAD = 3200
B_Q = 1536
B_OUT = 512
B_KV = 3072
B_W = 128
B_DILS = (1, 4, 16)
D_FF = 2816
RMS_EPS = 1e-6
SCALE = HEAD_DIM ** -0.5
NEG = -1e30

ADAM_LR = 0.001
ADAM_B1 = 0.9
ADAM_B2 = 0.999
ADAM_EPS = 1e-08
ADAM_WD = 0.01
ADAM_STEP = 10

LANES = 128
VMEM_LIMIT = 56 * 1024 * 1024
MESH = pl.DeviceIdType.MESH
ANY = pl.BlockSpec(memory_space=pl.ANY)

NT_DIMS = (((1,), (1,)), ((), ()))
TN_DIMS = (((0,), (0,)), ((), ()))
NN_DIMS = (((1,), (0,)), ((), ()))


def _params(*sem):
    return pltpu.CompilerParams(dimension_semantics=sem if sem else None, vmem_limit_bytes=VMEM_LIMIT)


def _dot(a, b, dims=NN_DIMS):
    return lax.dot_general(a, b, dims, preferred_element_type=F32)


def _split_dot(x, mat, pieces):
    out = None
    rem = x
    for _ in range(pieces):
        part = rem.astype(BF16)
        rem = rem - part.astype(F32)
        d = _dot(part, mat)
        out = d if out is None else out + d
    return out


def _pick(n, prefs):
    for p in prefs:
        if n % p == 0:
            return p
    return n


def _all_gather(arrays, name):
    n = len(arrays)

    def body(*refs):
        ins = refs[:n]
        outs = refs[n:2 * n]
        send_sems, recv_sems, local_sems = refs[2 * n:]
        x, y, c = lax.axis_index("x"), lax.axis_index("y"), lax.axis_index("c")
        me, sibling = (x, y, c), (x, y, 1 - c)
        chips = [(1 - x, y), (x, 1 - y), (1 - x, 1 - y)]

        def slot(a, px, py, pc):
            return outs[a].at[4 * px + 2 * py + pc]

        def copy(a, k, block, to, src=None):
            return pltpu.make_async_remote_copy(
                src_ref=slot(a, *block) if src is None else src, dst_ref=slot(a, *block),
                send_sem=send_sems.at[a, k], recv_sem=recv_sems.at[a, k],
                device_id=to, device_id_type=MESH)

        mine = [pltpu.make_async_copy(ins[a], slot(a, *me), local_sems.at[a]) for a in range(n)]
        for cp in mine:
            cp.start()
        first = []
        for a in range(n):
            first.append(copy(a, 0, me, sibling, src=ins[a]))
            first += [copy(a, 1 + j, me, (*chip, c), src=ins[a]) for j, chip in enumerate(chips)]
        for cp in first:
            cp.start()
        passed = []
        for j, chip in enumerate(chips):
            for a in range(n):
                copy(a, 1 + j, (*chip, c), me).wait_recv()
                fwd = copy(a, 4 + j, (*chip, c), sibling)
                fwd.start()
                passed.append(fwd)
        for a in range(n):
            copy(a, 0, sibling, me).wait_recv()
            for j, chip in enumerate(chips):
                copy(a, 4 + j, (*chip, 1 - c), me).wait_recv()
        for cp in first + passed:
            cp.wait_send()
        for cp in mine:
            cp.wait()

    return pl.pallas_call(
        body, name=name,
        out_shape=[jax.ShapeDtypeStruct((N_DEV,) + a.shape, a.dtype) for a in arrays],
        in_specs=[ANY] * n, out_specs=[ANY] * n,
        scratch_shapes=[pltpu.SemaphoreType.DMA((n, 7)), pltpu.SemaphoreType.DMA((n, 7)),
                        pltpu.SemaphoreType.DMA((n,))],
    )(*arrays)


def _sibling_exchange(arrays, name):
    n = len(arrays)

    def body(*refs):
        ins = refs[:n]
        outs = refs[n:2 * n]
        send_sems, recv_sems = refs[2 * n:]
        x, y, c = lax.axis_index("x"), lax.axis_index("y"), lax.axis_index("c")
        copies = []
        for a in range(n):
            for chip in range(4):
                copies.append(pltpu.make_async_remote_copy(
                    src_ref=ins[a].at[chip, 1 - c], dst_ref=outs[a].at[chip],
                    send_sem=send_sems.at[a, chip], recv_sem=recv_sems.at[a, chip],
                    device_id=(x, y, 1 - c), device_id_type=MESH))
        for cp in copies:
            cp.start()
        for cp in copies:
            cp.wait()

    return pl.pallas_call(
        body, name=name,
        out_shape=[jax.ShapeDtypeStruct((4,) + a.shape[2:], a.dtype) for a in arrays],
        in_specs=[ANY] * n, out_specs=[ANY] * n,
        scratch_shapes=[pltpu.SemaphoreType.DMA((n, 4)), pltpu.SemaphoreType.DMA((n, 4))],
    )(*arrays)


def _chip_exchange(arrays, name):
    n = len(arrays)

    def body(*refs):
        ins = refs[:n]
        outs = refs[n:2 * n]
        send_sems, recv_sems = refs[2 * n:]
        x, y, c = lax.axis_index("x"), lax.axis_index("y"), lax.axis_index("c")
        chips = [(1 - x, y), (x, 1 - y), (1 - x, 1 - y)]
        copies = []
        for a in range(n):
            for j, (qx, qy) in enumerate(chips):
                copies.append(pltpu.make_async_remote_copy(
                    src_ref=ins[a].at[2 * qx + qy], dst_ref=outs[a].at[j],
                    send_sem=send_sems.at[a, j], recv_sem=recv_sems.at[a, j],
                    device_id=(qx, qy, c), device_id_type=MESH))
        for cp in copies:
            cp.start()
        for cp in copies:
            cp.wait()

    return pl.pallas_call(
        body, name=name,
        out_shape=[jax.ShapeDtypeStruct((3,) + a.shape[1:], a.dtype) for a in arrays],
        in_specs=[ANY] * n, out_specs=[ANY] * n,
        scratch_shapes=[pltpu.SemaphoreType.DMA((n, 3)), pltpu.SemaphoreType.DMA((n, 3))],
    )(*arrays)


MM_ROWS = 512


def _matmul(a, b, *, mode, out_dtype, name, tm, tn, res=None):
    if mode == "nn":
        (M, K), (K2, N) = a.shape, b.shape
    else:
        (M, K), (N, K2) = a.shape, b.shape
    assert K == K2, (a.shape, b.shape, mode)
    tm, tn = min(tm, M), min(tn, N)
    sm = min(tm, MM_ROWS)
    assert M % tm == 0 and N % tn == 0 and tm % sm == 0, (M, N, K, tm, tn)
    dims = NN_DIMS if mode == "nn" else NT_DIMS
    a_spec = pl.BlockSpec((tm, K), lambda i, j: (i, 0))
    if mode == "nt":
        b_spec = pl.BlockSpec((tn, K), lambda i, j: (j, 0))
    else:
        b_spec = pl.BlockSpec((K, tn), lambda i, j: (0, j))
    o_spec = pl.BlockSpec((tm, tn), lambda i, j: (i, j))
    has_res = res is not None

    def body(*refs):
        a_ref, b_ref = refs[0], refs[1]
        r_ref = refs[2] if has_res else None
        o_ref = refs[2 + has_res]

        def chunk(r, carry):
            rows = pl.ds(pl.multiple_of(r * sm, sm), sm)
            total = _dot(a_ref[rows, :], b_ref[...], dims)
            if has_res:
                total = total + r_ref[rows, :]
            o_ref[rows, :] = total.astype(out_dtype)
            return carry

        lax.fori_loop(0, tm // sm, chunk, 0)

    return pl.pallas_call(
        body, name=name, grid=(M // tm, N // tn),
        out_shape=jax.ShapeDtypeStruct((M, N), out_dtype),
        in_specs=[a_spec, b_spec] + ([o_spec] if has_res else []),
        out_specs=o_spec,
        compiler_params=_params("parallel", "parallel"),
    )(*((a, b, res) if has_res else (a, b)))


def _matmul_tn(a, b, *, out_dtype, name, tk=512, sm=256):
    (K, M), (K2, N) = a.shape, b.shape
    assert K == K2 and K % tk == 0 and M % sm == 0, (a.shape, b.shape)
    nk = K // tk

    def body(a_ref, b_ref, o_ref, acc_ref):
        k = pl.program_id(0)

        @pl.when(k == 0)
        def _():
            acc_ref[...] = jnp.zeros_like(acc_ref)

        def chunk(mi, carry):
            cols = pl.ds(pl.multiple_of(mi * sm, sm), sm)
            acc_ref[cols, :] += _dot(a_ref[:, cols].T, b_ref[...])
            return carry

        lax.fori_loop(0, M // sm, chunk, 0)

        @pl.when(k == nk - 1)
        def _():
            def emit(mi, carry):
                rows = pl.ds(pl.multiple_of(mi * sm, sm), sm)
                o_ref[rows, :] = acc_ref[rows, :].astype(out_dtype)
                return carry
            lax.fori_loop(0, M // sm, emit, 0)

    return pl.pallas_call(
        body, name=name, grid=(nk,),
        out_shape=jax.ShapeDtypeStruct((M, N), out_dtype),
        in_specs=[pl.BlockSpec((tk, M), lambda k: (k, 0)), pl.BlockSpec((tk, N), lambda k: (k, 0))],
        out_specs=pl.BlockSpec((M, N), lambda k: (0, 0)),
        scratch_shapes=[pltpu.VMEM((M, N), F32)],
        compiler_params=_params("arbitrary"),
    )(a, b)


def _rmsnorm_fwd(x, gains, name, tr=256):
    n = len(gains)

    def body(*refs):
        x_ref = refs[0]
        xv = x_ref[...]
        r = lax.rsqrt(jnp.mean(xv * xv, axis=-1, keepdims=True) + RMS_EPS)
        y = xv * r
        for a in range(n):
            refs[1 + n + a][...] = (y * refs[1 + a][...]).astype(BF16)

    row = pl.BlockSpec((tr, D), lambda i: (i, 0))
    gain = pl.BlockSpec((1, D), lambda i: (0, 0))
    return pl.pallas_call(
        body, name=name, grid=(S // tr,),
        out_shape=[jax.ShapeDtypeStruct((S, D), BF16)] * n,
        in_specs=[row] + [gain] * n, out_specs=[row] * n,
        compiler_params=_params("parallel"),
    )(x, *gains)


def _rmsnorm_bwd(x, dy, g, dres, name, tr=256):
    def body(x_ref, dy_ref, g_ref, dres_ref, dx_ref, dxb_ref, dg_ref):
        xv = x_ref[...]
        dyv = dy_ref[...]
        r = lax.rsqrt(jnp.mean(xv * xv, axis=-1, keepdims=True) + RMS_EPS)
        xhat = xv * r
        dxhat = dyv * g_ref[...]
        mean_term = jnp.mean(dxhat * xhat, axis=-1, keepdims=True)
        dx = r * (dxhat - xhat * mean_term) + dres_ref[...]
        dx_ref[...] = dx
        dxb_ref[...] = dx.astype(BF16)
        part = jnp.sum(dyv * xhat, axis=0, keepdims=True)

        @pl.when(pl.program_id(0) == 0)
        def _():
            dg_ref[...] = part

        @pl.when(pl.program_id(0) > 0)
        def _():
            dg_ref[...] += part

    row = pl.BlockSpec((tr, D), lambda i: (i, 0))
    gain = pl.BlockSpec((1, D), lambda i: (0, 0))
    return pl.pallas_call(
        body, name=name, grid=(S // tr,),
        out_shape=[jax.ShapeDtypeStruct((S, D), F32), jax.ShapeDtypeStruct((S, D), BF16),
                   jax.ShapeDtypeStruct((1, D), F32)],
        in_specs=[row, row, gain, row], out_specs=[row, row, gain],
        compiler_params=_params("arbitrary"),
    )(x, dy, g, dres)


def _final_loss(x, target, g, name, tr=256):
    def body(x_ref, t_ref, g_ref, loss_ref, dx_ref, dxb_ref, dg_ref):
        xv = x_ref[...]
        gv = g_ref[...]
        r = lax.rsqrt(jnp.mean(xv * xv, axis=-1, keepdims=True) + RMS_EPS)
        xhat = xv * r
        err = xhat * gv - t_ref[...]
        row_loss = jnp.mean(err * err, axis=-1, keepdims=True)
        lpart = 0.5 * jnp.sum(row_loss, axis=0, keepdims=True)
        dyv = err / D
        dxhat = dyv * gv
        mean_term = jnp.mean(dxhat * xhat, axis=-1, keepdims=True)
        dx = r * (dxhat - xhat * mean_term)
        dx_ref[...] = dx
        dxb_ref[...] = dx.astype(BF16)
        gpart = jnp.sum(dyv * xhat, axis=0, keepdims=True)

        @pl.when(pl.program_id(0) == 0)
        def _():
            dg_ref[...] = gpart
            loss_ref[...] = jnp.broadcast_to(lpart, loss_ref.shape)

        @pl.when(pl.program_id(0) > 0)
        def _():
            dg_ref[...] += gpart
            loss_ref[...] += jnp.broadcast_to(lpart, loss_ref.shape)

    row = pl.BlockSpec((tr, D), lambda i: (i, 0))
    gain = pl.BlockSpec((1, D), lambda i: (0, 0))
    lspec = pl.BlockSpec((8, LANES), lambda i: (0, 0))
    return pl.pallas_call(
        body, name=name, grid=(S // tr,),
        out_shape=[jax.ShapeDtypeStruct((8, LANES), F32), jax.ShapeDtypeStruct((S, D), F32),
                   jax.ShapeDtypeStruct((S, D), BF16), jax.ShapeDtypeStruct((1, D), F32)],
        in_specs=[row, row, gain], out_specs=[lspec, row, row, gain],
        compiler_params=_params("arbitrary"),
    )(x, target, g)


CONV_TR = 512
CONV_TC = 256
CONV_NJ = D_FF // CONV_TC
HALO = 16


def _causal_taps(cur_ref, prev_ref, first):
    xv = cur_ref[...].astype(F32)
    pv = prev_ref[...].astype(F32)
    p1 = jnp.where(first, 0.0, pv[HALO - 1:HALO, :])
    p2 = jnp.where(first, 0.0, pv[HALO - 2:HALO - 1, :])
    r1, r2 = pltpu.roll(xv, 1, 0), pltpu.roll(xv, 2, 0)
    row = lax.broadcasted_iota(jnp.int32, (8, xv.shape[1]), 0)
    xm1 = jnp.concatenate([jnp.where(row == 0, p1, r1[0:8]), r1[8:]], axis=0)
    xm2 = jnp.concatenate([jnp.where(row == 0, p2, jnp.where(row == 1, p1, r2[0:8])), r2[8:]], axis=0)
    return xv, xm1, xm2


def _conv_specs():
    def prev_row(i):
        return jnp.maximum(i * (CONV_TR // HALO) - 1, 0)
    ua = pl.BlockSpec((CONV_TR, CONV_TC), lambda i, j: (i, j))
    ug = pl.BlockSpec((CONV_TR, CONV_TC), lambda i, j: (i, j + CONV_NJ))
    pa = pl.BlockSpec((HALO, CONV_TC), lambda i, j: (prev_row(i), j))
    pg = pl.BlockSpec((HALO, CONV_TC), lambda i, j: (prev_row(i), j + CONV_NJ))
    wa = pl.BlockSpec((3, CONV_TC), lambda i, j: (0, j))
    wg = pl.BlockSpec((3, CONV_TC), lambda i, j: (0, j + CONV_NJ))
    ba = pl.BlockSpec((1, CONV_TC), lambda i, j: (0, j))
    bg = pl.BlockSpec((1, CONV_TC), lambda i, j: (0, j + CONV_NJ))
    return [ua, pa, ug, pg, wa, wg, ba, bg]


def _convgate_fwd(u, w, b, name):
    def body(ua, pa, ug, pg, wa, wg, ba, bg, o_ref):
        first = pl.program_id(0) == 0
        x0, x1, x2 = _causal_taps(ua, pa, first)
        ac = wa[0:1, :] * x2 + wa[1:2, :] * x1 + wa[2:3, :] * x0 + ba[...]
        x0, x1, x2 = _causal_taps(ug, pg, first)
        gc = wg[0:1, :] * x2 + wg[1:2, :] * x1 + wg[2:3, :] * x0 + bg[...]
        sg = 1.0 / (1.0 + jnp.exp(-gc))
        o_ref[...] = (gc * sg * ac).astype(BF16)

    return pl.pallas_call(
        body, name=name, grid=(S // CONV_TR, CONV_NJ),
        out_shape=jax.ShapeDtypeStruct((S, D_FF), BF16),
        in_specs=_conv_specs(),
        out_specs=pl.BlockSpec((CONV_TR, CONV_TC), lambda i, j: (i, j)),
        compiler_params=_params("parallel", "parallel"),
    )(u, u, u, u, w, w, b, b)


def _convgate_bwd(u, w, b, dact, name):
    def body(ua, pa, ug, pg, wa, wg, ba, bg, d_ref, da_ref, dg_ref, dwa_ref, dwg_ref, dba_ref, dbg_ref):
        i = pl.program_id(1)
        first = i == 0
        a0, a1, a2 = _causal_taps(ua, pa, first)
        ac = wa[0:1, :] * a2 + wa[1:2, :] * a1 + wa[2:3, :] * a0 + ba[...]
        g0, g1, g2 = _causal_taps(ug, pg, first)
        gc = wg[0:1, :] * g2 + wg[1:2, :] * g1 + wg[2:3, :] * g0 + bg[...]
        sg = 1.0 / (1.0 + jnp.exp(-gc))
        dact_v = d_ref[...].astype(F32)
        da = dact_v * (gc * sg)
        dg = dact_v * ac * (sg * (1.0 + gc * (1.0 - sg)))
        da_ref[...] = da.astype(BF16)
        dg_ref[...] = dg.astype(BF16)

        def col(v):
            return jnp.sum(v, axis=0, keepdims=True)

        parts = [col(da * a2), col(da * a1), col(da * a0), col(dg * g2), col(dg * g1), col(dg * g0),
                 col(da), col(dg)]

        @pl.when(first)
        def _():
            for k in range(3):
                dwa_ref[k:k + 1, :] = parts[k]
                dwg_ref[k:k + 1, :] = parts[3 + k]
            dba_ref[...] = parts[6]
            dbg_ref[...] = parts[7]

        @pl.when(i > 0)
        def _():
            for k in range(3):
                dwa_ref[k:k + 1, :] += parts[k]
                dwg_ref[k:k + 1, :] += parts[3 + k]
            dba_ref[...] += parts[6]
            dbg_ref[...] += parts[7]

    def swap(spec):
        return pl.BlockSpec(spec.block_shape, lambda j, i, f=spec.index_map: f(i, j))

    blk = pl.BlockSpec((CONV_TR, CONV_TC), lambda j, i: (i, j))
    w3 = pl.BlockSpec((3, CONV_TC), lambda j, i: (0, j))
    b1 = pl.BlockSpec((1, CONV_TC), lambda j, i: (0, j))
    return pl.pallas_call(
        body, name=name, grid=(CONV_NJ, S // CONV_TR),
        out_shape=[jax.ShapeDtypeStruct((S, D_FF), BF16), jax.ShapeDtypeStruct((S, D_FF), BF16),
                   jax.ShapeDtypeStruct((3, D_FF), F32), jax.ShapeDtypeStruct((3, D_FF), F32),
                   jax.ShapeDtypeStruct((1, D_FF), F32), jax.ShapeDtypeStruct((1, D_FF), F32)],
        in_specs=[swap(s) for s in _conv_specs()] + [blk],
        out_specs=[blk, blk, w3, w3, b1, b1],
        compiler_params=_params("parallel", "arbitrary"),
    )(u, u, u, u, w, w, b, b, dact)


def _conv_input_bwd(d, w, name):
    n_i = S // CONV_TR

    def body(d_ref, n_ref, w_ref, o_ref):
        last = pl.program_id(0) == n_i - 1
        dv = d_ref[...].astype(F32)
        nv = n_ref[...].astype(F32)
        n1 = jnp.where(last, 0.0, nv[0:1, :])
        n2 = jnp.where(last, 0.0, nv[1:2, :])
        r1, r2 = pltpu.roll(dv, CONV_TR - 1, 0), pltpu.roll(dv, CONV_TR - 2, 0)
        row = lax.broadcasted_iota(jnp.int32, (8, dv.shape[1]), 0)
        cut = CONV_TR - 8
        dp1 = jnp.concatenate([r1[:cut], jnp.where(row == 7, n1, r1[cut:])], axis=0)
        dp2 = jnp.concatenate([r2[:cut], jnp.where(row == 7, n2, jnp.where(row == 6, n1, r2[cut:]))], axis=0)
        o_ref[...] = (w_ref[2:3, :] * dv + w_ref[1:2, :] * dp1 + w_ref[0:1, :] * dp2).astype(BF16)

    def next_row(i):
        return jnp.minimum((i + 1) * (CONV_TR // HALO), S // HALO - 1)

    blk = pl.BlockSpec((CONV_TR, CONV_TC), lambda i, j: (i, j))
    return pl.pallas_call(
        body, name=name, grid=(n_i, CONV_NJ),
        out_shape=jax.ShapeDtypeStruct((S, D_FF), BF16),
        in_specs=[blk, pl.BlockSpec((HALO, CONV_TC), lambda i, j: (next_row(i), j)),
                  pl.BlockSpec((3, CONV_TC), lambda i, j: (0, j))],
        out_specs=blk,
        compiler_params=_params("parallel", "parallel"),
    )(d, d, w)


FOX_T = 256
FOX_TQ, FOX_TK = 256, 256
N_PAIRS = A_HEADS // 2


def _lane_masks():
    lane = lax.broadcasted_iota(jnp.int32, (1, LANES), 1)
    return lane, (lane < HEAD_DIM, lane >= HEAD_DIM)


def _fox_prep_fwd(z_t, b, name):
    def body(z_ref, b_ref, c_ref):
        r = lax.broadcasted_iota(jnp.int32, (LANES, LANES), 0)
        cc = lax.broadcasted_iota(jnp.int32, (LANES, LANES), 1)
        upper = (r <= cc).astype(BF16)
        carry = jnp.zeros((A_HEADS, 1), F32)
        for blk in range(S // LANES):
            sl = slice(blk * LANES, (blk + 1) * LANES)
            z = z_ref[:, sl] + b_ref[...]
            lf = jnp.minimum(z, 0.0) - jnp.log(1.0 + jnp.exp(-jnp.abs(z)))
            cs = _split_dot(lf, upper, 3) + carry
            c_ref[:, sl] = cs
            carry = cs[:, LANES - 1:LANES]

    return pl.pallas_call(
        body, name=name, out_shape=jax.ShapeDtypeStruct((A_HEADS, S), F32),
        compiler_params=_params(),
    )(z_t, b)


def _fox_prep_bwd(drow_t, dcol_t, z_t, b, name):
    def body(dr_ref, dc_ref, z_ref, b_ref, dz_ref, db_ref):
        r = lax.broadcasted_iota(jnp.int32, (LANES, LANES), 0)
        cc = lax.broadcasted_iota(jnp.int32, (LANES, LANES), 1)
        lower = (r >= cc).astype(BF16)
        carry = jnp.zeros((A_HEADS, 1), F32)
        db = jnp.zeros((A_HEADS, 1), F32)
        for blk in reversed(range(S // LANES)):
            sl = slice(blk * LANES, (blk + 1) * LANES)
            rc = _split_dot(dr_ref[:, sl] - dc_ref[:, sl], lower, 3) + carry
            carry = rc[:, 0:1]
            z = z_ref[:, sl] + b_ref[...]
            dz = rc / (1.0 + jnp.exp(z))
            dz_ref[:, sl] = dz
            db = db + jnp.sum(dz, axis=1, keepdims=True)
        db_ref[...] = db

    return pl.pallas_call(
        body, name=name,
        out_shape=[jax.ShapeDtypeStruct((A_HEADS, S), F32), jax.ShapeDtypeStruct((A_HEADS, 1), F32)],
        compiler_params=_params(),
    )(drow_t, dcol_t, z_t, b)


def _fox_fwd(qkv, c_t2, name):
    tq, tk = FOX_TQ, FOX_TK

    def body(q_ref, k_ref, v_ref, ct_ref, o_ref, lse_ref):
        qi = pl.program_id(1)
        n_full = jnp.right_shift(qi, (tk // tq).bit_length() - 1)
        lane, masks = _lane_masks()
        q = q_ref[...] * SCALE
        qs = [jnp.where(masks[e], q, jnp.zeros_like(q)) for e in range(2)]

        def scores(j):
            start = pl.multiple_of(j * tk, tk)
            kb = k_ref[pl.ds(start, tk), :]
            return tuple(_dot(qs[e], kb, NT_DIMS) - ct_ref[e:e + 1, pl.ds(start, tk)] for e in range(2))

        def softmax(s, m, masked):
            if masked:
                rows = lax.broadcasted_iota(jnp.int32, (tq, tk), 0) + (qi * tq - n_full * tk)
                cols = lax.broadcasted_iota(jnp.int32, (tq, tk), 1)
                s = tuple(jnp.where(cols <= rows, s[e], NEG) for e in range(2))
            m_new = tuple(jnp.maximum(m[e], jnp.max(s[e], axis=1, keepdims=True)) for e in range(2))
            p = tuple(jnp.exp(s[e] - m_new[e]).astype(BF16) for e in range(2))
            alpha = tuple(jnp.exp(m[e] - m_new[e]) for e in range(2))
            return m_new, p, alpha

        def weighted_values(j, p, alpha, acc):
            start = pl.multiple_of(j * tk, tk)
            vb = v_ref[pl.ds(start, tk), :]
            return tuple(alpha[e] * acc[e] + _dot(p[e], jnp.where(masks[e], vb, jnp.ones_like(vb)))
                         for e in range(2))

        def step(j, carry):
            s, p_prev, a_prev, m, acc = carry
            s_next = scores(j + 1)
            acc = weighted_values(jnp.maximum(j - 1, 0), p_prev, a_prev, acc)
            m, p, alpha = softmax(s, m, False)
            return s_next, p, alpha, m, acc

        two = lambda x: (x, x)
        init = (scores(0), two(jnp.zeros((tq, tk), BF16)), two(jnp.ones((tq, 1), F32)),
                two(jnp.full((tq, 1), NEG, F32)), two(jnp.zeros((tq, LANES), F32)))
        s, p_prev, a_prev, m, acc = lax.fori_loop(0, n_full, step, init)
        acc = weighted_values(jnp.maximum(n_full - 1, 0), p_prev, a_prev, acc)
        (m0, m1), p, alpha = softmax(s, m, True)
        acc0, acc1 = weighted_values(n_full, p, alpha, acc)
        l0 = acc0[:, HEAD_DIM:HEAD_DIM + 1]
        l1 = acc1[:, 0:1]
        o_ref[...] = jnp.where(masks[0], acc0 / l0, acc1 / l1).astype(BF16)
        lse_ref[...] = jnp.where(masks[0], m0 + jnp.log(l0), m1 + jnp.log(l1))

    qspec = pl.BlockSpec((tq, LANES), lambda h, i: (i, h))
    return pl.pallas_call(
        body, name=name, grid=(N_PAIRS, S // tq),
        out_shape=[jax.ShapeDtypeStruct((S, D), BF16), jax.ShapeDtypeStruct((S, D), F32)],
        in_specs=[qspec,
                  pl.BlockSpec((S, LANES), lambda h, i: (0, N_PAIRS + h)),
                  pl.BlockSpec((S, LANES), lambda h, i: (0, 2 * N_PAIRS + h)),
                  pl.BlockSpec((None, 2, S), lambda h, i: (h, 0, 0))],
        out_specs=[qspec, qspec],
        compiler_params=_params("parallel", "parallel"),
    )(qkv, qkv, qkv, c_t2)


def _head_rowsum(a, b, name, tr=256):
    C = a.shape[1]

    def body(a_ref, b_ref, o_ref):
        r = lax.broadcasted_iota(jnp.int32, (LANES, LANES), 0) < HEAD_DIM
        cc = lax.broadcasted_iota(jnp.int32, (LANES, LANES), 1) < HEAD_DIM
        same_head = (r == cc).astype(BF16)
        for blk in range(C // LANES):
            sl = slice(blk * LANES, (blk + 1) * LANES)
            prod = a_ref[:, sl].astype(F32) * b_ref[:, sl].astype(F32)
            o_ref[:, sl] = _split_dot(prod, same_head, 2)

    row = pl.BlockSpec((tr, C), lambda i: (i, 0))
    return pl.pallas_call(
        body, name=name, grid=(S // tr,), out_shape=jax.ShapeDtypeStruct((S, C), F32),
        in_specs=[row, row], out_specs=row, compiler_params=_params("parallel"),
    )(a, b)


def _fox_bwd(qkv, do, lse, delta, c_t2, name):
    t = FOX_T
    nq = S // t

    def body(q_ref, k_ref, v_ref, do_ref, lse_ref, dl_ref, ct_ref, dq_ref, dk_ref, dv_ref, dcol_ref, drow_ref):
        kj = pl.program_id(1)

        @pl.when(kj == 0)
        def _():
            dq_ref[...] = jnp.zeros_like(dq_ref)
            drow_ref[...] = jnp.zeros_like(drow_ref)

        lane, masks = _lane_masks()
        k = k_ref[...]
        v = v_ref[...]
        k_aug = [jnp.where(masks[e], k * SCALE, jnp.ones_like(k)) for e in range(2)]
        cs = [ct_ref[e:e + 1, :] for e in range(2)]

        def step(i, carry, masked):
            dk_acc, dv_acc = list(carry[0]), carry[1]
            r0 = pl.multiple_of(i * t, t)
            qb = q_ref[pl.ds(r0, t), :] * SCALE
            dob = do_ref[pl.ds(r0, t), :]
            dq_parts = []
            for e in range(2):
                lo = e * HEAD_DIM
                qe = jnp.where(masks[e], qb, jnp.zeros_like(qb))
                q_aug = jnp.where(masks[e], qb, jnp.ones_like(qb))
                doe = jnp.where(masks[e], dob, jnp.zeros_like(dob))
                s = _dot(qe, k, NT_DIMS) - cs[e]
                if masked:
                    rows = lax.broadcasted_iota(jnp.int32, (t, t), 0)
                    cols = lax.broadcasted_iota(jnp.int32, (t, t), 1)
                    s = jnp.where(cols <= rows, s, NEG)
                p = jnp.exp(s - lse_ref[pl.ds(r0, t), lo:lo + 1])
                ds = (p * (_dot(doe, v, NT_DIMS) - dl_ref[pl.ds(r0, t), lo:lo + 1])).astype(BF16)
                dv_acc = dv_acc + _dot(p.astype(BF16), doe, TN_DIMS)
                dk_acc[e] = dk_acc[e] + _dot(ds, q_aug, TN_DIMS)
                dq_parts.append(_dot(ds, k_aug[e]))
            dq_ref[pl.ds(r0, t), :] += jnp.where(masks[0], dq_parts[0], dq_parts[1])
            drow_ref[pl.ds(r0, t), :] += jnp.where(masks[0], dq_parts[1], dq_parts[0])
            return tuple(dk_acc), dv_acc

        zero = jnp.zeros((t, LANES), F32)
        carry = step(kj, ((zero, zero), zero), True)
        (dk0, dk1), dv = lax.fori_loop(kj + 1, nq, lambda i, cr: step(i, cr, False), carry)
        dk_ref[...] = jnp.where(masks[0], dk0, dk1).astype(BF16)
        dcol_ref[...] = jnp.where(masks[0], dk1, dk0)
        dv_ref[...] = dv.astype(BF16)

    full = lambda off: pl.BlockSpec((S, LANES), lambda h, j, off=off: (0, off + h))
    kv = lambda off: pl.BlockSpec((t, LANES), lambda h, j, off=off: (j, off + h))
    return pl.pallas_call(
        body, name=name, grid=(N_PAIRS, nq),
        out_shape=[jax.ShapeDtypeStruct((S, D), F32), jax.ShapeDtypeStruct((S, D), BF16),
                   jax.ShapeDtypeStruct((S, D), BF16), jax.ShapeDtypeStruct((S, D), F32),
                   jax.ShapeDtypeStruct((S, D), F32)],
        in_specs=[full(0), kv(N_PAIRS), kv(2 * N_PAIRS), full(0), full(0), full(0),
                  pl.BlockSpec((None, 2, t), lambda h, j: (h, 0, j))],
        out_specs=[full(0), kv(0), kv(0), kv(0), full(0)],
        compiler_params=_params("parallel", "arbitrary"),
    )(qkv, qkv, qkv, do, lse, delta, c_t2)


B_PAIRS = 4
B_NB = S // B_W


def _group_consts(g):
    nbs = jnp.where(g == 0, B_NB // B_DILS[0], jnp.where(g == 1, B_NB // B_DILS[1], B_NB // B_DILS[2]))
    dil = jnp.where(g == 0, B_DILS[0], jnp.where(g == 1, B_DILS[1], B_DILS[2]))
    return nbs, dil


def _band(dil):
    qi = lax.broadcasted_iota(jnp.int32, (B_W, B_W), 0)
    kj = lax.broadcasted_iota(jnp.int32, (B_W, B_W), 1)
    dist_c = qi - kj
    dist_p = qi + B_W - kj
    return (dist_c * dil).astype(F32), dist_c >= 0, (dist_p * dil).astype(F32), dist_p <= B_W


def _dil_fwd(qp, kp, vp, slopes, name):
    def body(sl_ref, q_ref, kp_ref, kc_ref, vp_ref, vc_ref, o_ref, lse_ref):
        g, n = pl.program_id(0), pl.program_id(1)
        nbs, dil = _group_consts(g)
        has_prev = (n % nbs) != 0
        lane, masks = _lane_masks()
        bias_c, ok_c, bias_p, ok_p = _band(dil)
        ok_p = jnp.logical_and(ok_p, has_prev)
        for hp in range(B_PAIRS):
            cols = slice(hp * LANES, (hp + 1) * LANES)
            q = q_ref[:, cols] * SCALE
            kc, kpv, vc, vpv = kc_ref[:, cols], kp_ref[:, cols], vc_ref[:, cols], vp_ref[:, cols]
            outs, lses = [], []
            for e in range(2):
                slope = sl_ref[g * 8 + 2 * hp + e]
                qe = jnp.where(masks[e], q, jnp.zeros_like(q))
                sc = jnp.where(ok_c, _dot(qe, kc, NT_DIMS) - slope * bias_c, NEG)
                sp = jnp.where(ok_p, _dot(qe, kpv, NT_DIMS) - slope * bias_p, NEG)
                m = jnp.maximum(jnp.max(sc, axis=1, keepdims=True), jnp.max(sp, axis=1, keepdims=True))
                pc = jnp.exp(sc - m).astype(BF16)
                pp = jnp.exp(sp - m).astype(BF16)
                acc = (_dot(pc, jnp.where(masks[e], vc, jnp.ones_like(vc)))
                       + _dot(pp, jnp.where(masks[e], vpv, jnp.ones_like(vpv))))
                l = acc[:, HEAD_DIM:HEAD_DIM + 1] if e == 0 else acc[:, 0:1]
                outs.append(acc / l)
                lses.append(m + jnp.log(l))
            o_ref[:, cols] = jnp.where(masks[0], outs[0], outs[1])
            lse_ref[:, cols] = jnp.where(masks[0], lses[0], lses[1])

    cur = pl.BlockSpec((None, B_W, B_OUT), lambda g, n, sl: (g, n, 0))
    prev = pl.BlockSpec((None, B_W, B_OUT), lambda g, n, sl: (g, jnp.maximum(n - 1, 0), 0))
    return pl.pallas_call(
        body, name=name,
        grid_spec=pltpu.PrefetchScalarGridSpec(
            num_scalar_prefetch=1, grid=(3, B_NB),
            in_specs=[cur, prev, cur, prev, cur], out_specs=[cur, cur]),
        out_shape=[jax.ShapeDtypeStruct((3, S, B_OUT), F32), jax.ShapeDtypeStruct((3, S, B_OUT), F32)],
        compiler_params=_params("parallel", "parallel"),
    )(slopes, qp, kp, kp, vp, vp)


def _dil_merge(og, lseg, name, tr=256):
    def body(o_ref, l_ref, out_ref, lse_ref):
        l0, l1, l2 = l_ref[0], l_ref[1], l_ref[2]
        m = jnp.maximum(jnp.maximum(l0, l1), l2)
        w0, w1, w2 = jnp.exp(l0 - m), jnp.exp(l1 - m), jnp.exp(l2 - m)
        den = w0 + w1 + w2
        out_ref[...] = ((w0 * o_ref[0] + w1 * o_ref[1] + w2 * o_ref[2]) / den).astype(BF16)
        lse_ref[...] = m + jnp.log(den)

    blk3 = pl.BlockSpec((3, tr, B_OUT), lambda i: (0, i, 0))
    blk = pl.BlockSpec((tr, B_OUT), lambda i: (i, 0))
    return pl.pallas_call(
        body, name=name, grid=(S // tr,),
        out_shape=[jax.ShapeDtypeStruct((S, B_OUT), BF16), jax.ShapeDtypeStruct((S, B_OUT), F32)],
        in_specs=[blk3, blk3], out_specs=[blk, blk], compiler_params=_params("parallel"),
    )(og, lseg)


def _dil_bwd(qp, kp, vp, dop, lsep, dlp, slopes, name):
    def body(sl_ref, qc_ref, qn_ref, kp_ref, kc_ref, vp_ref, vc_ref, doc_ref, don_ref,
             lc_ref, ln_ref, dc_ref, dn_ref, dq_ref, dk_ref, dv_ref):
        g, n = pl.program_id(0), pl.program_id(1)
        nbs, dil = _group_consts(g)
        has_prev = (n % nbs) != 0
        has_next = jnp.logical_and(n + 1 < B_NB, ((n + 1) % nbs) != 0)
        lane, masks = _lane_masks()
        bias_c, ok_c, bias_p, ok_p = _band(dil)
        ok_pp = jnp.logical_and(ok_p, has_prev)
        ok_np = jnp.logical_and(ok_p, has_next)
        for hp in range(B_PAIRS):
            cols = slice(hp * LANES, (hp + 1) * LANES)
            qc, qn = qc_ref[:, cols] * SCALE, qn_ref[:, cols] * SCALE
            kc, kpv, vc, vpv = kc_ref[:, cols], kp_ref[:, cols], vc_ref[:, cols], vp_ref[:, cols]
            doc, don = doc_ref[:, cols], don_ref[:, cols]
            dq = jnp.zeros((B_W, LANES), F32)
            dk = jnp.zeros((B_W, LANES), F32)
            dv = jnp.zeros((B_W, LANES), F32)
            for e in range(2):
                lo = hp * LANES + e * HEAD_DIM
                slope = sl_ref[g * 8 + 2 * hp + e]
                qce = jnp.where(masks[e], qc, jnp.zeros_like(qc))
                qne = jnp.where(masks[e], qn, jnp.zeros_like(qn))
                doce = jnp.where(masks[e], doc, jnp.zeros_like(doc))
                done = jnp.where(masks[e], don, jnp.zeros_like(don))
                kce = jnp.where(masks[e], kc * SCALE, jnp.zeros_like(kc))
                kpe = jnp.where(masks[e], kpv * SCALE, jnp.zeros_like(kpv))
                lse_c, dl_c = lc_ref[:, lo:lo + 1], dc_ref[:, lo:lo + 1]
                lse_n, dl_n = ln_ref[:, lo:lo + 1], dn_ref[:, lo:lo + 1]
                s = jnp.where(ok_c, _dot(qce, kc, NT_DIMS) - slope * bias_c, NEG)
                p = jnp.exp(s - lse_c)
                dsb = (p * (_dot(doce, vc, NT_DIMS) - dl_c)).astype(BF16)
                dq = dq + _dot(dsb, kce)
                dk = dk + _dot(dsb, qce, TN_DIMS)
                dv = dv + _dot(p.astype(BF16), doce, TN_DIMS)
                s = jnp.where(ok_pp, _dot(qce, kpv, NT_DIMS) - slope * bias_p, NEG)
                p = jnp.exp(s - lse_c)
                dsb = (p * (_dot(doce, vpv, NT_DIMS) - dl_c)).astype(BF16)
                dq = dq + _dot(dsb, kpe)
                s = jnp.where(ok_np, _dot(qne, kc, NT_DIMS) - slope * bias_p, NEG)
                p = jnp.exp(s - lse_n)
                dsb = (p * (_dot(done, vc, NT_DIMS) - dl_n)).astype(BF16)
                dk = dk + _dot(dsb, qne, TN_DIMS)
                dv = dv + _dot(p.astype(BF16), done, TN_DIMS)
            dq_ref[:, cols] = dq.astype(BF16)
            dk_ref[:, cols] = dk.astype(BF16)
            dv_ref[:, cols] = dv.astype(BF16)

    cur = pl.BlockSpec((None, B_W, B_OUT), lambda g, n, sl: (g, n, 0))
    prev = pl.BlockSpec((None, B_W, B_OUT), lambda g, n, sl: (g, jnp.maximum(n - 1, 0), 0))
    nxt = pl.BlockSpec((None, B_W, B_OUT), lambda g, n, sl: (g, jnp.minimum(n + 1, B_NB - 1), 0))
    return pl.pallas_call(
        body, name=name,
        grid_spec=pltpu.PrefetchScalarGridSpec(
            num_scalar_prefetch=1, grid=(3, B_NB),
            in_specs=[cur, nxt, prev, cur, prev, cur, cur, nxt, cur, nxt, cur, nxt],
            out_specs=[cur, cur, cur]),
        out_shape=[jax.ShapeDtypeStruct((3, S, B_OUT), BF16)] * 3,
        compiler_params=_params("parallel", "parallel"),
    )(slopes, qp, qp, kp, kp, vp, vp, dop, dop, lsep, lsep, dlp, dlp)


def _rows_block(shape, max_bytes=2 * 1024 * 1024):
    rows, cols = shape
    padded_cols = -(-cols // LANES) * LANES
    for tr in (1024, 512, 256, 128, 64, 32, 16):
        if rows % tr == 0 and tr * padded_cols * 4 <= max_bytes:
            return tr
    return rows


def _pair_add(grads, recv, core, name):
    _, _, R, C = grads.shape
    tr = _rows_block((R, C))

    def body(core_ref, g_ref, r_ref, o_ref):
        o_ref[...] = (g_ref[...].astype(F32) + r_ref[...].astype(F32)).astype(o_ref.dtype)

    return pl.pallas_call(
        body, name=name,
        grid_spec=pltpu.PrefetchScalarGridSpec(
            num_scalar_prefetch=1, grid=(4, R // tr),
            in_specs=[pl.BlockSpec((None, None, tr, C), lambda q, i, cr: (q, cr[0], i, 0)),
                      pl.BlockSpec((None, tr, C), lambda q, i, cr: (q, i, 0))],
            out_specs=pl.BlockSpec((None, tr, C), lambda q, i, cr: (q, i, 0))),
        out_shape=jax.ShapeDtypeStruct((4, R, C), grads.dtype),
        compiler_params=_params("parallel", "parallel"),
    )(core, grads, recv)


def _adam_update(w, m, v, g):
    m_new = ADAM_B1 * m + (1.0 - ADAM_B1) * g
    v_new = ADAM_B2 * v + (1.0 - ADAM_B2) * (g * g)
    m_hat = m_new / (1.0 - ADAM_B1 ** ADAM_STEP)
    v_hat = v_new / (1.0 - ADAM_B2 ** ADAM_STEP)
    delta = -ADAM_LR * (m_hat / (jnp.sqrt(v_hat) + ADAM_EPS) + ADAM_WD * w)
    return delta, m_new, v_new


def _adamw_sharded(w, m, v, chip_sums, recv, chip, name):
    R, C = w.shape
    tr = _rows_block((R, C), max_bytes=1024 * 1024)

    def body(chip_ref, w_ref, m_ref, v_ref, p_ref, r_ref, g_ref, d_ref, mo_ref, vo_ref):
        g = ((p_ref[...].astype(F32) + r_ref[0].astype(F32)) + r_ref[1].astype(F32)) + r_ref[2].astype(F32)
        g_ref[...] = g
        d_ref[...], mo_ref[...], vo_ref[...] = _adam_update(w_ref[...], m_ref[...], v_ref[...], g)

    blk = pl.BlockSpec((tr, C), lambda i, ch: (i, 0))
    out = jax.ShapeDtypeStruct((R, C), F32)
    return pl.pallas_call(
        body, name=name,
        grid_spec=pltpu.PrefetchScalarGridSpec(
            num_scalar_prefetch=1, grid=(R // tr,),
            in_specs=[blk, blk, blk,
                      pl.BlockSpec((None, tr, C), lambda i, ch: (ch[0], i, 0)),
                      pl.BlockSpec((3, tr, C), lambda i, ch: (0, i, 0))],
            out_specs=[blk, blk, blk, blk]),
        out_shape=[out, out, out, out],
        compiler_params=_params("parallel"),
    )(chip, w, m, v, chip_sums, recv)


def _adamw_replicated(w, m, v, parts, name):
    def body(w_ref, m_ref, v_ref, p_ref, g_ref, d_ref, mo_ref, vo_ref):
        g = p_ref[0]
        for dev in range(1, N_DEV):
            g = g + p_ref[dev]
        g_ref[...] = g
        d_ref[...], mo_ref[...], vo_ref[...] = _adam_update(w_ref[...], m_ref[...], v_ref[...], g)

    out = jax.ShapeDtypeStruct(w.shape, F32)
    return pl.pallas_call(body, name=name, out_shape=[out, out, out, out], compiler_params=_params())(w, m, v, parts)


def _cols_from_slots(g):
    return g.transpose(1, 0, 2).reshape(g.shape[1], N_DEV * g.shape[2])


def _cols_to_slots(w):
    k, n = w.shape
    return w.reshape(k, N_DEV, n // N_DEV).transpose(1, 0, 2)


def _permute(t, dil):
    c = t.shape[1]
    return t.reshape(S // dil, dil, c).transpose(1, 0, 2).reshape(S, c)


def _unpermute(t, dil):
    c = t.shape[1]
    return t.reshape(dil, S // dil, c).transpose(1, 0, 2).reshape(S, c)


def _group_permute(t):
    return jnp.stack([_permute(t[:, g * B_OUT:(g + 1) * B_OUT], B_DILS[g]) for g in range(3)])


def _same_permute(t):
    return jnp.stack([_permute(t, d) for d in B_DILS])


def _group_unpermute(t):
    return jnp.stack([_unpermute(t[g], B_DILS[g]) for g in range(3)])


SMALL_ROWS = 144


def _pack_small(a_b_f, kv_g, mix_g, ffn_g, conv_b, fin_g):
    flat = jnp.concatenate([a_b_f.reshape(-1), kv_g.reshape(-1), mix_g.reshape(-1), ffn_g.reshape(-1),
                            conv_b.reshape(-1), fin_g.reshape(-1)])
    return jnp.pad(flat, (0, SMALL_ROWS * LANES - flat.shape[0])).reshape(SMALL_ROWS, LANES)


def _unpack_small(p):
    flat = p.reshape(-1)
    out, off = [], 0
    for shape in ((1, A_HEADS), (D,), (2, D), (2, D), (2, 2 * D_FF), (D,)):
        size = math.prod(shape)
        out.append(flat[off:off + size].reshape(shape))
        off += size
    return out


def _local_step(x0, target, w_in_pad, w_out, w_q, w_bo, w_kvf, w_up, w_down, conv_w,
                a_b_f, kv_norm_g, mix_norm_g, ffn_norm_g, ffn_conv_b, final_norm_g):
    w_qkv, w_f = w_in_pad[:, :A_QKV], w_in_pad[:, A_QKV:]
    conv_b = ffn_conv_b.reshape(2, 1, 2 * D_FF)
    slopes = jnp.exp2(-8.0 * jnp.arange(1, 25, dtype=F32) / 24)

    def gain(g):
        return g.reshape(1, D)

    (h1,) = _rmsnorm_fwd(x0, [gain(mix_norm_g[0])], "norm_mix0")
    qkv = _matmul(h1, w_qkv, mode="nn", out_dtype=BF16, name="proj_qkv", tm=S, tn=512)
    z = _matmul(h1, w_f, mode="nn", out_dtype=F32, name="proj_gate", tm=S, tn=LANES)
    z_t = z[:, :A_HEADS].T
    b_f = a_b_f.reshape(A_HEADS, 1)
    c_t = _fox_prep_fwd(z_t, b_f, "fox_prep")
    c_t2 = c_t.reshape(N_PAIRS, 2, S)
    o_a, lse_a = _fox_fwd(qkv, c_t2, "fox_fwd")
    x1 = _matmul(o_a, w_out, mode="nn", out_dtype=F32, name="a_out", tm=512, tn=D, res=x0)

    def ffn_fwd(xin, layer):
        (h,) = _rmsnorm_fwd(xin, [gain(ffn_norm_g[layer])], f"norm_ffn{layer}")
        u = _matmul(h, w_up[layer], mode="nn", out_dtype=BF16, name=f"ffn_up{layer}", tm=S, tn=512)
        act = _convgate_fwd(u, conv_w[layer], conv_b[layer], f"convgate{layer}")
        xout = _matmul(act, w_down[layer], mode="nn", out_dtype=F32, name=f"ffn_down{layer}", tm=512, tn=D, res=xin)
        return h, u, act, xout

    h2, u0, act0, x2 = ffn_fwd(x1, 0)
    hk, h3 = _rmsnorm_fwd(x2, [gain(kv_norm_g), gain(mix_norm_g[1])], "norm_kv_mix1")
    kv = _matmul(hk, w_kvf, mode="nn", out_dtype=BF16, name="proj_kv", tm=S, tn=512)
    qb = _matmul(h3, w_q, mode="nn", out_dtype=BF16, name="proj_qb", tm=S, tn=512)
    qp, kp, vp = _group_permute(qb), _group_permute(kv[:, :B_Q]), _group_permute(kv[:, B_Q:])
    og_p, lseg_p = _dil_fwd(qp, kp, vp, slopes, "dil_fwd")
    o_b, lse_b = _dil_merge(_group_unpermute(og_p), _group_unpermute(lseg_p), "dil_merge")
    x3 = _matmul(o_b, w_bo, mode="nn", out_dtype=F32, name="b_out", tm=512, tn=D, res=x2)
    h4, u1, act1, x4 = ffn_fwd(x3, 1)
    loss_blk, dx4, dx4b, dg_final = _final_loss(x4, target, gain(final_norm_g), "final_loss")

    def ffn_bwd(dx, dxb, xin, h, u, act, layer):
        dact = _matmul(dxb, w_down[layer], mode="nt", out_dtype=BF16, name=f"d_act{layer}", tm=S, tn=256)
        dw_down = _matmul_tn(act, dxb, out_dtype=BF16, name=f"dw_down{layer}")
        da, dg, dwa, dwg, dba, dbg = _convgate_bwd(u, conv_w[layer], conv_b[layer], dact, f"convgate_bwd{layer}")
        du_a = _conv_input_bwd(da, conv_w[layer][:, :D_FF], f"conv_in_bwd_a{layer}")
        du_g = _conv_input_bwd(dg, conv_w[layer][:, D_FF:], f"conv_in_bwd_g{layer}")
        dw_up = jnp.concatenate(
            [_matmul_tn(h, du_a, out_dtype=BF16, name=f"dw_up_a{layer}"),
             _matmul_tn(h, du_g, out_dtype=BF16, name=f"dw_up_g{layer}")], axis=1)
        dh = _matmul(du_a, w_up[layer][:, :D_FF], mode="nt", out_dtype=F32, name=f"dh_ffn_a{layer}", tm=512, tn=D)
        dh = _matmul(du_g, w_up[layer][:, D_FF:], mode="nt", out_dtype=F32, name=f"dh_ffn_g{layer}", tm=512, tn=D,
                     res=dh)
        dxin, dxinb, dgain = _rmsnorm_bwd(xin, dh, gain(ffn_norm_g[layer]), dx, f"norm_ffn_bwd{layer}")
        dconv_w = jnp.concatenate([dwa, dwg], axis=1)
        dconv_b = jnp.concatenate([dba, dbg], axis=1)
        return dxin, dxinb, dgain, dw_up, dw_down, dconv_w, dconv_b

    dx3, dx3b, dg_ffn1, dw_up1, dw_down1, dconv_w1, dconv_b1 = ffn_bwd(dx4, dx4b, x3, h4, u1, act1, 1)

    do_b = _matmul(dx3b, w_bo, mode="nt", out_dtype=BF16, name="d_ob", tm=1024, tn=B_OUT)
    dw_bo = _matmul_tn(o_b, dx3b, out_dtype=BF16, name="dw_bo")
    dl_b = _head_rowsum(do_b, o_b, "delta_b")
    dqp, dkp, dvp = _dil_bwd(qp, kp, vp, _same_permute(do_b), _same_permute(lse_b), _same_permute(dl_b),
                             slopes, "dil_bwd")

    def natural(tp):
        return jnp.concatenate([_unpermute(tp[g], B_DILS[g]) for g in range(3)], axis=1)

    dqb = natural(dqp)
    dkv = jnp.concatenate([natural(dkp), natural(dvp)], axis=1)
    dw_q = _matmul_tn(h3, dqb, out_dtype=BF16, name="dw_q")
    dw_kv = _matmul_tn(hk, dkv, out_dtype=BF16, name="dw_kv")
    dh3 = _matmul(dqb, w_q, mode="nt", out_dtype=F32, name="dh_mix1", tm=512, tn=D)
    dhk = _matmul(dkv, w_kvf, mode="nt", out_dtype=F32, name="dh_kv", tm=512, tn=D)
    dx2, _, dg_mix1 = _rmsnorm_bwd(x2, dh3, gain(mix_norm_g[1]), dx3, "norm_mix1_bwd")
    dx2, dx2b, dg_kv = _rmsnorm_bwd(x2, dhk, gain(kv_norm_g), dx2, "norm_kv_bwd")

    dx1, dx1b, dg_ffn0, dw_up0, dw_down0, dconv_w0, dconv_b0 = ffn_bwd(dx2, dx2b, x1, h2, u0, act0, 0)

    do_a = _matmul(dx1b, w_out, mode="nt", out_dtype=BF16, name="d_oa", tm=512, tn=D)
    dw_out = _matmul_tn(o_a, dx1b, out_dtype=BF16, name="dw_out")
    dl_a = _head_rowsum(do_a, o_a, "delta_a")
    dq_a, dk_a, dv_a, dcol, drow = _fox_bwd(qkv, do_a, lse_a, dl_a, c_t2, "fox_bwd")

    def head_sums(t):
        return t.reshape(S, N_PAIRS, 2, HEAD_DIM)[:, :, ::-1, 0].reshape(S, A_HEADS).T

    dz_t, db_f = _fox_prep_bwd(head_sums(drow), head_sums(dcol), z_t, b_f, "fox_prep_bwd")
    dz = jnp.pad(dz_t.T, ((0, 0), (0, LANES - A_HEADS))).astype(BF16)
    dproj = jnp.concatenate([dq_a.astype(BF16), dk_a, dv_a, dz], axis=1)
    dw_in = _matmul_tn(h1, dproj, out_dtype=BF16, name="dw_in")
    dh1 = _matmul(dproj, w_in_pad, mode="nt", out_dtype=F32, name="dh_mix0", tm=512, tn=D)
    grad_x, _, dg_mix0 = _rmsnorm_bwd(x0, dh1, gain(mix_norm_g[0]), dx1, "norm_mix0_bwd")

    dw_up = jnp.stack([dw_up0, dw_up1])
    dw_down = jnp.stack([dw_down0, dw_down1])
    dconv_w = jnp.stack([dconv_w0, dconv_w1])
    dg_mix = jnp.concatenate([dg_mix0, dg_mix1], axis=0)
    dg_ffn = jnp.concatenate([dg_ffn0, dg_ffn1], axis=0)
    dconv_b = jnp.concatenate([dconv_b0, dconv_b1], axis=0)
    big = (dw_in[:, :A_QKV + A_HEADS], dw_out, dw_q, dw_bo, dw_kv, dw_up, dw_down, dconv_w)
    small = (db_f, dg_kv, dg_mix, dg_ffn, dconv_b, dg_final)
    return loss_blk, grad_x, big, small


def kernel(x, a_w_in, a_b_f, a_w_out, b_w_q, b_w_out, kv_norm_g, w_kv, mix_norm_g, ffn_norm_g, ffn_w_up, ffn_conv_w, ffn_conv_b, ffn_w_down, final_norm_g, loss_target, m_a_w_in, m_a_b_f, m_a_w_out, m_b_w_q, m_b_w_out, m_kv_norm_g, m_w_kv, m_mix_norm_g, m_ffn_norm_g, m_ffn_w_up, m_ffn_conv_w, m_ffn_conv_b, m_ffn_w_down, m_final_norm_g, v_a_w_in, v_a_b_f, v_a_w_out, v_b_w_q, v_b_w_out, v_kv_norm_g, v_w_kv, v_mix_norm_g, v_ffn_norm_g, v_ffn_w_up, v_ffn_conv_w, v_ffn_conv_b, v_ffn_w_down, v_final_norm_g):
    xi, yi, ci = lax.axis_index("x"), lax.axis_index("y"), lax.axis_index("c")
    core = jnp.reshape(ci, (1,)).astype(jnp.int32)
    chip = jnp.reshape(2 * xi + yi, (1,)).astype(jnp.int32)

    def shards(a_w_in, a_w_out, b_w_q, b_w_out, w_kv, ffn_w_up, ffn_w_down, ffn_conv_w):
        return [a_w_in[0], a_w_out[0], b_w_q[0], b_w_out[0], w_kv, ffn_w_up.reshape(2 * D, -1),
                ffn_w_down.reshape(-1, D), ffn_conv_w.reshape(6, -1)]

    w_loc = shards(a_w_in, a_w_out, b_w_q, b_w_out, w_kv, ffn_w_up, ffn_w_down, ffn_conv_w)
    m_loc = shards(m_a_w_in, m_a_w_out, m_b_w_q, m_b_w_out, m_w_kv, m_ffn_w_up, m_ffn_w_down, m_ffn_conv_w)
    v_loc = shards(v_a_w_in, v_a_w_out, v_b_w_q, v_b_w_out, v_w_kv, v_ffn_w_up, v_ffn_w_down, v_ffn_conv_w)

    gathered = _all_gather([w.astype(BF16) for w in w_loc[:7]] + [w_loc[7]], "gather_weights")
    w_in = _cols_from_slots(gathered[0])
    w_in_pad = jnp.pad(w_in, ((0, 0), (0, A_PROJ_PAD - w_in.shape[1])))
    w_out = gathered[1].reshape(D, D)
    w_q = _cols_from_slots(gathered[2])
    w_bo = _cols_from_slots(gathered[3])
    w_kvf = _cols_from_slots(gathered[4])
    w_up = gathered[5].reshape(N_DEV, 2, D, -1).transpose(1, 2, 0, 3).reshape(2, D, 2 * D_FF)
    w_down = gathered[6].reshape(N_DEV, 2, -1, D).transpose(1, 0, 2, 3).reshape(2, D_FF, D)
    conv_w = gathered[7].reshape(N_DEV, 2, 3, -1).transpose(1, 2, 0, 3).reshape(2, 3, 2 * D_FF)

    loss_blk, grad_x, big_grads, small_grads = _local_step(
        x[0], loss_target[0], w_in_pad, w_out, w_q, w_bo, w_kvf, w_up, w_down, conv_w,
        a_b_f, kv_norm_g, mix_norm_g, ffn_norm_g, ffn_conv_b, final_norm_g)
    dw_in, dw_out, dw_q, dw_bo, dw_kv, dw_up, dw_down, dconv_w = big_grads

    slots = [
        _cols_to_slots(dw_in),
        dw_out.reshape(N_DEV, D // N_DEV, D),
        _cols_to_slots(dw_q),
        _cols_to_slots(dw_bo),
        _cols_to_slots(dw_kv),
        dw_up.reshape(2, D, N_DEV, -1).transpose(2, 0, 1, 3).reshape(N_DEV, 2 * D, -1),
        dw_down.reshape(2, N_DEV, -1, D).transpose(1, 0, 2, 3).reshape(N_DEV, -1, D),
        dconv_w.reshape(2, 3, N_DEV, -1).transpose(2, 0, 1, 3).reshape(N_DEV, 6, -1),
    ]
    by_chip = [s.reshape((4, 2) + s.shape[1:]) for s in slots]
    from_sibling = _sibling_exchange(by_chip, "grad_sibling_exchange")
    chip_sums = [_pair_add(g, r, core, f"grad_pair_add{k}") for k, (g, r) in enumerate(zip(by_chip, from_sibling))]
    from_chips = _chip_exchange(chip_sums, "grad_chip_exchange")

    big = [_adamw_sharded(w_loc[k], m_loc[k], v_loc[k], chip_sums[k], from_chips[k], chip, f"adamw{k}")
           for k in range(8)]

    small_part = _pack_small(*small_grads)
    (small_parts,) = _all_gather([small_part], "gather_small_grads")
    small = _adamw_replicated(
        _pack_small(a_b_f, kv_norm_g, mix_norm_g, ffn_norm_g, ffn_conv_b, final_norm_g),
        _pack_small(m_a_b_f, m_kv_norm_g, m_mix_norm_g, m_ffn_norm_g, m_ffn_conv_b, m_final_norm_g),
        _pack_small(v_a_b_f, v_kv_norm_g, v_mix_norm_g, v_ffn_norm_g, v_ffn_conv_b, v_final_norm_g),
        small_parts, "adamw_small")

    loss = lax.psum(loss_blk[0, 0], ("x", "y", "c"))

    def assemble(kind):
        b = [r[kind] for r in big]
        s_abf, s_kv, s_mix, s_ffn, s_cb, s_fin = _unpack_small(small[kind])
        return [b[0][None], s_abf, b[1][None], b[2][None], b[3][None], s_kv, b[4], s_mix, s_ffn,
                b[5].reshape(2, D, -1), b[7].reshape(2, 3, -1), s_cb, b[6].reshape(2, -1, D), s_fin]

    return (loss, grad_x[None], *assemble(0), *assemble(1), *assemble(2), *assemble(3))
```

```python
import functools
import math

import jax
import jax.numpy as jnp
from jax import lax
from jax.experimental import pallas as pl
from jax.experimental.pallas import tpu as pltpu

F32 = jnp.float32
BF16 = jnp.bfloat16

S = 4096
D = 1024
N_DEV = 8
A_HEADS = 16
HEAD_DIM = 64
A_QKV = 3072
A_PROJ_PAD = 3200
B_Q = 1536
B_OUT = 512
B_KV = 3072
B_W = 128
B_DILS = (1, 4, 16)
D_FF = 2816
RMS_EPS = 1e-6
SCALE = HEAD_DIM ** -0.5
NEG = -1e30

ADAM_LR = 0.001
ADAM_B1 = 0.9
ADAM_B2 = 0.999
ADAM_EPS = 1e-08
ADAM_WD = 0.01
ADAM_STEP = 10

LANES = 128
VMEM_LIMIT = 56 * 1024 * 1024
MESH = pl.DeviceIdType.MESH
ANY = pl.BlockSpec(memory_space=pl.ANY)

NT_DIMS = (((1,), (1,)), ((), ()))
TN_DIMS = (((0,), (0,)), ((), ()))
NN_DIMS = (((1,), (0,)), ((), ()))


def _params(*sem):
    return pltpu.CompilerParams(dimension_semantics=sem if sem else None, vmem_limit_bytes=VMEM_LIMIT)


def _dot(a, b, dims=NN_DIMS):
    return lax.dot_general(a, b, dims, preferred_element_type=F32)


def _split_dot(x, mat, pieces):
    out = None
    rem = x
    for _ in range(pieces):
        part = rem.astype(BF16)
        rem = rem - part.astype(F32)
        d = _dot(part, mat)
        out = d if out is None else out + d
    return out


def _pick(n, prefs):
    for p in prefs:
        if n % p == 0:
            return p
    return n


def _all_gather(arrays, name):
    n = len(arrays)

    def body(*refs):
        ins = refs[:n]
        outs = refs[n:2 * n]
        send_sems, recv_sems, local_sems = refs[2 * n:]
        x, y, c = lax.axis_index("x"), lax.axis_index("y"), lax.axis_index("c")
        me, sibling = (x, y, c), (x, y, 1 - c)
        chips = [(1 - x, y), (x, 1 - y), (1 - x, 1 - y)]

        def slot(a, px, py, pc):
            return outs[a].at[4 * px + 2 * py + pc]

        def copy(a, k, block, to, src=None):
            return pltpu.make_async_remote_copy(
                src_ref=slot(a, *block) if src is None else src, dst_ref=slot(a, *block),
                send_sem=send_sems.at[a, k], recv_sem=recv_sems.at[a, k],
                device_id=to, device_id_type=MESH)

        mine = [pltpu.make_async_copy(ins[a], slot(a, *me), local_sems.at[a]) for a in range(n)]
        for cp in mine:
            cp.start()
        first = []
        for a in range(n):
            first.append(copy(a, 0, me, sibling, src=ins[a]))
            first += [copy(a, 1 + j, me, (*chip, c), src=ins[a]) for j, chip in enumerate(chips)]
        for cp in first:
            cp.start()
        passed = []
        for j, chip in enumerate(chips):
            for a in range(n):
                copy(a, 1 + j, (*chip, c), me).wait_recv()
                fwd = copy(a, 4 + j, (*chip, c), sibling)
                fwd.start()
                passed.append(fwd)
        for a in range(n):
            copy(a, 0, sibling, me).wait_recv()
            for j, chip in enumerate(chips):
                copy(a, 4 + j, (*chip, 1 - c), me).wait_recv()
        for cp in first + passed:
            cp.wait_send()
        for cp in mine:
            cp.wait()

    return pl.pallas_call(
        body, name=name,
        out_shape=[jax.ShapeDtypeStruct((N_DEV,) + a.shape, a.dtype) for a in arrays],
        in_specs=[ANY] * n, out_specs=[ANY] * n,
        scratch_shapes=[pltpu.SemaphoreType.DMA((n, 7)), pltpu.SemaphoreType.DMA((n, 7)),
                        pltpu.SemaphoreType.DMA((n,))],
    )(*arrays)


PEER_FLIPS = [(dx, dy, dc) for dx in (0, 1) for dy in (0, 1) for dc in (0, 1) if (dx, dy, dc) != (0, 0, 0)]


def _exchange_copies(ins, outs, sems, scatter):
    if not ins:
        return []
    send_sems, recv_sems, local_sems = sems
    x, y, c = lax.axis_index("x"), lax.axis_index("y"), lax.axis_index("c")
    me = 4 * x + 2 * y + c
    copies = []
    for a in range(len(ins)):
        copies.append(pltpu.make_async_copy(ins[a].at[me] if scatter else ins[a], outs[a].at[me], local_sems.at[a]))
        for k, (dx, dy, dc) in enumerate(PEER_FLIPS):
            px, py, pc = (1 - x if dx else x), (1 - y if dy else y), (1 - c if dc else c)
            copies.append(pltpu.make_async_remote_copy(
                src_ref=ins[a].at[4 * px + 2 * py + pc] if scatter else ins[a], dst_ref=outs[a].at[me],
                send_sem=send_sems.at[a, k], recv_sem=recv_sems.at[a, k],
                device_id=(px, py, pc), device_id_type=MESH))
    return copies


def _exchange_scratch(n):
    if n == 0:
        return []
    return [pltpu.SemaphoreType.DMA((n, 7)), pltpu.SemaphoreType.DMA((n, 7)), pltpu.SemaphoreType.DMA((n,))]


def _exchange_shapes(arrays, scatter):
    return [jax.ShapeDtypeStruct((N_DEV,) + (a.shape[1:] if scatter else a.shape), a.dtype) for a in arrays]


def _scatter_now(arrays, name):
    n = len(arrays)

    def body(*refs):
        copies = _exchange_copies(refs[:n], refs[n:2 * n], refs[2 * n:], True)
        for cp in copies:
            cp.start()
        for cp in copies:
            cp.wait()

    return pl.pallas_call(
        body, name=name, out_shape=_exchange_shapes(arrays, True),
        in_specs=[ANY] * n, out_specs=[ANY] * n, scratch_shapes=_exchange_scratch(n),
    )(*arrays)


MM_ROWS = 512
MM_COLS = 1024


def _matmul(a, b, *, mode, out_dtype, name, tm, tn, res=None):
    if mode == "nn":
        (M, K), (K2, N) = a.shape, b.shape
    else:
        (M, K), (N, K2) = a.shape, b.shape
    assert K == K2, (a.shape, b.shape, mode)
    tm, tn = min(tm, M), min(tn, N)
    sm = min(tm, MM_ROWS)
    sn = tn if tn <= MM_COLS else _pick(tn, (512, 256, 128))
    assert M % tm == 0 and N % tn == 0 and tm % sm == 0, (M, N, K, tm, tn)
    dims = NN_DIMS if mode == "nn" else NT_DIMS
    a_spec = pl.BlockSpec((tm, K), lambda i, j: (i, 0))
    if mode == "nt":
        b_spec = pl.BlockSpec((tn, K), lambda i, j: (j, 0))
    else:
        b_spec = pl.BlockSpec((K, tn), lambda i, j: (0, j))
    o_spec = pl.BlockSpec((tm, tn), lambda i, j: (i, j))
    has_res = res is not None

    def body(*refs):
        a_ref, b_ref = refs[0], refs[1]
        r_ref = refs[2] if has_res else None
        o_ref = refs[2 + has_res]

        def chunk(r, carry):
            rows = pl.ds(pl.multiple_of(r * sm, sm), sm)
            av = a_ref[rows, :]
            for c0 in range(0, tn, sn):
                bv = b_ref[c0:c0 + sn, :] if mode == "nt" else b_ref[:, c0:c0 + sn]
                total = _dot(av, bv, dims)
                if has_res:
                    total = total + r_ref[rows, c0:c0 + sn]
                o_ref[rows, c0:c0 + sn] = total.astype(out_dtype)
            return carry

        lax.fori_loop(0, tm // sm, chunk, 0)

    return pl.pallas_call(
        body, name=name, grid=(M // tm, N // tn),
        out_shape=jax.ShapeDtypeStruct((M, N), out_dtype),
        in_specs=[a_spec, b_spec] + ([o_spec] if has_res else []),
        out_specs=o_spec,
        compiler_params=_params("parallel", "parallel"),
    )(*((a, b, res) if has_res else (a, b)))


def _matmul_tn(a, b, *, out_dtype, name, tk=512, sm=256):
    (K, M), (K2, N) = a.shape, b.shape
    assert K == K2 and K % tk == 0 and M % sm == 0, (a.shape, b.shape)
    nk = K // tk

    def body(a_ref, b_ref, o_ref, acc_ref):
        k = pl.program_id(0)

        @pl.when(k == 0)
        def _():
            acc_ref[...] = jnp.zeros_like(acc_ref)

        def chunk(mi, carry):
            cols = pl.ds(pl.multiple_of(mi * sm, sm), sm)
            acc_ref[cols, :] += _dot(a_ref[:, cols].T, b_ref[...])
            return carry

        lax.fori_loop(0, M // sm, chunk, 0)

        @pl.when(k == nk - 1)
        def _():
            def emit(mi, carry):
                rows = pl.ds(pl.multiple_of(mi * sm, sm), sm)
                o_ref[rows, :] = acc_ref[rows, :].astype(out_dtype)
                return carry
            lax.fori_loop(0, M // sm, emit, 0)

    return pl.pallas_call(
        body, name=name, grid=(nk,),
        out_shape=jax.ShapeDtypeStruct((M, N), out_dtype),
        in_specs=[pl.BlockSpec((tk, M), lambda k: (k, 0)), pl.BlockSpec((tk, N), lambda k: (k, 0))],
        out_specs=pl.BlockSpec((M, N), lambda k: (0, 0)),
        scratch_shapes=[pltpu.VMEM((M, N), F32)],
        compiler_params=_params("arbitrary"),
    )(a, b)


def _rmsnorm_fwd(x, gains, name, tr=256):
    n = len(gains)

    def body(*refs):
        x_ref = refs[0]
        xv = x_ref[...]
        r = lax.rsqrt(jnp.mean(xv * xv, axis=-1, keepdims=True) + RMS_EPS)
        y = xv * r
        for a in range(n):
            refs[1 + n + a][...] = (y * refs[1 + a][...]).astype(BF16)

    row = pl.BlockSpec((tr, D), lambda i: (i, 0))
    gain = pl.BlockSpec((1, D), lambda i: (0, 0))
    return pl.pallas_call(
        body, name=name, grid=(S // tr,),
        out_shape=[jax.ShapeDtypeStruct((S, D), BF16)] * n,
        in_specs=[row] + [gain] * n, out_specs=[row] * n,
        compiler_params=_params("parallel"),
    )(x, *gains)


def _rmsnorm_bwd(x, dy, g, dres, name, tr=256):
    def body(x_ref, dy_ref, g_ref, dres_ref, dx_ref, dxb_ref, dg_ref):
        xv = x_ref[...]
        dyv = dy_ref[...]
        r = lax.rsqrt(jnp.mean(xv * xv, axis=-1, keepdims=True) + RMS_EPS)
        xhat = xv * r
        dxhat = dyv * g_ref[...]
        mean_term = jnp.mean(dxhat * xhat, axis=-1, keepdims=True)
        dx = r * (dxhat - xhat * mean_term) + dres_ref[...]
        dx_ref[...] = dx
        dxb_ref[...] = dx.astype(BF16)
        part = jnp.sum(dyv * xhat, axis=0, keepdims=True)

        @pl.when(pl.program_id(0) == 0)
        def _():
            dg_ref[...] = part

        @pl.when(pl.program_id(0) > 0)
        def _():
            dg_ref[...] += part

    row = pl.BlockSpec((tr, D), lambda i: (i, 0))
    gain = pl.BlockSpec((1, D), lambda i: (0, 0))
    return pl.pallas_call(
        body, name=name, grid=(S // tr,),
        out_shape=[jax.ShapeDtypeStruct((S, D), F32), jax.ShapeDtypeStruct((S, D), BF16),
                   jax.ShapeDtypeStruct((1, D), F32)],
        in_specs=[row, row, gain, row], out_specs=[row, row, gain],
        compiler_params=_params("arbitrary"),
    )(x, dy, g, dres)


def _final_loss(x, target, g, name, tr=256):
    def body(x_ref, t_ref, g_ref, loss_ref, dx_ref, dxb_ref, dg_ref):
        xv = x_ref[...]
        gv = g_ref[...]
        r = lax.rsqrt(jnp.mean(xv * xv, axis=-1, keepdims=True) + RMS_EPS)
        xhat = xv * r
        err = xhat * gv - t_ref[...]
        row_loss = jnp.mean(err * err, axis=-1, keepdims=True)
        lpart = 0.5 * jnp.sum(row_loss, axis=0, keepdims=True)
        dyv = err / D
        dxhat = dyv * gv
        mean_term = jnp.mean(dxhat * xhat, axis=-1, keepdims=True)
        dx = r * (dxhat - xhat * mean_term)
        dx_ref[...] = dx
        dxb_ref[...] = dx.astype(BF16)
        gpart = jnp.sum(dyv * xhat, axis=0, keepdims=True)

        @pl.when(pl.program_id(0) == 0)
        def _():
            dg_ref[...] = gpart
            loss_ref[...] = jnp.broadcast_to(lpart, loss_ref.shape)

        @pl.when(pl.program_id(0) > 0)
        def _():
            dg_ref[...] += gpart
            loss_ref[...] += jnp.broadcast_to(lpart, loss_ref.shape)

    row = pl.BlockSpec((tr, D), lambda i: (i, 0))
    gain = pl.BlockSpec((1, D), lambda i: (0, 0))
    lspec = pl.BlockSpec((8, LANES), lambda i: (0, 0))
    return pl.pallas_call(
        body, name=name, grid=(S // tr,),
        out_shape=[jax.ShapeDtypeStruct((8, LANES), F32), jax.ShapeDtypeStruct((S, D), F32),
                   jax.ShapeDtypeStruct((S, D), BF16), jax.ShapeDtypeStruct((1, D), F32)],
        in_specs=[row, row, gain], out_specs=[lspec, row, row, gain],
        compiler_params=_params("arbitrary"),
    )(x, target, g)


CONV_TR = 128
CONV_TC = D_FF
CONV_NJ = D_FF // CONV_TC
HALO = 16


def _causal_taps(cur_ref, prev_ref, first):
    xv = cur_ref[...].astype(F32)
    pv = prev_ref[...].astype(F32)
    p1 = jnp.where(first, 0.0, pv[HALO - 1:HALO, :])
    p2 = jnp.where(first, 0.0, pv[HALO - 2:HALO - 1, :])
    r1, r2 = pltpu.roll(xv, 1, 0), pltpu.roll(xv, 2, 0)
    row = lax.broadcasted_iota(jnp.int32, (8, xv.shape[1]), 0)
    xm1 = jnp.concatenate([jnp.where(row == 0, p1, r1[0:8]), r1[8:]], axis=0)
    xm2 = jnp.concatenate([jnp.where(row == 0, p2, jnp.where(row == 1, p1, r2[0:8])), r2[8:]], axis=0)
    return xv, xm1, xm2


def _conv_specs():
    def prev_row(i):
        return jnp.maximum(i * (CONV_TR // HALO) - 1, 0)
    ua = pl.BlockSpec((CONV_TR, CONV_TC), lambda i, j: (i, j))
    ug = pl.BlockSpec((CONV_TR, CONV_TC), lambda i, j: (i, j + CONV_NJ))
    pa = pl.BlockSpec((HALO, CONV_TC), lambda i, j: (prev_row(i), j))
    pg = pl.BlockSpec((HALO, CONV_TC), lambda i, j: (prev_row(i), j + CONV_NJ))
    wa = pl.BlockSpec((3, CONV_TC), lambda i, j: (0, j))
    wg = pl.BlockSpec((3, CONV_TC), lambda i, j: (0, j + CONV_NJ))
    ba = pl.BlockSpec((1, CONV_TC), lambda i, j: (0, j))
    bg = pl.BlockSpec((1, CONV_TC), lambda i, j: (0, j + CONV_NJ))
    return [ua, pa, ug, pg, wa, wg, ba, bg]


def _convgate_fwd(u, w, b, name):
    def body(ua, pa, ug, pg, wa, wg, ba, bg, o_ref):
        first = pl.program_id(0) == 0
        x0, x1, x2 = _causal_taps(ua, pa, first)
        ac = wa[0:1, :] * x2 + wa[1:2, :] * x1 + wa[2:3, :] * x0 + ba[...]
        x0, x1, x2 = _causal_taps(ug, pg, first)
        gc = wg[0:1, :] * x2 + wg[1:2, :] * x1 + wg[2:3, :] * x0 + bg[...]
        sg = 1.0 / (1.0 + jnp.exp(-gc))
        o_ref[...] = (gc * sg * ac).astype(BF16)

    return pl.pallas_call(
        body, name=name, grid=(S // CONV_TR, CONV_NJ),
        out_shape=jax.ShapeDtypeStruct((S, D_FF), BF16),
        in_specs=_conv_specs(),
        out_specs=pl.BlockSpec((CONV_TR, CONV_TC), lambda i, j: (i, j)),
        compiler_params=_params("parallel", "parallel"),
    )(u, u, u, u, w, w, b, b)


def _convgate_bwd(u, w, b, dact, name):
    def body(ua, pa, ug, pg, wa, wg, ba, bg, d_ref, da_ref, dg_ref, dwa_ref, dwg_ref, dba_ref, dbg_ref):
        i = pl.program_id(1)
        first = i == 0
        a0, a1, a2 = _causal_taps(ua, pa, first)
        ac = wa[0:1, :] * a2 + wa[1:2, :] * a1 + wa[2:3, :] * a0 + ba[...]
        g0, g1, g2 = _causal_taps(ug, pg, first)
        gc = wg[0:1, :] * g2 + wg[1:2, :] * g1 + wg[2:3, :] * g0 + bg[...]
        sg = 1.0 / (1.0 + jnp.exp(-gc))
        dact_v = d_ref[...].astype(F32)
        da = dact_v * (gc * sg)
        dg = dact_v * ac * (sg * (1.0 + gc * (1.0 - sg)))
        da_ref[...] = da.astype(BF16)
        dg_ref[...] = dg.astype(BF16)

        def col(v):
            return jnp.sum(v, axis=0, keepdims=True)

        parts = [col(da * a2), col(da * a1), col(da * a0), col(dg * g2), col(dg * g1), col(dg * g0),
                 col(da), col(dg)]

        @pl.when(first)
        def _():
            for k in range(3):
                dwa_ref[k:k + 1, :] = parts[k]
                dwg_ref[k:k + 1, :] = parts[3 + k]
            dba_ref[...] = parts[6]
            dbg_ref[...] = parts[7]

        @pl.when(i > 0)
        def _():
            for k in range(3):
                dwa_ref[k:k + 1, :] += parts[k]
                dwg_ref[k:k + 1, :] += parts[3 + k]
            dba_ref[...] += parts[6]
            dbg_ref[...] += parts[7]

    def swap(spec):
        return pl.BlockSpec(spec.block_shape, lambda j, i, f=spec.index_map: f(i, j))

    blk = pl.BlockSpec((CONV_TR, CONV_TC), lambda j, i: (i, j))
    w3 = pl.BlockSpec((3, CONV_TC), lambda j, i: (0, j))
    b1 = pl.BlockSpec((1, CONV_TC), lambda j, i: (0, j))
    return pl.pallas_call(
        body, name=name, grid=(CONV_NJ, S // CONV_TR),
        out_shape=[jax.ShapeDtypeStruct((S, D_FF), BF16), jax.ShapeDtypeStruct((S, D_FF), BF16),
                   jax.ShapeDtypeStruct((3, D_FF), F32), jax.ShapeDtypeStruct((3, D_FF), F32),
                   jax.ShapeDtypeStruct((1, D_FF), F32), jax.ShapeDtypeStruct((1, D_FF), F32)],
        in_specs=[swap(s) for s in _conv_specs()] + [blk],
        out_specs=[blk, blk, w3, w3, b1, b1],
        compiler_params=_params("parallel", "arbitrary"),
    )(u, u, u, u, w, w, b, b, dact)


def _conv_input_bwd(d, w, name):
    n_i = S // CONV_TR

    def body(d_ref, n_ref, w_ref, o_ref):
        last = pl.program_id(0) == n_i - 1
        dv = d_ref[...].astype(F32)
        nv = n_ref[...].astype(F32)
        n1 = jnp.where(last, 0.0, nv[0:1, :])
        n2 = jnp.where(last, 0.0, nv[1:2, :])
        r1, r2 = pltpu.roll(dv, CONV_TR - 1, 0), pltpu.roll(dv, CONV_TR - 2, 0)
        row = lax.broadcasted_iota(jnp.int32, (8, dv.shape[1]), 0)
        cut = CONV_TR - 8
        dp1 = jnp.concatenate([r1[:cut], jnp.where(row == 7, n1, r1[cut:])], axis=0)
        dp2 = jnp.concatenate([r2[:cut], jnp.where(row == 7, n2, jnp.where(row == 6, n1, r2[cut:]))], axis=0)
        o_ref[...] = (w_ref[2:3, :] * dv + w_ref[1:2, :] * dp1 + w_ref[0:1, :] * dp2).astype(BF16)

    def next_row(i):
        return jnp.minimum((i + 1) * (CONV_TR // HALO), S // HALO - 1)

    blk = pl.BlockSpec((CONV_TR, CONV_TC), lambda i, j: (i, j))
    return pl.pallas_call(
        body, name=name, grid=(n_i, CONV_NJ),
        out_shape=jax.ShapeDtypeStruct((S, D_FF), BF16),
        in_specs=[blk, pl.BlockSpec((HALO, CONV_TC), lambda i, j: (next_row(i), j)),
                  pl.BlockSpec((3, CONV_TC), lambda i, j: (0, j))],
        out_specs=blk,
        compiler_params=_params("parallel", "parallel"),
    )(d, d, w)


FOX_T = 256
FOX_TQ, FOX_TK = 256, 256
N_PAIRS = A_HEADS // 2


def _lane_masks():
    lane = lax.broadcasted_iota(jnp.int32, (1, LANES), 1)
    return lane, (lane < HEAD_DIM, lane >= HEAD_DIM)


def _fox_prep_fwd(z_t, b, name):
    def body(z_ref, b_ref, c_ref):
        r = lax.broadcasted_iota(jnp.int32, (LANES, LANES), 0)
        cc = lax.broadcasted_iota(jnp.int32, (LANES, LANES), 1)
        upper = (r <= cc).astype(BF16)
        carry = jnp.zeros((A_HEADS, 1), F32)
        for blk in range(S // LANES):
            sl = slice(blk * LANES, (blk + 1) * LANES)
            z = z_ref[:, sl] + b_ref[...]
            lf = jnp.minimum(z, 0.0) - jnp.log(1.0 + jnp.exp(-jnp.abs(z)))
            cs = _split_dot(lf, upper, 3) + carry
            c_ref[:, sl] = cs
            carry = cs[:, LANES - 1:LANES]

    return pl.pallas_call(
        body, name=name, out_shape=jax.ShapeDtypeStruct((A_HEADS, S), F32),
        compiler_params=_params(),
    )(z_t, b)


def _fox_prep_bwd(drow_t, dcol_t, z_t, b, name):
    def body(dr_ref, dc_ref, z_ref, b_ref, dz_ref, db_ref):
        r = lax.broadcasted_iota(jnp.int32, (LANES, LANES), 0)
        cc = lax.broadcasted_iota(jnp.int32, (LANES, LANES), 1)
        lower = (r >= cc).astype(BF16)
        carry = jnp.zeros((A_HEADS, 1), F32)
        db = jnp.zeros((A_HEADS, 1), F32)
        for blk in reversed(range(S // LANES)):
            sl = slice(blk * LANES, (blk + 1) * LANES)
            rc = _split_dot(dr_ref[:, sl] - dc_ref[:, sl], lower, 3) + carry
            carry = rc[:, 0:1]
            z = z_ref[:, sl] + b_ref[...]
            dz = rc / (1.0 + jnp.exp(z))
            dz_ref[:, sl] = dz
            db = db + jnp.sum(dz, axis=1, keepdims=True)
        db_ref[...] = db

    return pl.pallas_call(
        body, name=name,
        out_shape=[jax.ShapeDtypeStruct((A_HEADS, S), F32), jax.ShapeDtypeStruct((A_HEADS, 1), F32)],
        compiler_params=_params(),
    )(drow_t, dcol_t, z_t, b)


def _fox_fwd(qkv, c_t2, name, gather):
    tq, tk = FOX_TQ, FOX_TK

    n = len(gather)

    def body(*refs):
        q_ref, k_ref, v_ref, ct_ref = refs[:4]
        o_ref, lse_ref = refs[4 + n:6 + n]
        exchange = (refs[4:4 + n], refs[6 + n:6 + 2 * n], refs[6 + 2 * n:], False)
        qi = pl.program_id(1)

        @pl.when(jnp.logical_and(pl.program_id(0) == 0, qi == 0))
        def _():
            for cp in _exchange_copies(*exchange):
                cp.start()

        n_full =jnp.right_shift(qi, (tk // tq).bit_length() - 1)
        lane, masks = _lane_masks()
        q = q_ref[...] * SCALE
        qs = [jnp.where(masks[e], q, jnp.zeros_like(q)) for e in range(2)]

        def scores(j):
            start = pl.multiple_of(j * tk, tk)
            kb = k_ref[pl.ds(start, tk), :]
            return tuple(_dot(qs[e], kb, NT_DIMS) - ct_ref[e:e + 1, pl.ds(start, tk)] for e in range(2))

        def softmax(s, m, masked):
            if masked:
                rows = lax.broadcasted_iota(jnp.int32, (tq, tk), 0) + (qi * tq - n_full * tk)
                cols = lax.broadcasted_iota(jnp.int32, (tq, tk), 1)
                s = tuple(jnp.where(cols <= rows, s[e], NEG) for e in range(2))
            m_new = tuple(jnp.maximum(m[e], jnp.max(s[e], axis=1, keepdims=True)) for e in range(2))
            p = tuple(jnp.exp(s[e] - m_new[e]).astype(BF16) for e in range(2))
            alpha = tuple(jnp.exp(m[e] - m_new[e]) for e in range(2))
            return m_new, p, alpha

        def weighted_values(j, p, alpha, acc):
            start = pl.multiple_of(j * tk, tk)
            vb = v_ref[pl.ds(start, tk), :]
            return tuple(alpha[e] * acc[e] + _dot(p[e], jnp.where(masks[e], vb, jnp.ones_like(vb)))
                         for e in range(2))

        def step(j, carry):
            s, p_prev, a_prev, m, acc = carry
            s_next = scores(j + 1)
            acc = weighted_values(jnp.maximum(j - 1, 0), p_prev, a_prev, acc)
            m, p, alpha = softmax(s, m, False)
            return s_next, p, alpha, m, acc

        two = lambda x: (x, x)
        init = (scores(0), two(jnp.zeros((tq, tk), BF16)), two(jnp.ones((tq, 1), F32)),
                two(jnp.full((tq, 1), NEG, F32)), two(jnp.zeros((tq, LANES), F32)))
        s, p_prev, a_prev, m, acc = lax.fori_loop(0, n_full, step, init)
        acc = weighted_values(jnp.maximum(n_full - 1, 0), p_prev, a_prev, acc)
        (m0, m1), p, alpha = softmax(s, m, True)
        acc0, acc1 = weighted_values(n_full, p, alpha, acc)
        l0 = acc0[:, HEAD_DIM:HEAD_DIM + 1]
        l1 = acc1[:, 0:1]
        o_ref[...] = jnp.where(masks[0], acc0 / l0, acc1 / l1).astype(BF16)
        lse_ref[...] = jnp.where(masks[0], m0 + jnp.log(l0), m1 + jnp.log(l1))

        @pl.when(jnp.logical_and(pl.program_id(0) == N_PAIRS - 1, qi == S // tq - 1))
        def _():
            for cp in _exchange_copies(*exchange):
                cp.wait()

    qspec = pl.BlockSpec((tq, LANES), lambda h, i: (i, h))
    return pl.pallas_call(
        body, name=name, grid=(N_PAIRS, S // tq),
        out_shape=[jax.ShapeDtypeStruct((S, D), BF16), jax.ShapeDtypeStruct((S, D), F32)]
        + _exchange_shapes(gather, False),
        in_specs=[qspec,
                  pl.BlockSpec((S, LANES), lambda h, i: (0, N_PAIRS + h)),
                  pl.BlockSpec((S, LANES), lambda h, i: (0, 2 * N_PAIRS + h)),
                  pl.BlockSpec((None, 2, S), lambda h, i: (h, 0, 0))] + [ANY] * n,
        out_specs=[qspec, qspec] + [ANY] * n,
        scratch_shapes=_exchange_scratch(n),
        compiler_params=_params("arbitrary", "arbitrary"),
    )(qkv, qkv, qkv, c_t2, *gather)


def _head_rowsum(a, b, name, tr=256):
    C = a.shape[1]

    def body(a_ref, b_ref, o_ref):
        r = lax.broadcasted_iota(jnp.int32, (LANES, LANES), 0) < HEAD_DIM
        cc = lax.broadcasted_iota(jnp.int32, (LANES, LANES), 1) < HEAD_DIM
        same_head = (r == cc).astype(BF16)
        for blk in range(C // LANES):
            sl = slice(blk * LANES, (blk + 1) * LANES)
            prod = a_ref[:, sl].astype(F32) * b_ref[:, sl].astype(F32)
            o_ref[:, sl] = _split_dot(prod, same_head, 2)

    row = pl.BlockSpec((tr, C), lambda i: (i, 0))
    return pl.pallas_call(
        body, name=name, grid=(S // tr,), out_shape=jax.ShapeDtypeStruct((S, C), F32),
        in_specs=[row, row], out_specs=row, compiler_params=_params("parallel"),
    )(a, b)


def _fox_bwd(qkv, do, lse, delta, c_t2, name, scatter):
    t = FOX_T
    nq = S // t

    n = len(scatter)

    def body(*refs):
        q_ref, k_ref, v_ref, do_ref, lse_ref, dl_ref, ct_ref = refs[:7]
        dq_ref, dk_ref, dv_ref, dcol_ref, drow_ref = refs[7 + n:12 + n]
        exchange = (refs[7:7 + n], refs[12 + n:12 + 2 * n], refs[12 + 2 * n:], True)
        kj = pl.program_id(1)

        @pl.when(jnp.logical_and(pl.program_id(0) == 0, kj == 0))
        def _():
            for cp in _exchange_copies(*exchange):
                cp.start()

        @pl.when(kj == 0)
        def _():
            dq_ref[...] = jnp.zeros_like(dq_ref)
            drow_ref[...] = jnp.zeros_like(drow_ref)

        lane, masks = _lane_masks()
        k = k_ref[...]
        v = v_ref[...]
        k_aug = [jnp.where(masks[e], k * SCALE, jnp.ones_like(k)) for e in range(2)]
        cs = [ct_ref[e:e + 1, :] for e in range(2)]

        def step(i, carry, masked):
            dk_acc, dv_acc = list(carry[0]), carry[1]
            r0 = pl.multiple_of(i * t, t)
            qb = q_ref[pl.ds(r0, t), :] * SCALE
            dob = do_ref[pl.ds(r0, t), :]
            dq_parts = []
            for e in range(2):
                lo = e * HEAD_DIM
                qe = jnp.where(masks[e], qb, jnp.zeros_like(qb))
                q_aug = jnp.where(masks[e], qb, jnp.ones_like(qb))
                doe = jnp.where(masks[e], dob, jnp.zeros_like(dob))
                s = _dot(qe, k, NT_DIMS) - cs[e]
                if masked:
                    rows = lax.broadcasted_iota(jnp.int32, (t, t), 0)
                    cols = lax.broadcasted_iota(jnp.int32, (t, t), 1)
                    s = jnp.where(cols <= rows, s, NEG)
                p = jnp.exp(s - lse_ref[pl.ds(r0, t), lo:lo + 1])
                ds = (p * (_dot(doe, v, NT_DIMS) - dl_ref[pl.ds(r0, t), lo:lo + 1])).astype(BF16)
                dv_acc = dv_acc + _dot(p.astype(BF16), doe, TN_DIMS)
                dk_acc[e] = dk_acc[e] + _dot(ds, q_aug, TN_DIMS)
                dq_parts.append(_dot(ds, k_aug[e]))
            dq_ref[pl.ds(r0, t), :] += jnp.where(masks[0], dq_parts[0], dq_parts[1])
            drow_ref[pl.ds(r0, t), :] += jnp.where(masks[0], dq_parts[1], dq_parts[0])
            return tuple(dk_acc), dv_acc

        zero = jnp.zeros((t, LANES), F32)
        carry = step(kj, ((zero, zero), zero), True)
        (dk0, dk1), dv = lax.fori_loop(kj + 1, nq, lambda i, cr: step(i, cr, False), carry)
        dk_ref[...] = jnp.where(masks[0], dk0, dk1).astype(BF16)
        dcol_ref[...] = jnp.where(masks[0], dk1, dk0)
        dv_ref[...] = dv.astype(BF16)

        @pl.when(jnp.logical_and(pl.program_id(0) == N_PAIRS - 1, kj == nq - 1))
        def _():
            for cp in _exchange_copies(*exchange):
                cp.wait()

    full = lambda off: pl.BlockSpec((S, LANES), lambda h, j, off=off: (0, off + h))
    kv = lambda off: pl.BlockSpec((t, LANES), lambda h, j, off=off: (j, off + h))
    return pl.pallas_call(
        body, name=name, grid=(N_PAIRS, nq),
        out_shape=[jax.ShapeDtypeStruct((S, D), F32), jax.ShapeDtypeStruct((S, D), BF16),
                   jax.ShapeDtypeStruct((S, D), BF16), jax.ShapeDtypeStruct((S, D), F32),
                   jax.ShapeDtypeStruct((S, D), F32)] + _exchange_shapes(scatter, True),
        in_specs=[full(0), kv(N_PAIRS), kv(2 * N_PAIRS), full(0), full(0), full(0),
                  pl.BlockSpec((None, 2, t), lambda h, j: (h, 0, j))] + [ANY] * n,
        out_specs=[full(0), kv(0), kv(0), kv(0), full(0)] + [ANY] * n,
        scratch_shapes=_exchange_scratch(n),
        compiler_params=_params("arbitrary", "arbitrary"),
    )(qkv, qkv, qkv, do, lse, delta, c_t2, *scatter)


B_PAIRS = 4
B_NB = S // B_W


def _group_consts(g):
    nbs = jnp.where(g == 0, B_NB // B_DILS[0], jnp.where(g == 1, B_NB // B_DILS[1], B_NB // B_DILS[2]))
    dil = jnp.where(g == 0, B_DILS[0], jnp.where(g == 1, B_DILS[1], B_DILS[2]))
    return nbs, dil


def _band(dil):
    qi = lax.broadcasted_iota(jnp.int32, (B_W, B_W), 0)
    kj = lax.broadcasted_iota(jnp.int32, (B_W, B_W), 1)
    dist_c = qi - kj
    dist_p = qi + B_W - kj
    return (dist_c * dil).astype(F32), dist_c >= 0, (dist_p * dil).astype(F32), dist_p <= B_W


def _dil_fwd(qp, kp, vp, slopes, name):
    def body(sl_ref, q_ref, kp_ref, kc_ref, vp_ref, vc_ref, o_ref, lse_ref):
        g, n = pl.program_id(0), pl.program_id(1)
        nbs, dil = _group_consts(g)
        has_prev = (n % nbs) != 0
        lane, masks = _lane_masks()
        bias_c, ok_c, bias_p, ok_p = _band(dil)
        ok_p = jnp.logical_and(ok_p, has_prev)
        for hp in range(B_PAIRS):
            cols = slice(hp * LANES, (hp + 1) * LANES)
            q = q_ref[:, cols] * SCALE
            kc, kpv, vc, vpv = kc_ref[:, cols], kp_ref[:, cols], vc_ref[:, cols], vp_ref[:, cols]
            outs, lses = [], []
            for e in range(2):
                slope = sl_ref[g * 8 + 2 * hp + e]
                qe = jnp.where(masks[e], q, jnp.zeros_like(q))
                sc = jnp.where(ok_c, _dot(qe, kc, NT_DIMS) - slope * bias_c, NEG)
                sp = jnp.where(ok_p, _dot(qe, kpv, NT_DIMS) - slope * bias_p, NEG)
                m = jnp.maximum(jnp.max(sc, axis=1, keepdims=True), jnp.max(sp, axis=1, keepdims=True))
                pc = jnp.exp(sc - m).astype(BF16)
                pp = jnp.exp(sp - m).astype(BF16)
                acc = (_dot(pc, jnp.where(masks[e], vc, jnp.ones_like(vc)))
                       + _dot(pp, jnp.where(masks[e], vpv, jnp.ones_like(vpv))))
                l = acc[:, HEAD_DIM:HEAD_DIM + 1] if e == 0 else acc[:, 0:1]
                outs.append(acc / l)
                lses.append(m + jnp.log(l))
            o_ref[:, cols] = jnp.where(masks[0], outs[0], outs[1])
            lse_ref[:, cols] = jnp.where(masks[0], lses[0], lses[1])

    cur = pl.BlockSpec((None, B_W, B_OUT), lambda g, n, sl: (g, n, 0))
    prev = pl.BlockSpec((None, B_W, B_OUT), lambda g, n, sl: (g, jnp.maximum(n - 1, 0), 0))
    return pl.pallas_call(
        body, name=name,
        grid_spec=pltpu.PrefetchScalarGridSpec(
            num_scalar_prefetch=1, grid=(3, B_NB),
            in_specs=[cur, prev, cur, prev, cur], out_specs=[cur, cur]),
        out_shape=[jax.ShapeDtypeStruct((3, S, B_OUT), F32), jax.ShapeDtypeStruct((3, S, B_OUT), F32)],
        compiler_params=_params("parallel", "parallel"),
    )(slopes, qp, kp, kp, vp, vp)


def _dil_merge(og, lseg, name, tr=256):
    def body(o_ref, l_ref, out_ref, lse_ref):
        l0, l1, l2 = l_ref[0], l_ref[1], l_ref[2]
        m = jnp.maximum(jnp.maximum(l0, l1), l2)
        w0, w1, w2 = jnp.exp(l0 - m), jnp.exp(l1 - m), jnp.exp(l2 - m)
        den = w0 + w1 + w2
        out_ref[...] = ((w0 * o_ref[0] + w1 * o_ref[1] + w2 * o_ref[2]) / den).astype(BF16)
        lse_ref[...] = m + jnp.log(den)

    blk3 = pl.BlockSpec((3, tr, B_OUT), lambda i: (0, i, 0))
    blk = pl.BlockSpec((tr, B_OUT), lambda i: (i, 0))
    return pl.pallas_call(
        body, name=name, grid=(S // tr,),
        out_shape=[jax.ShapeDtypeStruct((S, B_OUT), BF16), jax.ShapeDtypeStruct((S, B_OUT), F32)],
        in_specs=[blk3, blk3], out_specs=[blk, blk], compiler_params=_params("parallel"),
    )(og, lseg)


def _dil_bwd(qp, kp, vp, dop, lsep, dlp, slopes, name, scatter):
    n_ex = len(scatter)

    def body(sl_ref, *refs):
        (qc_ref, qn_ref, kp_ref, kc_ref, vp_ref, vc_ref, doc_ref, don_ref,
         lc_ref, ln_ref, dc_ref, dn_ref) = refs[:12]
        dq_ref, dk_ref, dv_ref = refs[12 + n_ex:15 + n_ex]
        exchange = (refs[12:12 + n_ex], refs[15 + n_ex:15 + 2 * n_ex], refs[15 + 2 * n_ex:], True)
        g, n = pl.program_id(0), pl.program_id(1)

        @pl.when(jnp.logical_and(g == 0, n == 0))
        def _():
            for cp in _exchange_copies(*exchange):
                cp.start()

        nbs, dil = _group_consts(g)
        has_prev = (n % nbs) != 0
        has_next = jnp.logical_and(n + 1 < B_NB, ((n + 1) % nbs) != 0)
        lane, masks = _lane_masks()
        bias_c, ok_c, bias_p, ok_p = _band(dil)
        ok_pp = jnp.logical_and(ok_p, has_prev)
        ok_np = jnp.logical_and(ok_p, has_next)
        for hp in range(B_PAIRS):
            cols = slice(hp * LANES, (hp + 1) * LANES)
            qc, qn = qc_ref[:, cols] * SCALE, qn_ref[:, cols] * SCALE
            kc, kpv, vc, vpv = kc_ref[:, cols], kp_ref[:, cols], vc_ref[:, cols], vp_ref[:, cols]
            doc, don = doc_ref[:, cols], don_ref[:, cols]
            dq = jnp.zeros((B_W, LANES), F32)
            dk = jnp.zeros((B_W, LANES), F32)
            dv = jnp.zeros((B_W, LANES), F32)
            for e in range(2):
                lo = hp * LANES + e * HEAD_DIM
                slope = sl_ref[g * 8 + 2 * hp + e]
                qce = jnp.where(masks[e], qc, jnp.zeros_like(qc))
                qne = jnp.where(masks[e], qn, jnp.zeros_like(qn))
                doce = jnp.where(masks[e], doc, jnp.zeros_like(doc))
                done = jnp.where(masks[e], don, jnp.zeros_like(don))
                kce = jnp.where(masks[e], kc * SCALE, jnp.zeros_like(kc))
                kpe = jnp.where(masks[e], kpv * SCALE, jnp.zeros_like(kpv))
                lse_c, dl_c = lc_ref[:, lo:lo + 1], dc_ref[:, lo:lo + 1]
                lse_n, dl_n = ln_ref[:, lo:lo + 1], dn_ref[:, lo:lo + 1]
                s = jnp.where(ok_c, _dot(qce, kc, NT_DIMS) - slope * bias_c, NEG)
                p = jnp.exp(s - lse_c)
                dsb = (p * (_dot(doce, vc, NT_DIMS) - dl_c)).astype(BF16)
                dq = dq + _dot(dsb, kce)
                dk = dk + _dot(dsb, qce, TN_DIMS)
                dv = dv + _dot(p.astype(BF16), doce, TN_DIMS)
                s = jnp.where(ok_pp, _dot(qce, kpv, NT_DIMS) - slope * bias_p, NEG)
                p = jnp.exp(s - lse_c)
                dsb = (p * (_dot(doce, vpv, NT_DIMS) - dl_c)).astype(BF16)
                dq = dq + _dot(dsb, kpe)
                s = jnp.where(ok_np, _dot(qne, kc, NT_DIMS) - slope * bias_p, NEG)
                p = jnp.exp(s - lse_n)
                dsb = (p * (_dot(done, vc, NT_DIMS) - dl_n)).astype(BF16)
                dk = dk + _dot(dsb, qne, TN_DIMS)
                dv = dv + _dot(p.astype(BF16), done, TN_DIMS)
            dq_ref[:, cols] = dq.astype(BF16)
            dk_ref[:, cols] = dk.astype(BF16)
            dv_ref[:, cols] = dv.astype(BF16)

        @pl.when(jnp.logical_and(g == 2, n == B_NB - 1))
        def _():
            for cp in _exchange_copies(*exchange):
                cp.wait()

    cur = pl.BlockSpec((None, B_W, B_OUT), lambda g, n, sl: (g, n, 0))
    prev = pl.BlockSpec((None, B_W, B_OUT), lambda g, n, sl: (g, jnp.maximum(n - 1, 0), 0))
    nxt = pl.BlockSpec((None, B_W, B_OUT), lambda g, n, sl: (g, jnp.minimum(n + 1, B_NB - 1), 0))
    return pl.pallas_call(
        body, name=name,
        grid_spec=pltpu.PrefetchScalarGridSpec(
            num_scalar_prefetch=1, grid=(3, B_NB),
            in_specs=[cur, nxt, prev, cur, prev, cur, cur, nxt, cur, nxt, cur, nxt] + [ANY] * n_ex,
            out_specs=[cur, cur, cur] + [ANY] * n_ex,
            scratch_shapes=_exchange_scratch(n_ex)),
        out_shape=[jax.ShapeDtypeStruct((3, S, B_OUT), BF16)] * 3 + _exchange_shapes(scatter, True),
        compiler_params=_params("arbitrary", "arbitrary"),
    )(slopes, qp, qp, kp, kp, vp, vp, dop, dop, lsep, lsep, dlp, dlp, *scatter)


def _rows_block(shape, max_bytes=2 * 1024 * 1024):
    rows, cols = shape
    padded_cols = -(-cols // LANES) * LANES
    for tr in (1024, 512, 256, 128, 64, 32, 16):
        if rows % tr == 0 and tr * padded_cols * 4 <= max_bytes:
            return tr
    return rows


def _adam_update(w, m, v, g):
    m_new = ADAM_B1 * m + (1.0 - ADAM_B1) * g
    v_new = ADAM_B2 * v + (1.0 - ADAM_B2) * (g * g)
    m_hat = m_new / (1.0 - ADAM_B1 ** ADAM_STEP)
    v_hat = v_new / (1.0 - ADAM_B2 ** ADAM_STEP)
    delta = -ADAM_LR * (m_hat / (jnp.sqrt(v_hat) + ADAM_EPS) + ADAM_WD * w)
    return delta, m_new, v_new


def _adamw_sharded(w, m, v, parts, name):
    R, C = w.shape
    tr = _rows_block((R, C), max_bytes=1024 * 1024)

    def body(w_ref, m_ref, v_ref, p_ref, g_ref, d_ref, mo_ref, vo_ref):
        g = p_ref[0].astype(F32)
        for dev in range(1, N_DEV):
            g = g + p_ref[dev].astype(F32)
        g_ref[...] = g
        d_ref[...], mo_ref[...], vo_ref[...] = _adam_update(w_ref[...], m_ref[...], v_ref[...], g)

    blk = pl.BlockSpec((tr, C), lambda i: (i, 0))
    out = jax.ShapeDtypeStruct((R, C), F32)
    return pl.pallas_call(
        body, name=name, grid=(R // tr,),
        in_specs=[blk, blk, blk, pl.BlockSpec((N_DEV, tr, C), lambda i: (0, i, 0))],
        out_specs=[blk, blk, blk, blk], out_shape=[out, out, out, out],
        compiler_params=_params("parallel"),
    )(w, m, v, parts)


def _adamw_replicated(w, m, v, parts, name):
    def body(w_ref, m_ref, v_ref, p_ref, g_ref, d_ref, mo_ref, vo_ref):
        g = p_ref[0]
        for dev in range(1, N_DEV):
            g = g + p_ref[dev]
        g_ref[...] = g
        d_ref[...], mo_ref[...], vo_ref[...] = _adam_update(w_ref[...], m_ref[...], v_ref[...], g)

    out = jax.ShapeDtypeStruct(w.shape, F32)
    return pl.pallas_call(body, name=name, out_shape=[out, out, out, out], compiler_params=_params())(w, m, v, parts)


def _cols_from_slots(g):
    return g.transpose(1, 0, 2).reshape(g.shape[1], N_DEV * g.shape[2])


def _cols_to_slots(w):
    k, n = w.shape
    return w.reshape(k, N_DEV, n // N_DEV).transpose(1, 0, 2)


def _permute(t, dil):
    c = t.shape[1]
    return t.reshape(S // dil, dil, c).transpose(1, 0, 2).reshape(S, c)


def _unpermute(t, dil):
    c = t.shape[1]
    return t.reshape(dil, S // dil, c).transpose(1, 0, 2).reshape(S, c)


def _group_permute(t):
    return jnp.stack([_permute(t[:, g * B_OUT:(g + 1) * B_OUT], B_DILS[g]) for g in range(3)])


def _same_permute(t):
    return jnp.stack([_permute(t, d) for d in B_DILS])


def _group_unpermute(t):
    return jnp.stack([_unpermute(t[g], B_DILS[g]) for g in range(3)])


SMALL_ROWS = 144


def _pack_small(a_b_f, kv_g, mix_g, ffn_g, conv_b, fin_g):
    flat = jnp.concatenate([a_b_f.reshape(-1), kv_g.reshape(-1), mix_g.reshape(-1), ffn_g.reshape(-1),
                            conv_b.reshape(-1), fin_g.reshape(-1)])
    return jnp.pad(flat, (0, SMALL_ROWS * LANES - flat.shape[0])).reshape(SMALL_ROWS, LANES)


def _unpack_small(p):
    flat = p.reshape(-1)
    out, off = [], 0
    for shape in ((1, A_HEADS), (D,), (2, D), (2, D), (2, 2 * D_FF), (D,)):
        size = math.prod(shape)
        out.append(flat[off:off + size].reshape(shape))
        off += size
    return out


def _unpack_late(g):
    w_up = g[3].reshape(N_DEV, 2, D, -1).transpose(1, 2, 0, 3).reshape(2, D, 2 * D_FF)
    w_down = g[4].reshape(N_DEV, 2, -1, D).transpose(1, 0, 2, 3).reshape(2, D_FF, D)
    conv_w = g[5].reshape(N_DEV, 2, 3, -1).transpose(1, 2, 0, 3).reshape(2, 3, 2 * D_FF)
    return _cols_from_slots(g[0]), _cols_from_slots(g[1]), _cols_from_slots(g[2]), w_up, w_down, conv_w


def _ffn_slots(dw_up, dw_down, dconv_w):
    return [_cols_to_slots(dw_up), dw_down.reshape(N_DEV, -1, D), _cols_to_slots(dconv_w)]


def _local_step(x0, target, w_in_pad, w_out, late_shards,
                a_b_f, kv_norm_g, mix_norm_g, ffn_norm_g, ffn_conv_b, final_norm_g):
    w_qkv, w_f = w_in_pad[:, :A_QKV], w_in_pad[:, A_QKV:]
    conv_b = ffn_conv_b.reshape(2, 1, 2 * D_FF)
    slopes = jnp.exp2(-8.0 * jnp.arange(1, 25, dtype=F32) / 24)

    def gain(g):
        return g.reshape(1, D)

    (h1,) = _rmsnorm_fwd(x0, [gain(mix_norm_g[0])], "norm_mix0")
    qkv = _matmul(h1, w_qkv, mode="nn", out_dtype=BF16, name="proj_qkv", tm=512, tn=A_QKV)
    z = _matmul(h1, w_f, mode="nn", out_dtype=F32, name="proj_gate", tm=S, tn=LANES)
    z_t = z[:, :A_HEADS].T
    b_f = a_b_f.reshape(A_HEADS, 1)
    c_t = _fox_prep_fwd(z_t, b_f, "fox_prep")
    c_t2 = c_t.reshape(N_PAIRS, 2, S)
    o_a, lse_a, *late = _fox_fwd(qkv, c_t2, "fox_fwd", late_shards)
    w_q, w_bo, w_kvf, w_up, w_down, conv_w = _unpack_late(late)
    x1 =_matmul(o_a, w_out, mode="nn", out_dtype=F32, name="a_out", tm=512, tn=D, res=x0)

    def ffn_fwd(xin, layer):
        (h,) = _rmsnorm_fwd(xin, [gain(ffn_norm_g[layer])], f"norm_ffn{layer}")
        u = _matmul(h, w_up[layer], mode="nn", out_dtype=BF16, name=f"ffn_up{layer}", tm=512, tn=2 * D_FF)
        act = _convgate_fwd(u, conv_w[layer], conv_b[layer], f"convgate{layer}")
        xout = _matmul(act, w_down[layer], mode="nn", out_dtype=F32, name=f"ffn_down{layer}", tm=512, tn=D, res=xin)
        return h, u, act, xout

    h2, u0, act0, x2 = ffn_fwd(x1, 0)
    hk, h3 = _rmsnorm_fwd(x2, [gain(kv_norm_g), gain(mix_norm_g[1])], "norm_kv_mix1")
    kv = _matmul(hk, w_kvf, mode="nn", out_dtype=BF16, name="proj_kv", tm=512, tn=B_KV)
    qb = _matmul(h3, w_q, mode="nn", out_dtype=BF16, name="proj_qb", tm=512, tn=B_Q)
    qp, kp, vp = _group_permute(qb), _group_permute(kv[:, :B_Q]), _group_permute(kv[:, B_Q:])
    og_p, lseg_p = _dil_fwd(qp, kp, vp, slopes, "dil_fwd")
    o_b, lse_b = _dil_merge(_group_unpermute(og_p), _group_unpermute(lseg_p), "dil_merge")
    x3 = _matmul(o_b, w_bo, mode="nn", out_dtype=F32, name="b_out", tm=512, tn=D, res=x2)
    h4, u1, act1, x4 = ffn_fwd(x3, 1)
    loss_blk, dx4, dx4b, dg_final = _final_loss(x4, target, gain(final_norm_g), "final_loss")

    def ffn_bwd(dx, dxb, xin, h, u, act, layer):
        dact = _matmul(dxb, w_down[layer], mode="nt", out_dtype=BF16, name=f"d_act{layer}", tm=512, tn=D_FF)
        dw_down = _matmul_tn(act, dxb, out_dtype=BF16, name=f"dw_down{layer}")
        da, dg, dwa, dwg, dba, dbg = _convgate_bwd(u, conv_w[layer], conv_b[layer], dact, f"convgate_bwd{layer}")
        du_a = _conv_input_bwd(da, conv_w[layer][:, :D_FF], f"conv_in_bwd_a{layer}")
        du_g = _conv_input_bwd(dg, conv_w[layer][:, D_FF:], f"conv_in_bwd_g{layer}")
        dw_up = jnp.concatenate(
            [_matmul_tn(h, du_a, out_dtype=BF16, name=f"dw_up_a{layer}"),
             _matmul_tn(h, du_g, out_dtype=BF16, name=f"dw_up_g{layer}")], axis=1)
        dh = _matmul(du_a, w_up[layer][:, :D_FF], mode="nt", out_dtype=F32, name=f"dh_ffn_a{layer}", tm=512, tn=D)
        dh = _matmul(du_g, w_up[layer][:, D_FF:], mode="nt", out_dtype=F32, name=f"dh_ffn_g{layer}", tm=512, tn=D,
                     res=dh)
        dxin, dxinb, dgain = _rmsnorm_bwd(xin, dh, gain(ffn_norm_g[layer]), dx, f"norm_ffn_bwd{layer}")
        dconv_w = jnp.concatenate([dwa, dwg], axis=1)
        dconv_b = jnp.concatenate([dba, dbg], axis=1)
        return dxin, dxinb, dgain, dw_up, dw_down, dconv_w, dconv_b

    dx3, dx3b, dg_ffn1, dw_up1, dw_down1, dconv_w1, dconv_b1 = ffn_bwd(dx4, dx4b, x3, h4, u1, act1, 1)

    do_b = _matmul(dx3b, w_bo, mode="nt", out_dtype=BF16, name="d_ob", tm=1024, tn=B_OUT)
    dw_bo = _matmul_tn(o_b, dx3b, out_dtype=BF16, name="dw_bo")
    dl_b = _head_rowsum(do_b, o_b, "delta_b")
    dqp, dkp, dvp, *land_ffn1 = _dil_bwd(qp, kp, vp, _same_permute(do_b), _same_permute(lse_b), _same_permute(dl_b),
                                         slopes, "dil_bwd", _ffn_slots(dw_up1, dw_down1, dconv_w1))

    def natural(tp):
        return jnp.concatenate([_unpermute(tp[g], B_DILS[g]) for g in range(3)], axis=1)

    dqb = natural(dqp)
    dkv = jnp.concatenate([natural(dkp), natural(dvp)], axis=1)
    dw_q = _matmul_tn(h3, dqb, out_dtype=BF16, name="dw_q")
    dw_kv = _matmul_tn(hk, dkv, out_dtype=BF16, name="dw_kv")
    dh3 = _matmul(dqb, w_q, mode="nt", out_dtype=F32, name="dh_mix1", tm=512, tn=D)
    dhk = _matmul(dkv, w_kvf, mode="nt", out_dtype=F32, name="dh_kv", tm=512, tn=D)
    dx2, _, dg_mix1 = _rmsnorm_bwd(x2, dh3, gain(mix_norm_g[1]), dx3, "norm_mix1_bwd")
    dx2, dx2b, dg_kv = _rmsnorm_bwd(x2, dhk, gain(kv_norm_g), dx2, "norm_kv_bwd")

    dx1, dx1b, dg_ffn0, dw_up0, dw_down0, dconv_w0, dconv_b0 = ffn_bwd(dx2, dx2b, x1, h2, u0, act0, 0)

    do_a = _matmul(dx1b, w_out, mode="nt", out_dtype=BF16, name="d_oa", tm=512, tn=D)
    dw_out = _matmul_tn(o_a, dx1b, out_dtype=BF16, name="dw_out")
    dl_a = _head_rowsum(do_a, o_a, "delta_a")
    dq_a, dk_a, dv_a, dcol, drow, *land = _fox_bwd(
        qkv, do_a, lse_a, dl_a, c_t2, "fox_bwd",
        [_cols_to_slots(dw_q), _cols_to_slots(dw_bo), _cols_to_slots(dw_kv)] + _ffn_slots(dw_up0, dw_down0, dconv_w0))
    land_q, land_bo, land_kv, *land_ffn0 = land

    def head_sums(t):
        return t.reshape(S, N_PAIRS, 2, HEAD_DIM)[:, :, ::-1, 0].reshape(S, A_HEADS).T

    dz_t, db_f = _fox_prep_bwd(head_sums(drow), head_sums(dcol), z_t, b_f, "fox_prep_bwd")
    dz = jnp.pad(dz_t.T, ((0, 0), (0, LANES - A_HEADS))).astype(BF16)
    dproj = jnp.concatenate([dq_a.astype(BF16), dk_a, dv_a, dz], axis=1)
    dw_in = _matmul_tn(h1, dproj, out_dtype=BF16, name="dw_in")
    dh1 = _matmul(dproj, w_in_pad, mode="nt", out_dtype=F32, name="dh_mix0", tm=512, tn=D)
    grad_x, _, dg_mix0 = _rmsnorm_bwd(x0, dh1, gain(mix_norm_g[0]), dx1, "norm_mix0_bwd")

    land_in, land_out = _scatter_now(
        [_cols_to_slots(dw_in[:, :A_QKV + A_HEADS]), dw_out.reshape(N_DEV, D // N_DEV, D)], "scatter_mixer_a_grads")
    dg_mix = jnp.concatenate([dg_mix0, dg_mix1], axis=0)
    dg_ffn = jnp.concatenate([dg_ffn0, dg_ffn1], axis=0)
    dconv_b = jnp.concatenate([dconv_b0, dconv_b1], axis=0)
    landed = [land_in, land_out, land_q, land_bo, land_kv, land_ffn0[0], land_ffn1[0], land_ffn0[1], land_ffn1[1],
              land_ffn0[2], land_ffn1[2]]
    small = (db_f, dg_kv, dg_mix, dg_ffn, dconv_b, dg_final)
    return loss_blk, grad_x, landed, small


def kernel(x, a_w_in, a_b_f, a_w_out, b_w_q, b_w_out, kv_norm_g, w_kv, mix_norm_g, ffn_norm_g, ffn_w_up, ffn_conv_w, ffn_conv_b, ffn_w_down, final_norm_g, loss_target, m_a_w_in, m_a_b_f, m_a_w_out, m_b_w_q, m_b_w_out, m_kv_norm_g, m_w_kv, m_mix_norm_g, m_ffn_norm_g, m_ffn_w_up, m_ffn_conv_w, m_ffn_conv_b, m_ffn_w_down, m_final_norm_g, v_a_w_in, v_a_b_f, v_a_w_out, v_b_w_q, v_b_w_out, v_kv_norm_g, v_w_kv, v_mix_norm_g, v_ffn_norm_g, v_ffn_w_up, v_ffn_conv_w, v_ffn_conv_b, v_ffn_w_down, v_final_norm_g):
    def shards(a_w_in, a_w_out, b_w_q, b_w_out, w_kv, ffn_w_up, ffn_w_down, ffn_conv_w):
        return [a_w_in[0], a_w_out[0], b_w_q[0], b_w_out[0], w_kv, ffn_w_up[0], ffn_w_up[1],
                ffn_w_down[0], ffn_w_down[1], ffn_conv_w[0], ffn_conv_w[1]]

    w_loc = shards(a_w_in, a_w_out, b_w_q, b_w_out, w_kv, ffn_w_up, ffn_w_down, ffn_conv_w)
    m_loc = shards(m_a_w_in, m_a_w_out, m_b_w_q, m_b_w_out, m_w_kv, m_ffn_w_up, m_ffn_w_down, m_ffn_conv_w)
    v_loc = shards(v_a_w_in, v_a_w_out, v_b_w_q, v_b_w_out, v_w_kv, v_ffn_w_up, v_ffn_w_down, v_ffn_conv_w)

    g_in, g_out = _all_gather([a_w_in[0].astype(BF16), a_w_out[0].astype(BF16)], "gather_mixer_a")
    w_in = _cols_from_slots(g_in)
    w_in_pad = jnp.pad(w_in, ((0, 0), (0, A_PROJ_PAD - w_in.shape[1])))
    w_out = g_out.reshape(D, D)
    late_shards = [b_w_q[0].astype(BF16), b_w_out[0].astype(BF16), w_kv.astype(BF16),
                   ffn_w_up.reshape(2 * D, -1).astype(BF16), ffn_w_down.reshape(-1, D).astype(BF16),
                   ffn_conv_w.reshape(6, -1)]

    loss_blk, grad_x, landed, small_grads = _local_step(
        x[0], loss_target[0], w_in_pad, w_out, late_shards,
        a_b_f, kv_norm_g, mix_norm_g, ffn_norm_g, ffn_conv_b, final_norm_g)

    big = [_adamw_sharded(w_loc[k], m_loc[k], v_loc[k], landed[k], f"adamw{k}") for k in range(11)]

    small_part = _pack_small(*small_grads)
    (small_parts,) = _all_gather([small_part], "gather_small_grads")
    small = _adamw_replicated(
        _pack_small(a_b_f, kv_norm_g, mix_norm_g, ffn_norm_g, ffn_conv_b, final_norm_g),
        _pack_small(m_a_b_f, m_kv_norm_g, m_mix_norm_g, m_ffn_norm_g, m_ffn_conv_b, m_final_norm_g),
        _pack_small(v_a_b_f, v_kv_norm_g, v_mix_norm_g, v_ffn_norm_g, v_ffn_conv_b, v_final_norm_g),
        small_parts, "adamw_small")

    loss = lax.psum(loss_blk[0, 0], ("x", "y", "c"))

    def assemble(kind):
        b = [r[kind] for r in big]
        s_abf, s_kv, s_mix, s_ffn, s_cb, s_fin = _unpack_small(small[kind])
        return [b[0][None], s_abf, b[1][None], b[2][None], b[3][None], s_kv, b[4], s_mix, s_ffn,
                jnp.stack([b[5], b[6]]), jnp.stack([b[9], b[10]]), s_cb, jnp.stack([b[7], b[8]]), s_fin]

    return (loss, grad_x[None], *assemble(0), *assemble(1), *assemble(2), *assemble(3))
```

```python
import functools
import math

import jax
import jax.numpy as jnp
from jax import lax
from jax.experimental import pallas as pl
from jax.experimental.pallas import tpu as pltpu

F32 = jnp.float32
BF16 = jnp.bfloat16

S = 4096
D = 1024
N_DEV = 8
A_HEADS = 16
HEAD_DIM = 64
A_QKV = 3072
A_PROJ_PAD = 3200
B_Q = 1536
B_OUT = 512
B_KV = 3072
B_W = 128
B_DILS = (1, 4, 16)
D_FF = 2816
RMS_EPS = 1e-6
SCALE = HEAD_DIM ** -0.5
NEG = -1e30

ADAM_LR = 0.001
ADAM_B1 = 0.9
ADAM_B2 = 0.999
ADAM_EPS = 1e-08
ADAM_WD = 0.01
ADAM_STEP = 10

LANES = 128
VMEM_LIMIT = 56 * 1024 * 1024
MESH = pl.DeviceIdType.MESH
ANY = pl.BlockSpec(memory_space=pl.ANY)

NT_DIMS = (((1,), (1,)), ((), ()))
TN_DIMS = (((0,), (0,)), ((), ()))
NN_DIMS = (((1,), (0,)), ((), ()))


def _params(*sem):
    return pltpu.CompilerParams(dimension_semantics=sem if sem else None, vmem_limit_bytes=VMEM_LIMIT)


def _dot(a, b, dims=NN_DIMS):
    return lax.dot_general(a, b, dims, preferred_element_type=F32)


def _split_dot(x, mat, pieces):
    out = None
    rem = x
    for _ in range(pieces):
        part = rem.astype(BF16)
        rem = rem - part.astype(F32)
        d = _dot(part, mat)
        out = d if out is None else out + d
    return out


def _pick(n, prefs):
    for p in prefs:
        if n % p == 0:
            return p
    return n


def _all_gather(arrays, name):
    n = len(arrays)

    def body(*refs):
        ins = refs[:n]
        outs = refs[n:2 * n]
        send_sems, recv_sems, local_sems = refs[2 * n:]
        x, y, c = lax.axis_index("x"), lax.axis_index("y"), lax.axis_index("c")
        me, sibling = (x, y, c), (x, y, 1 - c)
        chips = [(1 - x, y), (x, 1 - y), (1 - x, 1 - y)]

        def slot(a, px, py, pc):
            return outs[a].at[4 * px + 2 * py + pc]

        def copy(a, k, block, to, src=None):
            return pltpu.make_async_remote_copy(
                src_ref=slot(a, *block) if src is None else src, dst_ref=slot(a, *block),
                send_sem=send_sems.at[a, k], recv_sem=recv_sems.at[a, k],
                device_id=to, device_id_type=MESH)

        mine = [pltpu.make_async_copy(ins[a], slot(a, *me), local_sems.at[a]) for a in range(n)]
        for cp in mine:
            cp.start()
        first = []
        for a in range(n):
            first.append(copy(a, 0, me, sibling, src=ins[a]))
            first += [copy(a, 1 + j, me, (*chip, c), src=ins[a]) for j, chip in enumerate(chips)]
        for cp in first:
            cp.start()
        passed = []
        for j, chip in enumerate(chips):
            for a in range(n):
                copy(a, 1 + j, (*chip, c), me).wait_recv()
                fwd = copy(a, 4 + j, (*chip, c), sibling)
                fwd.start()
                passed.append(fwd)
        for a in range(n):
            copy(a, 0, sibling, me).wait_recv()
            for j, chip in enumerate(chips):
                copy(a, 4 + j, (*chip, 1 - c), me).wait_recv()
        for cp in first + passed:
            cp.wait_send()
        for cp in mine:
            cp.wait()

    return pl.pallas_call(
        body, name=name,
        out_shape=[jax.ShapeDtypeStruct((N_DEV,) + a.shape, a.dtype) for a in arrays],
        in_specs=[ANY] * n, out_specs=[ANY] * n,
        scratch_shapes=[pltpu.SemaphoreType.DMA((n, 7)), pltpu.SemaphoreType.DMA((n, 7)),
                        pltpu.SemaphoreType.DMA((n,))],
    )(*arrays)


PEER_FLIPS = [(dx, dy, dc) for dx in (0, 1) for dy in (0, 1) for dc in (0, 1) if (dx, dy, dc) != (0, 0, 0)]


def _exchange_copies(ins, outs, sems, scatter):
    if not ins:
        return []
    send_sems, recv_sems, local_sems = sems
    x, y, c = lax.axis_index("x"), lax.axis_index("y"), lax.axis_index("c")
    me = 4 * x + 2 * y + c
    copies = []
    for a in range(len(ins)):
        copies.append(pltpu.make_async_copy(ins[a].at[me] if scatter else ins[a], outs[a].at[me], local_sems.at[a]))
        for k, (dx, dy, dc) in enumerate(PEER_FLIPS):
            px, py, pc = (1 - x if dx else x), (1 - y if dy else y), (1 - c if dc else c)
            copies.append(pltpu.make_async_remote_copy(
                src_ref=ins[a].at[4 * px + 2 * py + pc] if scatter else ins[a], dst_ref=outs[a].at[me],
                send_sem=send_sems.at[a, k], recv_sem=recv_sems.at[a, k],
                device_id=(px, py, pc), device_id_type=MESH))
    return copies


def _exchange_scratch(n):
    if n == 0:
        return []
    return [pltpu.SemaphoreType.DMA((n, 7)), pltpu.SemaphoreType.DMA((n, 7)), pltpu.SemaphoreType.DMA((n,))]


def _exchange_shapes(arrays, scatter):
    return [jax.ShapeDtypeStruct((N_DEV,) + (a.shape[1:] if scatter else a.shape), a.dtype) for a in arrays]


def _final_exchange(scatter, gather, name):
    ns, ng = len(scatter), len(gather)

    def body(*refs):
        ins, outs, sems = refs[:ns + ng], refs[ns + ng:2 * (ns + ng)], refs[2 * (ns + ng):]
        copies = (_exchange_copies(ins[:ns], outs[:ns], sems[:3], True)
                  + _exchange_copies(ins[ns:], outs[ns:], sems[3:], False))
        for cp in copies:
            cp.start()
        for cp in copies:
            cp.wait()

    res = pl.pallas_call(
        body, name=name, out_shape=_exchange_shapes(scatter, True) + _exchange_shapes(gather, False),
        in_specs=[ANY] * (ns + ng), out_specs=[ANY] * (ns + ng),
        scratch_shapes=_exchange_scratch(ns) + _exchange_scratch(ng),
    )(*scatter, *gather)
    return res[:ns], res[ns:]


MM_ROWS = 512
MM_COLS = 1024


def _matmul(a, b, *, mode, out_dtype, name, tm, tn, res=None):
    if mode == "nn":
        (M, K), (K2, N) = a.shape, b.shape
    else:
        (M, K), (N, K2) = a.shape, b.shape
    assert K == K2, (a.shape, b.shape, mode)
    tm, tn = min(tm, M), min(tn, N)
    sm = min(tm, MM_ROWS)
    sn = tn if tn <= MM_COLS else _pick(tn, (512, 256, 128))
    assert M % tm == 0 and N % tn == 0 and tm % sm == 0, (M, N, K, tm, tn)
    dims = NN_DIMS if mode == "nn" else NT_DIMS
    a_spec = pl.BlockSpec((tm, K), lambda i, j: (i, 0))
    if mode == "nt":
        b_spec = pl.BlockSpec((tn, K), lambda i, j: (j, 0))
    else:
        b_spec = pl.BlockSpec((K, tn), lambda i, j: (0, j))
    o_spec = pl.BlockSpec((tm, tn), lambda i, j: (i, j))
    has_res = res is not None

    def body(*refs):
        a_ref, b_ref = refs[0], refs[1]
        r_ref = refs[2] if has_res else None
        o_ref = refs[2 + has_res]

        def chunk(r, carry):
            rows = pl.ds(pl.multiple_of(r * sm, sm), sm)
            av = a_ref[rows, :]
            for c0 in range(0, tn, sn):
                bv = b_ref[c0:c0 + sn, :] if mode == "nt" else b_ref[:, c0:c0 + sn]
                total = _dot(av, bv, dims)
                if has_res:
                    total = total + r_ref[rows, c0:c0 + sn]
                o_ref[rows, c0:c0 + sn] = total.astype(out_dtype)
            return carry

        lax.fori_loop(0, tm // sm, chunk, 0)

    return pl.pallas_call(
        body, name=name, grid=(M // tm, N // tn),
        out_shape=jax.ShapeDtypeStruct((M, N), out_dtype),
        in_specs=[a_spec, b_spec] + ([o_spec] if has_res else []),
        out_specs=o_spec,
        compiler_params=_params("parallel", "parallel"),
    )(*((a, b, res) if has_res else (a, b)))


def _matmul_tn(a, b, *, out_dtype, name, tk=512, sm=256):
    (K, M), (K2, N) = a.shape, b.shape
    assert K == K2 and K % tk == 0 and M % sm == 0, (a.shape, b.shape)
    nk = K // tk

    def body(a_ref, b_ref, o_ref, acc_ref):
        k = pl.program_id(0)

        @pl.when(k == 0)
        def _():
            acc_ref[...] = jnp.zeros_like(acc_ref)

        def chunk(mi, carry):
            cols = pl.ds(pl.multiple_of(mi * sm, sm), sm)
            acc_ref[cols, :] += _dot(a_ref[:, cols].T, b_ref[...])
            return carry

        lax.fori_loop(0, M // sm, chunk, 0)

        @pl.when(k == nk - 1)
        def _():
            def emit(mi, carry):
                rows = pl.ds(pl.multiple_of(mi * sm, sm), sm)
                o_ref[rows, :] = acc_ref[rows, :].astype(out_dtype)
                return carry
            lax.fori_loop(0, M // sm, emit, 0)

    return pl.pallas_call(
        body, name=name, grid=(nk,),
        out_shape=jax.ShapeDtypeStruct((M, N), out_dtype),
        in_specs=[pl.BlockSpec((tk, M), lambda k: (k, 0)), pl.BlockSpec((tk, N), lambda k: (k, 0))],
        out_specs=pl.BlockSpec((M, N), lambda k: (0, 0)),
        scratch_shapes=[pltpu.VMEM((M, N), F32)],
        compiler_params=_params("arbitrary"),
    )(a, b)


def _rmsnorm_fwd(x, gains, name, tr=256):
    n = len(gains)

    def body(*refs):
        x_ref = refs[0]
        xv = x_ref[...]
        r = lax.rsqrt(jnp.mean(xv * xv, axis=-1, keepdims=True) + RMS_EPS)
        y = xv * r
        for a in range(n):
            refs[1 + n + a][...] = (y * refs[1 + a][...]).astype(BF16)

    row = pl.BlockSpec((tr, D), lambda i: (i, 0))
    gain = pl.BlockSpec((1, D), lambda i: (0, 0))
    return pl.pallas_call(
        body, name=name, grid=(S // tr,),
        out_shape=[jax.ShapeDtypeStruct((S, D), BF16)] * n,
        in_specs=[row] + [gain] * n, out_specs=[row] * n,
        compiler_params=_params("parallel"),
    )(x, *gains)


def _rmsnorm_bwd(x, dy, g, dres, name, tr=256):
    def body(x_ref, dy_ref, g_ref, dres_ref, dx_ref, dxb_ref, dg_ref):
        xv = x_ref[...]
        dyv = dy_ref[...]
        r = lax.rsqrt(jnp.mean(xv * xv, axis=-1, keepdims=True) + RMS_EPS)
        xhat = xv * r
        dxhat = dyv * g_ref[...]
        mean_term = jnp.mean(dxhat * xhat, axis=-1, keepdims=True)
        dx = r * (dxhat - xhat * mean_term) + dres_ref[...]
        dx_ref[...] = dx
        dxb_ref[...] = dx.astype(BF16)
        part = jnp.sum(dyv * xhat, axis=0, keepdims=True)

        @pl.when(pl.program_id(0) == 0)
        def _():
            dg_ref[...] = part

        @pl.when(pl.program_id(0) > 0)
        def _():
            dg_ref[...] += part

    row = pl.BlockSpec((tr, D), lambda i: (i, 0))
    gain = pl.BlockSpec((1, D), lambda i: (0, 0))
    return pl.pallas_call(
        body, name=name, grid=(S // tr,),
        out_shape=[jax.ShapeDtypeStruct((S, D), F32), jax.ShapeDtypeStruct((S, D), BF16),
                   jax.ShapeDtypeStruct((1, D), F32)],
        in_specs=[row, row, gain, row], out_specs=[row, row, gain],
        compiler_params=_params("arbitrary"),
    )(x, dy, g, dres)


def _final_loss(x, target, g, name, tr=256):
    def body(x_ref, t_ref, g_ref, loss_ref, dx_ref, dxb_ref, dg_ref):
        xv = x_ref[...]
        gv = g_ref[...]
        r = lax.rsqrt(jnp.mean(xv * xv, axis=-1, keepdims=True) + RMS_EPS)
        xhat = xv * r
        err = xhat * gv - t_ref[...]
        row_loss = jnp.mean(err * err, axis=-1, keepdims=True)
        lpart = 0.5 * jnp.sum(row_loss, axis=0, keepdims=True)
        dyv = err / D
        dxhat = dyv * gv
        mean_term = jnp.mean(dxhat * xhat, axis=-1, keepdims=True)
        dx = r * (dxhat - xhat * mean_term)
        dx_ref[...] = dx
        dxb_ref[...] = dx.astype(BF16)
        gpart = jnp.sum(dyv * xhat, axis=0, keepdims=True)

        @pl.when(pl.program_id(0) == 0)
        def _():
            dg_ref[...] = gpart
            loss_ref[...] = jnp.broadcast_to(lpart, loss_ref.shape)

        @pl.when(pl.program_id(0) > 0)
        def _():
            dg_ref[...] += gpart
            loss_ref[...] += jnp.broadcast_to(lpart, loss_ref.shape)

    row = pl.BlockSpec((tr, D), lambda i: (i, 0))
    gain = pl.BlockSpec((1, D), lambda i: (0, 0))
    lspec = pl.BlockSpec((8, LANES), lambda i: (0, 0))
    return pl.pallas_call(
        body, name=name, grid=(S // tr,),
        out_shape=[jax.ShapeDtypeStruct((8, LANES), F32), jax.ShapeDtypeStruct((S, D), F32),
                   jax.ShapeDtypeStruct((S, D), BF16), jax.ShapeDtypeStruct((1, D), F32)],
        in_specs=[row, row, gain], out_specs=[lspec, row, row, gain],
        compiler_params=_params("arbitrary"),
    )(x, target, g)


CONV_TR = 128
CONV_TC = D_FF
CONV_NJ = D_FF // CONV_TC
HALO = 16


def _causal_taps(cur_ref, prev_ref, first):
    xv = cur_ref[...].astype(F32)
    pv = prev_ref[...].astype(F32)
    p1 = jnp.where(first, 0.0, pv[HALO - 1:HALO, :])
    p2 = jnp.where(first, 0.0, pv[HALO - 2:HALO - 1, :])
    r1, r2 = pltpu.roll(xv, 1, 0), pltpu.roll(xv, 2, 0)
    row = lax.broadcasted_iota(jnp.int32, (8, xv.shape[1]), 0)
    xm1 = jnp.concatenate([jnp.where(row == 0, p1, r1[0:8]), r1[8:]], axis=0)
    xm2 = jnp.concatenate([jnp.where(row == 0, p2, jnp.where(row == 1, p1, r2[0:8])), r2[8:]], axis=0)
    return xv, xm1, xm2


def _conv_specs():
    def prev_row(i):
        return jnp.maximum(i * (CONV_TR // HALO) - 1, 0)
    ua = pl.BlockSpec((CONV_TR, CONV_TC), lambda i, j: (i, j))
    ug = pl.BlockSpec((CONV_TR, CONV_TC), lambda i, j: (i, j + CONV_NJ))
    pa = pl.BlockSpec((HALO, CONV_TC), lambda i, j: (prev_row(i), j))
    pg = pl.BlockSpec((HALO, CONV_TC), lambda i, j: (prev_row(i), j + CONV_NJ))
    wa = pl.BlockSpec((3, CONV_TC), lambda i, j: (0, j))
    wg = pl.BlockSpec((3, CONV_TC), lambda i, j: (0, j + CONV_NJ))
    ba = pl.BlockSpec((1, CONV_TC), lambda i, j: (0, j))
    bg = pl.BlockSpec((1, CONV_TC), lambda i, j: (0, j + CONV_NJ))
    return [ua, pa, ug, pg, wa, wg, ba, bg]


def _convgate_fwd(u, w, b, name):
    def body(ua, pa, ug, pg, wa, wg, ba, bg, o_ref):
        first = pl.program_id(0) == 0
        x0, x1, x2 = _causal_taps(ua, pa, first)
        ac = wa[0:1, :] * x2 + wa[1:2, :] * x1 + wa[2:3, :] * x0 + ba[...]
        x0, x1, x2 = _causal_taps(ug, pg, first)
        gc = wg[0:1, :] * x2 + wg[1:2, :] * x1 + wg[2:3, :] * x0 + bg[...]
        sg = 1.0 / (1.0 + jnp.exp(-gc))
        o_ref[...] = (gc * sg * ac).astype(BF16)

    return pl.pallas_call(
        body, name=name, grid=(S // CONV_TR, CONV_NJ),
        out_shape=jax.ShapeDtypeStruct((S, D_FF), BF16),
        in_specs=_conv_specs(),
        out_specs=pl.BlockSpec((CONV_TR, CONV_TC), lambda i, j: (i, j)),
        compiler_params=_params("parallel", "parallel"),
    )(u, u, u, u, w, w, b, b)


def _convgate_bwd(u, w, b, dact, name):
    def body(ua, pa, ug, pg, wa, wg, ba, bg, d_ref, da_ref, dg_ref, dwa_ref, dwg_ref, dba_ref, dbg_ref):
        i = pl.program_id(1)
        first = i == 0
        a0, a1, a2 = _causal_taps(ua, pa, first)
        ac = wa[0:1, :] * a2 + wa[1:2, :] * a1 + wa[2:3, :] * a0 + ba[...]
        g0, g1, g2 = _causal_taps(ug, pg, first)
        gc = wg[0:1, :] * g2 + wg[1:2, :] * g1 + wg[2:3, :] * g0 + bg[...]
        sg = 1.0 / (1.0 + jnp.exp(-gc))
        dact_v = d_ref[...].astype(F32)
        da = dact_v * (gc * sg)
        dg = dact_v * ac * (sg * (1.0 + gc * (1.0 - sg)))
        da_ref[...] = da.astype(BF16)
        dg_ref[...] = dg.astype(BF16)

        def col(v):
            return jnp.sum(v, axis=0, keepdims=True)

        parts = [col(da * a2), col(da * a1), col(da * a0), col(dg * g2), col(dg * g1), col(dg * g0),
                 col(da), col(dg)]

        @pl.when(first)
        def _():
            for k in range(3):
                dwa_ref[k:k + 1, :] = parts[k]
                dwg_ref[k:k + 1, :] = parts[3 + k]
            dba_ref[...] = parts[6]
            dbg_ref[...] = parts[7]

        @pl.when(i > 0)
        def _():
            for k in range(3):
                dwa_ref[k:k + 1, :] += parts[k]
                dwg_ref[k:k + 1, :] += parts[3 + k]
            dba_ref[...] += parts[6]
            dbg_ref[...] += parts[7]

    def swap(spec):
        return pl.BlockSpec(spec.block_shape, lambda j, i, f=spec.index_map: f(i, j))

    blk = pl.BlockSpec((CONV_TR, CONV_TC), lambda j, i: (i, j))
    w3 = pl.BlockSpec((3, CONV_TC), lambda j, i: (0, j))
    b1 = pl.BlockSpec((1, CONV_TC), lambda j, i: (0, j))
    return pl.pallas_call(
        body, name=name, grid=(CONV_NJ, S // CONV_TR),
        out_shape=[jax.ShapeDtypeStruct((S, D_FF), BF16), jax.ShapeDtypeStruct((S, D_FF), BF16),
                   jax.ShapeDtypeStruct((3, D_FF), F32), jax.ShapeDtypeStruct((3, D_FF), F32),
                   jax.ShapeDtypeStruct((1, D_FF), F32), jax.ShapeDtypeStruct((1, D_FF), F32)],
        in_specs=[swap(s) for s in _conv_specs()] + [blk],
        out_specs=[blk, blk, w3, w3, b1, b1],
        compiler_params=_params("parallel", "arbitrary"),
    )(u, u, u, u, w, w, b, b, dact)


def _conv_input_bwd(d, w, name):
    n_i = S // CONV_TR

    def body(d_ref, n_ref, w_ref, o_ref):
        last = pl.program_id(0) == n_i - 1
        dv = d_ref[...].astype(F32)
        nv = n_ref[...].astype(F32)
        n1 = jnp.where(last, 0.0, nv[0:1, :])
        n2 = jnp.where(last, 0.0, nv[1:2, :])
        r1, r2 = pltpu.roll(dv, CONV_TR - 1, 0), pltpu.roll(dv, CONV_TR - 2, 0)
        row = lax.broadcasted_iota(jnp.int32, (8, dv.shape[1]), 0)
        cut = CONV_TR - 8
        dp1 = jnp.concatenate([r1[:cut], jnp.where(row == 7, n1, r1[cut:])], axis=0)
        dp2 = jnp.concatenate([r2[:cut], jnp.where(row == 7, n2, jnp.where(row == 6, n1, r2[cut:]))], axis=0)
        o_ref[...] = (w_ref[2:3, :] * dv + w_ref[1:2, :] * dp1 + w_ref[0:1, :] * dp2).astype(BF16)

    def next_row(i):
        return jnp.minimum((i + 1) * (CONV_TR // HALO), S // HALO - 1)

    blk = pl.BlockSpec((CONV_TR, CONV_TC), lambda i, j: (i, j))
    return pl.pallas_call(
        body, name=name, grid=(n_i, CONV_NJ),
        out_shape=jax.ShapeDtypeStruct((S, D_FF), BF16),
        in_specs=[blk, pl.BlockSpec((HALO, CONV_TC), lambda i, j: (next_row(i), j)),
                  pl.BlockSpec((3, CONV_TC), lambda i, j: (0, j))],
        out_specs=blk,
        compiler_params=_params("parallel", "parallel"),
    )(d, d, w)


FOX_T = 256
FOX_TQ, FOX_TK = 256, 256
N_PAIRS = A_HEADS // 2


def _lane_masks():
    lane = lax.broadcasted_iota(jnp.int32, (1, LANES), 1)
    return lane, (lane < HEAD_DIM, lane >= HEAD_DIM)


def _fox_prep_fwd(z_t, b, name):
    def body(z_ref, b_ref, c_ref):
        r = lax.broadcasted_iota(jnp.int32, (LANES, LANES), 0)
        cc = lax.broadcasted_iota(jnp.int32, (LANES, LANES), 1)
        upper = (r <= cc).astype(BF16)
        carry = jnp.zeros((A_HEADS, 1), F32)
        for blk in range(S // LANES):
            sl = slice(blk * LANES, (blk + 1) * LANES)
            z = z_ref[:, sl] + b_ref[...]
            lf = jnp.minimum(z, 0.0) - jnp.log(1.0 + jnp.exp(-jnp.abs(z)))
            cs = _split_dot(lf, upper, 3) + carry
            c_ref[:, sl] = cs
            carry = cs[:, LANES - 1:LANES]

    return pl.pallas_call(
        body, name=name, out_shape=jax.ShapeDtypeStruct((A_HEADS, S), F32),
        compiler_params=_params(),
    )(z_t, b)


def _fox_prep_bwd(drow_t, dcol_t, z_t, b, name):
    def body(dr_ref, dc_ref, z_ref, b_ref, dz_ref, db_ref):
        r = lax.broadcasted_iota(jnp.int32, (LANES, LANES), 0)
        cc = lax.broadcasted_iota(jnp.int32, (LANES, LANES), 1)
        lower = (r >= cc).astype(BF16)
        carry = jnp.zeros((A_HEADS, 1), F32)
        db = jnp.zeros((A_HEADS, 1), F32)
        for blk in reversed(range(S // LANES)):
            sl = slice(blk * LANES, (blk + 1) * LANES)
            rc = _split_dot(dr_ref[:, sl] - dc_ref[:, sl], lower, 3) + carry
            carry = rc[:, 0:1]
            z = z_ref[:, sl] + b_ref[...]
            dz = rc / (1.0 + jnp.exp(z))
            dz_ref[:, sl] = dz
            db = db + jnp.sum(dz, axis=1, keepdims=True)
        db_ref[...] = db

    return pl.pallas_call(
        body, name=name,
        out_shape=[jax.ShapeDtypeStruct((A_HEADS, S), F32), jax.ShapeDtypeStruct((A_HEADS, 1), F32)],
        compiler_params=_params(),
    )(drow_t, dcol_t, z_t, b)


def _fox_fwd(qkv, c_t2, name, gather):
    tq, tk = FOX_TQ, FOX_TK

    n = len(gather)

    def body(*refs):
        q_ref, k_ref, v_ref, ct_ref = refs[:4]
        o_ref, lse_ref = refs[4 + n:6 + n]
        exchange = (refs[4:4 + n], refs[6 + n:6 + 2 * n], refs[6 + 2 * n:], False)
        qi = pl.program_id(1)

        @pl.when(jnp.logical_and(pl.program_id(0) == 0, qi == 0))
        def _():
            for cp in _exchange_copies(*exchange):
                cp.start()

        n_full =jnp.right_shift(qi, (tk // tq).bit_length() - 1)
        lane, masks = _lane_masks()
        q = q_ref[...] * SCALE
        qs = [jnp.where(masks[e], q, jnp.zeros_like(q)) for e in range(2)]

        def scores(j):
            start = pl.multiple_of(j * tk, tk)
            kb = k_ref[pl.ds(start, tk), :]
            return tuple(_dot(qs[e], kb, NT_DIMS) - ct_ref[e:e + 1, pl.ds(start, tk)] for e in range(2))

        def softmax(s, m, masked):
            if masked:
                rows = lax.broadcasted_iota(jnp.int32, (tq, tk), 0) + (qi * tq - n_full * tk)
                cols = lax.broadcasted_iota(jnp.int32, (tq, tk), 1)
                s = tuple(jnp.where(cols <= rows, s[e], NEG) for e in range(2))
            m_new = tuple(jnp.maximum(m[e], jnp.max(s[e], axis=1, keepdims=True)) for e in range(2))
            p = tuple(jnp.exp(s[e] - m_new[e]).astype(BF16) for e in range(2))
            alpha = tuple(jnp.exp(m[e] - m_new[e]) for e in range(2))
            return m_new, p, alpha

        def weighted_values(j, p, alpha, acc):
            start = pl.multiple_of(j * tk, tk)
            vb = v_ref[pl.ds(start, tk), :]
            return tuple(alpha[e] * acc[e] + _dot(p[e], jnp.where(masks[e], vb, jnp.ones_like(vb)))
                         for e in range(2))

        def step(j, carry):
            s, p_prev, a_prev, m, acc = carry
            s_next = scores(j + 1)
            acc = weighted_values(jnp.maximum(j - 1, 0), p_prev, a_prev, acc)
            m, p, alpha = softmax(s, m, False)
            return s_next, p, alpha, m, acc

        two = lambda x: (x, x)
        init = (scores(0), two(jnp.zeros((tq, tk), BF16)), two(jnp.ones((tq, 1), F32)),
                two(jnp.full((tq, 1), NEG, F32)), two(jnp.zeros((tq, LANES), F32)))
        s, p_prev, a_prev, m, acc = lax.fori_loop(0, n_full, step, init)
        acc = weighted_values(jnp.maximum(n_full - 1, 0), p_prev, a_prev, acc)
        (m0, m1), p, alpha = softmax(s, m, True)
        acc0, acc1 = weighted_values(n_full, p, alpha, acc)
        l0 = acc0[:, HEAD_DIM:HEAD_DIM + 1]
        l1 = acc1[:, 0:1]
        o_ref[...] = jnp.where(masks[0], acc0 / l0, acc1 / l1).astype(BF16)
        lse_ref[...] = jnp.where(masks[0], m0 + jnp.log(l0), m1 + jnp.log(l1))

        @pl.when(jnp.logical_and(pl.program_id(0) == N_PAIRS - 1, qi == S // tq - 1))
        def _():
            for cp in _exchange_copies(*exchange):
                cp.wait()

    qspec = pl.BlockSpec((tq, LANES), lambda h, i: (i, h))
    return pl.pallas_call(
        body, name=name, grid=(N_PAIRS, S // tq),
        out_shape=[jax.ShapeDtypeStruct((S, D), BF16), jax.ShapeDtypeStruct((S, D), F32)]
        + _exchange_shapes(gather, False),
        in_specs=[qspec,
                  pl.BlockSpec((S, LANES), lambda h, i: (0, N_PAIRS + h)),
                  pl.BlockSpec((S, LANES), lambda h, i: (0, 2 * N_PAIRS + h)),
                  pl.BlockSpec((None, 2, S), lambda h, i: (h, 0, 0))] + [ANY] * n,
        out_specs=[qspec, qspec] + [ANY] * n,
        scratch_shapes=_exchange_scratch(n),
        compiler_params=_params("arbitrary", "arbitrary"),
    )(qkv, qkv, qkv, c_t2, *gather)


def _head_rowsum(a, b, name, tr=256):
    C = a.shape[1]

    def body(a_ref, b_ref, o_ref):
        r = lax.broadcasted_iota(jnp.int32, (LANES, LANES), 0) < HEAD_DIM
        cc = lax.broadcasted_iota(jnp.int32, (LANES, LANES), 1) < HEAD_DIM
        same_head = (r == cc).astype(BF16)
        for blk in range(C // LANES):
            sl = slice(blk * LANES, (blk + 1) * LANES)
            prod = a_ref[:, sl].astype(F32) * b_ref[:, sl].astype(F32)
            o_ref[:, sl] = _split_dot(prod, same_head, 2)

    row = pl.BlockSpec((tr, C), lambda i: (i, 0))
    return pl.pallas_call(
        body, name=name, grid=(S // tr,), out_shape=jax.ShapeDtypeStruct((S, C), F32),
        in_specs=[row, row], out_specs=row, compiler_params=_params("parallel"),
    )(a, b)


def _fox_bwd(qkv, do, lse, delta, c_t2, name, scatter):
    t = FOX_T
    nq = S // t

    n = len(scatter)

    def body(*refs):
        q_ref, k_ref, v_ref, do_ref, lse_ref, dl_ref, ct_ref = refs[:7]
        dq_ref, dk_ref, dv_ref, dcol_ref, drow_ref = refs[7 + n:12 + n]
        exchange = (refs[7:7 + n], refs[12 + n:12 + 2 * n], refs[12 + 2 * n:], True)
        kj = pl.program_id(1)

        @pl.when(jnp.logical_and(pl.program_id(0) == 0, kj == 0))
        def _():
            for cp in _exchange_copies(*exchange):
                cp.start()

        @pl.when(kj == 0)
        def _():
            dq_ref[...] = jnp.zeros_like(dq_ref)
            drow_ref[...] = jnp.zeros_like(drow_ref)

        lane, masks = _lane_masks()
        k = k_ref[...]
        v = v_ref[...]
        k_aug = [jnp.where(masks[e], k * SCALE, jnp.ones_like(k)) for e in range(2)]
        cs = [ct_ref[e:e + 1, :] for e in range(2)]

        def rows_of(i):
            r0 = pl.multiple_of(i * t, t)
            return pl.ds(r0, t), q_ref[pl.ds(r0, t), :] * SCALE, do_ref[pl.ds(r0, t), :]

        def scores(i):
            _, qb, dob = rows_of(i)
            out = []
            for e in range(2):
                qe = jnp.where(masks[e], qb, jnp.zeros_like(qb))
                doe = jnp.where(masks[e], dob, jnp.zeros_like(dob))
                out.append((_dot(qe, k, NT_DIMS) - cs[e], _dot(doe, v, NT_DIMS)))
            return tuple(out)

        def pointwise(i, sd, masked):
            rows, _, _ = rows_of(i)
            out = []
            for e in range(2):
                lo = e * HEAD_DIM
                s, dp = sd[e]
                if masked:
                    r = lax.broadcasted_iota(jnp.int32, (t, t), 0)
                    c = lax.broadcasted_iota(jnp.int32, (t, t), 1)
                    s = jnp.where(c <= r, s, NEG)
                p = jnp.exp(s - lse_ref[rows, lo:lo + 1])
                out.append((p.astype(BF16), (p * (dp - dl_ref[rows, lo:lo + 1])).astype(BF16)))
            return tuple(out)

        def accumulate(i, pd, acc):
            rows, qb, dob = rows_of(i)
            dk_acc, dv_acc = list(acc[0]), acc[1]
            dq_parts = []
            for e in range(2):
                p, ds = pd[e]
                q_aug = jnp.where(masks[e], qb, jnp.ones_like(qb))
                doe = jnp.where(masks[e], dob, jnp.zeros_like(dob))
                dv_acc = dv_acc + _dot(p, doe, TN_DIMS)
                dk_acc[e] = dk_acc[e] + _dot(ds, q_aug, TN_DIMS)
                dq_parts.append(_dot(ds, k_aug[e]))
            dq_ref[rows, :] += jnp.where(masks[0], dq_parts[0], dq_parts[1])
            drow_ref[rows, :] += jnp.where(masks[0], dq_parts[1], dq_parts[0])
            return tuple(dk_acc), dv_acc

        def step(i, carry):
            pd_prev, acc = carry
            sd = scores(i)
            acc = accumulate(i - 1, pd_prev, acc)
            return pointwise(i, sd, False), acc

        zero = jnp.zeros((t, LANES), F32)
        carry = (pointwise(kj, scores(kj), True), ((zero, zero), zero))
        pd, acc = lax.fori_loop(kj + 1, nq, step, carry)
        (dk0, dk1), dv = accumulate(nq - 1, pd, acc)
        dk_ref[...] = jnp.where(masks[0], dk0, dk1).astype(BF16)
        dcol_ref[...] = jnp.where(masks[0], dk1, dk0)
        dv_ref[...] = dv.astype(BF16)

        @pl.when(jnp.logical_and(pl.program_id(0) == N_PAIRS - 1, kj == nq - 1))
        def _():
            for cp in _exchange_copies(*exchange):
                cp.wait()

    full = lambda off: pl.BlockSpec((S, LANES), lambda h, j, off=off: (0, off + h))
    kv = lambda off: pl.BlockSpec((t, LANES), lambda h, j, off=off: (j, off + h))
    return pl.pallas_call(
        body, name=name, grid=(N_PAIRS, nq),
        out_shape=[jax.ShapeDtypeStruct((S, D), F32), jax.ShapeDtypeStruct((S, D), BF16),
                   jax.ShapeDtypeStruct((S, D), BF16), jax.ShapeDtypeStruct((S, D), F32),
                   jax.ShapeDtypeStruct((S, D), F32)] + _exchange_shapes(scatter, True),
        in_specs=[full(0), kv(N_PAIRS), kv(2 * N_PAIRS), full(0), full(0), full(0),
                  pl.BlockSpec((None, 2, t), lambda h, j: (h, 0, j))] + [ANY] * n,
        out_specs=[full(0), kv(0), kv(0), kv(0), full(0)] + [ANY] * n,
        scratch_shapes=_exchange_scratch(n),
        compiler_params=_params("arbitrary", "arbitrary"),
    )(qkv, qkv, qkv, do, lse, delta, c_t2, *scatter)


B_PAIRS = 4
B_NB = S // B_W


def _group_consts(g):
    nbs = jnp.where(g == 0, B_NB // B_DILS[0], jnp.where(g == 1, B_NB // B_DILS[1], B_NB // B_DILS[2]))
    dil = jnp.where(g == 0, B_DILS[0], jnp.where(g == 1, B_DILS[1], B_DILS[2]))
    return nbs, dil


def _band(dil):
    qi = lax.broadcasted_iota(jnp.int32, (B_W, B_W), 0)
    kj = lax.broadcasted_iota(jnp.int32, (B_W, B_W), 1)
    dist_c = qi - kj
    dist_p = qi + B_W - kj
    return (dist_c * dil).astype(F32), dist_c >= 0, (dist_p * dil).astype(F32), dist_p <= B_W


def _dil_fwd(qp, kp, vp, slopes, name):
    def body(sl_ref, q_ref, kp_ref, kc_ref, vp_ref, vc_ref, o_ref, lse_ref):
        g, n = pl.program_id(0), pl.program_id(1)
        nbs, dil = _group_consts(g)
        has_prev = (n % nbs) != 0
        lane, masks = _lane_masks()
        bias_c, ok_c, bias_p, ok_p = _band(dil)
        ok_p = jnp.logical_and(ok_p, has_prev)
        for hp in range(B_PAIRS):
            cols = slice(hp * LANES, (hp + 1) * LANES)
            q = q_ref[:, cols] * SCALE
            kc, kpv, vc, vpv = kc_ref[:, cols], kp_ref[:, cols], vc_ref[:, cols], vp_ref[:, cols]
            outs, lses = [], []
            for e in range(2):
                slope = sl_ref[g * 8 + 2 * hp + e]
                qe = jnp.where(masks[e], q, jnp.zeros_like(q))
                sc = jnp.where(ok_c, _dot(qe, kc, NT_DIMS) - slope * bias_c, NEG)
                sp = jnp.where(ok_p, _dot(qe, kpv, NT_DIMS) - slope * bias_p, NEG)
                m = jnp.maximum(jnp.max(sc, axis=1, keepdims=True), jnp.max(sp, axis=1, keepdims=True))
                pc = jnp.exp(sc - m).astype(BF16)
                pp = jnp.exp(sp - m).astype(BF16)
                acc = (_dot(pc, jnp.where(masks[e], vc, jnp.ones_like(vc)))
                       + _dot(pp, jnp.where(masks[e], vpv, jnp.ones_like(vpv))))
                l = acc[:, HEAD_DIM:HEAD_DIM + 1] if e == 0 else acc[:, 0:1]
                outs.append(acc / l)
                lses.append(m + jnp.log(l))
            o_ref[:, cols] = jnp.where(masks[0], outs[0], outs[1])
            lse_ref[:, cols] = jnp.where(masks[0], lses[0], lses[1])

    cur = pl.BlockSpec((None, B_W, B_OUT), lambda g, n, sl: (g, n, 0))
    prev = pl.BlockSpec((None, B_W, B_OUT), lambda g, n, sl: (g, jnp.maximum(n - 1, 0), 0))
    return pl.pallas_call(
        body, name=name,
        grid_spec=pltpu.PrefetchScalarGridSpec(
            num_scalar_prefetch=1, grid=(3, B_NB),
            in_specs=[cur, prev, cur, prev, cur], out_specs=[cur, cur]),
        out_shape=[jax.ShapeDtypeStruct((3, S, B_OUT), F32), jax.ShapeDtypeStruct((3, S, B_OUT), F32)],
        compiler_params=_params("parallel", "parallel"),
    )(slopes, qp, kp, kp, vp, vp)


def _dil_merge(og, lseg, name, tr=256):
    def body(o_ref, l_ref, out_ref, lse_ref):
        l0, l1, l2 = l_ref[0], l_ref[1], l_ref[2]
        m = jnp.maximum(jnp.maximum(l0, l1), l2)
        w0, w1, w2 = jnp.exp(l0 - m), jnp.exp(l1 - m), jnp.exp(l2 - m)
        den = w0 + w1 + w2
        out_ref[...] = ((w0 * o_ref[0] + w1 * o_ref[1] + w2 * o_ref[2]) / den).astype(BF16)
        lse_ref[...] = m + jnp.log(den)

    blk3 = pl.BlockSpec((3, tr, B_OUT), lambda i: (0, i, 0))
    blk = pl.BlockSpec((tr, B_OUT), lambda i: (i, 0))
    return pl.pallas_call(
        body, name=name, grid=(S // tr,),
        out_shape=[jax.ShapeDtypeStruct((S, B_OUT), BF16), jax.ShapeDtypeStruct((S, B_OUT), F32)],
        in_specs=[blk3, blk3], out_specs=[blk, blk], compiler_params=_params("parallel"),
    )(og, lseg)


def _dil_bwd(qp, kp, vp, dop, lsep, dlp, slopes, name, scatter):
    n_ex = len(scatter)

    def body(sl_ref, *refs):
        (qc_ref, qn_ref, kp_ref, kc_ref, vp_ref, vc_ref, doc_ref, don_ref,
         lc_ref, ln_ref, dc_ref, dn_ref) = refs[:12]
        dq_ref, dk_ref, dv_ref = refs[12 + n_ex:15 + n_ex]
        exchange = (refs[12:12 + n_ex], refs[15 + n_ex:15 + 2 * n_ex], refs[15 + 2 * n_ex:], True)
        g, n = pl.program_id(0), pl.program_id(1)

        @pl.when(jnp.logical_and(g == 0, n == 0))
        def _():
            for cp in _exchange_copies(*exchange):
                cp.start()

        nbs, dil = _group_consts(g)
        has_prev = (n % nbs) != 0
        has_next = jnp.logical_and(n + 1 < B_NB, ((n + 1) % nbs) != 0)
        lane, masks = _lane_masks()
        bias_c, ok_c, bias_p, ok_p = _band(dil)
        ok_pp = jnp.logical_and(ok_p, has_prev)
        ok_np = jnp.logical_and(ok_p, has_next)
        for hp in range(B_PAIRS):
            cols = slice(hp * LANES, (hp + 1) * LANES)
            qc, qn = qc_ref[:, cols] * SCALE, qn_ref[:, cols] * SCALE
            kc, kpv, vc, vpv = kc_ref[:, cols], kp_ref[:, cols], vc_ref[:, cols], vp_ref[:, cols]
            doc, don = doc_ref[:, cols], don_ref[:, cols]
            dq = jnp.zeros((B_W, LANES), F32)
            dk = jnp.zeros((B_W, LANES), F32)
            dv = jnp.zeros((B_W, LANES), F32)
            for e in range(2):
                lo = hp * LANES + e * HEAD_DIM
                slope = sl_ref[g * 8 + 2 * hp + e]
                qce = jnp.where(masks[e], qc, jnp.zeros_like(qc))
                qne = jnp.where(masks[e], qn, jnp.zeros_like(qn))
                doce = jnp.where(masks[e], doc, jnp.zeros_like(doc))
                done = jnp.where(masks[e], don, jnp.zeros_like(don))
                kce = jnp.where(masks[e], kc * SCALE, jnp.zeros_like(kc))
                kpe = jnp.where(masks[e], kpv * SCALE, jnp.zeros_like(kpv))
                lse_c, dl_c = lc_ref[:, lo:lo + 1], dc_ref[:, lo:lo + 1]
                lse_n, dl_n = ln_ref[:, lo:lo + 1], dn_ref[:, lo:lo + 1]
                s = jnp.where(ok_c, _dot(qce, kc, NT_DIMS) - slope * bias_c, NEG)
                p = jnp.exp(s - lse_c)
                dsb = (p * (_dot(doce, vc, NT_DIMS) - dl_c)).astype(BF16)
                dq = dq + _dot(dsb, kce)
                dk = dk + _dot(dsb, qce, TN_DIMS)
                dv = dv + _dot(p.astype(BF16), doce, TN_DIMS)
                s = jnp.where(ok_pp, _dot(qce, kpv, NT_DIMS) - slope * bias_p, NEG)
                p = jnp.exp(s - lse_c)
                dsb = (p * (_dot(doce, vpv, NT_DIMS) - dl_c)).astype(BF16)
                dq = dq + _dot(dsb, kpe)
                s = jnp.where(ok_np, _dot(qne, kc, NT_DIMS) - slope * bias_p, NEG)
                p = jnp.exp(s - lse_n)
                dsb = (p * (_dot(done, vc, NT_DIMS) - dl_n)).astype(BF16)
                dk = dk + _dot(dsb, qne, TN_DIMS)
                dv = dv + _dot(p.astype(BF16), done, TN_DIMS)
            dq_ref[:, cols] = dq.astype(BF16)
            dk_ref[:, cols] = dk.astype(BF16)
            dv_ref[:, cols] = dv.astype(BF16)

        @pl.when(jnp.logical_and(g == 2, n == B_NB - 1))
        def _():
            for cp in _exchange_copies(*exchange):
                cp.wait()

    cur = pl.BlockSpec((None, B_W, B_OUT), lambda g, n, sl: (g, n, 0))
    prev = pl.BlockSpec((None, B_W, B_OUT), lambda g, n, sl: (g, jnp.maximum(n - 1, 0), 0))
    nxt = pl.BlockSpec((None, B_W, B_OUT), lambda g, n, sl: (g, jnp.minimum(n + 1, B_NB - 1), 0))
    return pl.pallas_call(
        body, name=name,
        grid_spec=pltpu.PrefetchScalarGridSpec(
            num_scalar_prefetch=1, grid=(3, B_NB),
            in_specs=[cur, nxt, prev, cur, prev, cur, cur, nxt, cur, nxt, cur, nxt] + [ANY] * n_ex,
            out_specs=[cur, cur, cur] + [ANY] * n_ex,
            scratch_shapes=_exchange_scratch(n_ex)),
        out_shape=[jax.ShapeDtypeStruct((3, S, B_OUT), BF16)] * 3 + _exchange_shapes(scatter, True),
        compiler_params=_params("arbitrary", "arbitrary"),
    )(slopes, qp, qp, kp, kp, vp, vp, dop, dop, lsep, lsep, dlp, dlp, *scatter)


def _rows_block(shape, max_bytes=2 * 1024 * 1024):
    rows, cols = shape
    padded_cols = -(-cols // LANES) * LANES
    for tr in (1024, 512, 256, 128, 64, 32, 16):
        if rows % tr == 0 and tr * padded_cols * 4 <= max_bytes:
            return tr
    return rows


def _adam_update(w, m, v, g):
    m_new = ADAM_B1 * m + (1.0 - ADAM_B1) * g
    v_new = ADAM_B2 * v + (1.0 - ADAM_B2) * (g * g)
    m_hat = m_new / (1.0 - ADAM_B1 ** ADAM_STEP)
    v_hat = v_new / (1.0 - ADAM_B2 ** ADAM_STEP)
    delta = -ADAM_LR * (m_hat / (jnp.sqrt(v_hat) + ADAM_EPS) + ADAM_WD * w)
    return delta, m_new, v_new


def _adamw_sharded(w, m, v, parts, name):
    R, C = w.shape
    tr = _rows_block((R, C), max_bytes=1024 * 1024)

    def body(w_ref, m_ref, v_ref, p_ref, g_ref, d_ref, mo_ref, vo_ref):
        g = p_ref[0].astype(F32)
        for dev in range(1, N_DEV):
            g = g + p_ref[dev].astype(F32)
        g_ref[...] = g
        d_ref[...], mo_ref[...], vo_ref[...] = _adam_update(w_ref[...], m_ref[...], v_ref[...], g)

    blk = pl.BlockSpec((tr, C), lambda i: (i, 0))
    out = jax.ShapeDtypeStruct((R, C), F32)
    return pl.pallas_call(
        body, name=name, grid=(R // tr,),
        in_specs=[blk, blk, blk, pl.BlockSpec((N_DEV, tr, C), lambda i: (0, i, 0))],
        out_specs=[blk, blk, blk, blk], out_shape=[out, out, out, out],
        compiler_params=_params("parallel"),
    )(w, m, v, parts)


def _adamw_replicated(w, m, v, parts, name):
    def body(w_ref, m_ref, v_ref, p_ref, g_ref, d_ref, mo_ref, vo_ref):
        g = p_ref[0]
        for dev in range(1, N_DEV):
            g = g + p_ref[dev]
        g_ref[...] = g
        d_ref[...], mo_ref[...], vo_ref[...] = _adam_update(w_ref[...], m_ref[...], v_ref[...], g)

    out = jax.ShapeDtypeStruct(w.shape, F32)
    return pl.pallas_call(body, name=name, out_shape=[out, out, out, out], compiler_params=_params())(w, m, v, parts)


def _cols_from_slots(g):
    return g.transpose(1, 0, 2).reshape(g.shape[1], N_DEV * g.shape[2])


def _cols_to_slots(w):
    k, n = w.shape
    return w.reshape(k, N_DEV, n // N_DEV).transpose(1, 0, 2)


def _permute(t, dil):
    c = t.shape[1]
    return t.reshape(S // dil, dil, c).transpose(1, 0, 2).reshape(S, c)


def _unpermute(t, dil):
    c = t.shape[1]
    return t.reshape(dil, S // dil, c).transpose(1, 0, 2).reshape(S, c)


def _group_permute(t):
    return jnp.stack([_permute(t[:, g * B_OUT:(g + 1) * B_OUT], B_DILS[g]) for g in range(3)])


def _same_permute(t):
    return jnp.stack([_permute(t, d) for d in B_DILS])


def _group_unpermute(t):
    return jnp.stack([_unpermute(t[g], B_DILS[g]) for g in range(3)])


SMALL_ROWS = 144


def _pack_small(a_b_f, kv_g, mix_g, ffn_g, conv_b, fin_g):
    flat = jnp.concatenate([a_b_f.reshape(-1), kv_g.reshape(-1), mix_g.reshape(-1), ffn_g.reshape(-1),
                            conv_b.reshape(-1), fin_g.reshape(-1)])
    return jnp.pad(flat, (0, SMALL_ROWS * LANES - flat.shape[0])).reshape(SMALL_ROWS, LANES)


def _unpack_small(p):
    flat = p.reshape(-1)
    out, off = [], 0
    for shape in ((1, A_HEADS), (D,), (2, D), (2, D), (2, 2 * D_FF), (D,)):
        size = math.prod(shape)
        out.append(flat[off:off + size].reshape(shape))
        off += size
    return out


def _unpack_late(g):
    w_up = g[4].reshape(N_DEV, 2, D, -1).transpose(1, 2, 0, 3).reshape(2, D, 2 * D_FF)
    w_down = g[5].reshape(N_DEV, 2, -1, D).transpose(1, 0, 2, 3).reshape(2, D_FF, D)
    conv_w = g[6].reshape(N_DEV, 2, 3, -1).transpose(1, 2, 0, 3).reshape(2, 3, 2 * D_FF)
    return (g[0].reshape(D, D), _cols_from_slots(g[1]), _cols_from_slots(g[2]), _cols_from_slots(g[3]),
            w_up, w_down, conv_w)


def _ffn_slots(dw_up, dw_down, dconv_w):
    return [_cols_to_slots(dw_up), dw_down.reshape(N_DEV, -1, D), _cols_to_slots(dconv_w)]


def _local_step(x0, target, w_in_pad, late_shards,
                a_b_f, kv_norm_g, mix_norm_g, ffn_norm_g, ffn_conv_b, final_norm_g):
    w_qkv, w_f = w_in_pad[:, :A_QKV], w_in_pad[:, A_QKV:]
    conv_b = ffn_conv_b.reshape(2, 1, 2 * D_FF)
    slopes = jnp.exp2(-8.0 * jnp.arange(1, 25, dtype=F32) / 24)

    def gain(g):
        return g.reshape(1, D)

    (h1,) = _rmsnorm_fwd(x0, [gain(mix_norm_g[0])], "norm_mix0")
    qkv = _matmul(h1, w_qkv, mode="nn", out_dtype=BF16, name="proj_qkv", tm=512, tn=A_QKV)
    z = _matmul(h1, w_f, mode="nn", out_dtype=F32, name="proj_gate", tm=S, tn=LANES)
    z_t = z[:, :A_HEADS].T
    b_f = a_b_f.reshape(A_HEADS, 1)
    c_t = _fox_prep_fwd(z_t, b_f, "fox_prep")
    c_t2 = c_t.reshape(N_PAIRS, 2, S)
    o_a, lse_a, *late = _fox_fwd(qkv, c_t2, "fox_fwd", late_shards)
    w_out, w_q, w_bo, w_kvf, w_up, w_down, conv_w = _unpack_late(late)
    x1 = _matmul(o_a, w_out, mode="nn", out_dtype=F32, name="a_out", tm=512, tn=D, res=x0)

    def ffn_fwd(xin, layer):
        (h,) = _rmsnorm_fwd(xin, [gain(ffn_norm_g[layer])], f"norm_ffn{layer}")
        u = _matmul(h, w_up[layer], mode="nn", out_dtype=BF16, name=f"ffn_up{layer}", tm=512, tn=2 * D_FF)
        act = _convgate_fwd(u, conv_w[layer], conv_b[layer], f"convgate{layer}")
        xout = _matmul(act, w_down[layer], mode="nn", out_dtype=F32, name=f"ffn_down{layer}", tm=512, tn=D, res=xin)
        return h, u, act, xout

    h2, u0, act0, x2 = ffn_fwd(x1, 0)
    hk, h3 = _rmsnorm_fwd(x2, [gain(kv_norm_g), gain(mix_norm_g[1])], "norm_kv_mix1")
    kv = _matmul(hk, w_kvf, mode="nn", out_dtype=BF16, name="proj_kv", tm=512, tn=B_KV)
    qb = _matmul(h3, w_q, mode="nn", out_dtype=BF16, name="proj_qb", tm=512, tn=B_Q)
    qp, kp, vp = _group_permute(qb), _group_permute(kv[:, :B_Q]), _group_permute(kv[:, B_Q:])
    og_p, lseg_p = _dil_fwd(qp, kp, vp, slopes, "dil_fwd")
    o_b, lse_b = _dil_merge(_group_unpermute(og_p), _group_unpermute(lseg_p), "dil_merge")
    x3 = _matmul(o_b, w_bo, mode="nn", out_dtype=F32, name="b_out", tm=512, tn=D, res=x2)
    h4, u1, act1, x4 = ffn_fwd(x3, 1)
    loss_blk, dx4, dx4b, dg_final = _final_loss(x4, target, gain(final_norm_g), "final_loss")

    def ffn_bwd(dx, dxb, xin, h, u, act, layer):
        dact = _matmul(dxb, w_down[layer], mode="nt", out_dtype=BF16, name=f"d_act{layer}", tm=512, tn=D_FF)
        dw_down = _matmul_tn(act, dxb, out_dtype=BF16, name=f"dw_down{layer}")
        da, dg, dwa, dwg, dba, dbg = _convgate_bwd(u, conv_w[layer], conv_b[layer], dact, f"convgate_bwd{layer}")
        du_a = _conv_input_bwd(da, conv_w[layer][:, :D_FF], f"conv_in_bwd_a{layer}")
        du_g = _conv_input_bwd(dg, conv_w[layer][:, D_FF:], f"conv_in_bwd_g{layer}")
        dw_up = jnp.concatenate(
            [_matmul_tn(h, du_a, out_dtype=BF16, name=f"dw_up_a{layer}"),
             _matmul_tn(h, du_g, out_dtype=BF16, name=f"dw_up_g{layer}")], axis=1)
        dh = _matmul(du_a, w_up[layer][:, :D_FF], mode="nt", out_dtype=F32, name=f"dh_ffn_a{layer}", tm=512, tn=D)
        dh = _matmul(du_g, w_up[layer][:, D_FF:], mode="nt", out_dtype=F32, name=f"dh_ffn_g{layer}", tm=512, tn=D,
                     res=dh)
        dxin, dxinb, dgain = _rmsnorm_bwd(xin, dh, gain(ffn_norm_g[layer]), dx, f"norm_ffn_bwd{layer}")
        dconv_w = jnp.concatenate([dwa, dwg], axis=1)
        dconv_b = jnp.concatenate([dba, dbg], axis=1)
        return dxin, dxinb, dgain, dw_up, dw_down, dconv_w, dconv_b

    dx3, dx3b, dg_ffn1, dw_up1, dw_down1, dconv_w1, dconv_b1 = ffn_bwd(dx4, dx4b, x3, h4, u1, act1, 1)

    do_b = _matmul(dx3b, w_bo, mode="nt", out_dtype=BF16, name="d_ob", tm=1024, tn=B_OUT)
    dw_bo = _matmul_tn(o_b, dx3b, out_dtype=BF16, name="dw_bo")
    dl_b = _head_rowsum(do_b, o_b, "delta_b")
    dqp, dkp, dvp, *land_ffn1 = _dil_bwd(qp, kp, vp, _same_permute(do_b), _same_permute(lse_b), _same_permute(dl_b),
                                         slopes, "dil_bwd", _ffn_slots(dw_up1, dw_down1, dconv_w1))

    def natural(tp):
        return jnp.concatenate([_unpermute(tp[g], B_DILS[g]) for g in range(3)], axis=1)

    dqb = natural(dqp)
    dkv = jnp.concatenate([natural(dkp), natural(dvp)], axis=1)
    dw_q = _matmul_tn(h3, dqb, out_dtype=BF16, name="dw_q")
    dw_kv = _matmul_tn(hk, dkv, out_dtype=BF16, name="dw_kv")
    dh3 = _matmul(dqb, w_q, mode="nt", out_dtype=F32, name="dh_mix1", tm=512, tn=D)
    dhk = _matmul(dkv, w_kvf, mode="nt", out_dtype=F32, name="dh_kv", tm=512, tn=D)
    dx2, _, dg_mix1 = _rmsnorm_bwd(x2, dh3, gain(mix_norm_g[1]), dx3, "norm_mix1_bwd")
    dx2, dx2b, dg_kv = _rmsnorm_bwd(x2, dhk, gain(kv_norm_g), dx2, "norm_kv_bwd")

    dx1, dx1b, dg_ffn0, dw_up0, dw_down0, dconv_w0, dconv_b0 = ffn_bwd(dx2, dx2b, x1, h2, u0, act0, 0)

    do_a = _matmul(dx1b, w_out, mode="nt", out_dtype=BF16, name="d_oa", tm=512, tn=D)
    dw_out = _matmul_tn(o_a, dx1b, out_dtype=BF16, name="dw_out")
    dl_a = _head_rowsum(do_a, o_a, "delta_a")
    dq_a, dk_a, dv_a, dcol, drow, *land = _fox_bwd(
        qkv, do_a, lse_a, dl_a, c_t2, "fox_bwd",
        [dw_out.reshape(N_DEV, D // N_DEV, D), _cols_to_slots(dw_q), _cols_to_slots(dw_bo), _cols_to_slots(dw_kv)]
        + _ffn_slots(dw_up0, dw_down0, dconv_w0))
    land_out, land_q, land_bo, land_kv, *land_ffn0 = land

    def head_sums(t):
        return t.reshape(S, N_PAIRS, 2, HEAD_DIM)[:, :, ::-1, 0].reshape(S, A_HEADS).T

    dz_t, db_f = _fox_prep_bwd(head_sums(drow), head_sums(dcol), z_t, b_f, "fox_prep_bwd")
    dz = jnp.pad(dz_t.T, ((0, 0), (0, LANES - A_HEADS))).astype(BF16)
    dproj = jnp.concatenate([dq_a.astype(BF16), dk_a, dv_a, dz], axis=1)
    dw_in = _matmul_tn(h1, dproj, out_dtype=BF16, name="dw_in")
    dh1 = _matmul(dproj, w_in_pad, mode="nt", out_dtype=F32, name="dh_mix0", tm=512, tn=D)
    grad_x, _, dg_mix0 = _rmsnorm_bwd(x0, dh1, gain(mix_norm_g[0]), dx1, "norm_mix0_bwd")

    dg_mix = jnp.concatenate([dg_mix0, dg_mix1], axis=0)
    dg_ffn = jnp.concatenate([dg_ffn0, dg_ffn1], axis=0)
    dconv_b = jnp.concatenate([dconv_b0, dconv_b1], axis=0)
    small_part = _pack_small(db_f, dg_kv, dg_mix, dg_ffn, dconv_b, dg_final)
    (land_in,), (small_parts,) = _final_exchange(
        [_cols_to_slots(dw_in[:, :A_QKV + A_HEADS])], [small_part], "exchange_last_grads")
    landed = [land_in, land_out, land_q, land_bo, land_kv, land_ffn0[0], land_ffn1[0], land_ffn0[1], land_ffn1[1],
              land_ffn0[2], land_ffn1[2]]
    return loss_blk, grad_x, landed, small_parts


def kernel(x, a_w_in, a_b_f, a_w_out, b_w_q, b_w_out, kv_norm_g, w_kv, mix_norm_g, ffn_norm_g, ffn_w_up, ffn_conv_w, ffn_conv_b, ffn_w_down, final_norm_g, loss_target, m_a_w_in, m_a_b_f, m_a_w_out, m_b_w_q, m_b_w_out, m_kv_norm_g, m_w_kv, m_mix_norm_g, m_ffn_norm_g, m_ffn_w_up, m_ffn_conv_w, m_ffn_conv_b, m_ffn_w_down, m_final_norm_g, v_a_w_in, v_a_b_f, v_a_w_out, v_b_w_q, v_b_w_out, v_kv_norm_g, v_w_kv, v_mix_norm_g, v_ffn_norm_g, v_ffn_w_up, v_ffn_conv_w, v_ffn_conv_b, v_ffn_w_down, v_final_norm_g):
    def shards(a_w_in, a_w_out, b_w_q, b_w_out, w_kv, ffn_w_up, ffn_w_down, ffn_conv_w):
        return [a_w_in[0], a_w_out[0], b_w_q[0], b_w_out[0], w_kv, ffn_w_up[0], ffn_w_up[1],
                ffn_w_down[0], ffn_w_down[1], ffn_conv_w[0], ffn_conv_w[1]]

    w_loc = shards(a_w_in, a_w_out, b_w_q, b_w_out, w_kv, ffn_w_up, ffn_w_down, ffn_conv_w)
    m_loc = shards(m_a_w_in, m_a_w_out, m_b_w_q, m_b_w_out, m_w_kv, m_ffn_w_up, m_ffn_w_down, m_ffn_conv_w)
    v_loc = shards(v_a_w_in, v_a_w_out, v_b_w_q, v_b_w_out, v_w_kv, v_ffn_w_up, v_ffn_w_down, v_ffn_conv_w)

    (g_in,) = _all_gather([a_w_in[0].astype(BF16)], "gather_a_w_in")
    w_in = _cols_from_slots(g_in)
    w_in_pad = jnp.pad(w_in, ((0, 0), (0, A_PROJ_PAD - w_in.shape[1])))
    late_shards = [a_w_out[0].astype(BF16), b_w_q[0].astype(BF16), b_w_out[0].astype(BF16), w_kv.astype(BF16),
                   ffn_w_up.reshape(2 * D, -1).astype(BF16), ffn_w_down.reshape(-1, D).astype(BF16),
                   ffn_conv_w.reshape(6, -1)]

    loss_blk, grad_x, landed, small_parts = _local_step(
        x[0], loss_target[0], w_in_pad, late_shards,
        a_b_f, kv_norm_g, mix_norm_g, ffn_norm_g, ffn_conv_b, final_norm_g)

    big = [_adamw_sharded(w_loc[k], m_loc[k], v_loc[k], landed[k], f"adamw{k}") for k in range(11)]

    small = _adamw_replicated(
        _pack_small(a_b_f, kv_norm_g, mix_norm_g, ffn_norm_g, ffn_conv_b, final_norm_g),
        _pack_small(m_a_b_f, m_kv_norm_g, m_mix_norm_g, m_ffn_norm_g, m_ffn_conv_b, m_final_norm_g),
        _pack_small(v_a_b_f, v_kv_norm_g, v_mix_norm_g, v_ffn_norm_g, v_ffn_conv_b, v_final_norm_g),
        small_parts, "adamw_small")

    loss = lax.psum(loss_blk[0, 0], ("x", "y", "c"))

    def assemble(kind):
        b = [r[kind] for r in big]
        s_abf, s_kv, s_mix, s_ffn, s_cb, s_fin = _unpack_small(small[kind])
        return [b[0][None], s_abf, b[1][None], b[2][None], b[3][None], s_kv, b[4], s_mix, s_ffn,
                jnp.stack([b[5], b[6]]), jnp.stack([b[9], b[10]]), s_cb, jnp.stack([b[7], b[8]]), s_fin]

    return (loss, grad_x[None], *assemble(0), *assemble(1), *assemble(2), *assemble(3))
```

```python
import functools
import math

import jax
import jax.numpy as jnp
from jax import lax
from jax.experimental import pallas as pl
from jax.experimental.pallas import tpu as pltpu

F32 = jnp.float32
BF16 = jnp.bfloat16

S = 4096
D = 1024
N_DEV = 8
A_HEADS = 16
HEAD_DIM = 64
A_QKV = 3072
A_PROJ_PAD = 3200
B_Q = 1536
B_OUT = 512
B_KV = 3072
B_W = 128
B_DILS = (1, 4, 16)
D_FF = 2816
RMS_EPS = 1e-6
SCALE = HEAD_DIM ** -0.5
NEG = -1e30

ADAM_LR = 0.001
ADAM_B1 = 0.9
ADAM_B2 = 0.999
ADAM_EPS = 1e-08
ADAM_WD = 0.01
ADAM_STEP = 10

LANES = 128
VMEM_LIMIT = 56 * 1024 * 1024
MESH = pl.DeviceIdType.MESH
ANY = pl.BlockSpec(memory_space=pl.ANY)

NT_DIMS = (((1,), (1,)), ((), ()))
TN_DIMS = (((0,), (0,)), ((), ()))
NN_DIMS = (((1,), (0,)), ((), ()))


def _params(*sem):
    return pltpu.CompilerParams(dimension_semantics=sem if sem else None, vmem_limit_bytes=VMEM_LIMIT)


def _dot(a, b, dims=NN_DIMS):
    return lax.dot_general(a, b, dims, preferred_element_type=F32)


def _split_dot(x, mat, pieces):
    out = None
    rem = x
    for _ in range(pieces):
        part = rem.astype(BF16)
        rem = rem - part.astype(F32)
        d = _dot(part, mat)
        out = d if out is None else out + d
    return out


def _pick(n, prefs):
    for p in prefs:
        if n % p == 0:
            return p
    return n


def _all_gather(arrays, name):
    n = len(arrays)

    def body(*refs):
        ins = refs[:n]
        outs = refs[n:2 * n]
        send_sems, recv_sems, local_sems = refs[2 * n:]
        x, y, c = lax.axis_index("x"), lax.axis_index("y"), lax.axis_index("c")
        me, sibling = (x, y, c), (x, y, 1 - c)
        chips = [(1 - x, y), (x, 1 - y), (1 - x, 1 - y)]

        def slot(a, px, py, pc):
            return outs[a].at[4 * px + 2 * py + pc]

        def copy(a, k, block, to, src=None):
            return pltpu.make_async_remote_copy(
                src_ref=slot(a, *block) if src is None else src, dst_ref=slot(a, *block),
                send_sem=send_sems.at[a, k], recv_sem=recv_sems.at[a, k],
                device_id=to, device_id_type=MESH)

        mine = [pltpu.make_async_copy(ins[a], slot(a, *me), local_sems.at[a]) for a in range(n)]
        for cp in mine:
            cp.start()
        first = []
        for a in range(n):
            first.append(copy(a, 0, me, sibling, src=ins[a]))
            first += [copy(a, 1 + j, me, (*chip, c), src=ins[a]) for j, chip in enumerate(chips)]
        for cp in first:
            cp.start()
        passed = []
        for j, chip in enumerate(chips):
            for a in range(n):
                copy(a, 1 + j, (*chip, c), me).wait_recv()
                fwd = copy(a, 4 + j, (*chip, c), sibling)
                fwd.start()
                passed.append(fwd)
        for a in range(n):
            copy(a, 0, sibling, me).wait_recv()
            for j, chip in enumerate(chips):
                copy(a, 4 + j, (*chip, 1 - c), me).wait_recv()
        for cp in first + passed:
            cp.wait_send()
        for cp in mine:
            cp.wait()

    return pl.pallas_call(
        body, name=name,
        out_shape=[jax.ShapeDtypeStruct((N_DEV,) + a.shape, a.dtype) for a in arrays],
        in_specs=[ANY] * n, out_specs=[ANY] * n,
        scratch_shapes=[pltpu.SemaphoreType.DMA((n, 7)), pltpu.SemaphoreType.DMA((n, 7)),
                        pltpu.SemaphoreType.DMA((n,))],
    )(*arrays)


PEER_FLIPS = [(dx, dy, dc) for dx in (0, 1) for dy in (0, 1) for dc in (0, 1) if (dx, dy, dc) != (0, 0, 0)]


def _exchange_copies(ins, outs, sems, scatter):
    if not ins:
        return []
    send_sems, recv_sems, local_sems = sems
    x, y, c = lax.axis_index("x"), lax.axis_index("y"), lax.axis_index("c")
    me = 4 * x + 2 * y + c
    copies = []
    for a in range(len(ins)):
        copies.append(pltpu.make_async_copy(ins[a].at[me] if scatter else ins[a], outs[a].at[me], local_sems.at[a]))
        for k, (dx, dy, dc) in enumerate(PEER_FLIPS):
            px, py, pc = (1 - x if dx else x), (1 - y if dy else y), (1 - c if dc else c)
            copies.append(pltpu.make_async_remote_copy(
                src_ref=ins[a].at[4 * px + 2 * py + pc] if scatter else ins[a], dst_ref=outs[a].at[me],
                send_sem=send_sems.at[a, k], recv_sem=recv_sems.at[a, k],
                device_id=(px, py, pc), device_id_type=MESH))
    return copies


def _exchange_scratch(n):
    if n == 0:
        return []
    return [pltpu.SemaphoreType.DMA((n, 7)), pltpu.SemaphoreType.DMA((n, 7)), pltpu.SemaphoreType.DMA((n,))]


def _exchange_shapes(arrays, scatter):
    return [jax.ShapeDtypeStruct((N_DEV,) + (a.shape[1:] if scatter else a.shape), a.dtype) for a in arrays]


def _final_exchange(scatter, gather, name):
    ns, ng = len(scatter), len(gather)

    def body(*refs):
        ins, outs, sems = refs[:ns + ng], refs[ns + ng:2 * (ns + ng)], refs[2 * (ns + ng):]
        copies = (_exchange_copies(ins[:ns], outs[:ns], sems[:3], True)
                  + _exchange_copies(ins[ns:], outs[ns:], sems[3:], False))
        for cp in copies:
            cp.start()
        for cp in copies:
            cp.wait()

    res = pl.pallas_call(
        body, name=name, out_shape=_exchange_shapes(scatter, True) + _exchange_shapes(gather, False),
        in_specs=[ANY] * (ns + ng), out_specs=[ANY] * (ns + ng),
        scratch_shapes=_exchange_scratch(ns) + _exchange_scratch(ng),
    )(*scatter, *gather)
    return res[:ns], res[ns:]


MM_ROWS = 512
MM_COLS = 1024


def _matmul(a, b, *, mode, out_dtype, name, tm, tn, res=None):
    if mode == "nn":
        (M, K), (K2, N) = a.shape, b.shape
    else:
        (M, K), (N, K2) = a.shape, b.shape
    assert K == K2, (a.shape, b.shape, mode)
    tm, tn = min(tm, M), min(tn, N)
    sm = min(tm, MM_ROWS)
    sn = tn if tn <= MM_COLS else _pick(tn, (512, 256, 128))
    assert M % tm == 0 and N % tn == 0 and tm % sm == 0, (M, N, K, tm, tn)
    dims = NN_DIMS if mode == "nn" else NT_DIMS
    a_spec = pl.BlockSpec((tm, K), lambda i, j: (i, 0))
    if mode == "nt":
        b_spec = pl.BlockSpec((tn, K), lambda i, j: (j, 0))
    else:
        b_spec = pl.BlockSpec((K, tn), lambda i, j: (0, j))
    o_spec = pl.BlockSpec((tm, tn), lambda i, j: (i, j))
    has_res = res is not None

    def body(*refs):
        a_ref, b_ref = refs[0], refs[1]
        r_ref = refs[2] if has_res else None
        o_ref = refs[2 + has_res]

        def chunk(r, carry):
            rows = pl.ds(pl.multiple_of(r * sm, sm), sm)
            av = a_ref[rows, :]
            for c0 in range(0, tn, sn):
                bv = b_ref[c0:c0 + sn, :] if mode == "nt" else b_ref[:, c0:c0 + sn]
                total = _dot(av, bv, dims)
                if has_res:
                    total = total + r_ref[rows, c0:c0 + sn]
                o_ref[rows, c0:c0 + sn] = total.astype(out_dtype)
            return carry

        lax.fori_loop(0, tm // sm, chunk, 0)

    return pl.pallas_call(
        body, name=name, grid=(M // tm, N // tn),
        out_shape=jax.ShapeDtypeStruct((M, N), out_dtype),
        in_specs=[a_spec, b_spec] + ([o_spec] if has_res else []),
        out_specs=o_spec,
        compiler_params=_params("parallel", "parallel"),
    )(*((a, b, res) if has_res else (a, b)))


def _matmul_tn(a, b, *, out_dtype, name, tk=512, sm=256):
    (K, M), (K2, N) = a.shape, b.shape
    assert K == K2 and K % tk == 0 and M % sm == 0, (a.shape, b.shape)
    nk = K // tk

    def body(a_ref, b_ref, o_ref, acc_ref):
        k = pl.program_id(0)

        @pl.when(k == 0)
        def _():
            acc_ref[...] = jnp.zeros_like(acc_ref)

        def chunk(mi, carry):
            cols = pl.ds(pl.multiple_of(mi * sm, sm), sm)
            acc_ref[cols, :] += _dot(a_ref[:, cols].T, b_ref[...])
            return carry

        lax.fori_loop(0, M // sm, chunk, 0)

        @pl.when(k == nk - 1)
        def _():
            def emit(mi, carry):
                rows = pl.ds(pl.multiple_of(mi * sm, sm), sm)
                o_ref[rows, :] = acc_ref[rows, :].astype(out_dtype)
                return carry
            lax.fori_loop(0, M // sm, emit, 0)

    return pl.pallas_call(
        body, name=name, grid=(nk,),
        out_shape=jax.ShapeDtypeStruct((M, N), out_dtype),
        in_specs=[pl.BlockSpec((tk, M), lambda k: (k, 0)), pl.BlockSpec((tk, N), lambda k: (k, 0))],
        out_specs=pl.BlockSpec((M, N), lambda k: (0, 0)),
        scratch_shapes=[pltpu.VMEM((M, N), F32)],
        compiler_params=_params("arbitrary"),
    )(a, b)


def _rmsnorm_fwd(x, gains, name, tr=256):
    n = len(gains)

    def body(*refs):
        x_ref = refs[0]
        xv = x_ref[...]
        r = lax.rsqrt(jnp.mean(xv * xv, axis=-1, keepdims=True) + RMS_EPS)
        y = xv * r
        for a in range(n):
            refs[1 + n + a][...] = (y * refs[1 + a][...]).astype(BF16)

    row = pl.BlockSpec((tr, D), lambda i: (i, 0))
    gain = pl.BlockSpec((1, D), lambda i: (0, 0))
    return pl.pallas_call(
        body, name=name, grid=(S // tr,),
        out_shape=[jax.ShapeDtypeStruct((S, D), BF16)] * n,
        in_specs=[row] + [gain] * n, out_specs=[row] * n,
        compiler_params=_params("parallel"),
    )(x, *gains)


def _rmsnorm_bwd(x, dy, g, dres, name, tr=256):
    def body(x_ref, dy_ref, g_ref, dres_ref, dx_ref, dxb_ref, dg_ref):
        xv = x_ref[...]
        dyv = dy_ref[...]
        r = lax.rsqrt(jnp.mean(xv * xv, axis=-1, keepdims=True) + RMS_EPS)
        xhat = xv * r
        dxhat = dyv * g_ref[...]
        mean_term = jnp.mean(dxhat * xhat, axis=-1, keepdims=True)
        dx = r * (dxhat - xhat * mean_term) + dres_ref[...]
        dx_ref[...] = dx
        dxb_ref[...] = dx.astype(BF16)
        part = jnp.sum(dyv * xhat, axis=0, keepdims=True)

        @pl.when(pl.program_id(0) == 0)
        def _():
            dg_ref[...] = part

        @pl.when(pl.program_id(0) > 0)
        def _():
            dg_ref[...] += part

    row = pl.BlockSpec((tr, D), lambda i: (i, 0))
    gain = pl.BlockSpec((1, D), lambda i: (0, 0))
    return pl.pallas_call(
        body, name=name, grid=(S // tr,),
        out_shape=[jax.ShapeDtypeStruct((S, D), F32), jax.ShapeDtypeStruct((S, D), BF16),
                   jax.ShapeDtypeStruct((1, D), F32)],
        in_specs=[row, row, gain, row], out_specs=[row, row, gain],
        compiler_params=_params("arbitrary"),
    )(x, dy, g, dres)


def _final_loss(x, target, g, name, tr=256):
    def body(x_ref, t_ref, g_ref, loss_ref, dx_ref, dxb_ref, dg_ref):
        xv = x_ref[...]
        gv = g_ref[...]
        r = lax.rsqrt(jnp.mean(xv * xv, axis=-1, keepdims=True) + RMS_EPS)
        xhat = xv * r
        err = xhat * gv - t_ref[...]
        row_loss = jnp.mean(err * err, axis=-1, keepdims=True)
        lpart = 0.5 * jnp.sum(row_loss, axis=0, keepdims=True)
        dyv = err / D
        dxhat = dyv * gv
        mean_term = jnp.mean(dxhat * xhat, axis=-1, keepdims=True)
        dx = r * (dxhat - xhat * mean_term)
        dx_ref[...] = dx
        dxb_ref[...] = dx.astype(BF16)
        gpart = jnp.sum(dyv * xhat, axis=0, keepdims=True)

        @pl.when(pl.program_id(0) == 0)
        def _():
            dg_ref[...] = gpart
            loss_ref[...] = jnp.broadcast_to(lpart, loss_ref.shape)

        @pl.when(pl.program_id(0) > 0)
        def _():
            dg_ref[...] += gpart
            loss_ref[...] += jnp.broadcast_to(lpart, loss_ref.shape)

    row = pl.BlockSpec((tr, D), lambda i: (i, 0))
    gain = pl.BlockSpec((1, D), lambda i: (0, 0))
    lspec = pl.BlockSpec((8, LANES), lambda i: (0, 0))
    return pl.pallas_call(
        body, name=name, grid=(S // tr,),
        out_shape=[jax.ShapeDtypeStruct((8, LANES), F32), jax.ShapeDtypeStruct((S, D), F32),
                   jax.ShapeDtypeStruct((S, D), BF16), jax.ShapeDtypeStruct((1, D), F32)],
        in_specs=[row, row, gain], out_specs=[lspec, row, row, gain],
        compiler_params=_params("arbitrary"),
    )(x, target, g)


CONV_TR = 128
CONV_TC = D_FF
CONV_NJ = D_FF // CONV_TC
HALO = 16


def _causal_taps(cur_ref, prev_ref, first):
    xv = cur_ref[...].astype(F32)
    pv = prev_ref[...].astype(F32)
    p1 = jnp.where(first, 0.0, pv[HALO - 1:HALO, :])
    p2 = jnp.where(first, 0.0, pv[HALO - 2:HALO - 1, :])
    r1, r2 = pltpu.roll(xv, 1, 0), pltpu.roll(xv, 2, 0)
    row = lax.broadcasted_iota(jnp.int32, (8, xv.shape[1]), 0)
    xm1 = jnp.concatenate([jnp.where(row == 0, p1, r1[0:8]), r1[8:]], axis=0)
    xm2 = jnp.concatenate([jnp.where(row == 0, p2, jnp.where(row == 1, p1, r2[0:8])), r2[8:]], axis=0)
    return xv, xm1, xm2


def _conv_specs():
    def prev_row(i):
        return jnp.maximum(i * (CONV_TR // HALO) - 1, 0)
    ua = pl.BlockSpec((CONV_TR, CONV_TC), lambda i, j: (i, j))
    ug = pl.BlockSpec((CONV_TR, CONV_TC), lambda i, j: (i, j + CONV_NJ))
    pa = pl.BlockSpec((HALO, CONV_TC), lambda i, j: (prev_row(i), j))
    pg = pl.BlockSpec((HALO, CONV_TC), lambda i, j: (prev_row(i), j + CONV_NJ))
    wa = pl.BlockSpec((3, CONV_TC), lambda i, j: (0, j))
    wg = pl.BlockSpec((3, CONV_TC), lambda i, j: (0, j + CONV_NJ))
    ba = pl.BlockSpec((1, CONV_TC), lambda i, j: (0, j))
    bg = pl.BlockSpec((1, CONV_TC), lambda i, j: (0, j + CONV_NJ))
    return [ua, pa, ug, pg, wa, wg, ba, bg]


def _convgate_fwd(u, w, b, name):
    def body(ua, pa, ug, pg, wa, wg, ba, bg, o_ref):
        first = pl.program_id(0) == 0
        x0, x1, x2 = _causal_taps(ua, pa, first)
        ac = wa[0:1, :] * x2 + wa[1:2, :] * x1 + wa[2:3, :] * x0 + ba[...]
        x0, x1, x2 = _causal_taps(ug, pg, first)
        gc = wg[0:1, :] * x2 + wg[1:2, :] * x1 + wg[2:3, :] * x0 + bg[...]
        sg = 1.0 / (1.0 + jnp.exp(-gc))
        o_ref[...] = (gc * sg * ac).astype(BF16)

    return pl.pallas_call(
        body, name=name, grid=(S // CONV_TR, CONV_NJ),
        out_shape=jax.ShapeDtypeStruct((S, D_FF), BF16),
        in_specs=_conv_specs(),
        out_specs=pl.BlockSpec((CONV_TR, CONV_TC), lambda i, j: (i, j)),
        compiler_params=_params("parallel", "parallel"),
    )(u, u, u, u, w, w, b, b)


def _convgate_bwd(u, w, b, dact, name):
    def body(ua, pa, ug, pg, wa, wg, ba, bg, d_ref, da_ref, dg_ref, dwa_ref, dwg_ref, dba_ref, dbg_ref):
        i = pl.program_id(1)
        first = i == 0
        a0, a1, a2 = _causal_taps(ua, pa, first)
        ac = wa[0:1, :] * a2 + wa[1:2, :] * a1 + wa[2:3, :] * a0 + ba[...]
        g0, g1, g2 = _causal_taps(ug, pg, first)
        gc = wg[0:1, :] * g2 + wg[1:2, :] * g1 + wg[2:3, :] * g0 + bg[...]
        sg = 1.0 / (1.0 + jnp.exp(-gc))
        dact_v = d_ref[...].astype(F32)
        da = dact_v * (gc * sg)
        dg = dact_v * ac * (sg * (1.0 + gc * (1.0 - sg)))
        da_ref[...] = da.astype(BF16)
        dg_ref[...] = dg.astype(BF16)

        def col(v):
            return jnp.sum(v, axis=0, keepdims=True)

        parts = [col(da * a2), col(da * a1), col(da * a0), col(dg * g2), col(dg * g1), col(dg * g0),
                 col(da), col(dg)]

        @pl.when(first)
        def _():
            for k in range(3):
                dwa_ref[k:k + 1, :] = parts[k]
                dwg_ref[k:k + 1, :] = parts[3 + k]
            dba_ref[...] = parts[6]
            dbg_ref[...] = parts[7]

        @pl.when(i > 0)
        def _():
            for k in range(3):
                dwa_ref[k:k + 1, :] += parts[k]
                dwg_ref[k:k + 1, :] += parts[3 + k]
            dba_ref[...] += parts[6]
            dbg_ref[...] += parts[7]

    def swap(spec):
        return pl.BlockSpec(spec.block_shape, lambda j, i, f=spec.index_map: f(i, j))

    blk = pl.BlockSpec((CONV_TR, CONV_TC), lambda j, i: (i, j))
    w3 = pl.BlockSpec((3, CONV_TC), lambda j, i: (0, j))
    b1 = pl.BlockSpec((1, CONV_TC), lambda j, i: (0, j))
    return pl.pallas_call(
        body, name=name, grid=(CONV_NJ, S // CONV_TR),
        out_shape=[jax.ShapeDtypeStruct((S, D_FF), BF16), jax.ShapeDtypeStruct((S, D_FF), BF16),
                   jax.ShapeDtypeStruct((3, D_FF), F32), jax.ShapeDtypeStruct((3, D_FF), F32),
                   jax.ShapeDtypeStruct((1, D_FF), F32), jax.ShapeDtypeStruct((1, D_FF), F32)],
        in_specs=[swap(s) for s in _conv_specs()] + [blk],
        out_specs=[blk, blk, w3, w3, b1, b1],
        compiler_params=_params("parallel", "arbitrary"),
    )(u, u, u, u, w, w, b, b, dact)


def _conv_input_bwd(d, w, name):
    n_i = S // CONV_TR

    def body(d_ref, n_ref, w_ref, o_ref):
        last = pl.program_id(0) == n_i - 1
        dv = d_ref[...].astype(F32)
        nv = n_ref[...].astype(F32)
        n1 = jnp.where(last, 0.0, nv[0:1, :])
        n2 = jnp.where(last, 0.0, nv[1:2, :])
        r1, r2 = pltpu.roll(dv, CONV_TR - 1, 0), pltpu.roll(dv, CONV_TR - 2, 0)
        row = lax.broadcasted_iota(jnp.int32, (8, dv.shape[1]), 0)
        cut = CONV_TR - 8
        dp1 = jnp.concatenate([r1[:cut], jnp.where(row == 7, n1, r1[cut:])], axis=0)
        dp2 = jnp.concatenate([r2[:cut], jnp.where(row == 7, n2, jnp.where(row == 6, n1, r2[cut:]))], axis=0)
        o_ref[...] = (w_ref[2:3, :] * dv + w_ref[1:2, :] * dp1 + w_ref[0:1, :] * dp2).astype(BF16)

    def next_row(i):
        return jnp.minimum((i + 1) * (CONV_TR // HALO), S // HALO - 1)

    blk = pl.BlockSpec((CONV_TR, CONV_TC), lambda i, j: (i, j))
    return pl.pallas_call(
        body, name=name, grid=(n_i, CONV_NJ),
        out_shape=jax.ShapeDtypeStruct((S, D_FF), BF16),
        in_specs=[blk, pl.BlockSpec((HALO, CONV_TC), lambda i, j: (next_row(i), j)),
                  pl.BlockSpec((3, CONV_TC), lambda i, j: (0, j))],
        out_specs=blk,
        compiler_params=_params("parallel", "parallel"),
    )(d, d, w)


FOX_T = 256
FOX_TQ, FOX_TK = 256, 256
N_PAIRS = A_HEADS // 2


def _lane_masks():
    lane = lax.broadcasted_iota(jnp.int32, (1, LANES), 1)
    return lane, (lane < HEAD_DIM, lane >= HEAD_DIM)


def _fox_prep_fwd(z_t, b, name):
    def body(z_ref, b_ref, c_ref):
        r = lax.broadcasted_iota(jnp.int32, (LANES, LANES), 0)
        cc = lax.broadcasted_iota(jnp.int32, (LANES, LANES), 1)
        upper = (r <= cc).astype(BF16)
        carry = jnp.zeros((A_HEADS, 1), F32)
        for blk in range(S // LANES):
            sl = slice(blk * LANES, (blk + 1) * LANES)
            z = z_ref[:, sl] + b_ref[...]
            lf = jnp.minimum(z, 0.0) - jnp.log(1.0 + jnp.exp(-jnp.abs(z)))
            cs = _split_dot(lf, upper, 3) + carry
            c_ref[:, sl] = cs
            carry = cs[:, LANES - 1:LANES]

    return pl.pallas_call(
        body, name=name, out_shape=jax.ShapeDtypeStruct((A_HEADS, S), F32),
        compiler_params=_params(),
    )(z_t, b)


def _fox_prep_bwd(drow_t, dcol_t, z_t, b, name):
    def body(dr_ref, dc_ref, z_ref, b_ref, dz_ref, db_ref):
        r = lax.broadcasted_iota(jnp.int32, (LANES, LANES), 0)
        cc = lax.broadcasted_iota(jnp.int32, (LANES, LANES), 1)
        lower = (r >= cc).astype(BF16)
        carry = jnp.zeros((A_HEADS, 1), F32)
        db = jnp.zeros((A_HEADS, 1), F32)
        for blk in reversed(range(S // LANES)):
            sl = slice(blk * LANES, (blk + 1) * LANES)
            rc = _split_dot(dr_ref[:, sl] - dc_ref[:, sl], lower, 3) + carry
            carry = rc[:, 0:1]
            z = z_ref[:, sl] + b_ref[...]
            dz = rc / (1.0 + jnp.exp(z))
            dz_ref[:, sl] = dz
            db = db + jnp.sum(dz, axis=1, keepdims=True)
        db_ref[...] = db

    return pl.pallas_call(
        body, name=name,
        out_shape=[jax.ShapeDtypeStruct((A_HEADS, S), F32), jax.ShapeDtypeStruct((A_HEADS, 1), F32)],
        compiler_params=_params(),
    )(drow_t, dcol_t, z_t, b)


def _fox_fwd(qkv, c_t2, name, gather):
    tq, tk = FOX_TQ, FOX_TK

    n = len(gather)

    def body(*refs):
        q_ref, k_ref, v_ref, ct_ref = refs[:4]
        o_ref, lse_ref = refs[4 + n:6 + n]
        exchange = (refs[4:4 + n], refs[6 + n:6 + 2 * n], refs[6 + 2 * n:], False)
        qi = pl.program_id(1)

        @pl.when(jnp.logical_and(pl.program_id(0) == 0, qi == 0))
        def _():
            for cp in _exchange_copies(*exchange):
                cp.start()

        n_full =jnp.right_shift(qi, (tk // tq).bit_length() - 1)
        lane, masks = _lane_masks()
        q = q_ref[...] * SCALE
        qs = [jnp.where(masks[e], q, jnp.zeros_like(q)) for e in range(2)]

        def scores(j):
            start = pl.multiple_of(j * tk, tk)
            kb = k_ref[pl.ds(start, tk), :]
            return tuple(_dot(qs[e], kb, NT_DIMS) - ct_ref[e:e + 1, pl.ds(start, tk)] for e in range(2))

        def softmax(s, m, masked):
            if masked:
                rows = lax.broadcasted_iota(jnp.int32, (tq, tk), 0) + (qi * tq - n_full * tk)
                cols = lax.broadcasted_iota(jnp.int32, (tq, tk), 1)
                s = tuple(jnp.where(cols <= rows, s[e], NEG) for e in range(2))
            m_new = tuple(jnp.maximum(m[e], jnp.max(s[e], axis=1, keepdims=True)) for e in range(2))
            p = tuple(jnp.exp(s[e] - m_new[e]).astype(BF16) for e in range(2))
            alpha = tuple(jnp.exp(m[e] - m_new[e]) for e in range(2))
            return m_new, p, alpha

        def weighted_values(j, p, alpha, acc):
            start = pl.multiple_of(j * tk, tk)
            vb = v_ref[pl.ds(start, tk), :]
            return tuple(alpha[e] * acc[e] + _dot(p[e], jnp.where(masks[e], vb, jnp.ones_like(vb)))
                         for e in range(2))

        def step(j, carry):
            s, p_prev, a_prev, m, acc = carry
            s_next = scores(j + 1)
            acc = weighted_values(jnp.maximum(j - 1, 0), p_prev, a_prev, acc)
            m, p, alpha = softmax(s, m, False)
            return s_next, p, alpha, m, acc

        two = lambda x: (x, x)
        init = (scores(0), two(jnp.zeros((tq, tk), BF16)), two(jnp.ones((tq, 1), F32)),
                two(jnp.full((tq, 1), NEG, F32)), two(jnp.zeros((tq, LANES), F32)))
        s, p_prev, a_prev, m, acc = lax.fori_loop(0, n_full, step, init)
        acc = weighted_values(jnp.maximum(n_full - 1, 0), p_prev, a_prev, acc)
        (m0, m1), p, alpha = softmax(s, m, True)
        acc0, acc1 = weighted_values(n_full, p, alpha, acc)
        l0 = acc0[:, HEAD_DIM:HEAD_DIM + 1]
        l1 = acc1[:, 0:1]
        o_ref[...] = jnp.where(masks[0], acc0 / l0, acc1 / l1).astype(BF16)
        lse_ref[...] = jnp.where(masks[0], m0 + jnp.log(l0), m1 + jnp.log(l1))

        @pl.when(jnp.logical_and(pl.program_id(0) == N_PAIRS - 1, qi == S // tq - 1))
        def _():
            for cp in _exchange_copies(*exchange):
                cp.wait()

    qspec = pl.BlockSpec((tq, LANES), lambda h, i: (i, h))
    return pl.pallas_call(
        body, name=name, grid=(N_PAIRS, S // tq),
        out_shape=[jax.ShapeDtypeStruct((S, D), BF16), jax.ShapeDtypeStruct((S, D), F32)]
        + _exchange_shapes(gather, False),
        in_specs=[qspec,
                  pl.BlockSpec((S, LANES), lambda h, i: (0, N_PAIRS + h)),
                  pl.BlockSpec((S, LANES), lambda h, i: (0, 2 * N_PAIRS + h)),
                  pl.BlockSpec((None, 2, S), lambda h, i: (h, 0, 0))] + [ANY] * n,
        out_specs=[qspec, qspec] + [ANY] * n,
        scratch_shapes=_exchange_scratch(n),
        compiler_params=_params("arbitrary", "arbitrary"),
    )(qkv, qkv, qkv, c_t2, *gather)


def _head_rowsum(a, b, name, tr=256):
    C = a.shape[1]

    def body(a_ref, b_ref, o_ref):
        r = lax.broadcasted_iota(jnp.int32, (LANES, LANES), 0) < HEAD_DIM
        cc = lax.broadcasted_iota(jnp.int32, (LANES, LANES), 1) < HEAD_DIM
        same_head = (r == cc).astype(BF16)
        for blk in range(C // LANES):
            sl = slice(blk * LANES, (blk + 1) * LANES)
            prod = a_ref[:, sl].astype(F32) * b_ref[:, sl].astype(F32)
            o_ref[:, sl] = _split_dot(prod, same_head, 2)

    row = pl.BlockSpec((tr, C), lambda i: (i, 0))
    return pl.pallas_call(
        body, name=name, grid=(S // tr,), out_shape=jax.ShapeDtypeStruct((S, C), F32),
        in_specs=[row, row], out_specs=row, compiler_params=_params("parallel"),
    )(a, b)


def _fox_bwd(qkv, do, lse, delta, c_t2, name, scatter):
    t = FOX_T
    nq = S // t

    n = len(scatter)

    def body(*refs):
        q_ref, k_ref, v_ref, do_ref, lse_ref, dl_ref, ct_ref = refs[:7]
        dq_ref, dk_ref, dv_ref, dcol_ref, drow_ref = refs[7 + n:12 + n]
        exchange = (refs[7:7 + n], refs[12 + n:12 + 2 * n], refs[12 + 2 * n:], True)
        kj = pl.program_id(1)

        @pl.when(jnp.logical_and(pl.program_id(0) == 0, kj == 0))
        def _():
            for cp in _exchange_copies(*exchange):
                cp.start()

        @pl.when(kj == 0)
        def _():
            dq_ref[...] = jnp.zeros_like(dq_ref)
            drow_ref[...] = jnp.zeros_like(drow_ref)

        lane, masks = _lane_masks()
        k = k_ref[...]
        v = v_ref[...]
        k_aug = [jnp.where(masks[e], k * SCALE, jnp.ones_like(k)) for e in range(2)]
        cs = [ct_ref[e:e + 1, :] for e in range(2)]

        def rows_of(i):
            r0 = pl.multiple_of(i * t, t)
            return pl.ds(r0, t), q_ref[pl.ds(r0, t), :] * SCALE, do_ref[pl.ds(r0, t), :]

        def scores(i):
            _, qb, dob = rows_of(i)
            out = []
            for e in range(2):
                qe = jnp.where(masks[e], qb, jnp.zeros_like(qb))
                doe = jnp.where(masks[e], dob, jnp.zeros_like(dob))
                out.append((_dot(qe, k, NT_DIMS) - cs[e], _dot(doe, v, NT_DIMS)))
            return tuple(out)

        def pointwise(i, sd, masked):
            rows, _, _ = rows_of(i)
            out = []
            for e in range(2):
                lo = e * HEAD_DIM
                s, dp = sd[e]
                if masked:
                    r = lax.broadcasted_iota(jnp.int32, (t, t), 0)
                    c = lax.broadcasted_iota(jnp.int32, (t, t), 1)
                    s = jnp.where(c <= r, s, NEG)
                p = jnp.exp(s - lse_ref[rows, lo:lo + 1])
                out.append((p.astype(BF16), (p * (dp - dl_ref[rows, lo:lo + 1])).astype(BF16)))
            return tuple(out)

        def accumulate(i, pd, acc):
            rows, qb, dob = rows_of(i)
            dk_acc, dv_acc = list(acc[0]), acc[1]
            dq_parts = []
            for e in range(2):
                p, ds = pd[e]
                q_aug = jnp.where(masks[e], qb, jnp.ones_like(qb))
                doe = jnp.where(masks[e], dob, jnp.zeros_like(dob))
                dv_acc = dv_acc + _dot(p, doe, TN_DIMS)
                dk_acc[e] = dk_acc[e] + _dot(ds, q_aug, TN_DIMS)
                dq_parts.append(_dot(ds, k_aug[e]))
            dq_ref[rows, :] += jnp.where(masks[0], dq_parts[0], dq_parts[1])
            drow_ref[rows, :] += jnp.where(masks[0], dq_parts[1], dq_parts[0])
            return tuple(dk_acc), dv_acc

        def step(i, carry):
            pd_prev, acc = carry
            sd = scores(i)
            acc = accumulate(i - 1, pd_prev, acc)
            return pointwise(i, sd, False), acc

        zero = jnp.zeros((t, LANES), F32)
        carry = (pointwise(kj, scores(kj), True), ((zero, zero), zero))
        pd, acc = lax.fori_loop(kj + 1, nq, step, carry)
        (dk0, dk1), dv = accumulate(nq - 1, pd, acc)
        dk_ref[...] = jnp.where(masks[0], dk0, dk1).astype(BF16)
        dcol_ref[...] = jnp.where(masks[0], dk1, dk0)
        dv_ref[...] = dv.astype(BF16)

        @pl.when(jnp.logical_and(pl.program_id(0) == N_PAIRS - 1, kj == nq - 1))
        def _():
            for cp in _exchange_copies(*exchange):
                cp.wait()

    full = lambda off: pl.BlockSpec((S, LANES), lambda h, j, off=off: (0, off + h))
    kv = lambda off: pl.BlockSpec((t, LANES), lambda h, j, off=off: (j, off + h))
    return pl.pallas_call(
        body, name=name, grid=(N_PAIRS, nq),
        out_shape=[jax.ShapeDtypeStruct((S, D), F32), jax.ShapeDtypeStruct((S, D), BF16),
                   jax.ShapeDtypeStruct((S, D), BF16), jax.ShapeDtypeStruct((S, D), F32),
                   jax.ShapeDtypeStruct((S, D), F32)] + _exchange_shapes(scatter, True),
        in_specs=[full(0), kv(N_PAIRS), kv(2 * N_PAIRS), full(0), full(0), full(0),
                  pl.BlockSpec((None, 2, t), lambda h, j: (h, 0, j))] + [ANY] * n,
        out_specs=[full(0), kv(0), kv(0), kv(0), full(0)] + [ANY] * n,
        scratch_shapes=_exchange_scratch(n),
        compiler_params=_params("arbitrary", "arbitrary"),
    )(qkv, qkv, qkv, do, lse, delta, c_t2, *scatter)


B_PAIRS = 4
B_NB = S // B_W


def _group_consts(g):
    nbs = jnp.where(g == 0, B_NB // B_DILS[0], jnp.where(g == 1, B_NB // B_DILS[1], B_NB // B_DILS[2]))
    dil = jnp.where(g == 0, B_DILS[0], jnp.where(g == 1, B_DILS[1], B_DILS[2]))
    return nbs, dil


def _band(dil):
    qi = lax.broadcasted_iota(jnp.int32, (B_W, B_W), 0)
    kj = lax.broadcasted_iota(jnp.int32, (B_W, B_W), 1)
    dist_c = qi - kj
    dist_p = qi + B_W - kj
    return (dist_c * dil).astype(F32), dist_c >= 0, (dist_p * dil).astype(F32), dist_p <= B_W


def _dil_fwd(qp, kp, vp, slopes, name):
    def body(sl_ref, q_ref, kp_ref, kc_ref, vp_ref, vc_ref, o_ref, lse_ref):
        g, n = pl.program_id(0), pl.program_id(1)
        nbs, dil = _group_consts(g)
        has_prev = (n % nbs) != 0
        lane, masks = _lane_masks()
        bias_c, ok_c, bias_p, ok_p = _band(dil)
        ok_p = jnp.logical_and(ok_p, has_prev)
        heads = [(hp, e) for hp in range(B_PAIRS) for e in range(2)]
        col = lambda ref, hp: ref[:, hp * LANES:(hp + 1) * LANES]
        logits = []
        for hp, e in heads:
            q = col(q_ref, hp) * SCALE
            qe = jnp.where(masks[e], q, jnp.zeros_like(q))
            logits.append((_dot(qe, col(kc_ref, hp), NT_DIMS), _dot(qe, col(kp_ref, hp), NT_DIMS)))
        probs = []
        for (hp, e), (sc, sp) in zip(heads, logits):
            slope = sl_ref[g * 8 + 2 * hp + e]
            sc = jnp.where(ok_c, sc - slope * bias_c, NEG)
            sp = jnp.where(ok_p, sp - slope * bias_p, NEG)
            m = jnp.maximum(jnp.max(sc, axis=1, keepdims=True), jnp.max(sp, axis=1, keepdims=True))
            probs.append((jnp.exp(sc - m).astype(BF16), jnp.exp(sp - m).astype(BF16), m))
        outs, lses = [], []
        for (hp, e), (pc, pp, m) in zip(heads, probs):
            vc, vpv = col(vc_ref, hp), col(vp_ref, hp)
            acc = (_dot(pc, jnp.where(masks[e], vc, jnp.ones_like(vc)))
                   + _dot(pp, jnp.where(masks[e], vpv, jnp.ones_like(vpv))))
            l = acc[:, HEAD_DIM:HEAD_DIM + 1] if e == 0 else acc[:, 0:1]
            outs.append(acc / l)
            lses.append(m + jnp.log(l))
        o_ref[...] = jnp.concatenate(
            [jnp.where(masks[0], outs[2 * hp], outs[2 * hp + 1]) for hp in range(B_PAIRS)], axis=1)
        lse_ref[...] = jnp.concatenate(
            [jnp.where(masks[0], lses[2 * hp], lses[2 * hp + 1]) for hp in range(B_PAIRS)], axis=1)

    cur = pl.BlockSpec((None, B_W, B_OUT), lambda g, n, sl: (g, n, 0))
    prev = pl.BlockSpec((None, B_W, B_OUT), lambda g, n, sl: (g, jnp.maximum(n - 1, 0), 0))
    return pl.pallas_call(
        body, name=name,
        grid_spec=pltpu.PrefetchScalarGridSpec(
            num_scalar_prefetch=1, grid=(3, B_NB),
            in_specs=[cur, prev, cur, prev, cur], out_specs=[cur, cur]),
        out_shape=[jax.ShapeDtypeStruct((3, S, B_OUT), F32), jax.ShapeDtypeStruct((3, S, B_OUT), F32)],
        compiler_params=_params("parallel", "parallel"),
    )(slopes, qp, kp, kp, vp, vp)


def _dil_merge(og, lseg, name, tr=256):
    def body(o_ref, l_ref, out_ref, lse_ref):
        l0, l1, l2 = l_ref[0], l_ref[1], l_ref[2]
        m = jnp.maximum(jnp.maximum(l0, l1), l2)
        w0, w1, w2 = jnp.exp(l0 - m), jnp.exp(l1 - m), jnp.exp(l2 - m)
        den = w0 + w1 + w2
        out_ref[...] = ((w0 * o_ref[0] + w1 * o_ref[1] + w2 * o_ref[2]) / den).astype(BF16)
        lse_ref[...] = m + jnp.log(den)

    blk3 = pl.BlockSpec((3, tr, B_OUT), lambda i: (0, i, 0))
    blk = pl.BlockSpec((tr, B_OUT), lambda i: (i, 0))
    return pl.pallas_call(
        body, name=name, grid=(S // tr,),
        out_shape=[jax.ShapeDtypeStruct((S, B_OUT), BF16), jax.ShapeDtypeStruct((S, B_OUT), F32)],
        in_specs=[blk3, blk3], out_specs=[blk, blk], compiler_params=_params("parallel"),
    )(og, lseg)


def _dil_bwd(qp, kp, vp, dop, lsep, dlp, slopes, name, scatter):
    n_ex = len(scatter)

    def body(sl_ref, *refs):
        (qc_ref, qn_ref, kp_ref, kc_ref, vp_ref, vc_ref, doc_ref, don_ref,
         lc_ref, ln_ref, dc_ref, dn_ref) = refs[:12]
        dq_ref, dk_ref, dv_ref = refs[12 + n_ex:15 + n_ex]
        exchange = (refs[12:12 + n_ex], refs[15 + n_ex:15 + 2 * n_ex], refs[15 + 2 * n_ex:], True)
        g, n = pl.program_id(0), pl.program_id(1)

        @pl.when(jnp.logical_and(g == 0, n == 0))
        def _():
            for cp in _exchange_copies(*exchange):
                cp.start()

        nbs, dil = _group_consts(g)
        has_prev = (n % nbs) != 0
        has_next = jnp.logical_and(n + 1 < B_NB, ((n + 1) % nbs) != 0)
        lane, masks = _lane_masks()
        bias_c, ok_c, bias_p, ok_p = _band(dil)
        ok_pp = jnp.logical_and(ok_p, has_prev)
        ok_np = jnp.logical_and(ok_p, has_next)
        heads = [(hp, e) for hp in range(B_PAIRS) for e in range(2)]
        col = lambda ref, hp: ref[:, hp * LANES:(hp + 1) * LANES]
        mask = lambda t, e: jnp.where(masks[e], t, jnp.zeros_like(t))
        raw = []
        for hp, e in heads:
            qce, qne = mask(col(qc_ref, hp) * SCALE, e), mask(col(qn_ref, hp) * SCALE, e)
            doce, done = mask(col(doc_ref, hp), e), mask(col(don_ref, hp), e)
            kc, kpv, vc, vpv = col(kc_ref, hp), col(kp_ref, hp), col(vc_ref, hp), col(vp_ref, hp)
            raw.append(((_dot(qce, kc, NT_DIMS), _dot(doce, vc, NT_DIMS)),
                        (_dot(qce, kpv, NT_DIMS), _dot(doce, vpv, NT_DIMS)),
                        (_dot(qne, kc, NT_DIMS), _dot(done, vc, NT_DIMS))))
        pds = []
        for (hp, e), tiles in zip(heads, raw):
            lo = hp * LANES + e * HEAD_DIM
            slope = sl_ref[g * 8 + 2 * hp + e]
            lse_c, dl_c = lc_ref[:, lo:lo + 1], dc_ref[:, lo:lo + 1]
            lse_n, dl_n = ln_ref[:, lo:lo + 1], dn_ref[:, lo:lo + 1]
            out = []
            for (s, dp), ok, bias, lse, dl in ((tiles[0], ok_c, bias_c, lse_c, dl_c),
                                               (tiles[1], ok_pp, bias_p, lse_c, dl_c),
                                               (tiles[2], ok_np, bias_p, lse_n, dl_n)):
                p = jnp.exp(jnp.where(ok, s - slope * bias, NEG) - lse)
                out.append((p.astype(BF16), (p * (dp - dl)).astype(BF16)))
            pds.append(out)
        dq_all, dk_all, dv_all = [], [], []
        for hp in range(B_PAIRS):
            dq = jnp.zeros((B_W, LANES), F32)
            dk = jnp.zeros((B_W, LANES), F32)
            dv = jnp.zeros((B_W, LANES), F32)
            for e in range(2):
                (p_c, ds_c), (_, ds_p), (p_n, ds_n) = pds[2 * hp + e]
                qce, qne = mask(col(qc_ref, hp) * SCALE, e), mask(col(qn_ref, hp) * SCALE, e)
                doce, done = mask(col(doc_ref, hp), e), mask(col(don_ref, hp), e)
                dq = dq + _dot(ds_c, mask(col(kc_ref, hp) * SCALE, e)) + _dot(ds_p, mask(col(kp_ref, hp) * SCALE, e))
                dk = dk + _dot(ds_c, qce, TN_DIMS) + _dot(ds_n, qne, TN_DIMS)
                dv = dv + _dot(p_c, doce, TN_DIMS) + _dot(p_n, done, TN_DIMS)
            dq_all.append(dq)
            dk_all.append(dk)
            dv_all.append(dv)
        dq_ref[...] = jnp.concatenate(dq_all, axis=1).astype(BF16)
        dk_ref[...] = jnp.concatenate(dk_all, axis=1).astype(BF16)
        dv_ref[...] = jnp.concatenate(dv_all, axis=1).astype(BF16)

        @pl.when(jnp.logical_and(g == 2, n == B_NB - 1))
        def _():
            for cp in _exchange_copies(*exchange):
                cp.wait()

    cur = pl.BlockSpec((None, B_W, B_OUT), lambda g, n, sl: (g, n, 0))
    prev = pl.BlockSpec((None, B_W, B_OUT), lambda g, n, sl: (g, jnp.maximum(n - 1, 0), 0))
    nxt = pl.BlockSpec((None, B_W, B_OUT), lambda g, n, sl: (g, jnp.minimum(n + 1, B_NB - 1), 0))
    return pl.pallas_call(
        body, name=name,
        grid_spec=pltpu.PrefetchScalarGridSpec(
            num_scalar_prefetch=1, grid=(3, B_NB),
            in_specs=[cur, nxt, prev, cur, prev, cur, cur, nxt, cur, nxt, cur, nxt] + [ANY] * n_ex,
            out_specs=[cur, cur, cur] + [ANY] * n_ex,
            scratch_shapes=_exchange_scratch(n_ex)),
        out_shape=[jax.ShapeDtypeStruct((3, S, B_OUT), BF16)] * 3 + _exchange_shapes(scatter, True),
        compiler_params=_params("arbitrary", "arbitrary"),
    )(slopes, qp, qp, kp, kp, vp, vp, dop, dop, lsep, lsep, dlp, dlp, *scatter)


def _rows_block(shape, max_bytes=2 * 1024 * 1024):
    rows, cols = shape
    padded_cols = -(-cols // LANES) * LANES
    for tr in (1024, 512, 256, 128, 64, 32, 16):
        if rows % tr == 0 and tr * padded_cols * 4 <= max_bytes:
            return tr
    return rows


def _adam_update(w, m, v, g):
    m_new = ADAM_B1 * m + (1.0 - ADAM_B1) * g
    v_new = ADAM_B2 * v + (1.0 - ADAM_B2) * (g * g)
    m_hat = m_new / (1.0 - ADAM_B1 ** ADAM_STEP)
    v_hat = v_new / (1.0 - ADAM_B2 ** ADAM_STEP)
    delta = -ADAM_LR * (m_hat / (jnp.sqrt(v_hat) + ADAM_EPS) + ADAM_WD * w)
    return delta, m_new, v_new


def _adamw_sharded(w, m, v, parts, name):
    R, C = w.shape
    tr = _rows_block((R, C), max_bytes=1024 * 1024)

    def body(w_ref, m_ref, v_ref, p_ref, g_ref, d_ref, mo_ref, vo_ref):
        g = p_ref[0].astype(F32)
        for dev in range(1, N_DEV):
            g = g + p_ref[dev].astype(F32)
        g_ref[...] = g
        d_ref[...], mo_ref[...], vo_ref[...] = _adam_update(w_ref[...], m_ref[...], v_ref[...], g)

    blk = pl.BlockSpec((tr, C), lambda i: (i, 0))
    out = jax.ShapeDtypeStruct((R, C), F32)
    return pl.pallas_call(
        body, name=name, grid=(R // tr,),
        in_specs=[blk, blk, blk, pl.BlockSpec((N_DEV, tr, C), lambda i: (0, i, 0))],
        out_specs=[blk, blk, blk, blk], out_shape=[out, out, out, out],
        compiler_params=_params("parallel"),
    )(w, m, v, parts)


def _adamw_replicated(w, m, v, parts, name):
    def body(w_ref, m_ref, v_ref, p_ref, g_ref, d_ref, mo_ref, vo_ref):
        g = p_ref[0]
        for dev in range(1, N_DEV):
            g = g + p_ref[dev]
        g_ref[...] = g
        d_ref[...], mo_ref[...], vo_ref[...] = _adam_update(w_ref[...], m_ref[...], v_ref[...], g)

    out = jax.ShapeDtypeStruct(w.shape, F32)
    return pl.pallas_call(body, name=name, out_shape=[out, out, out, out], compiler_params=_params())(w, m, v, parts)


def _cols_from_slots(g):
    return g.transpose(1, 0, 2).reshape(g.shape[1], N_DEV * g.shape[2])


def _cols_to_slots(w):
    k, n = w.shape
    return w.reshape(k, N_DEV, n // N_DEV).transpose(1, 0, 2)


def _permute(t, dil):
    c = t.shape[1]
    return t.reshape(S // dil, dil, c).transpose(1, 0, 2).reshape(S, c)


def _unpermute(t, dil):
    c = t.shape[1]
    return t.reshape(dil, S // dil, c).transpose(1, 0, 2).reshape(S, c)


def _group_permute(t):
    return jnp.stack([_permute(t[:, g * B_OUT:(g + 1) * B_OUT], B_DILS[g]) for g in range(3)])


def _same_permute(t):
    return jnp.stack([_permute(t, d) for d in B_DILS])


def _group_unpermute(t):
    return jnp.stack([_unpermute(t[g], B_DILS[g]) for g in range(3)])


SMALL_ROWS = 144


def _pack_small(a_b_f, kv_g, mix_g, ffn_g, conv_b, fin_g):
    flat = jnp.concatenate([a_b_f.reshape(-1), kv_g.reshape(-1), mix_g.reshape(-1), ffn_g.reshape(-1),
                            conv_b.reshape(-1), fin_g.reshape(-1)])
    return jnp.pad(flat, (0, SMALL_ROWS * LANES - flat.shape[0])).reshape(SMALL_ROWS, LANES)


def _unpack_small(p):
    flat = p.reshape(-1)
    out, off = [], 0
    for shape in ((1, A_HEADS), (D,), (2, D), (2, D), (2, 2 * D_FF), (D,)):
        size = math.prod(shape)
        out.append(flat[off:off + size].reshape(shape))
        off += size
    return out


def _unpack_late(g):
    w_up = g[4].reshape(N_DEV, 2, D, -1).transpose(1, 2, 0, 3).reshape(2, D, 2 * D_FF)
    w_down = g[5].reshape(N_DEV, 2, -1, D).transpose(1, 0, 2, 3).reshape(2, D_FF, D)
    conv_w = g[6].reshape(N_DEV, 2, 3, -1).transpose(1, 2, 0, 3).reshape(2, 3, 2 * D_FF)
    return (g[0].reshape(D, D), _cols_from_slots(g[1]), _cols_from_slots(g[2]), _cols_from_slots(g[3]),
            w_up, w_down, conv_w)


def _ffn_slots(dw_up, dw_down, dconv_w):
    return [_cols_to_slots(dw_up), dw_down.reshape(N_DEV, -1, D), _cols_to_slots(dconv_w)]


def _local_step(x0, target, w_in_pad, late_shards,
                a_b_f, kv_norm_g, mix_norm_g, ffn_norm_g, ffn_conv_b, final_norm_g):
    w_qkv, w_f = w_in_pad[:, :A_QKV], w_in_pad[:, A_QKV:]
    conv_b = ffn_conv_b.reshape(2, 1, 2 * D_FF)
    slopes = jnp.exp2(-8.0 * jnp.arange(1, 25, dtype=F32) / 24)

    def gain(g):
        return g.reshape(1, D)

    (h1,) = _rmsnorm_fwd(x0, [gain(mix_norm_g[0])], "norm_mix0")
    qkv = _matmul(h1, w_qkv, mode="nn", out_dtype=BF16, name="proj_qkv", tm=512, tn=A_QKV)
    z = _matmul(h1, w_f, mode="nn", out_dtype=F32, name="proj_gate", tm=S, tn=LANES)
    z_t = z[:, :A_HEADS].T
    b_f = a_b_f.reshape(A_HEADS, 1)
    c_t = _fox_prep_fwd(z_t, b_f, "fox_prep")
    c_t2 = c_t.reshape(N_PAIRS, 2, S)
    o_a, lse_a, *late = _fox_fwd(qkv, c_t2, "fox_fwd", late_shards)
    w_out, w_q, w_bo, w_kvf, w_up, w_down, conv_w = _unpack_late(late)
    x1 = _matmul(o_a, w_out, mode="nn", out_dtype=F32, name="a_out", tm=512, tn=D, res=x0)

    def ffn_fwd(xin, layer):
        (h,) = _rmsnorm_fwd(xin, [gain(ffn_norm_g[layer])], f"norm_ffn{layer}")
        u = _matmul(h, w_up[layer], mode="nn", out_dtype=BF16, name=f"ffn_up{layer}", tm=512, tn=2 * D_FF)
        act = _convgate_fwd(u, conv_w[layer], conv_b[layer], f"convgate{layer}")
        xout = _matmul(act, w_down[layer], mode="nn", out_dtype=F32, name=f"ffn_down{layer}", tm=512, tn=D, res=xin)
        return h, u, act, xout

    h2, u0, act0, x2 = ffn_fwd(x1, 0)
    hk, h3 = _rmsnorm_fwd(x2, [gain(kv_norm_g), gain(mix_norm_g[1])], "norm_kv_mix1")
    kv = _matmul(hk, w_kvf, mode="nn", out_dtype=BF16, name="proj_kv", tm=512, tn=B_KV)
    qb = _matmul(h3, w_q, mode="nn", out_dtype=BF16, name="proj_qb", tm=512, tn=B_Q)
    qp, kp, vp = _group_permute(qb), _group_permute(kv[:, :B_Q]), _group_permute(kv[:, B_Q:])
    og_p, lseg_p = _dil_fwd(qp, kp, vp, slopes, "dil_fwd")
    o_b, lse_b = _dil_merge(_group_unpermute(og_p), _group_unpermute(lseg_p), "dil_merge")
    x3 = _matmul(o_b, w_bo, mode="nn", out_dtype=F32, name="b_out", tm=512, tn=D, res=x2)
    h4, u1, act1, x4 = ffn_fwd(x3, 1)
    loss_blk, dx4, dx4b, dg_final = _final_loss(x4, target, gain(final_norm_g), "final_loss")

    def ffn_bwd(dx, dxb, xin, h, u, act, layer):
        dact = _matmul(dxb, w_down[layer], mode="nt", out_dtype=BF16, name=f"d_act{layer}", tm=512, tn=D_FF)
        dw_down = _matmul_tn(act, dxb, out_dtype=BF16, name=f"dw_down{layer}")
        da, dg, dwa, dwg, dba, dbg = _convgate_bwd(u, conv_w[layer], conv_b[layer], dact, f"convgate_bwd{layer}")
        du_a = _conv_input_bwd(da, conv_w[layer][:, :D_FF], f"conv_in_bwd_a{layer}")
        du_g = _conv_input_bwd(dg, conv_w[layer][:, D_FF:], f"conv_in_bwd_g{layer}")
        dw_up = jnp.concatenate(
            [_matmul_tn(h, du_a, out_dtype=BF16, name=f"dw_up_a{layer}"),
             _matmul_tn(h, du_g, out_dtype=BF16, name=f"dw_up_g{layer}")], axis=1)
        dh = _matmul(du_a, w_up[layer][:, :D_FF], mode="nt", out_dtype=F32, name=f"dh_ffn_a{layer}", tm=512, tn=D)
        dh = _matmul(du_g, w_up[layer][:, D_FF:], mode="nt", out_dtype=F32, name=f"dh_ffn_g{layer}", tm=512, tn=D,
                     res=dh)
        dxin, dxinb, dgain = _rmsnorm_bwd(xin, dh, gain(ffn_norm_g[layer]), dx, f"norm_ffn_bwd{layer}")
        dconv_w = jnp.concatenate([dwa, dwg], axis=1)
        dconv_b = jnp.concatenate([dba, dbg], axis=1)
        return dxin, dxinb, dgain, dw_up, dw_down, dconv_w, dconv_b

    dx3, dx3b, dg_ffn1, dw_up1, dw_down1, dconv_w1, dconv_b1 = ffn_bwd(dx4, dx4b, x3, h4, u1, act1, 1)

    do_b = _matmul(dx3b, w_bo, mode="nt", out_dtype=BF16, name="d_ob", tm=1024, tn=B_OUT)
    dw_bo = _matmul_tn(o_b, dx3b, out_dtype=BF16, name="dw_bo")
    dl_b = _head_rowsum(do_b, o_b, "delta_b")
    slots_up1, slots_down1, slots_conv1 = _ffn_slots(dw_up1, dw_down1, dconv_w1)
    dqp, dkp, dvp, land_down1, land_conv1 = _dil_bwd(
        qp, kp, vp, _same_permute(do_b), _same_permute(lse_b), _same_permute(dl_b), slopes, "dil_bwd",
        [slots_down1, slots_conv1])

    def natural(tp):
        return jnp.concatenate([_unpermute(tp[g], B_DILS[g]) for g in range(3)], axis=1)

    dqb = natural(dqp)
    dkv = jnp.concatenate([natural(dkp), natural(dvp)], axis=1)
    dw_q = _matmul_tn(h3, dqb, out_dtype=BF16, name="dw_q")
    dw_kv = _matmul_tn(hk, dkv, out_dtype=BF16, name="dw_kv")
    dh3 = _matmul(dqb, w_q, mode="nt", out_dtype=F32, name="dh_mix1", tm=512, tn=D)
    dhk = _matmul(dkv, w_kvf, mode="nt", out_dtype=F32, name="dh_kv", tm=512, tn=D)
    dx2, _, dg_mix1 = _rmsnorm_bwd(x2, dh3, gain(mix_norm_g[1]), dx3, "norm_mix1_bwd")
    dx2, dx2b, dg_kv = _rmsnorm_bwd(x2, dhk, gain(kv_norm_g), dx2, "norm_kv_bwd")

    dx1, dx1b, dg_ffn0, dw_up0, dw_down0, dconv_w0, dconv_b0 = ffn_bwd(dx2, dx2b, x1, h2, u0, act0, 0)

    do_a = _matmul(dx1b, w_out, mode="nt", out_dtype=BF16, name="d_oa", tm=512, tn=D)
    dw_out = _matmul_tn(o_a, dx1b, out_dtype=BF16, name="dw_out")
    dl_a = _head_rowsum(do_a, o_a, "delta_a")
    dq_a, dk_a, dv_a, dcol, drow, *land = _fox_bwd(
        qkv, do_a, lse_a, dl_a, c_t2, "fox_bwd",
        [dw_out.reshape(N_DEV, D // N_DEV, D), _cols_to_slots(dw_q), _cols_to_slots(dw_bo), _cols_to_slots(dw_kv)]
        + _ffn_slots(dw_up0, dw_down0, dconv_w0) + [slots_up1])
    land_out, land_q, land_bo, land_kv, land_up0, land_down0, land_conv0, land_up1 = land

    def head_sums(t):
        return t.reshape(S, N_PAIRS, 2, HEAD_DIM)[:, :, ::-1, 0].reshape(S, A_HEADS).T

    dz_t, db_f = _fox_prep_bwd(head_sums(drow), head_sums(dcol), z_t, b_f, "fox_prep_bwd")
    dz = jnp.pad(dz_t.T, ((0, 0), (0, LANES - A_HEADS))).astype(BF16)
    dproj = jnp.concatenate([dq_a.astype(BF16), dk_a, dv_a, dz], axis=1)
    dw_in = _matmul_tn(h1, dproj, out_dtype=BF16, name="dw_in")
    dh1 = _matmul(dproj, w_in_pad, mode="nt", out_dtype=F32, name="dh_mix0", tm=512, tn=D)
    grad_x, _, dg_mix0 = _rmsnorm_bwd(x0, dh1, gain(mix_norm_g[0]), dx1, "norm_mix0_bwd")

    dg_mix = jnp.concatenate([dg_mix0, dg_mix1], axis=0)
    dg_ffn = jnp.concatenate([dg_ffn0, dg_ffn1], axis=0)
    dconv_b = jnp.concatenate([dconv_b0, dconv_b1], axis=0)
    small_part = _pack_small(db_f, dg_kv, dg_mix, dg_ffn, dconv_b, dg_final)
    (land_in,), (small_parts,) = _final_exchange(
        [_cols_to_slots(dw_in[:, :A_QKV + A_HEADS])], [small_part], "exchange_last_grads")
    landed = [land_in, land_out, land_q, land_bo, land_kv, land_up0, land_up1, land_down0, land_down1,
              land_conv0, land_conv1]
    return loss_blk, grad_x, landed, small_parts


def kernel(x, a_w_in, a_b_f, a_w_out, b_w_q, b_w_out, kv_norm_g, w_kv, mix_norm_g, ffn_norm_g, ffn_w_up, ffn_conv_w, ffn_conv_b, ffn_w_down, final_norm_g, loss_target, m_a_w_in, m_a_b_f, m_a_w_out, m_b_w_q, m_b_w_out, m_kv_norm_g, m_w_kv, m_mix_norm_g, m_ffn_norm_g, m_ffn_w_up, m_ffn_conv_w, m_ffn_conv_b, m_ffn_w_down, m_final_norm_g, v_a_w_in, v_a_b_f, v_a_w_out, v_b_w_q, v_b_w_out, v_kv_norm_g, v_w_kv, v_mix_norm_g, v_ffn_norm_g, v_ffn_w_up, v_ffn_conv_w, v_ffn_conv_b, v_ffn_w_down, v_final_norm_g):
    def shards(a_w_in, a_w_out, b_w_q, b_w_out, w_kv, ffn_w_up, ffn_w_down, ffn_conv_w):
        return [a_w_in[0], a_w_out[0], b_w_q[0], b_w_out[0], w_kv, ffn_w_up[0], ffn_w_up[1],
                ffn_w_down[0], ffn_w_down[1], ffn_conv_w[0], ffn_conv_w[1]]

    w_loc = shards(a_w_in, a_w_out, b_w_q, b_w_out, w_kv, ffn_w_up, ffn_w_down, ffn_conv_w)
    m_loc = shards(m_a_w_in, m_a_w_out, m_b_w_q, m_b_w_out, m_w_kv, m_ffn_w_up, m_ffn_w_down, m_ffn_conv_w)
    v_loc = shards(v_a_w_in, v_a_w_out, v_b_w_q, v_b_w_out, v_w_kv, v_ffn_w_up, v_ffn_w_down, v_ffn_conv_w)

    (g_in,) = _all_gather([a_w_in[0].astype(BF16)], "gather_a_w_in")
    w_in = _cols_from_slots(g_in)
    w_in_pad = jnp.pad(w_in, ((0, 0), (0, A_PROJ_PAD - w_in.shape[1])))
    late_shards = [a_w_out[0].astype(BF16), b_w_q[0].astype(BF16), b_w_out[0].astype(BF16), w_kv.astype(BF16),
                   ffn_w_up.reshape(2 * D, -1).astype(BF16), ffn_w_down.reshape(-1, D).astype(BF16),
                   ffn_conv_w.reshape(6, -1)]

    loss_blk, grad_x, landed, small_parts = _local_step(
        x[0], loss_target[0], w_in_pad, late_shards,
        a_b_f, kv_norm_g, mix_norm_g, ffn_norm_g, ffn_conv_b, final_norm_g)

    big = [_adamw_sharded(w_loc[k], m_loc[k], v_loc[k], landed[k], f"adamw{k}") for k in range(11)]

    small = _adamw_replicated(
        _pack_small(a_b_f, kv_norm_g, mix_norm_g, ffn_norm_g, ffn_conv_b, final_norm_g),
        _pack_small(m_a_b_f, m_kv_norm_g, m_mix_norm_g, m_ffn_norm_g, m_ffn_conv_b, m_final_norm_g),
        _pack_small(v_a_b_f, v_kv_norm_g, v_mix_norm_g, v_ffn_norm_g, v_ffn_conv_b, v_final_norm_g),
        small_parts, "adamw_small")

    loss = lax.psum(loss_blk[0, 0], ("x", "y", "c"))

    def assemble(kind):
        b = [r[kind] for r in big]
        s_abf, s_kv, s_mix, s_ffn, s_cb, s_fin = _unpack_small(small[kind])
        return [b[0][None], s_abf, b[1][None], b[2][None], b[3][None], s_kv, b[4], s_mix, s_ffn,
                jnp.stack([b[5], b[6]]), jnp.stack([b[9], b[10]]), s_cb, jnp.stack([b[7], b[8]]), s_fin]

    return (loss, grad_x[None], *assemble(0), *assemble(1), *assemble(2), *assemble(3))
```

```python
import functools
import math

import jax
import jax.numpy as jnp
from jax import lax
from jax.experimental import pallas as pl
from jax.experimental.pallas import tpu as pltpu

F32 = jnp.float32
BF16 = jnp.bfloat16

S = 4096
D = 1024
N_DEV = 8
A_HEADS = 16
HEAD_DIM = 64
A_QKV = 3072
A_PROJ_PAD = 3200
B_Q = 1536
B_OUT = 512
B_KV = 3072
B_W = 128
B_DILS = (1, 4, 16)
D_FF = 2816
RMS_EPS = 1e-6
SCALE = HEAD_DIM ** -0.5
NEG = -1e30

ADAM_LR = 0.001
ADAM_B1 = 0.9
ADAM_B2 = 0.999
ADAM_EPS = 1e-08
ADAM_WD = 0.01
ADAM_STEP = 10

LANES = 128
VMEM_LIMIT = 56 * 1024 * 1024
MESH = pl.DeviceIdType.MESH
ANY = pl.BlockSpec(memory_space=pl.ANY)

NT_DIMS = (((1,), (1,)), ((), ()))
TN_DIMS = (((0,), (0,)), ((), ()))
NN_DIMS = (((1,), (0,)), ((), ()))


def _params(*sem):
    return pltpu.CompilerParams(dimension_semantics=sem if sem else None, vmem_limit_bytes=VMEM_LIMIT)


def _dot(a, b, dims=NN_DIMS):
    return lax.dot_general(a, b, dims, preferred_element_type=F32)


def _split_dot(x, mat, pieces):
    out = None
    rem = x
    for _ in range(pieces):
        part = rem.astype(BF16)
        rem = rem - part.astype(F32)
        d = _dot(part, mat)
        out = d if out is None else out + d
    return out


def _pick(n, prefs):
    for p in prefs:
        if n % p == 0:
            return p
    return n


def _all_gather(arrays, name):
    n = len(arrays)

    def body(*refs):
        ins = refs[:n]
        outs = refs[n:2 * n]
        send_sems, recv_sems, local_sems = refs[2 * n:]
        x, y, c = lax.axis_index("x"), lax.axis_index("y"), lax.axis_index("c")
        me, sibling = (x, y, c), (x, y, 1 - c)
        chips = [(1 - x, y), (x, 1 - y), (1 - x, 1 - y)]

        def slot(a, px, py, pc):
            return outs[a].at[4 * px + 2 * py + pc]

        def copy(a, k, block, to, src=None):
            return pltpu.make_async_remote_copy(
                src_ref=slot(a, *block) if src is None else src, dst_ref=slot(a, *block),
                send_sem=send_sems.at[a, k], recv_sem=recv_sems.at[a, k],
                device_id=to, device_id_type=MESH)

        mine = [pltpu.make_async_copy(ins[a], slot(a, *me), local_sems.at[a]) for a in range(n)]
        for cp in mine:
            cp.start()
        first = []
        for a in range(n):
            first.append(copy(a, 0, me, sibling, src=ins[a]))
            first += [copy(a, 1 + j, me, (*chip, c), src=ins[a]) for j, chip in enumerate(chips)]
        for cp in first:
            cp.start()
        passed = []
        for j, chip in enumerate(chips):
            for a in range(n):
                copy(a, 1 + j, (*chip, c), me).wait_recv()
                fwd = copy(a, 4 + j, (*chip, c), sibling)
                fwd.start()
                passed.append(fwd)
        for a in range(n):
            copy(a, 0, sibling, me).wait_recv()
            for j, chip in enumerate(chips):
                copy(a, 4 + j, (*chip, 1 - c), me).wait_recv()
        for cp in first + passed:
            cp.wait_send()
        for cp in mine:
            cp.wait()

    return pl.pallas_call(
        body, name=name,
        out_shape=[jax.ShapeDtypeStruct((N_DEV,) + a.shape, a.dtype) for a in arrays],
        in_specs=[ANY] * n, out_specs=[ANY] * n,
        scratch_shapes=[pltpu.SemaphoreType.DMA((n, 7)), pltpu.SemaphoreType.DMA((n, 7)),
                        pltpu.SemaphoreType.DMA((n,))],
    )(*arrays)


PEER_FLIPS = [(dx, dy, dc) for dx in (0, 1) for dy in (0, 1) for dc in (0, 1) if (dx, dy, dc) != (0, 0, 0)]


def _exchange_copies(ins, outs, sems, scatter):
    if not ins:
        return []
    send_sems, recv_sems, local_sems = sems
    x, y, c = lax.axis_index("x"), lax.axis_index("y"), lax.axis_index("c")
    me = 4 * x + 2 * y + c
    copies = []
    for a in range(len(ins)):
        copies.append(pltpu.make_async_copy(ins[a].at[me] if scatter else ins[a], outs[a].at[me], local_sems.at[a]))
        for k, (dx, dy, dc) in enumerate(PEER_FLIPS):
            px, py, pc = (1 - x if dx else x), (1 - y if dy else y), (1 - c if dc else c)
            copies.append(pltpu.make_async_remote_copy(
                src_ref=ins[a].at[4 * px + 2 * py + pc] if scatter else ins[a], dst_ref=outs[a].at[me],
                send_sem=send_sems.at[a, k], recv_sem=recv_sems.at[a, k],
                device_id=(px, py, pc), device_id_type=MESH))
    return copies


def _exchange_scratch(n):
    if n == 0:
        return []
    return [pltpu.SemaphoreType.DMA((n, 7)), pltpu.SemaphoreType.DMA((n, 7)), pltpu.SemaphoreType.DMA((n,))]


def _exchange_shapes(arrays, scatter):
    return [jax.ShapeDtypeStruct((N_DEV,) + (a.shape[1:] if scatter else a.shape), a.dtype) for a in arrays]


def _final_exchange(scatter, gather, name):
    ns, ng = len(scatter), len(gather)

    def body(*refs):
        ins, outs, sems = refs[:ns + ng], refs[ns + ng:2 * (ns + ng)], refs[2 * (ns + ng):]
        n_sems = len(_exchange_scratch(ns))
        copies = (_exchange_copies(ins[:ns], outs[:ns], sems[:n_sems], True)
                  + _exchange_copies(ins[ns:], outs[ns:], sems[n_sems:], False))
        for cp in copies:
            cp.start()
        for cp in copies:
            cp.wait()

    res = pl.pallas_call(
        body, name=name, out_shape=_exchange_shapes(scatter, True) + _exchange_shapes(gather, False),
        in_specs=[ANY] * (ns + ng), out_specs=[ANY] * (ns + ng),
        scratch_shapes=_exchange_scratch(ns) + _exchange_scratch(ng),
    )(*scatter, *gather)
    return res[:ns], res[ns:]


MM_ROWS = 512
MM_COLS = 1024


def _matmul(a, b, *, mode, out_dtype, name, tm, tn, res=None, scatter=()):
    if mode == "nn":
        (M, K), (K2, N) = a.shape, b.shape
    else:
        (M, K), (N, K2) = a.shape, b.shape
    assert K == K2, (a.shape, b.shape, mode)
    tm, tn = min(tm, M), min(tn, N)
    sm = min(tm, MM_ROWS)
    sn = tn if tn <= MM_COLS else _pick(tn, (512, 256, 128))
    assert M % tm == 0 and N % tn == 0 and tm % sm == 0, (M, N, K, tm, tn)
    dims = NN_DIMS if mode == "nn" else NT_DIMS
    a_spec = pl.BlockSpec((tm, K), lambda i, j: (i, 0))
    if mode == "nt":
        b_spec = pl.BlockSpec((tn, K), lambda i, j: (j, 0))
    else:
        b_spec = pl.BlockSpec((K, tn), lambda i, j: (0, j))
    o_spec = pl.BlockSpec((tm, tn), lambda i, j: (i, j))
    has_res = res is not None
    n_in, n_ex = 2 + has_res, len(scatter)
    gm, gn = M // tm, N // tn

    def body(*refs):
        a_ref, b_ref = refs[0], refs[1]
        r_ref = refs[2] if has_res else None
        o_ref = refs[n_in + n_ex]
        exchange = (refs[n_in:n_in + n_ex], refs[n_in + n_ex + 1:n_in + 2 * n_ex + 1], refs[n_in + 2 * n_ex + 1:], True)

        @pl.when(jnp.logical_and(pl.program_id(0) == 0, pl.program_id(1) == 0))
        def _():
            for cp in _exchange_copies(*exchange):
                cp.start()

        def chunk(r, carry):
            rows = pl.ds(pl.multiple_of(r * sm, sm), sm)
            av = a_ref[rows, :]
            for c0 in range(0, tn, sn):
                bv = b_ref[c0:c0 + sn, :] if mode == "nt" else b_ref[:, c0:c0 + sn]
                total = _dot(av, bv, dims)
                if has_res:
                    total = total + r_ref[rows, c0:c0 + sn]
                o_ref[rows, c0:c0 + sn] = total.astype(out_dtype)
            return carry

        lax.fori_loop(0, tm // sm, chunk, 0)

        @pl.when(jnp.logical_and(pl.program_id(0) == gm - 1, pl.program_id(1) == gn - 1))
        def _():
            for cp in _exchange_copies(*exchange):
                cp.wait()

    out = pl.pallas_call(
        body, name=name, grid=(gm, gn),
        out_shape=[jax.ShapeDtypeStruct((M, N), out_dtype)] + _exchange_shapes(scatter, True),
        in_specs=[a_spec, b_spec] + ([o_spec] if has_res else []) + [ANY] * n_ex,
        out_specs=[o_spec] + [ANY] * n_ex,
        scratch_shapes=_exchange_scratch(n_ex),
        compiler_params=_params("arbitrary", "arbitrary"),
    )(*((a, b, res) if has_res else (a, b)), *scatter)
    return out if n_ex else out[0]


def _matmul_tn(a, b, *, out_dtype, name, tk=512, sm=256):
    (K, M), (K2, N) = a.shape, b.shape
    assert K == K2 and K % tk == 0 and M % sm == 0, (a.shape, b.shape)
    nk = K // tk

    def body(a_ref, b_ref, o_ref, acc_ref):
        k = pl.program_id(0)

        @pl.when(k == 0)
        def _():
            acc_ref[...] = jnp.zeros_like(acc_ref)

        def chunk(mi, carry):
            cols = pl.ds(pl.multiple_of(mi * sm, sm), sm)
            acc_ref[cols, :] += _dot(a_ref[:, cols].T, b_ref[...])
            return carry

        lax.fori_loop(0, M // sm, chunk, 0)

        @pl.when(k == nk - 1)
        def _():
            def emit(mi, carry):
                rows = pl.ds(pl.multiple_of(mi * sm, sm), sm)
                o_ref[rows, :] = acc_ref[rows, :].astype(out_dtype)
                return carry
            lax.fori_loop(0, M // sm, emit, 0)

    return pl.pallas_call(
        body, name=name, grid=(nk,),
        out_shape=jax.ShapeDtypeStruct((M, N), out_dtype),
        in_specs=[pl.BlockSpec((tk, M), lambda k: (k, 0)), pl.BlockSpec((tk, N), lambda k: (k, 0))],
        out_specs=pl.BlockSpec((M, N), lambda k: (0, 0)),
        scratch_shapes=[pltpu.VMEM((M, N), F32)],
        compiler_params=_params("arbitrary"),
    )(a, b)


def _rmsnorm_fwd(x, gains, name, tr=256):
    n = len(gains)

    def body(*refs):
        x_ref = refs[0]
        xv = x_ref[...]
        r = lax.rsqrt(jnp.mean(xv * xv, axis=-1, keepdims=True) + RMS_EPS)
        y = xv * r
        for a in range(n):
            refs[1 + n + a][...] = (y * refs[1 + a][...]).astype(BF16)

    row = pl.BlockSpec((tr, D), lambda i: (i, 0))
    gain = pl.BlockSpec((1, D), lambda i: (0, 0))
    return pl.pallas_call(
        body, name=name, grid=(S // tr,),
        out_shape=[jax.ShapeDtypeStruct((S, D), BF16)] * n,
        in_specs=[row] + [gain] * n, out_specs=[row] * n,
        compiler_params=_params("parallel"),
    )(x, *gains)


def _rmsnorm_bwd(x, dy, g, dres, name, tr=256):
    def body(x_ref, dy_ref, g_ref, dres_ref, dx_ref, dxb_ref, dg_ref):
        xv = x_ref[...]
        dyv = dy_ref[...]
        r = lax.rsqrt(jnp.mean(xv * xv, axis=-1, keepdims=True) + RMS_EPS)
        xhat = xv * r
        dxhat = dyv * g_ref[...]
        mean_term = jnp.mean(dxhat * xhat, axis=-1, keepdims=True)
        dx = r * (dxhat - xhat * mean_term) + dres_ref[...]
        dx_ref[...] = dx
        dxb_ref[...] = dx.astype(BF16)
        part = jnp.sum(dyv * xhat, axis=0, keepdims=True)

        @pl.when(pl.program_id(0) == 0)
        def _():
            dg_ref[...] = part

        @pl.when(pl.program_id(0) > 0)
        def _():
            dg_ref[...] += part

    row = pl.BlockSpec((tr, D), lambda i: (i, 0))
    gain = pl.BlockSpec((1, D), lambda i: (0, 0))
    return pl.pallas_call(
        body, name=name, grid=(S // tr,),
        out_shape=[jax.ShapeDtypeStruct((S, D), F32), jax.ShapeDtypeStruct((S, D), BF16),
                   jax.ShapeDtypeStruct((1, D), F32)],
        in_specs=[row, row, gain, row], out_specs=[row, row, gain],
        compiler_params=_params("arbitrary"),
    )(x, dy, g, dres)


def _final_loss(x, target, g, name, tr=256):
    def body(x_ref, t_ref, g_ref, loss_ref, dx_ref, dxb_ref, dg_ref):
        xv = x_ref[...]
        gv = g_ref[...]
        r = lax.rsqrt(jnp.mean(xv * xv, axis=-1, keepdims=True) + RMS_EPS)
        xhat = xv * r
        err = xhat * gv - t_ref[...]
        row_loss = jnp.mean(err * err, axis=-1, keepdims=True)
        lpart = 0.5 * jnp.sum(row_loss, axis=0, keepdims=True)
        dyv = err / D
        dxhat = dyv * gv
        mean_term = jnp.mean(dxhat * xhat, axis=-1, keepdims=True)
        dx = r * (dxhat - xhat * mean_term)
        dx_ref[...] = dx
        dxb_ref[...] = dx.astype(BF16)
        gpart = jnp.sum(dyv * xhat, axis=0, keepdims=True)

        @pl.when(pl.program_id(0) == 0)
        def _():
            dg_ref[...] = gpart
            loss_ref[...] = jnp.broadcast_to(lpart, loss_ref.shape)

        @pl.when(pl.program_id(0) > 0)
        def _():
            dg_ref[...] += gpart
            loss_ref[...] += jnp.broadcast_to(lpart, loss_ref.shape)

    row = pl.BlockSpec((tr, D), lambda i: (i, 0))
    gain = pl.BlockSpec((1, D), lambda i: (0, 0))
    lspec = pl.BlockSpec((8, LANES), lambda i: (0, 0))
    return pl.pallas_call(
        body, name=name, grid=(S // tr,),
        out_shape=[jax.ShapeDtypeStruct((8, LANES), F32), jax.ShapeDtypeStruct((S, D), F32),
                   jax.ShapeDtypeStruct((S, D), BF16), jax.ShapeDtypeStruct((1, D), F32)],
        in_specs=[row, row, gain], out_specs=[lspec, row, row, gain],
        compiler_params=_params("arbitrary"),
    )(x, target, g)


CONV_TR = 128
CONV_TC = D_FF
CONV_NJ = D_FF // CONV_TC
HALO = 16


def _causal_taps(cur_ref, prev_ref, first):
    xv = cur_ref[...].astype(F32)
    pv = prev_ref[...].astype(F32)
    p1 = jnp.where(first, 0.0, pv[HALO - 1:HALO, :])
    p2 = jnp.where(first, 0.0, pv[HALO - 2:HALO - 1, :])
    r1, r2 = pltpu.roll(xv, 1, 0), pltpu.roll(xv, 2, 0)
    row = lax.broadcasted_iota(jnp.int32, (8, xv.shape[1]), 0)
    xm1 = jnp.concatenate([jnp.where(row == 0, p1, r1[0:8]), r1[8:]], axis=0)
    xm2 = jnp.concatenate([jnp.where(row == 0, p2, jnp.where(row == 1, p1, r2[0:8])), r2[8:]], axis=0)
    return xv, xm1, xm2


def _conv_specs():
    def prev_row(i):
        return jnp.maximum(i * (CONV_TR // HALO) - 1, 0)
    ua = pl.BlockSpec((CONV_TR, CONV_TC), lambda i, j: (i, j))
    ug = pl.BlockSpec((CONV_TR, CONV_TC), lambda i, j: (i, j + CONV_NJ))
    pa = pl.BlockSpec((HALO, CONV_TC), lambda i, j: (prev_row(i), j))
    pg = pl.BlockSpec((HALO, CONV_TC), lambda i, j: (prev_row(i), j + CONV_NJ))
    wa = pl.BlockSpec((3, CONV_TC), lambda i, j: (0, j))
    wg = pl.BlockSpec((3, CONV_TC), lambda i, j: (0, j + CONV_NJ))
    ba = pl.BlockSpec((1, CONV_TC), lambda i, j: (0, j))
    bg = pl.BlockSpec((1, CONV_TC), lambda i, j: (0, j + CONV_NJ))
    return [ua, pa, ug, pg, wa, wg, ba, bg]


def _convgate_fwd(u, w, b, name):
    def body(ua, pa, ug, pg, wa, wg, ba, bg, o_ref):
        first = pl.program_id(0) == 0
        x0, x1, x2 = _causal_taps(ua, pa, first)
        ac = wa[0:1, :] * x2 + wa[1:2, :] * x1 + wa[2:3, :] * x0 + ba[...]
        x0, x1, x2 = _causal_taps(ug, pg, first)
        gc = wg[0:1, :] * x2 + wg[1:2, :] * x1 + wg[2:3, :] * x0 + bg[...]
        sg = 1.0 / (1.0 + jnp.exp(-gc))
        o_ref[...] = (gc * sg * ac).astype(BF16)

    return pl.pallas_call(
        body, name=name, grid=(S // CONV_TR, CONV_NJ),
        out_shape=jax.ShapeDtypeStruct((S, D_FF), BF16),
        in_specs=_conv_specs(),
        out_specs=pl.BlockSpec((CONV_TR, CONV_TC), lambda i, j: (i, j)),
        compiler_params=_params("parallel", "parallel"),
    )(u, u, u, u, w, w, b, b)


def _anticausal_conv(d, nxt_ref, w_ref, last):
    n1 = jnp.where(last, 0.0, nxt_ref[0:1, :])
    n2 = jnp.where(last, 0.0, nxt_ref[1:2, :])
    r1, r2 = pltpu.roll(d, CONV_TR - 1, 0), pltpu.roll(d, CONV_TR - 2, 0)
    row = lax.broadcasted_iota(jnp.int32, (8, d.shape[1]), 0)
    cut = CONV_TR - 8
    dp1 = jnp.concatenate([r1[:cut], jnp.where(row == 7, n1, r1[cut:])], axis=0)
    dp2 = jnp.concatenate([r2[:cut], jnp.where(row == 7, n2, jnp.where(row == 6, n1, r2[cut:]))], axis=0)
    return w_ref[2:3, :] * d + w_ref[1:2, :] * dp1 + w_ref[0:1, :] * dp2


def _convgate_bwd(u, w, b, dact, name):
    n_i = S // CONV_TR

    def body(ua, pa, ug, pg, wa, wg, ba, bg, d_ref, dua_ref, dug_ref, dwa_ref, dwg_ref, dba_ref, dbg_ref,
             nxt_a, nxt_g):
        i = pl.program_id(1)
        last = i == 0
        first = i == n_i - 1
        a0, a1, a2 = _causal_taps(ua, pa, first)
        ac = wa[0:1, :] * a2 + wa[1:2, :] * a1 + wa[2:3, :] * a0 + ba[...]
        g0, g1, g2 = _causal_taps(ug, pg, first)
        gc = wg[0:1, :] * g2 + wg[1:2, :] * g1 + wg[2:3, :] * g0 + bg[...]
        sg = 1.0 / (1.0 + jnp.exp(-gc))
        dact_v = d_ref[...].astype(F32)
        da = dact_v * (gc * sg)
        dg = dact_v * ac * (sg * (1.0 + gc * (1.0 - sg)))
        dua_ref[...] = _anticausal_conv(da, nxt_a, wa, last).astype(BF16)
        dug_ref[...] = _anticausal_conv(dg, nxt_g, wg, last).astype(BF16)
        nxt_a[...] = da[0:8]
        nxt_g[...] = dg[0:8]

        def col(v):
            return jnp.sum(v, axis=0, keepdims=True)

        parts = [col(da * a2), col(da * a1), col(da * a0), col(dg * g2), col(dg * g1), col(dg * g0),
                 col(da), col(dg)]

        @pl.when(last)
        def _():
            for k in range(3):
                dwa_ref[k:k + 1, :] = parts[k]
                dwg_ref[k:k + 1, :] = parts[3 + k]
            dba_ref[...] = parts[6]
            dbg_ref[...] = parts[7]

        @pl.when(i > 0)
        def _():
            for k in range(3):
                dwa_ref[k:k + 1, :] += parts[k]
                dwg_ref[k:k + 1, :] += parts[3 + k]
            dba_ref[...] += parts[6]
            dbg_ref[...] += parts[7]

    def swap(spec):
        return pl.BlockSpec(spec.block_shape, lambda j, i, f=spec.index_map: f(n_i - 1 - i, j))

    blk = pl.BlockSpec((CONV_TR, CONV_TC), lambda j, i: (n_i - 1 - i, j))
    w3 = pl.BlockSpec((3, CONV_TC), lambda j, i: (0, j))
    b1 = pl.BlockSpec((1, CONV_TC), lambda j, i: (0, j))
    return pl.pallas_call(
        body, name=name, grid=(CONV_NJ, n_i),
        out_shape=[jax.ShapeDtypeStruct((S, D_FF), BF16), jax.ShapeDtypeStruct((S, D_FF), BF16),
                   jax.ShapeDtypeStruct((3, D_FF), F32), jax.ShapeDtypeStruct((3, D_FF), F32),
                   jax.ShapeDtypeStruct((1, D_FF), F32), jax.ShapeDtypeStruct((1, D_FF), F32)],
        in_specs=[swap(s) for s in _conv_specs()] + [blk],
        out_specs=[blk, blk, w3, w3, b1, b1],
        scratch_shapes=[pltpu.VMEM((8, CONV_TC), F32), pltpu.VMEM((8, CONV_TC), F32)],
        compiler_params=_params("arbitrary", "arbitrary"),
    )(u, u, u, u, w, w, b, b, dact)


FOX_T = 256
FOX_TQ, FOX_TK = 256, 256
N_PAIRS = A_HEADS // 2


def _lane_masks():
    lane = lax.broadcasted_iota(jnp.int32, (1, LANES), 1)
    return lane, (lane < HEAD_DIM, lane >= HEAD_DIM)


def _fox_prep_fwd(z_t, b, name):
    def body(z_ref, b_ref, c_ref):
        r = lax.broadcasted_iota(jnp.int32, (LANES, LANES), 0)
        cc = lax.broadcasted_iota(jnp.int32, (LANES, LANES), 1)
        upper = (r <= cc).astype(BF16)
        carry = jnp.zeros((A_HEADS, 1), F32)
        for blk in range(S // LANES):
            sl = slice(blk * LANES, (blk + 1) * LANES)
            z = z_ref[:, sl] + b_ref[...]
            lf = jnp.minimum(z, 0.0) - jnp.log(1.0 + jnp.exp(-jnp.abs(z)))
            cs = _split_dot(lf, upper, 3) + carry
            c_ref[:, sl] = cs
            carry = cs[:, LANES - 1:LANES]

    return pl.pallas_call(
        body, name=name, out_shape=jax.ShapeDtypeStruct((A_HEADS, S), F32),
        compiler_params=_params(),
    )(z_t, b)


def _fox_prep_bwd(drow_t, dcol_t, z_t, b, name):
    def body(dr_ref, dc_ref, z_ref, b_ref, dz_ref, db_ref):
        r = lax.broadcasted_iota(jnp.int32, (LANES, LANES), 0)
        cc = lax.broadcasted_iota(jnp.int32, (LANES, LANES), 1)
        lower = (r >= cc).astype(BF16)
        carry = jnp.zeros((A_HEADS, 1), F32)
        db = jnp.zeros((A_HEADS, 1), F32)
        for blk in reversed(range(S // LANES)):
            sl = slice(blk * LANES, (blk + 1) * LANES)
            rc = _split_dot(dr_ref[:, sl] - dc_ref[:, sl], lower, 3) + carry
            carry = rc[:, 0:1]
            z = z_ref[:, sl] + b_ref[...]
            dz = rc / (1.0 + jnp.exp(z))
            dz_ref[:, sl] = dz
            db = db + jnp.sum(dz, axis=1, keepdims=True)
        db_ref[...] = db

    return pl.pallas_call(
        body, name=name,
        out_shape=[jax.ShapeDtypeStruct((A_HEADS, S), F32), jax.ShapeDtypeStruct((A_HEADS, 1), F32)],
        compiler_params=_params(),
    )(drow_t, dcol_t, z_t, b)


def _fox_fwd(qkv, c_t2, name, gather):
    tq, tk = FOX_TQ, FOX_TK

    n = len(gather)

    def body(*refs):
        q_ref, k_ref, v_ref, ct_ref = refs[:4]
        o_ref, lse_ref = refs[4 + n:6 + n]
        exchange = (refs[4:4 + n], refs[6 + n:6 + 2 * n], refs[6 + 2 * n:], False)
        qi = pl.program_id(1)

        @pl.when(jnp.logical_and(pl.program_id(0) == 0, qi == 0))
        def _():
            for cp in _exchange_copies(*exchange):
                cp.start()

        n_full =jnp.right_shift(qi, (tk // tq).bit_length() - 1)
        lane, masks = _lane_masks()
        q = q_ref[...] * SCALE
        qs = [jnp.where(masks[e], q, jnp.zeros_like(q)) for e in range(2)]

        def scores(j):
            start = pl.multiple_of(j * tk, tk)
            kb = k_ref[pl.ds(start, tk), :]
            return tuple(_dot(qs[e], kb, NT_DIMS) - ct_ref[e:e + 1, pl.ds(start, tk)] for e in range(2))

        def softmax(s, m, masked):
            if masked:
                rows = lax.broadcasted_iota(jnp.int32, (tq, tk), 0) + (qi * tq - n_full * tk)
                cols = lax.broadcasted_iota(jnp.int32, (tq, tk), 1)
                s = tuple(jnp.where(cols <= rows, s[e], NEG) for e in range(2))
            m_new = tuple(jnp.maximum(m[e], jnp.max(s[e], axis=1, keepdims=True)) for e in range(2))
            p = tuple(jnp.exp(s[e] - m_new[e]).astype(BF16) for e in range(2))
            alpha = tuple(jnp.exp(m[e] - m_new[e]) for e in range(2))
            return m_new, p, alpha

        def weighted_values(j, p, alpha, acc):
            start = pl.multiple_of(j * tk, tk)
            vb = v_ref[pl.ds(start, tk), :]
            return tuple(alpha[e] * acc[e] + _dot(p[e], jnp.where(masks[e], vb, jnp.ones_like(vb)))
                         for e in range(2))

        def step(j, carry):
            s, p_prev, a_prev, m, acc = carry
            s_next = scores(j + 1)
            acc = weighted_values(jnp.maximum(j - 1, 0), p_prev, a_prev, acc)
            m, p, alpha = softmax(s, m, False)
            return s_next, p, alpha, m, acc

        two = lambda x: (x, x)
        init = (scores(0), two(jnp.zeros((tq, tk), BF16)), two(jnp.ones((tq, 1), F32)),
                two(jnp.full((tq, 1), NEG, F32)), two(jnp.zeros((tq, LANES), F32)))
        s, p_prev, a_prev, m, acc = lax.fori_loop(0, n_full, step, init)
        acc = weighted_values(jnp.maximum(n_full - 1, 0), p_prev, a_prev, acc)
        (m0, m1), p, alpha = softmax(s, m, True)
        acc0, acc1 = weighted_values(n_full, p, alpha, acc)
        l0 = acc0[:, HEAD_DIM:HEAD_DIM + 1]
        l1 = acc1[:, 0:1]
        o_ref[...] = jnp.where(masks[0], acc0 / l0, acc1 / l1).astype(BF16)
        lse_ref[...] = jnp.where(masks[0], m0 + jnp.log(l0), m1 + jnp.log(l1))

        @pl.when(jnp.logical_and(pl.program_id(0) == N_PAIRS - 1, qi == S // tq - 1))
        def _():
            for cp in _exchange_copies(*exchange):
                cp.wait()

    qspec = pl.BlockSpec((tq, LANES), lambda h, i: (i, h))
    return pl.pallas_call(
        body, name=name, grid=(N_PAIRS, S // tq),
        out_shape=[jax.ShapeDtypeStruct((S, D), BF16), jax.ShapeDtypeStruct((S, D), F32)]
        + _exchange_shapes(gather, False),
        in_specs=[qspec,
                  pl.BlockSpec((S, LANES), lambda h, i: (0, N_PAIRS + h)),
                  pl.BlockSpec((S, LANES), lambda h, i: (0, 2 * N_PAIRS + h)),
                  pl.BlockSpec((None, 2, S), lambda h, i: (h, 0, 0))] + [ANY] * n,
        out_specs=[qspec, qspec] + [ANY] * n,
        scratch_shapes=_exchange_scratch(n),
        compiler_params=_params("arbitrary", "arbitrary"),
    )(qkv, qkv, qkv, c_t2, *gather)


def _head_rowsum(a, b, name, tr=256):
    C = a.shape[1]

    def body(a_ref, b_ref, o_ref):
        r = lax.broadcasted_iota(jnp.int32, (LANES, LANES), 0) < HEAD_DIM
        cc = lax.broadcasted_iota(jnp.int32, (LANES, LANES), 1) < HEAD_DIM
        same_head = (r == cc).astype(BF16)
        for blk in range(C // LANES):
            sl = slice(blk * LANES, (blk + 1) * LANES)
            prod = a_ref[:, sl].astype(F32) * b_ref[:, sl].astype(F32)
            o_ref[:, sl] = _split_dot(prod, same_head, 2)

    row = pl.BlockSpec((tr, C), lambda i: (i, 0))
    return pl.pallas_call(
        body, name=name, grid=(S // tr,), out_shape=jax.ShapeDtypeStruct((S, C), F32),
        in_specs=[row, row], out_specs=row, compiler_params=_params("parallel"),
    )(a, b)


def _fox_bwd(qkv, do, lse, delta, c_t2, name, scatter):
    t = FOX_T
    nq = S // t

    n = len(scatter)

    def body(*refs):
        q_ref, k_ref, v_ref, do_ref, lse_ref, dl_ref, ct_ref = refs[:7]
        dq_ref, dk_ref, dv_ref, dcol_ref, drow_ref = refs[7 + n:12 + n]
        exchange = (refs[7:7 + n], refs[12 + n:12 + 2 * n], refs[12 + 2 * n:], True)
        kj = pl.program_id(1)

        @pl.when(jnp.logical_and(pl.program_id(0) == 0, kj == 0))
        def _():
            for cp in _exchange_copies(*exchange):
                cp.start()

        @pl.when(kj == 0)
        def _():
            dq_ref[...] = jnp.zeros_like(dq_ref)
            drow_ref[...] = jnp.zeros_like(drow_ref)

        lane, masks = _lane_masks()
        k = k_ref[...]
        v = v_ref[...]
        k_aug = [jnp.where(masks[e], k * SCALE, jnp.ones_like(k)) for e in range(2)]
        cs = [ct_ref[e:e + 1, :] for e in range(2)]

        def rows_of(i):
            r0 = pl.multiple_of(i * t, t)
            return pl.ds(r0, t), q_ref[pl.ds(r0, t), :] * SCALE, do_ref[pl.ds(r0, t), :]

        def scores(i):
            _, qb, dob = rows_of(i)
            out = []
            for e in range(2):
                qe = jnp.where(masks[e], qb, jnp.zeros_like(qb))
                doe = jnp.where(masks[e], dob, jnp.zeros_like(dob))
                out.append((_dot(qe, k, NT_DIMS) - cs[e], _dot(doe, v, NT_DIMS)))
            return tuple(out)

        def pointwise(i, sd, masked):
            rows, _, _ = rows_of(i)
            out = []
            for e in range(2):
                lo = e * HEAD_DIM
                s, dp = sd[e]
                if masked:
                    r = lax.broadcasted_iota(jnp.int32, (t, t), 0)
                    c = lax.broadcasted_iota(jnp.int32, (t, t), 1)
                    s = jnp.where(c <= r, s, NEG)
                p = jnp.exp(s - lse_ref[rows, lo:lo + 1])
                out.append((p.astype(BF16), (p * (dp - dl_ref[rows, lo:lo + 1])).astype(BF16)))
            return tuple(out)

        def accumulate(i, pd, acc):
            rows, qb, dob = rows_of(i)
            dk_acc, dv_acc = list(acc[0]), acc[1]
            dq_parts = []
            for e in range(2):
                p, ds = pd[e]
                q_aug = jnp.where(masks[e], qb, jnp.ones_like(qb))
                doe = jnp.where(masks[e], dob, jnp.zeros_like(dob))
                dv_acc = dv_acc + _dot(p, doe, TN_DIMS)
                dk_acc[e] = dk_acc[e] + _dot(ds, q_aug, TN_DIMS)
                dq_parts.append(_dot(ds, k_aug[e]))
            dq_ref[rows, :] += jnp.where(masks[0], dq_parts[0], dq_parts[1])
            drow_ref[rows, :] += jnp.where(masks[0], dq_parts[1], dq_parts[0])
            return tuple(dk_acc), dv_acc

        def step(i, carry):
            pd_prev, acc = carry
            sd = scores(i)
            acc = accumulate(i - 1, pd_prev, acc)
            return pointwise(i, sd, False), acc

        zero = jnp.zeros((t, LANES), F32)
        carry = (pointwise(kj, scores(kj), True), ((zero, zero), zero))
        pd, acc = lax.fori_loop(kj + 1, nq, step, carry)
        (dk0, dk1), dv = accumulate(nq - 1, pd, acc)
        dk_ref[...] = jnp.where(masks[0], dk0, dk1).astype(BF16)
        dcol_ref[...] = jnp.where(masks[0], dk1, dk0)
        dv_ref[...] = dv.astype(BF16)

        @pl.when(jnp.logical_and(pl.program_id(0) == N_PAIRS - 1, kj == nq - 1))
        def _():
            for cp in _exchange_copies(*exchange):
                cp.wait()

    full = lambda off: pl.BlockSpec((S, LANES), lambda h, j, off=off: (0, off + h))
    kv = lambda off: pl.BlockSpec((t, LANES), lambda h, j, off=off: (j, off + h))
    return pl.pallas_call(
        body, name=name, grid=(N_PAIRS, nq),
        out_shape=[jax.ShapeDtypeStruct((S, D), F32), jax.ShapeDtypeStruct((S, D), BF16),
                   jax.ShapeDtypeStruct((S, D), BF16), jax.ShapeDtypeStruct((S, D), F32),
                   jax.ShapeDtypeStruct((S, D), F32)] + _exchange_shapes(scatter, True),
        in_specs=[full(0), kv(N_PAIRS), kv(2 * N_PAIRS), full(0), full(0), full(0),
                  pl.BlockSpec((None, 2, t), lambda h, j: (h, 0, j))] + [ANY] * n,
        out_specs=[full(0), kv(0), kv(0), kv(0), full(0)] + [ANY] * n,
        scratch_shapes=_exchange_scratch(n),
        compiler_params=_params("arbitrary", "arbitrary"),
    )(qkv, qkv, qkv, do, lse, delta, c_t2, *scatter)


B_PAIRS = 4
B_NB = S // B_W


def _group_consts(g):
    nbs = jnp.where(g == 0, B_NB // B_DILS[0], jnp.where(g == 1, B_NB // B_DILS[1], B_NB // B_DILS[2]))
    dil = jnp.where(g == 0, B_DILS[0], jnp.where(g == 1, B_DILS[1], B_DILS[2]))
    return nbs, dil


def _band(dil):
    qi = lax.broadcasted_iota(jnp.int32, (B_W, B_W), 0)
    kj = lax.broadcasted_iota(jnp.int32, (B_W, B_W), 1)
    dist_c = qi - kj
    dist_p = qi + B_W - kj
    return (dist_c * dil).astype(F32), dist_c >= 0, (dist_p * dil).astype(F32), dist_p <= B_W


def _dil_fwd(qp, kp, vp, slopes, name):
    def body(sl_ref, q_ref, kp_ref, kc_ref, vp_ref, vc_ref, o_ref, lse_ref):
        g, n = pl.program_id(0), pl.program_id(1)
        nbs, dil = _group_consts(g)
        has_prev = (n % nbs) != 0
        lane, masks = _lane_masks()
        bias_c, ok_c, bias_p, ok_p = _band(dil)
        ok_p = jnp.logical_and(ok_p, has_prev)
        heads = [(hp, e) for hp in range(B_PAIRS) for e in range(2)]
        col = lambda ref, hp: ref[:, hp * LANES:(hp + 1) * LANES]
        logits = []
        for hp, e in heads:
            q = col(q_ref, hp) * SCALE
            qe = jnp.where(masks[e], q, jnp.zeros_like(q))
            logits.append((_dot(qe, col(kc_ref, hp), NT_DIMS), _dot(qe, col(kp_ref, hp), NT_DIMS)))
        probs = []
        for (hp, e), (sc, sp) in zip(heads, logits):
            slope = sl_ref[g * 8 + 2 * hp + e]
            sc = jnp.where(ok_c, sc - slope * bias_c, NEG)
            sp = jnp.where(ok_p, sp - slope * bias_p, NEG)
            m = jnp.maximum(jnp.max(sc, axis=1, keepdims=True), jnp.max(sp, axis=1, keepdims=True))
            probs.append((jnp.exp(sc - m).astype(BF16), jnp.exp(sp - m).astype(BF16), m))
        outs, lses = [], []
        for (hp, e), (pc, pp, m) in zip(heads, probs):
            vc, vpv = col(vc_ref, hp), col(vp_ref, hp)
            acc = (_dot(pc, jnp.where(masks[e], vc, jnp.ones_like(vc)))
                   + _dot(pp, jnp.where(masks[e], vpv, jnp.ones_like(vpv))))
            l = acc[:, HEAD_DIM:HEAD_DIM + 1] if e == 0 else acc[:, 0:1]
            outs.append(acc / l)
            lses.append(m + jnp.log(l))
        o_ref[...] = jnp.concatenate(
            [jnp.where(masks[0], outs[2 * hp], outs[2 * hp + 1]) for hp in range(B_PAIRS)], axis=1)
        lse_ref[...] = jnp.concatenate(
            [jnp.where(masks[0], lses[2 * hp], lses[2 * hp + 1]) for hp in range(B_PAIRS)], axis=1)

    cur = pl.BlockSpec((None, B_W, B_OUT), lambda g, n, sl: (g, n, 0))
    prev = pl.BlockSpec((None, B_W, B_OUT), lambda g, n, sl: (g, jnp.maximum(n - 1, 0), 0))
    return pl.pallas_call(
        body, name=name,
        grid_spec=pltpu.PrefetchScalarGridSpec(
            num_scalar_prefetch=1, grid=(3, B_NB),
            in_specs=[cur, prev, cur, prev, cur], out_specs=[cur, cur]),
        out_shape=[jax.ShapeDtypeStruct((3, S, B_OUT), F32), jax.ShapeDtypeStruct((3, S, B_OUT), F32)],
        compiler_params=_params("parallel", "parallel"),
    )(slopes, qp, kp, kp, vp, vp)


def _dil_merge(og, lseg, name, tr=256):
    def body(o_ref, l_ref, out_ref, lse_ref):
        l0, l1, l2 = l_ref[0], l_ref[1], l_ref[2]
        m = jnp.maximum(jnp.maximum(l0, l1), l2)
        w0, w1, w2 = jnp.exp(l0 - m), jnp.exp(l1 - m), jnp.exp(l2 - m)
        den = w0 + w1 + w2
        out_ref[...] = ((w0 * o_ref[0] + w1 * o_ref[1] + w2 * o_ref[2]) / den).astype(BF16)
        lse_ref[...] = m + jnp.log(den)

    blk3 = pl.BlockSpec((3, tr, B_OUT), lambda i: (0, i, 0))
    blk = pl.BlockSpec((tr, B_OUT), lambda i: (i, 0))
    return pl.pallas_call(
        body, name=name, grid=(S // tr,),
        out_shape=[jax.ShapeDtypeStruct((S, B_OUT), BF16), jax.ShapeDtypeStruct((S, B_OUT), F32)],
        in_specs=[blk3, blk3], out_specs=[blk, blk], compiler_params=_params("parallel"),
    )(og, lseg)


def _dil_bwd(qp, kp, vp, dop, lsep, dlp, slopes, name, scatter):
    n_ex = len(scatter)

    def body(sl_ref, *refs):
        (qc_ref, qn_ref, kp_ref, kc_ref, vp_ref, vc_ref, doc_ref, don_ref,
         lc_ref, ln_ref, dc_ref, dn_ref) = refs[:12]
        dq_ref, dk_ref, dv_ref = refs[12 + n_ex:15 + n_ex]
        exchange = (refs[12:12 + n_ex], refs[15 + n_ex:15 + 2 * n_ex], refs[15 + 2 * n_ex:], True)
        g, n = pl.program_id(0), pl.program_id(1)

        @pl.when(jnp.logical_and(g == 0, n == 0))
        def _():
            for cp in _exchange_copies(*exchange):
                cp.start()

        nbs, dil = _group_consts(g)
        has_prev = (n % nbs) != 0
        has_next = jnp.logical_and(n + 1 < B_NB, ((n + 1) % nbs) != 0)
        lane, masks = _lane_masks()
        bias_c, ok_c, bias_p, ok_p = _band(dil)
        ok_pp = jnp.logical_and(ok_p, has_prev)
        ok_np = jnp.logical_and(ok_p, has_next)
        heads = [(hp, e) for hp in range(B_PAIRS) for e in range(2)]
        col = lambda ref, hp: ref[:, hp * LANES:(hp + 1) * LANES]
        mask = lambda t, e: jnp.where(masks[e], t, jnp.zeros_like(t))
        raw = []
        for hp, e in heads:
            qce, qne = mask(col(qc_ref, hp) * SCALE, e), mask(col(qn_ref, hp) * SCALE, e)
            doce, done = mask(col(doc_ref, hp), e), mask(col(don_ref, hp), e)
            kc, kpv, vc, vpv = col(kc_ref, hp), col(kp_ref, hp), col(vc_ref, hp), col(vp_ref, hp)
            raw.append(((_dot(qce, kc, NT_DIMS), _dot(doce, vc, NT_DIMS)),
                        (_dot(qce, kpv, NT_DIMS), _dot(doce, vpv, NT_DIMS)),
                        (_dot(qne, kc, NT_DIMS), _dot(done, vc, NT_DIMS))))
        pds = []
        for (hp, e), tiles in zip(heads, raw):
            lo = hp * LANES + e * HEAD_DIM
            slope = sl_ref[g * 8 + 2 * hp + e]
            lse_c, dl_c = lc_ref[:, lo:lo + 1], dc_ref[:, lo:lo + 1]
            lse_n, dl_n = ln_ref[:, lo:lo + 1], dn_ref[:, lo:lo + 1]
            out = []
            for (s, dp), ok, bias, lse, dl in ((tiles[0], ok_c, bias_c, lse_c, dl_c),
                                               (tiles[1], ok_pp, bias_p, lse_c, dl_c),
                                               (tiles[2], ok_np, bias_p, lse_n, dl_n)):
                p = jnp.exp(jnp.where(ok, s - slope * bias, NEG) - lse)
                out.append((p.astype(BF16), (p * (dp - dl)).astype(BF16)))
            pds.append(out)
        dq_all, dk_all, dv_all = [], [], []
        for hp in range(B_PAIRS):
            dq = jnp.zeros((B_W, LANES), F32)
            dk = jnp.zeros((B_W, LANES), F32)
            dv = jnp.zeros((B_W, LANES), F32)
            for e in range(2):
                (p_c, ds_c), (_, ds_p), (p_n, ds_n) = pds[2 * hp + e]
                qce, qne = mask(col(qc_ref, hp) * SCALE, e), mask(col(qn_ref, hp) * SCALE, e)
                doce, done = mask(col(doc_ref, hp), e), mask(col(don_ref, hp), e)
                dq = dq + _dot(ds_c, mask(col(kc_ref, hp) * SCALE, e)) + _dot(ds_p, mask(col(kp_ref, hp) * SCALE, e))
                dk = dk + _dot(ds_c, qce, TN_DIMS) + _dot(ds_n, qne, TN_DIMS)
                dv = dv + _dot(p_c, doce, TN_DIMS) + _dot(p_n, done, TN_DIMS)
            dq_all.append(dq)
            dk_all.append(dk)
            dv_all.append(dv)
        dq_ref[...] = jnp.concatenate(dq_all, axis=1).astype(BF16)
        dk_ref[...] = jnp.concatenate(dk_all, axis=1).astype(BF16)
        dv_ref[...] = jnp.concatenate(dv_all, axis=1).astype(BF16)

        @pl.when(jnp.logical_and(g == 2, n == B_NB - 1))
        def _():
            for cp in _exchange_copies(*exchange):
                cp.wait()

    cur = pl.BlockSpec((None, B_W, B_OUT), lambda g, n, sl: (g, n, 0))
    prev = pl.BlockSpec((None, B_W, B_OUT), lambda g, n, sl: (g, jnp.maximum(n - 1, 0), 0))
    nxt = pl.BlockSpec((None, B_W, B_OUT), lambda g, n, sl: (g, jnp.minimum(n + 1, B_NB - 1), 0))
    return pl.pallas_call(
        body, name=name,
        grid_spec=pltpu.PrefetchScalarGridSpec(
            num_scalar_prefetch=1, grid=(3, B_NB),
            in_specs=[cur, nxt, prev, cur, prev, cur, cur, nxt, cur, nxt, cur, nxt] + [ANY] * n_ex,
            out_specs=[cur, cur, cur] + [ANY] * n_ex,
            scratch_shapes=_exchange_scratch(n_ex)),
        out_shape=[jax.ShapeDtypeStruct((3, S, B_OUT), BF16)] * 3 + _exchange_shapes(scatter, True),
        compiler_params=_params("arbitrary", "arbitrary"),
    )(slopes, qp, qp, kp, kp, vp, vp, dop, dop, lsep, lsep, dlp, dlp, *scatter)


def _rows_block(shape, max_bytes=2 * 1024 * 1024):
    rows, cols = shape
    padded_cols = -(-cols // LANES) * LANES
    for tr in (1024, 512, 256, 128, 64, 32, 16):
        if rows % tr == 0 and tr * padded_cols * 4 <= max_bytes:
            return tr
    return rows


def _adam_update(w, m, v, g):
    m_new = ADAM_B1 * m + (1.0 - ADAM_B1) * g
    v_new = ADAM_B2 * v + (1.0 - ADAM_B2) * (g * g)
    m_hat = m_new / (1.0 - ADAM_B1 ** ADAM_STEP)
    v_hat = v_new / (1.0 - ADAM_B2 ** ADAM_STEP)
    delta = -ADAM_LR * (m_hat / (jnp.sqrt(v_hat) + ADAM_EPS) + ADAM_WD * w)
    return delta, m_new, v_new


def _adamw_sharded(w, m, v, parts, name):
    R, C = w.shape
    tr = _rows_block((R, C), max_bytes=1024 * 1024)

    def body(w_ref, m_ref, v_ref, p_ref, g_ref, d_ref, mo_ref, vo_ref):
        g = p_ref[0].astype(F32)
        for dev in range(1, N_DEV):
            g = g + p_ref[dev].astype(F32)
        g_ref[...] = g
        d_ref[...], mo_ref[...], vo_ref[...] = _adam_update(w_ref[...], m_ref[...], v_ref[...], g)

    blk = pl.BlockSpec((tr, C), lambda i: (i, 0))
    out = jax.ShapeDtypeStruct((R, C), F32)
    return pl.pallas_call(
        body, name=name, grid=(R // tr,),
        in_specs=[blk, blk, blk, pl.BlockSpec((N_DEV, tr, C), lambda i: (0, i, 0))],
        out_specs=[blk, blk, blk, blk], out_shape=[out, out, out, out],
        compiler_params=_params("parallel"),
    )(w, m, v, parts)


def _adamw_replicated(w, m, v, parts, name):
    def body(w_ref, m_ref, v_ref, p_ref, g_ref, d_ref, mo_ref, vo_ref):
        g = p_ref[0]
        for dev in range(1, N_DEV):
            g = g + p_ref[dev]
        g_ref[...] = g
        d_ref[...], mo_ref[...], vo_ref[...] = _adam_update(w_ref[...], m_ref[...], v_ref[...], g)

    out = jax.ShapeDtypeStruct(w.shape, F32)
    return pl.pallas_call(body, name=name, out_shape=[out, out, out, out], compiler_params=_params())(w, m, v, parts)


def _cols_from_slots(g):
    return g.transpose(1, 0, 2).reshape(g.shape[1], N_DEV * g.shape[2])


def _cols_to_slots(w):
    k, n = w.shape
    return w.reshape(k, N_DEV, n // N_DEV).transpose(1, 0, 2)


def _permute(t, dil):
    c = t.shape[1]
    return t.reshape(S // dil, dil, c).transpose(1, 0, 2).reshape(S, c)


def _unpermute(t, dil):
    c = t.shape[1]
    return t.reshape(dil, S // dil, c).transpose(1, 0, 2).reshape(S, c)


def _group_permute(t):
    return jnp.stack([_permute(t[:, g * B_OUT:(g + 1) * B_OUT], B_DILS[g]) for g in range(3)])


def _same_permute(t):
    return jnp.stack([_permute(t, d) for d in B_DILS])


def _group_unpermute(t):
    return jnp.stack([_unpermute(t[g], B_DILS[g]) for g in range(3)])


SMALL_ROWS = 144


def _pack_small(a_b_f, kv_g, mix_g, ffn_g, conv_b, fin_g):
    flat = jnp.concatenate([a_b_f.reshape(-1), kv_g.reshape(-1), mix_g.reshape(-1), ffn_g.reshape(-1),
                            conv_b.reshape(-1), fin_g.reshape(-1)])
    return jnp.pad(flat, (0, SMALL_ROWS * LANES - flat.shape[0])).reshape(SMALL_ROWS, LANES)


def _unpack_small(p):
    flat = p.reshape(-1)
    out, off = [], 0
    for shape in ((1, A_HEADS), (D,), (2, D), (2, D), (2, 2 * D_FF), (D,)):
        size = math.prod(shape)
        out.append(flat[off:off + size].reshape(shape))
        off += size
    return out


def _unpack_late(g):
    w_up = g[4].reshape(N_DEV, 2, D, -1).transpose(1, 2, 0, 3).reshape(2, D, 2 * D_FF)
    w_down = g[5].reshape(N_DEV, 2, -1, D).transpose(1, 0, 2, 3).reshape(2, D_FF, D)
    conv_w = g[6].reshape(N_DEV, 2, 3, -1).transpose(1, 2, 0, 3).reshape(2, 3, 2 * D_FF)
    return (g[0].reshape(D, D), _cols_from_slots(g[1]), _cols_from_slots(g[2]), _cols_from_slots(g[3]),
            w_up, w_down, conv_w)


def _ffn_slots(dw_up, dw_down, dconv_w):
    return [_cols_to_slots(dw_up), dw_down.reshape(N_DEV, -1, D), _cols_to_slots(dconv_w)]


def _local_step(x0, target, w_in_pad, late_shards,
                a_b_f, kv_norm_g, mix_norm_g, ffn_norm_g, ffn_conv_b, final_norm_g):
    w_qkv, w_f = w_in_pad[:, :A_QKV], w_in_pad[:, A_QKV:]
    conv_b = ffn_conv_b.reshape(2, 1, 2 * D_FF)
    slopes = jnp.exp2(-8.0 * jnp.arange(1, 25, dtype=F32) / 24)

    def gain(g):
        return g.reshape(1, D)

    (h1,) = _rmsnorm_fwd(x0, [gain(mix_norm_g[0])], "norm_mix0")
    qkv = _matmul(h1, w_qkv, mode="nn", out_dtype=BF16, name="proj_qkv", tm=512, tn=A_QKV)
    z = _matmul(h1, w_f, mode="nn", out_dtype=F32, name="proj_gate", tm=S, tn=LANES)
    z_t = z[:, :A_HEADS].T
    b_f = a_b_f.reshape(A_HEADS, 1)
    c_t = _fox_prep_fwd(z_t, b_f, "fox_prep")
    c_t2 = c_t.reshape(N_PAIRS, 2, S)
    o_a, lse_a, *late = _fox_fwd(qkv, c_t2, "fox_fwd", late_shards)
    w_out, w_q, w_bo, w_kvf, w_up, w_down, conv_w = _unpack_late(late)
    x1 = _matmul(o_a, w_out, mode="nn", out_dtype=F32, name="a_out", tm=512, tn=D, res=x0)

    def ffn_fwd(xin, layer):
        (h,) = _rmsnorm_fwd(xin, [gain(ffn_norm_g[layer])], f"norm_ffn{layer}")
        u = _matmul(h, w_up[layer], mode="nn", out_dtype=BF16, name=f"ffn_up{layer}", tm=512, tn=2 * D_FF)
        act = _convgate_fwd(u, conv_w[layer], conv_b[layer], f"convgate{layer}")
        xout = _matmul(act, w_down[layer], mode="nn", out_dtype=F32, name=f"ffn_down{layer}", tm=512, tn=D, res=xin)
        return h, u, act, xout

    h2, u0, act0, x2 = ffn_fwd(x1, 0)
    hk, h3 = _rmsnorm_fwd(x2, [gain(kv_norm_g), gain(mix_norm_g[1])], "norm_kv_mix1")
    kv = _matmul(hk, w_kvf, mode="nn", out_dtype=BF16, name="proj_kv", tm=512, tn=B_KV)
    qb = _matmul(h3, w_q, mode="nn", out_dtype=BF16, name="proj_qb", tm=512, tn=B_Q)
    qp, kp, vp = _group_permute(qb), _group_permute(kv[:, :B_Q]), _group_permute(kv[:, B_Q:])
    og_p, lseg_p = _dil_fwd(qp, kp, vp, slopes, "dil_fwd")
    o_b, lse_b = _dil_merge(_group_unpermute(og_p), _group_unpermute(lseg_p), "dil_merge")
    x3 = _matmul(o_b, w_bo, mode="nn", out_dtype=F32, name="b_out", tm=512, tn=D, res=x2)
    h4, u1, act1, x4 = ffn_fwd(x3, 1)
    loss_blk, dx4, dx4b, dg_final = _final_loss(x4, target, gain(final_norm_g), "final_loss")

    def ffn_bwd(dx, dxb, xin, h, u, act, layer):
        dact = _matmul(dxb, w_down[layer], mode="nt", out_dtype=BF16, name=f"d_act{layer}", tm=512, tn=D_FF)
        dw_down = _matmul_tn(act, dxb, out_dtype=BF16, name=f"dw_down{layer}")
        du_a, du_g, dwa, dwg, dba, dbg = _convgate_bwd(u, conv_w[layer], conv_b[layer], dact, f"convgate_bwd{layer}")
        dw_up = jnp.concatenate(
            [_matmul_tn(h, du_a, out_dtype=BF16, name=f"dw_up_a{layer}"),
             _matmul_tn(h, du_g, out_dtype=BF16, name=f"dw_up_g{layer}")], axis=1)
        dh = _matmul(du_a, w_up[layer][:, :D_FF], mode="nt", out_dtype=F32, name=f"dh_ffn_a{layer}", tm=512, tn=D)
        dh = _matmul(du_g, w_up[layer][:, D_FF:], mode="nt", out_dtype=F32, name=f"dh_ffn_g{layer}", tm=512, tn=D,
                     res=dh)
        dxin, dxinb, dgain = _rmsnorm_bwd(xin, dh, gain(ffn_norm_g[layer]), dx, f"norm_ffn_bwd{layer}")
        dconv_w = jnp.concatenate([dwa, dwg], axis=1)
        dconv_b = jnp.concatenate([dba, dbg], axis=1)
        return dxin, dxinb, dgain, dw_up, dw_down, dconv_w, dconv_b

    dx3, dx3b, dg_ffn1, dw_up1, dw_down1, dconv_w1, dconv_b1 = ffn_bwd(dx4, dx4b, x3, h4, u1, act1, 1)

    do_b = _matmul(dx3b, w_bo, mode="nt", out_dtype=BF16, name="d_ob", tm=1024, tn=B_OUT)
    dw_bo = _matmul_tn(o_b, dx3b, out_dtype=BF16, name="dw_bo")
    dl_b = _head_rowsum(do_b, o_b, "delta_b")
    slots_up1, slots_down1, slots_conv1 = _ffn_slots(dw_up1, dw_down1, dconv_w1)
    dqp, dkp, dvp, land_down1, land_conv1 = _dil_bwd(
        qp, kp, vp, _same_permute(do_b), _same_permute(lse_b), _same_permute(dl_b), slopes, "dil_bwd",
        [slots_down1, slots_conv1])

    def natural(tp):
        return jnp.concatenate([_unpermute(tp[g], B_DILS[g]) for g in range(3)], axis=1)

    dqb = natural(dqp)
    dkv = jnp.concatenate([natural(dkp), natural(dvp)], axis=1)
    dw_q = _matmul_tn(h3, dqb, out_dtype=BF16, name="dw_q")
    dw_kv = _matmul_tn(hk, dkv, out_dtype=BF16, name="dw_kv")
    dh3 = _matmul(dqb, w_q, mode="nt", out_dtype=F32, name="dh_mix1", tm=512, tn=D)
    dhk = _matmul(dkv, w_kvf, mode="nt", out_dtype=F32, name="dh_kv", tm=512, tn=D)
    dx2, _, dg_mix1 = _rmsnorm_bwd(x2, dh3, gain(mix_norm_g[1]), dx3, "norm_mix1_bwd")
    dx2, dx2b, dg_kv = _rmsnorm_bwd(x2, dhk, gain(kv_norm_g), dx2, "norm_kv_bwd")

    dx1, dx1b, dg_ffn0, dw_up0, dw_down0, dconv_w0, dconv_b0 = ffn_bwd(dx2, dx2b, x1, h2, u0, act0, 0)

    do_a = _matmul(dx1b, w_out, mode="nt", out_dtype=BF16, name="d_oa", tm=512, tn=D)
    dw_out = _matmul_tn(o_a, dx1b, out_dtype=BF16, name="dw_out")
    dl_a = _head_rowsum(do_a, o_a, "delta_a")
    dq_a, dk_a, dv_a, dcol, drow, *land = _fox_bwd(
        qkv, do_a, lse_a, dl_a, c_t2, "fox_bwd",
        [dw_out.reshape(N_DEV, D // N_DEV, D), _cols_to_slots(dw_q), _cols_to_slots(dw_bo), _cols_to_slots(dw_kv)]
        + _ffn_slots(dw_up0, dw_down0, dconv_w0) + [slots_up1])
    land_out, land_q, land_bo, land_kv, land_up0, land_down0, land_conv0, land_up1 = land

    def head_sums(t):
        return t.reshape(S, N_PAIRS, 2, HEAD_DIM)[:, :, ::-1, 0].reshape(S, A_HEADS).T

    dz_t, db_f = _fox_prep_bwd(head_sums(drow), head_sums(dcol), z_t, b_f, "fox_prep_bwd")
    dz = jnp.pad(dz_t.T, ((0, 0), (0, LANES - A_HEADS))).astype(BF16)
    dproj = jnp.concatenate([dq_a.astype(BF16), dk_a, dv_a, dz], axis=1)
    dw_in = _matmul_tn(h1, dproj, out_dtype=BF16, name="dw_in")
    dh1, land_in = _matmul(dproj, w_in_pad, mode="nt", out_dtype=F32, name="dh_mix0", tm=512, tn=D,
                           scatter=[_cols_to_slots(dw_in[:, :A_QKV + A_HEADS])])
    grad_x, _, dg_mix0 = _rmsnorm_bwd(x0, dh1, gain(mix_norm_g[0]), dx1, "norm_mix0_bwd")

    dg_mix = jnp.concatenate([dg_mix0, dg_mix1], axis=0)
    dg_ffn = jnp.concatenate([dg_ffn0, dg_ffn1], axis=0)
    dconv_b = jnp.concatenate([dconv_b0, dconv_b1], axis=0)
    small_part = _pack_small(db_f, dg_kv, dg_mix, dg_ffn, dconv_b, dg_final)
    _, (small_parts,) = _final_exchange([], [small_part], "gather_small_grads")
    landed = [land_in, land_out, land_q, land_bo, land_kv, land_up0, land_up1, land_down0, land_down1,
              land_conv0, land_conv1]
    return loss_blk, grad_x, landed, small_parts


def kernel(x, a_w_in, a_b_f, a_w_out, b_w_q, b_w_out, kv_norm_g, w_kv, mix_norm_g, ffn_norm_g, ffn_w_up, ffn_conv_w, ffn_conv_b, ffn_w_down, final_norm_g, loss_target, m_a_w_in, m_a_b_f, m_a_w_out, m_b_w_q, m_b_w_out, m_kv_norm_g, m_w_kv, m_mix_norm_g, m_ffn_norm_g, m_ffn_w_up, m_ffn_conv_w, m_ffn_conv_b, m_ffn_w_down, m_final_norm_g, v_a_w_in, v_a_b_f, v_a_w_out, v_b_w_q, v_b_w_out, v_kv_norm_g, v_w_kv, v_mix_norm_g, v_ffn_norm_g, v_ffn_w_up, v_ffn_conv_w, v_ffn_conv_b, v_ffn_w_down, v_final_norm_g):
    def shards(a_w_in, a_w_out, b_w_q, b_w_out, w_kv, ffn_w_up, ffn_w_down, ffn_conv_w):
        return [a_w_in[0], a_w_out[0], b_w_q[0], b_w_out[0], w_kv, ffn_w_up[0], ffn_w_up[1],
                ffn_w_down[0], ffn_w_down[1], ffn_conv_w[0], ffn_conv_w[1]]

    w_loc = shards(a_w_in, a_w_out, b_w_q, b_w_out, w_kv, ffn_w_up, ffn_w_down, ffn_conv_w)
    m_loc = shards(m_a_w_in, m_a_w_out, m_b_w_q, m_b_w_out, m_w_kv, m_ffn_w_up, m_ffn_w_down, m_ffn_conv_w)
    v_loc = shards(v_a_w_in, v_a_w_out, v_b_w_q, v_b_w_out, v_w_kv, v_ffn_w_up, v_ffn_w_down, v_ffn_conv_w)

    (g_in,) = _all_gather([a_w_in[0].astype(BF16)], "gather_a_w_in")
    w_in = _cols_from_slots(g_in)
    w_in_pad = jnp.pad(w_in, ((0, 0), (0, A_PROJ_PAD - w_in.shape[1])))
    late_shards = [a_w_out[0].astype(BF16), b_w_q[0].astype(BF16), b_w_out[0].astype(BF16), w_kv.astype(BF16),
                   ffn_w_up.reshape(2 * D, -1).astype(BF16), ffn_w_down.reshape(-1, D).astype(BF16),
                   ffn_conv_w.reshape(6, -1)]

    loss_blk, grad_x, landed, small_parts = _local_step(
        x[0], loss_target[0], w_in_pad, late_shards,
        a_b_f, kv_norm_g, mix_norm_g, ffn_norm_g, ffn_conv_b, final_norm_g)

    big = [_adamw_sharded(w_loc[k], m_loc[k], v_loc[k], landed[k], f"adamw{k}") for k in range(11)]

    small = _adamw_replicated(
        _pack_small(a_b_f, kv_norm_g, mix_norm_g, ffn_norm_g, ffn_conv_b, final_norm_g),
        _pack_small(m_a_b_f, m_kv_norm_g, m_mix_norm_g, m_ffn_norm_g, m_ffn_conv_b, m_final_norm_g),
        _pack_small(v_a_b_f, v_kv_norm_g, v_mix_norm_g, v_ffn_norm_g, v_ffn_conv_b, v_final_norm_g),
        small_parts, "adamw_small")

    loss = lax.psum(loss_blk[0, 0], ("x", "y", "c"))

    def assemble(kind):
        b = [r[kind] for r in big]
        s_abf, s_kv, s_mix, s_ffn, s_cb, s_fin = _unpack_small(small[kind])
        return [b[0][None], s_abf, b[1][None], b[2][None], b[3][None], s_kv, b[4], s_mix, s_ffn,
                jnp.stack([b[5], b[6]]), jnp.stack([b[9], b[10]]), s_cb, jnp.stack([b[7], b[8]]), s_fin]

    return (loss, grad_x[None], *assemble(0), *assemble(1), *assemble(2), *assemble(3))
```

```python
import functools
import math

import jax
import jax.numpy as jnp
from jax import lax
from jax.experimental import pallas as pl
from jax.experimental.pallas import tpu as pltpu

F32 = jnp.float32
BF16 = jnp.bfloat16

S = 4096
D = 1024
N_DEV = 8
A_HEADS = 16
HEAD_DIM = 64
A_QKV = 3072
A_PROJ_PAD = 3200
B_Q = 1536
B_OUT = 512
B_KV = 3072
B_W = 128
B_DILS = (1, 4, 16)
D_FF = 2816
RMS_EPS = 1e-6
SCALE = HEAD_DIM ** -0.5
NEG = -1e30

ADAM_LR = 0.001
ADAM_B1 = 0.9
ADAM_B2 = 0.999
ADAM_EPS = 1e-08
ADAM_WD = 0.01
ADAM_STEP = 10

LANES = 128
VMEM_LIMIT = 56 * 1024 * 1024
MESH = pl.DeviceIdType.MESH
ANY = pl.BlockSpec(memory_space=pl.ANY)

NT_DIMS = (((1,), (1,)), ((), ()))
TN_DIMS = (((0,), (0,)), ((), ()))
NN_DIMS = (((1,), (0,)), ((), ()))


def _params(*sem):
    return pltpu.CompilerParams(dimension_semantics=sem if sem else None, vmem_limit_bytes=VMEM_LIMIT)


def _dot(a, b, dims=NN_DIMS):
    return lax.dot_general(a, b, dims, preferred_element_type=F32)


def _split_dot(x, mat, pieces):
    out = None
    rem = x
    for _ in range(pieces):
        part = rem.astype(BF16)
        rem = rem - part.astype(F32)
        d = _dot(part, mat)
        out = d if out is None else out + d
    return out


def _pick(n, prefs):
    for p in prefs:
        if n % p == 0:
            return p
    return n


def _all_gather(arrays, name):
    n = len(arrays)

    def body(*refs):
        ins = refs[:n]
        outs = refs[n:2 * n]
        send_sems, recv_sems, local_sems = refs[2 * n:]
        x, y, c = lax.axis_index("x"), lax.axis_index("y"), lax.axis_index("c")
        me, sibling = (x, y, c), (x, y, 1 - c)
        chips = [(1 - x, y), (x, 1 - y), (1 - x, 1 - y)]

        def slot(a, px, py, pc):
            return outs[a].at[4 * px + 2 * py + pc]

        def copy(a, k, block, to, src=None):
            return pltpu.make_async_remote_copy(
                src_ref=slot(a, *block) if src is None else src, dst_ref=slot(a, *block),
                send_sem=send_sems.at[a, k], recv_sem=recv_sems.at[a, k],
                device_id=to, device_id_type=MESH)

        mine = [pltpu.make_async_copy(ins[a], slot(a, *me), local_sems.at[a]) for a in range(n)]
        for cp in mine:
            cp.start()
        first = []
        for a in range(n):
            first.append(copy(a, 0, me, sibling, src=ins[a]))
            first += [copy(a, 1 + j, me, (*chip, c), src=ins[a]) for j, chip in enumerate(chips)]
        for cp in first:
            cp.start()
        passed = []
        for j, chip in enumerate(chips):
            for a in range(n):
                copy(a, 1 + j, (*chip, c), me).wait_recv()
                fwd = copy(a, 4 + j, (*chip, c), sibling)
                fwd.start()
                passed.append(fwd)
        for a in range(n):
            copy(a, 0, sibling, me).wait_recv()
            for j, chip in enumerate(chips):
                copy(a, 4 + j, (*chip, 1 - c), me).wait_recv()
        for cp in first + passed:
            cp.wait_send()
        for cp in mine:
            cp.wait()

    return pl.pallas_call(
        body, name=name,
        out_shape=[jax.ShapeDtypeStruct((N_DEV,) + a.shape, a.dtype) for a in arrays],
        in_specs=[ANY] * n, out_specs=[ANY] * n,
        scratch_shapes=[pltpu.SemaphoreType.DMA((n, 7)), pltpu.SemaphoreType.DMA((n, 7)),
                        pltpu.SemaphoreType.DMA((n,))],
    )(*arrays)


PEER_FLIPS = [(dx, dy, dc) for dx in (0, 1) for dy in (0, 1) for dc in (0, 1) if (dx, dy, dc) != (0, 0, 0)]


def _exchange_copies(ins, outs, sems, scatter):
    if not ins:
        return []
    send_sems, recv_sems, local_sems = sems
    x, y, c = lax.axis_index("x"), lax.axis_index("y"), lax.axis_index("c")
    me = 4 * x + 2 * y + c
    copies = []
    for a in range(len(ins)):
        copies.append(pltpu.make_async_copy(ins[a].at[me] if scatter else ins[a], outs[a].at[me], local_sems.at[a]))
        for k, (dx, dy, dc) in enumerate(PEER_FLIPS):
            px, py, pc = (1 - x if dx else x), (1 - y if dy else y), (1 - c if dc else c)
            copies.append(pltpu.make_async_remote_copy(
                src_ref=ins[a].at[4 * px + 2 * py + pc] if scatter else ins[a], dst_ref=outs[a].at[me],
                send_sem=send_sems.at[a, k], recv_sem=recv_sems.at[a, k],
                device_id=(px, py, pc), device_id_type=MESH))
    return copies


def _exchange_scratch(n):
    if n == 0:
        return []
    return [pltpu.SemaphoreType.DMA((n, 7)), pltpu.SemaphoreType.DMA((n, 7)), pltpu.SemaphoreType.DMA((n,))]


def _exchange_shapes(arrays, scatter):
    return [jax.ShapeDtypeStruct((N_DEV,) + (a.shape[1:] if scatter else a.shape), a.dtype) for a in arrays]


def _final_exchange(scatter, gather, name):
    ns, ng = len(scatter), len(gather)

    def body(*refs):
        ins, outs, sems = refs[:ns + ng], refs[ns + ng:2 * (ns + ng)], refs[2 * (ns + ng):]
        n_sems = len(_exchange_scratch(ns))
        copies = (_exchange_copies(ins[:ns], outs[:ns], sems[:n_sems], True)
                  + _exchange_copies(ins[ns:], outs[ns:], sems[n_sems:], False))
        for cp in copies:
            cp.start()
        for cp in copies:
            cp.wait()

    res = pl.pallas_call(
        body, name=name, out_shape=_exchange_shapes(scatter, True) + _exchange_shapes(gather, False),
        in_specs=[ANY] * (ns + ng), out_specs=[ANY] * (ns + ng),
        scratch_shapes=_exchange_scratch(ns) + _exchange_scratch(ng),
    )(*scatter, *gather)
    return res[:ns], res[ns:]


MM_ROWS = 512
MM_COLS = 1024


def _matmul(a, b, *, mode, out_dtype, name, tm, tn, res=None, scatter=()):
    if mode == "nn":
        (M, K), (K2, N) = a.shape, b.shape
    else:
        (M, K), (N, K2) = a.shape, b.shape
    assert K == K2, (a.shape, b.shape, mode)
    tm, tn = min(tm, M), min(tn, N)
    sm = min(tm, MM_ROWS)
    sn = tn if tn <= MM_COLS else _pick(tn, (512, 256, 128))
    assert M % tm == 0 and N % tn == 0 and tm % sm == 0, (M, N, K, tm, tn)
    dims = NN_DIMS if mode == "nn" else NT_DIMS
    a_spec = pl.BlockSpec((tm, K), lambda i, j: (i, 0))
    if mode == "nt":
        b_spec = pl.BlockSpec((tn, K), lambda i, j: (j, 0))
    else:
        b_spec = pl.BlockSpec((K, tn), lambda i, j: (0, j))
    o_spec = pl.BlockSpec((tm, tn), lambda i, j: (i, j))
    has_res = res is not None
    n_in, n_ex = 2 + has_res, len(scatter)
    gm, gn = M // tm, N // tn

    def body(*refs):
        a_ref, b_ref = refs[0], refs[1]
        r_ref = refs[2] if has_res else None
        o_ref = refs[n_in + n_ex]
        exchange = (refs[n_in:n_in + n_ex], refs[n_in + n_ex + 1:n_in + 2 * n_ex + 1], refs[n_in + 2 * n_ex + 1:], True)

        @pl.when(jnp.logical_and(pl.program_id(0) == 0, pl.program_id(1) == 0))
        def _():
            for cp in _exchange_copies(*exchange):
                cp.start()

        def chunk(r, carry):
            rows = pl.ds(pl.multiple_of(r * sm, sm), sm)
            av = a_ref[rows, :]
            for c0 in range(0, tn, sn):
                bv = b_ref[c0:c0 + sn, :] if mode == "nt" else b_ref[:, c0:c0 + sn]
                total = _dot(av, bv, dims)
                if has_res:
                    total = total + r_ref[rows, c0:c0 + sn]
                o_ref[rows, c0:c0 + sn] = total.astype(out_dtype)
            return carry

        lax.fori_loop(0, tm // sm, chunk, 0)

        @pl.when(jnp.logical_and(pl.program_id(0) == gm - 1, pl.program_id(1) == gn - 1))
        def _():
            for cp in _exchange_copies(*exchange):
                cp.wait()

    out = pl.pallas_call(
        body, name=name, grid=(gm, gn),
        out_shape=[jax.ShapeDtypeStruct((M, N), out_dtype)] + _exchange_shapes(scatter, True),
        in_specs=[a_spec, b_spec] + ([o_spec] if has_res else []) + [ANY] * n_ex,
        out_specs=[o_spec] + [ANY] * n_ex,
        scratch_shapes=_exchange_scratch(n_ex),
        compiler_params=_params("arbitrary", "arbitrary"),
    )(*((a, b, res) if has_res else (a, b)), *scatter)
    return out if n_ex else out[0]


def _matmul_tn(a, b, *, out_dtype, name, tk=512, sm=256):
    (K, M), (K2, N) = a.shape, b.shape
    assert K == K2 and K % tk == 0 and M % sm == 0, (a.shape, b.shape)
    nk = K // tk

    def body(a_ref, b_ref, o_ref, acc_ref):
        k = pl.program_id(0)

        @pl.when(k == 0)
        def _():
            acc_ref[...] = jnp.zeros_like(acc_ref)

        def chunk(mi, carry):
            cols = pl.ds(pl.multiple_of(mi * sm, sm), sm)
            acc_ref[cols, :] += _dot(a_ref[:, cols].T, b_ref[...])
            return carry

        lax.fori_loop(0, M // sm, chunk, 0)

        @pl.when(k == nk - 1)
        def _():
            def emit(mi, carry):
                rows = pl.ds(pl.multiple_of(mi * sm, sm), sm)
                o_ref[rows, :] = acc_ref[rows, :].astype(out_dtype)
                return carry
            lax.fori_loop(0, M // sm, emit, 0)

    return pl.pallas_call(
        body, name=name, grid=(nk,),
        out_shape=jax.ShapeDtypeStruct((M, N), out_dtype),
        in_specs=[pl.BlockSpec((tk, M), lambda k: (k, 0)), pl.BlockSpec((tk, N), lambda k: (k, 0))],
        out_specs=pl.BlockSpec((M, N), lambda k: (0, 0)),
        scratch_shapes=[pltpu.VMEM((M, N), F32)],
        compiler_params=_params("arbitrary"),
    )(a, b)


def _rmsnorm_fwd(x, gains, name, tr=256):
    n = len(gains)

    def body(*refs):
        x_ref = refs[0]
        xv = x_ref[...]
        r = lax.rsqrt(jnp.mean(xv * xv, axis=-1, keepdims=True) + RMS_EPS)
        y = xv * r
        for a in range(n):
            refs[1 + n + a][...] = (y * refs[1 + a][...]).astype(BF16)

    row = pl.BlockSpec((tr, D), lambda i: (i, 0))
    gain = pl.BlockSpec((1, D), lambda i: (0, 0))
    return pl.pallas_call(
        body, name=name, grid=(S // tr,),
        out_shape=[jax.ShapeDtypeStruct((S, D), BF16)] * n,
        in_specs=[row] + [gain] * n, out_specs=[row] * n,
        compiler_params=_params("parallel"),
    )(x, *gains)


def _rmsnorm_bwd(x, dy, g, dres, name, tr=256):
    def body(x_ref, dy_ref, g_ref, dres_ref, dx_ref, dxb_ref, dg_ref):
        xv = x_ref[...]
        dyv = dy_ref[...]
        r = lax.rsqrt(jnp.mean(xv * xv, axis=-1, keepdims=True) + RMS_EPS)
        xhat = xv * r
        dxhat = dyv * g_ref[...]
        mean_term = jnp.mean(dxhat * xhat, axis=-1, keepdims=True)
        dx = r * (dxhat - xhat * mean_term) + dres_ref[...]
        dx_ref[...] = dx
        dxb_ref[...] = dx.astype(BF16)
        part = jnp.sum(dyv * xhat, axis=0, keepdims=True)

        @pl.when(pl.program_id(0) == 0)
        def _():
            dg_ref[...] = part

        @pl.when(pl.program_id(0) > 0)
        def _():
            dg_ref[...] += part

    row = pl.BlockSpec((tr, D), lambda i: (i, 0))
    gain = pl.BlockSpec((1, D), lambda i: (0, 0))
    return pl.pallas_call(
        body, name=name, grid=(S // tr,),
        out_shape=[jax.ShapeDtypeStruct((S, D), F32), jax.ShapeDtypeStruct((S, D), BF16),
                   jax.ShapeDtypeStruct((1, D), F32)],
        in_specs=[row, row, gain, row], out_specs=[row, row, gain],
        compiler_params=_params("arbitrary"),
    )(x, dy, g, dres)


def _final_loss(x, target, g, name, tr=256):
    def body(x_ref, t_ref, g_ref, loss_ref, dx_ref, dxb_ref, dg_ref):
        xv = x_ref[...]
        gv = g_ref[...]
        r = lax.rsqrt(jnp.mean(xv * xv, axis=-1, keepdims=True) + RMS_EPS)
        xhat = xv * r
        err = xhat * gv - t_ref[...]
        row_loss = jnp.mean(err * err, axis=-1, keepdims=True)
        lpart = 0.5 * jnp.sum(row_loss, axis=0, keepdims=True)
        dyv = err / D
        dxhat = dyv * gv
        mean_term = jnp.mean(dxhat * xhat, axis=-1, keepdims=True)
        dx = r * (dxhat - xhat * mean_term)
        dx_ref[...] = dx
        dxb_ref[...] = dx.astype(BF16)
        gpart = jnp.sum(dyv * xhat, axis=0, keepdims=True)

        @pl.when(pl.program_id(0) == 0)
        def _():
            dg_ref[...] = gpart
            loss_ref[...] = jnp.broadcast_to(lpart, loss_ref.shape)

        @pl.when(pl.program_id(0) > 0)
        def _():
            dg_ref[...] += gpart
            loss_ref[...] += jnp.broadcast_to(lpart, loss_ref.shape)

    row = pl.BlockSpec((tr, D), lambda i: (i, 0))
    gain = pl.BlockSpec((1, D), lambda i: (0, 0))
    lspec = pl.BlockSpec((8, LANES), lambda i: (0, 0))
    return pl.pallas_call(
        body, name=name, grid=(S // tr,),
        out_shape=[jax.ShapeDtypeStruct((8, LANES), F32), jax.ShapeDtypeStruct((S, D), F32),
                   jax.ShapeDtypeStruct((S, D), BF16), jax.ShapeDtypeStruct((1, D), F32)],
        in_specs=[row, row, gain], out_specs=[lspec, row, row, gain],
        compiler_params=_params("arbitrary"),
    )(x, target, g)


CONV_TR = 128
CONV_TC = D_FF
CONV_NJ = D_FF // CONV_TC
HALO = 16


def _causal_taps(cur_ref, prev_ref, first):
    xv = cur_ref[...].astype(F32)
    pv = prev_ref[...].astype(F32)
    p1 = jnp.where(first, 0.0, pv[HALO - 1:HALO, :])
    p2 = jnp.where(first, 0.0, pv[HALO - 2:HALO - 1, :])
    r1, r2 = pltpu.roll(xv, 1, 0), pltpu.roll(xv, 2, 0)
    row = lax.broadcasted_iota(jnp.int32, (8, xv.shape[1]), 0)
    xm1 = jnp.concatenate([jnp.where(row == 0, p1, r1[0:8]), r1[8:]], axis=0)
    xm2 = jnp.concatenate([jnp.where(row == 0, p2, jnp.where(row == 1, p1, r2[0:8])), r2[8:]], axis=0)
    return xv, xm1, xm2


def _conv_specs():
    def prev_row(i):
        return jnp.maximum(i * (CONV_TR // HALO) - 1, 0)
    ua = pl.BlockSpec((CONV_TR, CONV_TC), lambda i, j: (i, j))
    ug = pl.BlockSpec((CONV_TR, CONV_TC), lambda i, j: (i, j + CONV_NJ))
    pa = pl.BlockSpec((HALO, CONV_TC), lambda i, j: (prev_row(i), j))
    pg = pl.BlockSpec((HALO, CONV_TC), lambda i, j: (prev_row(i), j + CONV_NJ))
    wa = pl.BlockSpec((3, CONV_TC), lambda i, j: (0, j))
    wg = pl.BlockSpec((3, CONV_TC), lambda i, j: (0, j + CONV_NJ))
    ba = pl.BlockSpec((1, CONV_TC), lambda i, j: (0, j))
    bg = pl.BlockSpec((1, CONV_TC), lambda i, j: (0, j + CONV_NJ))
    return [ua, pa, ug, pg, wa, wg, ba, bg]


def _convgate_fwd(u, w, b, name):
    def body(ua, pa, ug, pg, wa, wg, ba, bg, o_ref):
        first = pl.program_id(0) == 0
        x0, x1, x2 = _causal_taps(ua, pa, first)
        ac = wa[0:1, :] * x2 + wa[1:2, :] * x1 + wa[2:3, :] * x0 + ba[...]
        x0, x1, x2 = _causal_taps(ug, pg, first)
        gc = wg[0:1, :] * x2 + wg[1:2, :] * x1 + wg[2:3, :] * x0 + bg[...]
        sg = 1.0 / (1.0 + jnp.exp(-gc))
        o_ref[...] = (gc * sg * ac).astype(BF16)

    return pl.pallas_call(
        body, name=name, grid=(S // CONV_TR, CONV_NJ),
        out_shape=jax.ShapeDtypeStruct((S, D_FF), BF16),
        in_specs=_conv_specs(),
        out_specs=pl.BlockSpec((CONV_TR, CONV_TC), lambda i, j: (i, j)),
        compiler_params=_params("parallel", "parallel"),
    )(u, u, u, u, w, w, b, b)


def _anticausal_conv(d, nxt_ref, w_ref, last):
    n1 = jnp.where(last, 0.0, nxt_ref[0:1, :])
    n2 = jnp.where(last, 0.0, nxt_ref[1:2, :])
    r1, r2 = pltpu.roll(d, CONV_TR - 1, 0), pltpu.roll(d, CONV_TR - 2, 0)
    row = lax.broadcasted_iota(jnp.int32, (8, d.shape[1]), 0)
    cut = CONV_TR - 8
    dp1 = jnp.concatenate([r1[:cut], jnp.where(row == 7, n1, r1[cut:])], axis=0)
    dp2 = jnp.concatenate([r2[:cut], jnp.where(row == 7, n2, jnp.where(row == 6, n1, r2[cut:]))], axis=0)
    return w_ref[2:3, :] * d + w_ref[1:2, :] * dp1 + w_ref[0:1, :] * dp2


def _convgate_bwd(u, w, b, dact, name):
    n_i = S // CONV_TR

    def body(ua, pa, ug, pg, wa, wg, ba, bg, d_ref, dua_ref, dug_ref, dwa_ref, dwg_ref, dba_ref, dbg_ref,
             nxt_a, nxt_g):
        i = pl.program_id(1)
        last = i == 0
        first = i == n_i - 1
        a0, a1, a2 = _causal_taps(ua, pa, first)
        ac = wa[0:1, :] * a2 + wa[1:2, :] * a1 + wa[2:3, :] * a0 + ba[...]
        g0, g1, g2 = _causal_taps(ug, pg, first)
        gc = wg[0:1, :] * g2 + wg[1:2, :] * g1 + wg[2:3, :] * g0 + bg[...]
        sg = 1.0 / (1.0 + jnp.exp(-gc))
        dact_v = d_ref[...].astype(F32)
        da = dact_v * (gc * sg)
        dg = dact_v * ac * (sg * (1.0 + gc * (1.0 - sg)))
        dua_ref[...] = _anticausal_conv(da, nxt_a, wa, last).astype(BF16)
        dug_ref[...] = _anticausal_conv(dg, nxt_g, wg, last).astype(BF16)
        nxt_a[...] = da[0:8]
        nxt_g[...] = dg[0:8]

        def col(v):
            return jnp.sum(v, axis=0, keepdims=True)

        parts = [col(da * a2), col(da * a1), col(da * a0), col(dg * g2), col(dg * g1), col(dg * g0),
                 col(da), col(dg)]

        @pl.when(last)
        def _():
            for k in range(3):
                dwa_ref[k:k + 1, :] = parts[k]
                dwg_ref[k:k + 1, :] = parts[3 + k]
            dba_ref[...] = parts[6]
            dbg_ref[...] = parts[7]

        @pl.when(i > 0)
        def _():
            for k in range(3):
                dwa_ref[k:k + 1, :] += parts[k]
                dwg_ref[k:k + 1, :] += parts[3 + k]
            dba_ref[...] += parts[6]
            dbg_ref[...] += parts[7]

    def swap(spec):
        return pl.BlockSpec(spec.block_shape, lambda j, i, f=spec.index_map: f(n_i - 1 - i, j))

    blk = pl.BlockSpec((CONV_TR, CONV_TC), lambda j, i: (n_i - 1 - i, j))
    w3 = pl.BlockSpec((3, CONV_TC), lambda j, i: (0, j))
    b1 = pl.BlockSpec((1, CONV_TC), lambda j, i: (0, j))
    return pl.pallas_call(
        body, name=name, grid=(CONV_NJ, n_i),
        out_shape=[jax.ShapeDtypeStruct((S, D_FF), BF16), jax.ShapeDtypeStruct((S, D_FF), BF16),
                   jax.ShapeDtypeStruct((3, D_FF), F32), jax.ShapeDtypeStruct((3, D_FF), F32),
                   jax.ShapeDtypeStruct((1, D_FF), F32), jax.ShapeDtypeStruct((1, D_FF), F32)],
        in_specs=[swap(s) for s in _conv_specs()] + [blk],
        out_specs=[blk, blk, w3, w3, b1, b1],
        scratch_shapes=[pltpu.VMEM((8, CONV_TC), F32), pltpu.VMEM((8, CONV_TC), F32)],
        compiler_params=_params("arbitrary", "arbitrary"),
    )(u, u, u, u, w, w, b, b, dact)


FOX_T = 256
FOX_TQ, FOX_TK = 256, 256
N_PAIRS = A_HEADS // 2


def _lane_masks():
    lane = lax.broadcasted_iota(jnp.int32, (1, LANES), 1)
    return lane, (lane < HEAD_DIM, lane >= HEAD_DIM)


def _fox_prep_fwd(z_t, b, name):
    def body(z_ref, b_ref, c_ref):
        r = lax.broadcasted_iota(jnp.int32, (LANES, LANES), 0)
        cc = lax.broadcasted_iota(jnp.int32, (LANES, LANES), 1)
        upper = (r <= cc).astype(BF16)
        carry = jnp.zeros((A_HEADS, 1), F32)
        for blk in range(S // LANES):
            sl = slice(blk * LANES, (blk + 1) * LANES)
            z = z_ref[:, sl] + b_ref[...]
            lf = jnp.minimum(z, 0.0) - jnp.log(1.0 + jnp.exp(-jnp.abs(z)))
            cs = _split_dot(lf, upper, 3) + carry
            c_ref[:, sl] = cs
            carry = cs[:, LANES - 1:LANES]

    return pl.pallas_call(
        body, name=name, out_shape=jax.ShapeDtypeStruct((A_HEADS, S), F32),
        compiler_params=_params(),
    )(z_t, b)


def _fox_prep_bwd(drow_t, dcol_t, z_t, b, name):
    def body(dr_ref, dc_ref, z_ref, b_ref, dz_ref, db_ref):
        r = lax.broadcasted_iota(jnp.int32, (LANES, LANES), 0)
        cc = lax.broadcasted_iota(jnp.int32, (LANES, LANES), 1)
        lower = (r >= cc).astype(BF16)
        carry = jnp.zeros((A_HEADS, 1), F32)
        db = jnp.zeros((A_HEADS, 1), F32)
        for blk in reversed(range(S // LANES)):
            sl = slice(blk * LANES, (blk + 1) * LANES)
            rc = _split_dot(dr_ref[:, sl] - dc_ref[:, sl], lower, 3) + carry
            carry = rc[:, 0:1]
            z = z_ref[:, sl] + b_ref[...]
            dz = rc / (1.0 + jnp.exp(z))
            dz_ref[:, sl] = dz
            db = db + jnp.sum(dz, axis=1, keepdims=True)
        db_ref[...] = db

    return pl.pallas_call(
        body, name=name,
        out_shape=[jax.ShapeDtypeStruct((A_HEADS, S), F32), jax.ShapeDtypeStruct((A_HEADS, 1), F32)],
        compiler_params=_params(),
    )(drow_t, dcol_t, z_t, b)


def _fox_fwd(qkv, c_t2, name, gather):
    tq, tk = FOX_TQ, FOX_TK

    n = len(gather)

    def body(*refs):
        q_ref, k_ref, v_ref, ct_ref = refs[:4]
        o_ref, lse_ref = refs[4 + n:6 + n]
        exchange = (refs[4:4 + n], refs[6 + n:6 + 2 * n], refs[6 + 2 * n:len(refs) - 5], False)
        s_scr, p_scr, acc_scr = refs[-5:-3], refs[-3:-1], refs[-1]
        qi = pl.program_id(1)

        @pl.when(jnp.logical_and(pl.program_id(0) == 0, qi == 0))
        def _():
            for cp in _exchange_copies(*exchange):
                cp.start()

        n_full = jnp.right_shift(qi, (tk // tq).bit_length() - 1)
        lane, masks = _lane_masks()
        q = q_ref[...] * SCALE
        qs = [jnp.where(masks[e], q, jnp.zeros_like(q)) for e in range(2)]

        def scores_into(j, slot):
            start = pl.multiple_of(j * tk, tk)
            kb = k_ref[pl.ds(start, tk), :]
            for e in range(2):
                s_scr[slot][e] = _dot(qs[e], kb, NT_DIMS) - ct_ref[e:e + 1, pl.ds(start, tk)]

        def softmax_of(slot, m, masked):
            m_new, alpha = [], []
            for e in range(2):
                s = s_scr[slot][e]
                if masked:
                    rows = lax.broadcasted_iota(jnp.int32, (tq, tk), 0) + (qi * tq - n_full * tk)
                    cols = lax.broadcasted_iota(jnp.int32, (tq, tk), 1)
                    s = jnp.where(cols <= rows, s, NEG)
                m_new.append(jnp.maximum(m[e], jnp.max(s, axis=1, keepdims=True)))
                p_scr[slot][e] = jnp.exp(s - m_new[e]).astype(BF16)
                alpha.append(jnp.exp(m[e] - m_new[e]))
            return tuple(m_new), tuple(alpha)

        def values_of(j, slot, alpha):
            start = pl.multiple_of(j * tk, tk)
            vb = v_ref[pl.ds(start, tk), :]
            for e in range(2):
                acc_scr[e] = (alpha[e] * acc_scr[e]
                              + _dot(p_scr[slot][e], jnp.where(masks[e], vb, jnp.ones_like(vb))))

        def stage(j, cur, nxt, carry):
            m, a_prev = carry
            scores_into(j + 1, nxt)
            values_of(jnp.maximum(j - 1, 0), nxt, a_prev)
            return softmax_of(cur, m, False)

        def finish(cur, nxt, carry):
            m, a_prev = carry
            values_of(jnp.maximum(n_full - 1, 0), nxt, a_prev)
            (m0, m1), alpha = softmax_of(cur, m, True)
            values_of(n_full, cur, alpha)
            l0 = acc_scr[0][:, HEAD_DIM:HEAD_DIM + 1]
            l1 = acc_scr[1][:, 0:1]
            o_ref[...] = jnp.where(masks[0], acc_scr[0] / l0, acc_scr[1] / l1).astype(BF16)
            lse_ref[...] = jnp.where(masks[0], m0 + jnp.log(l0), m1 + jnp.log(l1))

        scores_into(0, 0)
        for e in range(2):
            p_scr[1][e] = jnp.zeros((tq, tk), BF16)
            acc_scr[e] = jnp.zeros((tq, LANES), F32)
        two = lambda x: (x, x)
        init = (two(jnp.full((tq, 1), NEG, F32)), two(jnp.ones((tq, 1), F32)))

        def two_stages(jj, carry):
            return stage(2 * jj + 1, 1, 0, stage(2 * jj, 0, 1, carry))

        carry = lax.fori_loop(0, jnp.right_shift(n_full, 1), two_stages, init)
        odd = jnp.bitwise_and(n_full, 1) == 1

        @pl.when(odd)
        def _():
            finish(1, 0, stage(n_full - 1, 0, 1, carry))

        @pl.when(jnp.logical_not(odd))
        def _():
            finish(0, 1, carry)

        @pl.when(jnp.logical_and(pl.program_id(0) == N_PAIRS - 1, qi == S // tq - 1))
        def _():
            for cp in _exchange_copies(*exchange):
                cp.wait()

    qspec = pl.BlockSpec((tq, LANES), lambda h, i: (i, h))
    return pl.pallas_call(
        body, name=name, grid=(N_PAIRS, S // tq),
        out_shape=[jax.ShapeDtypeStruct((S, D), BF16), jax.ShapeDtypeStruct((S, D), F32)]
        + _exchange_shapes(gather, False),
        in_specs=[qspec,
                  pl.BlockSpec((S, LANES), lambda h, i: (0, N_PAIRS + h)),
                  pl.BlockSpec((S, LANES), lambda h, i: (0, 2 * N_PAIRS + h)),
                  pl.BlockSpec((None, 2, S), lambda h, i: (h, 0, 0))] + [ANY] * n,
        out_specs=[qspec, qspec] + [ANY] * n,
        scratch_shapes=_exchange_scratch(n) + [
            pltpu.VMEM((2, tq, tk), F32), pltpu.VMEM((2, tq, tk), F32),
            pltpu.VMEM((2, tq, tk), BF16), pltpu.VMEM((2, tq, tk), BF16),
            pltpu.VMEM((2, tq, LANES), F32)],
        compiler_params=_params("arbitrary", "arbitrary"),
    )(qkv, qkv, qkv, c_t2, *gather)


def _head_rowsum(a, b, name, tr=256):
    C = a.shape[1]

    def body(a_ref, b_ref, o_ref):
        r = lax.broadcasted_iota(jnp.int32, (LANES, LANES), 0) < HEAD_DIM
        cc = lax.broadcasted_iota(jnp.int32, (LANES, LANES), 1) < HEAD_DIM
        same_head = (r == cc).astype(BF16)
        for blk in range(C // LANES):
            sl = slice(blk * LANES, (blk + 1) * LANES)
            prod = a_ref[:, sl].astype(F32) * b_ref[:, sl].astype(F32)
            o_ref[:, sl] = _split_dot(prod, same_head, 2)

    row = pl.BlockSpec((tr, C), lambda i: (i, 0))
    return pl.pallas_call(
        body, name=name, grid=(S // tr,), out_shape=jax.ShapeDtypeStruct((S, C), F32),
        in_specs=[row, row], out_specs=row, compiler_params=_params("parallel"),
    )(a, b)


def _fox_bwd(qkv, do, lse, delta, c_t2, name, scatter):
    t = FOX_T
    nq = S // t

    n = len(scatter)

    def body(*refs):
        q_ref, k_ref, v_ref, do_ref, lse_ref, dl_ref, ct_ref = refs[:7]
        dq_ref, dk_ref, dv_ref, dcol_ref, drow_ref = refs[7 + n:12 + n]
        exchange = (refs[7:7 + n], refs[12 + n:12 + 2 * n], refs[12 + 2 * n:len(refs) - 5], True)
        sd_scr, pd_scr, acc_scr = refs[-5:-3], refs[-3:-1], refs[-1]
        kj = pl.program_id(1)

        @pl.when(jnp.logical_and(pl.program_id(0) == 0, kj == 0))
        def _():
            for cp in _exchange_copies(*exchange):
                cp.start()

        @pl.when(kj == 0)
        def _():
            dq_ref[...] = jnp.zeros_like(dq_ref)
            drow_ref[...] = jnp.zeros_like(drow_ref)

        lane, masks = _lane_masks()
        k = k_ref[...]
        v = v_ref[...]
        k_aug = [jnp.where(masks[e], k * SCALE, jnp.ones_like(k)) for e in range(2)]
        cs = [ct_ref[e:e + 1, :] for e in range(2)]

        def rows_of(i):
            r0 = pl.multiple_of(i * t, t)
            return pl.ds(r0, t), q_ref[pl.ds(r0, t), :] * SCALE, do_ref[pl.ds(r0, t), :]

        def scores_into(i, slot):
            _, qb, dob = rows_of(i)
            for e in range(2):
                qe = jnp.where(masks[e], qb, jnp.zeros_like(qb))
                doe = jnp.where(masks[e], dob, jnp.zeros_like(dob))
                sd_scr[slot][2 * e] = _dot(qe, k, NT_DIMS) - cs[e]
                sd_scr[slot][2 * e + 1] = _dot(doe, v, NT_DIMS)

        def pointwise(i, slot, masked):
            rows, _, _ = rows_of(i)
            for e in range(2):
                lo = e * HEAD_DIM
                s = sd_scr[slot][2 * e]
                if masked:
                    r = lax.broadcasted_iota(jnp.int32, (t, t), 0)
                    c = lax.broadcasted_iota(jnp.int32, (t, t), 1)
                    s = jnp.where(c <= r, s, NEG)
                p = jnp.exp(s - lse_ref[rows, lo:lo + 1])
                pd_scr[slot][2 * e] = p.astype(BF16)
                pd_scr[slot][2 * e + 1] = (p * (sd_scr[slot][2 * e + 1] - dl_ref[rows, lo:lo + 1])).astype(BF16)

        def accumulate(i, slot):
            rows, qb, dob = rows_of(i)
            dq_parts = []
            for e in range(2):
                p, ds = pd_scr[slot][2 * e], pd_scr[slot][2 * e + 1]
                q_aug = jnp.where(masks[e], qb, jnp.ones_like(qb))
                doe = jnp.where(masks[e], dob, jnp.zeros_like(dob))
                acc_scr[2] += _dot(p, doe, TN_DIMS)
                acc_scr[e] += _dot(ds, q_aug, TN_DIMS)
                dq_parts.append(_dot(ds, k_aug[e]))
            dq_ref[rows, :] += jnp.where(masks[0], dq_parts[0], dq_parts[1])
            drow_ref[rows, :] += jnp.where(masks[0], dq_parts[1], dq_parts[0])

        def stage(i, cur, nxt):
            scores_into(jnp.minimum(i + 1, nq - 1), nxt)
            accumulate(i - 1, nxt)
            pointwise(i, cur, False)

        acc_scr[...] = jnp.zeros_like(acc_scr)
        scores_into(kj, 0)
        pointwise(kj, 0, True)
        scores_into(jnp.minimum(kj + 1, nq - 1), 1)
        rest = nq - 1 - kj

        def two_stages(jj, carry):
            stage(kj + 1 + 2 * jj, 1, 0)
            stage(kj + 2 + 2 * jj, 0, 1)
            return carry

        lax.fori_loop(0, jnp.right_shift(rest, 1), two_stages, 0)
        odd = jnp.bitwise_and(rest, 1) == 1

        @pl.when(odd)
        def _():
            stage(nq - 1, 1, 0)
            accumulate(nq - 1, 1)

        @pl.when(jnp.logical_not(odd))
        def _():
            accumulate(nq - 1, 0)

        dk0, dk1, dv = acc_scr[0], acc_scr[1], acc_scr[2]
        dk_ref[...] = jnp.where(masks[0], dk0, dk1).astype(BF16)
        dcol_ref[...] = jnp.where(masks[0], dk1, dk0)
        dv_ref[...] = dv.astype(BF16)

        @pl.when(jnp.logical_and(pl.program_id(0) == N_PAIRS - 1, kj == nq - 1))
        def _():
            for cp in _exchange_copies(*exchange):
                cp.wait()

    full = lambda off: pl.BlockSpec((S, LANES), lambda h, j, off=off: (0, off + h))
    kv = lambda off: pl.BlockSpec((t, LANES), lambda h, j, off=off: (j, off + h))
    return pl.pallas_call(
        body, name=name, grid=(N_PAIRS, nq),
        out_shape=[jax.ShapeDtypeStruct((S, D), F32), jax.ShapeDtypeStruct((S, D), BF16),
                   jax.ShapeDtypeStruct((S, D), BF16), jax.ShapeDtypeStruct((S, D), F32),
                   jax.ShapeDtypeStruct((S, D), F32)] + _exchange_shapes(scatter, True),
        in_specs=[full(0), kv(N_PAIRS), kv(2 * N_PAIRS), full(0), full(0), full(0),
                  pl.BlockSpec((None, 2, t), lambda h, j: (h, 0, j))] + [ANY] * n,
        out_specs=[full(0), kv(0), kv(0), kv(0), full(0)] + [ANY] * n,
        scratch_shapes=_exchange_scratch(n) + [
            pltpu.VMEM((4, t, t), F32), pltpu.VMEM((4, t, t), F32),
            pltpu.VMEM((4, t, t), BF16), pltpu.VMEM((4, t, t), BF16),
            pltpu.VMEM((3, t, LANES), F32)],
        compiler_params=_params("arbitrary", "arbitrary"),
    )(qkv, qkv, qkv, do, lse, delta, c_t2, *scatter)


B_PAIRS = 4
B_NB = S // B_W


def _group_consts(g):
    nbs = jnp.where(g == 0, B_NB // B_DILS[0], jnp.where(g == 1, B_NB // B_DILS[1], B_NB // B_DILS[2]))
    dil = jnp.where(g == 0, B_DILS[0], jnp.where(g == 1, B_DILS[1], B_DILS[2]))
    return nbs, dil


def _band(dil):
    qi = lax.broadcasted_iota(jnp.int32, (B_W, B_W), 0)
    kj = lax.broadcasted_iota(jnp.int32, (B_W, B_W), 1)
    dist_c = qi - kj
    dist_p = qi + B_W - kj
    return (dist_c * dil).astype(F32), dist_c >= 0, (dist_p * dil).astype(F32), dist_p <= B_W


def _dil_fwd(qp, kp, vp, slopes, name):
    def body(sl_ref, q_ref, kp_ref, kc_ref, vp_ref, vc_ref, o_ref, lse_ref):
        g, n = pl.program_id(0), pl.program_id(1)
        nbs, dil = _group_consts(g)
        has_prev = (n % nbs) != 0
        lane, masks = _lane_masks()
        bias_c, ok_c, bias_p, ok_p = _band(dil)
        ok_p = jnp.logical_and(ok_p, has_prev)
        heads = [(hp, e) for hp in range(B_PAIRS) for e in range(2)]
        col = lambda ref, hp: ref[:, hp * LANES:(hp + 1) * LANES]
        logits = []
        for hp, e in heads:
            q = col(q_ref, hp) * SCALE
            qe = jnp.where(masks[e], q, jnp.zeros_like(q))
            logits.append((_dot(qe, col(kc_ref, hp), NT_DIMS), _dot(qe, col(kp_ref, hp), NT_DIMS)))
        probs = []
        for (hp, e), (sc, sp) in zip(heads, logits):
            slope = sl_ref[g * 8 + 2 * hp + e]
            sc = jnp.where(ok_c, sc - slope * bias_c, NEG)
            sp = jnp.where(ok_p, sp - slope * bias_p, NEG)
            m = jnp.maximum(jnp.max(sc, axis=1, keepdims=True), jnp.max(sp, axis=1, keepdims=True))
            probs.append((jnp.exp(sc - m).astype(BF16), jnp.exp(sp - m).astype(BF16), m))
        outs, lses = [], []
        for (hp, e), (pc, pp, m) in zip(heads, probs):
            vc, vpv = col(vc_ref, hp), col(vp_ref, hp)
            acc = (_dot(pc, jnp.where(masks[e], vc, jnp.ones_like(vc)))
                   + _dot(pp, jnp.where(masks[e], vpv, jnp.ones_like(vpv))))
            l = acc[:, HEAD_DIM:HEAD_DIM + 1] if e == 0 else acc[:, 0:1]
            outs.append(acc / l)
            lses.append(m + jnp.log(l))
        o_ref[...] = jnp.concatenate(
            [jnp.where(masks[0], outs[2 * hp], outs[2 * hp + 1]) for hp in range(B_PAIRS)], axis=1)
        lse_ref[...] = jnp.concatenate(
            [jnp.where(masks[0], lses[2 * hp], lses[2 * hp + 1]) for hp in range(B_PAIRS)], axis=1)

    cur = pl.BlockSpec((None, B_W, B_OUT), lambda g, n, sl: (g, n, 0))
    prev = pl.BlockSpec((None, B_W, B_OUT), lambda g, n, sl: (g, jnp.maximum(n - 1, 0), 0))
    return pl.pallas_call(
        body, name=name,
        grid_spec=pltpu.PrefetchScalarGridSpec(
            num_scalar_prefetch=1, grid=(3, B_NB),
            in_specs=[cur, prev, cur, prev, cur], out_specs=[cur, cur]),
        out_shape=[jax.ShapeDtypeStruct((3, S, B_OUT), F32), jax.ShapeDtypeStruct((3, S, B_OUT), F32)],
        compiler_params=_params("parallel", "parallel"),
    )(slopes, qp, kp, kp, vp, vp)


def _dil_merge(og, lseg, name, tr=256):
    def body(o_ref, l_ref, out_ref, lse_ref):
        l0, l1, l2 = l_ref[0], l_ref[1], l_ref[2]
        m = jnp.maximum(jnp.maximum(l0, l1), l2)
        w0, w1, w2 = jnp.exp(l0 - m), jnp.exp(l1 - m), jnp.exp(l2 - m)
        den = w0 + w1 + w2
        out_ref[...] = ((w0 * o_ref[0] + w1 * o_ref[1] + w2 * o_ref[2]) / den).astype(BF16)
        lse_ref[...] = m + jnp.log(den)

    blk3 = pl.BlockSpec((3, tr, B_OUT), lambda i: (0, i, 0))
    blk = pl.BlockSpec((tr, B_OUT), lambda i: (i, 0))
    return pl.pallas_call(
        body, name=name, grid=(S // tr,),
        out_shape=[jax.ShapeDtypeStruct((S, B_OUT), BF16), jax.ShapeDtypeStruct((S, B_OUT), F32)],
        in_specs=[blk3, blk3], out_specs=[blk, blk], compiler_params=_params("parallel"),
    )(og, lseg)


def _dil_bwd(qp, kp, vp, dop, lsep, dlp, slopes, name, scatter):
    n_ex = len(scatter)

    def body(sl_ref, *refs):
        (qc_ref, qn_ref, kp_ref, kc_ref, vp_ref, vc_ref, doc_ref, don_ref,
         lc_ref, ln_ref, dc_ref, dn_ref) = refs[:12]
        dq_ref, dk_ref, dv_ref = refs[12 + n_ex:15 + n_ex]
        exchange = (refs[12:12 + n_ex], refs[15 + n_ex:15 + 2 * n_ex], refs[15 + 2 * n_ex:], True)
        g, n = pl.program_id(0), pl.program_id(1)

        @pl.when(jnp.logical_and(g == 0, n == 0))
        def _():
            for cp in _exchange_copies(*exchange):
                cp.start()

        nbs, dil = _group_consts(g)
        has_prev = (n % nbs) != 0
        has_next = jnp.logical_and(n + 1 < B_NB, ((n + 1) % nbs) != 0)
        lane, masks = _lane_masks()
        bias_c, ok_c, bias_p, ok_p = _band(dil)
        ok_pp = jnp.logical_and(ok_p, has_prev)
        ok_np = jnp.logical_and(ok_p, has_next)
        heads = [(hp, e) for hp in range(B_PAIRS) for e in range(2)]
        col = lambda ref, hp: ref[:, hp * LANES:(hp + 1) * LANES]
        mask = lambda t, e: jnp.where(masks[e], t, jnp.zeros_like(t))
        raw = []
        for hp, e in heads:
            qce, qne = mask(col(qc_ref, hp) * SCALE, e), mask(col(qn_ref, hp) * SCALE, e)
            doce, done = mask(col(doc_ref, hp), e), mask(col(don_ref, hp), e)
            kc, kpv, vc, vpv = col(kc_ref, hp), col(kp_ref, hp), col(vc_ref, hp), col(vp_ref, hp)
            raw.append(((_dot(qce, kc, NT_DIMS), _dot(doce, vc, NT_DIMS)),
                        (_dot(qce, kpv, NT_DIMS), _dot(doce, vpv, NT_DIMS)),
                        (_dot(qne, kc, NT_DIMS), _dot(done, vc, NT_DIMS))))
        pds = []
        for (hp, e), tiles in zip(heads, raw):
            lo = hp * LANES + e * HEAD_DIM
            slope = sl_ref[g * 8 + 2 * hp + e]
            lse_c, dl_c = lc_ref[:, lo:lo + 1], dc_ref[:, lo:lo + 1]
            lse_n, dl_n = ln_ref[:, lo:lo + 1], dn_ref[:, lo:lo + 1]
            out = []
            for (s, dp), ok, bias, lse, dl in ((tiles[0], ok_c, bias_c, lse_c, dl_c),
                                               (tiles[1], ok_pp, bias_p, lse_c, dl_c),
                                               (tiles[2], ok_np, bias_p, lse_n, dl_n)):
                p = jnp.exp(jnp.where(ok, s - slope * bias, NEG) - lse)
                out.append((p.astype(BF16), (p * (dp - dl)).astype(BF16)))
            pds.append(out)
        dq_all, dk_all, dv_all = [], [], []
        for hp in range(B_PAIRS):
            dq = jnp.zeros((B_W, LANES), F32)
            dk = jnp.zeros((B_W, LANES), F32)
            dv = jnp.zeros((B_W, LANES), F32)
            for e in range(2):
                (p_c, ds_c), (_, ds_p), (p_n, ds_n) = pds[2 * hp + e]
                qce, qne = mask(col(qc_ref, hp) * SCALE, e), mask(col(qn_ref, hp) * SCALE, e)
                doce, done = mask(col(doc_ref, hp), e), mask(col(don_ref, hp), e)
                dq = dq + _dot(ds_c, mask(col(kc_ref, hp) * SCALE, e)) + _dot(ds_p, mask(col(kp_ref, hp) * SCALE, e))
                dk = dk + _dot(ds_c, qce, TN_DIMS) + _dot(ds_n, qne, TN_DIMS)
                dv = dv + _dot(p_c, doce, TN_DIMS) + _dot(p_n, done, TN_DIMS)
            dq_all.append(dq)
            dk_all.append(dk)
            dv_all.append(dv)
        dq_ref[...] = jnp.concatenate(dq_all, axis=1).astype(BF16)
        dk_ref[...] = jnp.concatenate(dk_all, axis=1).astype(BF16)
        dv_ref[...] = jnp.concatenate(dv_all, axis=1).astype(BF16)

        @pl.when(jnp.logical_and(g == 2, n == B_NB - 1))
        def _():
            for cp in _exchange_copies(*exchange):
                cp.wait()

    cur = pl.BlockSpec((None, B_W, B_OUT), lambda g, n, sl: (g, n, 0))
    prev = pl.BlockSpec((None, B_W, B_OUT), lambda g, n, sl: (g, jnp.maximum(n - 1, 0), 0))
    nxt = pl.BlockSpec((None, B_W, B_OUT), lambda g, n, sl: (g, jnp.minimum(n + 1, B_NB - 1), 0))
    return pl.pallas_call(
        body, name=name,
        grid_spec=pltpu.PrefetchScalarGridSpec(
            num_scalar_prefetch=1, grid=(3, B_NB),
            in_specs=[cur, nxt, prev, cur, prev, cur, cur, nxt, cur, nxt, cur, nxt] + [ANY] * n_ex,
            out_specs=[cur, cur, cur] + [ANY] * n_ex,
            scratch_shapes=_exchange_scratch(n_ex)),
        out_shape=[jax.ShapeDtypeStruct((3, S, B_OUT), BF16)] * 3 + _exchange_shapes(scatter, True),
        compiler_params=_params("arbitrary", "arbitrary"),
    )(slopes, qp, qp, kp, kp, vp, vp, dop, dop, lsep, lsep, dlp, dlp, *scatter)


def _rows_block(shape, max_bytes=2 * 1024 * 1024):
    rows, cols = shape
    padded_cols = -(-cols // LANES) * LANES
    for tr in (1024, 512, 256, 128, 64, 32, 16):
        if rows % tr == 0 and tr * padded_cols * 4 <= max_bytes:
            return tr
    return rows


def _adam_update(w, m, v, g):
    m_new = ADAM_B1 * m + (1.0 - ADAM_B1) * g
    v_new = ADAM_B2 * v + (1.0 - ADAM_B2) * (g * g)
    m_hat = m_new / (1.0 - ADAM_B1 ** ADAM_STEP)
    v_hat = v_new / (1.0 - ADAM_B2 ** ADAM_STEP)
    delta = -ADAM_LR * (m_hat / (jnp.sqrt(v_hat) + ADAM_EPS) + ADAM_WD * w)
    return delta, m_new, v_new


def _adamw_sharded(w, m, v, parts, name):
    R, C = w.shape
    tr = _rows_block((R, C), max_bytes=1024 * 1024)

    def body(w_ref, m_ref, v_ref, p_ref, g_ref, d_ref, mo_ref, vo_ref):
        g = p_ref[0].astype(F32)
        for dev in range(1, N_DEV):
            g = g + p_ref[dev].astype(F32)
        g_ref[...] = g
        d_ref[...], mo_ref[...], vo_ref[...] = _adam_update(w_ref[...], m_ref[...], v_ref[...], g)

    blk = pl.BlockSpec((tr, C), lambda i: (i, 0))
    out = jax.ShapeDtypeStruct((R, C), F32)
    return pl.pallas_call(
        body, name=name, grid=(R // tr,),
        in_specs=[blk, blk, blk, pl.BlockSpec((N_DEV, tr, C), lambda i: (0, i, 0))],
        out_specs=[blk, blk, blk, blk], out_shape=[out, out, out, out],
        compiler_params=_params("parallel"),
    )(w, m, v, parts)


def _adamw_replicated(w, m, v, parts, name):
    def body(w_ref, m_ref, v_ref, p_ref, g_ref, d_ref, mo_ref, vo_ref):
        g = p_ref[0]
        for dev in range(1, N_DEV):
            g = g + p_ref[dev]
        g_ref[...] = g
        d_ref[...], mo_ref[...], vo_ref[...] = _adam_update(w_ref[...], m_ref[...], v_ref[...], g)

    out = jax.ShapeDtypeStruct(w.shape, F32)
    return pl.pallas_call(body, name=name, out_shape=[out, out, out, out], compiler_params=_params())(w, m, v, parts)


def _cols_from_slots(g):
    return g.transpose(1, 0, 2).reshape(g.shape[1], N_DEV * g.shape[2])


def _cols_to_slots(w):
    k, n = w.shape
    return w.reshape(k, N_DEV, n // N_DEV).transpose(1, 0, 2)


def _permute(t, dil):
    c = t.shape[1]
    return t.reshape(S // dil, dil, c).transpose(1, 0, 2).reshape(S, c)


def _unpermute(t, dil):
    c = t.shape[1]
    return t.reshape(dil, S // dil, c).transpose(1, 0, 2).reshape(S, c)


def _group_permute(t):
    return jnp.stack([_permute(t[:, g * B_OUT:(g + 1) * B_OUT], B_DILS[g]) for g in range(3)])


def _same_permute(t):
    return jnp.stack([_permute(t, d) for d in B_DILS])


def _group_unpermute(t):
    return jnp.stack([_unpermute(t[g], B_DILS[g]) for g in range(3)])


SMALL_ROWS = 144


def _pack_small(a_b_f, kv_g, mix_g, ffn_g, conv_b, fin_g):
    flat = jnp.concatenate([a_b_f.reshape(-1), kv_g.reshape(-1), mix_g.reshape(-1), ffn_g.reshape(-1),
                            conv_b.reshape(-1), fin_g.reshape(-1)])
    return jnp.pad(flat, (0, SMALL_ROWS * LANES - flat.shape[0])).reshape(SMALL_ROWS, LANES)


def _unpack_small(p):
    flat = p.reshape(-1)
    out, off = [], 0
    for shape in ((1, A_HEADS), (D,), (2, D), (2, D), (2, 2 * D_FF), (D,)):
        size = math.prod(shape)
        out.append(flat[off:off + size].reshape(shape))
        off += size
    return out


def _unpack_late(g):
    w_up = g[4].reshape(N_DEV, 2, D, -1).transpose(1, 2, 0, 3).reshape(2, D, 2 * D_FF)
    w_down = g[5].reshape(N_DEV, 2, -1, D).transpose(1, 0, 2, 3).reshape(2, D_FF, D)
    conv_w = g[6].reshape(N_DEV, 2, 3, -1).transpose(1, 2, 0, 3).reshape(2, 3, 2 * D_FF)
    return (g[0].reshape(D, D), _cols_from_slots(g[1]), _cols_from_slots(g[2]), _cols_from_slots(g[3]),
            w_up, w_down, conv_w)


def _ffn_slots(dw_up, dw_down, dconv_w):
    return [_cols_to_slots(dw_up), dw_down.reshape(N_DEV, -1, D), _cols_to_slots(dconv_w)]


def _local_step(x0, target, w_in_pad, late_shards,
                a_b_f, kv_norm_g, mix_norm_g, ffn_norm_g, ffn_conv_b, final_norm_g):
    w_qkv, w_f = w_in_pad[:, :A_QKV], w_in_pad[:, A_QKV:]
    conv_b = ffn_conv_b.reshape(2, 1, 2 * D_FF)
    slopes = jnp.exp2(-8.0 * jnp.arange(1, 25, dtype=F32) / 24)

    def gain(g):
        return g.reshape(1, D)

    (h1,) = _rmsnorm_fwd(x0, [gain(mix_norm_g[0])], "norm_mix0")
    qkv = _matmul(h1, w_qkv, mode="nn", out_dtype=BF16, name="proj_qkv", tm=512, tn=A_QKV)
    z = _matmul(h1, w_f, mode="nn", out_dtype=F32, name="proj_gate", tm=S, tn=LANES)
    z_t = z[:, :A_HEADS].T
    b_f = a_b_f.reshape(A_HEADS, 1)
    c_t = _fox_prep_fwd(z_t, b_f, "fox_prep")
    c_t2 = c_t.reshape(N_PAIRS, 2, S)
    o_a, lse_a, *late = _fox_fwd(qkv, c_t2, "fox_fwd", late_shards)
    w_out, w_q, w_bo, w_kvf, w_up, w_down, conv_w = _unpack_late(late)
    x1 = _matmul(o_a, w_out, mode="nn", out_dtype=F32, name="a_out", tm=512, tn=D, res=x0)

    def ffn_fwd(xin, layer):
        (h,) = _rmsnorm_fwd(xin, [gain(ffn_norm_g[layer])], f"norm_ffn{layer}")
        u = _matmul(h, w_up[layer], mode="nn", out_dtype=BF16, name=f"ffn_up{layer}", tm=512, tn=2 * D_FF)
        act = _convgate_fwd(u, conv_w[layer], conv_b[layer], f"convgate{layer}")
        xout = _matmul(act, w_down[layer], mode="nn", out_dtype=F32, name=f"ffn_down{layer}", tm=512, tn=D, res=xin)
        return h, u, act, xout

    h2, u0, act0, x2 = ffn_fwd(x1, 0)
    hk, h3 = _rmsnorm_fwd(x2, [gain(kv_norm_g), gain(mix_norm_g[1])], "norm_kv_mix1")
    kv = _matmul(hk, w_kvf, mode="nn", out_dtype=BF16, name="proj_kv", tm=512, tn=B_KV)
    qb = _matmul(h3, w_q, mode="nn", out_dtype=BF16, name="proj_qb", tm=512, tn=B_Q)
    qp, kp, vp = _group_permute(qb), _group_permute(kv[:, :B_Q]), _group_permute(kv[:, B_Q:])
    og_p, lseg_p = _dil_fwd(qp, kp, vp, slopes, "dil_fwd")
    o_b, lse_b = _dil_merge(_group_unpermute(og_p), _group_unpermute(lseg_p), "dil_merge")
    x3 = _matmul(o_b, w_bo, mode="nn", out_dtype=F32, name="b_out", tm=512, tn=D, res=x2)
    h4, u1, act1, x4 = ffn_fwd(x3, 1)
    loss_blk, dx4, dx4b, dg_final = _final_loss(x4, target, gain(final_norm_g), "final_loss")

    def ffn_bwd(dx, dxb, xin, h, u, act, layer):
        dact = _matmul(dxb, w_down[layer], mode="nt", out_dtype=BF16, name=f"d_act{layer}", tm=512, tn=D_FF)
        dw_down = _matmul_tn(act, dxb, out_dtype=BF16, name=f"dw_down{layer}")
        du_a, du_g, dwa, dwg, dba, dbg = _convgate_bwd(u, conv_w[layer], conv_b[layer], dact, f"convgate_bwd{layer}")
        dw_up = jnp.concatenate(
            [_matmul_tn(h, du_a, out_dtype=BF16, name=f"dw_up_a{layer}"),
             _matmul_tn(h, du_g, out_dtype=BF16, name=f"dw_up_g{layer}")], axis=1)
        dh = _matmul(du_a, w_up[layer][:, :D_FF], mode="nt", out_dtype=F32, name=f"dh_ffn_a{layer}", tm=512, tn=D)
        dh = _matmul(du_g, w_up[layer][:, D_FF:], mode="nt", out_dtype=F32, name=f"dh_ffn_g{layer}", tm=512, tn=D,
                     res=dh)
        dxin, dxinb, dgain = _rmsnorm_bwd(xin, dh, gain(ffn_norm_g[layer]), dx, f"norm_ffn_bwd{layer}")
        dconv_w = jnp.concatenate([dwa, dwg], axis=1)
        dconv_b = jnp.concatenate([dba, dbg], axis=1)
        return dxin, dxinb, dgain, dw_up, dw_down, dconv_w, dconv_b

    dx3, dx3b, dg_ffn1, dw_up1, dw_down1, dconv_w1, dconv_b1 = ffn_bwd(dx4, dx4b, x3, h4, u1, act1, 1)

    do_b = _matmul(dx3b, w_bo, mode="nt", out_dtype=BF16, name="d_ob", tm=1024, tn=B_OUT)
    dw_bo = _matmul_tn(o_b, dx3b, out_dtype=BF16, name="dw_bo")
    dl_b = _head_rowsum(do_b, o_b, "delta_b")
    slots_up1, slots_down1, slots_conv1 = _ffn_slots(dw_up1, dw_down1, dconv_w1)
    dqp, dkp, dvp, land_down1, land_conv1 = _dil_bwd(
        qp, kp, vp, _same_permute(do_b), _same_permute(lse_b), _same_permute(dl_b), slopes, "dil_bwd",
        [slots_down1, slots_conv1])

    def natural(tp):
        return jnp.concatenate([_unpermute(tp[g], B_DILS[g]) for g in range(3)], axis=1)

    dqb = natural(dqp)
    dkv = jnp.concatenate([natural(dkp), natural(dvp)], axis=1)
    dw_q = _matmul_tn(h3, dqb, out_dtype=BF16, name="dw_q")
    dw_kv = _matmul_tn(hk, dkv, out_dtype=BF16, name="dw_kv")
    dh3 = _matmul(dqb, w_q, mode="nt", out_dtype=F32, name="dh_mix1", tm=512, tn=D)
    dhk = _matmul(dkv, w_kvf, mode="nt", out_dtype=F32, name="dh_kv", tm=512, tn=D)
    dx2, _, dg_mix1 = _rmsnorm_bwd(x2, dh3, gain(mix_norm_g[1]), dx3, "norm_mix1_bwd")
    dx2, dx2b, dg_kv = _rmsnorm_bwd(x2, dhk, gain(kv_norm_g), dx2, "norm_kv_bwd")

    dx1, dx1b, dg_ffn0, dw_up0, dw_down0, dconv_w0, dconv_b0 = ffn_bwd(dx2, dx2b, x1, h2, u0, act0, 0)

    do_a = _matmul(dx1b, w_out, mode="nt", out_dtype=BF16, name="d_oa", tm=512, tn=D)
    dw_out = _matmul_tn(o_a, dx1b, out_dtype=BF16, name="dw_out")
    dl_a = _head_rowsum(do_a, o_a, "delta_a")
    dq_a, dk_a, dv_a, dcol, drow, *land = _fox_bwd(
        qkv, do_a, lse_a, dl_a, c_t2, "fox_bwd",
        [dw_out.reshape(N_DEV, D // N_DEV, D), _cols_to_slots(dw_q), _cols_to_slots(dw_bo), _cols_to_slots(dw_kv)]
        + _ffn_slots(dw_up0, dw_down0, dconv_w0) + [slots_up1])
    land_out, land_q, land_bo, land_kv, land_up0, land_down0, land_conv0, land_up1 = land

    def head_sums(t):
        return t.reshape(S, N_PAIRS, 2, HEAD_DIM)[:, :, ::-1, 0].reshape(S, A_HEADS).T

    dz_t, db_f = _fox_prep_bwd(head_sums(drow), head_sums(dcol), z_t, b_f, "fox_prep_bwd")
    dz = jnp.pad(dz_t.T, ((0, 0), (0, LANES - A_HEADS))).astype(BF16)
    dproj = jnp.concatenate([dq_a.astype(BF16), dk_a, dv_a, dz], axis=1)
    dw_in = _matmul_tn(h1, dproj, out_dtype=BF16, name="dw_in")
    dh1, land_in = _matmul(dproj, w_in_pad, mode="nt", out_dtype=F32, name="dh_mix0", tm=512, tn=D,
                           scatter=[_cols_to_slots(dw_in[:, :A_QKV + A_HEADS])])
    grad_x, _, dg_mix0 = _rmsnorm_bwd(x0, dh1, gain(mix_norm_g[0]), dx1, "norm_mix0_bwd")

    dg_mix = jnp.concatenate([dg_mix0, dg_mix1], axis=0)
    dg_ffn = jnp.concatenate([dg_ffn0, dg_ffn1], axis=0)
    dconv_b = jnp.concatenate([dconv_b0, dconv_b1], axis=0)
    small_part = _pack_small(db_f, dg_kv, dg_mix, dg_ffn, dconv_b, dg_final)
    _, (small_parts,) = _final_exchange([], [small_part], "gather_small_grads")
    landed = [land_in, land_out, land_q, land_bo, land_kv, land_up0, land_up1, land_down0, land_down1,
              land_conv0, land_conv1]
    return loss_blk, grad_x, landed, small_parts


def kernel(x, a_w_in, a_b_f, a_w_out, b_w_q, b_w_out, kv_norm_g, w_kv, mix_norm_g, ffn_norm_g, ffn_w_up, ffn_conv_w, ffn_conv_b, ffn_w_down, final_norm_g, loss_target, m_a_w_in, m_a_b_f, m_a_w_out, m_b_w_q, m_b_w_out, m_kv_norm_g, m_w_kv, m_mix_norm_g, m_ffn_norm_g, m_ffn_w_up, m_ffn_conv_w, m_ffn_conv_b, m_ffn_w_down, m_final_norm_g, v_a_w_in, v_a_b_f, v_a_w_out, v_b_w_q, v_b_w_out, v_kv_norm_g, v_w_kv, v_mix_norm_g, v_ffn_norm_g, v_ffn_w_up, v_ffn_conv_w, v_ffn_conv_b, v_ffn_w_down, v_final_norm_g):
    def shards(a_w_in, a_w_out, b_w_q, b_w_out, w_kv, ffn_w_up, ffn_w_down, ffn_conv_w):
        return [a_w_in[0], a_w_out[0], b_w_q[0], b_w_out[0], w_kv, ffn_w_up[0], ffn_w_up[1],
                ffn_w_down[0], ffn_w_down[1], ffn_conv_w[0], ffn_conv_w[1]]

    w_loc = shards(a_w_in, a_w_out, b_w_q, b_w_out, w_kv, ffn_w_up, ffn_w_down, ffn_conv_w)
    m_loc = shards(m_a_w_in, m_a_w_out, m_b_w_q, m_b_w_out, m_w_kv, m_ffn_w_up, m_ffn_w_down, m_ffn_conv_w)
    v_loc = shards(v_a_w_in, v_a_w_out, v_b_w_q, v_b_w_out, v_w_kv, v_ffn_w_up, v_ffn_w_down, v_ffn_conv_w)

    (g_in,) = _all_gather([a_w_in[0].astype(BF16)], "gather_a_w_in")
    w_in = _cols_from_slots(g_in)
    w_in_pad = jnp.pad(w_in, ((0, 0), (0, A_PROJ_PAD - w_in.shape[1])))
    late_shards = [a_w_out[0].astype(BF16), b_w_q[0].astype(BF16), b_w_out[0].astype(BF16), w_kv.astype(BF16),
                   ffn_w_up.reshape(2 * D, -1).astype(BF16), ffn_w_down.reshape(-1, D).astype(BF16),
                   ffn_conv_w.reshape(6, -1)]

    loss_blk, grad_x, landed, small_parts = _local_step(
        x[0], loss_target[0], w_in_pad, late_shards,
        a_b_f, kv_norm_g, mix_norm_g, ffn_norm_g, ffn_conv_b, final_norm_g)

    big = [_adamw_sharded(w_loc[k], m_loc[k], v_loc[k], landed[k], f"adamw{k}") for k in range(11)]

    small = _adamw_replicated(
        _pack_small(a_b_f, kv_norm_g, mix_norm_g, ffn_norm_g, ffn_conv_b, final_norm_g),
        _pack_small(m_a_b_f, m_kv_norm_g, m_mix_norm_g, m_ffn_norm_g, m_ffn_conv_b, m_final_norm_g),
        _pack_small(v_a_b_f, v_kv_norm_g, v_mix_norm_g, v_ffn_norm_g, v_ffn_conv_b, v_final_norm_g),
        small_parts, "adamw_small")

    loss = lax.psum(loss_blk[0, 0], ("x", "y", "c"))

    def assemble(kind):
        b = [r[kind] for r in big]
        s_abf, s_kv, s_mix, s_ffn, s_cb, s_fin = _unpack_small(small[kind])
        return [b[0][None], s_abf, b[1][None], b[2][None], b[3][None], s_kv, b[4], s_mix, s_ffn,
                jnp.stack([b[5], b[6]]), jnp.stack([b[9], b[10]]), s_cb, jnp.stack([b[7], b[8]]), s_fin]

    return (loss, grad_x[None], *assemble(0), *assemble(1), *assemble(2), *assemble(3))
```

```python
import functools
import math

import jax
import jax.numpy as jnp
from jax import lax
from jax.experimental import pallas as pl
from jax.experimental.pallas import tpu as pltpu

F32 = jnp.float32
BF16 = jnp.bfloat16

S = 4096
D = 1024
N_DEV = 8
A_HEADS = 16
HEAD_DIM = 64
A_QKV = 3072
A_PROJ_PAD = 3200
B_Q = 1536
B_OUT = 512
B_KV = 3072
B_W = 128
B_DILS = (1, 4, 16)
D_FF = 2816
RMS_EPS = 1e-6
SCALE = HEAD_DIM ** -0.5
NEG = -1e30

ADAM_LR = 0.001
ADAM_B1 = 0.9
ADAM_B2 = 0.999
ADAM_EPS = 1e-08
ADAM_WD = 0.01
ADAM_STEP = 10

LANES = 128
VMEM_LIMIT = 56 * 1024 * 1024
MESH = pl.DeviceIdType.MESH
ANY = pl.BlockSpec(memory_space=pl.ANY)

NT_DIMS = (((1,), (1,)), ((), ()))
TN_DIMS = (((0,), (0,)), ((), ()))
NN_DIMS = (((1,), (0,)), ((), ()))


def _params(*sem):
    return pltpu.CompilerParams(dimension_semantics=sem if sem else None, vmem_limit_bytes=VMEM_LIMIT)


def _dot(a, b, dims=NN_DIMS):
    return lax.dot_general(a, b, dims, preferred_element_type=F32)


def _split_dot(x, mat, pieces):
    out = None
    rem = x
    for _ in range(pieces):
        part = rem.astype(BF16)
        rem = rem - part.astype(F32)
        d = _dot(part, mat)
        out = d if out is None else out + d
    return out


def _pick(n, prefs):
    for p in prefs:
        if n % p == 0:
            return p
    return n


def _all_gather(arrays, name):
    n = len(arrays)

    def body(*refs):
        ins = refs[:n]
        outs = refs[n:2 * n]
        send_sems, recv_sems, local_sems = refs[2 * n:]
        x, y, c = lax.axis_index("x"), lax.axis_index("y"), lax.axis_index("c")
        me, sibling = (x, y, c), (x, y, 1 - c)
        chips = [(1 - x, y), (x, 1 - y), (1 - x, 1 - y)]

        def slot(a, px, py, pc):
            return outs[a].at[4 * px + 2 * py + pc]

        def copy(a, k, block, to, src=None):
            return pltpu.make_async_remote_copy(
                src_ref=slot(a, *block) if src is None else src, dst_ref=slot(a, *block),
                send_sem=send_sems.at[a, k], recv_sem=recv_sems.at[a, k],
                device_id=to, device_id_type=MESH)

        mine = [pltpu.make_async_copy(ins[a], slot(a, *me), local_sems.at[a]) for a in range(n)]
        for cp in mine:
            cp.start()
        first = []
        for a in range(n):
            first.append(copy(a, 0, me, sibling, src=ins[a]))
            first += [copy(a, 1 + j, me, (*chip, c), src=ins[a]) for j, chip in enumerate(chips)]
        for cp in first:
            cp.start()
        passed = []
        for j, chip in enumerate(chips):
            for a in range(n):
                copy(a, 1 + j, (*chip, c), me).wait_recv()
                fwd = copy(a, 4 + j, (*chip, c), sibling)
                fwd.start()
                passed.append(fwd)
        for a in range(n):
            copy(a, 0, sibling, me).wait_recv()
            for j, chip in enumerate(chips):
                copy(a, 4 + j, (*chip, 1 - c), me).wait_recv()
        for cp in first + passed:
            cp.wait_send()
        for cp in mine:
            cp.wait()

    return pl.pallas_call(
        body, name=name,
        out_shape=[jax.ShapeDtypeStruct((N_DEV,) + a.shape, a.dtype) for a in arrays],
        in_specs=[ANY] * n, out_specs=[ANY] * n,
        scratch_shapes=[pltpu.SemaphoreType.DMA((n, 7)), pltpu.SemaphoreType.DMA((n, 7)),
                        pltpu.SemaphoreType.DMA((n,))],
    )(*arrays)


PEER_FLIPS = [(dx, dy, dc) for dx in (0, 1) for dy in (0, 1) for dc in (0, 1) if (dx, dy, dc) != (0, 0, 0)]


def _exchange_copies(ins, outs, sems, scatter):
    if not ins:
        return []
    send_sems, recv_sems, local_sems = sems
    x, y, c = lax.axis_index("x"), lax.axis_index("y"), lax.axis_index("c")
    me = 4 * x + 2 * y + c
    copies = []
    for a in range(len(ins)):
        copies.append(pltpu.make_async_copy(ins[a].at[me] if scatter else ins[a], outs[a].at[me], local_sems.at[a]))
        for k, (dx, dy, dc) in enumerate(PEER_FLIPS):
            px, py, pc = (1 - x if dx else x), (1 - y if dy else y), (1 - c if dc else c)
            copies.append(pltpu.make_async_remote_copy(
                src_ref=ins[a].at[4 * px + 2 * py + pc] if scatter else ins[a], dst_ref=outs[a].at[me],
                send_sem=send_sems.at[a, k], recv_sem=recv_sems.at[a, k],
                device_id=(px, py, pc), device_id_type=MESH))
    return copies


def _exchange_scratch(n):
    if n == 0:
        return []
    return [pltpu.SemaphoreType.DMA((n, 7)), pltpu.SemaphoreType.DMA((n, 7)), pltpu.SemaphoreType.DMA((n,))]


def _exchange_shapes(arrays, scatter):
    return [jax.ShapeDtypeStruct((N_DEV,) + (a.shape[1:] if scatter else a.shape), a.dtype) for a in arrays]


def _final_exchange(scatter, gather, name):
    ns, ng = len(scatter), len(gather)

    def body(*refs):
        ins, outs, sems = refs[:ns + ng], refs[ns + ng:2 * (ns + ng)], refs[2 * (ns + ng):]
        n_sems = len(_exchange_scratch(ns))
        copies = (_exchange_copies(ins[:ns], outs[:ns], sems[:n_sems], True)
                  + _exchange_copies(ins[ns:], outs[ns:], sems[n_sems:], False))
        for cp in copies:
            cp.start()
        for cp in copies:
            cp.wait()

    res = pl.pallas_call(
        body, name=name, out_shape=_exchange_shapes(scatter, True) + _exchange_shapes(gather, False),
        in_specs=[ANY] * (ns + ng), out_specs=[ANY] * (ns + ng),
        scratch_shapes=_exchange_scratch(ns) + _exchange_scratch(ng),
    )(*scatter, *gather)
    return res[:ns], res[ns:]


MM_ROWS = 512
MM_COLS = 1024


def _matmul(a, b, *, mode, out_dtype, name, tm, tn, res=None, scatter=()):
    if mode == "nn":
        (M, K), (K2, N) = a.shape, b.shape
    else:
        (M, K), (N, K2) = a.shape, b.shape
    assert K == K2, (a.shape, b.shape, mode)
    tm, tn = min(tm, M), min(tn, N)
    sm = min(tm, MM_ROWS)
    sn = tn if tn <= MM_COLS else _pick(tn, (512, 256, 128))
    assert M % tm == 0 and N % tn == 0 and tm % sm == 0, (M, N, K, tm, tn)
    dims = NN_DIMS if mode == "nn" else NT_DIMS
    a_spec = pl.BlockSpec((tm, K), lambda i, j: (i, 0))
    if mode == "nt":
        b_spec = pl.BlockSpec((tn, K), lambda i, j: (j, 0))
    else:
        b_spec = pl.BlockSpec((K, tn), lambda i, j: (0, j))
    o_spec = pl.BlockSpec((tm, tn), lambda i, j: (i, j))
    has_res = res is not None
    n_in, n_ex = 2 + has_res, len(scatter)
    gm, gn = M // tm, N // tn

    def body(*refs):
        a_ref, b_ref = refs[0], refs[1]
        r_ref = refs[2] if has_res else None
        o_ref = refs[n_in + n_ex]
        exchange = (refs[n_in:n_in + n_ex], refs[n_in + n_ex + 1:n_in + 2 * n_ex + 1], refs[n_in + 2 * n_ex + 1:], True)

        @pl.when(jnp.logical_and(pl.program_id(0) == 0, pl.program_id(1) == 0))
        def _():
            for cp in _exchange_copies(*exchange):
                cp.start()

        def chunk(r, carry):
            rows = pl.ds(pl.multiple_of(r * sm, sm), sm)
            av = a_ref[rows, :]
            for c0 in range(0, tn, sn):
                bv = b_ref[c0:c0 + sn, :] if mode == "nt" else b_ref[:, c0:c0 + sn]
                total = _dot(av, bv, dims)
                if has_res:
                    total = total + r_ref[rows, c0:c0 + sn]
                o_ref[rows, c0:c0 + sn] = total.astype(out_dtype)
            return carry

        lax.fori_loop(0, tm // sm, chunk, 0)

        @pl.when(jnp.logical_and(pl.program_id(0) == gm - 1, pl.program_id(1) == gn - 1))
        def _():
            for cp in _exchange_copies(*exchange):
                cp.wait()

    out = pl.pallas_call(
        body, name=name, grid=(gm, gn),
        out_shape=[jax.ShapeDtypeStruct((M, N), out_dtype)] + _exchange_shapes(scatter, True),
        in_specs=[a_spec, b_spec] + ([o_spec] if has_res else []) + [ANY] * n_ex,
        out_specs=[o_spec] + [ANY] * n_ex,
        scratch_shapes=_exchange_scratch(n_ex),
        compiler_params=_params("arbitrary", "arbitrary"),
    )(*((a, b, res) if has_res else (a, b)), *scatter)
    return out if n_ex else out[0]


def _matmul_tn(a, b, *, out_dtype, name, tk=512, sm=256):
    (K, M), (K2, N) = a.shape, b.shape
    assert K == K2 and K % tk == 0 and M % sm == 0, (a.shape, b.shape)
    nk = K // tk

    def body(a_ref, b_ref, o_ref, acc_ref):
        k = pl.program_id(0)

        @pl.when(k == 0)
        def _():
            acc_ref[...] = jnp.zeros_like(acc_ref)

        def chunk(mi, carry):
            cols = pl.ds(pl.multiple_of(mi * sm, sm), sm)
            acc_ref[cols, :] += _dot(a_ref[:, cols].T, b_ref[...])
            return carry

        lax.fori_loop(0, M // sm, chunk, 0)

        @pl.when(k == nk - 1)
        def _():
            def emit(mi, carry):
                rows = pl.ds(pl.multiple_of(mi * sm, sm), sm)
                o_ref[rows, :] = acc_ref[rows, :].astype(out_dtype)
                return carry
            lax.fori_loop(0, M // sm, emit, 0)

    return pl.pallas_call(
        body, name=name, grid=(nk,),
        out_shape=jax.ShapeDtypeStruct((M, N), out_dtype),
        in_specs=[pl.BlockSpec((tk, M), lambda k: (k, 0)), pl.BlockSpec((tk, N), lambda k: (k, 0))],
        out_specs=pl.BlockSpec((M, N), lambda k: (0, 0)),
        scratch_shapes=[pltpu.VMEM((M, N), F32)],
        compiler_params=_params("arbitrary"),
    )(a, b)


def _rmsnorm_fwd(x, gains, name, tr=256):
    n = len(gains)

    def body(*refs):
        x_ref = refs[0]
        xv = x_ref[...]
        r = lax.rsqrt(jnp.mean(xv * xv, axis=-1, keepdims=True) + RMS_EPS)
        y = xv * r
        for a in range(n):
            refs[1 + n + a][...] = (y * refs[1 + a][...]).astype(BF16)

    row = pl.BlockSpec((tr, D), lambda i: (i, 0))
    gain = pl.BlockSpec((1, D), lambda i: (0, 0))
    return pl.pallas_call(
        body, name=name, grid=(S // tr,),
        out_shape=[jax.ShapeDtypeStruct((S, D), BF16)] * n,
        in_specs=[row] + [gain] * n, out_specs=[row] * n,
        compiler_params=_params("parallel"),
    )(x, *gains)


def _rmsnorm_bwd(x, dy, g, dres, name, tr=256):
    def body(x_ref, dy_ref, g_ref, dres_ref, dx_ref, dxb_ref, dg_ref):
        xv = x_ref[...]
        dyv = dy_ref[...]
        r = lax.rsqrt(jnp.mean(xv * xv, axis=-1, keepdims=True) + RMS_EPS)
        xhat = xv * r
        dxhat = dyv * g_ref[...]
        mean_term = jnp.mean(dxhat * xhat, axis=-1, keepdims=True)
        dx = r * (dxhat - xhat * mean_term) + dres_ref[...]
        dx_ref[...] = dx
        dxb_ref[...] = dx.astype(BF16)
        part = jnp.sum(dyv * xhat, axis=0, keepdims=True)

        @pl.when(pl.program_id(0) == 0)
        def _():
            dg_ref[...] = part

        @pl.when(pl.program_id(0) > 0)
        def _():
            dg_ref[...] += part

    row = pl.BlockSpec((tr, D), lambda i: (i, 0))
    gain = pl.BlockSpec((1, D), lambda i: (0, 0))
    return pl.pallas_call(
        body, name=name, grid=(S // tr,),
        out_shape=[jax.ShapeDtypeStruct((S, D), F32), jax.ShapeDtypeStruct((S, D), BF16),
                   jax.ShapeDtypeStruct((1, D), F32)],
        in_specs=[row, row, gain, row], out_specs=[row, row, gain],
        compiler_params=_params("arbitrary"),
    )(x, dy, g, dres)


def _final_loss(x, target, g, name, tr=256):
    def body(x_ref, t_ref, g_ref, loss_ref, dx_ref, dxb_ref, dg_ref):
        xv = x_ref[...]
        gv = g_ref[...]
        r = lax.rsqrt(jnp.mean(xv * xv, axis=-1, keepdims=True) + RMS_EPS)
        xhat = xv * r
        err = xhat * gv - t_ref[...]
        row_loss = jnp.mean(err * err, axis=-1, keepdims=True)
        lpart = 0.5 * jnp.sum(row_loss, axis=0, keepdims=True)
        dyv = err / D
        dxhat = dyv * gv
        mean_term = jnp.mean(dxhat * xhat, axis=-1, keepdims=True)
        dx = r * (dxhat - xhat * mean_term)
        dx_ref[...] = dx
        dxb_ref[...] = dx.astype(BF16)
        gpart = jnp.sum(dyv * xhat, axis=0, keepdims=True)

        @pl.when(pl.program_id(0) == 0)
        def _():
            dg_ref[...] = gpart
            loss_ref[...] = jnp.broadcast_to(lpart, loss_ref.shape)

        @pl.when(pl.program_id(0) > 0)
        def _():
            dg_ref[...] += gpart
            loss_ref[...] += jnp.broadcast_to(lpart, loss_ref.shape)

    row = pl.BlockSpec((tr, D), lambda i: (i, 0))
    gain = pl.BlockSpec((1, D), lambda i: (0, 0))
    lspec = pl.BlockSpec((8, LANES), lambda i: (0, 0))
    return pl.pallas_call(
        body, name=name, grid=(S // tr,),
        out_shape=[jax.ShapeDtypeStruct((8, LANES), F32), jax.ShapeDtypeStruct((S, D), F32),
                   jax.ShapeDtypeStruct((S, D), BF16), jax.ShapeDtypeStruct((1, D), F32)],
        in_specs=[row, row, gain], out_specs=[lspec, row, row, gain],
        compiler_params=_params("arbitrary"),
    )(x, target, g)


CONV_TR = 128
CONV_TC = D_FF
CONV_NJ = D_FF // CONV_TC
HALO = 16


def _causal_taps(cur_ref, prev_ref, first):
    xv = cur_ref[...].astype(F32)
    pv = prev_ref[...].astype(F32)
    p1 = jnp.where(first, 0.0, pv[HALO - 1:HALO, :])
    p2 = jnp.where(first, 0.0, pv[HALO - 2:HALO - 1, :])
    r1, r2 = pltpu.roll(xv, 1, 0), pltpu.roll(xv, 2, 0)
    row = lax.broadcasted_iota(jnp.int32, (8, xv.shape[1]), 0)
    xm1 = jnp.concatenate([jnp.where(row == 0, p1, r1[0:8]), r1[8:]], axis=0)
    xm2 = jnp.concatenate([jnp.where(row == 0, p2, jnp.where(row == 1, p1, r2[0:8])), r2[8:]], axis=0)
    return xv, xm1, xm2


def _conv_specs():
    def prev_row(i):
        return jnp.maximum(i * (CONV_TR // HALO) - 1, 0)
    ua = pl.BlockSpec((CONV_TR, CONV_TC), lambda i, j: (i, j))
    ug = pl.BlockSpec((CONV_TR, CONV_TC), lambda i, j: (i, j + CONV_NJ))
    pa = pl.BlockSpec((HALO, CONV_TC), lambda i, j: (prev_row(i), j))
    pg = pl.BlockSpec((HALO, CONV_TC), lambda i, j: (prev_row(i), j + CONV_NJ))
    wa = pl.BlockSpec((3, CONV_TC), lambda i, j: (0, j))
    wg = pl.BlockSpec((3, CONV_TC), lambda i, j: (0, j + CONV_NJ))
    ba = pl.BlockSpec((1, CONV_TC), lambda i, j: (0, j))
    bg = pl.BlockSpec((1, CONV_TC), lambda i, j: (0, j + CONV_NJ))
    return [ua, pa, ug, pg, wa, wg, ba, bg]


def _convgate_fwd(u, w, b, name):
    def body(ua, pa, ug, pg, wa, wg, ba, bg, o_ref):
        first = pl.program_id(0) == 0
        x0, x1, x2 = _causal_taps(ua, pa, first)
        ac = wa[0:1, :] * x2 + wa[1:2, :] * x1 + wa[2:3, :] * x0 + ba[...]
        x0, x1, x2 = _causal_taps(ug, pg, first)
        gc = wg[0:1, :] * x2 + wg[1:2, :] * x1 + wg[2:3, :] * x0 + bg[...]
        sg = 1.0 / (1.0 + jnp.exp(-gc))
        o_ref[...] = (gc * sg * ac).astype(BF16)

    return pl.pallas_call(
        body, name=name, grid=(S // CONV_TR, CONV_NJ),
        out_shape=jax.ShapeDtypeStruct((S, D_FF), BF16),
        in_specs=_conv_specs(),
        out_specs=pl.BlockSpec((CONV_TR, CONV_TC), lambda i, j: (i, j)),
        compiler_params=_params("parallel", "parallel"),
    )(u, u, u, u, w, w, b, b)


def _anticausal_conv(d, nxt_ref, w_ref, last):
    n1 = jnp.where(last, 0.0, nxt_ref[0:1, :])
    n2 = jnp.where(last, 0.0, nxt_ref[1:2, :])
    r1, r2 = pltpu.roll(d, CONV_TR - 1, 0), pltpu.roll(d, CONV_TR - 2, 0)
    row = lax.broadcasted_iota(jnp.int32, (8, d.shape[1]), 0)
    cut = CONV_TR - 8
    dp1 = jnp.concatenate([r1[:cut], jnp.where(row == 7, n1, r1[cut:])], axis=0)
    dp2 = jnp.concatenate([r2[:cut], jnp.where(row == 7, n2, jnp.where(row == 6, n1, r2[cut:]))], axis=0)
    return w_ref[2:3, :] * d + w_ref[1:2, :] * dp1 + w_ref[0:1, :] * dp2


def _convgate_bwd(u, w, b, dact, name):
    n_i = S // CONV_TR

    def body(ua, pa, ug, pg, wa, wg, ba, bg, d_ref, dua_ref, dug_ref, dwa_ref, dwg_ref, dba_ref, dbg_ref,
             nxt_a, nxt_g):
        i = pl.program_id(1)
        last = i == 0
        first = i == n_i - 1
        a0, a1, a2 = _causal_taps(ua, pa, first)
        ac = wa[0:1, :] * a2 + wa[1:2, :] * a1 + wa[2:3, :] * a0 + ba[...]
        g0, g1, g2 = _causal_taps(ug, pg, first)
        gc = wg[0:1, :] * g2 + wg[1:2, :] * g1 + wg[2:3, :] * g0 + bg[...]
        sg = 1.0 / (1.0 + jnp.exp(-gc))
        dact_v = d_ref[...].astype(F32)
        da = dact_v * (gc * sg)
        dg = dact_v * ac * (sg * (1.0 + gc * (1.0 - sg)))
        dua_ref[...] = _anticausal_conv(da, nxt_a, wa, last).astype(BF16)
        dug_ref[...] = _anticausal_conv(dg, nxt_g, wg, last).astype(BF16)
        nxt_a[...] = da[0:8]
        nxt_g[...] = dg[0:8]

        def col(v):
            return jnp.sum(v, axis=0, keepdims=True)

        parts = [col(da * a2), col(da * a1), col(da * a0), col(dg * g2), col(dg * g1), col(dg * g0),
                 col(da), col(dg)]

        @pl.when(last)
        def _():
            for k in range(3):
                dwa_ref[k:k + 1, :] = parts[k]
                dwg_ref[k:k + 1, :] = parts[3 + k]
            dba_ref[...] = parts[6]
            dbg_ref[...] = parts[7]

        @pl.when(i > 0)
        def _():
            for k in range(3):
                dwa_ref[k:k + 1, :] += parts[k]
                dwg_ref[k:k + 1, :] += parts[3 + k]
            dba_ref[...] += parts[6]
            dbg_ref[...] += parts[7]

    def swap(spec):
        return pl.BlockSpec(spec.block_shape, lambda j, i, f=spec.index_map: f(n_i - 1 - i, j))

    blk = pl.BlockSpec((CONV_TR, CONV_TC), lambda j, i: (n_i - 1 - i, j))
    w3 = pl.BlockSpec((3, CONV_TC), lambda j, i: (0, j))
    b1 = pl.BlockSpec((1, CONV_TC), lambda j, i: (0, j))
    return pl.pallas_call(
        body, name=name, grid=(CONV_NJ, n_i),
        out_shape=[jax.ShapeDtypeStruct((S, D_FF), BF16), jax.ShapeDtypeStruct((S, D_FF), BF16),
                   jax.ShapeDtypeStruct((3, D_FF), F32), jax.ShapeDtypeStruct((3, D_FF), F32),
                   jax.ShapeDtypeStruct((1, D_FF), F32), jax.ShapeDtypeStruct((1, D_FF), F32)],
        in_specs=[swap(s) for s in _conv_specs()] + [blk],
        out_specs=[blk, blk, w3, w3, b1, b1],
        scratch_shapes=[pltpu.VMEM((8, CONV_TC), F32), pltpu.VMEM((8, CONV_TC), F32)],
        compiler_params=_params("arbitrary", "arbitrary"),
    )(u, u, u, u, w, w, b, b, dact)


FOX_T = 512
FOX_TQ, FOX_TK = 512, 512
N_PAIRS = A_HEADS // 2


def _lane_masks():
    lane = lax.broadcasted_iota(jnp.int32, (1, LANES), 1)
    return lane, (lane < HEAD_DIM, lane >= HEAD_DIM)


def _fox_prep_fwd(z_t, b, name):
    def body(z_ref, b_ref, c_ref):
        r = lax.broadcasted_iota(jnp.int32, (LANES, LANES), 0)
        cc = lax.broadcasted_iota(jnp.int32, (LANES, LANES), 1)
        upper = (r <= cc).astype(BF16)
        carry = jnp.zeros((A_HEADS, 1), F32)
        for blk in range(S // LANES):
            sl = slice(blk * LANES, (blk + 1) * LANES)
            z = z_ref[:, sl] + b_ref[...]
            lf = jnp.minimum(z, 0.0) - jnp.log(1.0 + jnp.exp(-jnp.abs(z)))
            cs = _split_dot(lf, upper, 3) + carry
            c_ref[:, sl] = cs
            carry = cs[:, LANES - 1:LANES]

    return pl.pallas_call(
        body, name=name, out_shape=jax.ShapeDtypeStruct((A_HEADS, S), F32),
        compiler_params=_params(),
    )(z_t, b)


def _fox_prep_bwd(drow_t, dcol_t, z_t, b, name):
    def body(dr_ref, dc_ref, z_ref, b_ref, dz_ref, db_ref):
        r = lax.broadcasted_iota(jnp.int32, (LANES, LANES), 0)
        cc = lax.broadcasted_iota(jnp.int32, (LANES, LANES), 1)
        lower = (r >= cc).astype(BF16)
        carry = jnp.zeros((A_HEADS, 1), F32)
        db = jnp.zeros((A_HEADS, 1), F32)
        for blk in reversed(range(S // LANES)):
            sl = slice(blk * LANES, (blk + 1) * LANES)
            rc = _split_dot(dr_ref[:, sl] - dc_ref[:, sl], lower, 3) + carry
            carry = rc[:, 0:1]
            z = z_ref[:, sl] + b_ref[...]
            dz = rc / (1.0 + jnp.exp(z))
            dz_ref[:, sl] = dz
            db = db + jnp.sum(dz, axis=1, keepdims=True)
        db_ref[...] = db

    return pl.pallas_call(
        body, name=name,
        out_shape=[jax.ShapeDtypeStruct((A_HEADS, S), F32), jax.ShapeDtypeStruct((A_HEADS, 1), F32)],
        compiler_params=_params(),
    )(drow_t, dcol_t, z_t, b)


def _fox_fwd(qkv, c_t2, name, gather):
    tq, tk = FOX_TQ, FOX_TK

    n = len(gather)

    def body(*refs):
        q_ref, k_ref, v_ref, ct_ref = refs[:4]
        o_ref, lse_ref = refs[4 + n:6 + n]
        exchange = (refs[4:4 + n], refs[6 + n:6 + 2 * n], refs[6 + 2 * n:len(refs) - 5], False)
        s_scr, p_scr, acc_scr = refs[-5:-3], refs[-3:-1], refs[-1]
        qi = pl.program_id(1)

        @pl.when(jnp.logical_and(pl.program_id(0) == 0, qi == 0))
        def _():
            for cp in _exchange_copies(*exchange):
                cp.start()

        n_full = jnp.right_shift(qi, (tk // tq).bit_length() - 1)
        lane, masks = _lane_masks()
        q = q_ref[...] * SCALE
        qs = [jnp.where(masks[e], q, jnp.zeros_like(q)) for e in range(2)]

        def scores_into(j, slot):
            start = pl.multiple_of(j * tk, tk)
            kb = k_ref[pl.ds(start, tk), :]
            for e in range(2):
                s_scr[slot][e] = _dot(qs[e], kb, NT_DIMS) - ct_ref[e:e + 1, pl.ds(start, tk)]

        def softmax_of(slot, m, masked):
            m_new, alpha = [], []
            for e in range(2):
                s = s_scr[slot][e]
                if masked:
                    rows = lax.broadcasted_iota(jnp.int32, (tq, tk), 0) + (qi * tq - n_full * tk)
                    cols = lax.broadcasted_iota(jnp.int32, (tq, tk), 1)
                    s = jnp.where(cols <= rows, s, NEG)
                m_new.append(jnp.maximum(m[e], jnp.max(s, axis=1, keepdims=True)))
                p_scr[slot][e] = jnp.exp(s - m_new[e]).astype(BF16)
                alpha.append(jnp.exp(m[e] - m_new[e]))
            return tuple(m_new), tuple(alpha)

        def values_of(j, slot, alpha):
            start = pl.multiple_of(j * tk, tk)
            vb = v_ref[pl.ds(start, tk), :]
            for e in range(2):
                acc_scr[e] = (alpha[e] * acc_scr[e]
                              + _dot(p_scr[slot][e], jnp.where(masks[e], vb, jnp.ones_like(vb))))

        def stage(j, cur, nxt, carry):
            m, a_prev = carry
            scores_into(j + 1, nxt)
            values_of(jnp.maximum(j - 1, 0), nxt, a_prev)
            return softmax_of(cur, m, False)

        def finish(cur, nxt, carry):
            m, a_prev = carry
            values_of(jnp.maximum(n_full - 1, 0), nxt, a_prev)
            (m0, m1), alpha = softmax_of(cur, m, True)
            values_of(n_full, cur, alpha)
            l0 = acc_scr[0][:, HEAD_DIM:HEAD_DIM + 1]
            l1 = acc_scr[1][:, 0:1]
            o_ref[...] = jnp.where(masks[0], acc_scr[0] / l0, acc_scr[1] / l1).astype(BF16)
            lse_ref[...] = jnp.where(masks[0], m0 + jnp.log(l0), m1 + jnp.log(l1))

        scores_into(0, 0)
        for e in range(2):
            p_scr[1][e] = jnp.zeros((tq, tk), BF16)
            acc_scr[e] = jnp.zeros((tq, LANES), F32)
        two = lambda x: (x, x)
        init = (two(jnp.full((tq, 1), NEG, F32)), two(jnp.ones((tq, 1), F32)))

        def two_stages(jj, carry):
            return stage(2 * jj + 1, 1, 0, stage(2 * jj, 0, 1, carry))

        carry = lax.fori_loop(0, jnp.right_shift(n_full, 1), two_stages, init)
        odd = jnp.bitwise_and(n_full, 1) == 1

        @pl.when(odd)
        def _():
            finish(1, 0, stage(n_full - 1, 0, 1, carry))

        @pl.when(jnp.logical_not(odd))
        def _():
            finish(0, 1, carry)

        @pl.when(jnp.logical_and(pl.program_id(0) == N_PAIRS - 1, qi == S // tq - 1))
        def _():
            for cp in _exchange_copies(*exchange):
                cp.wait()

    qspec = pl.BlockSpec((tq, LANES), lambda h, i: (i, h))
    return pl.pallas_call(
        body, name=name, grid=(N_PAIRS, S // tq),
        out_shape=[jax.ShapeDtypeStruct((S, D), BF16), jax.ShapeDtypeStruct((S, D), F32)]
        + _exchange_shapes(gather, False),
        in_specs=[qspec,
                  pl.BlockSpec((S, LANES), lambda h, i: (0, N_PAIRS + h)),
                  pl.BlockSpec((S, LANES), lambda h, i: (0, 2 * N_PAIRS + h)),
                  pl.BlockSpec((None, 2, S), lambda h, i: (h, 0, 0))] + [ANY] * n,
        out_specs=[qspec, qspec] + [ANY] * n,
        scratch_shapes=_exchange_scratch(n) + [
            pltpu.VMEM((2, tq, tk), F32), pltpu.VMEM((2, tq, tk), F32),
            pltpu.VMEM((2, tq, tk), BF16), pltpu.VMEM((2, tq, tk), BF16),
            pltpu.VMEM((2, tq, LANES), F32)],
        compiler_params=_params("arbitrary", "arbitrary"),
    )(qkv, qkv, qkv, c_t2, *gather)


def _head_rowsum(a, b, name, tr=256):
    C = a.shape[1]

    def body(a_ref, b_ref, o_ref):
        r = lax.broadcasted_iota(jnp.int32, (LANES, LANES), 0) < HEAD_DIM
        cc = lax.broadcasted_iota(jnp.int32, (LANES, LANES), 1) < HEAD_DIM
        same_head = (r == cc).astype(BF16)
        for blk in range(C // LANES):
            sl = slice(blk * LANES, (blk + 1) * LANES)
            prod = a_ref[:, sl].astype(F32) * b_ref[:, sl].astype(F32)
            o_ref[:, sl] = _split_dot(prod, same_head, 2)

    row = pl.BlockSpec((tr, C), lambda i: (i, 0))
    return pl.pallas_call(
        body, name=name, grid=(S // tr,), out_shape=jax.ShapeDtypeStruct((S, C), F32),
        in_specs=[row, row], out_specs=row, compiler_params=_params("parallel"),
    )(a, b)


def _fox_bwd(qkv, do, lse, delta, c_t2, name, scatter):
    t = FOX_T
    nq = S // t

    n = len(scatter)

    def body(*refs):
        q_ref, k_ref, v_ref, do_ref, lse_ref, dl_ref, ct_ref = refs[:7]
        dq_ref, dk_ref, dv_ref, dcol_ref, drow_ref = refs[7 + n:12 + n]
        exchange = (refs[7:7 + n], refs[12 + n:12 + 2 * n], refs[12 + 2 * n:len(refs) - 5], True)
        sd_scr, pd_scr, acc_scr = refs[-5:-3], refs[-3:-1], refs[-1]
        kj = pl.program_id(1)

        @pl.when(jnp.logical_and(pl.program_id(0) == 0, kj == 0))
        def _():
            for cp in _exchange_copies(*exchange):
                cp.start()

        @pl.when(kj == 0)
        def _():
            dq_ref[...] = jnp.zeros_like(dq_ref)
            drow_ref[...] = jnp.zeros_like(drow_ref)

        lane, masks = _lane_masks()
        k = k_ref[...]
        v = v_ref[...]
        k_aug = [jnp.where(masks[e], k * SCALE, jnp.ones_like(k)) for e in range(2)]
        cs = [ct_ref[e:e + 1, :] for e in range(2)]

        def rows_of(i):
            r0 = pl.multiple_of(i * t, t)
            return pl.ds(r0, t), q_ref[pl.ds(r0, t), :] * SCALE, do_ref[pl.ds(r0, t), :]

        def scores_into(i, slot):
            _, qb, dob = rows_of(i)
            for e in range(2):
                qe = jnp.where(masks[e], qb, jnp.zeros_like(qb))
                doe = jnp.where(masks[e], dob, jnp.zeros_like(dob))
                sd_scr[slot][2 * e] = _dot(qe, k, NT_DIMS) - cs[e]
                sd_scr[slot][2 * e + 1] = _dot(doe, v, NT_DIMS)

        def pointwise(i, slot, masked):
            rows, _, _ = rows_of(i)
            for e in range(2):
                lo = e * HEAD_DIM
                s = sd_scr[slot][2 * e]
                if masked:
                    r = lax.broadcasted_iota(jnp.int32, (t, t), 0)
                    c = lax.broadcasted_iota(jnp.int32, (t, t), 1)
                    s = jnp.where(c <= r, s, NEG)
                p = jnp.exp(s - lse_ref[rows, lo:lo + 1])
                pd_scr[slot][2 * e] = p.astype(BF16)
                pd_scr[slot][2 * e + 1] = (p * (sd_scr[slot][2 * e + 1] - dl_ref[rows, lo:lo + 1])).astype(BF16)

        def accumulate(i, slot):
            rows, qb, dob = rows_of(i)
            dq_parts = []
            for e in range(2):
                p, ds = pd_scr[slot][2 * e], pd_scr[slot][2 * e + 1]
                q_aug = jnp.where(masks[e], qb, jnp.ones_like(qb))
                doe = jnp.where(masks[e], dob, jnp.zeros_like(dob))
                acc_scr[2] += _dot(p, doe, TN_DIMS)
                acc_scr[e] += _dot(ds, q_aug, TN_DIMS)
                dq_parts.append(_dot(ds, k_aug[e]))
            dq_ref[rows, :] += jnp.where(masks[0], dq_parts[0], dq_parts[1])
            drow_ref[rows, :] += jnp.where(masks[0], dq_parts[1], dq_parts[0])

        def stage(i, cur, nxt):
            scores_into(jnp.minimum(i + 1, nq - 1), nxt)
            accumulate(i - 1, nxt)
            pointwise(i, cur, False)

        acc_scr[...] = jnp.zeros_like(acc_scr)
        scores_into(kj, 0)
        pointwise(kj, 0, True)
        scores_into(jnp.minimum(kj + 1, nq - 1), 1)
        rest = nq - 1 - kj

        def two_stages(jj, carry):
            stage(kj + 1 + 2 * jj, 1, 0)
            stage(kj + 2 + 2 * jj, 0, 1)
            return carry

        lax.fori_loop(0, jnp.right_shift(rest, 1), two_stages, 0)
        odd = jnp.bitwise_and(rest, 1) == 1

        @pl.when(odd)
        def _():
            stage(nq - 1, 1, 0)
            accumulate(nq - 1, 1)

        @pl.when(jnp.logical_not(odd))
        def _():
            accumulate(nq - 1, 0)

        dk0, dk1, dv = acc_scr[0], acc_scr[1], acc_scr[2]
        dk_ref[...] = jnp.where(masks[0], dk0, dk1).astype(BF16)
        dcol_ref[...] = jnp.where(masks[0], dk1, dk0)
        dv_ref[...] = dv.astype(BF16)

        @pl.when(jnp.logical_and(pl.program_id(0) == N_PAIRS - 1, kj == nq - 1))
        def _():
            for cp in _exchange_copies(*exchange):
                cp.wait()

    full = lambda off: pl.BlockSpec((S, LANES), lambda h, j, off=off: (0, off + h))
    kv = lambda off: pl.BlockSpec((t, LANES), lambda h, j, off=off: (j, off + h))
    return pl.pallas_call(
        body, name=name, grid=(N_PAIRS, nq),
        out_shape=[jax.ShapeDtypeStruct((S, D), F32), jax.ShapeDtypeStruct((S, D), BF16),
                   jax.ShapeDtypeStruct((S, D), BF16), jax.ShapeDtypeStruct((S, D), F32),
                   jax.ShapeDtypeStruct((S, D), F32)] + _exchange_shapes(scatter, True),
        in_specs=[full(0), kv(N_PAIRS), kv(2 * N_PAIRS), full(0), full(0), full(0),
                  pl.BlockSpec((None, 2, t), lambda h, j: (h, 0, j))] + [ANY] * n,
        out_specs=[full(0), kv(0), kv(0), kv(0), full(0)] + [ANY] * n,
        scratch_shapes=_exchange_scratch(n) + [
            pltpu.VMEM((4, t, t), F32), pltpu.VMEM((4, t, t), F32),
            pltpu.VMEM((4, t, t), BF16), pltpu.VMEM((4, t, t), BF16),
            pltpu.VMEM((3, t, LANES), F32)],
        compiler_params=_params("arbitrary", "arbitrary"),
    )(qkv, qkv, qkv, do, lse, delta, c_t2, *scatter)


B_PAIRS = 4
B_NB = S // B_W


def _group_consts(g):
    nbs = jnp.where(g == 0, B_NB // B_DILS[0], jnp.where(g == 1, B_NB // B_DILS[1], B_NB // B_DILS[2]))
    dil = jnp.where(g == 0, B_DILS[0], jnp.where(g == 1, B_DILS[1], B_DILS[2]))
    return nbs, dil


def _band(dil):
    qi = lax.broadcasted_iota(jnp.int32, (B_W, B_W), 0)
    kj = lax.broadcasted_iota(jnp.int32, (B_W, B_W), 1)
    dist_c = qi - kj
    dist_p = qi + B_W - kj
    return (dist_c * dil).astype(F32), dist_c >= 0, (dist_p * dil).astype(F32), dist_p <= B_W


def _dil_fwd(qp, kp, vp, slopes, name):
    def body(sl_ref, q_ref, kp_ref, kc_ref, vp_ref, vc_ref, o_ref, lse_ref):
        g, n = pl.program_id(0), pl.program_id(1)
        nbs, dil = _group_consts(g)
        has_prev = (n % nbs) != 0
        lane, masks = _lane_masks()
        bias_c, ok_c, bias_p, ok_p = _band(dil)
        ok_p = jnp.logical_and(ok_p, has_prev)
        heads = [(hp, e) for hp in range(B_PAIRS) for e in range(2)]
        col = lambda ref, hp: ref[:, hp * LANES:(hp + 1) * LANES]
        logits = []
        for hp, e in heads:
            q = col(q_ref, hp) * SCALE
            qe = jnp.where(masks[e], q, jnp.zeros_like(q))
            logits.append((_dot(qe, col(kc_ref, hp), NT_DIMS), _dot(qe, col(kp_ref, hp), NT_DIMS)))
        probs = []
        for (hp, e), (sc, sp) in zip(heads, logits):
            slope = sl_ref[g * 8 + 2 * hp + e]
            sc = jnp.where(ok_c, sc - slope * bias_c, NEG)
            sp = jnp.where(ok_p, sp - slope * bias_p, NEG)
            m = jnp.maximum(jnp.max(sc, axis=1, keepdims=True), jnp.max(sp, axis=1, keepdims=True))
            probs.append((jnp.exp(sc - m).astype(BF16), jnp.exp(sp - m).astype(BF16), m))
        outs, lses = [], []
        for (hp, e), (pc, pp, m) in zip(heads, probs):
            vc, vpv = col(vc_ref, hp), col(vp_ref, hp)
            acc = (_dot(pc, jnp.where(masks[e], vc, jnp.ones_like(vc)))
                   + _dot(pp, jnp.where(masks[e], vpv, jnp.ones_like(vpv))))
            l = acc[:, HEAD_DIM:HEAD_DIM + 1] if e == 0 else acc[:, 0:1]
            outs.append(acc / l)
            lses.append(m + jnp.log(l))
        o_ref[...] = jnp.concatenate(
            [jnp.where(masks[0], outs[2 * hp], outs[2 * hp + 1]) for hp in range(B_PAIRS)], axis=1)
        lse_ref[...] = jnp.concatenate(
            [jnp.where(masks[0], lses[2 * hp], lses[2 * hp + 1]) for hp in range(B_PAIRS)], axis=1)

    cur = pl.BlockSpec((None, B_W, B_OUT), lambda g, n, sl: (g, n, 0))
    prev = pl.BlockSpec((None, B_W, B_OUT), lambda g, n, sl: (g, jnp.maximum(n - 1, 0), 0))
    return pl.pallas_call(
        body, name=name,
        grid_spec=pltpu.PrefetchScalarGridSpec(
            num_scalar_prefetch=1, grid=(3, B_NB),
            in_specs=[cur, prev, cur, prev, cur], out_specs=[cur, cur]),
        out_shape=[jax.ShapeDtypeStruct((3, S, B_OUT), F32), jax.ShapeDtypeStruct((3, S, B_OUT), F32)],
        compiler_params=_params("parallel", "parallel"),
    )(slopes, qp, kp, kp, vp, vp)


def _dil_merge(og, lseg, name, tr=256):
    def body(o_ref, l_ref, out_ref, lse_ref):
        l0, l1, l2 = l_ref[0], l_ref[1], l_ref[2]
        m = jnp.maximum(jnp.maximum(l0, l1), l2)
        w0, w1, w2 = jnp.exp(l0 - m), jnp.exp(l1 - m), jnp.exp(l2 - m)
        den = w0 + w1 + w2
        out_ref[...] = ((w0 * o_ref[0] + w1 * o_ref[1] + w2 * o_ref[2]) / den).astype(BF16)
        lse_ref[...] = m + jnp.log(den)

    blk3 = pl.BlockSpec((3, tr, B_OUT), lambda i: (0, i, 0))
    blk = pl.BlockSpec((tr, B_OUT), lambda i: (i, 0))
    return pl.pallas_call(
        body, name=name, grid=(S // tr,),
        out_shape=[jax.ShapeDtypeStruct((S, B_OUT), BF16), jax.ShapeDtypeStruct((S, B_OUT), F32)],
        in_specs=[blk3, blk3], out_specs=[blk, blk], compiler_params=_params("parallel"),
    )(og, lseg)


def _dil_bwd(qp, kp, vp, dop, lsep, dlp, slopes, name, scatter):
    n_ex = len(scatter)

    def body(sl_ref, *refs):
        (qc_ref, qn_ref, kp_ref, kc_ref, vp_ref, vc_ref, doc_ref, don_ref,
         lc_ref, ln_ref, dc_ref, dn_ref) = refs[:12]
        dq_ref, dk_ref, dv_ref = refs[12 + n_ex:15 + n_ex]
        exchange = (refs[12:12 + n_ex], refs[15 + n_ex:15 + 2 * n_ex], refs[15 + 2 * n_ex:], True)
        g, n = pl.program_id(0), pl.program_id(1)

        @pl.when(jnp.logical_and(g == 0, n == 0))
        def _():
            for cp in _exchange_copies(*exchange):
                cp.start()

        nbs, dil = _group_consts(g)
        has_prev = (n % nbs) != 0
        has_next = jnp.logical_and(n + 1 < B_NB, ((n + 1) % nbs) != 0)
        lane, masks = _lane_masks()
        bias_c, ok_c, bias_p, ok_p = _band(dil)
        ok_pp = jnp.logical_and(ok_p, has_prev)
        ok_np = jnp.logical_and(ok_p, has_next)
        heads = [(hp, e) for hp in range(B_PAIRS) for e in range(2)]
        col = lambda ref, hp: ref[:, hp * LANES:(hp + 1) * LANES]
        mask = lambda t, e: jnp.where(masks[e], t, jnp.zeros_like(t))
        raw = []
        for hp, e in heads:
            qce, qne = mask(col(qc_ref, hp) * SCALE, e), mask(col(qn_ref, hp) * SCALE, e)
            doce, done = mask(col(doc_ref, hp), e), mask(col(don_ref, hp), e)
            kc, kpv, vc, vpv = col(kc_ref, hp), col(kp_ref, hp), col(vc_ref, hp), col(vp_ref, hp)
            raw.append(((_dot(qce, kc, NT_DIMS), _dot(doce, vc, NT_DIMS)),
                        (_dot(qce, kpv, NT_DIMS), _dot(doce, vpv, NT_DIMS)),
                        (_dot(qne, kc, NT_DIMS), _dot(done, vc, NT_DIMS))))
        pds = []
        for (hp, e), tiles in zip(heads, raw):
            lo = hp * LANES + e * HEAD_DIM
            slope = sl_ref[g * 8 + 2 * hp + e]
            lse_c, dl_c = lc_ref[:, lo:lo + 1], dc_ref[:, lo:lo + 1]
            lse_n, dl_n = ln_ref[:, lo:lo + 1], dn_ref[:, lo:lo + 1]
            out = []
            for (s, dp), ok, bias, lse, dl in ((tiles[0], ok_c, bias_c, lse_c, dl_c),
                                               (tiles[1], ok_pp, bias_p, lse_c, dl_c),
                                               (tiles[2], ok_np, bias_p, lse_n, dl_n)):
                p = jnp.exp(jnp.where(ok, s - slope * bias, NEG) - lse)
                out.append((p.astype(BF16), (p * (dp - dl)).astype(BF16)))
            pds.append(out)
        dq_all, dk_all, dv_all = [], [], []
        for hp in range(B_PAIRS):
            dq = jnp.zeros((B_W, LANES), F32)
            dk = jnp.zeros((B_W, LANES), F32)
            dv = jnp.zeros((B_W, LANES), F32)
            for e in range(2):
                (p_c, ds_c), (_, ds_p), (p_n, ds_n) = pds[2 * hp + e]
                qce, qne = mask(col(qc_ref, hp) * SCALE, e), mask(col(qn_ref, hp) * SCALE, e)
                doce, done = mask(col(doc_ref, hp), e), mask(col(don_ref, hp), e)
                dq = dq + _dot(ds_c, mask(col(kc_ref, hp) * SCALE, e)) + _dot(ds_p, mask(col(kp_ref, hp) * SCALE, e))
                dk = dk + _dot(ds_c, qce, TN_DIMS) + _dot(ds_n, qne, TN_DIMS)
                dv = dv + _dot(p_c, doce, TN_DIMS) + _dot(p_n, done, TN_DIMS)
            dq_all.append(dq)
            dk_all.append(dk)
            dv_all.append(dv)
        dq_ref[...] = jnp.concatenate(dq_all, axis=1).astype(BF16)
        dk_ref[...] = jnp.concatenate(dk_all, axis=1).astype(BF16)
        dv_ref[...] = jnp.concatenate(dv_all, axis=1).astype(BF16)

        @pl.when(jnp.logical_and(g == 2, n == B_NB - 1))
        def _():
            for cp in _exchange_copies(*exchange):
                cp.wait()

    cur = pl.BlockSpec((None, B_W, B_OUT), lambda g, n, sl: (g, n, 0))
    prev = pl.BlockSpec((None, B_W, B_OUT), lambda g, n, sl: (g, jnp.maximum(n - 1, 0), 0))
    nxt = pl.BlockSpec((None, B_W, B_OUT), lambda g, n, sl: (g, jnp.minimum(n + 1, B_NB - 1), 0))
    return pl.pallas_call(
        body, name=name,
        grid_spec=pltpu.PrefetchScalarGridSpec(
            num_scalar_prefetch=1, grid=(3, B_NB),
            in_specs=[cur, nxt, prev, cur, prev, cur, cur, nxt, cur, nxt, cur, nxt] + [ANY] * n_ex,
            out_specs=[cur, cur, cur] + [ANY] * n_ex,
            scratch_shapes=_exchange_scratch(n_ex)),
        out_shape=[jax.ShapeDtypeStruct((3, S, B_OUT), BF16)] * 3 + _exchange_shapes(scatter, True),
        compiler_params=_params("arbitrary", "arbitrary"),
    )(slopes, qp, qp, kp, kp, vp, vp, dop, dop, lsep, lsep, dlp, dlp, *scatter)


def _rows_block(shape, max_bytes=2 * 1024 * 1024):
    rows, cols = shape
    padded_cols = -(-cols // LANES) * LANES
    for tr in (1024, 512, 256, 128, 64, 32, 16):
        if rows % tr == 0 and tr * padded_cols * 4 <= max_bytes:
            return tr
    return rows


def _adam_update(w, m, v, g):
    m_new = ADAM_B1 * m + (1.0 - ADAM_B1) * g
    v_new = ADAM_B2 * v + (1.0 - ADAM_B2) * (g * g)
    m_hat = m_new / (1.0 - ADAM_B1 ** ADAM_STEP)
    v_hat = v_new / (1.0 - ADAM_B2 ** ADAM_STEP)
    delta = -ADAM_LR * (m_hat / (jnp.sqrt(v_hat) + ADAM_EPS) + ADAM_WD * w)
    return delta, m_new, v_new


def _adamw_sharded(w, m, v, parts, name):
    R, C = w.shape
    tr = _rows_block((R, C), max_bytes=1024 * 1024)

    def body(w_ref, m_ref, v_ref, p_ref, g_ref, d_ref, mo_ref, vo_ref):
        g = p_ref[0].astype(F32)
        for dev in range(1, N_DEV):
            g = g + p_ref[dev].astype(F32)
        g_ref[...] = g
        d_ref[...], mo_ref[...], vo_ref[...] = _adam_update(w_ref[...], m_ref[...], v_ref[...], g)

    blk = pl.BlockSpec((tr, C), lambda i: (i, 0))
    out = jax.ShapeDtypeStruct((R, C), F32)
    return pl.pallas_call(
        body, name=name, grid=(R // tr,),
        in_specs=[blk, blk, blk, pl.BlockSpec((N_DEV, tr, C), lambda i: (0, i, 0))],
        out_specs=[blk, blk, blk, blk], out_shape=[out, out, out, out],
        compiler_params=_params("parallel"),
    )(w, m, v, parts)


def _adamw_replicated(w, m, v, parts, name):
    def body(w_ref, m_ref, v_ref, p_ref, g_ref, d_ref, mo_ref, vo_ref):
        g = p_ref[0]
        for dev in range(1, N_DEV):
            g = g + p_ref[dev]
        g_ref[...] = g
        d_ref[...], mo_ref[...], vo_ref[...] = _adam_update(w_ref[...], m_ref[...], v_ref[...], g)

    out = jax.ShapeDtypeStruct(w.shape, F32)
    return pl.pallas_call(body, name=name, out_shape=[out, out, out, out], compiler_params=_params())(w, m, v, parts)


def _cols_from_slots(g):
    return g.transpose(1, 0, 2).reshape(g.shape[1], N_DEV * g.shape[2])


def _cols_to_slots(w):
    k, n = w.shape
    return w.reshape(k, N_DEV, n // N_DEV).transpose(1, 0, 2)


def _permute(t, dil):
    c = t.shape[1]
    return t.reshape(S // dil, dil, c).transpose(1, 0, 2).reshape(S, c)


def _unpermute(t, dil):
    c = t.shape[1]
    return t.reshape(dil, S // dil, c).transpose(1, 0, 2).reshape(S, c)


def _group_permute(t):
    return jnp.stack([_permute(t[:, g * B_OUT:(g + 1) * B_OUT], B_DILS[g]) for g in range(3)])


def _same_permute(t):
    return jnp.stack([_permute(t, d) for d in B_DILS])


def _group_unpermute(t):
    return jnp.stack([_unpermute(t[g], B_DILS[g]) for g in range(3)])


SMALL_ROWS = 144


def _pack_small(a_b_f, kv_g, mix_g, ffn_g, conv_b, fin_g):
    flat = jnp.concatenate([a_b_f.reshape(-1), kv_g.reshape(-1), mix_g.reshape(-1), ffn_g.reshape(-1),
                            conv_b.reshape(-1), fin_g.reshape(-1)])
    return jnp.pad(flat, (0, SMALL_ROWS * LANES - flat.shape[0])).reshape(SMALL_ROWS, LANES)


def _unpack_small(p):
    flat = p.reshape(-1)
    out, off = [], 0
    for shape in ((1, A_HEADS), (D,), (2, D), (2, D), (2, 2 * D_FF), (D,)):
        size = math.prod(shape)
        out.append(flat[off:off + size].reshape(shape))
        off += size
    return out


def _unpack_late(g):
    w_up = g[4].reshape(N_DEV, 2, D, -1).transpose(1, 2, 0, 3).reshape(2, D, 2 * D_FF)
    w_down = g[5].reshape(N_DEV, 2, -1, D).transpose(1, 0, 2, 3).reshape(2, D_FF, D)
    conv_w = g[6].reshape(N_DEV, 2, 3, -1).transpose(1, 2, 0, 3).reshape(2, 3, 2 * D_FF)
    return (g[0].reshape(D, D), _cols_from_slots(g[1]), _cols_from_slots(g[2]), _cols_from_slots(g[3]),
            w_up, w_down, conv_w)


def _ffn_slots(dw_up, dw_down, dconv_w):
    return [_cols_to_slots(dw_up), dw_down.reshape(N_DEV, -1, D), _cols_to_slots(dconv_w)]


def _local_step(x0, target, w_in_pad, late_shards,
                a_b_f, kv_norm_g, mix_norm_g, ffn_norm_g, ffn_conv_b, final_norm_g):
    w_qkv, w_f = w_in_pad[:, :A_QKV], w_in_pad[:, A_QKV:]
    conv_b = ffn_conv_b.reshape(2, 1, 2 * D_FF)
    slopes = jnp.exp2(-8.0 * jnp.arange(1, 25, dtype=F32) / 24)

    def gain(g):
        return g.reshape(1, D)

    (h1,) = _rmsnorm_fwd(x0, [gain(mix_norm_g[0])], "norm_mix0")
    qkv = _matmul(h1, w_qkv, mode="nn", out_dtype=BF16, name="proj_qkv", tm=512, tn=A_QKV)
    z = _matmul(h1, w_f, mode="nn", out_dtype=F32, name="proj_gate", tm=S, tn=LANES)
    z_t = z[:, :A_HEADS].T
    b_f = a_b_f.reshape(A_HEADS, 1)
    c_t = _fox_prep_fwd(z_t, b_f, "fox_prep")
    c_t2 = c_t.reshape(N_PAIRS, 2, S)
    o_a, lse_a, *late = _fox_fwd(qkv, c_t2, "fox_fwd", late_shards)
    w_out, w_q, w_bo, w_kvf, w_up, w_down, conv_w = _unpack_late(late)
    x1 = _matmul(o_a, w_out, mode="nn", out_dtype=F32, name="a_out", tm=512, tn=D, res=x0)

    def ffn_fwd(xin, layer):
        (h,) = _rmsnorm_fwd(xin, [gain(ffn_norm_g[layer])], f"norm_ffn{layer}")
        u = _matmul(h, w_up[layer], mode="nn", out_dtype=BF16, name=f"ffn_up{layer}", tm=512, tn=2 * D_FF)
        act = _convgate_fwd(u, conv_w[layer], conv_b[layer], f"convgate{layer}")
        xout = _matmul(act, w_down[layer], mode="nn", out_dtype=F32, name=f"ffn_down{layer}", tm=512, tn=D, res=xin)
        return h, u, act, xout

    h2, u0, act0, x2 = ffn_fwd(x1, 0)
    hk, h3 = _rmsnorm_fwd(x2, [gain(kv_norm_g), gain(mix_norm_g[1])], "norm_kv_mix1")
    kv = _matmul(hk, w_kvf, mode="nn", out_dtype=BF16, name="proj_kv", tm=512, tn=B_KV)
    qb = _matmul(h3, w_q, mode="nn", out_dtype=BF16, name="proj_qb", tm=512, tn=B_Q)
    qp, kp, vp = _group_permute(qb), _group_permute(kv[:, :B_Q]), _group_permute(kv[:, B_Q:])
    og_p, lseg_p = _dil_fwd(qp, kp, vp, slopes, "dil_fwd")
    o_b, lse_b = _dil_merge(_group_unpermute(og_p), _group_unpermute(lseg_p), "dil_merge")
    x3 = _matmul(o_b, w_bo, mode="nn", out_dtype=F32, name="b_out", tm=512, tn=D, res=x2)
    h4, u1, act1, x4 = ffn_fwd(x3, 1)
    loss_blk, dx4, dx4b, dg_final = _final_loss(x4, target, gain(final_norm_g), "final_loss")

    def ffn_bwd(dx, dxb, xin, h, u, act, layer):
        dact = _matmul(dxb, w_down[layer], mode="nt", out_dtype=BF16, name=f"d_act{layer}", tm=512, tn=D_FF)
        dw_down = _matmul_tn(act, dxb, out_dtype=BF16, name=f"dw_down{layer}")
        du_a, du_g, dwa, dwg, dba, dbg = _convgate_bwd(u, conv_w[layer], conv_b[layer], dact, f"convgate_bwd{layer}")
        dw_up = jnp.concatenate(
            [_matmul_tn(h, du_a, out_dtype=BF16, name=f"dw_up_a{layer}"),
             _matmul_tn(h, du_g, out_dtype=BF16, name=f"dw_up_g{layer}")], axis=1)
        dh = _matmul(du_a, w_up[layer][:, :D_FF], mode="nt", out_dtype=F32, name=f"dh_ffn_a{layer}", tm=512, tn=D)
        dh = _matmul(du_g, w_up[layer][:, D_FF:], mode="nt", out_dtype=F32, name=f"dh_ffn_g{layer}", tm=512, tn=D,
                     res=dh)
        dxin, dxinb, dgain = _rmsnorm_bwd(xin, dh, gain(ffn_norm_g[layer]), dx, f"norm_ffn_bwd{layer}")
        dconv_w = jnp.concatenate([dwa, dwg], axis=1)
        dconv_b = jnp.concatenate([dba, dbg], axis=1)
        return dxin, dxinb, dgain, dw_up, dw_down, dconv_w, dconv_b

    dx3, dx3b, dg_ffn1, dw_up1, dw_down1, dconv_w1, dconv_b1 = ffn_bwd(dx4, dx4b, x3, h4, u1, act1, 1)

    do_b = _matmul(dx3b, w_bo, mode="nt", out_dtype=BF16, name="d_ob", tm=1024, tn=B_OUT)
    dw_bo = _matmul_tn(o_b, dx3b, out_dtype=BF16, name="dw_bo")
    dl_b = _head_rowsum(do_b, o_b, "delta_b")
    slots_up1, slots_down1, slots_conv1 = _ffn_slots(dw_up1, dw_down1, dconv_w1)
    dqp, dkp, dvp, land_down1, land_conv1 = _dil_bwd(
        qp, kp, vp, _same_permute(do_b), _same_permute(lse_b), _same_permute(dl_b), slopes, "dil_bwd",
        [slots_down1, slots_conv1])

    def natural(tp):
        return jnp.concatenate([_unpermute(tp[g], B_DILS[g]) for g in range(3)], axis=1)

    dqb = natural(dqp)
    dkv = jnp.concatenate([natural(dkp), natural(dvp)], axis=1)
    dw_q = _matmul_tn(h3, dqb, out_dtype=BF16, name="dw_q")
    dw_kv = _matmul_tn(hk, dkv, out_dtype=BF16, name="dw_kv")
    dh3 = _matmul(dqb, w_q, mode="nt", out_dtype=F32, name="dh_mix1", tm=512, tn=D)
    dhk = _matmul(dkv, w_kvf, mode="nt", out_dtype=F32, name="dh_kv", tm=512, tn=D)
    dx2, _, dg_mix1 = _rmsnorm_bwd(x2, dh3, gain(mix_norm_g[1]), dx3, "norm_mix1_bwd")
    dx2, dx2b, dg_kv = _rmsnorm_bwd(x2, dhk, gain(kv_norm_g), dx2, "norm_kv_bwd")

    dx1, dx1b, dg_ffn0, dw_up0, dw_down0, dconv_w0, dconv_b0 = ffn_bwd(dx2, dx2b, x1, h2, u0, act0, 0)

    do_a = _matmul(dx1b, w_out, mode="nt", out_dtype=BF16, name="d_oa", tm=512, tn=D)
    dw_out = _matmul_tn(o_a, dx1b, out_dtype=BF16, name="dw_out")
    dl_a = _head_rowsum(do_a, o_a, "delta_a")
    dq_a, dk_a, dv_a, dcol, drow, *land = _fox_bwd(
        qkv, do_a, lse_a, dl_a, c_t2, "fox_bwd",
        [dw_out.reshape(N_DEV, D // N_DEV, D), _cols_to_slots(dw_q), _cols_to_slots(dw_bo), _cols_to_slots(dw_kv)]
        + _ffn_slots(dw_up0, dw_down0, dconv_w0) + [slots_up1])
    land_out, land_q, land_bo, land_kv, land_up0, land_down0, land_conv0, land_up1 = land

    def head_sums(t):
        return t.reshape(S, N_PAIRS, 2, HEAD_DIM)[:, :, ::-1, 0].reshape(S, A_HEADS).T

    dz_t, db_f = _fox_prep_bwd(head_sums(drow), head_sums(dcol), z_t, b_f, "fox_prep_bwd")
    dz = jnp.pad(dz_t.T, ((0, 0), (0, LANES - A_HEADS))).astype(BF16)
    dproj = jnp.concatenate([dq_a.astype(BF16), dk_a, dv_a, dz], axis=1)
    dw_in = _matmul_tn(h1, dproj, out_dtype=BF16, name="dw_in")
    dh1, land_in = _matmul(dproj, w_in_pad, mode="nt", out_dtype=F32, name="dh_mix0", tm=512, tn=D,
                           scatter=[_cols_to_slots(dw_in[:, :A_QKV + A_HEADS])])
    grad_x, _, dg_mix0 = _rmsnorm_bwd(x0, dh1, gain(mix_norm_g[0]), dx1, "norm_mix0_bwd")

    dg_mix = jnp.concatenate([dg_mix0, dg_mix1], axis=0)
    dg_ffn = jnp.concatenate([dg_ffn0, dg_ffn1], axis=0)
    dconv_b = jnp.concatenate([dconv_b0, dconv_b1], axis=0)
    small_part = _pack_small(db_f, dg_kv, dg_mix, dg_ffn, dconv_b, dg_final)
    _, (small_parts,) = _final_exchange([], [small_part], "gather_small_grads")
    landed = [land_in, land_out, land_q, land_bo, land_kv, land_up0, land_up1, land_down0, land_down1,
              land_conv0, land_conv1]
    return loss_blk, grad_x, landed, small_parts


def kernel(x, a_w_in, a_b_f, a_w_out, b_w_q, b_w_out, kv_norm_g, w_kv, mix_norm_g, ffn_norm_g, ffn_w_up, ffn_conv_w, ffn_conv_b, ffn_w_down, final_norm_g, loss_target, m_a_w_in, m_a_b_f, m_a_w_out, m_b_w_q, m_b_w_out, m_kv_norm_g, m_w_kv, m_mix_norm_g, m_ffn_norm_g, m_ffn_w_up, m_ffn_conv_w, m_ffn_conv_b, m_ffn_w_down, m_final_norm_g, v_a_w_in, v_a_b_f, v_a_w_out, v_b_w_q, v_b_w_out, v_kv_norm_g, v_w_kv, v_mix_norm_g, v_ffn_norm_g, v_ffn_w_up, v_ffn_conv_w, v_ffn_conv_b, v_ffn_w_down, v_final_norm_g):
    def shards(a_w_in, a_w_out, b_w_q, b_w_out, w_kv, ffn_w_up, ffn_w_down, ffn_conv_w):
        return [a_w_in[0], a_w_out[0], b_w_q[0], b_w_out[0], w_kv, ffn_w_up[0], ffn_w_up[1],
                ffn_w_down[0], ffn_w_down[1], ffn_conv_w[0], ffn_conv_w[1]]

    w_loc = shards(a_w_in, a_w_out, b_w_q, b_w_out, w_kv, ffn_w_up, ffn_w_down, ffn_conv_w)
    m_loc = shards(m_a_w_in, m_a_w_out, m_b_w_q, m_b_w_out, m_w_kv, m_ffn_w_up, m_ffn_w_down, m_ffn_conv_w)
    v_loc = shards(v_a_w_in, v_a_w_out, v_b_w_q, v_b_w_out, v_w_kv, v_ffn_w_up, v_ffn_w_down, v_ffn_conv_w)

    (g_in,) = _all_gather([a_w_in[0].astype(BF16)], "gather_a_w_in")
    w_in = _cols_from_slots(g_in)
    w_in_pad = jnp.pad(w_in, ((0, 0), (0, A_PROJ_PAD - w_in.shape[1])))
    late_shards = [a_w_out[0].astype(BF16), b_w_q[0].astype(BF16), b_w_out[0].astype(BF16), w_kv.astype(BF16),
                   ffn_w_up.reshape(2 * D, -1).astype(BF16), ffn_w_down.reshape(-1, D).astype(BF16),
                   ffn_conv_w.reshape(6, -1)]

    loss_blk, grad_x, landed, small_parts = _local_step(
        x[0], loss_target[0], w_in_pad, late_shards,
        a_b_f, kv_norm_g, mix_norm_g, ffn_norm_g, ffn_conv_b, final_norm_g)

    big = [_adamw_sharded(w_loc[k], m_loc[k], v_loc[k], landed[k], f"adamw{k}") for k in range(11)]

    small = _adamw_replicated(
        _pack_small(a_b_f, kv_norm_g, mix_norm_g, ffn_norm_g, ffn_conv_b, final_norm_g),
        _pack_small(m_a_b_f, m_kv_norm_g, m_mix_norm_g, m_ffn_norm_g, m_ffn_conv_b, m_final_norm_g),
        _pack_small(v_a_b_f, v_kv_norm_g, v_mix_norm_g, v_ffn_norm_g, v_ffn_conv_b, v_final_norm_g),
        small_parts, "adamw_small")

    loss = lax.psum(loss_blk[0, 0], ("x", "y", "c"))

    def assemble(kind):
        b = [r[kind] for r in big]
        s_abf, s_kv, s_mix, s_ffn, s_cb, s_fin = _unpack_small(small[kind])
        return [b[0][None], s_abf, b[1][None], b[2][None], b[3][None], s_kv, b[4], s_mix, s_ffn,
                jnp.stack([b[5], b[6]]), jnp.stack([b[9], b[10]]), s_cb, jnp.stack([b[7], b[8]]), s_fin]

    return (loss, grad_x[None], *assemble(0), *assemble(1), *assemble(2), *assemble(3))
```

```python
import functools
import math

import jax
import jax.numpy as jnp
from jax import lax
from jax.experimental import pallas as pl
from jax.experimental.pallas import tpu as pltpu

F32 = jnp.float32
BF16 = jnp.bfloat16

S = 4096
D = 1024
N_DEV = 8
A_HEADS = 16
HEAD_DIM = 64
A_QKV = 3072
A_PROJ_PAD = 3200
B_Q = 1536
B_OUT = 512
B_KV = 3072
B_W = 128
B_DILS = (1, 4, 16)
D_FF = 2816
RMS_EPS = 1e-6
SCALE = HEAD_DIM ** -0.5
NEG = -1e30

ADAM_LR = 0.001
ADAM_B1 = 0.9
ADAM_B2 = 0.999
ADAM_EPS = 1e-08
ADAM_WD = 0.01
ADAM_STEP = 10

LANES = 128
VMEM_LIMIT = 56 * 1024 * 1024
MESH = pl.DeviceIdType.MESH
ANY = pl.BlockSpec(memory_space=pl.ANY)

NT_DIMS = (((1,), (1,)), ((), ()))
TN_DIMS = (((0,), (0,)), ((), ()))
NN_DIMS = (((1,), (0,)), ((), ()))


def _params(*sem):
    return pltpu.CompilerParams(dimension_semantics=sem if sem else None, vmem_limit_bytes=VMEM_LIMIT)


def _dot(a, b, dims=NN_DIMS):
    return lax.dot_general(a, b, dims, preferred_element_type=F32)


def _split_dot(x, mat, pieces):
    out = None
    rem = x
    for _ in range(pieces):
        part = rem.astype(BF16)
        rem = rem - part.astype(F32)
        d = _dot(part, mat)
        out = d if out is None else out + d
    return out


def _pick(n, prefs):
    for p in prefs:
        if n % p == 0:
            return p
    return n


def _gather_phases(ins, outs, sems):
    n = len(ins)
    if n == 0:
        return (lambda: None,) * 3
    send_sems, recv_sems, local_sems = sems
    x, y, c = lax.axis_index("x"), lax.axis_index("y"), lax.axis_index("c")
    me, sibling = (x, y, c), (x, y, 1 - c)
    chips = [(1 - x, y), (x, 1 - y), (1 - x, 1 - y)]

    def slot(a, px, py, pc):
        return outs[a].at[4 * px + 2 * py + pc]

    def copy(a, k, block, to, src=None):
        return pltpu.make_async_remote_copy(
            src_ref=slot(a, *block) if src is None else src, dst_ref=slot(a, *block),
            send_sem=send_sems.at[a, k], recv_sem=recv_sems.at[a, k],
            device_id=to, device_id_type=MESH)

    mine = [pltpu.make_async_copy(ins[a], slot(a, *me), local_sems.at[a]) for a in range(n)]
    first = []
    for a in range(n):
        first.append(copy(a, 0, me, sibling, src=ins[a]))
        first += [copy(a, 1 + j, me, (*chip, c), src=ins[a]) for j, chip in enumerate(chips)]
    passed = [copy(a, 4 + j, (*chip, c), sibling) for j, chip in enumerate(chips) for a in range(n)]

    def start():
        for cp in mine + first:
            cp.start()

    def forward():
        k = 0
        for j, chip in enumerate(chips):
            for a in range(n):
                copy(a, 1 + j, (*chip, c), me).wait_recv()
                passed[k].start()
                k += 1

    def finish():
        for a in range(n):
            copy(a, 0, sibling, me).wait_recv()
            for j, chip in enumerate(chips):
                copy(a, 4 + j, (*chip, 1 - c), me).wait_recv()
        for cp in first + passed:
            cp.wait_send()
        for cp in mine:
            cp.wait()

    return start, forward, finish


def _all_gather(arrays, name):
    n = len(arrays)

    def body(*refs):
        for phase in _gather_phases(refs[:n], refs[n:2 * n], refs[2 * n:]):
            phase()

    return pl.pallas_call(
        body, name=name,
        out_shape=[jax.ShapeDtypeStruct((N_DEV,) + a.shape, a.dtype) for a in arrays],
        in_specs=[ANY] * n, out_specs=[ANY] * n,
        scratch_shapes=[pltpu.SemaphoreType.DMA((n, 7)), pltpu.SemaphoreType.DMA((n, 7)),
                        pltpu.SemaphoreType.DMA((n,))],
    )(*arrays)


PEER_FLIPS = [(dx, dy, dc) for dx in (0, 1) for dy in (0, 1) for dc in (0, 1) if (dx, dy, dc) != (0, 0, 0)]


def _exchange_copies(ins, outs, sems, scatter):
    if not ins:
        return []
    send_sems, recv_sems, local_sems = sems
    x, y, c = lax.axis_index("x"), lax.axis_index("y"), lax.axis_index("c")
    me = 4 * x + 2 * y + c
    copies = []
    for a in range(len(ins)):
        copies.append(pltpu.make_async_copy(ins[a].at[me] if scatter else ins[a], outs[a].at[me], local_sems.at[a]))
        for k, (dx, dy, dc) in enumerate(PEER_FLIPS):
            px, py, pc = (1 - x if dx else x), (1 - y if dy else y), (1 - c if dc else c)
            copies.append(pltpu.make_async_remote_copy(
                src_ref=ins[a].at[4 * px + 2 * py + pc] if scatter else ins[a], dst_ref=outs[a].at[me],
                send_sem=send_sems.at[a, k], recv_sem=recv_sems.at[a, k],
                device_id=(px, py, pc), device_id_type=MESH))
    return copies


def _exchange_scratch(n):
    if n == 0:
        return []
    return [pltpu.SemaphoreType.DMA((n, 7)), pltpu.SemaphoreType.DMA((n, 7)), pltpu.SemaphoreType.DMA((n,))]


def _exchange_shapes(arrays, scatter):
    return [jax.ShapeDtypeStruct((N_DEV,) + (a.shape[1:] if scatter else a.shape), a.dtype) for a in arrays]


def _final_exchange(scatter, gather, name):
    ns, ng = len(scatter), len(gather)

    def body(*refs):
        ins, outs, sems = refs[:ns + ng], refs[ns + ng:2 * (ns + ng)], refs[2 * (ns + ng):]
        n_sems = len(_exchange_scratch(ns))
        copies = (_exchange_copies(ins[:ns], outs[:ns], sems[:n_sems], True)
                  + _exchange_copies(ins[ns:], outs[ns:], sems[n_sems:], False))
        for cp in copies:
            cp.start()
        for cp in copies:
            cp.wait()

    res = pl.pallas_call(
        body, name=name, out_shape=_exchange_shapes(scatter, True) + _exchange_shapes(gather, False),
        in_specs=[ANY] * (ns + ng), out_specs=[ANY] * (ns + ng),
        scratch_shapes=_exchange_scratch(ns) + _exchange_scratch(ng),
    )(*scatter, *gather)
    return res[:ns], res[ns:]


MM_ROWS = 512
MM_COLS = 1024


def _matmul(a, b, *, mode, out_dtype, name, tm, tn, res=None, scatter=()):
    if mode == "nn":
        (M, K), (K2, N) = a.shape, b.shape
    else:
        (M, K), (N, K2) = a.shape, b.shape
    assert K == K2, (a.shape, b.shape, mode)
    tm, tn = min(tm, M), min(tn, N)
    sm = min(tm, MM_ROWS)
    sn = tn if tn <= MM_COLS else _pick(tn, (512, 256, 128))
    assert M % tm == 0 and N % tn == 0 and tm % sm == 0, (M, N, K, tm, tn)
    dims = NN_DIMS if mode == "nn" else NT_DIMS
    a_spec = pl.BlockSpec((tm, K), lambda i, j: (i, 0))
    if mode == "nt":
        b_spec = pl.BlockSpec((tn, K), lambda i, j: (j, 0))
    else:
        b_spec = pl.BlockSpec((K, tn), lambda i, j: (0, j))
    o_spec = pl.BlockSpec((tm, tn), lambda i, j: (i, j))
    has_res = res is not None
    n_in, n_ex = 2 + has_res, len(scatter)
    gm, gn = M // tm, N // tn

    def body(*refs):
        a_ref, b_ref = refs[0], refs[1]
        r_ref = refs[2] if has_res else None
        o_ref = refs[n_in + n_ex]
        exchange = (refs[n_in:n_in + n_ex], refs[n_in + n_ex + 1:n_in + 2 * n_ex + 1], refs[n_in + 2 * n_ex + 1:], True)

        @pl.when(jnp.logical_and(pl.program_id(0) == 0, pl.program_id(1) == 0))
        def _():
            for cp in _exchange_copies(*exchange):
                cp.start()

        def chunk(r, carry):
            rows = pl.ds(pl.multiple_of(r * sm, sm), sm)
            av = a_ref[rows, :]
            for c0 in range(0, tn, sn):
                bv = b_ref[c0:c0 + sn, :] if mode == "nt" else b_ref[:, c0:c0 + sn]
                total = _dot(av, bv, dims)
                if has_res:
                    total = total + r_ref[rows, c0:c0 + sn]
                o_ref[rows, c0:c0 + sn] = total.astype(out_dtype)
            return carry

        lax.fori_loop(0, tm // sm, chunk, 0)

        @pl.when(jnp.logical_and(pl.program_id(0) == gm - 1, pl.program_id(1) == gn - 1))
        def _():
            for cp in _exchange_copies(*exchange):
                cp.wait()

    out = pl.pallas_call(
        body, name=name, grid=(gm, gn),
        out_shape=[jax.ShapeDtypeStruct((M, N), out_dtype)] + _exchange_shapes(scatter, True),
        in_specs=[a_spec, b_spec] + ([o_spec] if has_res else []) + [ANY] * n_ex,
        out_specs=[o_spec] + [ANY] * n_ex,
        scratch_shapes=_exchange_scratch(n_ex),
        compiler_params=_params("arbitrary", "arbitrary"),
    )(*((a, b, res) if has_res else (a, b)), *scatter)
    return out if n_ex else out[0]


def _matmul_tn(a, b, *, out_dtype, name, tk=512, sm=256):
    (K, M), (K2, N) = a.shape, b.shape
    assert K == K2 and K % tk == 0 and M % sm == 0, (a.shape, b.shape)
    nk = K // tk

    def body(a_ref, b_ref, o_ref, acc_ref):
        k = pl.program_id(0)

        @pl.when(k == 0)
        def _():
            acc_ref[...] = jnp.zeros_like(acc_ref)

        def chunk(mi, carry):
            cols = pl.ds(pl.multiple_of(mi * sm, sm), sm)
            acc_ref[cols, :] += _dot(a_ref[:, cols].T, b_ref[...])
            return carry

        lax.fori_loop(0, M // sm, chunk, 0)

        @pl.when(k == nk - 1)
        def _():
            def emit(mi, carry):
                rows = pl.ds(pl.multiple_of(mi * sm, sm), sm)
                o_ref[rows, :] = acc_ref[rows, :].astype(out_dtype)
                return carry
            lax.fori_loop(0, M // sm, emit, 0)

    return pl.pallas_call(
        body, name=name, grid=(nk,),
        out_shape=jax.ShapeDtypeStruct((M, N), out_dtype),
        in_specs=[pl.BlockSpec((tk, M), lambda k: (k, 0)), pl.BlockSpec((tk, N), lambda k: (k, 0))],
        out_specs=pl.BlockSpec((M, N), lambda k: (0, 0)),
        scratch_shapes=[pltpu.VMEM((M, N), F32)],
        compiler_params=_params("arbitrary"),
    )(a, b)


def _rmsnorm_fwd(x, gains, name, tr=256):
    n = len(gains)

    def body(*refs):
        x_ref = refs[0]
        xv = x_ref[...]
        r = lax.rsqrt(jnp.mean(xv * xv, axis=-1, keepdims=True) + RMS_EPS)
        y = xv * r
        for a in range(n):
            refs[1 + n + a][...] = (y * refs[1 + a][...]).astype(BF16)

    row = pl.BlockSpec((tr, D), lambda i: (i, 0))
    gain = pl.BlockSpec((1, D), lambda i: (0, 0))
    return pl.pallas_call(
        body, name=name, grid=(S // tr,),
        out_shape=[jax.ShapeDtypeStruct((S, D), BF16)] * n,
        in_specs=[row] + [gain] * n, out_specs=[row] * n,
        compiler_params=_params("parallel"),
    )(x, *gains)


def _rmsnorm_bwd(x, dy, g, dres, name, tr=256):
    def body(x_ref, dy_ref, g_ref, dres_ref, dx_ref, dxb_ref, dg_ref):
        xv = x_ref[...]
        dyv = dy_ref[...]
        r = lax.rsqrt(jnp.mean(xv * xv, axis=-1, keepdims=True) + RMS_EPS)
        xhat = xv * r
        dxhat = dyv * g_ref[...]
        mean_term = jnp.mean(dxhat * xhat, axis=-1, keepdims=True)
        dx = r * (dxhat - xhat * mean_term) + dres_ref[...]
        dx_ref[...] = dx
        dxb_ref[...] = dx.astype(BF16)
        part = jnp.sum(dyv * xhat, axis=0, keepdims=True)

        @pl.when(pl.program_id(0) == 0)
        def _():
            dg_ref[...] = part

        @pl.when(pl.program_id(0) > 0)
        def _():
            dg_ref[...] += part

    row = pl.BlockSpec((tr, D), lambda i: (i, 0))
    gain = pl.BlockSpec((1, D), lambda i: (0, 0))
    return pl.pallas_call(
        body, name=name, grid=(S // tr,),
        out_shape=[jax.ShapeDtypeStruct((S, D), F32), jax.ShapeDtypeStruct((S, D), BF16),
                   jax.ShapeDtypeStruct((1, D), F32)],
        in_specs=[row, row, gain, row], out_specs=[row, row, gain],
        compiler_params=_params("arbitrary"),
    )(x, dy, g, dres)


def _final_loss(x, target, g, name, tr=256):
    def body(x_ref, t_ref, g_ref, loss_ref, dx_ref, dxb_ref, dg_ref):
        xv = x_ref[...]
        gv = g_ref[...]
        r = lax.rsqrt(jnp.mean(xv * xv, axis=-1, keepdims=True) + RMS_EPS)
        xhat = xv * r
        err = xhat * gv - t_ref[...]
        row_loss = jnp.mean(err * err, axis=-1, keepdims=True)
        lpart = 0.5 * jnp.sum(row_loss, axis=0, keepdims=True)
        dyv = err / D
        dxhat = dyv * gv
        mean_term = jnp.mean(dxhat * xhat, axis=-1, keepdims=True)
        dx = r * (dxhat - xhat * mean_term)
        dx_ref[...] = dx
        dxb_ref[...] = dx.astype(BF16)
        gpart = jnp.sum(dyv * xhat, axis=0, keepdims=True)

        @pl.when(pl.program_id(0) == 0)
        def _():
            dg_ref[...] = gpart
            loss_ref[...] = jnp.broadcast_to(lpart, loss_ref.shape)

        @pl.when(pl.program_id(0) > 0)
        def _():
            dg_ref[...] += gpart
            loss_ref[...] += jnp.broadcast_to(lpart, loss_ref.shape)

    row = pl.BlockSpec((tr, D), lambda i: (i, 0))
    gain = pl.BlockSpec((1, D), lambda i: (0, 0))
    lspec = pl.BlockSpec((8, LANES), lambda i: (0, 0))
    return pl.pallas_call(
        body, name=name, grid=(S // tr,),
        out_shape=[jax.ShapeDtypeStruct((8, LANES), F32), jax.ShapeDtypeStruct((S, D), F32),
                   jax.ShapeDtypeStruct((S, D), BF16), jax.ShapeDtypeStruct((1, D), F32)],
        in_specs=[row, row, gain], out_specs=[lspec, row, row, gain],
        compiler_params=_params("arbitrary"),
    )(x, target, g)


CONV_TR = 128
CONV_TC = D_FF
CONV_NJ = D_FF // CONV_TC
HALO = 16


def _causal_taps(cur_ref, prev_ref, first):
    xv = cur_ref[...].astype(F32)
    pv = prev_ref[...].astype(F32)
    p1 = jnp.where(first, 0.0, pv[HALO - 1:HALO, :])
    p2 = jnp.where(first, 0.0, pv[HALO - 2:HALO - 1, :])
    r1, r2 = pltpu.roll(xv, 1, 0), pltpu.roll(xv, 2, 0)
    row = lax.broadcasted_iota(jnp.int32, (8, xv.shape[1]), 0)
    xm1 = jnp.concatenate([jnp.where(row == 0, p1, r1[0:8]), r1[8:]], axis=0)
    xm2 = jnp.concatenate([jnp.where(row == 0, p2, jnp.where(row == 1, p1, r2[0:8])), r2[8:]], axis=0)
    return xv, xm1, xm2


def _conv_specs():
    def prev_row(i):
        return jnp.maximum(i * (CONV_TR // HALO) - 1, 0)
    ua = pl.BlockSpec((CONV_TR, CONV_TC), lambda i, j: (i, j))
    ug = pl.BlockSpec((CONV_TR, CONV_TC), lambda i, j: (i, j + CONV_NJ))
    pa = pl.BlockSpec((HALO, CONV_TC), lambda i, j: (prev_row(i), j))
    pg = pl.BlockSpec((HALO, CONV_TC), lambda i, j: (prev_row(i), j + CONV_NJ))
    wa = pl.BlockSpec((3, CONV_TC), lambda i, j: (0, j))
    wg = pl.BlockSpec((3, CONV_TC), lambda i, j: (0, j + CONV_NJ))
    ba = pl.BlockSpec((1, CONV_TC), lambda i, j: (0, j))
    bg = pl.BlockSpec((1, CONV_TC), lambda i, j: (0, j + CONV_NJ))
    return [ua, pa, ug, pg, wa, wg, ba, bg]


def _convgate_fwd(u, w, b, name):
    def body(ua, pa, ug, pg, wa, wg, ba, bg, o_ref):
        first = pl.program_id(0) == 0
        x0, x1, x2 = _causal_taps(ua, pa, first)
        ac = wa[0:1, :] * x2 + wa[1:2, :] * x1 + wa[2:3, :] * x0 + ba[...]
        x0, x1, x2 = _causal_taps(ug, pg, first)
        gc = wg[0:1, :] * x2 + wg[1:2, :] * x1 + wg[2:3, :] * x0 + bg[...]
        sg = 1.0 / (1.0 + jnp.exp(-gc))
        o_ref[...] = (gc * sg * ac).astype(BF16)

    return pl.pallas_call(
        body, name=name, grid=(S // CONV_TR, CONV_NJ),
        out_shape=jax.ShapeDtypeStruct((S, D_FF), BF16),
        in_specs=_conv_specs(),
        out_specs=pl.BlockSpec((CONV_TR, CONV_TC), lambda i, j: (i, j)),
        compiler_params=_params("parallel", "parallel"),
    )(u, u, u, u, w, w, b, b)


def _anticausal_conv(d, nxt_ref, w_ref, last):
    n1 = jnp.where(last, 0.0, nxt_ref[0:1, :])
    n2 = jnp.where(last, 0.0, nxt_ref[1:2, :])
    r1, r2 = pltpu.roll(d, CONV_TR - 1, 0), pltpu.roll(d, CONV_TR - 2, 0)
    row = lax.broadcasted_iota(jnp.int32, (8, d.shape[1]), 0)
    cut = CONV_TR - 8
    dp1 = jnp.concatenate([r1[:cut], jnp.where(row == 7, n1, r1[cut:])], axis=0)
    dp2 = jnp.concatenate([r2[:cut], jnp.where(row == 7, n2, jnp.where(row == 6, n1, r2[cut:]))], axis=0)
    return w_ref[2:3, :] * d + w_ref[1:2, :] * dp1 + w_ref[0:1, :] * dp2


def _convgate_bwd(u, w, b, dact, name):
    n_i = S // CONV_TR

    def body(ua, pa, ug, pg, wa, wg, ba, bg, d_ref, dua_ref, dug_ref, dwa_ref, dwg_ref, dba_ref, dbg_ref,
             nxt_a, nxt_g):
        i = pl.program_id(1)
        last = i == 0
        first = i == n_i - 1
        a0, a1, a2 = _causal_taps(ua, pa, first)
        ac = wa[0:1, :] * a2 + wa[1:2, :] * a1 + wa[2:3, :] * a0 + ba[...]
        g0, g1, g2 = _causal_taps(ug, pg, first)
        gc = wg[0:1, :] * g2 + wg[1:2, :] * g1 + wg[2:3, :] * g0 + bg[...]
        sg = 1.0 / (1.0 + jnp.exp(-gc))
        dact_v = d_ref[...].astype(F32)
        da = dact_v * (gc * sg)
        dg = dact_v * ac * (sg * (1.0 + gc * (1.0 - sg)))
        dua_ref[...] = _anticausal_conv(da, nxt_a, wa, last).astype(BF16)
        dug_ref[...] = _anticausal_conv(dg, nxt_g, wg, last).astype(BF16)
        nxt_a[...] = da[0:8]
        nxt_g[...] = dg[0:8]

        def col(v):
            return jnp.sum(v, axis=0, keepdims=True)

        parts = [col(da * a2), col(da * a1), col(da * a0), col(dg * g2), col(dg * g1), col(dg * g0),
                 col(da), col(dg)]

        @pl.when(last)
        def _():
            for k in range(3):
                dwa_ref[k:k + 1, :] = parts[k]
                dwg_ref[k:k + 1, :] = parts[3 + k]
            dba_ref[...] = parts[6]
            dbg_ref[...] = parts[7]

        @pl.when(i > 0)
        def _():
            for k in range(3):
                dwa_ref[k:k + 1, :] += parts[k]
                dwg_ref[k:k + 1, :] += parts[3 + k]
            dba_ref[...] += parts[6]
            dbg_ref[...] += parts[7]

    def swap(spec):
        return pl.BlockSpec(spec.block_shape, lambda j, i, f=spec.index_map: f(n_i - 1 - i, j))

    blk = pl.BlockSpec((CONV_TR, CONV_TC), lambda j, i: (n_i - 1 - i, j))
    w3 = pl.BlockSpec((3, CONV_TC), lambda j, i: (0, j))
    b1 = pl.BlockSpec((1, CONV_TC), lambda j, i: (0, j))
    return pl.pallas_call(
        body, name=name, grid=(CONV_NJ, n_i),
        out_shape=[jax.ShapeDtypeStruct((S, D_FF), BF16), jax.ShapeDtypeStruct((S, D_FF), BF16),
                   jax.ShapeDtypeStruct((3, D_FF), F32), jax.ShapeDtypeStruct((3, D_FF), F32),
                   jax.ShapeDtypeStruct((1, D_FF), F32), jax.ShapeDtypeStruct((1, D_FF), F32)],
        in_specs=[swap(s) for s in _conv_specs()] + [blk],
        out_specs=[blk, blk, w3, w3, b1, b1],
        scratch_shapes=[pltpu.VMEM((8, CONV_TC), F32), pltpu.VMEM((8, CONV_TC), F32)],
        compiler_params=_params("arbitrary", "arbitrary"),
    )(u, u, u, u, w, w, b, b, dact)


FOX_T = 512
FOX_TQ, FOX_TK = 512, 512
FOX_FORWARD_AT = 4
N_PAIRS = A_HEADS // 2


def _lane_masks():
    lane = lax.broadcasted_iota(jnp.int32, (1, LANES), 1)
    return lane, (lane < HEAD_DIM, lane >= HEAD_DIM)


def _fox_prep_fwd(z_t, b, name):
    def body(z_ref, b_ref, c_ref):
        r = lax.broadcasted_iota(jnp.int32, (LANES, LANES), 0)
        cc = lax.broadcasted_iota(jnp.int32, (LANES, LANES), 1)
        upper = (r <= cc).astype(BF16)
        carry = jnp.zeros((A_HEADS, 1), F32)
        for blk in range(S // LANES):
            sl = slice(blk * LANES, (blk + 1) * LANES)
            z = z_ref[:, sl] + b_ref[...]
            lf = jnp.minimum(z, 0.0) - jnp.log(1.0 + jnp.exp(-jnp.abs(z)))
            cs = _split_dot(lf, upper, 3) + carry
            c_ref[:, sl] = cs
            carry = cs[:, LANES - 1:LANES]

    return pl.pallas_call(
        body, name=name, out_shape=jax.ShapeDtypeStruct((A_HEADS, S), F32),
        compiler_params=_params(),
    )(z_t, b)


def _fox_prep_bwd(drow_t, dcol_t, z_t, b, name):
    def body(dr_ref, dc_ref, z_ref, b_ref, dz_ref, db_ref):
        r = lax.broadcasted_iota(jnp.int32, (LANES, LANES), 0)
        cc = lax.broadcasted_iota(jnp.int32, (LANES, LANES), 1)
        lower = (r >= cc).astype(BF16)
        carry = jnp.zeros((A_HEADS, 1), F32)
        db = jnp.zeros((A_HEADS, 1), F32)
        for blk in reversed(range(S // LANES)):
            sl = slice(blk * LANES, (blk + 1) * LANES)
            rc = _split_dot(dr_ref[:, sl] - dc_ref[:, sl], lower, 3) + carry
            carry = rc[:, 0:1]
            z = z_ref[:, sl] + b_ref[...]
            dz = rc / (1.0 + jnp.exp(z))
            dz_ref[:, sl] = dz
            db = db + jnp.sum(dz, axis=1, keepdims=True)
        db_ref[...] = db

    return pl.pallas_call(
        body, name=name,
        out_shape=[jax.ShapeDtypeStruct((A_HEADS, S), F32), jax.ShapeDtypeStruct((A_HEADS, 1), F32)],
        compiler_params=_params(),
    )(drow_t, dcol_t, z_t, b)


def _fox_fwd(qkv, c_t2, name, gather):
    tq, tk = FOX_TQ, FOX_TK

    n = len(gather)

    def body(*refs):
        q_ref, k_ref, v_ref, ct_ref = refs[:4]
        o_ref, lse_ref = refs[4 + n:6 + n]
        s_scr, p_scr, acc_scr = refs[-5:-3], refs[-3:-1], refs[-1]
        qi = pl.program_id(1)

        gather_start, gather_forward, gather_finish = _gather_phases(
            refs[4:4 + n], refs[6 + n:6 + 2 * n], refs[6 + 2 * n:len(refs) - 5])

        @pl.when(jnp.logical_and(pl.program_id(0) == 0, qi == 0))
        def _():
            gather_start()

        @pl.when(jnp.logical_and(pl.program_id(0) == FOX_FORWARD_AT, qi == 0))
        def _():
            gather_forward()

        n_full = jnp.right_shift(qi, (tk // tq).bit_length() - 1)
        lane, masks = _lane_masks()
        q = q_ref[...] * SCALE
        qs = [jnp.where(masks[e], q, jnp.zeros_like(q)) for e in range(2)]

        def scores_into(j, slot):
            start = pl.multiple_of(j * tk, tk)
            kb = k_ref[pl.ds(start, tk), :]
            for e in range(2):
                s_scr[slot][e] = _dot(qs[e], kb, NT_DIMS) - ct_ref[e:e + 1, pl.ds(start, tk)]

        def softmax_of(slot, m, masked):
            m_new, alpha = [], []
            for e in range(2):
                s = s_scr[slot][e]
                if masked:
                    rows = lax.broadcasted_iota(jnp.int32, (tq, tk), 0) + (qi * tq - n_full * tk)
                    cols = lax.broadcasted_iota(jnp.int32, (tq, tk), 1)
                    s = jnp.where(cols <= rows, s, NEG)
                m_new.append(jnp.maximum(m[e], jnp.max(s, axis=1, keepdims=True)))
                p_scr[slot][e] = jnp.exp(s - m_new[e]).astype(BF16)
                alpha.append(jnp.exp(m[e] - m_new[e]))
            return tuple(m_new), tuple(alpha)

        def values_of(j, slot, alpha):
            start = pl.multiple_of(j * tk, tk)
            vb = v_ref[pl.ds(start, tk), :]
            for e in range(2):
                acc_scr[e] = (alpha[e] * acc_scr[e]
                              + _dot(p_scr[slot][e], jnp.where(masks[e], vb, jnp.ones_like(vb))))

        def stage(j, cur, nxt, carry):
            m, a_prev = carry
            scores_into(j + 1, nxt)
            values_of(jnp.maximum(j - 1, 0), nxt, a_prev)
            return softmax_of(cur, m, False)

        def finish(cur, nxt, carry):
            m, a_prev = carry
            values_of(jnp.maximum(n_full - 1, 0), nxt, a_prev)
            (m0, m1), alpha = softmax_of(cur, m, True)
            values_of(n_full, cur, alpha)
            l0 = acc_scr[0][:, HEAD_DIM:HEAD_DIM + 1]
            l1 = acc_scr[1][:, 0:1]
            o_ref[...] = jnp.where(masks[0], acc_scr[0] / l0, acc_scr[1] / l1).astype(BF16)
            lse_ref[...] = jnp.where(masks[0], m0 + jnp.log(l0), m1 + jnp.log(l1))

        scores_into(0, 0)
        for e in range(2):
            p_scr[1][e] = jnp.zeros((tq, tk), BF16)
            acc_scr[e] = jnp.zeros((tq, LANES), F32)
        two = lambda x: (x, x)
        init = (two(jnp.full((tq, 1), NEG, F32)), two(jnp.ones((tq, 1), F32)))

        def two_stages(jj, carry):
            return stage(2 * jj + 1, 1, 0, stage(2 * jj, 0, 1, carry))

        carry = lax.fori_loop(0, jnp.right_shift(n_full, 1), two_stages, init)
        odd = jnp.bitwise_and(n_full, 1) == 1

        @pl.when(odd)
        def _():
            finish(1, 0, stage(n_full - 1, 0, 1, carry))

        @pl.when(jnp.logical_not(odd))
        def _():
            finish(0, 1, carry)

        @pl.when(jnp.logical_and(pl.program_id(0) == N_PAIRS - 1, qi == S // tq - 1))
        def _():
            gather_finish()

    qspec = pl.BlockSpec((tq, LANES), lambda h, i: (i, h))
    return pl.pallas_call(
        body, name=name, grid=(N_PAIRS, S // tq),
        out_shape=[jax.ShapeDtypeStruct((S, D), BF16), jax.ShapeDtypeStruct((S, D), F32)]
        + _exchange_shapes(gather, False),
        in_specs=[qspec,
                  pl.BlockSpec((S, LANES), lambda h, i: (0, N_PAIRS + h)),
                  pl.BlockSpec((S, LANES), lambda h, i: (0, 2 * N_PAIRS + h)),
                  pl.BlockSpec((None, 2, S), lambda h, i: (h, 0, 0))] + [ANY] * n,
        out_specs=[qspec, qspec] + [ANY] * n,
        scratch_shapes=_exchange_scratch(n) + [
            pltpu.VMEM((2, tq, tk), F32), pltpu.VMEM((2, tq, tk), F32),
            pltpu.VMEM((2, tq, tk), BF16), pltpu.VMEM((2, tq, tk), BF16),
            pltpu.VMEM((2, tq, LANES), F32)],
        compiler_params=_params("arbitrary", "arbitrary"),
    )(qkv, qkv, qkv, c_t2, *gather)


def _head_rowsum(a, b, name, tr=256):
    C = a.shape[1]

    def body(a_ref, b_ref, o_ref):
        r = lax.broadcasted_iota(jnp.int32, (LANES, LANES), 0) < HEAD_DIM
        cc = lax.broadcasted_iota(jnp.int32, (LANES, LANES), 1) < HEAD_DIM
        same_head = (r == cc).astype(BF16)
        for blk in range(C // LANES):
            sl = slice(blk * LANES, (blk + 1) * LANES)
            prod = a_ref[:, sl].astype(F32) * b_ref[:, sl].astype(F32)
            o_ref[:, sl] = _split_dot(prod, same_head, 2)

    row = pl.BlockSpec((tr, C), lambda i: (i, 0))
    return pl.pallas_call(
        body, name=name, grid=(S // tr,), out_shape=jax.ShapeDtypeStruct((S, C), F32),
        in_specs=[row, row], out_specs=row, compiler_params=_params("parallel"),
    )(a, b)


def _fox_bwd(qkv, do, lse, delta, c_t2, name, scatter):
    t = FOX_T
    nq = S // t

    n = len(scatter)

    def body(*refs):
        q_ref, k_ref, v_ref, do_ref, lse_ref, dl_ref, ct_ref = refs[:7]
        dq_ref, dk_ref, dv_ref, dcol_ref, drow_ref = refs[7 + n:12 + n]
        exchange = (refs[7:7 + n], refs[12 + n:12 + 2 * n], refs[12 + 2 * n:len(refs) - 5], True)
        sd_scr, pd_scr, acc_scr = refs[-5:-3], refs[-3:-1], refs[-1]
        kj = pl.program_id(1)

        @pl.when(jnp.logical_and(pl.program_id(0) == 0, kj == 0))
        def _():
            for cp in _exchange_copies(*exchange):
                cp.start()

        @pl.when(kj == 0)
        def _():
            dq_ref[...] = jnp.zeros_like(dq_ref)
            drow_ref[...] = jnp.zeros_like(drow_ref)

        lane, masks = _lane_masks()
        k = k_ref[...]
        v = v_ref[...]
        k_aug = [jnp.where(masks[e], k * SCALE, jnp.ones_like(k)) for e in range(2)]
        cs = [ct_ref[e:e + 1, :] for e in range(2)]

        def rows_of(i):
            r0 = pl.multiple_of(i * t, t)
            return pl.ds(r0, t), q_ref[pl.ds(r0, t), :] * SCALE, do_ref[pl.ds(r0, t), :]

        def scores_into(i, slot):
            _, qb, dob = rows_of(i)
            for e in range(2):
                qe = jnp.where(masks[e], qb, jnp.zeros_like(qb))
                doe = jnp.where(masks[e], dob, jnp.zeros_like(dob))
                sd_scr[slot][2 * e] = _dot(qe, k, NT_DIMS) - cs[e]
                sd_scr[slot][2 * e + 1] = _dot(doe, v, NT_DIMS)

        def pointwise(i, slot, masked):
            rows, _, _ = rows_of(i)
            for e in range(2):
                lo = e * HEAD_DIM
                s = sd_scr[slot][2 * e]
                if masked:
                    r = lax.broadcasted_iota(jnp.int32, (t, t), 0)
                    c = lax.broadcasted_iota(jnp.int32, (t, t), 1)
                    s = jnp.where(c <= r, s, NEG)
                p = jnp.exp(s - lse_ref[rows, lo:lo + 1])
                pd_scr[slot][2 * e] = p.astype(BF16)
                pd_scr[slot][2 * e + 1] = (p * (sd_scr[slot][2 * e + 1] - dl_ref[rows, lo:lo + 1])).astype(BF16)

        def accumulate(i, slot):
            rows, qb, dob = rows_of(i)
            dq_parts = []
            for e in range(2):
                p, ds = pd_scr[slot][2 * e], pd_scr[slot][2 * e + 1]
                q_aug = jnp.where(masks[e], qb, jnp.ones_like(qb))
                doe = jnp.where(masks[e], dob, jnp.zeros_like(dob))
                acc_scr[2] += _dot(p, doe, TN_DIMS)
                acc_scr[e] += _dot(ds, q_aug, TN_DIMS)
                dq_parts.append(_dot(ds, k_aug[e]))
            dq_ref[rows, :] += jnp.where(masks[0], dq_parts[0], dq_parts[1])
            drow_ref[rows, :] += jnp.where(masks[0], dq_parts[1], dq_parts[0])

        def stage(i, cur, nxt):
            scores_into(jnp.minimum(i + 1, nq - 1), nxt)
            accumulate(i - 1, nxt)
            pointwise(i, cur, False)

        acc_scr[...] = jnp.zeros_like(acc_scr)
        scores_into(kj, 0)
        pointwise(kj, 0, True)
        scores_into(jnp.minimum(kj + 1, nq - 1), 1)
        rest = nq - 1 - kj

        def two_stages(jj, carry):
            stage(kj + 1 + 2 * jj, 1, 0)
            stage(kj + 2 + 2 * jj, 0, 1)
            return carry

        lax.fori_loop(0, jnp.right_shift(rest, 1), two_stages, 0)
        odd = jnp.bitwise_and(rest, 1) == 1

        @pl.when(odd)
        def _():
            stage(nq - 1, 1, 0)
            accumulate(nq - 1, 1)

        @pl.when(jnp.logical_not(odd))
        def _():
            accumulate(nq - 1, 0)

        dk0, dk1, dv = acc_scr[0], acc_scr[1], acc_scr[2]
        dk_ref[...] = jnp.where(masks[0], dk0, dk1).astype(BF16)
        dcol_ref[...] = jnp.where(masks[0], dk1, dk0)
        dv_ref[...] = dv.astype(BF16)

        @pl.when(jnp.logical_and(pl.program_id(0) == N_PAIRS - 1, kj == nq - 1))
        def _():
            for cp in _exchange_copies(*exchange):
                cp.wait()

    full = lambda off: pl.BlockSpec((S, LANES), lambda h, j, off=off: (0, off + h))
    kv = lambda off: pl.BlockSpec((t, LANES), lambda h, j, off=off: (j, off + h))
    return pl.pallas_call(
        body, name=name, grid=(N_PAIRS, nq),
        out_shape=[jax.ShapeDtypeStruct((S, D), F32), jax.ShapeDtypeStruct((S, D), BF16),
                   jax.ShapeDtypeStruct((S, D), BF16), jax.ShapeDtypeStruct((S, D), F32),
                   jax.ShapeDtypeStruct((S, D), F32)] + _exchange_shapes(scatter, True),
        in_specs=[full(0), kv(N_PAIRS), kv(2 * N_PAIRS), full(0), full(0), full(0),
                  pl.BlockSpec((None, 2, t), lambda h, j: (h, 0, j))] + [ANY] * n,
        out_specs=[full(0), kv(0), kv(0), kv(0), full(0)] + [ANY] * n,
        scratch_shapes=_exchange_scratch(n) + [
            pltpu.VMEM((4, t, t), F32), pltpu.VMEM((4, t, t), F32),
            pltpu.VMEM((4, t, t), BF16), pltpu.VMEM((4, t, t), BF16),
            pltpu.VMEM((3, t, LANES), F32)],
        compiler_params=_params("arbitrary", "arbitrary"),
    )(qkv, qkv, qkv, do, lse, delta, c_t2, *scatter)


B_PAIRS = 4
B_NB = S // B_W


def _group_consts(g):
    nbs = jnp.where(g == 0, B_NB // B_DILS[0], jnp.where(g == 1, B_NB // B_DILS[1], B_NB // B_DILS[2]))
    dil = jnp.where(g == 0, B_DILS[0], jnp.where(g == 1, B_DILS[1], B_DILS[2]))
    return nbs, dil


def _band(dil):
    qi = lax.broadcasted_iota(jnp.int32, (B_W, B_W), 0)
    kj = lax.broadcasted_iota(jnp.int32, (B_W, B_W), 1)
    dist_c = qi - kj
    dist_p = qi + B_W - kj
    return (dist_c * dil).astype(F32), dist_c >= 0, (dist_p * dil).astype(F32), dist_p <= B_W


def _dil_fwd(qp, kp, vp, slopes, name):
    def body(sl_ref, q_ref, kp_ref, kc_ref, vp_ref, vc_ref, o_ref, lse_ref):
        g, n = pl.program_id(0), pl.program_id(1)
        nbs, dil = _group_consts(g)
        has_prev = (n % nbs) != 0
        lane, masks = _lane_masks()
        bias_c, ok_c, bias_p, ok_p = _band(dil)
        ok_p = jnp.logical_and(ok_p, has_prev)
        heads = [(hp, e) for hp in range(B_PAIRS) for e in range(2)]
        col = lambda ref, hp: ref[:, hp * LANES:(hp + 1) * LANES]
        logits = []
        for hp, e in heads:
            q = col(q_ref, hp) * SCALE
            qe = jnp.where(masks[e], q, jnp.zeros_like(q))
            logits.append((_dot(qe, col(kc_ref, hp), NT_DIMS), _dot(qe, col(kp_ref, hp), NT_DIMS)))
        probs = []
        for (hp, e), (sc, sp) in zip(heads, logits):
            slope = sl_ref[g * 8 + 2 * hp + e]
            sc = jnp.where(ok_c, sc - slope * bias_c, NEG)
            sp = jnp.where(ok_p, sp - slope * bias_p, NEG)
            m = jnp.maximum(jnp.max(sc, axis=1, keepdims=True), jnp.max(sp, axis=1, keepdims=True))
            probs.append((jnp.exp(sc - m).astype(BF16), jnp.exp(sp - m).astype(BF16), m))
        outs, lses = [], []
        for (hp, e), (pc, pp, m) in zip(heads, probs):
            vc, vpv = col(vc_ref, hp), col(vp_ref, hp)
            acc = (_dot(pc, jnp.where(masks[e], vc, jnp.ones_like(vc)))
                   + _dot(pp, jnp.where(masks[e], vpv, jnp.ones_like(vpv))))
            l = acc[:, HEAD_DIM:HEAD_DIM + 1] if e == 0 else acc[:, 0:1]
            outs.append(acc / l)
            lses.append(m + jnp.log(l))
        o_ref[...] = jnp.concatenate(
            [jnp.where(masks[0], outs[2 * hp], outs[2 * hp + 1]) for hp in range(B_PAIRS)], axis=1)
        lse_ref[...] = jnp.concatenate(
            [jnp.where(masks[0], lses[2 * hp], lses[2 * hp + 1]) for hp in range(B_PAIRS)], axis=1)

    cur = pl.BlockSpec((None, B_W, B_OUT), lambda g, n, sl: (g, n, 0))
    prev = pl.BlockSpec((None, B_W, B_OUT), lambda g, n, sl: (g, jnp.maximum(n - 1, 0), 0))
    return pl.pallas_call(
        body, name=name,
        grid_spec=pltpu.PrefetchScalarGridSpec(
            num_scalar_prefetch=1, grid=(3, B_NB),
            in_specs=[cur, prev, cur, prev, cur], out_specs=[cur, cur]),
        out_shape=[jax.ShapeDtypeStruct((3, S, B_OUT), F32), jax.ShapeDtypeStruct((3, S, B_OUT), F32)],
        compiler_params=_params("parallel", "parallel"),
    )(slopes, qp, kp, kp, vp, vp)


def _dil_merge(og, lseg, name, tr=256):
    def body(o_ref, l_ref, out_ref, lse_ref):
        l0, l1, l2 = l_ref[0], l_ref[1], l_ref[2]
        m = jnp.maximum(jnp.maximum(l0, l1), l2)
        w0, w1, w2 = jnp.exp(l0 - m), jnp.exp(l1 - m), jnp.exp(l2 - m)
        den = w0 + w1 + w2
        out_ref[...] = ((w0 * o_ref[0] + w1 * o_ref[1] + w2 * o_ref[2]) / den).astype(BF16)
        lse_ref[...] = m + jnp.log(den)

    blk3 = pl.BlockSpec((3, tr, B_OUT), lambda i: (0, i, 0))
    blk = pl.BlockSpec((tr, B_OUT), lambda i: (i, 0))
    return pl.pallas_call(
        body, name=name, grid=(S // tr,),
        out_shape=[jax.ShapeDtypeStruct((S, B_OUT), BF16), jax.ShapeDtypeStruct((S, B_OUT), F32)],
        in_specs=[blk3, blk3], out_specs=[blk, blk], compiler_params=_params("parallel"),
    )(og, lseg)


def _dil_bwd(qp, kp, vp, dop, lsep, dlp, slopes, name, scatter):
    n_ex = len(scatter)

    def body(sl_ref, *refs):
        (qc_ref, qn_ref, kp_ref, kc_ref, vp_ref, vc_ref, doc_ref, don_ref,
         lc_ref, ln_ref, dc_ref, dn_ref) = refs[:12]
        dq_ref, dk_ref, dv_ref = refs[12 + n_ex:15 + n_ex]
        exchange = (refs[12:12 + n_ex], refs[15 + n_ex:15 + 2 * n_ex], refs[15 + 2 * n_ex:], True)
        g, n = pl.program_id(0), pl.program_id(1)

        @pl.when(jnp.logical_and(g == 0, n == 0))
        def _():
            for cp in _exchange_copies(*exchange):
                cp.start()

        nbs, dil = _group_consts(g)
        has_prev = (n % nbs) != 0
        has_next = jnp.logical_and(n + 1 < B_NB, ((n + 1) % nbs) != 0)
        lane, masks = _lane_masks()
        bias_c, ok_c, bias_p, ok_p = _band(dil)
        ok_pp = jnp.logical_and(ok_p, has_prev)
        ok_np = jnp.logical_and(ok_p, has_next)
        heads = [(hp, e) for hp in range(B_PAIRS) for e in range(2)]
        col = lambda ref, hp: ref[:, hp * LANES:(hp + 1) * LANES]
        mask = lambda t, e: jnp.where(masks[e], t, jnp.zeros_like(t))
        raw = []
        for hp, e in heads:
            qce, qne = mask(col(qc_ref, hp) * SCALE, e), mask(col(qn_ref, hp) * SCALE, e)
            doce, done = mask(col(doc_ref, hp), e), mask(col(don_ref, hp), e)
            kc, kpv, vc, vpv = col(kc_ref, hp), col(kp_ref, hp), col(vc_ref, hp), col(vp_ref, hp)
            raw.append(((_dot(qce, kc, NT_DIMS), _dot(doce, vc, NT_DIMS)),
                        (_dot(qce, kpv, NT_DIMS), _dot(doce, vpv, NT_DIMS)),
                        (_dot(qne, kc, NT_DIMS), _dot(done, vc, NT_DIMS))))
        pds = []
        for (hp, e), tiles in zip(heads, raw):
            lo = hp * LANES + e * HEAD_DIM
            slope = sl_ref[g * 8 + 2 * hp + e]
            lse_c, dl_c = lc_ref[:, lo:lo + 1], dc_ref[:, lo:lo + 1]
            lse_n, dl_n = ln_ref[:, lo:lo + 1], dn_ref[:, lo:lo + 1]
            out = []
            for (s, dp), ok, bias, lse, dl in ((tiles[0], ok_c, bias_c, lse_c, dl_c),
                                               (tiles[1], ok_pp, bias_p, lse_c, dl_c),
                                               (tiles[2], ok_np, bias_p, lse_n, dl_n)):
                p = jnp.exp(jnp.where(ok, s - slope * bias, NEG) - lse)
                out.append((p.astype(BF16), (p * (dp - dl)).astype(BF16)))
            pds.append(out)
        dq_all, dk_all, dv_all = [], [], []
        for hp in range(B_PAIRS):
            dq = jnp.zeros((B_W, LANES), F32)
            dk = jnp.zeros((B_W, LANES), F32)
            dv = jnp.zeros((B_W, LANES), F32)
            for e in range(2):
                (p_c, ds_c), (_, ds_p), (p_n, ds_n) = pds[2 * hp + e]
                qce, qne = mask(col(qc_ref, hp) * SCALE, e), mask(col(qn_ref, hp) * SCALE, e)
                doce, done = mask(col(doc_ref, hp), e), mask(col(don_ref, hp), e)
                dq = dq + _dot(ds_c, mask(col(kc_ref, hp) * SCALE, e)) + _dot(ds_p, mask(col(kp_ref, hp) * SCALE, e))
                dk = dk + _dot(ds_c, qce, TN_DIMS) + _dot(ds_n, qne, TN_DIMS)
                dv = dv + _dot(p_c, doce, TN_DIMS) + _dot(p_n, done, TN_DIMS)
            dq_all.append(dq)
            dk_all.append(dk)
            dv_all.append(dv)
        dq_ref[...] = jnp.concatenate(dq_all, axis=1).astype(BF16)
        dk_ref[...] = jnp.concatenate(dk_all, axis=1).astype(BF16)
        dv_ref[...] = jnp.concatenate(dv_all, axis=1).astype(BF16)

        @pl.when(jnp.logical_and(g == 2, n == B_NB - 1))
        def _():
            for cp in _exchange_copies(*exchange):
                cp.wait()

    cur = pl.BlockSpec((None, B_W, B_OUT), lambda g, n, sl: (g, n, 0))
    prev = pl.BlockSpec((None, B_W, B_OUT), lambda g, n, sl: (g, jnp.maximum(n - 1, 0), 0))
    nxt = pl.BlockSpec((None, B_W, B_OUT), lambda g, n, sl: (g, jnp.minimum(n + 1, B_NB - 1), 0))
    return pl.pallas_call(
        body, name=name,
        grid_spec=pltpu.PrefetchScalarGridSpec(
            num_scalar_prefetch=1, grid=(3, B_NB),
            in_specs=[cur, nxt, prev, cur, prev, cur, cur, nxt, cur, nxt, cur, nxt] + [ANY] * n_ex,
            out_specs=[cur, cur, cur] + [ANY] * n_ex,
            scratch_shapes=_exchange_scratch(n_ex)),
        out_shape=[jax.ShapeDtypeStruct((3, S, B_OUT), BF16)] * 3 + _exchange_shapes(scatter, True),
        compiler_params=_params("arbitrary", "arbitrary"),
    )(slopes, qp, qp, kp, kp, vp, vp, dop, dop, lsep, lsep, dlp, dlp, *scatter)


def _rows_block(shape, max_bytes=2 * 1024 * 1024):
    rows, cols = shape
    padded_cols = -(-cols // LANES) * LANES
    for tr in (1024, 512, 256, 128, 64, 32, 16):
        if rows % tr == 0 and tr * padded_cols * 4 <= max_bytes:
            return tr
    return rows


def _adam_update(w, m, v, g):
    m_new = ADAM_B1 * m + (1.0 - ADAM_B1) * g
    v_new = ADAM_B2 * v + (1.0 - ADAM_B2) * (g * g)
    m_hat = m_new / (1.0 - ADAM_B1 ** ADAM_STEP)
    v_hat = v_new / (1.0 - ADAM_B2 ** ADAM_STEP)
    delta = -ADAM_LR * (m_hat / (jnp.sqrt(v_hat) + ADAM_EPS) + ADAM_WD * w)
    return delta, m_new, v_new


def _adamw_sharded(w, m, v, parts, name):
    R, C = w.shape
    tr = _rows_block((R, C), max_bytes=1024 * 1024)

    def body(w_ref, m_ref, v_ref, p_ref, g_ref, d_ref, mo_ref, vo_ref):
        g = p_ref[0].astype(F32)
        for dev in range(1, N_DEV):
            g = g + p_ref[dev].astype(F32)
        g_ref[...] = g
        d_ref[...], mo_ref[...], vo_ref[...] = _adam_update(w_ref[...], m_ref[...], v_ref[...], g)

    blk = pl.BlockSpec((tr, C), lambda i: (i, 0))
    out = jax.ShapeDtypeStruct((R, C), F32)
    return pl.pallas_call(
        body, name=name, grid=(R // tr,),
        in_specs=[blk, blk, blk, pl.BlockSpec((N_DEV, tr, C), lambda i: (0, i, 0))],
        out_specs=[blk, blk, blk, blk], out_shape=[out, out, out, out],
        compiler_params=_params("parallel"),
    )(w, m, v, parts)


def _adamw_replicated(w, m, v, parts, name):
    def body(w_ref, m_ref, v_ref, p_ref, g_ref, d_ref, mo_ref, vo_ref):
        g = p_ref[0]
        for dev in range(1, N_DEV):
            g = g + p_ref[dev]
        g_ref[...] = g
        d_ref[...], mo_ref[...], vo_ref[...] = _adam_update(w_ref[...], m_ref[...], v_ref[...], g)

    out = jax.ShapeDtypeStruct(w.shape, F32)
    return pl.pallas_call(body, name=name, out_shape=[out, out, out, out], compiler_params=_params())(w, m, v, parts)


def _cols_from_slots(g):
    return g.transpose(1, 0, 2).reshape(g.shape[1], N_DEV * g.shape[2])


def _cols_to_slots(w):
    k, n = w.shape
    return w.reshape(k, N_DEV, n // N_DEV).transpose(1, 0, 2)


def _permute(t, dil):
    c = t.shape[1]
    return t.reshape(S // dil, dil, c).transpose(1, 0, 2).reshape(S, c)


def _unpermute(t, dil):
    c = t.shape[1]
    return t.reshape(dil, S // dil, c).transpose(1, 0, 2).reshape(S, c)


def _group_permute(t):
    return jnp.stack([_permute(t[:, g * B_OUT:(g + 1) * B_OUT], B_DILS[g]) for g in range(3)])


def _same_permute(t):
    return jnp.stack([_permute(t, d) for d in B_DILS])


def _group_unpermute(t):
    return jnp.stack([_unpermute(t[g], B_DILS[g]) for g in range(3)])


SMALL_ROWS = 144


def _pack_small(a_b_f, kv_g, mix_g, ffn_g, conv_b, fin_g):
    flat = jnp.concatenate([a_b_f.reshape(-1), kv_g.reshape(-1), mix_g.reshape(-1), ffn_g.reshape(-1),
                            conv_b.reshape(-1), fin_g.reshape(-1)])
    return jnp.pad(flat, (0, SMALL_ROWS * LANES - flat.shape[0])).reshape(SMALL_ROWS, LANES)


def _unpack_small(p):
    flat = p.reshape(-1)
    out, off = [], 0
    for shape in ((1, A_HEADS), (D,), (2, D), (2, D), (2, 2 * D_FF), (D,)):
        size = math.prod(shape)
        out.append(flat[off:off + size].reshape(shape))
        off += size
    return out


def _unpack_late(g):
    w_up = g[4].reshape(N_DEV, 2, D, -1).transpose(1, 2, 0, 3).reshape(2, D, 2 * D_FF)
    w_down = g[5].reshape(N_DEV, 2, -1, D).transpose(1, 0, 2, 3).reshape(2, D_FF, D)
    conv_w = g[6].reshape(N_DEV, 2, 3, -1).transpose(1, 2, 0, 3).reshape(2, 3, 2 * D_FF)
    return (g[0].reshape(D, D), _cols_from_slots(g[1]), _cols_from_slots(g[2]), _cols_from_slots(g[3]),
            w_up, w_down, conv_w)


def _ffn_slots(dw_up, dw_down, dconv_w):
    return [_cols_to_slots(dw_up), dw_down.reshape(N_DEV, -1, D), _cols_to_slots(dconv_w)]


def _local_step(x0, target, w_in_pad, late_shards,
                a_b_f, kv_norm_g, mix_norm_g, ffn_norm_g, ffn_conv_b, final_norm_g):
    w_qkv, w_f = w_in_pad[:, :A_QKV], w_in_pad[:, A_QKV:]
    conv_b = ffn_conv_b.reshape(2, 1, 2 * D_FF)
    slopes = jnp.exp2(-8.0 * jnp.arange(1, 25, dtype=F32) / 24)

    def gain(g):
        return g.reshape(1, D)

    (h1,) = _rmsnorm_fwd(x0, [gain(mix_norm_g[0])], "norm_mix0")
    qkv = _matmul(h1, w_qkv, mode="nn", out_dtype=BF16, name="proj_qkv", tm=512, tn=A_QKV)
    z = _matmul(h1, w_f, mode="nn", out_dtype=F32, name="proj_gate", tm=S, tn=LANES)
    z_t = z[:, :A_HEADS].T
    b_f = a_b_f.reshape(A_HEADS, 1)
    c_t = _fox_prep_fwd(z_t, b_f, "fox_prep")
    c_t2 = c_t.reshape(N_PAIRS, 2, S)
    o_a, lse_a, *late = _fox_fwd(qkv, c_t2, "fox_fwd", late_shards)
    w_out, w_q, w_bo, w_kvf, w_up, w_down, conv_w = _unpack_late(late)
    x1 = _matmul(o_a, w_out, mode="nn", out_dtype=F32, name="a_out", tm=512, tn=D, res=x0)

    def ffn_fwd(xin, layer):
        (h,) = _rmsnorm_fwd(xin, [gain(ffn_norm_g[layer])], f"norm_ffn{layer}")
        u = _matmul(h, w_up[layer], mode="nn", out_dtype=BF16, name=f"ffn_up{layer}", tm=512, tn=2 * D_FF)
        act = _convgate_fwd(u, conv_w[layer], conv_b[layer], f"convgate{layer}")
        xout = _matmul(act, w_down[layer], mode="nn", out_dtype=F32, name=f"ffn_down{layer}", tm=512, tn=D, res=xin)
        return h, u, act, xout

    h2, u0, act0, x2 = ffn_fwd(x1, 0)
    hk, h3 = _rmsnorm_fwd(x2, [gain(kv_norm_g), gain(mix_norm_g[1])], "norm_kv_mix1")
    kv = _matmul(hk, w_kvf, mode="nn", out_dtype=BF16, name="proj_kv", tm=512, tn=B_KV)
    qb = _matmul(h3, w_q, mode="nn", out_dtype=BF16, name="proj_qb", tm=512, tn=B_Q)
    qp, kp, vp = _group_permute(qb), _group_permute(kv[:, :B_Q]), _group_permute(kv[:, B_Q:])
    og_p, lseg_p = _dil_fwd(qp, kp, vp, slopes, "dil_fwd")
    o_b, lse_b = _dil_merge(_group_unpermute(og_p), _group_unpermute(lseg_p), "dil_merge")
    x3 = _matmul(o_b, w_bo, mode="nn", out_dtype=F32, name="b_out", tm=512, tn=D, res=x2)
    h4, u1, act1, x4 = ffn_fwd(x3, 1)
    loss_blk, dx4, dx4b, dg_final = _final_loss(x4, target, gain(final_norm_g), "final_loss")

    def ffn_bwd(dx, dxb, xin, h, u, act, layer):
        dact = _matmul(dxb, w_down[layer], mode="nt", out_dtype=BF16, name=f"d_act{layer}", tm=512, tn=D_FF)
        dw_down = _matmul_tn(act, dxb, out_dtype=BF16, name=f"dw_down{layer}")
        du_a, du_g, dwa, dwg, dba, dbg = _convgate_bwd(u, conv_w[layer], conv_b[layer], dact, f"convgate_bwd{layer}")
        dw_up = jnp.concatenate(
            [_matmul_tn(h, du_a, out_dtype=BF16, name=f"dw_up_a{layer}"),
             _matmul_tn(h, du_g, out_dtype=BF16, name=f"dw_up_g{layer}")], axis=1)
        dh = _matmul(du_a, w_up[layer][:, :D_FF], mode="nt", out_dtype=F32, name=f"dh_ffn_a{layer}", tm=512, tn=D)
        dh = _matmul(du_g, w_up[layer][:, D_FF:], mode="nt", out_dtype=F32, name=f"dh_ffn_g{layer}", tm=512, tn=D,
                     res=dh)
        dxin, dxinb, dgain = _rmsnorm_bwd(xin, dh, gain(ffn_norm_g[layer]), dx, f"norm_ffn_bwd{layer}")
        dconv_w = jnp.concatenate([dwa, dwg], axis=1)
        dconv_b = jnp.concatenate([dba, dbg], axis=1)
        return dxin, dxinb, dgain, dw_up, dw_down, dconv_w, dconv_b

    dx3, dx3b, dg_ffn1, dw_up1, dw_down1, dconv_w1, dconv_b1 = ffn_bwd(dx4, dx4b, x3, h4, u1, act1, 1)

    do_b = _matmul(dx3b, w_bo, mode="nt", out_dtype=BF16, name="d_ob", tm=1024, tn=B_OUT)
    dw_bo = _matmul_tn(o_b, dx3b, out_dtype=BF16, name="dw_bo")
    dl_b = _head_rowsum(do_b, o_b, "delta_b")
    slots_up1, slots_down1, slots_conv1 = _ffn_slots(dw_up1, dw_down1, dconv_w1)
    dqp, dkp, dvp, land_down1, land_conv1 = _dil_bwd(
        qp, kp, vp, _same_permute(do_b), _same_permute(lse_b), _same_permute(dl_b), slopes, "dil_bwd",
        [slots_down1, slots_conv1])

    def natural(tp):
        return jnp.concatenate([_unpermute(tp[g], B_DILS[g]) for g in range(3)], axis=1)

    dqb = natural(dqp)
    dkv = jnp.concatenate([natural(dkp), natural(dvp)], axis=1)
    dw_q = _matmul_tn(h3, dqb, out_dtype=BF16, name="dw_q")
    dw_kv = _matmul_tn(hk, dkv, out_dtype=BF16, name="dw_kv")
    dh3 = _matmul(dqb, w_q, mode="nt", out_dtype=F32, name="dh_mix1", tm=512, tn=D)
    dhk = _matmul(dkv, w_kvf, mode="nt", out_dtype=F32, name="dh_kv", tm=512, tn=D)
    dx2, _, dg_mix1 = _rmsnorm_bwd(x2, dh3, gain(mix_norm_g[1]), dx3, "norm_mix1_bwd")
    dx2, dx2b, dg_kv = _rmsnorm_bwd(x2, dhk, gain(kv_norm_g), dx2, "norm_kv_bwd")

    dx1, dx1b, dg_ffn0, dw_up0, dw_down0, dconv_w0, dconv_b0 = ffn_bwd(dx2, dx2b, x1, h2, u0, act0, 0)

    do_a = _matmul(dx1b, w_out, mode="nt", out_dtype=BF16, name="d_oa", tm=512, tn=D)
    dw_out = _matmul_tn(o_a, dx1b, out_dtype=BF16, name="dw_out")
    dl_a = _head_rowsum(do_a, o_a, "delta_a")
    dq_a, dk_a, dv_a, dcol, drow, *land = _fox_bwd(
        qkv, do_a, lse_a, dl_a, c_t2, "fox_bwd",
        [dw_out.reshape(N_DEV, D // N_DEV, D), _cols_to_slots(dw_q), _cols_to_slots(dw_bo), _cols_to_slots(dw_kv)]
        + _ffn_slots(dw_up0, dw_down0, dconv_w0) + [slots_up1])
    land_out, land_q, land_bo, land_kv, land_up0, land_down0, land_conv0, land_up1 = land

    def head_sums(t):
        return t.reshape(S, N_PAIRS, 2, HEAD_DIM)[:, :, ::-1, 0].reshape(S, A_HEADS).T

    dz_t, db_f = _fox_prep_bwd(head_sums(drow), head_sums(dcol), z_t, b_f, "fox_prep_bwd")
    dz = jnp.pad(dz_t.T, ((0, 0), (0, LANES - A_HEADS))).astype(BF16)
    dproj = jnp.concatenate([dq_a.astype(BF16), dk_a, dv_a, dz], axis=1)
    dw_in = _matmul_tn(h1, dproj, out_dtype=BF16, name="dw_in")
    dh1, land_in = _matmul(dproj, w_in_pad, mode="nt", out_dtype=F32, name="dh_mix0", tm=512, tn=D,
                           scatter=[_cols_to_slots(dw_in[:, :A_QKV + A_HEADS])])
    grad_x, _, dg_mix0 = _rmsnorm_bwd(x0, dh1, gain(mix_norm_g[0]), dx1, "norm_mix0_bwd")

    dg_mix = jnp.concatenate([dg_mix0, dg_mix1], axis=0)
    dg_ffn = jnp.concatenate([dg_ffn0, dg_ffn1], axis=0)
    dconv_b = jnp.concatenate([dconv_b0, dconv_b1], axis=0)
    small_part = _pack_small(db_f, dg_kv, dg_mix, dg_ffn, dconv_b, dg_final)
    _, (small_parts,) = _final_exchange([], [small_part], "gather_small_grads")
    landed = [land_in, land_out, land_q, land_bo, land_kv, land_up0, land_up1, land_down0, land_down1,
              land_conv0, land_conv1]
    return loss_blk, grad_x, landed, small_parts


def kernel(x, a_w_in, a_b_f, a_w_out, b_w_q, b_w_out, kv_norm_g, w_kv, mix_norm_g, ffn_norm_g, ffn_w_up, ffn_conv_w, ffn_conv_b, ffn_w_down, final_norm_g, loss_target, m_a_w_in, m_a_b_f, m_a_w_out, m_b_w_q, m_b_w_out, m_kv_norm_g, m_w_kv, m_mix_norm_g, m_ffn_norm_g, m_ffn_w_up, m_ffn_conv_w, m_ffn_conv_b, m_ffn_w_down, m_final_norm_g, v_a_w_in, v_a_b_f, v_a_w_out, v_b_w_q, v_b_w_out, v_kv_norm_g, v_w_kv, v_mix_norm_g, v_ffn_norm_g, v_ffn_w_up, v_ffn_conv_w, v_ffn_conv_b, v_ffn_w_down, v_final_norm_g):
    def shards(a_w_in, a_w_out, b_w_q, b_w_out, w_kv, ffn_w_up, ffn_w_down, ffn_conv_w):
        return [a_w_in[0], a_w_out[0], b_w_q[0], b_w_out[0], w_kv, ffn_w_up[0], ffn_w_up[1],
                ffn_w_down[0], ffn_w_down[1], ffn_conv_w[0], ffn_conv_w[1]]

    w_loc = shards(a_w_in, a_w_out, b_w_q, b_w_out, w_kv, ffn_w_up, ffn_w_down, ffn_conv_w)
    m_loc = shards(m_a_w_in, m_a_w_out, m_b_w_q, m_b_w_out, m_w_kv, m_ffn_w_up, m_ffn_w_down, m_ffn_conv_w)
    v_loc = shards(v_a_w_in, v_a_w_out, v_b_w_q, v_b_w_out, v_w_kv, v_ffn_w_up, v_ffn_w_down, v_ffn_conv_w)

    (g_in,) = _all_gather([a_w_in[0].astype(BF16)], "gather_a_w_in")
    w_in = _cols_from_slots(g_in)
    w_in_pad = jnp.pad(w_in, ((0, 0), (0, A_PROJ_PAD - w_in.shape[1])))
    late_shards = [a_w_out[0].astype(BF16), b_w_q[0].astype(BF16), b_w_out[0].astype(BF16), w_kv.astype(BF16),
                   ffn_w_up.reshape(2 * D, -1).astype(BF16), ffn_w_down.reshape(-1, D).astype(BF16),
                   ffn_conv_w.reshape(6, -1)]

    loss_blk, grad_x, landed, small_parts = _local_step(
        x[0], loss_target[0], w_in_pad, late_shards,
        a_b_f, kv_norm_g, mix_norm_g, ffn_norm_g, ffn_conv_b, final_norm_g)

    big = [_adamw_sharded(w_loc[k], m_loc[k], v_loc[k], landed[k], f"adamw{k}") for k in range(11)]

    small = _adamw_replicated(
        _pack_small(a_b_f, kv_norm_g, mix_norm_g, ffn_norm_g, ffn_conv_b, final_norm_g),
        _pack_small(m_a_b_f, m_kv_norm_g, m_mix_norm_g, m_ffn_norm_g, m_ffn_conv_b, m_final_norm_g),
        _pack_small(v_a_b_f, v_kv_norm_g, v_mix_norm_g, v_ffn_norm_g, v_ffn_conv_b, v_final_norm_g),
        small_parts, "adamw_small")

    loss = lax.psum(loss_blk[0, 0], ("x", "y", "c"))

    def assemble(kind):
        b = [r[kind] for r in big]
        s_abf, s_kv, s_mix, s_ffn, s_cb, s_fin = _unpack_small(small[kind])
        return [b[0][None], s_abf, b[1][None], b[2][None], b[3][None], s_kv, b[4], s_mix, s_ffn,
                jnp.stack([b[5], b[6]]), jnp.stack([b[9], b[10]]), s_cb, jnp.stack([b[7], b[8]]), s_fin]

    return (loss, grad_x[None], *assemble(0), *assemble(1), *assemble(2), *assemble(3))
```

```python
import functools
import math

import jax
import jax.numpy as jnp
from jax import lax
from jax.experimental import pallas as pl
from jax.experimental.pallas import tpu as pltpu

F32 = jnp.float32
BF16 = jnp.bfloat16

S = 4096
D = 1024
N_DEV = 8
A_HEADS = 16
HEAD_DIM = 64
A_QKV = 3072
A_PROJ_PAD = 3200
B_Q = 1536
B_OUT = 512
B_KV = 3072
B_W = 128
B_DILS = (1, 4, 16)
D_FF = 2816
RMS_EPS = 1e-6
SCALE = HEAD_DIM ** -0.5
NEG = -1e30

ADAM_LR = 0.001
ADAM_B1 = 0.9
ADAM_B2 = 0.999
ADAM_EPS = 1e-08
ADAM_WD = 0.01
ADAM_STEP = 10

LANES = 128
VMEM_LIMIT = 56 * 1024 * 1024
MESH = pl.DeviceIdType.MESH
ANY = pl.BlockSpec(memory_space=pl.ANY)

NT_DIMS = (((1,), (1,)), ((), ()))
TN_DIMS = (((0,), (0,)), ((), ()))
NN_DIMS = (((1,), (0,)), ((), ()))


def _params(*sem):
    return pltpu.CompilerParams(dimension_semantics=sem if sem else None, vmem_limit_bytes=VMEM_LIMIT)


def _dot(a, b, dims=NN_DIMS):
    return lax.dot_general(a, b, dims, preferred_element_type=F32)


def _split_dot(x, mat, pieces):
    out = None
    rem = x
    for _ in range(pieces):
        part = rem.astype(BF16)
        rem = rem - part.astype(F32)
        d = _dot(part, mat)
        out = d if out is None else out + d
    return out


def _pick(n, prefs):
    for p in prefs:
        if n % p == 0:
            return p
    return n


def _gather_phases(ins, outs, sems):
    n = len(ins)
    if n == 0:
        return (lambda: None,) * 3
    send_sems, recv_sems, local_sems = sems
    x, y, c = lax.axis_index("x"), lax.axis_index("y"), lax.axis_index("c")
    me, sibling = (x, y, c), (x, y, 1 - c)
    chips = [(1 - x, y), (x, 1 - y), (1 - x, 1 - y)]

    def slot(a, px, py, pc):
        return outs[a].at[4 * px + 2 * py + pc]

    def copy(a, k, block, to, src=None):
        return pltpu.make_async_remote_copy(
            src_ref=slot(a, *block) if src is None else src, dst_ref=slot(a, *block),
            send_sem=send_sems.at[a, k], recv_sem=recv_sems.at[a, k],
            device_id=to, device_id_type=MESH)

    mine = [pltpu.make_async_copy(ins[a], slot(a, *me), local_sems.at[a]) for a in range(n)]
    first = []
    for a in range(n):
        first.append(copy(a, 0, me, sibling, src=ins[a]))
        first += [copy(a, 1 + j, me, (*chip, c), src=ins[a]) for j, chip in enumerate(chips)]
    passed = [copy(a, 4 + j, (*chip, c), sibling) for j, chip in enumerate(chips) for a in range(n)]

    def start():
        for cp in mine + first:
            cp.start()

    def forward():
        k = 0
        for j, chip in enumerate(chips):
            for a in range(n):
                copy(a, 1 + j, (*chip, c), me).wait_recv()
                passed[k].start()
                k += 1

    def finish():
        for a in range(n):
            copy(a, 0, sibling, me).wait_recv()
            for j, chip in enumerate(chips):
                copy(a, 4 + j, (*chip, 1 - c), me).wait_recv()
        for cp in first + passed:
            cp.wait_send()
        for cp in mine:
            cp.wait()

    return start, forward, finish


def _all_gather(arrays, name):
    n = len(arrays)

    def body(*refs):
        for phase in _gather_phases(refs[:n], refs[n:2 * n], refs[2 * n:]):
            phase()

    return pl.pallas_call(
        body, name=name,
        out_shape=[jax.ShapeDtypeStruct((N_DEV,) + a.shape, a.dtype) for a in arrays],
        in_specs=[ANY] * n, out_specs=[ANY] * n,
        scratch_shapes=[pltpu.SemaphoreType.DMA((n, 7)), pltpu.SemaphoreType.DMA((n, 7)),
                        pltpu.SemaphoreType.DMA((n,))],
    )(*arrays)


PEER_FLIPS = [(dx, dy, dc) for dx in (0, 1) for dy in (0, 1) for dc in (0, 1) if (dx, dy, dc) != (0, 0, 0)]


def _exchange_copies(ins, outs, sems, scatter):
    if not ins:
        return []
    send_sems, recv_sems, local_sems = sems
    x, y, c = lax.axis_index("x"), lax.axis_index("y"), lax.axis_index("c")
    me = 4 * x + 2 * y + c
    copies = []
    for a in range(len(ins)):
        copies.append(pltpu.make_async_copy(ins[a].at[me] if scatter else ins[a], outs[a].at[me], local_sems.at[a]))
        for k, (dx, dy, dc) in enumerate(PEER_FLIPS):
            px, py, pc = (1 - x if dx else x), (1 - y if dy else y), (1 - c if dc else c)
            copies.append(pltpu.make_async_remote_copy(
                src_ref=ins[a].at[4 * px + 2 * py + pc] if scatter else ins[a], dst_ref=outs[a].at[me],
                send_sem=send_sems.at[a, k], recv_sem=recv_sems.at[a, k],
                device_id=(px, py, pc), device_id_type=MESH))
    return copies


def _exchange_scratch(n):
    if n == 0:
        return []
    return [pltpu.SemaphoreType.DMA((n, 7)), pltpu.SemaphoreType.DMA((n, 7)), pltpu.SemaphoreType.DMA((n,))]


def _exchange_shapes(arrays, scatter):
    return [jax.ShapeDtypeStruct((N_DEV,) + (a.shape[1:] if scatter else a.shape), a.dtype) for a in arrays]


def _final_exchange(scatter, gather, name):
    ns, ng = len(scatter), len(gather)

    def body(*refs):
        ins, outs, sems = refs[:ns + ng], refs[ns + ng:2 * (ns + ng)], refs[2 * (ns + ng):]
        n_sems = len(_exchange_scratch(ns))
        copies = (_exchange_copies(ins[:ns], outs[:ns], sems[:n_sems], True)
                  + _exchange_copies(ins[ns:], outs[ns:], sems[n_sems:], False))
        for cp in copies:
            cp.start()
        for cp in copies:
            cp.wait()

    res = pl.pallas_call(
        body, name=name, out_shape=_exchange_shapes(scatter, True) + _exchange_shapes(gather, False),
        in_specs=[ANY] * (ns + ng), out_specs=[ANY] * (ns + ng),
        scratch_shapes=_exchange_scratch(ns) + _exchange_scratch(ng),
    )(*scatter, *gather)
    return res[:ns], res[ns:]


MM_ROWS = 512
MM_COLS = 1024


def _matmul(a, b, *, mode, out_dtype, name, tm, tn, res=None, scatter=()):
    if mode == "nn":
        (M, K), (K2, N) = a.shape, b.shape
    else:
        (M, K), (N, K2) = a.shape, b.shape
    assert K == K2, (a.shape, b.shape, mode)
    tm, tn = min(tm, M), min(tn, N)
    sm = min(tm, MM_ROWS)
    sn = tn if tn <= MM_COLS else _pick(tn, (512, 256, 128))
    assert M % tm == 0 and N % tn == 0 and tm % sm == 0, (M, N, K, tm, tn)
    dims = NN_DIMS if mode == "nn" else NT_DIMS
    a_spec = pl.BlockSpec((tm, K), lambda i, j: (i, 0))
    if mode == "nt":
        b_spec = pl.BlockSpec((tn, K), lambda i, j: (j, 0))
    else:
        b_spec = pl.BlockSpec((K, tn), lambda i, j: (0, j))
    o_spec = pl.BlockSpec((tm, tn), lambda i, j: (i, j))
    has_res = res is not None
    n_in, n_ex = 2 + has_res, len(scatter)
    gm, gn = M // tm, N // tn

    def body(*refs):
        a_ref, b_ref = refs[0], refs[1]
        r_ref = refs[2] if has_res else None
        o_ref = refs[n_in + n_ex]
        exchange = (refs[n_in:n_in + n_ex], refs[n_in + n_ex + 1:n_in + 2 * n_ex + 1], refs[n_in + 2 * n_ex + 1:], True)

        @pl.when(jnp.logical_and(pl.program_id(0) == 0, pl.program_id(1) == 0))
        def _():
            for cp in _exchange_copies(*exchange):
                cp.start()

        def chunk(r, carry):
            rows = pl.ds(pl.multiple_of(r * sm, sm), sm)
            av = a_ref[rows, :]
            for c0 in range(0, tn, sn):
                bv = b_ref[c0:c0 + sn, :] if mode == "nt" else b_ref[:, c0:c0 + sn]
                total = _dot(av, bv, dims)
                if has_res:
                    total = total + r_ref[rows, c0:c0 + sn]
                o_ref[rows, c0:c0 + sn] = total.astype(out_dtype)
            return carry

        lax.fori_loop(0, tm // sm, chunk, 0)

        @pl.when(jnp.logical_and(pl.program_id(0) == gm - 1, pl.program_id(1) == gn - 1))
        def _():
            for cp in _exchange_copies(*exchange):
                cp.wait()

    out = pl.pallas_call(
        body, name=name, grid=(gm, gn),
        out_shape=[jax.ShapeDtypeStruct((M, N), out_dtype)] + _exchange_shapes(scatter, True),
        in_specs=[a_spec, b_spec] + ([o_spec] if has_res else []) + [ANY] * n_ex,
        out_specs=[o_spec] + [ANY] * n_ex,
        scratch_shapes=_exchange_scratch(n_ex),
        compiler_params=_params("arbitrary", "arbitrary"),
    )(*((a, b, res) if has_res else (a, b)), *scatter)
    return out if n_ex else out[0]


def _matmul_tn(a, b, *, out_dtype, name, tk=512, sm=256):
    (K, M), (K2, N) = a.shape, b.shape
    assert K == K2 and K % tk == 0 and M % sm == 0, (a.shape, b.shape)
    nk = K // tk

    def body(a_ref, b_ref, o_ref, acc_ref):
        k = pl.program_id(0)

        @pl.when(k == 0)
        def _():
            acc_ref[...] = jnp.zeros_like(acc_ref)

        def chunk(mi, carry):
            cols = pl.ds(pl.multiple_of(mi * sm, sm), sm)
            acc_ref[cols, :] += _dot(a_ref[:, cols].T, b_ref[...])
            return carry

        lax.fori_loop(0, M // sm, chunk, 0)

        @pl.when(k == nk - 1)
        def _():
            def emit(mi, carry):
                rows = pl.ds(pl.multiple_of(mi * sm, sm), sm)
                o_ref[rows, :] = acc_ref[rows, :].astype(out_dtype)
                return carry
            lax.fori_loop(0, M // sm, emit, 0)

    return pl.pallas_call(
        body, name=name, grid=(nk,),
        out_shape=jax.ShapeDtypeStruct((M, N), out_dtype),
        in_specs=[pl.BlockSpec((tk, M), lambda k: (k, 0)), pl.BlockSpec((tk, N), lambda k: (k, 0))],
        out_specs=pl.BlockSpec((M, N), lambda k: (0, 0)),
        scratch_shapes=[pltpu.VMEM((M, N), F32)],
        compiler_params=_params("arbitrary"),
    )(a, b)


def _rmsnorm_fwd(x, gains, name, tr=256):
    n = len(gains)

    def body(*refs):
        x_ref = refs[0]
        xv = x_ref[...]
        r = lax.rsqrt(jnp.mean(xv * xv, axis=-1, keepdims=True) + RMS_EPS)
        y = xv * r
        for a in range(n):
            refs[1 + n + a][...] = (y * refs[1 + a][...]).astype(BF16)

    row = pl.BlockSpec((tr, D), lambda i: (i, 0))
    gain = pl.BlockSpec((1, D), lambda i: (0, 0))
    return pl.pallas_call(
        body, name=name, grid=(S // tr,),
        out_shape=[jax.ShapeDtypeStruct((S, D), BF16)] * n,
        in_specs=[row] + [gain] * n, out_specs=[row] * n,
        compiler_params=_params("parallel"),
    )(x, *gains)


def _rmsnorm_bwd(x, dy, g, dres, name, tr=256):
    def body(x_ref, dy_ref, g_ref, dres_ref, dx_ref, dxb_ref, dg_ref):
        xv = x_ref[...]
        dyv = dy_ref[...]
        r = lax.rsqrt(jnp.mean(xv * xv, axis=-1, keepdims=True) + RMS_EPS)
        xhat = xv * r
        dxhat = dyv * g_ref[...]
        mean_term = jnp.mean(dxhat * xhat, axis=-1, keepdims=True)
        dx = r * (dxhat - xhat * mean_term) + dres_ref[...]
        dx_ref[...] = dx
        dxb_ref[...] = dx.astype(BF16)
        part = jnp.sum(dyv * xhat, axis=0, keepdims=True)

        @pl.when(pl.program_id(0) == 0)
        def _():
            dg_ref[...] = part

        @pl.when(pl.program_id(0) > 0)
        def _():
            dg_ref[...] += part

    row = pl.BlockSpec((tr, D), lambda i: (i, 0))
    gain = pl.BlockSpec((1, D), lambda i: (0, 0))
    return pl.pallas_call(
        body, name=name, grid=(S // tr,),
        out_shape=[jax.ShapeDtypeStruct((S, D), F32), jax.ShapeDtypeStruct((S, D), BF16),
                   jax.ShapeDtypeStruct((1, D), F32)],
        in_specs=[row, row, gain, row], out_specs=[row, row, gain],
        compiler_params=_params("arbitrary"),
    )(x, dy, g, dres)


def _final_loss(x, target, g, name, tr=256):
    def body(x_ref, t_ref, g_ref, loss_ref, dx_ref, dxb_ref, dg_ref):
        xv = x_ref[...]
        gv = g_ref[...]
        r = lax.rsqrt(jnp.mean(xv * xv, axis=-1, keepdims=True) + RMS_EPS)
        xhat = xv * r
        err = xhat * gv - t_ref[...]
        row_loss = jnp.mean(err * err, axis=-1, keepdims=True)
        lpart = 0.5 * jnp.sum(row_loss, axis=0, keepdims=True)
        dyv = err / D
        dxhat = dyv * gv
        mean_term = jnp.mean(dxhat * xhat, axis=-1, keepdims=True)
        dx = r * (dxhat - xhat * mean_term)
        dx_ref[...] = dx
        dxb_ref[...] = dx.astype(BF16)
        gpart = jnp.sum(dyv * xhat, axis=0, keepdims=True)

        @pl.when(pl.program_id(0) == 0)
        def _():
            dg_ref[...] = gpart
            loss_ref[...] = jnp.broadcast_to(lpart, loss_ref.shape)

        @pl.when(pl.program_id(0) > 0)
        def _():
            dg_ref[...] += gpart
            loss_ref[...] += jnp.broadcast_to(lpart, loss_ref.shape)

    row = pl.BlockSpec((tr, D), lambda i: (i, 0))
    gain = pl.BlockSpec((1, D), lambda i: (0, 0))
    lspec = pl.BlockSpec((8, LANES), lambda i: (0, 0))
    return pl.pallas_call(
        body, name=name, grid=(S // tr,),
        out_shape=[jax.ShapeDtypeStruct((8, LANES), F32), jax.ShapeDtypeStruct((S, D), F32),
                   jax.ShapeDtypeStruct((S, D), BF16), jax.ShapeDtypeStruct((1, D), F32)],
        in_specs=[row, row, gain], out_specs=[lspec, row, row, gain],
        compiler_params=_params("arbitrary"),
    )(x, target, g)


CONV_TR = 128
CONV_TC = D_FF
CONV_NJ = D_FF // CONV_TC
HALO = 16


def _causal_taps(cur_ref, prev_ref, first):
    xv = cur_ref[...].astype(F32)
    pv = prev_ref[...].astype(F32)
    p1 = jnp.where(first, 0.0, pv[HALO - 1:HALO, :])
    p2 = jnp.where(first, 0.0, pv[HALO - 2:HALO - 1, :])
    r1, r2 = pltpu.roll(xv, 1, 0), pltpu.roll(xv, 2, 0)
    row = lax.broadcasted_iota(jnp.int32, (8, xv.shape[1]), 0)
    xm1 = jnp.concatenate([jnp.where(row == 0, p1, r1[0:8]), r1[8:]], axis=0)
    xm2 = jnp.concatenate([jnp.where(row == 0, p2, jnp.where(row == 1, p1, r2[0:8])), r2[8:]], axis=0)
    return xv, xm1, xm2


def _conv_specs():
    def prev_row(i):
        return jnp.maximum(i * (CONV_TR // HALO) - 1, 0)
    ua = pl.BlockSpec((CONV_TR, CONV_TC), lambda i, j: (i, j))
    ug = pl.BlockSpec((CONV_TR, CONV_TC), lambda i, j: (i, j + CONV_NJ))
    pa = pl.BlockSpec((HALO, CONV_TC), lambda i, j: (prev_row(i), j))
    pg = pl.BlockSpec((HALO, CONV_TC), lambda i, j: (prev_row(i), j + CONV_NJ))
    wa = pl.BlockSpec((3, CONV_TC), lambda i, j: (0, j))
    wg = pl.BlockSpec((3, CONV_TC), lambda i, j: (0, j + CONV_NJ))
    ba = pl.BlockSpec((1, CONV_TC), lambda i, j: (0, j))
    bg = pl.BlockSpec((1, CONV_TC), lambda i, j: (0, j + CONV_NJ))
    return [ua, pa, ug, pg, wa, wg, ba, bg]


def _convgate_fwd(u, w, b, name):
    def body(ua, pa, ug, pg, wa, wg, ba, bg, o_ref):
        first = pl.program_id(0) == 0
        x0, x1, x2 = _causal_taps(ua, pa, first)
        ac = wa[0:1, :] * x2 + wa[1:2, :] * x1 + wa[2:3, :] * x0 + ba[...]
        x0, x1, x2 = _causal_taps(ug, pg, first)
        gc = wg[0:1, :] * x2 + wg[1:2, :] * x1 + wg[2:3, :] * x0 + bg[...]
        sg = 1.0 / (1.0 + jnp.exp(-gc))
        o_ref[...] = (gc * sg * ac).astype(BF16)

    return pl.pallas_call(
        body, name=name, grid=(S // CONV_TR, CONV_NJ),
        out_shape=jax.ShapeDtypeStruct((S, D_FF), BF16),
        in_specs=_conv_specs(),
        out_specs=pl.BlockSpec((CONV_TR, CONV_TC), lambda i, j: (i, j)),
        compiler_params=_params("parallel", "parallel"),
    )(u, u, u, u, w, w, b, b)


def _anticausal_conv(d, nxt_ref, w_ref, last):
    n1 = jnp.where(last, 0.0, nxt_ref[0:1, :])
    n2 = jnp.where(last, 0.0, nxt_ref[1:2, :])
    r1, r2 = pltpu.roll(d, CONV_TR - 1, 0), pltpu.roll(d, CONV_TR - 2, 0)
    row = lax.broadcasted_iota(jnp.int32, (8, d.shape[1]), 0)
    cut = CONV_TR - 8
    dp1 = jnp.concatenate([r1[:cut], jnp.where(row == 7, n1, r1[cut:])], axis=0)
    dp2 = jnp.concatenate([r2[:cut], jnp.where(row == 7, n2, jnp.where(row == 6, n1, r2[cut:]))], axis=0)
    return w_ref[2:3, :] * d + w_ref[1:2, :] * dp1 + w_ref[0:1, :] * dp2


def _convgate_bwd(u, w, b, dact, name):
    n_i = S // CONV_TR

    def body(ua, pa, ug, pg, wa, wg, ba, bg, d_ref, dua_ref, dug_ref, dwa_ref, dwg_ref, dba_ref, dbg_ref,
             nxt_a, nxt_g):
        i = pl.program_id(1)
        last = i == 0
        first = i == n_i - 1
        a0, a1, a2 = _causal_taps(ua, pa, first)
        ac = wa[0:1, :] * a2 + wa[1:2, :] * a1 + wa[2:3, :] * a0 + ba[...]
        g0, g1, g2 = _causal_taps(ug, pg, first)
        gc = wg[0:1, :] * g2 + wg[1:2, :] * g1 + wg[2:3, :] * g0 + bg[...]
        sg = 1.0 / (1.0 + jnp.exp(-gc))
        dact_v = d_ref[...].astype(F32)
        da = dact_v * (gc * sg)
        dg = dact_v * ac * (sg * (1.0 + gc * (1.0 - sg)))
        dua_ref[...] = _anticausal_conv(da, nxt_a, wa, last).astype(BF16)
        dug_ref[...] = _anticausal_conv(dg, nxt_g, wg, last).astype(BF16)
        nxt_a[...] = da[0:8]
        nxt_g[...] = dg[0:8]

        def col(v):
            return jnp.sum(v, axis=0, keepdims=True)

        parts = [col(da * a2), col(da * a1), col(da * a0), col(dg * g2), col(dg * g1), col(dg * g0),
                 col(da), col(dg)]

        @pl.when(last)
        def _():
            for k in range(3):
                dwa_ref[k:k + 1, :] = parts[k]
                dwg_ref[k:k + 1, :] = parts[3 + k]
            dba_ref[...] = parts[6]
            dbg_ref[...] = parts[7]

        @pl.when(i > 0)
        def _():
            for k in range(3):
                dwa_ref[k:k + 1, :] += parts[k]
                dwg_ref[k:k + 1, :] += parts[3 + k]
            dba_ref[...] += parts[6]
            dbg_ref[...] += parts[7]

    def swap(spec):
        return pl.BlockSpec(spec.block_shape, lambda j, i, f=spec.index_map: f(n_i - 1 - i, j))

    blk = pl.BlockSpec((CONV_TR, CONV_TC), lambda j, i: (n_i - 1 - i, j))
    w3 = pl.BlockSpec((3, CONV_TC), lambda j, i: (0, j))
    b1 = pl.BlockSpec((1, CONV_TC), lambda j, i: (0, j))
    return pl.pallas_call(
        body, name=name, grid=(CONV_NJ, n_i),
        out_shape=[jax.ShapeDtypeStruct((S, D_FF), BF16), jax.ShapeDtypeStruct((S, D_FF), BF16),
                   jax.ShapeDtypeStruct((3, D_FF), F32), jax.ShapeDtypeStruct((3, D_FF), F32),
                   jax.ShapeDtypeStruct((1, D_FF), F32), jax.ShapeDtypeStruct((1, D_FF), F32)],
        in_specs=[swap(s) for s in _conv_specs()] + [blk],
        out_specs=[blk, blk, w3, w3, b1, b1],
        scratch_shapes=[pltpu.VMEM((8, CONV_TC), F32), pltpu.VMEM((8, CONV_TC), F32)],
        compiler_params=_params("arbitrary", "arbitrary"),
    )(u, u, u, u, w, w, b, b, dact)


FOX_T = 512
FOX_TQ, FOX_TK = 512, 512
FOX_FORWARD_AT = 4
N_PAIRS = A_HEADS // 2


def _lane_masks():
    lane = lax.broadcasted_iota(jnp.int32, (1, LANES), 1)
    return lane, (lane < HEAD_DIM, lane >= HEAD_DIM)


def _fox_prep_fwd(z_t, b, name):
    def body(z_ref, b_ref, c_ref):
        r = lax.broadcasted_iota(jnp.int32, (LANES, LANES), 0)
        cc = lax.broadcasted_iota(jnp.int32, (LANES, LANES), 1)
        upper = (r <= cc).astype(BF16)
        carry = jnp.zeros((A_HEADS, 1), F32)
        for blk in range(S // LANES):
            sl = slice(blk * LANES, (blk + 1) * LANES)
            z = z_ref[:, sl] + b_ref[...]
            lf = jnp.minimum(z, 0.0) - jnp.log(1.0 + jnp.exp(-jnp.abs(z)))
            cs = _split_dot(lf, upper, 3) + carry
            c_ref[:, sl] = cs
            carry = cs[:, LANES - 1:LANES]

    return pl.pallas_call(
        body, name=name, out_shape=jax.ShapeDtypeStruct((A_HEADS, S), F32),
        compiler_params=_params(),
    )(z_t, b)


def _fox_prep_bwd(drow_t, dcol_t, z_t, b, name):
    def body(dr_ref, dc_ref, z_ref, b_ref, dz_ref, db_ref):
        r = lax.broadcasted_iota(jnp.int32, (LANES, LANES), 0)
        cc = lax.broadcasted_iota(jnp.int32, (LANES, LANES), 1)
        lower = (r >= cc).astype(BF16)
        carry = jnp.zeros((A_HEADS, 1), F32)
        db = jnp.zeros((A_HEADS, 1), F32)
        for blk in reversed(range(S // LANES)):
            sl = slice(blk * LANES, (blk + 1) * LANES)
            rc = _split_dot(dr_ref[:, sl] - dc_ref[:, sl], lower, 3) + carry
            carry = rc[:, 0:1]
            z = z_ref[:, sl] + b_ref[...]
            dz = rc / (1.0 + jnp.exp(z))
            dz_ref[:, sl] = dz
            db = db + jnp.sum(dz, axis=1, keepdims=True)
        db_ref[...] = db

    return pl.pallas_call(
        body, name=name,
        out_shape=[jax.ShapeDtypeStruct((A_HEADS, S), F32), jax.ShapeDtypeStruct((A_HEADS, 1), F32)],
        compiler_params=_params(),
    )(drow_t, dcol_t, z_t, b)


def _fox_fwd(qkv, c_t2, name, gather):
    tq, tk = FOX_TQ, FOX_TK

    n = len(gather)

    def body(*refs):
        q_ref, k_ref, v_ref, ct_ref = refs[:4]
        o_ref, lse_ref = refs[4 + n:6 + n]
        s_scr, p_scr, acc_scr = refs[-5:-3], refs[-3:-1], refs[-1]
        qi = pl.program_id(1)

        gather_start, gather_forward, gather_finish = _gather_phases(
            refs[4:4 + n], refs[6 + n:6 + 2 * n], refs[6 + 2 * n:len(refs) - 5])

        @pl.when(jnp.logical_and(pl.program_id(0) == 0, qi == 0))
        def _():
            gather_start()

        @pl.when(jnp.logical_and(pl.program_id(0) == FOX_FORWARD_AT, qi == 0))
        def _():
            gather_forward()

        n_full = jnp.right_shift(qi, (tk // tq).bit_length() - 1)
        lane, masks = _lane_masks()
        q = q_ref[...] * SCALE
        qs = [jnp.where(masks[e], q, jnp.zeros_like(q)) for e in range(2)]

        def scores_into(j, slot):
            start = pl.multiple_of(j * tk, tk)
            kb = k_ref[pl.ds(start, tk), :]
            for e in range(2):
                s_scr[slot][e] = _dot(qs[e], kb, NT_DIMS) - ct_ref[e:e + 1, pl.ds(start, tk)]

        def softmax_of(slot, m, masked):
            m_new, alpha = [], []
            for e in range(2):
                s = s_scr[slot][e]
                if masked:
                    rows = lax.broadcasted_iota(jnp.int32, (tq, tk), 0) + (qi * tq - n_full * tk)
                    cols = lax.broadcasted_iota(jnp.int32, (tq, tk), 1)
                    s = jnp.where(cols <= rows, s, NEG)
                m_new.append(jnp.maximum(m[e], jnp.max(s, axis=1, keepdims=True)))
                p_scr[slot][e] = jnp.exp(s - m_new[e]).astype(BF16)
                alpha.append(jnp.exp(m[e] - m_new[e]))
            return tuple(m_new), tuple(alpha)

        def values_of(j, slot, alpha):
            start = pl.multiple_of(j * tk, tk)
            vb = v_ref[pl.ds(start, tk), :]
            for e in range(2):
                acc_scr[e] = (alpha[e] * acc_scr[e]
                              + _dot(p_scr[slot][e], jnp.where(masks[e], vb, jnp.ones_like(vb))))

        def stage(j, cur, nxt, carry):
            m, a_prev = carry
            scores_into(j + 1, nxt)
            values_of(jnp.maximum(j - 1, 0), nxt, a_prev)
            return softmax_of(cur, m, False)

        def finish(cur, nxt, carry):
            m, a_prev = carry
            values_of(jnp.maximum(n_full - 1, 0), nxt, a_prev)
            (m0, m1), alpha = softmax_of(cur, m, True)
            values_of(n_full, cur, alpha)
            l0 = acc_scr[0][:, HEAD_DIM:HEAD_DIM + 1]
            l1 = acc_scr[1][:, 0:1]
            o_ref[...] = jnp.where(masks[0], acc_scr[0] / l0, acc_scr[1] / l1).astype(BF16)
            lse_ref[...] = jnp.where(masks[0], m0 + jnp.log(l0), m1 + jnp.log(l1))

        scores_into(0, 0)
        for e in range(2):
            p_scr[1][e] = jnp.zeros((tq, tk), BF16)
            acc_scr[e] = jnp.zeros((tq, LANES), F32)
        two = lambda x: (x, x)
        init = (two(jnp.full((tq, 1), NEG, F32)), two(jnp.ones((tq, 1), F32)))

        def two_stages(jj, carry):
            return stage(2 * jj + 1, 1, 0, stage(2 * jj, 0, 1, carry))

        carry = lax.fori_loop(0, jnp.right_shift(n_full, 1), two_stages, init)
        odd = jnp.bitwise_and(n_full, 1) == 1

        @pl.when(odd)
        def _():
            finish(1, 0, stage(n_full - 1, 0, 1, carry))

        @pl.when(jnp.logical_not(odd))
        def _():
            finish(0, 1, carry)

        @pl.when(jnp.logical_and(pl.program_id(0) == N_PAIRS - 1, qi == S // tq - 1))
        def _():
            gather_finish()

    qspec = pl.BlockSpec((tq, LANES), lambda h, i: (i, h))
    return pl.pallas_call(
        body, name=name, grid=(N_PAIRS, S // tq),
        out_shape=[jax.ShapeDtypeStruct((S, D), BF16), jax.ShapeDtypeStruct((S, D), F32)]
        + _exchange_shapes(gather, False),
        in_specs=[qspec,
                  pl.BlockSpec((S, LANES), lambda h, i: (0, N_PAIRS + h)),
                  pl.BlockSpec((S, LANES), lambda h, i: (0, 2 * N_PAIRS + h)),
                  pl.BlockSpec((None, 2, S), lambda h, i: (h, 0, 0))] + [ANY] * n,
        out_specs=[qspec, qspec] + [ANY] * n,
        scratch_shapes=_exchange_scratch(n) + [
            pltpu.VMEM((2, tq, tk), F32), pltpu.VMEM((2, tq, tk), F32),
            pltpu.VMEM((2, tq, tk), BF16), pltpu.VMEM((2, tq, tk), BF16),
            pltpu.VMEM((2, tq, LANES), F32)],
        compiler_params=_params("arbitrary", "arbitrary"),
    )(qkv, qkv, qkv, c_t2, *gather)


def _head_rowsum(a, b, name, tr=256):
    C = a.shape[1]

    def body(a_ref, b_ref, o_ref):
        r = lax.broadcasted_iota(jnp.int32, (LANES, LANES), 0) < HEAD_DIM
        cc = lax.broadcasted_iota(jnp.int32, (LANES, LANES), 1) < HEAD_DIM
        same_head = (r == cc).astype(BF16)
        for blk in range(C // LANES):
            sl = slice(blk * LANES, (blk + 1) * LANES)
            prod = a_ref[:, sl].astype(F32) * b_ref[:, sl].astype(F32)
            o_ref[:, sl] = _split_dot(prod, same_head, 2)

    row = pl.BlockSpec((tr, C), lambda i: (i, 0))
    return pl.pallas_call(
        body, name=name, grid=(S // tr,), out_shape=jax.ShapeDtypeStruct((S, C), F32),
        in_specs=[row, row], out_specs=row, compiler_params=_params("parallel"),
    )(a, b)


def _fox_bwd(qkv, do, lse, delta, c_t2, name, scatter):
    t = FOX_T
    nq = S // t

    n = len(scatter)

    def body(*refs):
        q_ref, k_ref, v_ref, do_ref, lse_ref, dl_ref, ct_ref = refs[:7]
        dq_ref, dk_ref, dv_ref, dcol_ref, drow_ref = refs[7 + n:12 + n]
        exchange = (refs[7:7 + n], refs[12 + n:12 + 2 * n], refs[12 + 2 * n:len(refs) - 5], True)
        sd_scr, pd_scr, acc_scr = refs[-5:-3], refs[-3:-1], refs[-1]
        kj = pl.program_id(1)

        @pl.when(jnp.logical_and(pl.program_id(0) == 0, kj == 0))
        def _():
            for cp in _exchange_copies(*exchange):
                cp.start()

        @pl.when(kj == 0)
        def _():
            dq_ref[...] = jnp.zeros_like(dq_ref)
            drow_ref[...] = jnp.zeros_like(drow_ref)

        lane, masks = _lane_masks()
        k = k_ref[...]
        v = v_ref[...]
        k_aug = [jnp.where(masks[e], k * SCALE, jnp.ones_like(k)) for e in range(2)]
        cs = [ct_ref[e:e + 1, :] for e in range(2)]

        def rows_of(i):
            r0 = pl.multiple_of(i * t, t)
            return pl.ds(r0, t), q_ref[pl.ds(r0, t), :] * SCALE, do_ref[pl.ds(r0, t), :]

        def scores_into(i, slot):
            _, qb, dob = rows_of(i)
            for e in range(2):
                qe = jnp.where(masks[e], qb, jnp.zeros_like(qb))
                doe = jnp.where(masks[e], dob, jnp.zeros_like(dob))
                sd_scr[slot][2 * e] = _dot(qe, k, NT_DIMS) - cs[e]
                sd_scr[slot][2 * e + 1] = _dot(doe, v, NT_DIMS)

        def pointwise(i, slot, masked):
            rows, _, _ = rows_of(i)
            for e in range(2):
                lo = e * HEAD_DIM
                s = sd_scr[slot][2 * e]
                if masked:
                    r = lax.broadcasted_iota(jnp.int32, (t, t), 0)
                    c = lax.broadcasted_iota(jnp.int32, (t, t), 1)
                    s = jnp.where(c <= r, s, NEG)
                p = jnp.exp(s - lse_ref[rows, lo:lo + 1])
                pd_scr[slot][2 * e] = p.astype(BF16)
                pd_scr[slot][2 * e + 1] = (p * (sd_scr[slot][2 * e + 1] - dl_ref[rows, lo:lo + 1])).astype(BF16)

        def accumulate(i, slot):
            rows, qb, dob = rows_of(i)
            dq_parts = []
            for e in range(2):
                p, ds = pd_scr[slot][2 * e], pd_scr[slot][2 * e + 1]
                q_aug = jnp.where(masks[e], qb, jnp.ones_like(qb))
                doe = jnp.where(masks[e], dob, jnp.zeros_like(dob))
                acc_scr[2] += _dot(p, doe, TN_DIMS)
                acc_scr[e] += _dot(ds, q_aug, TN_DIMS)
                dq_parts.append(_dot(ds, k_aug[e]))
            dq_ref[rows, :] += jnp.where(masks[0], dq_parts[0], dq_parts[1])
            drow_ref[rows, :] += jnp.where(masks[0], dq_parts[1], dq_parts[0])

        def stage(i, cur, nxt):
            scores_into(jnp.minimum(i + 1, nq - 1), nxt)
            accumulate(i - 1, nxt)
            pointwise(i, cur, False)

        acc_scr[...] = jnp.zeros_like(acc_scr)
        scores_into(kj, 0)
        pointwise(kj, 0, True)
        scores_into(jnp.minimum(kj + 1, nq - 1), 1)
        rest = nq - 1 - kj

        def two_stages(jj, carry):
            stage(kj + 1 + 2 * jj, 1, 0)
            stage(kj + 2 + 2 * jj, 0, 1)
            return carry

        lax.fori_loop(0, jnp.right_shift(rest, 1), two_stages, 0)
        odd = jnp.bitwise_and(rest, 1) == 1

        @pl.when(odd)
        def _():
            stage(nq - 1, 1, 0)
            accumulate(nq - 1, 1)

        @pl.when(jnp.logical_not(odd))
        def _():
            accumulate(nq - 1, 0)

        dk0, dk1, dv = acc_scr[0], acc_scr[1], acc_scr[2]
        dk_ref[...] = jnp.where(masks[0], dk0, dk1).astype(BF16)
        dcol_ref[...] = jnp.where(masks[0], dk1, dk0)
        dv_ref[...] = dv.astype(BF16)

        @pl.when(jnp.logical_and(pl.program_id(0) == N_PAIRS - 1, kj == nq - 1))
        def _():
            for cp in _exchange_copies(*exchange):
                cp.wait()

    full = lambda off: pl.BlockSpec((S, LANES), lambda h, j, off=off: (0, off + h))
    kv = lambda off: pl.BlockSpec((t, LANES), lambda h, j, off=off: (j, off + h))
    return pl.pallas_call(
        body, name=name, grid=(N_PAIRS, nq),
        out_shape=[jax.ShapeDtypeStruct((S, D), F32), jax.ShapeDtypeStruct((S, D), BF16),
                   jax.ShapeDtypeStruct((S, D), BF16), jax.ShapeDtypeStruct((S, D), F32),
                   jax.ShapeDtypeStruct((S, D), F32)] + _exchange_shapes(scatter, True),
        in_specs=[full(0), kv(N_PAIRS), kv(2 * N_PAIRS), full(0), full(0), full(0),
                  pl.BlockSpec((None, 2, t), lambda h, j: (h, 0, j))] + [ANY] * n,
        out_specs=[full(0), kv(0), kv(0), kv(0), full(0)] + [ANY] * n,
        scratch_shapes=_exchange_scratch(n) + [
            pltpu.VMEM((4, t, t), F32), pltpu.VMEM((4, t, t), F32),
            pltpu.VMEM((4, t, t), BF16), pltpu.VMEM((4, t, t), BF16),
            pltpu.VMEM((3, t, LANES), F32)],
        compiler_params=_params("arbitrary", "arbitrary"),
    )(qkv, qkv, qkv, do, lse, delta, c_t2, *scatter)


B_PAIRS = 4
B_NB = S // B_W


def _group_consts(g):
    nbs = jnp.where(g == 0, B_NB // B_DILS[0], jnp.where(g == 1, B_NB // B_DILS[1], B_NB // B_DILS[2]))
    dil = jnp.where(g == 0, B_DILS[0], jnp.where(g == 1, B_DILS[1], B_DILS[2]))
    return nbs, dil


def _band(dil):
    qi = lax.broadcasted_iota(jnp.int32, (B_W, B_W), 0)
    kj = lax.broadcasted_iota(jnp.int32, (B_W, B_W), 1)
    dist_c = qi - kj
    dist_p = qi + B_W - kj
    return (dist_c * dil).astype(F32), dist_c >= 0, (dist_p * dil).astype(F32), dist_p <= B_W


def _dil_fwd(qp, kp, vp, slopes, name):
    def body(sl_ref, q_ref, kp_ref, kc_ref, vp_ref, vc_ref, o_ref, lse_ref):
        g, n = pl.program_id(0), pl.program_id(1)
        nbs, dil = _group_consts(g)
        has_prev = (n % nbs) != 0
        lane, masks = _lane_masks()
        bias_c, ok_c, bias_p, ok_p = _band(dil)
        ok_p = jnp.logical_and(ok_p, has_prev)
        heads = [(hp, e) for hp in range(B_PAIRS) for e in range(2)]
        col = lambda ref, hp: ref[:, hp * LANES:(hp + 1) * LANES]
        logits = []
        for hp, e in heads:
            q = col(q_ref, hp) * SCALE
            qe = jnp.where(masks[e], q, jnp.zeros_like(q))
            logits.append((_dot(qe, col(kc_ref, hp), NT_DIMS), _dot(qe, col(kp_ref, hp), NT_DIMS)))
        probs = []
        for (hp, e), (sc, sp) in zip(heads, logits):
            slope = sl_ref[g * 8 + 2 * hp + e]
            sc = jnp.where(ok_c, sc - slope * bias_c, NEG)
            sp = jnp.where(ok_p, sp - slope * bias_p, NEG)
            m = jnp.maximum(jnp.max(sc, axis=1, keepdims=True), jnp.max(sp, axis=1, keepdims=True))
            probs.append((jnp.exp(sc - m).astype(BF16), jnp.exp(sp - m).astype(BF16), m))
        outs, lses = [], []
        for (hp, e), (pc, pp, m) in zip(heads, probs):
            vc, vpv = col(vc_ref, hp), col(vp_ref, hp)
            acc = (_dot(pc, jnp.where(masks[e], vc, jnp.ones_like(vc)))
                   + _dot(pp, jnp.where(masks[e], vpv, jnp.ones_like(vpv))))
            l = acc[:, HEAD_DIM:HEAD_DIM + 1] if e == 0 else acc[:, 0:1]
            outs.append(acc / l)
            lses.append(m + jnp.log(l))
        o_ref[...] = jnp.concatenate(
            [jnp.where(masks[0], outs[2 * hp], outs[2 * hp + 1]) for hp in range(B_PAIRS)], axis=1).astype(BF16)
        lse = jnp.zeros((B_W, LANES), F32)
        for h in range(2 * B_PAIRS):
            lse = jnp.where(lane == h, lses[h], lse)
        lse_ref[...] = lse

    cur = pl.BlockSpec((None, B_W, B_OUT), lambda g, n, sl: (g, n, 0))
    prev = pl.BlockSpec((None, B_W, B_OUT), lambda g, n, sl: (g, jnp.maximum(n - 1, 0), 0))
    stat = pl.BlockSpec((None, B_W, LANES), lambda g, n, sl: (g, n, 0))
    return pl.pallas_call(
        body, name=name,
        grid_spec=pltpu.PrefetchScalarGridSpec(
            num_scalar_prefetch=1, grid=(3, B_NB),
            in_specs=[cur, prev, cur, prev, cur], out_specs=[cur, stat]),
        out_shape=[jax.ShapeDtypeStruct((3, S, B_OUT), BF16), jax.ShapeDtypeStruct((3, S, LANES), F32)],
        compiler_params=_params("parallel", "parallel"),
    )(slopes, qp, kp, kp, vp, vp)


def _head_expander():
    r = lax.broadcasted_iota(jnp.int32, (LANES, B_OUT), 0)
    c = lax.broadcasted_iota(jnp.int32, (LANES, B_OUT), 1)
    return jnp.logical_and(c >= r * HEAD_DIM, c < (r + 1) * HEAD_DIM).astype(BF16)


def _dil_merge(og, lseg, name, tr=256):
    def body(o_ref, l_ref, out_ref, lse_ref):
        l0, l1, l2 = l_ref[0], l_ref[1], l_ref[2]
        m = jnp.maximum(jnp.maximum(l0, l1), l2)
        w0, w1, w2 = jnp.exp(l0 - m), jnp.exp(l1 - m), jnp.exp(l2 - m)
        den = w0 + w1 + w2
        lse_ref[...] = m + jnp.log(den)
        expand = _head_expander()
        out = None
        for g, w in enumerate((w0, w1, w2)):
            part = _split_dot(w / den, expand, 3) * o_ref[g].astype(F32)
            out = part if out is None else out + part
        out_ref[...] = out.astype(BF16)

    blk3 = pl.BlockSpec((3, tr, B_OUT), lambda i: (0, i, 0))
    stat3 = pl.BlockSpec((3, tr, LANES), lambda i: (0, i, 0))
    blk = pl.BlockSpec((tr, B_OUT), lambda i: (i, 0))
    stat = pl.BlockSpec((tr, LANES), lambda i: (i, 0))
    return pl.pallas_call(
        body, name=name, grid=(S // tr,),
        out_shape=[jax.ShapeDtypeStruct((S, B_OUT), BF16), jax.ShapeDtypeStruct((S, LANES), F32)],
        in_specs=[blk3, stat3], out_specs=[blk, stat], compiler_params=_params("parallel"),
    )(og, lseg)


def _head_rowsum_compact(a, b, name, tr=256):
    def body(a_ref, b_ref, o_ref):
        r = lax.broadcasted_iota(jnp.int32, (B_OUT, LANES), 0)
        c = lax.broadcasted_iota(jnp.int32, (B_OUT, LANES), 1)
        collect = jnp.logical_and(r >= c * HEAD_DIM, r < (c + 1) * HEAD_DIM).astype(BF16)
        prod = a_ref[...].astype(F32) * b_ref[...].astype(F32)
        o_ref[...] = _split_dot(prod, collect, 2)

    row = pl.BlockSpec((tr, B_OUT), lambda i: (i, 0))
    return pl.pallas_call(
        body, name=name, grid=(S // tr,), out_shape=jax.ShapeDtypeStruct((S, LANES), F32),
        in_specs=[row, row], out_specs=pl.BlockSpec((tr, LANES), lambda i: (i, 0)),
        compiler_params=_params("parallel"),
    )(a, b)


def _dil_bwd(qp, kp, vp, dop, lsep, dlp, slopes, name, scatter):
    n_ex = len(scatter)

    def body(sl_ref, *refs):
        (qc_ref, qn_ref, kp_ref, kc_ref, vp_ref, vc_ref, doc_ref, don_ref,
         lc_ref, ln_ref, dc_ref, dn_ref) = refs[:12]
        dq_ref, dk_ref, dv_ref = refs[12 + n_ex:15 + n_ex]
        exchange = (refs[12:12 + n_ex], refs[15 + n_ex:15 + 2 * n_ex], refs[15 + 2 * n_ex:], True)
        g, n = pl.program_id(0), pl.program_id(1)

        @pl.when(jnp.logical_and(g == 0, n == 0))
        def _():
            for cp in _exchange_copies(*exchange):
                cp.start()

        nbs, dil = _group_consts(g)
        has_prev = (n % nbs) != 0
        has_next = jnp.logical_and(n + 1 < B_NB, ((n + 1) % nbs) != 0)
        lane, masks = _lane_masks()
        bias_c, ok_c, bias_p, ok_p = _band(dil)
        ok_pp = jnp.logical_and(ok_p, has_prev)
        ok_np = jnp.logical_and(ok_p, has_next)
        heads = [(hp, e) for hp in range(B_PAIRS) for e in range(2)]
        col = lambda ref, hp: ref[:, hp * LANES:(hp + 1) * LANES]
        mask = lambda t, e: jnp.where(masks[e], t, jnp.zeros_like(t))
        raw = []
        for hp, e in heads:
            qce, qne = mask(col(qc_ref, hp) * SCALE, e), mask(col(qn_ref, hp) * SCALE, e)
            doce, done = mask(col(doc_ref, hp), e), mask(col(don_ref, hp), e)
            kc, kpv, vc, vpv = col(kc_ref, hp), col(kp_ref, hp), col(vc_ref, hp), col(vp_ref, hp)
            raw.append(((_dot(qce, kc, NT_DIMS), _dot(doce, vc, NT_DIMS)),
                        (_dot(qce, kpv, NT_DIMS), _dot(doce, vpv, NT_DIMS)),
                        (_dot(qne, kc, NT_DIMS), _dot(done, vc, NT_DIMS))))
        pds = []
        for (hp, e), tiles in zip(heads, raw):
            lo = 2 * hp + e
            slope = sl_ref[g * 8 + 2 * hp + e]
            lse_c, dl_c = lc_ref[:, lo:lo + 1], dc_ref[:, lo:lo + 1]
            lse_n, dl_n = ln_ref[:, lo:lo + 1], dn_ref[:, lo:lo + 1]
            out = []
            for (s, dp), ok, bias, lse, dl in ((tiles[0], ok_c, bias_c, lse_c, dl_c),
                                               (tiles[1], ok_pp, bias_p, lse_c, dl_c),
                                               (tiles[2], ok_np, bias_p, lse_n, dl_n)):
                p = jnp.exp(jnp.where(ok, s - slope * bias, NEG) - lse)
                out.append((p.astype(BF16), (p * (dp - dl)).astype(BF16)))
            pds.append(out)
        dq_all, dk_all, dv_all = [], [], []
        for hp in range(B_PAIRS):
            dq = jnp.zeros((B_W, LANES), F32)
            dk = jnp.zeros((B_W, LANES), F32)
            dv = jnp.zeros((B_W, LANES), F32)
            for e in range(2):
                (p_c, ds_c), (_, ds_p), (p_n, ds_n) = pds[2 * hp + e]
                qce, qne = mask(col(qc_ref, hp) * SCALE, e), mask(col(qn_ref, hp) * SCALE, e)
                doce, done = mask(col(doc_ref, hp), e), mask(col(don_ref, hp), e)
                dq = dq + _dot(ds_c, mask(col(kc_ref, hp) * SCALE, e)) + _dot(ds_p, mask(col(kp_ref, hp) * SCALE, e))
                dk = dk + _dot(ds_c, qce, TN_DIMS) + _dot(ds_n, qne, TN_DIMS)
                dv = dv + _dot(p_c, doce, TN_DIMS) + _dot(p_n, done, TN_DIMS)
            dq_all.append(dq)
            dk_all.append(dk)
            dv_all.append(dv)
        dq_ref[...] = jnp.concatenate(dq_all, axis=1).astype(BF16)
        dk_ref[...] = jnp.concatenate(dk_all, axis=1).astype(BF16)
        dv_ref[...] = jnp.concatenate(dv_all, axis=1).astype(BF16)

        @pl.when(jnp.logical_and(g == 2, n == B_NB - 1))
        def _():
            for cp in _exchange_copies(*exchange):
                cp.wait()

    cur = pl.BlockSpec((None, B_W, B_OUT), lambda g, n, sl: (g, n, 0))
    prev = pl.BlockSpec((None, B_W, B_OUT), lambda g, n, sl: (g, jnp.maximum(n - 1, 0), 0))
    nxt = pl.BlockSpec((None, B_W, B_OUT), lambda g, n, sl: (g, jnp.minimum(n + 1, B_NB - 1), 0))
    stat_cur = pl.BlockSpec((None, B_W, LANES), lambda g, n, sl: (g, n, 0))
    stat_nxt = pl.BlockSpec((None, B_W, LANES), lambda g, n, sl: (g, jnp.minimum(n + 1, B_NB - 1), 0))
    return pl.pallas_call(
        body, name=name,
        grid_spec=pltpu.PrefetchScalarGridSpec(
            num_scalar_prefetch=1, grid=(3, B_NB),
            in_specs=[cur, nxt, prev, cur, prev, cur, cur, nxt, stat_cur, stat_nxt, stat_cur, stat_nxt]
            + [ANY] * n_ex,
            out_specs=[cur, cur, cur] + [ANY] * n_ex,
            scratch_shapes=_exchange_scratch(n_ex)),
        out_shape=[jax.ShapeDtypeStruct((3, S, B_OUT), BF16)] * 3 + _exchange_shapes(scatter, True),
        compiler_params=_params("arbitrary", "arbitrary"),
    )(slopes, qp, qp, kp, kp, vp, vp, dop, dop, lsep, lsep, dlp, dlp, *scatter)


def _rows_block(shape, max_bytes=2 * 1024 * 1024):
    rows, cols = shape
    padded_cols = -(-cols // LANES) * LANES
    for tr in (1024, 512, 256, 128, 64, 32, 16):
        if rows % tr == 0 and tr * padded_cols * 4 <= max_bytes:
            return tr
    return rows


def _adam_update(w, m, v, g):
    m_new = ADAM_B1 * m + (1.0 - ADAM_B1) * g
    v_new = ADAM_B2 * v + (1.0 - ADAM_B2) * (g * g)
    m_hat = m_new / (1.0 - ADAM_B1 ** ADAM_STEP)
    v_hat = v_new / (1.0 - ADAM_B2 ** ADAM_STEP)
    delta = -ADAM_LR * (m_hat / (jnp.sqrt(v_hat) + ADAM_EPS) + ADAM_WD * w)
    return delta, m_new, v_new


def _adamw_sharded(w, m, v, parts, name):
    R, C = w.shape
    tr = _rows_block((R, C), max_bytes=1024 * 1024)

    def body(w_ref, m_ref, v_ref, p_ref, g_ref, d_ref, mo_ref, vo_ref):
        g = p_ref[0].astype(F32)
        for dev in range(1, N_DEV):
            g = g + p_ref[dev].astype(F32)
        g_ref[...] = g
        d_ref[...], mo_ref[...], vo_ref[...] = _adam_update(w_ref[...], m_ref[...], v_ref[...], g)

    blk = pl.BlockSpec((tr, C), lambda i: (i, 0))
    out = jax.ShapeDtypeStruct((R, C), F32)
    return pl.pallas_call(
        body, name=name, grid=(R // tr,),
        in_specs=[blk, blk, blk, pl.BlockSpec((N_DEV, tr, C), lambda i: (0, i, 0))],
        out_specs=[blk, blk, blk, blk], out_shape=[out, out, out, out],
        compiler_params=_params("parallel"),
    )(w, m, v, parts)


def _adamw_replicated(w, m, v, parts, name):
    def body(w_ref, m_ref, v_ref, p_ref, g_ref, d_ref, mo_ref, vo_ref):
        g = p_ref[0]
        for dev in range(1, N_DEV):
            g = g + p_ref[dev]
        g_ref[...] = g
        d_ref[...], mo_ref[...], vo_ref[...] = _adam_update(w_ref[...], m_ref[...], v_ref[...], g)

    out = jax.ShapeDtypeStruct(w.shape, F32)
    return pl.pallas_call(body, name=name, out_shape=[out, out, out, out], compiler_params=_params())(w, m, v, parts)


def _cols_from_slots(g):
    return g.transpose(1, 0, 2).reshape(g.shape[1], N_DEV * g.shape[2])


def _cols_to_slots(w):
    k, n = w.shape
    return w.reshape(k, N_DEV, n // N_DEV).transpose(1, 0, 2)


def _permute(t, dil):
    c = t.shape[1]
    return t.reshape(S // dil, dil, c).transpose(1, 0, 2).reshape(S, c)


def _unpermute(t, dil):
    c = t.shape[1]
    return t.reshape(dil, S // dil, c).transpose(1, 0, 2).reshape(S, c)


def _group_permute(t):
    return jnp.stack([_permute(t[:, g * B_OUT:(g + 1) * B_OUT], B_DILS[g]) for g in range(3)])


def _same_permute(t):
    return jnp.stack([_permute(t, d) for d in B_DILS])


def _group_unpermute(t):
    return jnp.stack([_unpermute(t[g], B_DILS[g]) for g in range(3)])


SMALL_ROWS = 144


def _pack_small(a_b_f, kv_g, mix_g, ffn_g, conv_b, fin_g):
    flat = jnp.concatenate([a_b_f.reshape(-1), kv_g.reshape(-1), mix_g.reshape(-1), ffn_g.reshape(-1),
                            conv_b.reshape(-1), fin_g.reshape(-1)])
    return jnp.pad(flat, (0, SMALL_ROWS * LANES - flat.shape[0])).reshape(SMALL_ROWS, LANES)


def _unpack_small(p):
    flat = p.reshape(-1)
    out, off = [], 0
    for shape in ((1, A_HEADS), (D,), (2, D), (2, D), (2, 2 * D_FF), (D,)):
        size = math.prod(shape)
        out.append(flat[off:off + size].reshape(shape))
        off += size
    return out


def _unpack_late(g):
    w_up = g[4].reshape(N_DEV, 2, D, -1).transpose(1, 2, 0, 3).reshape(2, D, 2 * D_FF)
    w_down = g[5].reshape(N_DEV, 2, -1, D).transpose(1, 0, 2, 3).reshape(2, D_FF, D)
    conv_w = g[6].reshape(N_DEV, 2, 3, -1).transpose(1, 2, 0, 3).reshape(2, 3, 2 * D_FF)
    return (g[0].reshape(D, D), _cols_from_slots(g[1]), _cols_from_slots(g[2]), _cols_from_slots(g[3]),
            w_up, w_down, conv_w)


def _ffn_slots(dw_up, dw_down, dconv_w):
    return [_cols_to_slots(dw_up), dw_down.reshape(N_DEV, -1, D), _cols_to_slots(dconv_w)]


def _local_step(x0, target, w_in_pad, late_shards,
                a_b_f, kv_norm_g, mix_norm_g, ffn_norm_g, ffn_conv_b, final_norm_g):
    w_qkv, w_f = w_in_pad[:, :A_QKV], w_in_pad[:, A_QKV:]
    conv_b = ffn_conv_b.reshape(2, 1, 2 * D_FF)
    slopes = jnp.exp2(-8.0 * jnp.arange(1, 25, dtype=F32) / 24)

    def gain(g):
        return g.reshape(1, D)

    (h1,) = _rmsnorm_fwd(x0, [gain(mix_norm_g[0])], "norm_mix0")
    qkv = _matmul(h1, w_qkv, mode="nn", out_dtype=BF16, name="proj_qkv", tm=512, tn=A_QKV)
    z = _matmul(h1, w_f, mode="nn", out_dtype=F32, name="proj_gate", tm=S, tn=LANES)
    z_t = z[:, :A_HEADS].T
    b_f = a_b_f.reshape(A_HEADS, 1)
    c_t = _fox_prep_fwd(z_t, b_f, "fox_prep")
    c_t2 = c_t.reshape(N_PAIRS, 2, S)
    o_a, lse_a, *late = _fox_fwd(qkv, c_t2, "fox_fwd", late_shards)
    w_out, w_q, w_bo, w_kvf, w_up, w_down, conv_w = _unpack_late(late)
    x1 = _matmul(o_a, w_out, mode="nn", out_dtype=F32, name="a_out", tm=512, tn=D, res=x0)

    def ffn_fwd(xin, layer):
        (h,) = _rmsnorm_fwd(xin, [gain(ffn_norm_g[layer])], f"norm_ffn{layer}")
        u = _matmul(h, w_up[layer], mode="nn", out_dtype=BF16, name=f"ffn_up{layer}", tm=512, tn=2 * D_FF)
        act = _convgate_fwd(u, conv_w[layer], conv_b[layer], f"convgate{layer}")
        xout = _matmul(act, w_down[layer], mode="nn", out_dtype=F32, name=f"ffn_down{layer}", tm=512, tn=D, res=xin)
        return h, u, act, xout

    h2, u0, act0, x2 = ffn_fwd(x1, 0)
    hk, h3 = _rmsnorm_fwd(x2, [gain(kv_norm_g), gain(mix_norm_g[1])], "norm_kv_mix1")
    kv = _matmul(hk, w_kvf, mode="nn", out_dtype=BF16, name="proj_kv", tm=512, tn=B_KV)
    qb = _matmul(h3, w_q, mode="nn", out_dtype=BF16, name="proj_qb", tm=512, tn=B_Q)
    qp, kp, vp = _group_permute(qb), _group_permute(kv[:, :B_Q]), _group_permute(kv[:, B_Q:])
    og_p, lseg_p = _dil_fwd(qp, kp, vp, slopes, "dil_fwd")
    o_b, lse_b = _dil_merge(_group_unpermute(og_p), _group_unpermute(lseg_p), "dil_merge")
    x3 = _matmul(o_b, w_bo, mode="nn", out_dtype=F32, name="b_out", tm=512, tn=D, res=x2)
    h4, u1, act1, x4 = ffn_fwd(x3, 1)
    loss_blk, dx4, dx4b, dg_final = _final_loss(x4, target, gain(final_norm_g), "final_loss")

    def ffn_bwd(dx, dxb, xin, h, u, act, layer):
        dact = _matmul(dxb, w_down[layer], mode="nt", out_dtype=BF16, name=f"d_act{layer}", tm=512, tn=D_FF)
        dw_down = _matmul_tn(act, dxb, out_dtype=BF16, name=f"dw_down{layer}")
        du_a, du_g, dwa, dwg, dba, dbg = _convgate_bwd(u, conv_w[layer], conv_b[layer], dact, f"convgate_bwd{layer}")
        dw_up = jnp.concatenate(
            [_matmul_tn(h, du_a, out_dtype=BF16, name=f"dw_up_a{layer}"),
             _matmul_tn(h, du_g, out_dtype=BF16, name=f"dw_up_g{layer}")], axis=1)
        dh = _matmul(du_a, w_up[layer][:, :D_FF], mode="nt", out_dtype=F32, name=f"dh_ffn_a{layer}", tm=512, tn=D)
        dh = _matmul(du_g, w_up[layer][:, D_FF:], mode="nt", out_dtype=F32, name=f"dh_ffn_g{layer}", tm=512, tn=D,
                     res=dh)
        dxin, dxinb, dgain = _rmsnorm_bwd(xin, dh, gain(ffn_norm_g[layer]), dx, f"norm_ffn_bwd{layer}")
        dconv_w = jnp.concatenate([dwa, dwg], axis=1)
        dconv_b = jnp.concatenate([dba, dbg], axis=1)
        return dxin, dxinb, dgain, dw_up, dw_down, dconv_w, dconv_b

    dx3, dx3b, dg_ffn1, dw_up1, dw_down1, dconv_w1, dconv_b1 = ffn_bwd(dx4, dx4b, x3, h4, u1, act1, 1)

    do_b = _matmul(dx3b, w_bo, mode="nt", out_dtype=BF16, name="d_ob", tm=1024, tn=B_OUT)
    dw_bo = _matmul_tn(o_b, dx3b, out_dtype=BF16, name="dw_bo")
    dl_b = _head_rowsum_compact(do_b, o_b, "delta_b")
    slots_up1, slots_down1, slots_conv1 = _ffn_slots(dw_up1, dw_down1, dconv_w1)
    dqp, dkp, dvp, land_down1, land_conv1 = _dil_bwd(
        qp, kp, vp, _same_permute(do_b), _same_permute(lse_b), _same_permute(dl_b), slopes, "dil_bwd",
        [slots_down1, slots_conv1])

    def natural(tp):
        return jnp.concatenate([_unpermute(tp[g], B_DILS[g]) for g in range(3)], axis=1)

    dqb = natural(dqp)
    dkv = jnp.concatenate([natural(dkp), natural(dvp)], axis=1)
    dw_q = _matmul_tn(h3, dqb, out_dtype=BF16, name="dw_q")
    dw_kv = _matmul_tn(hk, dkv, out_dtype=BF16, name="dw_kv")
    dh3 = _matmul(dqb, w_q, mode="nt", out_dtype=F32, name="dh_mix1", tm=512, tn=D)
    dhk = _matmul(dkv, w_kvf, mode="nt", out_dtype=F32, name="dh_kv", tm=512, tn=D)
    dx2, _, dg_mix1 = _rmsnorm_bwd(x2, dh3, gain(mix_norm_g[1]), dx3, "norm_mix1_bwd")
    dx2, dx2b, dg_kv = _rmsnorm_bwd(x2, dhk, gain(kv_norm_g), dx2, "norm_kv_bwd")

    dx1, dx1b, dg_ffn0, dw_up0, dw_down0, dconv_w0, dconv_b0 = ffn_bwd(dx2, dx2b, x1, h2, u0, act0, 0)

    do_a = _matmul(dx1b, w_out, mode="nt", out_dtype=BF16, name="d_oa", tm=512, tn=D)
    dw_out = _matmul_tn(o_a, dx1b, out_dtype=BF16, name="dw_out")
    dl_a = _head_rowsum(do_a, o_a, "delta_a")
    dq_a, dk_a, dv_a, dcol, drow, *land = _fox_bwd(
        qkv, do_a, lse_a, dl_a, c_t2, "fox_bwd",
        [dw_out.reshape(N_DEV, D // N_DEV, D), _cols_to_slots(dw_q), _cols_to_slots(dw_bo), _cols_to_slots(dw_kv)]
        + _ffn_slots(dw_up0, dw_down0, dconv_w0) + [slots_up1])
    land_out, land_q, land_bo, land_kv, land_up0, land_down0, land_conv0, land_up1 = land

    def head_sums(t):
        return t.reshape(S, N_PAIRS, 2, HEAD_DIM)[:, :, ::-1, 0].reshape(S, A_HEADS).T

    dz_t, db_f = _fox_prep_bwd(head_sums(drow), head_sums(dcol), z_t, b_f, "fox_prep_bwd")
    dz = jnp.pad(dz_t.T, ((0, 0), (0, LANES - A_HEADS))).astype(BF16)
    dproj = jnp.concatenate([dq_a.astype(BF16), dk_a, dv_a, dz], axis=1)
    dw_in = _matmul_tn(h1, dproj, out_dtype=BF16, name="dw_in")
    dh1, land_in = _matmul(dproj, w_in_pad, mode="nt", out_dtype=F32, name="dh_mix0", tm=512, tn=D,
                           scatter=[_cols_to_slots(dw_in[:, :A_QKV + A_HEADS])])
    grad_x, _, dg_mix0 = _rmsnorm_bwd(x0, dh1, gain(mix_norm_g[0]), dx1, "norm_mix0_bwd")

    dg_mix = jnp.concatenate([dg_mix0, dg_mix1], axis=0)
    dg_ffn = jnp.concatenate([dg_ffn0, dg_ffn1], axis=0)
    dconv_b = jnp.concatenate([dconv_b0, dconv_b1], axis=0)
    small_part = _pack_small(db_f, dg_kv, dg_mix, dg_ffn, dconv_b, dg_final)
    _, (small_parts,) = _final_exchange([], [small_part], "gather_small_grads")
    landed = [land_in, land_out, land_q, land_bo, land_kv, land_up0, land_up1, land_down0, land_down1,
              land_conv0, land_conv1]
    return loss_blk, grad_x, landed, small_parts


def kernel(x, a_w_in, a_b_f, a_w_out, b_w_q, b_w_out, kv_norm_g, w_kv, mix_norm_g, ffn_norm_g, ffn_w_up, ffn_conv_w, ffn_conv_b, ffn_w_down, final_norm_g, loss_target, m_a_w_in, m_a_b_f, m_a_w_out, m_b_w_q, m_b_w_out, m_kv_norm_g, m_w_kv, m_mix_norm_g, m_ffn_norm_g, m_ffn_w_up, m_ffn_conv_w, m_ffn_conv_b, m_ffn_w_down, m_final_norm_g, v_a_w_in, v_a_b_f, v_a_w_out, v_b_w_q, v_b_w_out, v_kv_norm_g, v_w_kv, v_mix_norm_g, v_ffn_norm_g, v_ffn_w_up, v_ffn_conv_w, v_ffn_conv_b, v_ffn_w_down, v_final_norm_g):
    def shards(a_w_in, a_w_out, b_w_q, b_w_out, w_kv, ffn_w_up, ffn_w_down, ffn_conv_w):
        return [a_w_in[0], a_w_out[0], b_w_q[0], b_w_out[0], w_kv, ffn_w_up[0], ffn_w_up[1],
                ffn_w_down[0], ffn_w_down[1], ffn_conv_w[0], ffn_conv_w[1]]

    w_loc = shards(a_w_in, a_w_out, b_w_q, b_w_out, w_kv, ffn_w_up, ffn_w_down, ffn_conv_w)
    m_loc = shards(m_a_w_in, m_a_w_out, m_b_w_q, m_b_w_out, m_w_kv, m_ffn_w_up, m_ffn_w_down, m_ffn_conv_w)
    v_loc = shards(v_a_w_in, v_a_w_out, v_b_w_q, v_b_w_out, v_w_kv, v_ffn_w_up, v_ffn_w_down, v_ffn_conv_w)

    (g_in,) = _all_gather([a_w_in[0].astype(BF16)], "gather_a_w_in")
    w_in = _cols_from_slots(g_in)
    w_in_pad = jnp.pad(w_in, ((0, 0), (0, A_PROJ_PAD - w_in.shape[1])))
    late_shards = [a_w_out[0].astype(BF16), b_w_q[0].astype(BF16), b_w_out[0].astype(BF16), w_kv.astype(BF16),
                   ffn_w_up.reshape(2 * D, -1).astype(BF16), ffn_w_down.reshape(-1, D).astype(BF16),
                   ffn_conv_w.reshape(6, -1)]

    loss_blk, grad_x, landed, small_parts = _local_step(
        x[0], loss_target[0], w_in_pad, late_shards,
        a_b_f, kv_norm_g, mix_norm_g, ffn_norm_g, ffn_conv_b, final_norm_g)

    big = [_adamw_sharded(w_loc[k], m_loc[k], v_loc[k], landed[k], f"adamw{k}") for k in range(11)]

    small = _adamw_replicated(
        _pack_small(a_b_f, kv_norm_g, mix_norm_g, ffn_norm_g, ffn_conv_b, final_norm_g),
        _pack_small(m_a_b_f, m_kv_norm_g, m_mix_norm_g, m_ffn_norm_g, m_ffn_conv_b, m_final_norm_g),
        _pack_small(v_a_b_f, v_kv_norm_g, v_mix_norm_g, v_ffn_norm_g, v_ffn_conv_b, v_final_norm_g),
        small_parts, "adamw_small")

    loss = lax.psum(loss_blk[0, 0], ("x", "y", "c"))

    def assemble(kind):
        b = [r[kind] for r in big]
        s_abf, s_kv, s_mix, s_ffn, s_cb, s_fin = _unpack_small(small[kind])
        return [b[0][None], s_abf, b[1][None], b[2][None], b[3][None], s_kv, b[4], s_mix, s_ffn,
                jnp.stack([b[5], b[6]]), jnp.stack([b[9], b[10]]), s_cb, jnp.stack([b[7], b[8]]), s_fin]

    return (loss, grad_x[None], *assemble(0), *assemble(1), *assemble(2), *assemble(3))
```

```python
import functools
import math

import jax
import jax.numpy as jnp
from jax import lax
from jax.experimental import pallas as pl
from jax.experimental.pallas import tpu as pltpu

F32 = jnp.float32
BF16 = jnp.bfloat16

S = 4096
D = 1024
N_DEV = 8
A_HEADS = 16
HEAD_DIM = 64
A_QKV = 3072
A_PROJ_PAD = 3200
B_Q = 1536
B_OUT = 512
B_KV = 3072
B_W = 128
B_DILS = (1, 4, 16)
D_FF = 2816
RMS_EPS = 1e-6
SCALE = HEAD_DIM ** -0.5
NEG = -1e30

ADAM_LR = 0.001
ADAM_B1 = 0.9
ADAM_B2 = 0.999
ADAM_EPS = 1e-08
ADAM_WD = 0.01
ADAM_STEP = 10

LANES = 128
VMEM_LIMIT = 56 * 1024 * 1024
MESH = pl.DeviceIdType.MESH
ANY = pl.BlockSpec(memory_space=pl.ANY)

NT_DIMS = (((1,), (1,)), ((), ()))
TN_DIMS = (((0,), (0,)), ((), ()))
NN_DIMS = (((1,), (0,)), ((), ()))


def _params(*sem):
    return pltpu.CompilerParams(dimension_semantics=sem if sem else None, vmem_limit_bytes=VMEM_LIMIT)


def _dot(a, b, dims=NN_DIMS):
    return lax.dot_general(a, b, dims, preferred_element_type=F32)


def _split_dot(x, mat, pieces):
    out = None
    rem = x
    for _ in range(pieces):
        part = rem.astype(BF16)
        rem = rem - part.astype(F32)
        d = _dot(part, mat)
        out = d if out is None else out + d
    return out


def _pick(n, prefs):
    for p in prefs:
        if n % p == 0:
            return p
    return n


def _gather_phases(ins, outs, sems):
    n = len(ins)
    if n == 0:
        return (lambda: None,) * 3
    send_sems, recv_sems, local_sems = sems
    x, y, c = lax.axis_index("x"), lax.axis_index("y"), lax.axis_index("c")
    me, sibling = (x, y, c), (x, y, 1 - c)
    chips = [(1 - x, y), (x, 1 - y), (1 - x, 1 - y)]

    def slot(a, px, py, pc):
        return outs[a].at[4 * px + 2 * py + pc]

    def copy(a, k, block, to, src=None):
        return pltpu.make_async_remote_copy(
            src_ref=slot(a, *block) if src is None else src, dst_ref=slot(a, *block),
            send_sem=send_sems.at[a, k], recv_sem=recv_sems.at[a, k],
            device_id=to, device_id_type=MESH)

    mine = [pltpu.make_async_copy(ins[a], slot(a, *me), local_sems.at[a]) for a in range(n)]
    first = []
    for a in range(n):
        first.append(copy(a, 0, me, sibling, src=ins[a]))
        first += [copy(a, 1 + j, me, (*chip, c), src=ins[a]) for j, chip in enumerate(chips)]
    passed = [copy(a, 4 + j, (*chip, c), sibling) for j, chip in enumerate(chips) for a in range(n)]

    def start():
        for cp in mine + first:
            cp.start()

    def forward():
        k = 0
        for j, chip in enumerate(chips):
            for a in range(n):
                copy(a, 1 + j, (*chip, c), me).wait_recv()
                passed[k].start()
                k += 1

    def finish():
        for a in range(n):
            copy(a, 0, sibling, me).wait_recv()
            for j, chip in enumerate(chips):
                copy(a, 4 + j, (*chip, 1 - c), me).wait_recv()
        for cp in first + passed:
            cp.wait_send()
        for cp in mine:
            cp.wait()

    return start, forward, finish


def _all_gather(arrays, name):
    n = len(arrays)

    def body(*refs):
        for phase in _gather_phases(refs[:n], refs[n:2 * n], refs[2 * n:]):
            phase()

    return pl.pallas_call(
        body, name=name,
        out_shape=[jax.ShapeDtypeStruct((N_DEV,) + a.shape, a.dtype) for a in arrays],
        in_specs=[ANY] * n, out_specs=[ANY] * n,
        scratch_shapes=[pltpu.SemaphoreType.DMA((n, 7)), pltpu.SemaphoreType.DMA((n, 7)),
                        pltpu.SemaphoreType.DMA((n,))],
    )(*arrays)


PEER_FLIPS = [(dx, dy, dc) for dx in (0, 1) for dy in (0, 1) for dc in (0, 1) if (dx, dy, dc) != (0, 0, 0)]


def _exchange_copies(ins, outs, sems, scatter):
    if not ins:
        return []
    send_sems, recv_sems, local_sems = sems
    x, y, c = lax.axis_index("x"), lax.axis_index("y"), lax.axis_index("c")
    me = 4 * x + 2 * y + c
    copies = []
    for a in range(len(ins)):
        copies.append(pltpu.make_async_copy(ins[a].at[me] if scatter else ins[a], outs[a].at[me], local_sems.at[a]))
        for k, (dx, dy, dc) in enumerate(PEER_FLIPS):
            px, py, pc = (1 - x if dx else x), (1 - y if dy else y), (1 - c if dc else c)
            copies.append(pltpu.make_async_remote_copy(
                src_ref=ins[a].at[4 * px + 2 * py + pc] if scatter else ins[a], dst_ref=outs[a].at[me],
                send_sem=send_sems.at[a, k], recv_sem=recv_sems.at[a, k],
                device_id=(px, py, pc), device_id_type=MESH))
    return copies


def _exchange_scratch(n):
    if n == 0:
        return []
    return [pltpu.SemaphoreType.DMA((n, 7)), pltpu.SemaphoreType.DMA((n, 7)), pltpu.SemaphoreType.DMA((n,))]


def _exchange_shapes(arrays, scatter):
    return [jax.ShapeDtypeStruct((N_DEV,) + (a.shape[1:] if scatter else a.shape), a.dtype) for a in arrays]


def _final_exchange(scatter, gather, name):
    ns, ng = len(scatter), len(gather)

    def body(*refs):
        ins, outs, sems = refs[:ns + ng], refs[ns + ng:2 * (ns + ng)], refs[2 * (ns + ng):]
        n_sems = len(_exchange_scratch(ns))
        copies = (_exchange_copies(ins[:ns], outs[:ns], sems[:n_sems], True)
                  + _exchange_copies(ins[ns:], outs[ns:], sems[n_sems:], False))
        for cp in copies:
            cp.start()
        for cp in copies:
            cp.wait()

    res = pl.pallas_call(
        body, name=name, out_shape=_exchange_shapes(scatter, True) + _exchange_shapes(gather, False),
        in_specs=[ANY] * (ns + ng), out_specs=[ANY] * (ns + ng),
        scratch_shapes=_exchange_scratch(ns) + _exchange_scratch(ng),
    )(*scatter, *gather)
    return res[:ns], res[ns:]


MM_ROWS = 512
MM_COLS = 1024


def _matmul(a, b, *, mode, out_dtype, name, tm, tn, res=None, scatter=()):
    if mode == "nn":
        (M, K), (K2, N) = a.shape, b.shape
    else:
        (M, K), (N, K2) = a.shape, b.shape
    assert K == K2, (a.shape, b.shape, mode)
    tm, tn = min(tm, M), min(tn, N)
    sm = min(tm, MM_ROWS)
    sn = tn if tn <= MM_COLS else _pick(tn, (512, 256, 128))
    assert M % tm == 0 and N % tn == 0 and tm % sm == 0, (M, N, K, tm, tn)
    dims = NN_DIMS if mode == "nn" else NT_DIMS
    a_spec = pl.BlockSpec((tm, K), lambda i, j: (i, 0))
    if mode == "nt":
        b_spec = pl.BlockSpec((tn, K), lambda i, j: (j, 0))
    else:
        b_spec = pl.BlockSpec((K, tn), lambda i, j: (0, j))
    o_spec = pl.BlockSpec((tm, tn), lambda i, j: (i, j))
    has_res = res is not None
    n_in, n_ex = 2 + has_res, len(scatter)
    gm, gn = M // tm, N // tn

    def body(*refs):
        a_ref, b_ref = refs[0], refs[1]
        r_ref = refs[2] if has_res else None
        o_ref = refs[n_in + n_ex]
        exchange = (refs[n_in:n_in + n_ex], refs[n_in + n_ex + 1:n_in + 2 * n_ex + 1], refs[n_in + 2 * n_ex + 1:], True)

        @pl.when(jnp.logical_and(pl.program_id(0) == 0, pl.program_id(1) == 0))
        def _():
            for cp in _exchange_copies(*exchange):
                cp.start()

        def chunk(r, carry):
            rows = pl.ds(pl.multiple_of(r * sm, sm), sm)
            av = a_ref[rows, :]
            for c0 in range(0, tn, sn):
                bv = b_ref[c0:c0 + sn, :] if mode == "nt" else b_ref[:, c0:c0 + sn]
                total = _dot(av, bv, dims)
                if has_res:
                    total = total + r_ref[rows, c0:c0 + sn]
                o_ref[rows, c0:c0 + sn] = total.astype(out_dtype)
            return carry

        lax.fori_loop(0, tm // sm, chunk, 0)

        @pl.when(jnp.logical_and(pl.program_id(0) == gm - 1, pl.program_id(1) == gn - 1))
        def _():
            for cp in _exchange_copies(*exchange):
                cp.wait()

    out = pl.pallas_call(
        body, name=name, grid=(gm, gn),
        out_shape=[jax.ShapeDtypeStruct((M, N), out_dtype)] + _exchange_shapes(scatter, True),
        in_specs=[a_spec, b_spec] + ([o_spec] if has_res else []) + [ANY] * n_ex,
        out_specs=[o_spec] + [ANY] * n_ex,
        scratch_shapes=_exchange_scratch(n_ex),
        compiler_params=_params("arbitrary", "arbitrary"),
    )(*((a, b, res) if has_res else (a, b)), *scatter)
    return out if n_ex else out[0]


def _matmul_tn(a, b, *, out_dtype, name, tk=512, sm=256):
    (K, M), (K2, N) = a.shape, b.shape
    assert K == K2 and K % tk == 0 and M % sm == 0, (a.shape, b.shape)
    nk = K // tk

    def body(a_ref, b_ref, o_ref, acc_ref):
        k = pl.program_id(0)

        @pl.when(k == 0)
        def _():
            acc_ref[...] = jnp.zeros_like(acc_ref)

        def chunk(mi, carry):
            cols = pl.ds(pl.multiple_of(mi * sm, sm), sm)
            acc_ref[cols, :] += _dot(a_ref[:, cols].T, b_ref[...])
            return carry

        lax.fori_loop(0, M // sm, chunk, 0)

        @pl.when(k == nk - 1)
        def _():
            def emit(mi, carry):
                rows = pl.ds(pl.multiple_of(mi * sm, sm), sm)
                o_ref[rows, :] = acc_ref[rows, :].astype(out_dtype)
                return carry
            lax.fori_loop(0, M // sm, emit, 0)

    return pl.pallas_call(
        body, name=name, grid=(nk,),
        out_shape=jax.ShapeDtypeStruct((M, N), out_dtype),
        in_specs=[pl.BlockSpec((tk, M), lambda k: (k, 0)), pl.BlockSpec((tk, N), lambda k: (k, 0))],
        out_specs=pl.BlockSpec((M, N), lambda k: (0, 0)),
        scratch_shapes=[pltpu.VMEM((M, N), F32)],
        compiler_params=_params("arbitrary"),
    )(a, b)


def _rmsnorm_fwd(x, gains, name, tr=256):
    n = len(gains)

    def body(*refs):
        x_ref = refs[0]
        xv = x_ref[...]
        r = lax.rsqrt(jnp.mean(xv * xv, axis=-1, keepdims=True) + RMS_EPS)
        y = xv * r
        for a in range(n):
            refs[1 + n + a][...] = (y * refs[1 + a][...]).astype(BF16)

    row = pl.BlockSpec((tr, D), lambda i: (i, 0))
    gain = pl.BlockSpec((1, D), lambda i: (0, 0))
    return pl.pallas_call(
        body, name=name, grid=(S // tr,),
        out_shape=[jax.ShapeDtypeStruct((S, D), BF16)] * n,
        in_specs=[row] + [gain] * n, out_specs=[row] * n,
        compiler_params=_params("parallel"),
    )(x, *gains)


def _rmsnorm_bwd(x, dy, g, dres, name, tr=256):
    def body(x_ref, dy_ref, g_ref, dres_ref, dx_ref, dxb_ref, dg_ref):
        xv = x_ref[...]
        dyv = dy_ref[...]
        r = lax.rsqrt(jnp.mean(xv * xv, axis=-1, keepdims=True) + RMS_EPS)
        xhat = xv * r
        dxhat = dyv * g_ref[...]
        mean_term = jnp.mean(dxhat * xhat, axis=-1, keepdims=True)
        dx = r * (dxhat - xhat * mean_term) + dres_ref[...]
        dx_ref[...] = dx
        dxb_ref[...] = dx.astype(BF16)
        part = jnp.sum(dyv * xhat, axis=0, keepdims=True)

        @pl.when(pl.program_id(0) == 0)
        def _():
            dg_ref[...] = part

        @pl.when(pl.program_id(0) > 0)
        def _():
            dg_ref[...] += part

    row = pl.BlockSpec((tr, D), lambda i: (i, 0))
    gain = pl.BlockSpec((1, D), lambda i: (0, 0))
    return pl.pallas_call(
        body, name=name, grid=(S // tr,),
        out_shape=[jax.ShapeDtypeStruct((S, D), F32), jax.ShapeDtypeStruct((S, D), BF16),
                   jax.ShapeDtypeStruct((1, D), F32)],
        in_specs=[row, row, gain, row], out_specs=[row, row, gain],
        compiler_params=_params("arbitrary"),
    )(x, dy, g, dres)


def _final_loss(x, target, g, name, tr=256):
    def body(x_ref, t_ref, g_ref, loss_ref, dx_ref, dxb_ref, dg_ref):
        xv = x_ref[...]
        gv = g_ref[...]
        r = lax.rsqrt(jnp.mean(xv * xv, axis=-1, keepdims=True) + RMS_EPS)
        xhat = xv * r
        err = xhat * gv - t_ref[...]
        row_loss = jnp.mean(err * err, axis=-1, keepdims=True)
        lpart = 0.5 * jnp.sum(row_loss, axis=0, keepdims=True)
        dyv = err / D
        dxhat = dyv * gv
        mean_term = jnp.mean(dxhat * xhat, axis=-1, keepdims=True)
        dx = r * (dxhat - xhat * mean_term)
        dx_ref[...] = dx
        dxb_ref[...] = dx.astype(BF16)
        gpart = jnp.sum(dyv * xhat, axis=0, keepdims=True)

        @pl.when(pl.program_id(0) == 0)
        def _():
            dg_ref[...] = gpart
            loss_ref[...] = jnp.broadcast_to(lpart, loss_ref.shape)

        @pl.when(pl.program_id(0) > 0)
        def _():
            dg_ref[...] += gpart
            loss_ref[...] += jnp.broadcast_to(lpart, loss_ref.shape)

    row = pl.BlockSpec((tr, D), lambda i: (i, 0))
    gain = pl.BlockSpec((1, D), lambda i: (0, 0))
    lspec = pl.BlockSpec((8, LANES), lambda i: (0, 0))
    return pl.pallas_call(
        body, name=name, grid=(S // tr,),
        out_shape=[jax.ShapeDtypeStruct((8, LANES), F32), jax.ShapeDtypeStruct((S, D), F32),
                   jax.ShapeDtypeStruct((S, D), BF16), jax.ShapeDtypeStruct((1, D), F32)],
        in_specs=[row, row, gain], out_specs=[lspec, row, row, gain],
        compiler_params=_params("arbitrary"),
    )(x, target, g)


CONV_TR = 128
CONV_TC = D_FF
CONV_NJ = D_FF // CONV_TC
HALO = 16


def _causal_taps(cur_ref, prev_ref, first):
    xv = cur_ref[...].astype(F32)
    pv = prev_ref[...].astype(F32)
    p1 = jnp.where(first, 0.0, pv[HALO - 1:HALO, :])
    p2 = jnp.where(first, 0.0, pv[HALO - 2:HALO - 1, :])
    r1, r2 = pltpu.roll(xv, 1, 0), pltpu.roll(xv, 2, 0)
    row = lax.broadcasted_iota(jnp.int32, (8, xv.shape[1]), 0)
    xm1 = jnp.concatenate([jnp.where(row == 0, p1, r1[0:8]), r1[8:]], axis=0)
    xm2 = jnp.concatenate([jnp.where(row == 0, p2, jnp.where(row == 1, p1, r2[0:8])), r2[8:]], axis=0)
    return xv, xm1, xm2


def _conv_specs():
    def prev_row(i):
        return jnp.maximum(i * (CONV_TR // HALO) - 1, 0)
    ua = pl.BlockSpec((CONV_TR, CONV_TC), lambda i, j: (i, j))
    ug = ua
    pa = pl.BlockSpec((HALO, CONV_TC), lambda i, j: (prev_row(i), j))
    pg = pa
    wa = pl.BlockSpec((3, CONV_TC), lambda i, j: (0, j))
    wg = pl.BlockSpec((3, CONV_TC), lambda i, j: (0, j + CONV_NJ))
    ba = pl.BlockSpec((1, CONV_TC), lambda i, j: (0, j))
    bg = pl.BlockSpec((1, CONV_TC), lambda i, j: (0, j + CONV_NJ))
    return [ua, pa, ug, pg, wa, wg, ba, bg]


def _convgate_fwd(u_a, u_g, w, b, name):
    def body(ua, pa, ug, pg, wa, wg, ba, bg, o_ref):
        first = pl.program_id(0) == 0
        x0, x1, x2 = _causal_taps(ua, pa, first)
        ac = wa[0:1, :] * x2 + wa[1:2, :] * x1 + wa[2:3, :] * x0 + ba[...]
        x0, x1, x2 = _causal_taps(ug, pg, first)
        gc = wg[0:1, :] * x2 + wg[1:2, :] * x1 + wg[2:3, :] * x0 + bg[...]
        sg = 0.5 * jnp.tanh(0.5 * gc) + 0.5
        o_ref[...] = (gc * sg * ac).astype(BF16)

    return pl.pallas_call(
        body, name=name, grid=(S // CONV_TR, CONV_NJ),
        out_shape=jax.ShapeDtypeStruct((S, D_FF), BF16),
        in_specs=_conv_specs(),
        out_specs=pl.BlockSpec((CONV_TR, CONV_TC), lambda i, j: (i, j)),
        compiler_params=_params("parallel", "parallel"),
    )(u_a, u_a, u_g, u_g, w, w, b, b)


def _anticausal_conv(d, nxt_ref, w_ref, last):
    n1 = jnp.where(last, 0.0, nxt_ref[0:1, :])
    n2 = jnp.where(last, 0.0, nxt_ref[1:2, :])
    r1, r2 = pltpu.roll(d, CONV_TR - 1, 0), pltpu.roll(d, CONV_TR - 2, 0)
    row = lax.broadcasted_iota(jnp.int32, (8, d.shape[1]), 0)
    cut = CONV_TR - 8
    dp1 = jnp.concatenate([r1[:cut], jnp.where(row == 7, n1, r1[cut:])], axis=0)
    dp2 = jnp.concatenate([r2[:cut], jnp.where(row == 7, n2, jnp.where(row == 6, n1, r2[cut:]))], axis=0)
    return w_ref[2:3, :] * d + w_ref[1:2, :] * dp1 + w_ref[0:1, :] * dp2


def _convgate_bwd(u_a, u_g, w, b, dact, name):
    n_i = S // CONV_TR

    def body(ua, pa, ug, pg, wa, wg, ba, bg, d_ref, dua_ref, dug_ref, dwa_ref, dwg_ref, dba_ref, dbg_ref,
             nxt_a, nxt_g):
        i = pl.program_id(1)
        last = i == 0
        first = i == n_i - 1
        a0, a1, a2 = _causal_taps(ua, pa, first)
        ac = wa[0:1, :] * a2 + wa[1:2, :] * a1 + wa[2:3, :] * a0 + ba[...]
        g0, g1, g2 = _causal_taps(ug, pg, first)
        gc = wg[0:1, :] * g2 + wg[1:2, :] * g1 + wg[2:3, :] * g0 + bg[...]
        sg = 0.5 * jnp.tanh(0.5 * gc) + 0.5
        dact_v = d_ref[...].astype(F32)
        da = dact_v * (gc * sg)
        dg = dact_v * ac * (sg * (1.0 + gc * (1.0 - sg)))
        dua_ref[...] = _anticausal_conv(da, nxt_a, wa, last).astype(BF16)
        dug_ref[...] = _anticausal_conv(dg, nxt_g, wg, last).astype(BF16)
        nxt_a[...] = da[0:8]
        nxt_g[...] = dg[0:8]

        def col(v):
            return jnp.sum(v, axis=0, keepdims=True)

        parts = [col(da * a2), col(da * a1), col(da * a0), col(dg * g2), col(dg * g1), col(dg * g0),
                 col(da), col(dg)]

        @pl.when(last)
        def _():
            for k in range(3):
                dwa_ref[k:k + 1, :] = parts[k]
                dwg_ref[k:k + 1, :] = parts[3 + k]
            dba_ref[...] = parts[6]
            dbg_ref[...] = parts[7]

        @pl.when(i > 0)
        def _():
            for k in range(3):
                dwa_ref[k:k + 1, :] += parts[k]
                dwg_ref[k:k + 1, :] += parts[3 + k]
            dba_ref[...] += parts[6]
            dbg_ref[...] += parts[7]

    def swap(spec):
        return pl.BlockSpec(spec.block_shape, lambda j, i, f=spec.index_map: f(n_i - 1 - i, j))

    blk = pl.BlockSpec((CONV_TR, CONV_TC), lambda j, i: (n_i - 1 - i, j))
    w3 = pl.BlockSpec((3, CONV_TC), lambda j, i: (0, j))
    b1 = pl.BlockSpec((1, CONV_TC), lambda j, i: (0, j))
    return pl.pallas_call(
        body, name=name, grid=(CONV_NJ, n_i),
        out_shape=[jax.ShapeDtypeStruct((S, D_FF), BF16), jax.ShapeDtypeStruct((S, D_FF), BF16),
                   jax.ShapeDtypeStruct((3, D_FF), F32), jax.ShapeDtypeStruct((3, D_FF), F32),
                   jax.ShapeDtypeStruct((1, D_FF), F32), jax.ShapeDtypeStruct((1, D_FF), F32)],
        in_specs=[swap(s) for s in _conv_specs()] + [blk],
        out_specs=[blk, blk, w3, w3, b1, b1],
        scratch_shapes=[pltpu.VMEM((8, CONV_TC), F32), pltpu.VMEM((8, CONV_TC), F32)],
        compiler_params=_params("arbitrary", "arbitrary"),
    )(u_a, u_a, u_g, u_g, w, w, b, b, dact)


FOX_T = 512
FOX_TQ, FOX_TK = 512, 512
FOX_FORWARD_AT = 4
N_PAIRS = A_HEADS // 2


def _lane_masks():
    lane = lax.broadcasted_iota(jnp.int32, (1, LANES), 1)
    return lane, (lane < HEAD_DIM, lane >= HEAD_DIM)


def _fox_prep_fwd(z_t, b, name):
    def body(z_ref, b_ref, c_ref):
        r = lax.broadcasted_iota(jnp.int32, (LANES, LANES), 0)
        cc = lax.broadcasted_iota(jnp.int32, (LANES, LANES), 1)
        upper = (r <= cc).astype(BF16)
        carry = jnp.zeros((A_HEADS, 1), F32)
        for blk in range(S // LANES):
            sl = slice(blk * LANES, (blk + 1) * LANES)
            z = z_ref[:, sl] + b_ref[...]
            lf = jnp.minimum(z, 0.0) - jnp.log(1.0 + jnp.exp(-jnp.abs(z)))
            cs = _split_dot(lf, upper, 3) + carry
            c_ref[:, sl] = cs
            carry = cs[:, LANES - 1:LANES]

    return pl.pallas_call(
        body, name=name, out_shape=jax.ShapeDtypeStruct((A_HEADS, S), F32),
        compiler_params=_params(),
    )(z_t, b)


def _fox_prep_bwd(drow_t, dcol_t, z_t, b, name):
    def body(dr_ref, dc_ref, z_ref, b_ref, dz_ref, db_ref):
        r = lax.broadcasted_iota(jnp.int32, (LANES, LANES), 0)
        cc = lax.broadcasted_iota(jnp.int32, (LANES, LANES), 1)
        lower = (r >= cc).astype(BF16)
        carry = jnp.zeros((A_HEADS, 1), F32)
        db = jnp.zeros((A_HEADS, 1), F32)
        for blk in reversed(range(S // LANES)):
            sl = slice(blk * LANES, (blk + 1) * LANES)
            rc = _split_dot(dr_ref[:, sl] - dc_ref[:, sl], lower, 3) + carry
            carry = rc[:, 0:1]
            z = z_ref[:, sl] + b_ref[...]
            dz = rc / (1.0 + jnp.exp(z))
            dz_ref[:, sl] = dz
            db = db + jnp.sum(dz, axis=1, keepdims=True)
        db_ref[...] = db

    return pl.pallas_call(
        body, name=name,
        out_shape=[jax.ShapeDtypeStruct((A_HEADS, S), F32), jax.ShapeDtypeStruct((A_HEADS, 1), F32)],
        compiler_params=_params(),
    )(drow_t, dcol_t, z_t, b)


def _fox_fwd(qkv, c_t2, name, gather):
    tq, tk = FOX_TQ, FOX_TK

    n = len(gather)

    def body(*refs):
        q_ref, k_ref, v_ref, ct_ref = refs[:4]
        o_ref, lse_ref = refs[4 + n:6 + n]
        s_scr, p_scr, acc_scr = refs[-5:-3], refs[-3:-1], refs[-1]
        qi = pl.program_id(1)

        gather_start, gather_forward, gather_finish = _gather_phases(
            refs[4:4 + n], refs[6 + n:6 + 2 * n], refs[6 + 2 * n:len(refs) - 5])

        @pl.when(jnp.logical_and(pl.program_id(0) == 0, qi == 0))
        def _():
            gather_start()

        @pl.when(jnp.logical_and(pl.program_id(0) == FOX_FORWARD_AT, qi == 0))
        def _():
            gather_forward()

        n_full = jnp.right_shift(qi, (tk // tq).bit_length() - 1)
        lane, masks = _lane_masks()
        q = q_ref[...] * SCALE
        qs = [jnp.where(masks[e], q, jnp.zeros_like(q)) for e in range(2)]

        def scores_into(j, slot):
            start = pl.multiple_of(j * tk, tk)
            kb = k_ref[pl.ds(start, tk), :]
            for e in range(2):
                s_scr[slot][e] = _dot(qs[e], kb, NT_DIMS) - ct_ref[e:e + 1, pl.ds(start, tk)]

        def softmax_of(slot, m, masked):
            m_new, alpha = [], []
            for e in range(2):
                s = s_scr[slot][e]
                if masked:
                    rows = lax.broadcasted_iota(jnp.int32, (tq, tk), 0) + (qi * tq - n_full * tk)
                    cols = lax.broadcasted_iota(jnp.int32, (tq, tk), 1)
                    s = jnp.where(cols <= rows, s, NEG)
                m_new.append(jnp.maximum(m[e], jnp.max(s, axis=1, keepdims=True)))
                p_scr[slot][e] = jnp.exp(s - m_new[e]).astype(BF16)
                alpha.append(jnp.exp(m[e] - m_new[e]))
            return tuple(m_new), tuple(alpha)

        def values_of(j, slot, alpha):
            start = pl.multiple_of(j * tk, tk)
            vb = v_ref[pl.ds(start, tk), :]
            for e in range(2):
                acc_scr[e] = (alpha[e] * acc_scr[e]
                              + _dot(p_scr[slot][e], jnp.where(masks[e], vb, jnp.ones_like(vb))))

        def stage(j, cur, nxt, carry):
            m, a_prev = carry
            scores_into(j + 1, nxt)
            values_of(jnp.maximum(j - 1, 0), nxt, a_prev)
            return softmax_of(cur, m, False)

        def finish(cur, nxt, carry):
            m, a_prev = carry
            values_of(jnp.maximum(n_full - 1, 0), nxt, a_prev)
            (m0, m1), alpha = softmax_of(cur, m, True)
            values_of(n_full, cur, alpha)
            l0 = acc_scr[0][:, HEAD_DIM:HEAD_DIM + 1]
            l1 = acc_scr[1][:, 0:1]
            o_ref[...] = jnp.where(masks[0], acc_scr[0] / l0, acc_scr[1] / l1).astype(BF16)
            lse_ref[...] = jnp.where(masks[0], m0 + jnp.log(l0), m1 + jnp.log(l1))

        scores_into(0, 0)
        for e in range(2):
            p_scr[1][e] = jnp.zeros((tq, tk), BF16)
            acc_scr[e] = jnp.zeros((tq, LANES), F32)
        two = lambda x: (x, x)
        init = (two(jnp.full((tq, 1), NEG, F32)), two(jnp.ones((tq, 1), F32)))

        def two_stages(jj, carry):
            return stage(2 * jj + 1, 1, 0, stage(2 * jj, 0, 1, carry))

        carry = lax.fori_loop(0, jnp.right_shift(n_full, 1), two_stages, init)
        odd = jnp.bitwise_and(n_full, 1) == 1

        @pl.when(odd)
        def _():
            finish(1, 0, stage(n_full - 1, 0, 1, carry))

        @pl.when(jnp.logical_not(odd))
        def _():
            finish(0, 1, carry)

        @pl.when(jnp.logical_and(pl.program_id(0) == N_PAIRS - 1, qi == S // tq - 1))
        def _():
            gather_finish()

    qspec = pl.BlockSpec((tq, LANES), lambda h, i: (i, h))
    return pl.pallas_call(
        body, name=name, grid=(N_PAIRS, S // tq),
        out_shape=[jax.ShapeDtypeStruct((S, D), BF16), jax.ShapeDtypeStruct((S, D), F32)]
        + _exchange_shapes(gather, False),
        in_specs=[qspec,
                  pl.BlockSpec((S, LANES), lambda h, i: (0, N_PAIRS + h)),
                  pl.BlockSpec((S, LANES), lambda h, i: (0, 2 * N_PAIRS + h)),
                  pl.BlockSpec((None, 2, S), lambda h, i: (h, 0, 0))] + [ANY] * n,
        out_specs=[qspec, qspec] + [ANY] * n,
        scratch_shapes=_exchange_scratch(n) + [
            pltpu.VMEM((2, tq, tk), F32), pltpu.VMEM((2, tq, tk), F32),
            pltpu.VMEM((2, tq, tk), BF16), pltpu.VMEM((2, tq, tk), BF16),
            pltpu.VMEM((2, tq, LANES), F32)],
        compiler_params=_params("arbitrary", "arbitrary"),
    )(qkv, qkv, qkv, c_t2, *gather)


def _head_rowsum(a, b, name, tr=256):
    C = a.shape[1]

    def body(a_ref, b_ref, o_ref):
        r = lax.broadcasted_iota(jnp.int32, (LANES, LANES), 0) < HEAD_DIM
        cc = lax.broadcasted_iota(jnp.int32, (LANES, LANES), 1) < HEAD_DIM
        same_head = (r == cc).astype(BF16)
        for blk in range(C // LANES):
            sl = slice(blk * LANES, (blk + 1) * LANES)
            prod = a_ref[:, sl].astype(F32) * b_ref[:, sl].astype(F32)
            o_ref[:, sl] = _split_dot(prod, same_head, 2)

    row = pl.BlockSpec((tr, C), lambda i: (i, 0))
    return pl.pallas_call(
        body, name=name, grid=(S // tr,), out_shape=jax.ShapeDtypeStruct((S, C), F32),
        in_specs=[row, row], out_specs=row, compiler_params=_params("parallel"),
    )(a, b)


def _fox_bwd(qkv, do, lse, delta, c_t2, name, scatter):
    t = FOX_T
    nq = S // t

    n = len(scatter)

    def body(*refs):
        q_ref, k_ref, v_ref, do_ref, lse_ref, dl_ref, ct_ref = refs[:7]
        dq_ref, dk_ref, dv_ref, dcol_ref, drow_ref = refs[7 + n:12 + n]
        exchange = (refs[7:7 + n], refs[12 + n:12 + 2 * n], refs[12 + 2 * n:len(refs) - 5], True)
        sd_scr, pd_scr, acc_scr = refs[-5:-3], refs[-3:-1], refs[-1]
        kj = pl.program_id(1)

        @pl.when(jnp.logical_and(pl.program_id(0) == 0, kj == 0))
        def _():
            for cp in _exchange_copies(*exchange):
                cp.start()

        @pl.when(kj == 0)
        def _():
            dq_ref[...] = jnp.zeros_like(dq_ref)
            drow_ref[...] = jnp.zeros_like(drow_ref)

        lane, masks = _lane_masks()
        k = k_ref[...]
        v = v_ref[...]
        k_aug = [jnp.where(masks[e], k * SCALE, jnp.ones_like(k)) for e in range(2)]
        cs = [ct_ref[e:e + 1, :] for e in range(2)]

        def rows_of(i):
            r0 = pl.multiple_of(i * t, t)
            return pl.ds(r0, t), q_ref[pl.ds(r0, t), :] * SCALE, do_ref[pl.ds(r0, t), :]

        def scores_into(i, slot):
            _, qb, dob = rows_of(i)
            for e in range(2):
                qe = jnp.where(masks[e], qb, jnp.zeros_like(qb))
                doe = jnp.where(masks[e], dob, jnp.zeros_like(dob))
                sd_scr[slot][2 * e] = _dot(qe, k, NT_DIMS) - cs[e]
                sd_scr[slot][2 * e + 1] = _dot(doe, v, NT_DIMS)

        def pointwise(i, slot, masked):
            rows, _, _ = rows_of(i)
            for e in range(2):
                lo = e * HEAD_DIM
                s = sd_scr[slot][2 * e]
                if masked:
                    r = lax.broadcasted_iota(jnp.int32, (t, t), 0)
                    c = lax.broadcasted_iota(jnp.int32, (t, t), 1)
                    s = jnp.where(c <= r, s, NEG)
                p = jnp.exp(s - lse_ref[rows, lo:lo + 1])
                pd_scr[slot][2 * e] = p.astype(BF16)
                pd_scr[slot][2 * e + 1] = (p * (sd_scr[slot][2 * e + 1] - dl_ref[rows, lo:lo + 1])).astype(BF16)

        def accumulate(i, slot):
            rows, qb, dob = rows_of(i)
            dq_parts = []
            for e in range(2):
                p, ds = pd_scr[slot][2 * e], pd_scr[slot][2 * e + 1]
                q_aug = jnp.where(masks[e], qb, jnp.ones_like(qb))
                doe = jnp.where(masks[e], dob, jnp.zeros_like(dob))
                acc_scr[2] += _dot(p, doe, TN_DIMS)
                acc_scr[e] += _dot(ds, q_aug, TN_DIMS)
                dq_parts.append(_dot(ds, k_aug[e]))
            dq_ref[rows, :] += jnp.where(masks[0], dq_parts[0], dq_parts[1])
            drow_ref[rows, :] += jnp.where(masks[0], dq_parts[1], dq_parts[0])

        def stage(i, cur, nxt):
            scores_into(jnp.minimum(i + 1, nq - 1), nxt)
            accumulate(i - 1, nxt)
            pointwise(i, cur, False)

        acc_scr[...] = jnp.zeros_like(acc_scr)
        scores_into(kj, 0)
        pointwise(kj, 0, True)
        scores_into(jnp.minimum(kj + 1, nq - 1), 1)
        rest = nq - 1 - kj

        def two_stages(jj, carry):
            stage(kj + 1 + 2 * jj, 1, 0)
            stage(kj + 2 + 2 * jj, 0, 1)
            return carry

        lax.fori_loop(0, jnp.right_shift(rest, 1), two_stages, 0)
        odd = jnp.bitwise_and(rest, 1) == 1

        @pl.when(odd)
        def _():
            stage(nq - 1, 1, 0)
            accumulate(nq - 1, 1)

        @pl.when(jnp.logical_not(odd))
        def _():
            accumulate(nq - 1, 0)

        dk0, dk1, dv = acc_scr[0], acc_scr[1], acc_scr[2]
        dk_ref[...] = jnp.where(masks[0], dk0, dk1).astype(BF16)
        dcol_ref[...] = jnp.where(masks[0], dk1, dk0)
        dv_ref[...] = dv.astype(BF16)

        @pl.when(jnp.logical_and(pl.program_id(0) == N_PAIRS - 1, kj == nq - 1))
        def _():
            for cp in _exchange_copies(*exchange):
                cp.wait()

    full = lambda off: pl.BlockSpec((S, LANES), lambda h, j, off=off: (0, off + h))
    kv = lambda off: pl.BlockSpec((t, LANES), lambda h, j, off=off: (j, off + h))
    return pl.pallas_call(
        body, name=name, grid=(N_PAIRS, nq),
        out_shape=[jax.ShapeDtypeStruct((S, D), F32), jax.ShapeDtypeStruct((S, D), BF16),
                   jax.ShapeDtypeStruct((S, D), BF16), jax.ShapeDtypeStruct((S, D), F32),
                   jax.ShapeDtypeStruct((S, D), F32)] + _exchange_shapes(scatter, True),
        in_specs=[full(0), kv(N_PAIRS), kv(2 * N_PAIRS), full(0), full(0), full(0),
                  pl.BlockSpec((None, 2, t), lambda h, j: (h, 0, j))] + [ANY] * n,
        out_specs=[full(0), kv(0), kv(0), kv(0), full(0)] + [ANY] * n,
        scratch_shapes=_exchange_scratch(n) + [
            pltpu.VMEM((4, t, t), F32), pltpu.VMEM((4, t, t), F32),
            pltpu.VMEM((4, t, t), BF16), pltpu.VMEM((4, t, t), BF16),
            pltpu.VMEM((3, t, LANES), F32)],
        compiler_params=_params("arbitrary", "arbitrary"),
    )(qkv, qkv, qkv, do, lse, delta, c_t2, *scatter)


B_PAIRS = 4
B_NB = S // B_W


def _group_consts(g):
    nbs = jnp.where(g == 0, B_NB // B_DILS[0], jnp.where(g == 1, B_NB // B_DILS[1], B_NB // B_DILS[2]))
    dil = jnp.where(g == 0, B_DILS[0], jnp.where(g == 1, B_DILS[1], B_DILS[2]))
    return nbs, dil


def _band(dil):
    qi = lax.broadcasted_iota(jnp.int32, (B_W, B_W), 0)
    kj = lax.broadcasted_iota(jnp.int32, (B_W, B_W), 1)
    dist_c = qi - kj
    dist_p = qi + B_W - kj
    return (dist_c * dil).astype(F32), dist_c >= 0, (dist_p * dil).astype(F32), dist_p <= B_W


def _dil_fwd(qp, kp, vp, slopes, name):
    def body(sl_ref, q_ref, kp_ref, kc_ref, vp_ref, vc_ref, o_ref, lse_ref):
        g, n = pl.program_id(0), pl.program_id(1)
        nbs, dil = _group_consts(g)
        has_prev = (n % nbs) != 0
        lane, masks = _lane_masks()
        bias_c, ok_c, bias_p, ok_p = _band(dil)
        ok_p = jnp.logical_and(ok_p, has_prev)
        heads = [(hp, e) for hp in range(B_PAIRS) for e in range(2)]
        col = lambda ref, hp: ref[:, hp * LANES:(hp + 1) * LANES]
        logits = []
        for hp, e in heads:
            q = col(q_ref, hp) * SCALE
            qe = jnp.where(masks[e], q, jnp.zeros_like(q))
            logits.append((_dot(qe, col(kc_ref, hp), NT_DIMS), _dot(qe, col(kp_ref, hp), NT_DIMS)))
        probs = []
        for (hp, e), (sc, sp) in zip(heads, logits):
            slope = sl_ref[g * 8 + 2 * hp + e]
            sc = jnp.where(ok_c, sc - slope * bias_c, NEG)
            sp = jnp.where(ok_p, sp - slope * bias_p, NEG)
            m = jnp.maximum(jnp.max(sc, axis=1, keepdims=True), jnp.max(sp, axis=1, keepdims=True))
            probs.append((jnp.exp(sc - m).astype(BF16), jnp.exp(sp - m).astype(BF16), m))
        outs, lses = [], []
        for (hp, e), (pc, pp, m) in zip(heads, probs):
            vc, vpv = col(vc_ref, hp), col(vp_ref, hp)
            acc = (_dot(pc, jnp.where(masks[e], vc, jnp.ones_like(vc)))
                   + _dot(pp, jnp.where(masks[e], vpv, jnp.ones_like(vpv))))
            l = acc[:, HEAD_DIM:HEAD_DIM + 1] if e == 0 else acc[:, 0:1]
            outs.append(acc / l)
            lses.append(m + jnp.log(l))
        o_ref[...] = jnp.concatenate(
            [jnp.where(masks[0], outs[2 * hp], outs[2 * hp + 1]) for hp in range(B_PAIRS)], axis=1).astype(BF16)
        lse = jnp.zeros((B_W, LANES), F32)
        for h in range(2 * B_PAIRS):
            lse = jnp.where(lane == h, lses[h], lse)
        lse_ref[...] = lse

    cur = pl.BlockSpec((None, B_W, B_OUT), lambda g, n, sl: (g, n, 0))
    prev = pl.BlockSpec((None, B_W, B_OUT), lambda g, n, sl: (g, jnp.maximum(n - 1, 0), 0))
    stat = pl.BlockSpec((None, B_W, LANES), lambda g, n, sl: (g, n, 0))
    return pl.pallas_call(
        body, name=name,
        grid_spec=pltpu.PrefetchScalarGridSpec(
            num_scalar_prefetch=1, grid=(3, B_NB),
            in_specs=[cur, prev, cur, prev, cur], out_specs=[cur, stat]),
        out_shape=[jax.ShapeDtypeStruct((3, S, B_OUT), BF16), jax.ShapeDtypeStruct((3, S, LANES), F32)],
        compiler_params=_params("parallel", "parallel"),
    )(slopes, qp, kp, kp, vp, vp)


def _head_expander():
    r = lax.broadcasted_iota(jnp.int32, (LANES, B_OUT), 0)
    c = lax.broadcasted_iota(jnp.int32, (LANES, B_OUT), 1)
    return jnp.logical_and(c >= r * HEAD_DIM, c < (r + 1) * HEAD_DIM).astype(BF16)


def _dil_merge(og, lseg, name, tr=256):
    def body(o_ref, l_ref, out_ref, lse_ref):
        l0, l1, l2 = l_ref[0], l_ref[1], l_ref[2]
        m = jnp.maximum(jnp.maximum(l0, l1), l2)
        w0, w1, w2 = jnp.exp(l0 - m), jnp.exp(l1 - m), jnp.exp(l2 - m)
        den = w0 + w1 + w2
        lse_ref[...] = m + jnp.log(den)
        expand = _head_expander()
        out = None
        for g, w in enumerate((w0, w1, w2)):
            part = _split_dot(w / den, expand, 3) * o_ref[g].astype(F32)
            out = part if out is None else out + part
        out_ref[...] = out.astype(BF16)

    blk3 = pl.BlockSpec((3, tr, B_OUT), lambda i: (0, i, 0))
    stat3 = pl.BlockSpec((3, tr, LANES), lambda i: (0, i, 0))
    blk = pl.BlockSpec((tr, B_OUT), lambda i: (i, 0))
    stat = pl.BlockSpec((tr, LANES), lambda i: (i, 0))
    return pl.pallas_call(
        body, name=name, grid=(S // tr,),
        out_shape=[jax.ShapeDtypeStruct((S, B_OUT), BF16), jax.ShapeDtypeStruct((S, LANES), F32)],
        in_specs=[blk3, stat3], out_specs=[blk, stat], compiler_params=_params("parallel"),
    )(og, lseg)


def _head_rowsum_compact(a, b, name, tr=256):
    def body(a_ref, b_ref, o_ref):
        r = lax.broadcasted_iota(jnp.int32, (B_OUT, LANES), 0)
        c = lax.broadcasted_iota(jnp.int32, (B_OUT, LANES), 1)
        collect = jnp.logical_and(r >= c * HEAD_DIM, r < (c + 1) * HEAD_DIM).astype(BF16)
        prod = a_ref[...].astype(F32) * b_ref[...].astype(F32)
        o_ref[...] = _split_dot(prod, collect, 2)

    row = pl.BlockSpec((tr, B_OUT), lambda i: (i, 0))
    return pl.pallas_call(
        body, name=name, grid=(S // tr,), out_shape=jax.ShapeDtypeStruct((S, LANES), F32),
        in_specs=[row, row], out_specs=pl.BlockSpec((tr, LANES), lambda i: (i, 0)),
        compiler_params=_params("parallel"),
    )(a, b)


def _dil_bwd(qp, kp, vp, dop, lsep, dlp, slopes, name, scatter):
    n_ex = len(scatter)

    def body(sl_ref, *refs):
        (qc_ref, qn_ref, kp_ref, kc_ref, vp_ref, vc_ref, doc_ref, don_ref,
         lc_ref, ln_ref, dc_ref, dn_ref) = refs[:12]
        dq_ref, dk_ref, dv_ref = refs[12 + n_ex:15 + n_ex]
        exchange = (refs[12:12 + n_ex], refs[15 + n_ex:15 + 2 * n_ex], refs[15 + 2 * n_ex:], True)
        g, n = pl.program_id(0), pl.program_id(1)

        @pl.when(jnp.logical_and(g == 0, n == 0))
        def _():
            for cp in _exchange_copies(*exchange):
                cp.start()

        nbs, dil = _group_consts(g)
        has_prev = (n % nbs) != 0
        has_next = jnp.logical_and(n + 1 < B_NB, ((n + 1) % nbs) != 0)
        lane, masks = _lane_masks()
        bias_c, ok_c, bias_p, ok_p = _band(dil)
        ok_pp = jnp.logical_and(ok_p, has_prev)
        ok_np = jnp.logical_and(ok_p, has_next)
        heads = [(hp, e) for hp in range(B_PAIRS) for e in range(2)]
        col = lambda ref, hp: ref[:, hp * LANES:(hp + 1) * LANES]
        mask = lambda t, e: jnp.where(masks[e], t, jnp.zeros_like(t))
        raw = []
        for hp, e in heads:
            qce, qne = mask(col(qc_ref, hp) * SCALE, e), mask(col(qn_ref, hp) * SCALE, e)
            doce, done = mask(col(doc_ref, hp), e), mask(col(don_ref, hp), e)
            kc, kpv, vc, vpv = col(kc_ref, hp), col(kp_ref, hp), col(vc_ref, hp), col(vp_ref, hp)
            raw.append(((_dot(qce, kc, NT_DIMS), _dot(doce, vc, NT_DIMS)),
                        (_dot(qce, kpv, NT_DIMS), _dot(doce, vpv, NT_DIMS)),
                        (_dot(qne, kc, NT_DIMS), _dot(done, vc, NT_DIMS))))
        pds = []
        for (hp, e), tiles in zip(heads, raw):
            lo = 2 * hp + e
            slope = sl_ref[g * 8 + 2 * hp + e]
            lse_c, dl_c = lc_ref[:, lo:lo + 1], dc_ref[:, lo:lo + 1]
            lse_n, dl_n = ln_ref[:, lo:lo + 1], dn_ref[:, lo:lo + 1]
            out = []
            for (s, dp), ok, bias, lse, dl in ((tiles[0], ok_c, bias_c, lse_c, dl_c),
                                               (tiles[1], ok_pp, bias_p, lse_c, dl_c),
                                               (tiles[2], ok_np, bias_p, lse_n, dl_n)):
                p = jnp.exp(jnp.where(ok, s - slope * bias, NEG) - lse)
                out.append((p.astype(BF16), (p * (dp - dl)).astype(BF16)))
            pds.append(out)
        dq_all, dk_all, dv_all = [], [], []
        for hp in range(B_PAIRS):
            dq = jnp.zeros((B_W, LANES), F32)
            dk = jnp.zeros((B_W, LANES), F32)
            dv = jnp.zeros((B_W, LANES), F32)
            for e in range(2):
                (p_c, ds_c), (_, ds_p), (p_n, ds_n) = pds[2 * hp + e]
                qce, qne = mask(col(qc_ref, hp) * SCALE, e), mask(col(qn_ref, hp) * SCALE, e)
                doce, done = mask(col(doc_ref, hp), e), mask(col(don_ref, hp), e)
                dq = dq + _dot(ds_c, mask(col(kc_ref, hp) * SCALE, e)) + _dot(ds_p, mask(col(kp_ref, hp) * SCALE, e))
                dk = dk + _dot(ds_c, qce, TN_DIMS) + _dot(ds_n, qne, TN_DIMS)
                dv = dv + _dot(p_c, doce, TN_DIMS) + _dot(p_n, done, TN_DIMS)
            dq_all.append(dq)
            dk_all.append(dk)
            dv_all.append(dv)
        dq_ref[...] = jnp.concatenate(dq_all, axis=1).astype(BF16)
        dk_ref[...] = jnp.concatenate(dk_all, axis=1).astype(BF16)
        dv_ref[...] = jnp.concatenate(dv_all, axis=1).astype(BF16)

        @pl.when(jnp.logical_and(g == 2, n == B_NB - 1))
        def _():
            for cp in _exchange_copies(*exchange):
                cp.wait()

    cur = pl.BlockSpec((None, B_W, B_OUT), lambda g, n, sl: (g, n, 0))
    prev = pl.BlockSpec((None, B_W, B_OUT), lambda g, n, sl: (g, jnp.maximum(n - 1, 0), 0))
    nxt = pl.BlockSpec((None, B_W, B_OUT), lambda g, n, sl: (g, jnp.minimum(n + 1, B_NB - 1), 0))
    stat_cur = pl.BlockSpec((None, B_W, LANES), lambda g, n, sl: (g, n, 0))
    stat_nxt = pl.BlockSpec((None, B_W, LANES), lambda g, n, sl: (g, jnp.minimum(n + 1, B_NB - 1), 0))
    return pl.pallas_call(
        body, name=name,
        grid_spec=pltpu.PrefetchScalarGridSpec(
            num_scalar_prefetch=1, grid=(3, B_NB),
            in_specs=[cur, nxt, prev, cur, prev, cur, cur, nxt, stat_cur, stat_nxt, stat_cur, stat_nxt]
            + [ANY] * n_ex,
            out_specs=[cur, cur, cur] + [ANY] * n_ex,
            scratch_shapes=_exchange_scratch(n_ex)),
        out_shape=[jax.ShapeDtypeStruct((3, S, B_OUT), BF16)] * 3 + _exchange_shapes(scatter, True),
        compiler_params=_params("arbitrary", "arbitrary"),
    )(slopes, qp, qp, kp, kp, vp, vp, dop, dop, lsep, lsep, dlp, dlp, *scatter)


def _rows_block(shape, max_bytes=2 * 1024 * 1024):
    rows, cols = shape
    padded_cols = -(-cols // LANES) * LANES
    for tr in (1024, 512, 256, 128, 64, 32, 16):
        if rows % tr == 0 and tr * padded_cols * 4 <= max_bytes:
            return tr
    return rows


def _adam_update(w, m, v, g):
    m_new = ADAM_B1 * m + (1.0 - ADAM_B1) * g
    v_new = ADAM_B2 * v + (1.0 - ADAM_B2) * (g * g)
    m_hat = m_new / (1.0 - ADAM_B1 ** ADAM_STEP)
    v_hat = v_new / (1.0 - ADAM_B2 ** ADAM_STEP)
    delta = -ADAM_LR * (m_hat / (jnp.sqrt(v_hat) + ADAM_EPS) + ADAM_WD * w)
    return delta, m_new, v_new


def _adamw_sharded(w, m, v, parts, name):
    R, C = w.shape
    tr = _rows_block((R, C), max_bytes=1024 * 1024)

    def body(w_ref, m_ref, v_ref, p_ref, g_ref, d_ref, mo_ref, vo_ref):
        g = p_ref[0].astype(F32)
        for dev in range(1, N_DEV):
            g = g + p_ref[dev].astype(F32)
        g_ref[...] = g
        d_ref[...], mo_ref[...], vo_ref[...] = _adam_update(w_ref[...], m_ref[...], v_ref[...], g)

    blk = pl.BlockSpec((tr, C), lambda i: (i, 0))
    out = jax.ShapeDtypeStruct((R, C), F32)
    return pl.pallas_call(
        body, name=name, grid=(R // tr,),
        in_specs=[blk, blk, blk, pl.BlockSpec((N_DEV, tr, C), lambda i: (0, i, 0))],
        out_specs=[blk, blk, blk, blk], out_shape=[out, out, out, out],
        compiler_params=_params("parallel"),
    )(w, m, v, parts)


def _adamw_replicated(w, m, v, parts, name):
    def body(w_ref, m_ref, v_ref, p_ref, g_ref, d_ref, mo_ref, vo_ref):
        g = p_ref[0]
        for dev in range(1, N_DEV):
            g = g + p_ref[dev]
        g_ref[...] = g
        d_ref[...], mo_ref[...], vo_ref[...] = _adam_update(w_ref[...], m_ref[...], v_ref[...], g)

    out = jax.ShapeDtypeStruct(w.shape, F32)
    return pl.pallas_call(body, name=name, out_shape=[out, out, out, out], compiler_params=_params())(w, m, v, parts)


def _cols_from_slots(g):
    return g.transpose(1, 0, 2).reshape(g.shape[1], N_DEV * g.shape[2])


def _cols_to_slots(w):
    k, n = w.shape
    return w.reshape(k, N_DEV, n // N_DEV).transpose(1, 0, 2)


def _permute(t, dil):
    c = t.shape[1]
    return t.reshape(S // dil, dil, c).transpose(1, 0, 2).reshape(S, c)


def _unpermute(t, dil):
    c = t.shape[1]
    return t.reshape(dil, S // dil, c).transpose(1, 0, 2).reshape(S, c)


def _group_permute(t):
    return jnp.stack([_permute(t[:, g * B_OUT:(g + 1) * B_OUT], B_DILS[g]) for g in range(3)])


def _same_permute(t):
    return jnp.stack([_permute(t, d) for d in B_DILS])


def _group_unpermute(t):
    return jnp.stack([_unpermute(t[g], B_DILS[g]) for g in range(3)])


SMALL_ROWS = 144


def _pack_small(a_b_f, kv_g, mix_g, ffn_g, conv_b, fin_g):
    flat = jnp.concatenate([a_b_f.reshape(-1), kv_g.reshape(-1), mix_g.reshape(-1), ffn_g.reshape(-1),
                            conv_b.reshape(-1), fin_g.reshape(-1)])
    return jnp.pad(flat, (0, SMALL_ROWS * LANES - flat.shape[0])).reshape(SMALL_ROWS, LANES)


def _unpack_small(p):
    flat = p.reshape(-1)
    out, off = [], 0
    for shape in ((1, A_HEADS), (D,), (2, D), (2, D), (2, 2 * D_FF), (D,)):
        size = math.prod(shape)
        out.append(flat[off:off + size].reshape(shape))
        off += size
    return out


def _unpack_late(g):
    half = N_DEV // 2
    up = g[4].reshape(N_DEV, 2, D, -1)
    w_up_a = [up[:half, l].transpose(1, 0, 2).reshape(D, D_FF) for l in range(2)]
    w_up_g = [up[half:, l].transpose(1, 0, 2).reshape(D, D_FF) for l in range(2)]
    w_down = [g[5].reshape(N_DEV, 2, -1, D)[:, l].reshape(D_FF, D) for l in range(2)]
    conv_w = [g[6].reshape(N_DEV, 2, 3, -1)[:, l].transpose(1, 0, 2).reshape(3, 2 * D_FF) for l in range(2)]
    return (g[0].reshape(D, D), _cols_from_slots(g[1]), _cols_from_slots(g[2]), _cols_from_slots(g[3]),
            w_up_a, w_up_g, w_down, conv_w)


def _ffn_slots(dw_up, dw_down, dconv_w):
    return [_cols_to_slots(dw_up), dw_down.reshape(N_DEV, -1, D), _cols_to_slots(dconv_w)]


def _local_step(x0, target, w_in_pad, late_shards,
                a_b_f, kv_norm_g, mix_norm_g, ffn_norm_g, ffn_conv_b, final_norm_g):
    w_qkv, w_f = w_in_pad[:, :A_QKV], w_in_pad[:, A_QKV:]
    conv_b = ffn_conv_b.reshape(2, 1, 2 * D_FF)
    slopes = jnp.exp2(-8.0 * jnp.arange(1, 25, dtype=F32) / 24)

    def gain(g):
        return g.reshape(1, D)

    (h1,) = _rmsnorm_fwd(x0, [gain(mix_norm_g[0])], "norm_mix0")
    qkv = _matmul(h1, w_qkv, mode="nn", out_dtype=BF16, name="proj_qkv", tm=512, tn=A_QKV)
    z = _matmul(h1, w_f, mode="nn", out_dtype=F32, name="proj_gate", tm=S, tn=LANES)
    z_t = z[:, :A_HEADS].T
    b_f = a_b_f.reshape(A_HEADS, 1)
    c_t = _fox_prep_fwd(z_t, b_f, "fox_prep")
    c_t2 = c_t.reshape(N_PAIRS, 2, S)
    o_a, lse_a, *late = _fox_fwd(qkv, c_t2, "fox_fwd", late_shards)
    w_out, w_q, w_bo, w_kvf, w_up_a, w_up_g, w_down, conv_w = _unpack_late(late)
    x1 = _matmul(o_a, w_out, mode="nn", out_dtype=F32, name="a_out", tm=512, tn=D, res=x0)

    def ffn_fwd(xin, layer):
        (h,) = _rmsnorm_fwd(xin, [gain(ffn_norm_g[layer])], f"norm_ffn{layer}")
        u = (_matmul(h, w_up_a[layer], mode="nn", out_dtype=BF16, name=f"ffn_up_a{layer}", tm=512, tn=D_FF),
             _matmul(h, w_up_g[layer], mode="nn", out_dtype=BF16, name=f"ffn_up_g{layer}", tm=512, tn=D_FF))
        act = _convgate_fwd(*u, conv_w[layer], conv_b[layer], f"convgate{layer}")
        xout = _matmul(act, w_down[layer], mode="nn", out_dtype=F32, name=f"ffn_down{layer}", tm=512, tn=D, res=xin)
        return h, u, act, xout

    h2, u0, act0, x2 = ffn_fwd(x1, 0)
    hk, h3 = _rmsnorm_fwd(x2, [gain(kv_norm_g), gain(mix_norm_g[1])], "norm_kv_mix1")
    kv = _matmul(hk, w_kvf, mode="nn", out_dtype=BF16, name="proj_kv", tm=512, tn=B_KV)
    qb = _matmul(h3, w_q, mode="nn", out_dtype=BF16, name="proj_qb", tm=512, tn=B_Q)
    qp, kp, vp = _group_permute(qb), _group_permute(kv[:, :B_Q]), _group_permute(kv[:, B_Q:])
    og_p, lseg_p = _dil_fwd(qp, kp, vp, slopes, "dil_fwd")
    o_b, lse_b = _dil_merge(_group_unpermute(og_p), _group_unpermute(lseg_p), "dil_merge")
    x3 = _matmul(o_b, w_bo, mode="nn", out_dtype=F32, name="b_out", tm=512, tn=D, res=x2)
    h4, u1, act1, x4 = ffn_fwd(x3, 1)
    loss_blk, dx4, dx4b, dg_final = _final_loss(x4, target, gain(final_norm_g), "final_loss")

    def ffn_bwd(dx, dxb, xin, h, u, act, layer):
        dact = _matmul(dxb, w_down[layer], mode="nt", out_dtype=BF16, name=f"d_act{layer}", tm=512, tn=D_FF)
        dw_down = _matmul_tn(act, dxb, out_dtype=BF16, name=f"dw_down{layer}")
        du_a, du_g, dwa, dwg, dba, dbg = _convgate_bwd(*u, conv_w[layer], conv_b[layer], dact, f"convgate_bwd{layer}")
        dw_up = jnp.concatenate(
            [_matmul_tn(h, du_a, out_dtype=BF16, name=f"dw_up_a{layer}"),
             _matmul_tn(h, du_g, out_dtype=BF16, name=f"dw_up_g{layer}")], axis=1)
        dh = _matmul(du_a, w_up_a[layer], mode="nt", out_dtype=F32, name=f"dh_ffn_a{layer}", tm=512, tn=D)
        dh = _matmul(du_g, w_up_g[layer], mode="nt", out_dtype=F32, name=f"dh_ffn_g{layer}", tm=512, tn=D, res=dh)
        dxin, dxinb, dgain = _rmsnorm_bwd(xin, dh, gain(ffn_norm_g[layer]), dx, f"norm_ffn_bwd{layer}")
        dconv_w = jnp.concatenate([dwa, dwg], axis=1)
        dconv_b = jnp.concatenate([dba, dbg], axis=1)
        return dxin, dxinb, dgain, dw_up, dw_down, dconv_w, dconv_b

    dx3, dx3b, dg_ffn1, dw_up1, dw_down1, dconv_w1, dconv_b1 = ffn_bwd(dx4, dx4b, x3, h4, u1, act1, 1)

    do_b = _matmul(dx3b, w_bo, mode="nt", out_dtype=BF16, name="d_ob", tm=1024, tn=B_OUT)
    dw_bo = _matmul_tn(o_b, dx3b, out_dtype=BF16, name="dw_bo")
    dl_b = _head_rowsum_compact(do_b, o_b, "delta_b")
    slots_up1, slots_down1, slots_conv1 = _ffn_slots(dw_up1, dw_down1, dconv_w1)
    dqp, dkp, dvp, land_down1, land_conv1 = _dil_bwd(
        qp, kp, vp, _same_permute(do_b), _same_permute(lse_b), _same_permute(dl_b), slopes, "dil_bwd",
        [slots_down1, slots_conv1])

    def natural(tp):
        return jnp.concatenate([_unpermute(tp[g], B_DILS[g]) for g in range(3)], axis=1)

    dqb = natural(dqp)
    dkv = jnp.concatenate([natural(dkp), natural(dvp)], axis=1)
    dw_q = _matmul_tn(h3, dqb, out_dtype=BF16, name="dw_q")
    dw_kv = _matmul_tn(hk, dkv, out_dtype=BF16, name="dw_kv")
    dh3 = _matmul(dqb, w_q, mode="nt", out_dtype=F32, name="dh_mix1", tm=512, tn=D)
    dhk = _matmul(dkv, w_kvf, mode="nt", out_dtype=F32, name="dh_kv", tm=512, tn=D)
    dx2, _, dg_mix1 = _rmsnorm_bwd(x2, dh3, gain(mix_norm_g[1]), dx3, "norm_mix1_bwd")
    dx2, dx2b, dg_kv = _rmsnorm_bwd(x2, dhk, gain(kv_norm_g), dx2, "norm_kv_bwd")

    dx1, dx1b, dg_ffn0, dw_up0, dw_down0, dconv_w0, dconv_b0 = ffn_bwd(dx2, dx2b, x1, h2, u0, act0, 0)

    do_a = _matmul(dx1b, w_out, mode="nt", out_dtype=BF16, name="d_oa", tm=512, tn=D)
    dw_out = _matmul_tn(o_a, dx1b, out_dtype=BF16, name="dw_out")
    dl_a = _head_rowsum(do_a, o_a, "delta_a")
    dq_a, dk_a, dv_a, dcol, drow, *land = _fox_bwd(
        qkv, do_a, lse_a, dl_a, c_t2, "fox_bwd",
        [dw_out.reshape(N_DEV, D // N_DEV, D), _cols_to_slots(dw_q), _cols_to_slots(dw_bo), _cols_to_slots(dw_kv)]
        + _ffn_slots(dw_up0, dw_down0, dconv_w0) + [slots_up1])
    land_out, land_q, land_bo, land_kv, land_up0, land_down0, land_conv0, land_up1 = land

    def head_sums(t):
        return t.reshape(S, N_PAIRS, 2, HEAD_DIM)[:, :, ::-1, 0].reshape(S, A_HEADS).T

    dz_t, db_f = _fox_prep_bwd(head_sums(drow), head_sums(dcol), z_t, b_f, "fox_prep_bwd")
    dz = jnp.pad(dz_t.T, ((0, 0), (0, LANES - A_HEADS))).astype(BF16)
    dproj = jnp.concatenate([dq_a.astype(BF16), dk_a, dv_a, dz], axis=1)
    dw_in = _matmul_tn(h1, dproj, out_dtype=BF16, name="dw_in")
    dh1, land_in = _matmul(dproj, w_in_pad, mode="nt", out_dtype=F32, name="dh_mix0", tm=512, tn=D,
                           scatter=[_cols_to_slots(dw_in[:, :A_QKV + A_HEADS])])
    grad_x, _, dg_mix0 = _rmsnorm_bwd(x0, dh1, gain(mix_norm_g[0]), dx1, "norm_mix0_bwd")

    dg_mix = jnp.concatenate([dg_mix0, dg_mix1], axis=0)
    dg_ffn = jnp.concatenate([dg_ffn0, dg_ffn1], axis=0)
    dconv_b = jnp.concatenate([dconv_b0, dconv_b1], axis=0)
    small_part = _pack_small(db_f, dg_kv, dg_mix, dg_ffn, dconv_b, dg_final)
    _, (small_parts,) = _final_exchange([], [small_part], "gather_small_grads")
    landed = [land_in, land_out, land_q, land_bo, land_kv, land_up0, land_up1, land_down0, land_down1,
              land_conv0, land_conv1]
    return loss_blk, grad_x, landed, small_parts


def kernel(x, a_w_in, a_b_f, a_w_out, b_w_q, b_w_out, kv_norm_g, w_kv, mix_norm_g, ffn_norm_g, ffn_w_up, ffn_conv_w, ffn_conv_b, ffn_w_down, final_norm_g, loss_target, m_a_w_in, m_a_b_f, m_a_w_out, m_b_w_q, m_b_w_out, m_kv_norm_g, m_w_kv, m_mix_norm_g, m_ffn_norm_g, m_ffn_w_up, m_ffn_conv_w, m_ffn_conv_b, m_ffn_w_down, m_final_norm_g, v_a_w_in, v_a_b_f, v_a_w_out, v_b_w_q, v_b_w_out, v_kv_norm_g, v_w_kv, v_mix_norm_g, v_ffn_norm_g, v_ffn_w_up, v_ffn_conv_w, v_ffn_conv_b, v_ffn_w_down, v_final_norm_g):
    def shards(a_w_in, a_w_out, b_w_q, b_w_out, w_kv, ffn_w_up, ffn_w_down, ffn_conv_w):
        return [a_w_in[0], a_w_out[0], b_w_q[0], b_w_out[0], w_kv, ffn_w_up[0], ffn_w_up[1],
                ffn_w_down[0], ffn_w_down[1], ffn_conv_w[0], ffn_conv_w[1]]

    w_loc = shards(a_w_in, a_w_out, b_w_q, b_w_out, w_kv, ffn_w_up, ffn_w_down, ffn_conv_w)
    m_loc = shards(m_a_w_in, m_a_w_out, m_b_w_q, m_b_w_out, m_w_kv, m_ffn_w_up, m_ffn_w_down, m_ffn_conv_w)
    v_loc = shards(v_a_w_in, v_a_w_out, v_b_w_q, v_b_w_out, v_w_kv, v_ffn_w_up, v_ffn_w_down, v_ffn_conv_w)

    (g_in,) = _all_gather([a_w_in[0].astype(BF16)], "gather_a_w_in")
    w_in = _cols_from_slots(g_in)
    w_in_pad = jnp.pad(w_in, ((0, 0), (0, A_PROJ_PAD - w_in.shape[1])))
    late_shards = [a_w_out[0].astype(BF16), b_w_q[0].astype(BF16), b_w_out[0].astype(BF16), w_kv.astype(BF16),
                   ffn_w_up.reshape(2 * D, -1).astype(BF16), ffn_w_down.reshape(-1, D).astype(BF16),
                   ffn_conv_w.reshape(6, -1)]

    loss_blk, grad_x, landed, small_parts = _local_step(
        x[0], loss_target[0], w_in_pad, late_shards,
        a_b_f, kv_norm_g, mix_norm_g, ffn_norm_g, ffn_conv_b, final_norm_g)

    big = [_adamw_sharded(w_loc[k], m_loc[k], v_loc[k], landed[k], f"adamw{k}") for k in range(11)]

    small = _adamw_replicated(
        _pack_small(a_b_f, kv_norm_g, mix_norm_g, ffn_norm_g, ffn_conv_b, final_norm_g),
        _pack_small(m_a_b_f, m_kv_norm_g, m_mix_norm_g, m_ffn_norm_g, m_ffn_conv_b, m_final_norm_g),
        _pack_small(v_a_b_f, v_kv_norm_g, v_mix_norm_g, v_ffn_norm_g, v_ffn_conv_b, v_final_norm_g),
        small_parts, "adamw_small")

    loss = lax.psum(loss_blk[0, 0], ("x", "y", "c"))

    def assemble(kind):
        b = [r[kind] for r in big]
        s_abf, s_kv, s_mix, s_ffn, s_cb, s_fin = _unpack_small(small[kind])
        return [b[0][None], s_abf, b[1][None], b[2][None], b[3][None], s_kv, b[4], s_mix, s_ffn,
                jnp.stack([b[5], b[6]]), jnp.stack([b[9], b[10]]), s_cb, jnp.stack([b[7], b[8]]), s_fin]

    return (loss, grad_x[None], *assemble(0), *assemble(1), *assemble(2), *assemble(3))
```

```python
import functools
import math

import jax
import jax.numpy as jnp
from jax import lax
from jax.experimental import pallas as pl
from jax.experimental.pallas import tpu as pltpu

F32 = jnp.float32
BF16 = jnp.bfloat16

S = 4096
D = 1024
N_DEV = 8
A_HEADS = 16
HEAD_DIM = 64
A_QKV = 3072
A_PROJ_PAD = 3200
B_Q = 1536
B_OUT = 512
B_KV = 3072
B_W = 128
B_DILS = (1, 4, 16)
D_FF = 2816
RMS_EPS = 1e-6
SCALE = HEAD_DIM ** -0.5
NEG = -1e30

ADAM_LR = 0.001
ADAM_B1 = 0.9
ADAM_B2 = 0.999
ADAM_EPS = 1e-08
ADAM_WD = 0.01
ADAM_STEP = 10

LANES = 128
VMEM_LIMIT = 56 * 1024 * 1024
MESH = pl.DeviceIdType.MESH
ANY = pl.BlockSpec(memory_space=pl.ANY)

NT_DIMS = (((1,), (1,)), ((), ()))
TN_DIMS = (((0,), (0,)), ((), ()))
NN_DIMS = (((1,), (0,)), ((), ()))


def _params(*sem):
    return pltpu.CompilerParams(dimension_semantics=sem if sem else None, vmem_limit_bytes=VMEM_LIMIT)


def _dot(a, b, dims=NN_DIMS):
    return lax.dot_general(a, b, dims, preferred_element_type=F32)


def _split_dot(x, mat, pieces):
    out = None
    rem = x
    for _ in range(pieces):
        part = rem.astype(BF16)
        rem = rem - part.astype(F32)
        d = _dot(part, mat)
        out = d if out is None else out + d
    return out


def _pick(n, prefs):
    for p in prefs:
        if n % p == 0:
            return p
    return n


def _gather_phases(ins, outs, sems):
    n = len(ins)
    if n == 0:
        return (lambda: None,) * 3
    send_sems, recv_sems, local_sems = sems
    x, y, c = lax.axis_index("x"), lax.axis_index("y"), lax.axis_index("c")
    me, sibling = (x, y, c), (x, y, 1 - c)
    chips = [(1 - x, y), (x, 1 - y), (1 - x, 1 - y)]

    def slot(a, px, py, pc):
        return outs[a].at[4 * px + 2 * py + pc]

    def copy(a, k, block, to, src=None):
        return pltpu.make_async_remote_copy(
            src_ref=slot(a, *block) if src is None else src, dst_ref=slot(a, *block),
            send_sem=send_sems.at[a, k], recv_sem=recv_sems.at[a, k],
            device_id=to, device_id_type=MESH)

    mine = [pltpu.make_async_copy(ins[a], slot(a, *me), local_sems.at[a]) for a in range(n)]
    first = []
    for a in range(n):
        first.append(copy(a, 0, me, sibling, src=ins[a]))
        first += [copy(a, 1 + j, me, (*chip, c), src=ins[a]) for j, chip in enumerate(chips)]
    passed = [copy(a, 4 + j, (*chip, c), sibling) for j, chip in enumerate(chips) for a in range(n)]

    def start():
        for cp in mine + first:
            cp.start()

    def forward():
        k = 0
        for j, chip in enumerate(chips):
            for a in range(n):
                copy(a, 1 + j, (*chip, c), me).wait_recv()
                passed[k].start()
                k += 1

    def finish():
        for a in range(n):
            copy(a, 0, sibling, me).wait_recv()
            for j, chip in enumerate(chips):
                copy(a, 4 + j, (*chip, 1 - c), me).wait_recv()
        for cp in first + passed:
            cp.wait_send()
        for cp in mine:
            cp.wait()

    return start, forward, finish


def _all_gather(arrays, name):
    n = len(arrays)

    def body(*refs):
        for phase in _gather_phases(refs[:n], refs[n:2 * n], refs[2 * n:]):
            phase()

    return pl.pallas_call(
        body, name=name,
        out_shape=[jax.ShapeDtypeStruct((N_DEV,) + a.shape, a.dtype) for a in arrays],
        in_specs=[ANY] * n, out_specs=[ANY] * n,
        scratch_shapes=[pltpu.SemaphoreType.DMA((n, 7)), pltpu.SemaphoreType.DMA((n, 7)),
                        pltpu.SemaphoreType.DMA((n,))],
    )(*arrays)


PEER_FLIPS = [(dx, dy, dc) for dx in (0, 1) for dy in (0, 1) for dc in (0, 1) if (dx, dy, dc) != (0, 0, 0)]


def _exchange_copies(ins, outs, sems, scatter):
    if not ins:
        return []
    send_sems, recv_sems, local_sems = sems
    x, y, c = lax.axis_index("x"), lax.axis_index("y"), lax.axis_index("c")
    me = 4 * x + 2 * y + c
    copies = []
    for a in range(len(ins)):
        copies.append(pltpu.make_async_copy(ins[a].at[me] if scatter else ins[a], outs[a].at[me], local_sems.at[a]))
        for k, (dx, dy, dc) in enumerate(PEER_FLIPS):
            px, py, pc = (1 - x if dx else x), (1 - y if dy else y), (1 - c if dc else c)
            copies.append(pltpu.make_async_remote_copy(
                src_ref=ins[a].at[4 * px + 2 * py + pc] if scatter else ins[a], dst_ref=outs[a].at[me],
                send_sem=send_sems.at[a, k], recv_sem=recv_sems.at[a, k],
                device_id=(px, py, pc), device_id_type=MESH))
    return copies


def _exchange_scratch(n):
    if n == 0:
        return []
    return [pltpu.SemaphoreType.DMA((n, 7)), pltpu.SemaphoreType.DMA((n, 7)), pltpu.SemaphoreType.DMA((n,))]


def _exchange_shapes(arrays, scatter):
    return [jax.ShapeDtypeStruct((N_DEV,) + (a.shape[1:] if scatter else a.shape), a.dtype) for a in arrays]


def _final_exchange(scatter, gather, name):
    ns, ng = len(scatter), len(gather)

    def body(*refs):
        ins, outs, sems = refs[:ns + ng], refs[ns + ng:2 * (ns + ng)], refs[2 * (ns + ng):]
        n_sems = len(_exchange_scratch(ns))
        copies = (_exchange_copies(ins[:ns], outs[:ns], sems[:n_sems], True)
                  + _exchange_copies(ins[ns:], outs[ns:], sems[n_sems:], False))
        for cp in copies:
            cp.start()
        for cp in copies:
            cp.wait()

    res = pl.pallas_call(
        body, name=name, out_shape=_exchange_shapes(scatter, True) + _exchange_shapes(gather, False),
        in_specs=[ANY] * (ns + ng), out_specs=[ANY] * (ns + ng),
        scratch_shapes=_exchange_scratch(ns) + _exchange_scratch(ng),
    )(*scatter, *gather)
    return res[:ns], res[ns:]


MM_ROWS = 512
MM_COLS = 1024


def _matmul(a, b, *, mode, out_dtype, name, tm, tn, res=None, scatter=()):
    if mode == "nn":
        (M, K), (K2, N) = a.shape, b.shape
    else:
        (M, K), (N, K2) = a.shape, b.shape
    assert K == K2, (a.shape, b.shape, mode)
    tm, tn = min(tm, M), min(tn, N)
    sm = min(tm, MM_ROWS)
    sn = tn if tn <= MM_COLS else _pick(tn, (512, 256, 128))
    assert M % tm == 0 and N % tn == 0 and tm % sm == 0, (M, N, K, tm, tn)
    dims = NN_DIMS if mode == "nn" else NT_DIMS
    a_spec = pl.BlockSpec((tm, K), lambda i, j: (i, 0))
    if mode == "nt":
        b_spec = pl.BlockSpec((tn, K), lambda i, j: (j, 0))
    else:
        b_spec = pl.BlockSpec((K, tn), lambda i, j: (0, j))
    o_spec = pl.BlockSpec((tm, tn), lambda i, j: (i, j))
    has_res = res is not None
    n_in, n_ex = 2 + has_res, len(scatter)
    gm, gn = M // tm, N // tn

    def body(*refs):
        a_ref, b_ref = refs[0], refs[1]
        r_ref = refs[2] if has_res else None
        o_ref = refs[n_in + n_ex]
        exchange = (refs[n_in:n_in + n_ex], refs[n_in + n_ex + 1:n_in + 2 * n_ex + 1], refs[n_in + 2 * n_ex + 1:], True)

        @pl.when(jnp.logical_and(pl.program_id(0) == 0, pl.program_id(1) == 0))
        def _():
            for cp in _exchange_copies(*exchange):
                cp.start()

        def chunk(r, carry):
            rows = pl.ds(pl.multiple_of(r * sm, sm), sm)
            av = a_ref[rows, :]
            for c0 in range(0, tn, sn):
                bv = b_ref[c0:c0 + sn, :] if mode == "nt" else b_ref[:, c0:c0 + sn]
                total = _dot(av, bv, dims)
                if has_res:
                    total = total + r_ref[rows, c0:c0 + sn]
                o_ref[rows, c0:c0 + sn] = total.astype(out_dtype)
            return carry

        lax.fori_loop(0, tm // sm, chunk, 0)

        @pl.when(jnp.logical_and(pl.program_id(0) == gm - 1, pl.program_id(1) == gn - 1))
        def _():
            for cp in _exchange_copies(*exchange):
                cp.wait()

    out = pl.pallas_call(
        body, name=name, grid=(gm, gn),
        out_shape=[jax.ShapeDtypeStruct((M, N), out_dtype)] + _exchange_shapes(scatter, True),
        in_specs=[a_spec, b_spec] + ([o_spec] if has_res else []) + [ANY] * n_ex,
        out_specs=[o_spec] + [ANY] * n_ex,
        scratch_shapes=_exchange_scratch(n_ex),
        compiler_params=_params("arbitrary", "arbitrary"),
    )(*((a, b, res) if has_res else (a, b)), *scatter)
    return out if n_ex else out[0]


def _matmul_tn(a, b, *, out_dtype, name, tk=512, sm=256):
    (K, M), (K2, N) = a.shape, b.shape
    assert K == K2 and K % tk == 0 and M % sm == 0, (a.shape, b.shape)
    nk = K // tk

    def body(a_ref, b_ref, o_ref, acc_ref):
        k = pl.program_id(0)

        @pl.when(k == 0)
        def _():
            acc_ref[...] = jnp.zeros_like(acc_ref)

        def chunk(mi, carry):
            cols = pl.ds(pl.multiple_of(mi * sm, sm), sm)
            acc_ref[cols, :] += _dot(a_ref[:, cols].T, b_ref[...])
            return carry

        lax.fori_loop(0, M // sm, chunk, 0)

        @pl.when(k == nk - 1)
        def _():
            def emit(mi, carry):
                rows = pl.ds(pl.multiple_of(mi * sm, sm), sm)
                o_ref[rows, :] = acc_ref[rows, :].astype(out_dtype)
                return carry
            lax.fori_loop(0, M // sm, emit, 0)

    return pl.pallas_call(
        body, name=name, grid=(nk,),
        out_shape=jax.ShapeDtypeStruct((M, N), out_dtype),
        in_specs=[pl.BlockSpec((tk, M), lambda k: (k, 0)), pl.BlockSpec((tk, N), lambda k: (k, 0))],
        out_specs=pl.BlockSpec((M, N), lambda k: (0, 0)),
        scratch_shapes=[pltpu.VMEM((M, N), F32)],
        compiler_params=_params("arbitrary"),
    )(a, b)


def _rmsnorm_fwd(x, gains, name, tr=256):
    n = len(gains)

    def body(*refs):
        x_ref = refs[0]
        xv = x_ref[...]
        r = lax.rsqrt(jnp.mean(xv * xv, axis=-1, keepdims=True) + RMS_EPS)
        y = xv * r
        for a in range(n):
            refs[1 + n + a][...] = (y * refs[1 + a][...]).astype(BF16)

    row = pl.BlockSpec((tr, D), lambda i: (i, 0))
    gain = pl.BlockSpec((1, D), lambda i: (0, 0))
    return pl.pallas_call(
        body, name=name, grid=(S // tr,),
        out_shape=[jax.ShapeDtypeStruct((S, D), BF16)] * n,
        in_specs=[row] + [gain] * n, out_specs=[row] * n,
        compiler_params=_params("parallel"),
    )(x, *gains)


def _rmsnorm_bwd(x, dy, g, dres, name, tr=256):
    def body(x_ref, dy_ref, g_ref, dres_ref, dx_ref, dxb_ref, dg_ref):
        xv = x_ref[...]
        dyv = dy_ref[...]
        r = lax.rsqrt(jnp.mean(xv * xv, axis=-1, keepdims=True) + RMS_EPS)
        xhat = xv * r
        dxhat = dyv * g_ref[...]
        mean_term = jnp.mean(dxhat * xhat, axis=-1, keepdims=True)
        dx = r * (dxhat - xhat * mean_term) + dres_ref[...]
        dx_ref[...] = dx
        dxb_ref[...] = dx.astype(BF16)
        part = jnp.sum(dyv * xhat, axis=0, keepdims=True)

        @pl.when(pl.program_id(0) == 0)
        def _():
            dg_ref[...] = part

        @pl.when(pl.program_id(0) > 0)
        def _():
            dg_ref[...] += part

    row = pl.BlockSpec((tr, D), lambda i: (i, 0))
    gain = pl.BlockSpec((1, D), lambda i: (0, 0))
    return pl.pallas_call(
        body, name=name, grid=(S // tr,),
        out_shape=[jax.ShapeDtypeStruct((S, D), F32), jax.ShapeDtypeStruct((S, D), BF16),
                   jax.ShapeDtypeStruct((1, D), F32)],
        in_specs=[row, row, gain, row], out_specs=[row, row, gain],
        compiler_params=_params("arbitrary"),
    )(x, dy, g, dres)


def _matmul_norm_bwd(a, b, x, g, dres, name, res=None, tm=512):
    (M, K), has_res = a.shape, res is not None

    def body(*refs):
        a_ref, b_ref, x_ref, g_ref, dres_ref = refs[:5]
        dx_ref, dxb_ref, dg_ref = refs[5 + has_res:]
        dyv = _dot(a_ref[...], b_ref[...], NT_DIMS)
        if has_res:
            dyv = dyv + refs[5][...]
        xv = x_ref[...]
        r = lax.rsqrt(jnp.mean(xv * xv, axis=-1, keepdims=True) + RMS_EPS)
        xhat = xv * r
        dxhat = dyv * g_ref[...]
        mean_term = jnp.mean(dxhat * xhat, axis=-1, keepdims=True)
        dx = r * (dxhat - xhat * mean_term) + dres_ref[...]
        dx_ref[...] = dx
        dxb_ref[...] = dx.astype(BF16)
        part = jnp.sum(dyv * xhat, axis=0, keepdims=True)

        @pl.when(pl.program_id(0) == 0)
        def _():
            dg_ref[...] = part

        @pl.when(pl.program_id(0) > 0)
        def _():
            dg_ref[...] += part

    row = pl.BlockSpec((tm, D), lambda i: (i, 0))
    gain = pl.BlockSpec((1, D), lambda i: (0, 0))
    return pl.pallas_call(
        body, name=name, grid=(M // tm,),
        out_shape=[jax.ShapeDtypeStruct((M, D), F32), jax.ShapeDtypeStruct((M, D), BF16),
                   jax.ShapeDtypeStruct((1, D), F32)],
        in_specs=[pl.BlockSpec((tm, K), lambda i: (i, 0)), pl.BlockSpec((D, K), lambda i: (0, 0)), row, gain, row]
        + ([row] if has_res else []),
        out_specs=[row, row, gain],
        compiler_params=_params("arbitrary"),
    )(*((a, b, x, g, dres) + ((res,) if has_res else ())))


def _final_loss(x, target, g, name, tr=256):
    def body(x_ref, t_ref, g_ref, loss_ref, dx_ref, dxb_ref, dg_ref):
        xv = x_ref[...]
        gv = g_ref[...]
        r = lax.rsqrt(jnp.mean(xv * xv, axis=-1, keepdims=True) + RMS_EPS)
        xhat = xv * r
        err = xhat * gv - t_ref[...]
        row_loss = jnp.mean(err * err, axis=-1, keepdims=True)
        lpart = 0.5 * jnp.sum(row_loss, axis=0, keepdims=True)
        dyv = err / D
        dxhat = dyv * gv
        mean_term = jnp.mean(dxhat * xhat, axis=-1, keepdims=True)
        dx = r * (dxhat - xhat * mean_term)
        dx_ref[...] = dx
        dxb_ref[...] = dx.astype(BF16)
        gpart = jnp.sum(dyv * xhat, axis=0, keepdims=True)

        @pl.when(pl.program_id(0) == 0)
        def _():
            dg_ref[...] = gpart
            loss_ref[...] = jnp.broadcast_to(lpart, loss_ref.shape)

        @pl.when(pl.program_id(0) > 0)
        def _():
            dg_ref[...] += gpart
            loss_ref[...] += jnp.broadcast_to(lpart, loss_ref.shape)

    row = pl.BlockSpec((tr, D), lambda i: (i, 0))
    gain = pl.BlockSpec((1, D), lambda i: (0, 0))
    lspec = pl.BlockSpec((8, LANES), lambda i: (0, 0))
    return pl.pallas_call(
        body, name=name, grid=(S // tr,),
        out_shape=[jax.ShapeDtypeStruct((8, LANES), F32), jax.ShapeDtypeStruct((S, D), F32),
                   jax.ShapeDtypeStruct((S, D), BF16), jax.ShapeDtypeStruct((1, D), F32)],
        in_specs=[row, row, gain], out_specs=[lspec, row, row, gain],
        compiler_params=_params("arbitrary"),
    )(x, target, g)


CONV_TR = 128
CONV_TC = D_FF
CONV_NJ = D_FF // CONV_TC
HALO = 16


def _causal_taps(cur_ref, prev_ref, first):
    xv = cur_ref[...].astype(F32)
    pv = prev_ref[...].astype(F32)
    p1 = jnp.where(first, 0.0, pv[HALO - 1:HALO, :])
    p2 = jnp.where(first, 0.0, pv[HALO - 2:HALO - 1, :])
    r1, r2 = pltpu.roll(xv, 1, 0), pltpu.roll(xv, 2, 0)
    row = lax.broadcasted_iota(jnp.int32, (8, xv.shape[1]), 0)
    xm1 = jnp.concatenate([jnp.where(row == 0, p1, r1[0:8]), r1[8:]], axis=0)
    xm2 = jnp.concatenate([jnp.where(row == 0, p2, jnp.where(row == 1, p1, r2[0:8])), r2[8:]], axis=0)
    return xv, xm1, xm2


def _conv_specs():
    def prev_row(i):
        return jnp.maximum(i * (CONV_TR // HALO) - 1, 0)
    ua = pl.BlockSpec((CONV_TR, CONV_TC), lambda i, j: (i, j))
    ug = ua
    pa = pl.BlockSpec((HALO, CONV_TC), lambda i, j: (prev_row(i), j))
    pg = pa
    wa = pl.BlockSpec((3, CONV_TC), lambda i, j: (0, j))
    wg = pl.BlockSpec((3, CONV_TC), lambda i, j: (0, j + CONV_NJ))
    ba = pl.BlockSpec((1, CONV_TC), lambda i, j: (0, j))
    bg = pl.BlockSpec((1, CONV_TC), lambda i, j: (0, j + CONV_NJ))
    return [ua, pa, ug, pg, wa, wg, ba, bg]


def _convgate_fwd(u_a, u_g, w, b, name):
    def body(ua, pa, ug, pg, wa, wg, ba, bg, o_ref):
        first = pl.program_id(0) == 0
        x0, x1, x2 = _causal_taps(ua, pa, first)
        ac = wa[0:1, :] * x2 + wa[1:2, :] * x1 + wa[2:3, :] * x0 + ba[...]
        x0, x1, x2 = _causal_taps(ug, pg, first)
        gc = wg[0:1, :] * x2 + wg[1:2, :] * x1 + wg[2:3, :] * x0 + bg[...]
        sg = 0.5 * jnp.tanh(0.5 * gc) + 0.5
        o_ref[...] = (gc * sg * ac).astype(BF16)

    return pl.pallas_call(
        body, name=name, grid=(S // CONV_TR, CONV_NJ),
        out_shape=jax.ShapeDtypeStruct((S, D_FF), BF16),
        in_specs=_conv_specs(),
        out_specs=pl.BlockSpec((CONV_TR, CONV_TC), lambda i, j: (i, j)),
        compiler_params=_params("parallel", "parallel"),
    )(u_a, u_a, u_g, u_g, w, w, b, b)


def _anticausal_conv(d, nxt_ref, w_ref, last):
    n1 = jnp.where(last, 0.0, nxt_ref[0:1, :])
    n2 = jnp.where(last, 0.0, nxt_ref[1:2, :])
    r1, r2 = pltpu.roll(d, CONV_TR - 1, 0), pltpu.roll(d, CONV_TR - 2, 0)
    row = lax.broadcasted_iota(jnp.int32, (8, d.shape[1]), 0)
    cut = CONV_TR - 8
    dp1 = jnp.concatenate([r1[:cut], jnp.where(row == 7, n1, r1[cut:])], axis=0)
    dp2 = jnp.concatenate([r2[:cut], jnp.where(row == 7, n2, jnp.where(row == 6, n1, r2[cut:]))], axis=0)
    return w_ref[2:3, :] * d + w_ref[1:2, :] * dp1 + w_ref[0:1, :] * dp2


def _convgate_bwd(u_a, u_g, w, b, dact, name):
    n_i = S // CONV_TR

    def body(ua, pa, ug, pg, wa, wg, ba, bg, d_ref, dua_ref, dug_ref, dwa_ref, dwg_ref, dba_ref, dbg_ref,
             nxt_a, nxt_g):
        i = pl.program_id(1)
        last = i == 0
        first = i == n_i - 1
        a0, a1, a2 = _causal_taps(ua, pa, first)
        ac = wa[0:1, :] * a2 + wa[1:2, :] * a1 + wa[2:3, :] * a0 + ba[...]
        g0, g1, g2 = _causal_taps(ug, pg, first)
        gc = wg[0:1, :] * g2 + wg[1:2, :] * g1 + wg[2:3, :] * g0 + bg[...]
        sg = 0.5 * jnp.tanh(0.5 * gc) + 0.5
        dact_v = d_ref[...].astype(F32)
        da = dact_v * (gc * sg)
        dg = dact_v * ac * (sg * (1.0 + gc * (1.0 - sg)))
        dua_ref[...] = _anticausal_conv(da, nxt_a, wa, last).astype(BF16)
        dug_ref[...] = _anticausal_conv(dg, nxt_g, wg, last).astype(BF16)
        nxt_a[...] = da[0:8]
        nxt_g[...] = dg[0:8]

        def col(v):
            return jnp.sum(v, axis=0, keepdims=True)

        parts = [col(da * a2), col(da * a1), col(da * a0), col(dg * g2), col(dg * g1), col(dg * g0),
                 col(da), col(dg)]

        @pl.when(last)
        def _():
            for k in range(3):
                dwa_ref[k:k + 1, :] = parts[k]
                dwg_ref[k:k + 1, :] = parts[3 + k]
            dba_ref[...] = parts[6]
            dbg_ref[...] = parts[7]

        @pl.when(i > 0)
        def _():
            for k in range(3):
                dwa_ref[k:k + 1, :] += parts[k]
                dwg_ref[k:k + 1, :] += parts[3 + k]
            dba_ref[...] += parts[6]
            dbg_ref[...] += parts[7]

    def swap(spec):
        return pl.BlockSpec(spec.block_shape, lambda j, i, f=spec.index_map: f(n_i - 1 - i, j))

    blk = pl.BlockSpec((CONV_TR, CONV_TC), lambda j, i: (n_i - 1 - i, j))
    w3 = pl.BlockSpec((3, CONV_TC), lambda j, i: (0, j))
    b1 = pl.BlockSpec((1, CONV_TC), lambda j, i: (0, j))
    return pl.pallas_call(
        body, name=name, grid=(CONV_NJ, n_i),
        out_shape=[jax.ShapeDtypeStruct((S, D_FF), BF16), jax.ShapeDtypeStruct((S, D_FF), BF16),
                   jax.ShapeDtypeStruct((3, D_FF), F32), jax.ShapeDtypeStruct((3, D_FF), F32),
                   jax.ShapeDtypeStruct((1, D_FF), F32), jax.ShapeDtypeStruct((1, D_FF), F32)],
        in_specs=[swap(s) for s in _conv_specs()] + [blk],
        out_specs=[blk, blk, w3, w3, b1, b1],
        scratch_shapes=[pltpu.VMEM((8, CONV_TC), F32), pltpu.VMEM((8, CONV_TC), F32)],
        compiler_params=_params("arbitrary", "arbitrary"),
    )(u_a, u_a, u_g, u_g, w, w, b, b, dact)


FOX_T = 512
FOX_TQ, FOX_TK = 512, 512
FOX_FORWARD_AT = 4
N_PAIRS = A_HEADS // 2


def _lane_masks():
    lane = lax.broadcasted_iota(jnp.int32, (1, LANES), 1)
    return lane, (lane < HEAD_DIM, lane >= HEAD_DIM)


def _fox_prep_fwd(z_t, b, name):
    def body(z_ref, b_ref, c_ref):
        r = lax.broadcasted_iota(jnp.int32, (LANES, LANES), 0)
        cc = lax.broadcasted_iota(jnp.int32, (LANES, LANES), 1)
        upper = (r <= cc).astype(BF16)
        carry = jnp.zeros((A_HEADS, 1), F32)
        for blk in range(S // LANES):
            sl = slice(blk * LANES, (blk + 1) * LANES)
            z = z_ref[:, sl] + b_ref[...]
            lf = jnp.minimum(z, 0.0) - jnp.log(1.0 + jnp.exp(-jnp.abs(z)))
            cs = _split_dot(lf, upper, 3) + carry
            c_ref[:, sl] = cs
            carry = cs[:, LANES - 1:LANES]

    return pl.pallas_call(
        body, name=name, out_shape=jax.ShapeDtypeStruct((A_HEADS, S), F32),
        compiler_params=_params(),
    )(z_t, b)


def _fox_prep_bwd(drow_t, dcol_t, z_t, b, name):
    def body(dr_ref, dc_ref, z_ref, b_ref, dz_ref, db_ref):
        r = lax.broadcasted_iota(jnp.int32, (LANES, LANES), 0)
        cc = lax.broadcasted_iota(jnp.int32, (LANES, LANES), 1)
        lower = (r >= cc).astype(BF16)
        carry = jnp.zeros((A_HEADS, 1), F32)
        db = jnp.zeros((A_HEADS, 1), F32)
        for blk in reversed(range(S // LANES)):
            sl = slice(blk * LANES, (blk + 1) * LANES)
            rc = _split_dot(dr_ref[:, sl] - dc_ref[:, sl], lower, 3) + carry
            carry = rc[:, 0:1]
            z = z_ref[:, sl] + b_ref[...]
            dz = rc / (1.0 + jnp.exp(z))
            dz_ref[:, sl] = dz
            db = db + jnp.sum(dz, axis=1, keepdims=True)
        db_ref[...] = db

    return pl.pallas_call(
        body, name=name,
        out_shape=[jax.ShapeDtypeStruct((A_HEADS, S), F32), jax.ShapeDtypeStruct((A_HEADS, 1), F32)],
        compiler_params=_params(),
    )(drow_t, dcol_t, z_t, b)


def _fox_fwd(qkv, c_t2, name, gather):
    tq, tk = FOX_TQ, FOX_TK

    n = len(gather)

    def body(*refs):
        q_ref, k_ref, v_ref, ct_ref = refs[:4]
        o_ref, lse_ref = refs[4 + n:6 + n]
        s_scr, p_scr, acc_scr = refs[-5:-3], refs[-3:-1], refs[-1]
        qi = pl.program_id(1)

        gather_start, gather_forward, gather_finish = _gather_phases(
            refs[4:4 + n], refs[6 + n:6 + 2 * n], refs[6 + 2 * n:len(refs) - 5])

        @pl.when(jnp.logical_and(pl.program_id(0) == 0, qi == 0))
        def _():
            gather_start()

        @pl.when(jnp.logical_and(pl.program_id(0) == FOX_FORWARD_AT, qi == 0))
        def _():
            gather_forward()

        n_full = jnp.right_shift(qi, (tk // tq).bit_length() - 1)
        lane, masks = _lane_masks()
        q = q_ref[...] * SCALE
        qs = [jnp.where(masks[e], q, jnp.zeros_like(q)) for e in range(2)]

        def scores_into(j, slot):
            start = pl.multiple_of(j * tk, tk)
            kb = k_ref[pl.ds(start, tk), :]
            for e in range(2):
                s_scr[slot][e] = _dot(qs[e], kb, NT_DIMS) - ct_ref[e:e + 1, pl.ds(start, tk)]

        def softmax_of(slot, m, masked):
            m_new, alpha = [], []
            for e in range(2):
                s = s_scr[slot][e]
                if masked:
                    rows = lax.broadcasted_iota(jnp.int32, (tq, tk), 0) + (qi * tq - n_full * tk)
                    cols = lax.broadcasted_iota(jnp.int32, (tq, tk), 1)
                    s = jnp.where(cols <= rows, s, NEG)
                m_new.append(jnp.maximum(m[e], jnp.max(s, axis=1, keepdims=True)))
                p_scr[slot][e] = jnp.exp(s - m_new[e]).astype(BF16)
                alpha.append(jnp.exp(m[e] - m_new[e]))
            return tuple(m_new), tuple(alpha)

        def values_of(j, slot, alpha):
            start = pl.multiple_of(j * tk, tk)
            vb = v_ref[pl.ds(start, tk), :]
            for e in range(2):
                acc_scr[e] = (alpha[e] * acc_scr[e]
                              + _dot(p_scr[slot][e], jnp.where(masks[e], vb, jnp.ones_like(vb))))

        def stage(j, cur, nxt, carry):
            m, a_prev = carry
            scores_into(j + 1, nxt)
            values_of(jnp.maximum(j - 1, 0), nxt, a_prev)
            return softmax_of(cur, m, False)

        def finish(cur, nxt, carry):
            m, a_prev = carry
            values_of(jnp.maximum(n_full - 1, 0), nxt, a_prev)
            (m0, m1), alpha = softmax_of(cur, m, True)
            values_of(n_full, cur, alpha)
            l0 = acc_scr[0][:, HEAD_DIM:HEAD_DIM + 1]
            l1 = acc_scr[1][:, 0:1]
            o_ref[...] = jnp.where(masks[0], acc_scr[0] / l0, acc_scr[1] / l1).astype(BF16)
            lse_ref[...] = jnp.where(masks[0], m0 + jnp.log(l0), m1 + jnp.log(l1))

        scores_into(0, 0)
        for e in range(2):
            p_scr[1][e] = jnp.zeros((tq, tk), BF16)
            acc_scr[e] = jnp.zeros((tq, LANES), F32)
        two = lambda x: (x, x)
        init = (two(jnp.full((tq, 1), NEG, F32)), two(jnp.ones((tq, 1), F32)))

        def two_stages(jj, carry):
            return stage(2 * jj + 1, 1, 0, stage(2 * jj, 0, 1, carry))

        carry = lax.fori_loop(0, jnp.right_shift(n_full, 1), two_stages, init)
        odd = jnp.bitwise_and(n_full, 1) == 1

        @pl.when(odd)
        def _():
            finish(1, 0, stage(n_full - 1, 0, 1, carry))

        @pl.when(jnp.logical_not(odd))
        def _():
            finish(0, 1, carry)

        @pl.when(jnp.logical_and(pl.program_id(0) == N_PAIRS - 1, qi == S // tq - 1))
        def _():
            gather_finish()

    qspec = pl.BlockSpec((tq, LANES), lambda h, i: (i, h))
    return pl.pallas_call(
        body, name=name, grid=(N_PAIRS, S // tq),
        out_shape=[jax.ShapeDtypeStruct((S, D), BF16), jax.ShapeDtypeStruct((S, D), F32)]
        + _exchange_shapes(gather, False),
        in_specs=[qspec,
                  pl.BlockSpec((S, LANES), lambda h, i: (0, N_PAIRS + h)),
                  pl.BlockSpec((S, LANES), lambda h, i: (0, 2 * N_PAIRS + h)),
                  pl.BlockSpec((None, 2, S), lambda h, i: (h, 0, 0))] + [ANY] * n,
        out_specs=[qspec, qspec] + [ANY] * n,
        scratch_shapes=_exchange_scratch(n) + [
            pltpu.VMEM((2, tq, tk), F32), pltpu.VMEM((2, tq, tk), F32),
            pltpu.VMEM((2, tq, tk), BF16), pltpu.VMEM((2, tq, tk), BF16),
            pltpu.VMEM((2, tq, LANES), F32)],
        compiler_params=_params("arbitrary", "arbitrary"),
    )(qkv, qkv, qkv, c_t2, *gather)


def _head_rowsum(a, b, name, tr=256):
    C = a.shape[1]

    def body(a_ref, b_ref, o_ref):
        r = lax.broadcasted_iota(jnp.int32, (LANES, LANES), 0) < HEAD_DIM
        cc = lax.broadcasted_iota(jnp.int32, (LANES, LANES), 1) < HEAD_DIM
        same_head = (r == cc).astype(BF16)
        for blk in range(C // LANES):
            sl = slice(blk * LANES, (blk + 1) * LANES)
            prod = a_ref[:, sl].astype(F32) * b_ref[:, sl].astype(F32)
            o_ref[:, sl] = _split_dot(prod, same_head, 2)

    row = pl.BlockSpec((tr, C), lambda i: (i, 0))
    return pl.pallas_call(
        body, name=name, grid=(S // tr,), out_shape=jax.ShapeDtypeStruct((S, C), F32),
        in_specs=[row, row], out_specs=row, compiler_params=_params("parallel"),
    )(a, b)


def _fox_bwd(qkv, do, lse, delta, c_t2, name, scatter):
    t = FOX_T
    nq = S // t

    n = len(scatter)

    def body(*refs):
        q_ref, k_ref, v_ref, do_ref, lse_ref, dl_ref, ct_ref = refs[:7]
        dq_ref, dk_ref, dv_ref, dcol_ref, drow_ref = refs[7 + n:12 + n]
        exchange = (refs[7:7 + n], refs[12 + n:12 + 2 * n], refs[12 + 2 * n:len(refs) - 5], True)
        sd_scr, pd_scr, acc_scr = refs[-5:-3], refs[-3:-1], refs[-1]
        kj = pl.program_id(1)

        @pl.when(jnp.logical_and(pl.program_id(0) == 0, kj == 0))
        def _():
            for cp in _exchange_copies(*exchange):
                cp.start()

        @pl.when(kj == 0)
        def _():
            dq_ref[...] = jnp.zeros_like(dq_ref)
            drow_ref[...] = jnp.zeros_like(drow_ref)

        lane, masks = _lane_masks()
        k = k_ref[...]
        v = v_ref[...]
        k_aug = [jnp.where(masks[e], k * SCALE, jnp.ones_like(k)) for e in range(2)]
        cs = [ct_ref[e:e + 1, :] for e in range(2)]

        def rows_of(i):
            r0 = pl.multiple_of(i * t, t)
            return pl.ds(r0, t), q_ref[pl.ds(r0, t), :] * SCALE, do_ref[pl.ds(r0, t), :]

        def scores_into(i, slot):
            _, qb, dob = rows_of(i)
            for e in range(2):
                qe = jnp.where(masks[e], qb, jnp.zeros_like(qb))
                doe = jnp.where(masks[e], dob, jnp.zeros_like(dob))
                sd_scr[slot][2 * e] = _dot(qe, k, NT_DIMS) - cs[e]
                sd_scr[slot][2 * e + 1] = _dot(doe, v, NT_DIMS)

        def pointwise(i, slot, masked):
            rows, _, _ = rows_of(i)
            for e in range(2):
                lo = e * HEAD_DIM
                s = sd_scr[slot][2 * e]
                if masked:
                    r = lax.broadcasted_iota(jnp.int32, (t, t), 0)
                    c = lax.broadcasted_iota(jnp.int32, (t, t), 1)
                    s = jnp.where(c <= r, s, NEG)
                p = jnp.exp(s - lse_ref[rows, lo:lo + 1])
                pd_scr[slot][2 * e] = p.astype(BF16)
                pd_scr[slot][2 * e + 1] = (p * (sd_scr[slot][2 * e + 1] - dl_ref[rows, lo:lo + 1])).astype(BF16)

        def accumulate(i, slot):
            rows, qb, dob = rows_of(i)
            dq_parts = []
            for e in range(2):
                p, ds = pd_scr[slot][2 * e], pd_scr[slot][2 * e + 1]
                q_aug = jnp.where(masks[e], qb, jnp.ones_like(qb))
                doe = jnp.where(masks[e], dob, jnp.zeros_like(dob))
                acc_scr[2] += _dot(p, doe, TN_DIMS)
                acc_scr[e] += _dot(ds, q_aug, TN_DIMS)
                dq_parts.append(_dot(ds, k_aug[e]))
            dq_ref[rows, :] += jnp.where(masks[0], dq_parts[0], dq_parts[1])
            drow_ref[rows, :] += jnp.where(masks[0], dq_parts[1], dq_parts[0])

        def stage(i, cur, nxt):
            scores_into(jnp.minimum(i + 1, nq - 1), nxt)
            accumulate(i - 1, nxt)
            pointwise(i, cur, False)

        acc_scr[...] = jnp.zeros_like(acc_scr)
        scores_into(kj, 0)
        pointwise(kj, 0, True)
        scores_into(jnp.minimum(kj + 1, nq - 1), 1)
        rest = nq - 1 - kj

        def two_stages(jj, carry):
            stage(kj + 1 + 2 * jj, 1, 0)
            stage(kj + 2 + 2 * jj, 0, 1)
            return carry

        lax.fori_loop(0, jnp.right_shift(rest, 1), two_stages, 0)
        odd = jnp.bitwise_and(rest, 1) == 1

        @pl.when(odd)
        def _():
            stage(nq - 1, 1, 0)
            accumulate(nq - 1, 1)

        @pl.when(jnp.logical_not(odd))
        def _():
            accumulate(nq - 1, 0)

        dk0, dk1, dv = acc_scr[0], acc_scr[1], acc_scr[2]
        dk_ref[...] = jnp.where(masks[0], dk0, dk1).astype(BF16)
        dcol_ref[...] = jnp.where(masks[0], dk1, dk0)
        dv_ref[...] = dv.astype(BF16)

        @pl.when(jnp.logical_and(pl.program_id(0) == N_PAIRS - 1, kj == nq - 1))
        def _():
            for cp in _exchange_copies(*exchange):
                cp.wait()

    full = lambda off: pl.BlockSpec((S, LANES), lambda h, j, off=off: (0, off + h))
    kv = lambda off: pl.BlockSpec((t, LANES), lambda h, j, off=off: (j, off + h))
    return pl.pallas_call(
        body, name=name, grid=(N_PAIRS, nq),
        out_shape=[jax.ShapeDtypeStruct((S, D), F32), jax.ShapeDtypeStruct((S, D), BF16),
                   jax.ShapeDtypeStruct((S, D), BF16), jax.ShapeDtypeStruct((S, D), F32),
                   jax.ShapeDtypeStruct((S, D), F32)] + _exchange_shapes(scatter, True),
        in_specs=[full(0), kv(N_PAIRS), kv(2 * N_PAIRS), full(0), full(0), full(0),
                  pl.BlockSpec((None, 2, t), lambda h, j: (h, 0, j))] + [ANY] * n,
        out_specs=[full(0), kv(0), kv(0), kv(0), full(0)] + [ANY] * n,
        scratch_shapes=_exchange_scratch(n) + [
            pltpu.VMEM((4, t, t), F32), pltpu.VMEM((4, t, t), F32),
            pltpu.VMEM((4, t, t), BF16), pltpu.VMEM((4, t, t), BF16),
            pltpu.VMEM((3, t, LANES), F32)],
        compiler_params=_params("arbitrary", "arbitrary"),
    )(qkv, qkv, qkv, do, lse, delta, c_t2, *scatter)


B_PAIRS = 4
B_NB = S // B_W


def _group_consts(g):
    nbs = jnp.where(g == 0, B_NB // B_DILS[0], jnp.where(g == 1, B_NB // B_DILS[1], B_NB // B_DILS[2]))
    dil = jnp.where(g == 0, B_DILS[0], jnp.where(g == 1, B_DILS[1], B_DILS[2]))
    return nbs, dil


def _band(dil):
    qi = lax.broadcasted_iota(jnp.int32, (B_W, B_W), 0)
    kj = lax.broadcasted_iota(jnp.int32, (B_W, B_W), 1)
    dist_c = qi - kj
    dist_p = qi + B_W - kj
    return (dist_c * dil).astype(F32), dist_c >= 0, (dist_p * dil).astype(F32), dist_p <= B_W


def _dil_fwd(qp, kp, vp, slopes, name):
    def body(sl_ref, q_ref, kp_ref, kc_ref, vp_ref, vc_ref, o_ref, lse_ref):
        g, n = pl.program_id(0), pl.program_id(1)
        nbs, dil = _group_consts(g)
        has_prev = (n % nbs) != 0
        lane, masks = _lane_masks()
        bias_c, ok_c, bias_p, ok_p = _band(dil)
        ok_p = jnp.logical_and(ok_p, has_prev)
        heads = [(hp, e) for hp in range(B_PAIRS) for e in range(2)]
        col = lambda ref, hp: ref[:, hp * LANES:(hp + 1) * LANES]
        logits = []
        for hp, e in heads:
            q = col(q_ref, hp) * SCALE
            qe = jnp.where(masks[e], q, jnp.zeros_like(q))
            logits.append((_dot(qe, col(kc_ref, hp), NT_DIMS), _dot(qe, col(kp_ref, hp), NT_DIMS)))
        probs = []
        for (hp, e), (sc, sp) in zip(heads, logits):
            slope = sl_ref[g * 8 + 2 * hp + e]
            sc = jnp.where(ok_c, sc - slope * bias_c, NEG)
            sp = jnp.where(ok_p, sp - slope * bias_p, NEG)
            m = jnp.maximum(jnp.max(sc, axis=1, keepdims=True), jnp.max(sp, axis=1, keepdims=True))
            probs.append((jnp.exp(sc - m).astype(BF16), jnp.exp(sp - m).astype(BF16), m))
        outs, lses = [], []
        for (hp, e), (pc, pp, m) in zip(heads, probs):
            vc, vpv = col(vc_ref, hp), col(vp_ref, hp)
            acc = (_dot(pc, jnp.where(masks[e], vc, jnp.ones_like(vc)))
                   + _dot(pp, jnp.where(masks[e], vpv, jnp.ones_like(vpv))))
            l = acc[:, HEAD_DIM:HEAD_DIM + 1] if e == 0 else acc[:, 0:1]
            outs.append(acc / l)
            lses.append(m + jnp.log(l))
        o_ref[...] = jnp.concatenate(
            [jnp.where(masks[0], outs[2 * hp], outs[2 * hp + 1]) for hp in range(B_PAIRS)], axis=1).astype(BF16)
        lse = jnp.zeros((B_W, LANES), F32)
        for h in range(2 * B_PAIRS):
            lse = jnp.where(lane == h, lses[h], lse)
        lse_ref[...] = lse

    cur = pl.BlockSpec((None, B_W, B_OUT), lambda g, n, sl: (g, n, 0))
    prev = pl.BlockSpec((None, B_W, B_OUT), lambda g, n, sl: (g, jnp.maximum(n - 1, 0), 0))
    stat = pl.BlockSpec((None, B_W, LANES), lambda g, n, sl: (g, n, 0))
    return pl.pallas_call(
        body, name=name,
        grid_spec=pltpu.PrefetchScalarGridSpec(
            num_scalar_prefetch=1, grid=(3, B_NB),
            in_specs=[cur, prev, cur, prev, cur], out_specs=[cur, stat]),
        out_shape=[jax.ShapeDtypeStruct((3, S, B_OUT), BF16), jax.ShapeDtypeStruct((3, S, LANES), F32)],
        compiler_params=_params("parallel", "parallel"),
    )(slopes, qp, kp, kp, vp, vp)


def _head_expander():
    r = lax.broadcasted_iota(jnp.int32, (LANES, B_OUT), 0)
    c = lax.broadcasted_iota(jnp.int32, (LANES, B_OUT), 1)
    return jnp.logical_and(c >= r * HEAD_DIM, c < (r + 1) * HEAD_DIM).astype(BF16)


def _dil_merge(og, lseg, name, tr=256):
    def body(o_ref, l_ref, out_ref, lse_ref):
        l0, l1, l2 = l_ref[0], l_ref[1], l_ref[2]
        m = jnp.maximum(jnp.maximum(l0, l1), l2)
        w0, w1, w2 = jnp.exp(l0 - m), jnp.exp(l1 - m), jnp.exp(l2 - m)
        den = w0 + w1 + w2
        lse_ref[...] = m + jnp.log(den)
        expand = _head_expander()
        out = None
        for g, w in enumerate((w0, w1, w2)):
            part = _split_dot(w / den, expand, 3) * o_ref[g].astype(F32)
            out = part if out is None else out + part
        out_ref[...] = out.astype(BF16)

    blk3 = pl.BlockSpec((3, tr, B_OUT), lambda i: (0, i, 0))
    stat3 = pl.BlockSpec((3, tr, LANES), lambda i: (0, i, 0))
    blk = pl.BlockSpec((tr, B_OUT), lambda i: (i, 0))
    stat = pl.BlockSpec((tr, LANES), lambda i: (i, 0))
    return pl.pallas_call(
        body, name=name, grid=(S // tr,),
        out_shape=[jax.ShapeDtypeStruct((S, B_OUT), BF16), jax.ShapeDtypeStruct((S, LANES), F32)],
        in_specs=[blk3, stat3], out_specs=[blk, stat], compiler_params=_params("parallel"),
    )(og, lseg)


def _head_rowsum_compact(a, b, name, tr=256):
    def body(a_ref, b_ref, o_ref):
        r = lax.broadcasted_iota(jnp.int32, (B_OUT, LANES), 0)
        c = lax.broadcasted_iota(jnp.int32, (B_OUT, LANES), 1)
        collect = jnp.logical_and(r >= c * HEAD_DIM, r < (c + 1) * HEAD_DIM).astype(BF16)
        prod = a_ref[...].astype(F32) * b_ref[...].astype(F32)
        o_ref[...] = _split_dot(prod, collect, 2)

    row = pl.BlockSpec((tr, B_OUT), lambda i: (i, 0))
    return pl.pallas_call(
        body, name=name, grid=(S // tr,), out_shape=jax.ShapeDtypeStruct((S, LANES), F32),
        in_specs=[row, row], out_specs=pl.BlockSpec((tr, LANES), lambda i: (i, 0)),
        compiler_params=_params("parallel"),
    )(a, b)


def _dil_bwd(qp, kp, vp, dop, lsep, dlp, slopes, name, scatter):
    n_ex = len(scatter)

    def body(sl_ref, *refs):
        (qc_ref, qn_ref, kp_ref, kc_ref, vp_ref, vc_ref, doc_ref, don_ref,
         lc_ref, ln_ref, dc_ref, dn_ref) = refs[:12]
        dq_ref, dk_ref, dv_ref = refs[12 + n_ex:15 + n_ex]
        exchange = (refs[12:12 + n_ex], refs[15 + n_ex:15 + 2 * n_ex], refs[15 + 2 * n_ex:], True)
        g, n = pl.program_id(0), pl.program_id(1)

        @pl.when(jnp.logical_and(g == 0, n == 0))
        def _():
            for cp in _exchange_copies(*exchange):
                cp.start()

        nbs, dil = _group_consts(g)
        has_prev = (n % nbs) != 0
        has_next = jnp.logical_and(n + 1 < B_NB, ((n + 1) % nbs) != 0)
        lane, masks = _lane_masks()
        bias_c, ok_c, bias_p, ok_p = _band(dil)
        ok_pp = jnp.logical_and(ok_p, has_prev)
        ok_np = jnp.logical_and(ok_p, has_next)
        heads = [(hp, e) for hp in range(B_PAIRS) for e in range(2)]
        col = lambda ref, hp: ref[:, hp * LANES:(hp + 1) * LANES]
        mask = lambda t, e: jnp.where(masks[e], t, jnp.zeros_like(t))
        raw = []
        for hp, e in heads:
            qce, qne = mask(col(qc_ref, hp) * SCALE, e), mask(col(qn_ref, hp) * SCALE, e)
            doce, done = mask(col(doc_ref, hp), e), mask(col(don_ref, hp), e)
            kc, kpv, vc, vpv = col(kc_ref, hp), col(kp_ref, hp), col(vc_ref, hp), col(vp_ref, hp)
            raw.append(((_dot(qce, kc, NT_DIMS), _dot(doce, vc, NT_DIMS)),
                        (_dot(qce, kpv, NT_DIMS), _dot(doce, vpv, NT_DIMS)),
                        (_dot(qne, kc, NT_DIMS), _dot(done, vc, NT_DIMS))))
        pds = []
        for (hp, e), tiles in zip(heads, raw):
            lo = 2 * hp + e
            slope = sl_ref[g * 8 + 2 * hp + e]
            lse_c, dl_c = lc_ref[:, lo:lo + 1], dc_ref[:, lo:lo + 1]
            lse_n, dl_n = ln_ref[:, lo:lo + 1], dn_ref[:, lo:lo + 1]
            out = []
            for (s, dp), ok, bias, lse, dl in ((tiles[0], ok_c, bias_c, lse_c, dl_c),
                                               (tiles[1], ok_pp, bias_p, lse_c, dl_c),
                                               (tiles[2], ok_np, bias_p, lse_n, dl_n)):
                p = jnp.exp(jnp.where(ok, s - slope * bias, NEG) - lse)
                out.append((p.astype(BF16), (p * (dp - dl)).astype(BF16)))
            pds.append(out)
        dq_all, dk_all, dv_all = [], [], []
        for hp in range(B_PAIRS):
            dq = jnp.zeros((B_W, LANES), F32)
            dk = jnp.zeros((B_W, LANES), F32)
            dv = jnp.zeros((B_W, LANES), F32)
            for e in range(2):
                (p_c, ds_c), (_, ds_p), (p_n, ds_n) = pds[2 * hp + e]
                qce, qne = mask(col(qc_ref, hp) * SCALE, e), mask(col(qn_ref, hp) * SCALE, e)
                doce, done = mask(col(doc_ref, hp), e), mask(col(don_ref, hp), e)
                dq = dq + _dot(ds_c, mask(col(kc_ref, hp) * SCALE, e)) + _dot(ds_p, mask(col(kp_ref, hp) * SCALE, e))
                dk = dk + _dot(ds_c, qce, TN_DIMS) + _dot(ds_n, qne, TN_DIMS)
                dv = dv + _dot(p_c, doce, TN_DIMS) + _dot(p_n, done, TN_DIMS)
            dq_all.append(dq)
            dk_all.append(dk)
            dv_all.append(dv)
        dq_ref[...] = jnp.concatenate(dq_all, axis=1).astype(BF16)
        dk_ref[...] = jnp.concatenate(dk_all, axis=1).astype(BF16)
        dv_ref[...] = jnp.concatenate(dv_all, axis=1).astype(BF16)

        @pl.when(jnp.logical_and(g == 2, n == B_NB - 1))
        def _():
            for cp in _exchange_copies(*exchange):
                cp.wait()

    cur = pl.BlockSpec((None, B_W, B_OUT), lambda g, n, sl: (g, n, 0))
    prev = pl.BlockSpec((None, B_W, B_OUT), lambda g, n, sl: (g, jnp.maximum(n - 1, 0), 0))
    nxt = pl.BlockSpec((None, B_W, B_OUT), lambda g, n, sl: (g, jnp.minimum(n + 1, B_NB - 1), 0))
    stat_cur = pl.BlockSpec((None, B_W, LANES), lambda g, n, sl: (g, n, 0))
    stat_nxt = pl.BlockSpec((None, B_W, LANES), lambda g, n, sl: (g, jnp.minimum(n + 1, B_NB - 1), 0))
    return pl.pallas_call(
        body, name=name,
        grid_spec=pltpu.PrefetchScalarGridSpec(
            num_scalar_prefetch=1, grid=(3, B_NB),
            in_specs=[cur, nxt, prev, cur, prev, cur, cur, nxt, stat_cur, stat_nxt, stat_cur, stat_nxt]
            + [ANY] * n_ex,
            out_specs=[cur, cur, cur] + [ANY] * n_ex,
            scratch_shapes=_exchange_scratch(n_ex)),
        out_shape=[jax.ShapeDtypeStruct((3, S, B_OUT), BF16)] * 3 + _exchange_shapes(scatter, True),
        compiler_params=_params("arbitrary", "arbitrary"),
    )(slopes, qp, qp, kp, kp, vp, vp, dop, dop, lsep, lsep, dlp, dlp, *scatter)


def _rows_block(shape, max_bytes=2 * 1024 * 1024):
    rows, cols = shape
    padded_cols = -(-cols // LANES) * LANES
    for tr in (1024, 512, 256, 128, 64, 32, 16):
        if rows % tr == 0 and tr * padded_cols * 4 <= max_bytes:
            return tr
    return rows


def _adam_update(w, m, v, g):
    m_new = ADAM_B1 * m + (1.0 - ADAM_B1) * g
    v_new = ADAM_B2 * v + (1.0 - ADAM_B2) * (g * g)
    m_hat = m_new / (1.0 - ADAM_B1 ** ADAM_STEP)
    v_hat = v_new / (1.0 - ADAM_B2 ** ADAM_STEP)
    delta = -ADAM_LR * (m_hat / (jnp.sqrt(v_hat) + ADAM_EPS) + ADAM_WD * w)
    return delta, m_new, v_new


def _adamw_sharded(w, m, v, parts, name):
    R, C = w.shape
    tr = _rows_block((R, C), max_bytes=1024 * 1024)

    def body(w_ref, m_ref, v_ref, p_ref, g_ref, d_ref, mo_ref, vo_ref):
        g = p_ref[0].astype(F32)
        for dev in range(1, N_DEV):
            g = g + p_ref[dev].astype(F32)
        g_ref[...] = g
        d_ref[...], mo_ref[...], vo_ref[...] = _adam_update(w_ref[...], m_ref[...], v_ref[...], g)

    blk = pl.BlockSpec((tr, C), lambda i: (i, 0))
    out = jax.ShapeDtypeStruct((R, C), F32)
    return pl.pallas_call(
        body, name=name, grid=(R // tr,),
        in_specs=[blk, blk, blk, pl.BlockSpec((N_DEV, tr, C), lambda i: (0, i, 0))],
        out_specs=[blk, blk, blk, blk], out_shape=[out, out, out, out],
        compiler_params=_params("parallel"),
    )(w, m, v, parts)


def _adamw_replicated(w, m, v, parts, name):
    def body(w_ref, m_ref, v_ref, p_ref, g_ref, d_ref, mo_ref, vo_ref):
        g = p_ref[0]
        for dev in range(1, N_DEV):
            g = g + p_ref[dev]
        g_ref[...] = g
        d_ref[...], mo_ref[...], vo_ref[...] = _adam_update(w_ref[...], m_ref[...], v_ref[...], g)

    out = jax.ShapeDtypeStruct(w.shape, F32)
    return pl.pallas_call(body, name=name, out_shape=[out, out, out, out], compiler_params=_params())(w, m, v, parts)


def _cols_from_slots(g):
    return g.transpose(1, 0, 2).reshape(g.shape[1], N_DEV * g.shape[2])


def _cols_to_slots(w):
    k, n = w.shape
    return w.reshape(k, N_DEV, n // N_DEV).transpose(1, 0, 2)


def _permute(t, dil):
    c = t.shape[1]
    return t.reshape(S // dil, dil, c).transpose(1, 0, 2).reshape(S, c)


def _unpermute(t, dil):
    c = t.shape[1]
    return t.reshape(dil, S // dil, c).transpose(1, 0, 2).reshape(S, c)


def _group_permute(t):
    return jnp.stack([_permute(t[:, g * B_OUT:(g + 1) * B_OUT], B_DILS[g]) for g in range(3)])


def _same_permute(t):
    return jnp.stack([_permute(t, d) for d in B_DILS])


def _group_unpermute(t):
    return jnp.stack([_unpermute(t[g], B_DILS[g]) for g in range(3)])


SMALL_ROWS = 144


def _pack_small(a_b_f, kv_g, mix_g, ffn_g, conv_b, fin_g):
    flat = jnp.concatenate([a_b_f.reshape(-1), kv_g.reshape(-1), mix_g.reshape(-1), ffn_g.reshape(-1),
                            conv_b.reshape(-1), fin_g.reshape(-1)])
    return jnp.pad(flat, (0, SMALL_ROWS * LANES - flat.shape[0])).reshape(SMALL_ROWS, LANES)


def _unpack_small(p):
    flat = p.reshape(-1)
    out, off = [], 0
    for shape in ((1, A_HEADS), (D,), (2, D), (2, D), (2, 2 * D_FF), (D,)):
        size = math.prod(shape)
        out.append(flat[off:off + size].reshape(shape))
        off += size
    return out


def _unpack_late(g):
    half = N_DEV // 2
    up = g[4].reshape(N_DEV, 2, D, -1)
    w_up_a = [up[:half, l].transpose(1, 0, 2).reshape(D, D_FF) for l in range(2)]
    w_up_g = [up[half:, l].transpose(1, 0, 2).reshape(D, D_FF) for l in range(2)]
    w_down = [g[5].reshape(N_DEV, 2, -1, D)[:, l].reshape(D_FF, D) for l in range(2)]
    conv_w = [g[6].reshape(N_DEV, 2, 3, -1)[:, l].transpose(1, 0, 2).reshape(3, 2 * D_FF) for l in range(2)]
    return (g[0].reshape(D, D), _cols_from_slots(g[1]), _cols_from_slots(g[2]), _cols_from_slots(g[3]),
            w_up_a, w_up_g, w_down, conv_w)


def _ffn_slots(dw_up, dw_down, dconv_w):
    return [_cols_to_slots(dw_up), dw_down.reshape(N_DEV, -1, D), _cols_to_slots(dconv_w)]


def _local_step(x0, target, w_in_pad, late_shards,
                a_b_f, kv_norm_g, mix_norm_g, ffn_norm_g, ffn_conv_b, final_norm_g):
    w_qkv, w_f = w_in_pad[:, :A_QKV], w_in_pad[:, A_QKV:]
    conv_b = ffn_conv_b.reshape(2, 1, 2 * D_FF)
    slopes = jnp.exp2(-8.0 * jnp.arange(1, 25, dtype=F32) / 24)

    def gain(g):
        return g.reshape(1, D)

    (h1,) = _rmsnorm_fwd(x0, [gain(mix_norm_g[0])], "norm_mix0")
    qkv = _matmul(h1, w_qkv, mode="nn", out_dtype=BF16, name="proj_qkv", tm=512, tn=A_QKV)
    z = _matmul(h1, w_f, mode="nn", out_dtype=F32, name="proj_gate", tm=S, tn=LANES)
    z_t = z[:, :A_HEADS].T
    b_f = a_b_f.reshape(A_HEADS, 1)
    c_t = _fox_prep_fwd(z_t, b_f, "fox_prep")
    c_t2 = c_t.reshape(N_PAIRS, 2, S)
    o_a, lse_a, *late = _fox_fwd(qkv, c_t2, "fox_fwd", late_shards)
    w_out, w_q, w_bo, w_kvf, w_up_a, w_up_g, w_down, conv_w = _unpack_late(late)
    x1 = _matmul(o_a, w_out, mode="nn", out_dtype=F32, name="a_out", tm=512, tn=D, res=x0)

    def ffn_fwd(xin, layer):
        (h,) = _rmsnorm_fwd(xin, [gain(ffn_norm_g[layer])], f"norm_ffn{layer}")
        u = (_matmul(h, w_up_a[layer], mode="nn", out_dtype=BF16, name=f"ffn_up_a{layer}", tm=512, tn=D_FF),
             _matmul(h, w_up_g[layer], mode="nn", out_dtype=BF16, name=f"ffn_up_g{layer}", tm=512, tn=D_FF))
        act = _convgate_fwd(*u, conv_w[layer], conv_b[layer], f"convgate{layer}")
        xout = _matmul(act, w_down[layer], mode="nn", out_dtype=F32, name=f"ffn_down{layer}", tm=512, tn=D, res=xin)
        return h, u, act, xout

    h2, u0, act0, x2 = ffn_fwd(x1, 0)
    hk, h3 = _rmsnorm_fwd(x2, [gain(kv_norm_g), gain(mix_norm_g[1])], "norm_kv_mix1")
    kv = _matmul(hk, w_kvf, mode="nn", out_dtype=BF16, name="proj_kv", tm=512, tn=B_KV)
    qb = _matmul(h3, w_q, mode="nn", out_dtype=BF16, name="proj_qb", tm=512, tn=B_Q)
    qp, kp, vp = _group_permute(qb), _group_permute(kv[:, :B_Q]), _group_permute(kv[:, B_Q:])
    og_p, lseg_p = _dil_fwd(qp, kp, vp, slopes, "dil_fwd")
    o_b, lse_b = _dil_merge(_group_unpermute(og_p), _group_unpermute(lseg_p), "dil_merge")
    x3 = _matmul(o_b, w_bo, mode="nn", out_dtype=F32, name="b_out", tm=512, tn=D, res=x2)
    h4, u1, act1, x4 = ffn_fwd(x3, 1)
    loss_blk, dx4, dx4b, dg_final = _final_loss(x4, target, gain(final_norm_g), "final_loss")

    def ffn_bwd(dx, dxb, xin, h, u, act, layer):
        dact = _matmul(dxb, w_down[layer], mode="nt", out_dtype=BF16, name=f"d_act{layer}", tm=512, tn=D_FF)
        dw_down = _matmul_tn(act, dxb, out_dtype=BF16, name=f"dw_down{layer}")
        du_a, du_g, dwa, dwg, dba, dbg = _convgate_bwd(*u, conv_w[layer], conv_b[layer], dact, f"convgate_bwd{layer}")
        dw_up = jnp.concatenate(
            [_matmul_tn(h, du_a, out_dtype=BF16, name=f"dw_up_a{layer}"),
             _matmul_tn(h, du_g, out_dtype=BF16, name=f"dw_up_g{layer}")], axis=1)
        dh = _matmul(du_a, w_up_a[layer], mode="nt", out_dtype=F32, name=f"dh_ffn_a{layer}", tm=512, tn=D)
        dxin, dxinb, dgain = _matmul_norm_bwd(du_g, w_up_g[layer], xin, gain(ffn_norm_g[layer]), dx,
                                              f"dh_ffn_g_norm_bwd{layer}", res=dh)
        dconv_w = jnp.concatenate([dwa, dwg], axis=1)
        dconv_b = jnp.concatenate([dba, dbg], axis=1)
        return dxin, dxinb, dgain, dw_up, dw_down, dconv_w, dconv_b

    dx3, dx3b, dg_ffn1, dw_up1, dw_down1, dconv_w1, dconv_b1 = ffn_bwd(dx4, dx4b, x3, h4, u1, act1, 1)

    do_b = _matmul(dx3b, w_bo, mode="nt", out_dtype=BF16, name="d_ob", tm=1024, tn=B_OUT)
    dw_bo = _matmul_tn(o_b, dx3b, out_dtype=BF16, name="dw_bo")
    dl_b = _head_rowsum_compact(do_b, o_b, "delta_b")
    slots_up1, slots_down1, slots_conv1 = _ffn_slots(dw_up1, dw_down1, dconv_w1)
    dqp, dkp, dvp, land_down1, land_conv1 = _dil_bwd(
        qp, kp, vp, _same_permute(do_b), _same_permute(lse_b), _same_permute(dl_b), slopes, "dil_bwd",
        [slots_down1, slots_conv1])

    def natural(tp):
        return jnp.concatenate([_unpermute(tp[g], B_DILS[g]) for g in range(3)], axis=1)

    dqb = natural(dqp)
    dkv = jnp.concatenate([natural(dkp), natural(dvp)], axis=1)
    dw_q = _matmul_tn(h3, dqb, out_dtype=BF16, name="dw_q")
    dw_kv = _matmul_tn(hk, dkv, out_dtype=BF16, name="dw_kv")
    dx2, _, dg_mix1 = _matmul_norm_bwd(dqb, w_q, x2, gain(mix_norm_g[1]), dx3, "dh_mix1_norm_bwd")
    dx2, dx2b, dg_kv = _matmul_norm_bwd(dkv, w_kvf, x2, gain(kv_norm_g), dx2, "dh_kv_norm_bwd")

    dx1, dx1b, dg_ffn0, dw_up0, dw_down0, dconv_w0, dconv_b0 = ffn_bwd(dx2, dx2b, x1, h2, u0, act0, 0)

    do_a = _matmul(dx1b, w_out, mode="nt", out_dtype=BF16, name="d_oa", tm=512, tn=D)
    dw_out = _matmul_tn(o_a, dx1b, out_dtype=BF16, name="dw_out")
    dl_a = _head_rowsum(do_a, o_a, "delta_a")
    dq_a, dk_a, dv_a, dcol, drow, *land = _fox_bwd(
        qkv, do_a, lse_a, dl_a, c_t2, "fox_bwd",
        [dw_out.reshape(N_DEV, D // N_DEV, D), _cols_to_slots(dw_q), _cols_to_slots(dw_bo), _cols_to_slots(dw_kv)]
        + _ffn_slots(dw_up0, dw_down0, dconv_w0) + [slots_up1])
    land_out, land_q, land_bo, land_kv, land_up0, land_down0, land_conv0, land_up1 = land

    def head_sums(t):
        return t.reshape(S, N_PAIRS, 2, HEAD_DIM)[:, :, ::-1, 0].reshape(S, A_HEADS).T

    dz_t, db_f = _fox_prep_bwd(head_sums(drow), head_sums(dcol), z_t, b_f, "fox_prep_bwd")
    dz = jnp.pad(dz_t.T, ((0, 0), (0, LANES - A_HEADS))).astype(BF16)
    dproj = jnp.concatenate([dq_a.astype(BF16), dk_a, dv_a, dz], axis=1)
    dw_in = _matmul_tn(h1, dproj, out_dtype=BF16, name="dw_in")
    dh1, land_in = _matmul(dproj, w_in_pad, mode="nt", out_dtype=F32, name="dh_mix0", tm=512, tn=D,
                           scatter=[_cols_to_slots(dw_in[:, :A_QKV + A_HEADS])])
    grad_x, _, dg_mix0 = _rmsnorm_bwd(x0, dh1, gain(mix_norm_g[0]), dx1, "norm_mix0_bwd")

    dg_mix = jnp.concatenate([dg_mix0, dg_mix1], axis=0)
    dg_ffn = jnp.concatenate([dg_ffn0, dg_ffn1], axis=0)
    dconv_b = jnp.concatenate([dconv_b0, dconv_b1], axis=0)
    small_part = _pack_small(db_f, dg_kv, dg_mix, dg_ffn, dconv_b, dg_final)
    _, (small_parts,) = _final_exchange([], [small_part], "gather_small_grads")
    landed = [land_in, land_out, land_q, land_bo, land_kv, land_up0, land_up1, land_down0, land_down1,
              land_conv0, land_conv1]
    return loss_blk, grad_x, landed, small_parts


def kernel(x, a_w_in, a_b_f, a_w_out, b_w_q, b_w_out, kv_norm_g, w_kv, mix_norm_g, ffn_norm_g, ffn_w_up, ffn_conv_w, ffn_conv_b, ffn_w_down, final_norm_g, loss_target, m_a_w_in, m_a_b_f, m_a_w_out, m_b_w_q, m_b_w_out, m_kv_norm_g, m_w_kv, m_mix_norm_g, m_ffn_norm_g, m_ffn_w_up, m_ffn_conv_w, m_ffn_conv_b, m_ffn_w_down, m_final_norm_g, v_a_w_in, v_a_b_f, v_a_w_out, v_b_w_q, v_b_w_out, v_kv_norm_g, v_w_kv, v_mix_norm_g, v_ffn_norm_g, v_ffn_w_up, v_ffn_conv_w, v_ffn_conv_b, v_ffn_w_down, v_final_norm_g):
    def shards(a_w_in, a_w_out, b_w_q, b_w_out, w_kv, ffn_w_up, ffn_w_down, ffn_conv_w):
        return [a_w_in[0], a_w_out[0], b_w_q[0], b_w_out[0], w_kv, ffn_w_up[0], ffn_w_up[1],
                ffn_w_down[0], ffn_w_down[1], ffn_conv_w[0], ffn_conv_w[1]]

    w_loc = shards(a_w_in, a_w_out, b_w_q, b_w_out, w_kv, ffn_w_up, ffn_w_down, ffn_conv_w)
    m_loc = shards(m_a_w_in, m_a_w_out, m_b_w_q, m_b_w_out, m_w_kv, m_ffn_w_up, m_ffn_w_down, m_ffn_conv_w)
    v_loc = shards(v_a_w_in, v_a_w_out, v_b_w_q, v_b_w_out, v_w_kv, v_ffn_w_up, v_ffn_w_down, v_ffn_conv_w)

    (g_in,) = _all_gather([a_w_in[0].astype(BF16)], "gather_a_w_in")
    w_in = _cols_from_slots(g_in)
    w_in_pad = jnp.pad(w_in, ((0, 0), (0, A_PROJ_PAD - w_in.shape[1])))
    late_shards = [a_w_out[0].astype(BF16), b_w_q[0].astype(BF16), b_w_out[0].astype(BF16), w_kv.astype(BF16),
                   ffn_w_up.reshape(2 * D, -1).astype(BF16), ffn_w_down.reshape(-1, D).astype(BF16),
                   ffn_conv_w.reshape(6, -1)]

    loss_blk, grad_x, landed, small_parts = _local_step(
        x[0], loss_target[0], w_in_pad, late_shards,
        a_b_f, kv_norm_g, mix_norm_g, ffn_norm_g, ffn_conv_b, final_norm_g)

    big = [_adamw_sharded(w_loc[k], m_loc[k], v_loc[k], landed[k], f"adamw{k}") for k in range(11)]

    small = _adamw_replicated(
        _pack_small(a_b_f, kv_norm_g, mix_norm_g, ffn_norm_g, ffn_conv_b, final_norm_g),
        _pack_small(m_a_b_f, m_kv_norm_g, m_mix_norm_g, m_ffn_norm_g, m_ffn_conv_b, m_final_norm_g),
        _pack_small(v_a_b_f, v_kv_norm_g, v_mix_norm_g, v_ffn_norm_g, v_ffn_conv_b, v_final_norm_g),
        small_parts, "adamw_small")

    loss = lax.psum(loss_blk[0, 0], ("x", "y", "c"))

    def assemble(kind):
        b = [r[kind] for r in big]
        s_abf, s_kv, s_mix, s_ffn, s_cb, s_fin = _unpack_small(small[kind])
        return [b[0][None], s_abf, b[1][None], b[2][None], b[3][None], s_kv, b[4], s_mix, s_ffn,
                jnp.stack([b[5], b[6]]), jnp.stack([b[9], b[10]]), s_cb, jnp.stack([b[7], b[8]]), s_fin]

    return (loss, grad_x[None], *assemble(0), *assemble(1), *assemble(2), *assemble(3))
```

```python
import functools
import math

import jax
import jax.numpy as jnp
from jax import lax
from jax.experimental import pallas as pl
from jax.experimental.pallas import tpu as pltpu

F32 = jnp.float32
BF16 = jnp.bfloat16

S = 4096
D = 1024
N_DEV = 8
A_HEADS = 16
HEAD_DIM = 64
A_QKV = 3072
A_PROJ_PAD = 3200
B_Q = 1536
B_OUT = 512
B_KV = 3072
B_W = 128
B_DILS = (1, 4, 16)
D_FF = 2816
RMS_EPS = 1e-6
SCALE = HEAD_DIM ** -0.5
NEG = -1e30

ADAM_LR = 0.001
ADAM_B1 = 0.9
ADAM_B2 = 0.999
ADAM_EPS = 1e-08
ADAM_WD = 0.01
ADAM_STEP = 10

LANES = 128
VMEM_LIMIT = 56 * 1024 * 1024
MESH = pl.DeviceIdType.MESH
ANY = pl.BlockSpec(memory_space=pl.ANY)

NT_DIMS = (((1,), (1,)), ((), ()))
TN_DIMS = (((0,), (0,)), ((), ()))
NN_DIMS = (((1,), (0,)), ((), ()))


def _params(*sem):
    return pltpu.CompilerParams(dimension_semantics=sem if sem else None, vmem_limit_bytes=VMEM_LIMIT)


def _dot(a, b, dims=NN_DIMS):
    return lax.dot_general(a, b, dims, preferred_element_type=F32)


def _split_dot(x, mat, pieces):
    out = None
    rem = x
    for _ in range(pieces):
        part = rem.astype(BF16)
        rem = rem - part.astype(F32)
        d = _dot(part, mat)
        out = d if out is None else out + d
    return out


def _pick(n, prefs):
    for p in prefs:
        if n % p == 0:
            return p
    return n


def _gather_phases(ins, outs, sems):
    n = len(ins)
    if n == 0:
        return (lambda: None,) * 3
    send_sems, recv_sems, local_sems = sems
    x, y, c = lax.axis_index("x"), lax.axis_index("y"), lax.axis_index("c")
    me, sibling = (x, y, c), (x, y, 1 - c)
    chips = [(1 - x, y), (x, 1 - y), (1 - x, 1 - y)]

    def slot(a, px, py, pc):
        return outs[a].at[4 * px + 2 * py + pc]

    def copy(a, k, block, to, src=None):
        return pltpu.make_async_remote_copy(
            src_ref=slot(a, *block) if src is None else src, dst_ref=slot(a, *block),
            send_sem=send_sems.at[a, k], recv_sem=recv_sems.at[a, k],
            device_id=to, device_id_type=MESH)

    mine = [pltpu.make_async_copy(ins[a], slot(a, *me), local_sems.at[a]) for a in range(n)]
    first = []
    for a in range(n):
        first.append(copy(a, 0, me, sibling, src=ins[a]))
        first += [copy(a, 1 + j, me, (*chip, c), src=ins[a]) for j, chip in enumerate(chips)]
    passed = [copy(a, 4 + j, (*chip, c), sibling) for j, chip in enumerate(chips) for a in range(n)]

    def start():
        for cp in mine + first:
            cp.start()

    def forward():
        k = 0
        for j, chip in enumerate(chips):
            for a in range(n):
                copy(a, 1 + j, (*chip, c), me).wait_recv()
                passed[k].start()
                k += 1

    def finish():
        for a in range(n):
            copy(a, 0, sibling, me).wait_recv()
            for j, chip in enumerate(chips):
                copy(a, 4 + j, (*chip, 1 - c), me).wait_recv()
        for cp in first + passed:
            cp.wait_send()
        for cp in mine:
            cp.wait()

    return start, forward, finish


def _all_gather(arrays, name):
    n = len(arrays)

    def body(*refs):
        for phase in _gather_phases(refs[:n], refs[n:2 * n], refs[2 * n:]):
            phase()

    return pl.pallas_call(
        body, name=name,
        out_shape=[jax.ShapeDtypeStruct((N_DEV,) + a.shape, a.dtype) for a in arrays],
        in_specs=[ANY] * n, out_specs=[ANY] * n,
        scratch_shapes=[pltpu.SemaphoreType.DMA((n, 7)), pltpu.SemaphoreType.DMA((n, 7)),
                        pltpu.SemaphoreType.DMA((n,))],
    )(*arrays)


PEER_FLIPS = [(dx, dy, dc) for dx in (0, 1) for dy in (0, 1) for dc in (0, 1) if (dx, dy, dc) != (0, 0, 0)]


def _exchange_copies(ins, outs, sems, scatter):
    if not ins:
        return []
    send_sems, recv_sems, local_sems = sems
    x, y, c = lax.axis_index("x"), lax.axis_index("y"), lax.axis_index("c")
    me = 4 * x + 2 * y + c
    copies = []
    for a in range(len(ins)):
        copies.append(pltpu.make_async_copy(ins[a].at[me] if scatter else ins[a], outs[a].at[me], local_sems.at[a]))
        for k, (dx, dy, dc) in enumerate(PEER_FLIPS):
            px, py, pc = (1 - x if dx else x), (1 - y if dy else y), (1 - c if dc else c)
            copies.append(pltpu.make_async_remote_copy(
                src_ref=ins[a].at[4 * px + 2 * py + pc] if scatter else ins[a], dst_ref=outs[a].at[me],
                send_sem=send_sems.at[a, k], recv_sem=recv_sems.at[a, k],
                device_id=(px, py, pc), device_id_type=MESH))
    return copies


def _exchange_scratch(n):
    if n == 0:
        return []
    return [pltpu.SemaphoreType.DMA((n, 7)), pltpu.SemaphoreType.DMA((n, 7)), pltpu.SemaphoreType.DMA((n,))]


def _exchange_shapes(arrays, scatter):
    return [jax.ShapeDtypeStruct((N_DEV,) + (a.shape[1:] if scatter else a.shape), a.dtype) for a in arrays]


def _final_exchange(scatter, gather, name):
    ns, ng = len(scatter), len(gather)

    def body(*refs):
        ins, outs, sems = refs[:ns + ng], refs[ns + ng:2 * (ns + ng)], refs[2 * (ns + ng):]
        n_sems = len(_exchange_scratch(ns))
        copies = (_exchange_copies(ins[:ns], outs[:ns], sems[:n_sems], True)
                  + _exchange_copies(ins[ns:], outs[ns:], sems[n_sems:], False))
        for cp in copies:
            cp.start()
        for cp in copies:
            cp.wait()

    res = pl.pallas_call(
        body, name=name, out_shape=_exchange_shapes(scatter, True) + _exchange_shapes(gather, False),
        in_specs=[ANY] * (ns + ng), out_specs=[ANY] * (ns + ng),
        scratch_shapes=_exchange_scratch(ns) + _exchange_scratch(ng),
    )(*scatter, *gather)
    return res[:ns], res[ns:]


MM_ROWS = 512
MM_COLS = 1024


def _matmul(a, b, *, mode, out_dtype, name, tm, tn, res=None, scatter=()):
    if mode == "nn":
        (M, K), (K2, N) = a.shape, b.shape
    else:
        (M, K), (N, K2) = a.shape, b.shape
    assert K == K2, (a.shape, b.shape, mode)
    tm, tn = min(tm, M), min(tn, N)
    sm = min(tm, MM_ROWS)
    sn = tn if tn <= MM_COLS else _pick(tn, (512, 256, 128))
    assert M % tm == 0 and N % tn == 0 and tm % sm == 0, (M, N, K, tm, tn)
    dims = NN_DIMS if mode == "nn" else NT_DIMS
    a_spec = pl.BlockSpec((tm, K), lambda i, j: (i, 0))
    if mode == "nt":
        b_spec = pl.BlockSpec((tn, K), lambda i, j: (j, 0))
    else:
        b_spec = pl.BlockSpec((K, tn), lambda i, j: (0, j))
    o_spec = pl.BlockSpec((tm, tn), lambda i, j: (i, j))
    has_res = res is not None
    n_in, n_ex = 2 + has_res, len(scatter)
    gm, gn = M // tm, N // tn

    def body(*refs):
        a_ref, b_ref = refs[0], refs[1]
        r_ref = refs[2] if has_res else None
        o_ref = refs[n_in + n_ex]
        exchange = (refs[n_in:n_in + n_ex], refs[n_in + n_ex + 1:n_in + 2 * n_ex + 1], refs[n_in + 2 * n_ex + 1:], True)

        @pl.when(jnp.logical_and(pl.program_id(0) == 0, pl.program_id(1) == 0))
        def _():
            for cp in _exchange_copies(*exchange):
                cp.start()

        def chunk(r, carry):
            rows = pl.ds(pl.multiple_of(r * sm, sm), sm)
            av = a_ref[rows, :]
            for c0 in range(0, tn, sn):
                bv = b_ref[c0:c0 + sn, :] if mode == "nt" else b_ref[:, c0:c0 + sn]
                total = _dot(av, bv, dims)
                if has_res:
                    total = total + r_ref[rows, c0:c0 + sn]
                o_ref[rows, c0:c0 + sn] = total.astype(out_dtype)
            return carry

        lax.fori_loop(0, tm // sm, chunk, 0)

        @pl.when(jnp.logical_and(pl.program_id(0) == gm - 1, pl.program_id(1) == gn - 1))
        def _():
            for cp in _exchange_copies(*exchange):
                cp.wait()

    out = pl.pallas_call(
        body, name=name, grid=(gm, gn),
        out_shape=[jax.ShapeDtypeStruct((M, N), out_dtype)] + _exchange_shapes(scatter, True),
        in_specs=[a_spec, b_spec] + ([o_spec] if has_res else []) + [ANY] * n_ex,
        out_specs=[o_spec] + [ANY] * n_ex,
        scratch_shapes=_exchange_scratch(n_ex),
        compiler_params=_params("arbitrary", "arbitrary"),
    )(*((a, b, res) if has_res else (a, b)), *scatter)
    return out if n_ex else out[0]


def _matmul_tn(a, b, *, out_dtype, name, tk=512, sm=256):
    (K, M), (K2, N) = a.shape, b.shape
    assert K == K2 and K % tk == 0 and M % sm == 0, (a.shape, b.shape)
    nk = K // tk

    def body(a_ref, b_ref, o_ref, acc_ref):
        k = pl.program_id(0)

        @pl.when(k == 0)
        def _():
            acc_ref[...] = jnp.zeros_like(acc_ref)

        def chunk(mi, carry):
            cols = pl.ds(pl.multiple_of(mi * sm, sm), sm)
            acc_ref[cols, :] += _dot(a_ref[:, cols].T, b_ref[...])
            return carry

        lax.fori_loop(0, M // sm, chunk, 0)

        @pl.when(k == nk - 1)
        def _():
            def emit(mi, carry):
                rows = pl.ds(pl.multiple_of(mi * sm, sm), sm)
                o_ref[rows, :] = acc_ref[rows, :].astype(out_dtype)
                return carry
            lax.fori_loop(0, M // sm, emit, 0)

    return pl.pallas_call(
        body, name=name, grid=(nk,),
        out_shape=jax.ShapeDtypeStruct((M, N), out_dtype),
        in_specs=[pl.BlockSpec((tk, M), lambda k: (k, 0)), pl.BlockSpec((tk, N), lambda k: (k, 0))],
        out_specs=pl.BlockSpec((M, N), lambda k: (0, 0)),
        scratch_shapes=[pltpu.VMEM((M, N), F32)],
        compiler_params=_params("arbitrary"),
    )(a, b)


def _rmsnorm_fwd(x, gains, name, tr=256):
    n = len(gains)

    def body(*refs):
        x_ref = refs[0]
        xv = x_ref[...]
        r = lax.rsqrt(jnp.mean(xv * xv, axis=-1, keepdims=True) + RMS_EPS)
        y = xv * r
        for a in range(n):
            refs[1 + n + a][...] = (y * refs[1 + a][...]).astype(BF16)

    row = pl.BlockSpec((tr, D), lambda i: (i, 0))
    gain = pl.BlockSpec((1, D), lambda i: (0, 0))
    return pl.pallas_call(
        body, name=name, grid=(S // tr,),
        out_shape=[jax.ShapeDtypeStruct((S, D), BF16)] * n,
        in_specs=[row] + [gain] * n, out_specs=[row] * n,
        compiler_params=_params("parallel"),
    )(x, *gains)


def _matmul_norm_bwd(pairs, x, g, dres, name, scatter=(), tm=512):
    M = x.shape[0]
    n_p, n_ex = len(pairs), len(scatter)
    n_in = 2 * n_p + 3
    steps = M // tm

    def body(*refs):
        x_ref, g_ref, dres_ref = refs[2 * n_p:n_in]
        dx_ref, dxb_ref, dg_ref = refs[n_in + n_ex:n_in + n_ex + 3]
        exchange = (refs[n_in:n_in + n_ex], refs[n_in + n_ex + 3:n_in + 2 * n_ex + 3], refs[n_in + 2 * n_ex + 3:], True)

        @pl.when(pl.program_id(0) == 0)
        def _():
            for cp in _exchange_copies(*exchange):
                cp.start()

        dyv = _dot(refs[0][...], refs[1][...], NT_DIMS)
        for p in range(1, n_p):
            dyv = dyv + _dot(refs[2 * p][...], refs[2 * p + 1][...], NT_DIMS)
        xv = x_ref[...]
        r = lax.rsqrt(jnp.mean(xv * xv, axis=-1, keepdims=True) + RMS_EPS)
        xhat = xv * r
        dxhat = dyv * g_ref[...]
        mean_term = jnp.mean(dxhat * xhat, axis=-1, keepdims=True)
        dx = r * (dxhat - xhat * mean_term) + dres_ref[...]
        dx_ref[...] = dx
        dxb_ref[...] = dx.astype(BF16)
        part = jnp.sum(dyv * xhat, axis=0, keepdims=True)

        @pl.when(pl.program_id(0) == 0)
        def _():
            dg_ref[...] = part

        @pl.when(pl.program_id(0) > 0)
        def _():
            dg_ref[...] += part

        @pl.when(pl.program_id(0) == steps - 1)
        def _():
            for cp in _exchange_copies(*exchange):
                cp.wait()

    row = pl.BlockSpec((tm, D), lambda i: (i, 0))
    gain = pl.BlockSpec((1, D), lambda i: (0, 0))
    pair_specs, operands = [], []
    for a, b in pairs:
        assert a.shape == (M, b.shape[1]) and b.shape[0] == D, (a.shape, b.shape)
        pair_specs += [pl.BlockSpec((tm, a.shape[1]), lambda i: (i, 0)), pl.BlockSpec(b.shape, lambda i: (0, 0))]
        operands += [a, b]
    return pl.pallas_call(
        body, name=name, grid=(steps,),
        out_shape=[jax.ShapeDtypeStruct((M, D), F32), jax.ShapeDtypeStruct((M, D), BF16),
                   jax.ShapeDtypeStruct((1, D), F32)] + _exchange_shapes(scatter, True),
        in_specs=pair_specs + [row, gain, row] + [ANY] * n_ex,
        out_specs=[row, row, gain] + [ANY] * n_ex,
        scratch_shapes=_exchange_scratch(n_ex),
        compiler_params=_params("arbitrary"),
    )(*operands, x, g, dres, *scatter)


def _final_loss(x, target, g, name, tr=256):
    def body(x_ref, t_ref, g_ref, loss_ref, dx_ref, dxb_ref, dg_ref):
        xv = x_ref[...]
        gv = g_ref[...]
        r = lax.rsqrt(jnp.mean(xv * xv, axis=-1, keepdims=True) + RMS_EPS)
        xhat = xv * r
        err = xhat * gv - t_ref[...]
        row_loss = jnp.mean(err * err, axis=-1, keepdims=True)
        lpart = 0.5 * jnp.sum(row_loss, axis=0, keepdims=True)
        dyv = err / D
        dxhat = dyv * gv
        mean_term = jnp.mean(dxhat * xhat, axis=-1, keepdims=True)
        dx = r * (dxhat - xhat * mean_term)
        dx_ref[...] = dx
        dxb_ref[...] = dx.astype(BF16)
        gpart = jnp.sum(dyv * xhat, axis=0, keepdims=True)

        @pl.when(pl.program_id(0) == 0)
        def _():
            dg_ref[...] = gpart
            loss_ref[...] = jnp.broadcast_to(lpart, loss_ref.shape)

        @pl.when(pl.program_id(0) > 0)
        def _():
            dg_ref[...] += gpart
            loss_ref[...] += jnp.broadcast_to(lpart, loss_ref.shape)

    row = pl.BlockSpec((tr, D), lambda i: (i, 0))
    gain = pl.BlockSpec((1, D), lambda i: (0, 0))
    lspec = pl.BlockSpec((8, LANES), lambda i: (0, 0))
    return pl.pallas_call(
        body, name=name, grid=(S // tr,),
        out_shape=[jax.ShapeDtypeStruct((8, LANES), F32), jax.ShapeDtypeStruct((S, D), F32),
                   jax.ShapeDtypeStruct((S, D), BF16), jax.ShapeDtypeStruct((1, D), F32)],
        in_specs=[row, row, gain], out_specs=[lspec, row, row, gain],
        compiler_params=_params("arbitrary"),
    )(x, target, g)


CONV_TR = 128
CONV_TC = D_FF
CONV_NJ = D_FF // CONV_TC
HALO = 16


def _causal_taps(cur_ref, prev_ref, first):
    xv = cur_ref[...].astype(F32)
    pv = prev_ref[...].astype(F32)
    p1 = jnp.where(first, 0.0, pv[HALO - 1:HALO, :])
    p2 = jnp.where(first, 0.0, pv[HALO - 2:HALO - 1, :])
    r1, r2 = pltpu.roll(xv, 1, 0), pltpu.roll(xv, 2, 0)
    row = lax.broadcasted_iota(jnp.int32, (8, xv.shape[1]), 0)
    xm1 = jnp.concatenate([jnp.where(row == 0, p1, r1[0:8]), r1[8:]], axis=0)
    xm2 = jnp.concatenate([jnp.where(row == 0, p2, jnp.where(row == 1, p1, r2[0:8])), r2[8:]], axis=0)
    return xv, xm1, xm2


def _conv_specs():
    def prev_row(i):
        return jnp.maximum(i * (CONV_TR // HALO) - 1, 0)
    ua = pl.BlockSpec((CONV_TR, CONV_TC), lambda i, j: (i, j))
    ug = ua
    pa = pl.BlockSpec((HALO, CONV_TC), lambda i, j: (prev_row(i), j))
    pg = pa
    wa = pl.BlockSpec((3, CONV_TC), lambda i, j: (0, j))
    wg = pl.BlockSpec((3, CONV_TC), lambda i, j: (0, j + CONV_NJ))
    ba = pl.BlockSpec((1, CONV_TC), lambda i, j: (0, j))
    bg = pl.BlockSpec((1, CONV_TC), lambda i, j: (0, j + CONV_NJ))
    return [ua, pa, ug, pg, wa, wg, ba, bg]


def _convgate_fwd(u_a, u_g, w, b, name):
    def body(ua, pa, ug, pg, wa, wg, ba, bg, o_ref):
        first = pl.program_id(0) == 0
        x0, x1, x2 = _causal_taps(ua, pa, first)
        ac = wa[0:1, :] * x2 + wa[1:2, :] * x1 + wa[2:3, :] * x0 + ba[...]
        x0, x1, x2 = _causal_taps(ug, pg, first)
        gc = wg[0:1, :] * x2 + wg[1:2, :] * x1 + wg[2:3, :] * x0 + bg[...]
        sg = 0.5 * jnp.tanh(0.5 * gc) + 0.5
        o_ref[...] = (gc * sg * ac).astype(BF16)

    return pl.pallas_call(
        body, name=name, grid=(S // CONV_TR, CONV_NJ),
        out_shape=jax.ShapeDtypeStruct((S, D_FF), BF16),
        in_specs=_conv_specs(),
        out_specs=pl.BlockSpec((CONV_TR, CONV_TC), lambda i, j: (i, j)),
        compiler_params=_params("parallel", "parallel"),
    )(u_a, u_a, u_g, u_g, w, w, b, b)


def _anticausal_conv(d, nxt_ref, w_ref, last):
    n1 = jnp.where(last, 0.0, nxt_ref[0:1, :])
    n2 = jnp.where(last, 0.0, nxt_ref[1:2, :])
    r1, r2 = pltpu.roll(d, CONV_TR - 1, 0), pltpu.roll(d, CONV_TR - 2, 0)
    row = lax.broadcasted_iota(jnp.int32, (8, d.shape[1]), 0)
    cut = CONV_TR - 8
    dp1 = jnp.concatenate([r1[:cut], jnp.where(row == 7, n1, r1[cut:])], axis=0)
    dp2 = jnp.concatenate([r2[:cut], jnp.where(row == 7, n2, jnp.where(row == 6, n1, r2[cut:]))], axis=0)
    return w_ref[2:3, :] * d + w_ref[1:2, :] * dp1 + w_ref[0:1, :] * dp2


def _convgate_bwd(u_a, u_g, w, b, dact, name):
    n_i = S // CONV_TR

    def body(ua, pa, ug, pg, wa, wg, ba, bg, d_ref, dua_ref, dug_ref, dwa_ref, dwg_ref, dba_ref, dbg_ref,
             nxt_a, nxt_g):
        i = pl.program_id(1)
        last = i == 0
        first = i == n_i - 1
        a0, a1, a2 = _causal_taps(ua, pa, first)
        ac = wa[0:1, :] * a2 + wa[1:2, :] * a1 + wa[2:3, :] * a0 + ba[...]
        g0, g1, g2 = _causal_taps(ug, pg, first)
        gc = wg[0:1, :] * g2 + wg[1:2, :] * g1 + wg[2:3, :] * g0 + bg[...]
        sg = 0.5 * jnp.tanh(0.5 * gc) + 0.5
        dact_v = d_ref[...].astype(F32)
        da = dact_v * (gc * sg)
        dg = dact_v * ac * (sg * (1.0 + gc * (1.0 - sg)))
        dua_ref[...] = _anticausal_conv(da, nxt_a, wa, last).astype(BF16)
        dug_ref[...] = _anticausal_conv(dg, nxt_g, wg, last).astype(BF16)
        nxt_a[...] = da[0:8]
        nxt_g[...] = dg[0:8]

        def col(v):
            return jnp.sum(v, axis=0, keepdims=True)

        parts = [col(da * a2), col(da * a1), col(da * a0), col(dg * g2), col(dg * g1), col(dg * g0),
                 col(da), col(dg)]

        @pl.when(last)
        def _():
            for k in range(3):
                dwa_ref[k:k + 1, :] = parts[k]
                dwg_ref[k:k + 1, :] = parts[3 + k]
            dba_ref[...] = parts[6]
            dbg_ref[...] = parts[7]

        @pl.when(i > 0)
        def _():
            for k in range(3):
                dwa_ref[k:k + 1, :] += parts[k]
                dwg_ref[k:k + 1, :] += parts[3 + k]
            dba_ref[...] += parts[6]
            dbg_ref[...] += parts[7]

    def swap(spec):
        return pl.BlockSpec(spec.block_shape, lambda j, i, f=spec.index_map: f(n_i - 1 - i, j))

    blk = pl.BlockSpec((CONV_TR, CONV_TC), lambda j, i: (n_i - 1 - i, j))
    w3 = pl.BlockSpec((3, CONV_TC), lambda j, i: (0, j))
    b1 = pl.BlockSpec((1, CONV_TC), lambda j, i: (0, j))
    return pl.pallas_call(
        body, name=name, grid=(CONV_NJ, n_i),
        out_shape=[jax.ShapeDtypeStruct((S, D_FF), BF16), jax.ShapeDtypeStruct((S, D_FF), BF16),
                   jax.ShapeDtypeStruct((3, D_FF), F32), jax.ShapeDtypeStruct((3, D_FF), F32),
                   jax.ShapeDtypeStruct((1, D_FF), F32), jax.ShapeDtypeStruct((1, D_FF), F32)],
        in_specs=[swap(s) for s in _conv_specs()] + [blk],
        out_specs=[blk, blk, w3, w3, b1, b1],
        scratch_shapes=[pltpu.VMEM((8, CONV_TC), F32), pltpu.VMEM((8, CONV_TC), F32)],
        compiler_params=_params("arbitrary", "arbitrary"),
    )(u_a, u_a, u_g, u_g, w, w, b, b, dact)


FOX_T = 512
FOX_TQ, FOX_TK = 512, 512
FOX_FORWARD_AT = 4
N_PAIRS = A_HEADS // 2


def _lane_masks():
    lane = lax.broadcasted_iota(jnp.int32, (1, LANES), 1)
    return lane, (lane < HEAD_DIM, lane >= HEAD_DIM)


def _fox_prep_fwd(z_t, b, name):
    def body(z_ref, b_ref, c_ref):
        r = lax.broadcasted_iota(jnp.int32, (LANES, LANES), 0)
        cc = lax.broadcasted_iota(jnp.int32, (LANES, LANES), 1)
        upper = (r <= cc).astype(BF16)
        carry = jnp.zeros((A_HEADS, 1), F32)
        for blk in range(S // LANES):
            sl = slice(blk * LANES, (blk + 1) * LANES)
            z = z_ref[:, sl] + b_ref[...]
            lf = jnp.minimum(z, 0.0) - jnp.log(1.0 + jnp.exp(-jnp.abs(z)))
            cs = _split_dot(lf, upper, 3) + carry
            c_ref[:, sl] = cs
            carry = cs[:, LANES - 1:LANES]

    return pl.pallas_call(
        body, name=name, out_shape=jax.ShapeDtypeStruct((A_HEADS, S), F32),
        compiler_params=_params(),
    )(z_t, b)


def _fox_prep_bwd(drow_t, dcol_t, z_t, b, name):
    def body(dr_ref, dc_ref, z_ref, b_ref, dz_ref, db_ref):
        r = lax.broadcasted_iota(jnp.int32, (LANES, LANES), 0)
        cc = lax.broadcasted_iota(jnp.int32, (LANES, LANES), 1)
        lower = (r >= cc).astype(BF16)
        carry = jnp.zeros((A_HEADS, 1), F32)
        db = jnp.zeros((A_HEADS, 1), F32)
        for blk in reversed(range(S // LANES)):
            sl = slice(blk * LANES, (blk + 1) * LANES)
            rc = _split_dot(dr_ref[:, sl] - dc_ref[:, sl], lower, 3) + carry
            carry = rc[:, 0:1]
            z = z_ref[:, sl] + b_ref[...]
            dz = rc / (1.0 + jnp.exp(z))
            dz_ref[:, sl] = dz
            db = db + jnp.sum(dz, axis=1, keepdims=True)
        db_ref[...] = db

    return pl.pallas_call(
        body, name=name,
        out_shape=[jax.ShapeDtypeStruct((A_HEADS, S), F32), jax.ShapeDtypeStruct((A_HEADS, 1), F32)],
        compiler_params=_params(),
    )(drow_t, dcol_t, z_t, b)


def _fox_fwd(qkv, c_t2, name, gather):
    tq, tk = FOX_TQ, FOX_TK

    n = len(gather)

    def body(*refs):
        q_ref, k_ref, v_ref, ct_ref = refs[:4]
        o_ref, lse_ref = refs[4 + n:6 + n]
        s_scr, p_scr, acc_scr = refs[-5:-3], refs[-3:-1], refs[-1]
        qi = pl.program_id(1)

        gather_start, gather_forward, gather_finish = _gather_phases(
            refs[4:4 + n], refs[6 + n:6 + 2 * n], refs[6 + 2 * n:len(refs) - 5])

        @pl.when(jnp.logical_and(pl.program_id(0) == 0, qi == 0))
        def _():
            gather_start()

        @pl.when(jnp.logical_and(pl.program_id(0) == FOX_FORWARD_AT, qi == 0))
        def _():
            gather_forward()

        n_full = jnp.right_shift(qi, (tk // tq).bit_length() - 1)
        lane, masks = _lane_masks()
        q = q_ref[...] * SCALE
        qs = [jnp.where(masks[e], q, jnp.zeros_like(q)) for e in range(2)]

        def scores_into(j, slot):
            start = pl.multiple_of(j * tk, tk)
            kb = k_ref[pl.ds(start, tk), :]
            for e in range(2):
                s_scr[slot][e] = _dot(qs[e], kb, NT_DIMS) - ct_ref[e:e + 1, pl.ds(start, tk)]

        def softmax_of(slot, m, masked):
            m_new, alpha = [], []
            for e in range(2):
                s = s_scr[slot][e]
                if masked:
                    rows = lax.broadcasted_iota(jnp.int32, (tq, tk), 0) + (qi * tq - n_full * tk)
                    cols = lax.broadcasted_iota(jnp.int32, (tq, tk), 1)
                    s = jnp.where(cols <= rows, s, NEG)
                m_new.append(jnp.maximum(m[e], jnp.max(s, axis=1, keepdims=True)))
                p_scr[slot][e] = jnp.exp(s - m_new[e]).astype(BF16)
                alpha.append(jnp.exp(m[e] - m_new[e]))
            return tuple(m_new), tuple(alpha)

        def values_of(j, slot, alpha):
            start = pl.multiple_of(j * tk, tk)
            vb = v_ref[pl.ds(start, tk), :]
            for e in range(2):
                acc_scr[e] = (alpha[e] * acc_scr[e]
                              + _dot(p_scr[slot][e], jnp.where(masks[e], vb, jnp.ones_like(vb))))

        def stage(j, cur, nxt, carry):
            m, a_prev = carry
            scores_into(j + 1, nxt)
            values_of(jnp.maximum(j - 1, 0), nxt, a_prev)
            return softmax_of(cur, m, False)

        def finish(cur, nxt, carry):
            m, a_prev = carry
            values_of(jnp.maximum(n_full - 1, 0), nxt, a_prev)
            (m0, m1), alpha = softmax_of(cur, m, True)
            values_of(n_full, cur, alpha)
            l0 = acc_scr[0][:, HEAD_DIM:HEAD_DIM + 1]
            l1 = acc_scr[1][:, 0:1]
            o_ref[...] = jnp.where(masks[0], acc_scr[0] / l0, acc_scr[1] / l1).astype(BF16)
            lse_ref[...] = jnp.where(masks[0], m0 + jnp.log(l0), m1 + jnp.log(l1))

        scores_into(0, 0)
        for e in range(2):
            p_scr[1][e] = jnp.zeros((tq, tk), BF16)
            acc_scr[e] = jnp.zeros((tq, LANES), F32)
        two = lambda x: (x, x)
        init = (two(jnp.full((tq, 1), NEG, F32)), two(jnp.ones((tq, 1), F32)))

        def two_stages(jj, carry):
            return stage(2 * jj + 1, 1, 0, stage(2 * jj, 0, 1, carry))

        carry = lax.fori_loop(0, jnp.right_shift(n_full, 1), two_stages, init)
        odd = jnp.bitwise_and(n_full, 1) == 1

        @pl.when(odd)
        def _():
            finish(1, 0, stage(n_full - 1, 0, 1, carry))

        @pl.when(jnp.logical_not(odd))
        def _():
            finish(0, 1, carry)

        @pl.when(jnp.logical_and(pl.program_id(0) == N_PAIRS - 1, qi == S // tq - 1))
        def _():
            gather_finish()

    qspec = pl.BlockSpec((tq, LANES), lambda h, i: (i, h))
    return pl.pallas_call(
        body, name=name, grid=(N_PAIRS, S // tq),
        out_shape=[jax.ShapeDtypeStruct((S, D), BF16), jax.ShapeDtypeStruct((S, D), F32)]
        + _exchange_shapes(gather, False),
        in_specs=[qspec,
                  pl.BlockSpec((S, LANES), lambda h, i: (0, N_PAIRS + h)),
                  pl.BlockSpec((S, LANES), lambda h, i: (0, 2 * N_PAIRS + h)),
                  pl.BlockSpec((None, 2, S), lambda h, i: (h, 0, 0))] + [ANY] * n,
        out_specs=[qspec, qspec] + [ANY] * n,
        scratch_shapes=_exchange_scratch(n) + [
            pltpu.VMEM((2, tq, tk), F32), pltpu.VMEM((2, tq, tk), F32),
            pltpu.VMEM((2, tq, tk), BF16), pltpu.VMEM((2, tq, tk), BF16),
            pltpu.VMEM((2, tq, LANES), F32)],
        compiler_params=_params("arbitrary", "arbitrary"),
    )(qkv, qkv, qkv, c_t2, *gather)


def _head_rowsum(a, b, name, tr=256):
    C = a.shape[1]

    def body(a_ref, b_ref, o_ref):
        r = lax.broadcasted_iota(jnp.int32, (LANES, LANES), 0) < HEAD_DIM
        cc = lax.broadcasted_iota(jnp.int32, (LANES, LANES), 1) < HEAD_DIM
        same_head = (r == cc).astype(BF16)
        for blk in range(C // LANES):
            sl = slice(blk * LANES, (blk + 1) * LANES)
            prod = a_ref[:, sl].astype(F32) * b_ref[:, sl].astype(F32)
            o_ref[:, sl] = _split_dot(prod, same_head, 2)

    row = pl.BlockSpec((tr, C), lambda i: (i, 0))
    return pl.pallas_call(
        body, name=name, grid=(S // tr,), out_shape=jax.ShapeDtypeStruct((S, C), F32),
        in_specs=[row, row], out_specs=row, compiler_params=_params("parallel"),
    )(a, b)


def _fox_bwd(qkv, do, lse, delta, c_t2, name, scatter):
    t = FOX_T
    nq = S // t

    n = len(scatter)

    def body(*refs):
        q_ref, k_ref, v_ref, do_ref, lse_ref, dl_ref, ct_ref = refs[:7]
        dq_ref, dk_ref, dv_ref, dcol_ref, drow_ref = refs[7 + n:12 + n]
        exchange = (refs[7:7 + n], refs[12 + n:12 + 2 * n], refs[12 + 2 * n:len(refs) - 5], True)
        sd_scr, pd_scr, acc_scr = refs[-5:-3], refs[-3:-1], refs[-1]
        kj = pl.program_id(1)

        @pl.when(jnp.logical_and(pl.program_id(0) == 0, kj == 0))
        def _():
            for cp in _exchange_copies(*exchange):
                cp.start()

        @pl.when(kj == 0)
        def _():
            dq_ref[...] = jnp.zeros_like(dq_ref)
            drow_ref[...] = jnp.zeros_like(drow_ref)

        lane, masks = _lane_masks()
        k = k_ref[...]
        v = v_ref[...]
        k_aug = [jnp.where(masks[e], k * SCALE, jnp.ones_like(k)) for e in range(2)]
        cs = [ct_ref[e:e + 1, :] for e in range(2)]

        def rows_of(i):
            r0 = pl.multiple_of(i * t, t)
            return pl.ds(r0, t), q_ref[pl.ds(r0, t), :] * SCALE, do_ref[pl.ds(r0, t), :]

        def scores_into(i, slot):
            _, qb, dob = rows_of(i)
            for e in range(2):
                qe = jnp.where(masks[e], qb, jnp.zeros_like(qb))
                doe = jnp.where(masks[e], dob, jnp.zeros_like(dob))
                sd_scr[slot][2 * e] = _dot(qe, k, NT_DIMS) - cs[e]
                sd_scr[slot][2 * e + 1] = _dot(doe, v, NT_DIMS)

        def pointwise(i, slot, masked):
            rows, _, _ = rows_of(i)
            for e in range(2):
                lo = e * HEAD_DIM
                s = sd_scr[slot][2 * e]
                if masked:
                    r = lax.broadcasted_iota(jnp.int32, (t, t), 0)
                    c = lax.broadcasted_iota(jnp.int32, (t, t), 1)
                    s = jnp.where(c <= r, s, NEG)
                p = jnp.exp(s - lse_ref[rows, lo:lo + 1])
                pd_scr[slot][2 * e] = p.astype(BF16)
                pd_scr[slot][2 * e + 1] = (p * (sd_scr[slot][2 * e + 1] - dl_ref[rows, lo:lo + 1])).astype(BF16)

        def accumulate(i, slot):
            rows, qb, dob = rows_of(i)
            dq_parts = []
            for e in range(2):
                p, ds = pd_scr[slot][2 * e], pd_scr[slot][2 * e + 1]
                q_aug = jnp.where(masks[e], qb, jnp.ones_like(qb))
                doe = jnp.where(masks[e], dob, jnp.zeros_like(dob))
                acc_scr[2] += _dot(p, doe, TN_DIMS)
                acc_scr[e] += _dot(ds, q_aug, TN_DIMS)
                dq_parts.append(_dot(ds, k_aug[e]))
            dq_ref[rows, :] += jnp.where(masks[0], dq_parts[0], dq_parts[1])
            drow_ref[rows, :] += jnp.where(masks[0], dq_parts[1], dq_parts[0])

        def stage(i, cur, nxt):
            scores_into(jnp.minimum(i + 1, nq - 1), nxt)
            accumulate(i - 1, nxt)
            pointwise(i, cur, False)

        acc_scr[...] = jnp.zeros_like(acc_scr)
        scores_into(kj, 0)
        pointwise(kj, 0, True)
        scores_into(jnp.minimum(kj + 1, nq - 1), 1)
        rest = nq - 1 - kj

        def two_stages(jj, carry):
            stage(kj + 1 + 2 * jj, 1, 0)
            stage(kj + 2 + 2 * jj, 0, 1)
            return carry

        lax.fori_loop(0, jnp.right_shift(rest, 1), two_stages, 0)
        odd = jnp.bitwise_and(rest, 1) == 1

        @pl.when(odd)
        def _():
            stage(nq - 1, 1, 0)
            accumulate(nq - 1, 1)

        @pl.when(jnp.logical_not(odd))
        def _():
            accumulate(nq - 1, 0)

        dk0, dk1, dv = acc_scr[0], acc_scr[1], acc_scr[2]
        dk_ref[...] = jnp.where(masks[0], dk0, dk1).astype(BF16)
        dcol_ref[...] = jnp.where(masks[0], dk1, dk0)
        dv_ref[...] = dv.astype(BF16)

        @pl.when(jnp.logical_and(pl.program_id(0) == N_PAIRS - 1, kj == nq - 1))
        def _():
            for cp in _exchange_copies(*exchange):
                cp.wait()

    full = lambda off: pl.BlockSpec((S, LANES), lambda h, j, off=off: (0, off + h))
    kv = lambda off: pl.BlockSpec((t, LANES), lambda h, j, off=off: (j, off + h))
    return pl.pallas_call(
        body, name=name, grid=(N_PAIRS, nq),
        out_shape=[jax.ShapeDtypeStruct((S, D), F32), jax.ShapeDtypeStruct((S, D), BF16),
                   jax.ShapeDtypeStruct((S, D), BF16), jax.ShapeDtypeStruct((S, D), F32),
                   jax.ShapeDtypeStruct((S, D), F32)] + _exchange_shapes(scatter, True),
        in_specs=[full(0), kv(N_PAIRS), kv(2 * N_PAIRS), full(0), full(0), full(0),
                  pl.BlockSpec((None, 2, t), lambda h, j: (h, 0, j))] + [ANY] * n,
        out_specs=[full(0), kv(0), kv(0), kv(0), full(0)] + [ANY] * n,
        scratch_shapes=_exchange_scratch(n) + [
            pltpu.VMEM((4, t, t), F32), pltpu.VMEM((4, t, t), F32),
            pltpu.VMEM((4, t, t), BF16), pltpu.VMEM((4, t, t), BF16),
            pltpu.VMEM((3, t, LANES), F32)],
        compiler_params=_params("arbitrary", "arbitrary"),
    )(qkv, qkv, qkv, do, lse, delta, c_t2, *scatter)


B_PAIRS = 4
B_NB = S // B_W


def _group_consts(g):
    nbs = jnp.where(g == 0, B_NB // B_DILS[0], jnp.where(g == 1, B_NB // B_DILS[1], B_NB // B_DILS[2]))
    dil = jnp.where(g == 0, B_DILS[0], jnp.where(g == 1, B_DILS[1], B_DILS[2]))
    return nbs, dil


def _band(dil):
    qi = lax.broadcasted_iota(jnp.int32, (B_W, B_W), 0)
    kj = lax.broadcasted_iota(jnp.int32, (B_W, B_W), 1)
    dist_c = qi - kj
    dist_p = qi + B_W - kj
    return (dist_c * dil).astype(F32), dist_c >= 0, (dist_p * dil).astype(F32), dist_p <= B_W


def _dil_fwd(qp, kp, vp, slopes, name):
    def body(sl_ref, q_ref, kp_ref, kc_ref, vp_ref, vc_ref, o_ref, lse_ref):
        g, n = pl.program_id(0), pl.program_id(1)
        nbs, dil = _group_consts(g)
        has_prev = (n % nbs) != 0
        lane, masks = _lane_masks()
        bias_c, ok_c, bias_p, ok_p = _band(dil)
        ok_p = jnp.logical_and(ok_p, has_prev)
        heads = [(hp, e) for hp in range(B_PAIRS) for e in range(2)]
        col = lambda ref, hp: ref[:, hp * LANES:(hp + 1) * LANES]
        logits = []
        for hp, e in heads:
            q = col(q_ref, hp) * SCALE
            qe = jnp.where(masks[e], q, jnp.zeros_like(q))
            logits.append((_dot(qe, col(kc_ref, hp), NT_DIMS), _dot(qe, col(kp_ref, hp), NT_DIMS)))
        probs = []
        for (hp, e), (sc, sp) in zip(heads, logits):
            slope = sl_ref[g * 8 + 2 * hp + e]
            sc = jnp.where(ok_c, sc - slope * bias_c, NEG)
            sp = jnp.where(ok_p, sp - slope * bias_p, NEG)
            m = jnp.maximum(jnp.max(sc, axis=1, keepdims=True), jnp.max(sp, axis=1, keepdims=True))
            probs.append((jnp.exp(sc - m).astype(BF16), jnp.exp(sp - m).astype(BF16), m))
        outs, lses = [], []
        for (hp, e), (pc, pp, m) in zip(heads, probs):
            vc, vpv = col(vc_ref, hp), col(vp_ref, hp)
            acc = (_dot(pc, jnp.where(masks[e], vc, jnp.ones_like(vc)))
                   + _dot(pp, jnp.where(masks[e], vpv, jnp.ones_like(vpv))))
            l = acc[:, HEAD_DIM:HEAD_DIM + 1] if e == 0 else acc[:, 0:1]
            outs.append(acc / l)
            lses.append(m + jnp.log(l))
        o_ref[...] = jnp.concatenate(
            [jnp.where(masks[0], outs[2 * hp], outs[2 * hp + 1]) for hp in range(B_PAIRS)], axis=1).astype(BF16)
        lse = jnp.zeros((B_W, LANES), F32)
        for h in range(2 * B_PAIRS):
            lse = jnp.where(lane == h, lses[h], lse)
        lse_ref[...] = lse

    cur = pl.BlockSpec((None, B_W, B_OUT), lambda g, n, sl: (g, n, 0))
    prev = pl.BlockSpec((None, B_W, B_OUT), lambda g, n, sl: (g, jnp.maximum(n - 1, 0), 0))
    stat = pl.BlockSpec((None, B_W, LANES), lambda g, n, sl: (g, n, 0))
    return pl.pallas_call(
        body, name=name,
        grid_spec=pltpu.PrefetchScalarGridSpec(
            num_scalar_prefetch=1, grid=(3, B_NB),
            in_specs=[cur, prev, cur, prev, cur], out_specs=[cur, stat]),
        out_shape=[jax.ShapeDtypeStruct((3, S, B_OUT), BF16), jax.ShapeDtypeStruct((3, S, LANES), F32)],
        compiler_params=_params("parallel", "parallel"),
    )(slopes, qp, kp, kp, vp, vp)


def _head_expander():
    r = lax.broadcasted_iota(jnp.int32, (LANES, B_OUT), 0)
    c = lax.broadcasted_iota(jnp.int32, (LANES, B_OUT), 1)
    return jnp.logical_and(c >= r * HEAD_DIM, c < (r + 1) * HEAD_DIM).astype(BF16)


def _dil_merge(og, lseg, name, tr=256):
    def body(o_ref, l_ref, out_ref, lse_ref):
        l0, l1, l2 = l_ref[0], l_ref[1], l_ref[2]
        m = jnp.maximum(jnp.maximum(l0, l1), l2)
        w0, w1, w2 = jnp.exp(l0 - m), jnp.exp(l1 - m), jnp.exp(l2 - m)
        den = w0 + w1 + w2
        lse_ref[...] = m + jnp.log(den)
        expand = _head_expander()
        out = None
        for g, w in enumerate((w0, w1, w2)):
            part = _split_dot(w / den, expand, 3) * o_ref[g].astype(F32)
            out = part if out is None else out + part
        out_ref[...] = out.astype(BF16)

    blk3 = pl.BlockSpec((3, tr, B_OUT), lambda i: (0, i, 0))
    stat3 = pl.BlockSpec((3, tr, LANES), lambda i: (0, i, 0))
    blk = pl.BlockSpec((tr, B_OUT), lambda i: (i, 0))
    stat = pl.BlockSpec((tr, LANES), lambda i: (i, 0))
    return pl.pallas_call(
        body, name=name, grid=(S // tr,),
        out_shape=[jax.ShapeDtypeStruct((S, B_OUT), BF16), jax.ShapeDtypeStruct((S, LANES), F32)],
        in_specs=[blk3, stat3], out_specs=[blk, stat], compiler_params=_params("parallel"),
    )(og, lseg)


def _head_rowsum_compact(a, b, name, tr=256):
    def body(a_ref, b_ref, o_ref):
        r = lax.broadcasted_iota(jnp.int32, (B_OUT, LANES), 0)
        c = lax.broadcasted_iota(jnp.int32, (B_OUT, LANES), 1)
        collect = jnp.logical_and(r >= c * HEAD_DIM, r < (c + 1) * HEAD_DIM).astype(BF16)
        prod = a_ref[...].astype(F32) * b_ref[...].astype(F32)
        o_ref[...] = _split_dot(prod, collect, 2)

    row = pl.BlockSpec((tr, B_OUT), lambda i: (i, 0))
    return pl.pallas_call(
        body, name=name, grid=(S // tr,), out_shape=jax.ShapeDtypeStruct((S, LANES), F32),
        in_specs=[row, row], out_specs=pl.BlockSpec((tr, LANES), lambda i: (i, 0)),
        compiler_params=_params("parallel"),
    )(a, b)


def _dil_bwd(qp, kp, vp, dop, lsep, dlp, slopes, name, scatter):
    n_ex = len(scatter)

    def body(sl_ref, *refs):
        (qc_ref, qn_ref, kp_ref, kc_ref, vp_ref, vc_ref, doc_ref, don_ref,
         lc_ref, ln_ref, dc_ref, dn_ref) = refs[:12]
        dq_ref, dk_ref, dv_ref = refs[12 + n_ex:15 + n_ex]
        exchange = (refs[12:12 + n_ex], refs[15 + n_ex:15 + 2 * n_ex], refs[15 + 2 * n_ex:], True)
        g, n = pl.program_id(0), pl.program_id(1)

        @pl.when(jnp.logical_and(g == 0, n == 0))
        def _():
            for cp in _exchange_copies(*exchange):
                cp.start()

        nbs, dil = _group_consts(g)
        has_prev = (n % nbs) != 0
        has_next = jnp.logical_and(n + 1 < B_NB, ((n + 1) % nbs) != 0)
        lane, masks = _lane_masks()
        bias_c, ok_c, bias_p, ok_p = _band(dil)
        ok_pp = jnp.logical_and(ok_p, has_prev)
        ok_np = jnp.logical_and(ok_p, has_next)
        heads = [(hp, e) for hp in range(B_PAIRS) for e in range(2)]
        col = lambda ref, hp: ref[:, hp * LANES:(hp + 1) * LANES]
        mask = lambda t, e: jnp.where(masks[e], t, jnp.zeros_like(t))
        raw = []
        for hp, e in heads:
            qce, qne = mask(col(qc_ref, hp) * SCALE, e), mask(col(qn_ref, hp) * SCALE, e)
            doce, done = mask(col(doc_ref, hp), e), mask(col(don_ref, hp), e)
            kc, kpv, vc, vpv = col(kc_ref, hp), col(kp_ref, hp), col(vc_ref, hp), col(vp_ref, hp)
            raw.append(((_dot(qce, kc, NT_DIMS), _dot(doce, vc, NT_DIMS)),
                        (_dot(qce, kpv, NT_DIMS), _dot(doce, vpv, NT_DIMS)),
                        (_dot(qne, kc, NT_DIMS), _dot(done, vc, NT_DIMS))))
        pds = []
        for (hp, e), tiles in zip(heads, raw):
            lo = 2 * hp + e
            slope = sl_ref[g * 8 + 2 * hp + e]
            lse_c, dl_c = lc_ref[:, lo:lo + 1], dc_ref[:, lo:lo + 1]
            lse_n, dl_n = ln_ref[:, lo:lo + 1], dn_ref[:, lo:lo + 1]
            out = []
            for (s, dp), ok, bias, lse, dl in ((tiles[0], ok_c, bias_c, lse_c, dl_c),
                                               (tiles[1], ok_pp, bias_p, lse_c, dl_c),
                                               (tiles[2], ok_np, bias_p, lse_n, dl_n)):
                p = jnp.exp(jnp.where(ok, s - slope * bias, NEG) - lse)
                out.append((p.astype(BF16), (p * (dp - dl)).astype(BF16)))
            pds.append(out)
        dq_all, dk_all, dv_all = [], [], []
        for hp in range(B_PAIRS):
            dq = jnp.zeros((B_W, LANES), F32)
            dk = jnp.zeros((B_W, LANES), F32)
            dv = jnp.zeros((B_W, LANES), F32)
            for e in range(2):
                (p_c, ds_c), (_, ds_p), (p_n, ds_n) = pds[2 * hp + e]
                qce, qne = mask(col(qc_ref, hp) * SCALE, e), mask(col(qn_ref, hp) * SCALE, e)
                doce, done = mask(col(doc_ref, hp), e), mask(col(don_ref, hp), e)
                dq = dq + _dot(ds_c, mask(col(kc_ref, hp) * SCALE, e)) + _dot(ds_p, mask(col(kp_ref, hp) * SCALE, e))
                dk = dk + _dot(ds_c, qce, TN_DIMS) + _dot(ds_n, qne, TN_DIMS)
                dv = dv + _dot(p_c, doce, TN_DIMS) + _dot(p_n, done, TN_DIMS)
            dq_all.append(dq)
            dk_all.append(dk)
            dv_all.append(dv)
        dq_ref[...] = jnp.concatenate(dq_all, axis=1).astype(BF16)
        dk_ref[...] = jnp.concatenate(dk_all, axis=1).astype(BF16)
        dv_ref[...] = jnp.concatenate(dv_all, axis=1).astype(BF16)

        @pl.when(jnp.logical_and(g == 2, n == B_NB - 1))
        def _():
            for cp in _exchange_copies(*exchange):
                cp.wait()

    cur = pl.BlockSpec((None, B_W, B_OUT), lambda g, n, sl: (g, n, 0))
    prev = pl.BlockSpec((None, B_W, B_OUT), lambda g, n, sl: (g, jnp.maximum(n - 1, 0), 0))
    nxt = pl.BlockSpec((None, B_W, B_OUT), lambda g, n, sl: (g, jnp.minimum(n + 1, B_NB - 1), 0))
    stat_cur = pl.BlockSpec((None, B_W, LANES), lambda g, n, sl: (g, n, 0))
    stat_nxt = pl.BlockSpec((None, B_W, LANES), lambda g, n, sl: (g, jnp.minimum(n + 1, B_NB - 1), 0))
    return pl.pallas_call(
        body, name=name,
        grid_spec=pltpu.PrefetchScalarGridSpec(
            num_scalar_prefetch=1, grid=(3, B_NB),
            in_specs=[cur, nxt, prev, cur, prev, cur, cur, nxt, stat_cur, stat_nxt, stat_cur, stat_nxt]
            + [ANY] * n_ex,
            out_specs=[cur, cur, cur] + [ANY] * n_ex,
            scratch_shapes=_exchange_scratch(n_ex)),
        out_shape=[jax.ShapeDtypeStruct((3, S, B_OUT), BF16)] * 3 + _exchange_shapes(scatter, True),
        compiler_params=_params("arbitrary", "arbitrary"),
    )(slopes, qp, qp, kp, kp, vp, vp, dop, dop, lsep, lsep, dlp, dlp, *scatter)


def _rows_block(shape, max_bytes=2 * 1024 * 1024):
    rows, cols = shape
    padded_cols = -(-cols // LANES) * LANES
    for tr in (1024, 512, 256, 128, 64, 32, 16):
        if rows % tr == 0 and tr * padded_cols * 4 <= max_bytes:
            return tr
    return rows


def _adam_update(w, m, v, g):
    m_new = ADAM_B1 * m + (1.0 - ADAM_B1) * g
    v_new = ADAM_B2 * v + (1.0 - ADAM_B2) * (g * g)
    m_hat = m_new / (1.0 - ADAM_B1 ** ADAM_STEP)
    v_hat = v_new / (1.0 - ADAM_B2 ** ADAM_STEP)
    delta = -ADAM_LR * (m_hat / (jnp.sqrt(v_hat) + ADAM_EPS) + ADAM_WD * w)
    return delta, m_new, v_new


def _adamw_sharded(w, m, v, parts, name):
    R, C = w.shape
    tr = _rows_block((R, C), max_bytes=1024 * 1024)

    def body(w_ref, m_ref, v_ref, p_ref, g_ref, d_ref, mo_ref, vo_ref):
        g = p_ref[0].astype(F32)
        for dev in range(1, N_DEV):
            g = g + p_ref[dev].astype(F32)
        g_ref[...] = g
        d_ref[...], mo_ref[...], vo_ref[...] = _adam_update(w_ref[...], m_ref[...], v_ref[...], g)

    blk = pl.BlockSpec((tr, C), lambda i: (i, 0))
    out = jax.ShapeDtypeStruct((R, C), F32)
    return pl.pallas_call(
        body, name=name, grid=(R // tr,),
        in_specs=[blk, blk, blk, pl.BlockSpec((N_DEV, tr, C), lambda i: (0, i, 0))],
        out_specs=[blk, blk, blk, blk], out_shape=[out, out, out, out],
        compiler_params=_params("parallel"),
    )(w, m, v, parts)


def _adamw_replicated(w, m, v, parts, name):
    def body(w_ref, m_ref, v_ref, p_ref, g_ref, d_ref, mo_ref, vo_ref):
        g = p_ref[0]
        for dev in range(1, N_DEV):
            g = g + p_ref[dev]
        g_ref[...] = g
        d_ref[...], mo_ref[...], vo_ref[...] = _adam_update(w_ref[...], m_ref[...], v_ref[...], g)

    out = jax.ShapeDtypeStruct(w.shape, F32)
    return pl.pallas_call(body, name=name, out_shape=[out, out, out, out], compiler_params=_params())(w, m, v, parts)


def _cols_from_slots(g):
    return g.transpose(1, 0, 2).reshape(g.shape[1], N_DEV * g.shape[2])


def _cols_to_slots(w):
    k, n = w.shape
    return w.reshape(k, N_DEV, n // N_DEV).transpose(1, 0, 2)


def _permute(t, dil):
    c = t.shape[1]
    return t.reshape(S // dil, dil, c).transpose(1, 0, 2).reshape(S, c)


def _unpermute(t, dil):
    c = t.shape[1]
    return t.reshape(dil, S // dil, c).transpose(1, 0, 2).reshape(S, c)


def _group_permute(t):
    return jnp.stack([_permute(t[:, g * B_OUT:(g + 1) * B_OUT], B_DILS[g]) for g in range(3)])


def _same_permute(t):
    return jnp.stack([_permute(t, d) for d in B_DILS])


def _group_unpermute(t):
    return jnp.stack([_unpermute(t[g], B_DILS[g]) for g in range(3)])


SMALL_ROWS = 144


def _pack_small(a_b_f, kv_g, mix_g, ffn_g, conv_b, fin_g):
    flat = jnp.concatenate([a_b_f.reshape(-1), kv_g.reshape(-1), mix_g.reshape(-1), ffn_g.reshape(-1),
                            conv_b.reshape(-1), fin_g.reshape(-1)])
    return jnp.pad(flat, (0, SMALL_ROWS * LANES - flat.shape[0])).reshape(SMALL_ROWS, LANES)


def _unpack_small(p):
    flat = p.reshape(-1)
    out, off = [], 0
    for shape in ((1, A_HEADS), (D,), (2, D), (2, D), (2, 2 * D_FF), (D,)):
        size = math.prod(shape)
        out.append(flat[off:off + size].reshape(shape))
        off += size
    return out


def _unpack_late(g):
    half = N_DEV // 2
    up = g[4].reshape(N_DEV, 2, D, -1)
    w_up_a = [up[:half, l].transpose(1, 0, 2).reshape(D, D_FF) for l in range(2)]
    w_up_g = [up[half:, l].transpose(1, 0, 2).reshape(D, D_FF) for l in range(2)]
    w_down = [g[5].reshape(N_DEV, 2, -1, D)[:, l].reshape(D_FF, D) for l in range(2)]
    conv_w = [g[6].reshape(N_DEV, 2, 3, -1)[:, l].transpose(1, 0, 2).reshape(3, 2 * D_FF) for l in range(2)]
    return (g[0].reshape(D, D), _cols_from_slots(g[1]), _cols_from_slots(g[2]), _cols_from_slots(g[3]),
            w_up_a, w_up_g, w_down, conv_w)


def _ffn_slots(dw_up, dw_down, dconv_w):
    return [_cols_to_slots(dw_up), dw_down.reshape(N_DEV, -1, D), _cols_to_slots(dconv_w)]


def _local_step(x0, target, w_in_pad, late_shards,
                a_b_f, kv_norm_g, mix_norm_g, ffn_norm_g, ffn_conv_b, final_norm_g):
    w_qkv, w_f = w_in_pad[:, :A_QKV], w_in_pad[:, A_QKV:]
    conv_b = ffn_conv_b.reshape(2, 1, 2 * D_FF)
    slopes = jnp.exp2(-8.0 * jnp.arange(1, 25, dtype=F32) / 24)

    def gain(g):
        return g.reshape(1, D)

    (h1,) = _rmsnorm_fwd(x0, [gain(mix_norm_g[0])], "norm_mix0")
    qkv = _matmul(h1, w_qkv, mode="nn", out_dtype=BF16, name="proj_qkv", tm=512, tn=A_QKV)
    z = _matmul(h1, w_f, mode="nn", out_dtype=F32, name="proj_gate", tm=S, tn=LANES)
    z_t = z[:, :A_HEADS].T
    b_f = a_b_f.reshape(A_HEADS, 1)
    c_t = _fox_prep_fwd(z_t, b_f, "fox_prep")
    c_t2 = c_t.reshape(N_PAIRS, 2, S)
    o_a, lse_a, *late = _fox_fwd(qkv, c_t2, "fox_fwd", late_shards)
    w_out, w_q, w_bo, w_kvf, w_up_a, w_up_g, w_down, conv_w = _unpack_late(late)
    x1 = _matmul(o_a, w_out, mode="nn", out_dtype=F32, name="a_out", tm=512, tn=D, res=x0)

    def ffn_fwd(xin, layer):
        (h,) = _rmsnorm_fwd(xin, [gain(ffn_norm_g[layer])], f"norm_ffn{layer}")
        u = (_matmul(h, w_up_a[layer], mode="nn", out_dtype=BF16, name=f"ffn_up_a{layer}", tm=512, tn=D_FF),
             _matmul(h, w_up_g[layer], mode="nn", out_dtype=BF16, name=f"ffn_up_g{layer}", tm=512, tn=D_FF))
        act = _convgate_fwd(*u, conv_w[layer], conv_b[layer], f"convgate{layer}")
        xout = _matmul(act, w_down[layer], mode="nn", out_dtype=F32, name=f"ffn_down{layer}", tm=512, tn=D, res=xin)
        return h, u, act, xout

    h2, u0, act0, x2 = ffn_fwd(x1, 0)
    hk, h3 = _rmsnorm_fwd(x2, [gain(kv_norm_g), gain(mix_norm_g[1])], "norm_kv_mix1")
    kv = _matmul(hk, w_kvf, mode="nn", out_dtype=BF16, name="proj_kv", tm=512, tn=B_KV)
    qb = _matmul(h3, w_q, mode="nn", out_dtype=BF16, name="proj_qb", tm=512, tn=B_Q)
    qp, kp, vp = _group_permute(qb), _group_permute(kv[:, :B_Q]), _group_permute(kv[:, B_Q:])
    og_p, lseg_p = _dil_fwd(qp, kp, vp, slopes, "dil_fwd")
    o_b, lse_b = _dil_merge(_group_unpermute(og_p), _group_unpermute(lseg_p), "dil_merge")
    x3 = _matmul(o_b, w_bo, mode="nn", out_dtype=F32, name="b_out", tm=512, tn=D, res=x2)
    h4, u1, act1, x4 = ffn_fwd(x3, 1)
    loss_blk, dx4, dx4b, dg_final = _final_loss(x4, target, gain(final_norm_g), "final_loss")

    def ffn_bwd(dx, dxb, xin, h, u, act, layer):
        dact = _matmul(dxb, w_down[layer], mode="nt", out_dtype=BF16, name=f"d_act{layer}", tm=512, tn=D_FF)
        dw_down = _matmul_tn(act, dxb, out_dtype=BF16, name=f"dw_down{layer}")
        du_a, du_g, dwa, dwg, dba, dbg = _convgate_bwd(*u, conv_w[layer], conv_b[layer], dact, f"convgate_bwd{layer}")
        dw_up = jnp.concatenate(
            [_matmul_tn(h, du_a, out_dtype=BF16, name=f"dw_up_a{layer}"),
             _matmul_tn(h, du_g, out_dtype=BF16, name=f"dw_up_g{layer}")], axis=1)
        dxin, dxinb, dgain = _matmul_norm_bwd([(du_a, w_up_a[layer]), (du_g, w_up_g[layer])], xin,
                                              gain(ffn_norm_g[layer]), dx, f"dh_ffn_norm_bwd{layer}")
        dconv_w = jnp.concatenate([dwa, dwg], axis=1)
        dconv_b = jnp.concatenate([dba, dbg], axis=1)
        return dxin, dxinb, dgain, dw_up, dw_down, dconv_w, dconv_b

    dx3, dx3b, dg_ffn1, dw_up1, dw_down1, dconv_w1, dconv_b1 = ffn_bwd(dx4, dx4b, x3, h4, u1, act1, 1)

    do_b = _matmul(dx3b, w_bo, mode="nt", out_dtype=BF16, name="d_ob", tm=1024, tn=B_OUT)
    dw_bo = _matmul_tn(o_b, dx3b, out_dtype=BF16, name="dw_bo")
    dl_b = _head_rowsum_compact(do_b, o_b, "delta_b")
    slots_up1, slots_down1, slots_conv1 = _ffn_slots(dw_up1, dw_down1, dconv_w1)
    dqp, dkp, dvp, land_down1, land_conv1 = _dil_bwd(
        qp, kp, vp, _same_permute(do_b), _same_permute(lse_b), _same_permute(dl_b), slopes, "dil_bwd",
        [slots_down1, slots_conv1])

    def natural(tp):
        return jnp.concatenate([_unpermute(tp[g], B_DILS[g]) for g in range(3)], axis=1)

    dqb = natural(dqp)
    dkv = jnp.concatenate([natural(dkp), natural(dvp)], axis=1)
    dw_q = _matmul_tn(h3, dqb, out_dtype=BF16, name="dw_q")
    dw_kv = _matmul_tn(hk, dkv, out_dtype=BF16, name="dw_kv")
    dx2, _, dg_mix1 = _matmul_norm_bwd([(dqb, w_q)], x2, gain(mix_norm_g[1]), dx3, "dh_mix1_norm_bwd")
    dx2, dx2b, dg_kv = _matmul_norm_bwd([(dkv, w_kvf)], x2, gain(kv_norm_g), dx2, "dh_kv_norm_bwd")

    dx1, dx1b, dg_ffn0, dw_up0, dw_down0, dconv_w0, dconv_b0 = ffn_bwd(dx2, dx2b, x1, h2, u0, act0, 0)

    do_a = _matmul(dx1b, w_out, mode="nt", out_dtype=BF16, name="d_oa", tm=512, tn=D)
    dw_out = _matmul_tn(o_a, dx1b, out_dtype=BF16, name="dw_out")
    dl_a = _head_rowsum(do_a, o_a, "delta_a")
    dq_a, dk_a, dv_a, dcol, drow, *land = _fox_bwd(
        qkv, do_a, lse_a, dl_a, c_t2, "fox_bwd",
        [dw_out.reshape(N_DEV, D // N_DEV, D), _cols_to_slots(dw_q), _cols_to_slots(dw_bo), _cols_to_slots(dw_kv)]
        + _ffn_slots(dw_up0, dw_down0, dconv_w0) + [slots_up1])
    land_out, land_q, land_bo, land_kv, land_up0, land_down0, land_conv0, land_up1 = land

    def head_sums(t):
        return t.reshape(S, N_PAIRS, 2, HEAD_DIM)[:, :, ::-1, 0].reshape(S, A_HEADS).T

    dz_t, db_f = _fox_prep_bwd(head_sums(drow), head_sums(dcol), z_t, b_f, "fox_prep_bwd")
    dz = jnp.pad(dz_t.T, ((0, 0), (0, LANES - A_HEADS))).astype(BF16)
    dproj = jnp.concatenate([dq_a.astype(BF16), dk_a, dv_a, dz], axis=1)
    dw_in = _matmul_tn(h1, dproj, out_dtype=BF16, name="dw_in")
    grad_x, _, dg_mix0, land_in = _matmul_norm_bwd(
        [(dproj, w_in_pad)], x0, gain(mix_norm_g[0]), dx1, "dh_mix0_norm_bwd",
        scatter=[_cols_to_slots(dw_in[:, :A_QKV + A_HEADS])])

    dg_mix = jnp.concatenate([dg_mix0, dg_mix1], axis=0)
    dg_ffn = jnp.concatenate([dg_ffn0, dg_ffn1], axis=0)
    dconv_b = jnp.concatenate([dconv_b0, dconv_b1], axis=0)
    small_part = _pack_small(db_f, dg_kv, dg_mix, dg_ffn, dconv_b, dg_final)
    _, (small_parts,) = _final_exchange([], [small_part], "gather_small_grads")
    landed = [land_in, land_out, land_q, land_bo, land_kv, land_up0, land_up1, land_down0, land_down1,
              land_conv0, land_conv1]
    return loss_blk, grad_x, landed, small_parts


def kernel(x, a_w_in, a_b_f, a_w_out, b_w_q, b_w_out, kv_norm_g, w_kv, mix_norm_g, ffn_norm_g, ffn_w_up, ffn_conv_w, ffn_conv_b, ffn_w_down, final_norm_g, loss_target, m_a_w_in, m_a_b_f, m_a_w_out, m_b_w_q, m_b_w_out, m_kv_norm_g, m_w_kv, m_mix_norm_g, m_ffn_norm_g, m_ffn_w_up, m_ffn_conv_w, m_ffn_conv_b, m_ffn_w_down, m_final_norm_g, v_a_w_in, v_a_b_f, v_a_w_out, v_b_w_q, v_b_w_out, v_kv_norm_g, v_w_kv, v_mix_norm_g, v_ffn_norm_g, v_ffn_w_up, v_ffn_conv_w, v_ffn_conv_b, v_ffn_w_down, v_final_norm_g):
    def shards(a_w_in, a_w_out, b_w_q, b_w_out, w_kv, ffn_w_up, ffn_w_down, ffn_conv_w):
        return [a_w_in[0], a_w_out[0], b_w_q[0], b_w_out[0], w_kv, ffn_w_up[0], ffn_w_up[1],
                ffn_w_down[0], ffn_w_down[1], ffn_conv_w[0], ffn_conv_w[1]]

    w_loc = shards(a_w_in, a_w_out, b_w_q, b_w_out, w_kv, ffn_w_up, ffn_w_down, ffn_conv_w)
    m_loc = shards(m_a_w_in, m_a_w_out, m_b_w_q, m_b_w_out, m_w_kv, m_ffn_w_up, m_ffn_w_down, m_ffn_conv_w)
    v_loc = shards(v_a_w_in, v_a_w_out, v_b_w_q, v_b_w_out, v_w_kv, v_ffn_w_up, v_ffn_w_down, v_ffn_conv_w)

    (g_in,) = _all_gather([a_w_in[0].astype(BF16)], "gather_a_w_in")
    w_in = _cols_from_slots(g_in)
    w_in_pad = jnp.pad(w_in, ((0, 0), (0, A_PROJ_PAD - w_in.shape[1])))
    late_shards = [a_w_out[0].astype(BF16), b_w_q[0].astype(BF16), b_w_out[0].astype(BF16), w_kv.astype(BF16),
                   ffn_w_up.reshape(2 * D, -1).astype(BF16), ffn_w_down.reshape(-1, D).astype(BF16),
                   ffn_conv_w.reshape(6, -1)]

    loss_blk, grad_x, landed, small_parts = _local_step(
        x[0], loss_target[0], w_in_pad, late_shards,
        a_b_f, kv_norm_g, mix_norm_g, ffn_norm_g, ffn_conv_b, final_norm_g)

    big = [_adamw_sharded(w_loc[k], m_loc[k], v_loc[k], landed[k], f"adamw{k}") for k in range(11)]

    small = _adamw_replicated(
        _pack_small(a_b_f, kv_norm_g, mix_norm_g, ffn_norm_g, ffn_conv_b, final_norm_g),
        _pack_small(m_a_b_f, m_kv_norm_g, m_mix_norm_g, m_ffn_norm_g, m_ffn_conv_b, m_final_norm_g),
        _pack_small(v_a_b_f, v_kv_norm_g, v_mix_norm_g, v_ffn_norm_g, v_ffn_conv_b, v_final_norm_g),
        small_parts, "adamw_small")

    loss = lax.psum(loss_blk[0, 0], ("x", "y", "c"))

    def assemble(kind):
        b = [r[kind] for r in big]
        s_abf, s_kv, s_mix, s_ffn, s_cb, s_fin = _unpack_small(small[kind])
        return [b[0][None], s_abf, b[1][None], b[2][None], b[3][None], s_kv, b[4], s_mix, s_ffn,
                jnp.stack([b[5], b[6]]), jnp.stack([b[9], b[10]]), s_cb, jnp.stack([b[7], b[8]]), s_fin]

    return (loss, grad_x[None], *assemble(0), *assemble(1), *assemble(2), *assemble(3))
```

```python
import math

import jax
import jax.numpy as jnp
from jax import lax
from jax.experimental import pallas as pl
from jax.experimental.pallas import tpu as pltpu

F32 = jnp.float32
BF16 = jnp.bfloat16

S = 4096
D = 1024
N_DEV = 8
A_HEADS = 16
HEAD_DIM = 64
A_QKV = 3072
A_PROJ_PAD = 3200
B_Q = 1536
B_OUT = 512
B_KV = 3072
B_W = 128
B_DILS = (1, 4, 16)
D_FF = 2816
RMS_EPS = 1e-6
SCALE = HEAD_DIM ** -0.5
NEG = -1e30

ADAM_LR = 0.001
ADAM_B1 = 0.9
ADAM_B2 = 0.999
ADAM_EPS = 1e-08
ADAM_WD = 0.01
ADAM_STEP = 10

LANES = 128
VMEM_LIMIT = 56 * 1024 * 1024
MESH = pl.DeviceIdType.MESH
ANY = pl.BlockSpec(memory_space=pl.ANY)

NT_DIMS = (((1,), (1,)), ((), ()))
TN_DIMS = (((0,), (0,)), ((), ()))
NN_DIMS = (((1,), (0,)), ((), ()))


def _params(*sem):
    return pltpu.CompilerParams(dimension_semantics=sem if sem else None, vmem_limit_bytes=VMEM_LIMIT)


def _dot(a, b, dims=NN_DIMS):
    return lax.dot_general(a, b, dims, preferred_element_type=F32)


def _split_dot(x, mat, pieces):
    out = None
    rem = x
    for _ in range(pieces):
        part = rem.astype(BF16)
        rem = rem - part.astype(F32)
        d = _dot(part, mat)
        out = d if out is None else out + d
    return out


def _pick(n, prefs):
    for p in prefs:
        if n % p == 0:
            return p
    return n


def _gather_phases(ins, outs, sems):
    n = len(ins)
    if n == 0:
        return (lambda: None,) * 3
    send_sems, recv_sems, local_sems = sems
    x, y, c = lax.axis_index("x"), lax.axis_index("y"), lax.axis_index("c")
    me, sibling = (x, y, c), (x, y, 1 - c)
    chips = [(1 - x, y), (x, 1 - y), (1 - x, 1 - y)]

    def slot(a, px, py, pc):
        return outs[a].at[4 * px + 2 * py + pc]

    def copy(a, k, block, to, src=None):
        return pltpu.make_async_remote_copy(
            src_ref=slot(a, *block) if src is None else src, dst_ref=slot(a, *block),
            send_sem=send_sems.at[a, k], recv_sem=recv_sems.at[a, k],
            device_id=to, device_id_type=MESH)

    mine = [pltpu.make_async_copy(ins[a], slot(a, *me), local_sems.at[a]) for a in range(n)]
    first = []
    for a in range(n):
        first.append(copy(a, 0, me, sibling, src=ins[a]))
        first += [copy(a, 1 + j, me, (*chip, c), src=ins[a]) for j, chip in enumerate(chips)]
    passed = [copy(a, 4 + j, (*chip, c), sibling) for j, chip in enumerate(chips) for a in range(n)]

    def start():
        for cp in mine + first:
            cp.start()

    def forward():
        k = 0
        for j, chip in enumerate(chips):
            for a in range(n):
                copy(a, 1 + j, (*chip, c), me).wait_recv()
                passed[k].start()
                k += 1

    def finish():
        for a in range(n):
            copy(a, 0, sibling, me).wait_recv()
            for j, chip in enumerate(chips):
                copy(a, 4 + j, (*chip, 1 - c), me).wait_recv()
        for cp in first + passed:
            cp.wait_send()
        for cp in mine:
            cp.wait()

    return start, forward, finish


def _all_gather(arrays, name):
    n = len(arrays)

    def body(*refs):
        for phase in _gather_phases(refs[:n], refs[n:2 * n], refs[2 * n:]):
            phase()

    return pl.pallas_call(
        body, name=name,
        out_shape=[jax.ShapeDtypeStruct((N_DEV,) + a.shape, a.dtype) for a in arrays],
        in_specs=[ANY] * n, out_specs=[ANY] * n,
        scratch_shapes=[pltpu.SemaphoreType.DMA((n, 7)), pltpu.SemaphoreType.DMA((n, 7)),
                        pltpu.SemaphoreType.DMA((n,))],
    )(*arrays)


PEER_FLIPS = [(dx, dy, dc) for dx in (0, 1) for dy in (0, 1) for dc in (0, 1) if (dx, dy, dc) != (0, 0, 0)]


def _exchange_copies(ins, outs, sems, scatter):
    if not ins:
        return []
    send_sems, recv_sems, local_sems = sems
    x, y, c = lax.axis_index("x"), lax.axis_index("y"), lax.axis_index("c")
    me = 4 * x + 2 * y + c
    copies = []
    for a in range(len(ins)):
        copies.append(pltpu.make_async_copy(ins[a].at[me] if scatter else ins[a], outs[a].at[me], local_sems.at[a]))
        for k, (dx, dy, dc) in enumerate(PEER_FLIPS):
            px, py, pc = (1 - x if dx else x), (1 - y if dy else y), (1 - c if dc else c)
            copies.append(pltpu.make_async_remote_copy(
                src_ref=ins[a].at[4 * px + 2 * py + pc] if scatter else ins[a], dst_ref=outs[a].at[me],
                send_sem=send_sems.at[a, k], recv_sem=recv_sems.at[a, k],
                device_id=(px, py, pc), device_id_type=MESH))
    return copies


def _exchange_scratch(n):
    if n == 0:
        return []
    return [pltpu.SemaphoreType.DMA((n, 7)), pltpu.SemaphoreType.DMA((n, 7)), pltpu.SemaphoreType.DMA((n,))]


def _exchange_shapes(arrays, scatter):
    return [jax.ShapeDtypeStruct((N_DEV,) + (a.shape[1:] if scatter else a.shape), a.dtype) for a in arrays]


def _final_exchange(scatter, gather, name):
    ns, ng = len(scatter), len(gather)

    def body(*refs):
        ins, outs, sems = refs[:ns + ng], refs[ns + ng:2 * (ns + ng)], refs[2 * (ns + ng):]
        n_sems = len(_exchange_scratch(ns))
        copies = (_exchange_copies(ins[:ns], outs[:ns], sems[:n_sems], True)
                  + _exchange_copies(ins[ns:], outs[ns:], sems[n_sems:], False))
        for cp in copies:
            cp.start()
        for cp in copies:
            cp.wait()

    res = pl.pallas_call(
        body, name=name, out_shape=_exchange_shapes(scatter, True) + _exchange_shapes(gather, False),
        in_specs=[ANY] * (ns + ng), out_specs=[ANY] * (ns + ng),
        scratch_shapes=_exchange_scratch(ns) + _exchange_scratch(ng),
    )(*scatter, *gather)
    return res[:ns], res[ns:]


MM_ROWS = 512
MM_COLS = 1024


def _matmul(a, b, *, mode, out_dtype, name, tm, tn, res=None):
    if mode == "nn":
        (M, K), (K2, N) = a.shape, b.shape
    else:
        (M, K), (N, K2) = a.shape, b.shape
    assert K == K2, (a.shape, b.shape, mode)
    tm, tn = min(tm, M), min(tn, N)
    sm = min(tm, MM_ROWS)
    sn = tn if tn <= MM_COLS else _pick(tn, (512, 256, 128))
    assert M % tm == 0 and N % tn == 0 and tm % sm == 0, (M, N, K, tm, tn)
    dims = NN_DIMS if mode == "nn" else NT_DIMS
    a_spec = pl.BlockSpec((tm, K), lambda i, j: (i, 0))
    if mode == "nt":
        b_spec = pl.BlockSpec((tn, K), lambda i, j: (j, 0))
    else:
        b_spec = pl.BlockSpec((K, tn), lambda i, j: (0, j))
    o_spec = pl.BlockSpec((tm, tn), lambda i, j: (i, j))
    has_res = res is not None

    def body(*refs):
        a_ref, b_ref = refs[0], refs[1]
        r_ref = refs[2] if has_res else None
        o_ref = refs[2 + has_res]

        def chunk(r, carry):
            rows = pl.ds(pl.multiple_of(r * sm, sm), sm)
            av = a_ref[rows, :]
            for c0 in range(0, tn, sn):
                bv = b_ref[c0:c0 + sn, :] if mode == "nt" else b_ref[:, c0:c0 + sn]
                total = _dot(av, bv, dims)
                if has_res:
                    total = total + r_ref[rows, c0:c0 + sn]
                o_ref[rows, c0:c0 + sn] = total.astype(out_dtype)
            return carry

        lax.fori_loop(0, tm // sm, chunk, 0)

    return pl.pallas_call(
        body, name=name, grid=(M // tm, N // tn),
        out_shape=jax.ShapeDtypeStruct((M, N), out_dtype),
        in_specs=[a_spec, b_spec] + ([o_spec] if has_res else []),
        out_specs=o_spec,
        compiler_params=_params("parallel", "parallel"),
    )(*((a, b, res) if has_res else (a, b)))


def _matmul_tn(a, b, *, out_dtype, name, tk=512, sm=256):
    (K, M), (K2, N) = a.shape, b.shape
    assert K == K2 and K % tk == 0 and M % sm == 0, (a.shape, b.shape)
    nk = K // tk

    def body(a_ref, b_ref, o_ref, acc_ref):
        k = pl.program_id(0)

        @pl.when(k == 0)
        def _():
            acc_ref[...] = jnp.zeros_like(acc_ref)

        def chunk(mi, carry):
            cols = pl.ds(pl.multiple_of(mi * sm, sm), sm)
            acc_ref[cols, :] += _dot(a_ref[:, cols].T, b_ref[...])
            return carry

        lax.fori_loop(0, M // sm, chunk, 0)

        @pl.when(k == nk - 1)
        def _():
            def emit(mi, carry):
                rows = pl.ds(pl.multiple_of(mi * sm, sm), sm)
                o_ref[rows, :] = acc_ref[rows, :].astype(out_dtype)
                return carry
            lax.fori_loop(0, M // sm, emit, 0)

    return pl.pallas_call(
        body, name=name, grid=(nk,),
        out_shape=jax.ShapeDtypeStruct((M, N), out_dtype),
        in_specs=[pl.BlockSpec((tk, M), lambda k: (k, 0)), pl.BlockSpec((tk, N), lambda k: (k, 0))],
        out_specs=pl.BlockSpec((M, N), lambda k: (0, 0)),
        scratch_shapes=[pltpu.VMEM((M, N), F32)],
        compiler_params=_params("arbitrary"),
    )(a, b)


def _rmsnorm_fwd(x, gains, name, tr=256):
    n = len(gains)

    def body(*refs):
        x_ref = refs[0]
        xv = x_ref[...]
        r = lax.rsqrt(jnp.mean(xv * xv, axis=-1, keepdims=True) + RMS_EPS)
        y = xv * r
        for a in range(n):
            refs[1 + n + a][...] = (y * refs[1 + a][...]).astype(BF16)

    row = pl.BlockSpec((tr, D), lambda i: (i, 0))
    gain = pl.BlockSpec((1, D), lambda i: (0, 0))
    return pl.pallas_call(
        body, name=name, grid=(S // tr,),
        out_shape=[jax.ShapeDtypeStruct((S, D), BF16)] * n,
        in_specs=[row] + [gain] * n, out_specs=[row] * n,
        compiler_params=_params("parallel"),
    )(x, *gains)


def _matmul_norm_bwd(pairs, x, g, dres, name, scatter=(), tm=512):
    M = x.shape[0]
    n_p, n_ex = len(pairs), len(scatter)
    n_in = 2 * n_p + 3
    steps = M // tm

    def body(*refs):
        x_ref, g_ref, dres_ref = refs[2 * n_p:n_in]
        dx_ref, dxb_ref, dg_ref = refs[n_in + n_ex:n_in + n_ex + 3]
        exchange = (refs[n_in:n_in + n_ex], refs[n_in + n_ex + 3:n_in + 2 * n_ex + 3], refs[n_in + 2 * n_ex + 3:], True)

        @pl.when(pl.program_id(0) == 0)
        def _():
            for cp in _exchange_copies(*exchange):
                cp.start()

        dyv = _dot(refs[0][...], refs[1][...], NT_DIMS)
        for p in range(1, n_p):
            dyv = dyv + _dot(refs[2 * p][...], refs[2 * p + 1][...], NT_DIMS)
        xv = x_ref[...]
        r = lax.rsqrt(jnp.mean(xv * xv, axis=-1, keepdims=True) + RMS_EPS)
        xhat = xv * r
        dxhat = dyv * g_ref[...]
        mean_term = jnp.mean(dxhat * xhat, axis=-1, keepdims=True)
        dx = r * (dxhat - xhat * mean_term) + dres_ref[...]
        dx_ref[...] = dx
        dxb_ref[...] = dx.astype(BF16)
        part = jnp.sum(dyv * xhat, axis=0, keepdims=True)

        @pl.when(pl.program_id(0) == 0)
        def _():
            dg_ref[...] = part

        @pl.when(pl.program_id(0) > 0)
        def _():
            dg_ref[...] += part

        @pl.when(pl.program_id(0) == steps - 1)
        def _():
            for cp in _exchange_copies(*exchange):
                cp.wait()

    row = pl.BlockSpec((tm, D), lambda i: (i, 0))
    gain = pl.BlockSpec((1, D), lambda i: (0, 0))
    pair_specs, operands = [], []
    for a, b in pairs:
        assert a.shape == (M, b.shape[1]) and b.shape[0] == D, (a.shape, b.shape)
        pair_specs += [pl.BlockSpec((tm, a.shape[1]), lambda i: (i, 0)), pl.BlockSpec(b.shape, lambda i: (0, 0))]
        operands += [a, b]
    return pl.pallas_call(
        body, name=name, grid=(steps,),
        out_shape=[jax.ShapeDtypeStruct((M, D), F32), jax.ShapeDtypeStruct((M, D), BF16),
                   jax.ShapeDtypeStruct((1, D), F32)] + _exchange_shapes(scatter, True),
        in_specs=pair_specs + [row, gain, row] + [ANY] * n_ex,
        out_specs=[row, row, gain] + [ANY] * n_ex,
        scratch_shapes=_exchange_scratch(n_ex),
        compiler_params=_params("arbitrary"),
    )(*operands, x, g, dres, *scatter)


def _final_loss(x, target, g, name, tr=256):
    def body(x_ref, t_ref, g_ref, loss_ref, dx_ref, dxb_ref, dg_ref):
        xv = x_ref[...]
        gv = g_ref[...]
        r = lax.rsqrt(jnp.mean(xv * xv, axis=-1, keepdims=True) + RMS_EPS)
        xhat = xv * r
        err = xhat * gv - t_ref[...]
        row_loss = jnp.mean(err * err, axis=-1, keepdims=True)
        lpart = 0.5 * jnp.sum(row_loss, axis=0, keepdims=True)
        dyv = err / D
        dxhat = dyv * gv
        mean_term = jnp.mean(dxhat * xhat, axis=-1, keepdims=True)
        dx = r * (dxhat - xhat * mean_term)
        dx_ref[...] = dx
        dxb_ref[...] = dx.astype(BF16)
        gpart = jnp.sum(dyv * xhat, axis=0, keepdims=True)

        @pl.when(pl.program_id(0) == 0)
        def _():
            dg_ref[...] = gpart
            loss_ref[...] = jnp.broadcast_to(lpart, loss_ref.shape)

        @pl.when(pl.program_id(0) > 0)
        def _():
            dg_ref[...] += gpart
            loss_ref[...] += jnp.broadcast_to(lpart, loss_ref.shape)

    row = pl.BlockSpec((tr, D), lambda i: (i, 0))
    gain = pl.BlockSpec((1, D), lambda i: (0, 0))
    lspec = pl.BlockSpec((8, LANES), lambda i: (0, 0))
    return pl.pallas_call(
        body, name=name, grid=(S // tr,),
        out_shape=[jax.ShapeDtypeStruct((8, LANES), F32), jax.ShapeDtypeStruct((S, D), F32),
                   jax.ShapeDtypeStruct((S, D), BF16), jax.ShapeDtypeStruct((1, D), F32)],
        in_specs=[row, row, gain], out_specs=[lspec, row, row, gain],
        compiler_params=_params("arbitrary"),
    )(x, target, g)


CONV_TR = 128
CONV_TC = D_FF
CONV_NJ = D_FF // CONV_TC
HALO = 16


def _causal_taps(cur_ref, prev_ref, first):
    xv = cur_ref[...].astype(F32)
    pv = prev_ref[...].astype(F32)
    p1 = jnp.where(first, 0.0, pv[HALO - 1:HALO, :])
    p2 = jnp.where(first, 0.0, pv[HALO - 2:HALO - 1, :])
    r1, r2 = pltpu.roll(xv, 1, 0), pltpu.roll(xv, 2, 0)
    row = lax.broadcasted_iota(jnp.int32, (8, xv.shape[1]), 0)
    xm1 = jnp.concatenate([jnp.where(row == 0, p1, r1[0:8]), r1[8:]], axis=0)
    xm2 = jnp.concatenate([jnp.where(row == 0, p2, jnp.where(row == 1, p1, r2[0:8])), r2[8:]], axis=0)
    return xv, xm1, xm2


def _conv_specs():
    def prev_row(i):
        return jnp.maximum(i * (CONV_TR // HALO) - 1, 0)
    ua = pl.BlockSpec((CONV_TR, CONV_TC), lambda i, j: (i, j))
    ug = ua
    pa = pl.BlockSpec((HALO, CONV_TC), lambda i, j: (prev_row(i), j))
    pg = pa
    wa = pl.BlockSpec((3, CONV_TC), lambda i, j: (0, j))
    wg = pl.BlockSpec((3, CONV_TC), lambda i, j: (0, j + CONV_NJ))
    ba = pl.BlockSpec((1, CONV_TC), lambda i, j: (0, j))
    bg = pl.BlockSpec((1, CONV_TC), lambda i, j: (0, j + CONV_NJ))
    return [ua, pa, ug, pg, wa, wg, ba, bg]


def _convgate_fwd(u_a, u_g, w, b, name):
    def body(ua, pa, ug, pg, wa, wg, ba, bg, o_ref):
        first = pl.program_id(0) == 0
        x0, x1, x2 = _causal_taps(ua, pa, first)
        ac = wa[0:1, :] * x2 + wa[1:2, :] * x1 + wa[2:3, :] * x0 + ba[...]
        x0, x1, x2 = _causal_taps(ug, pg, first)
        gc = wg[0:1, :] * x2 + wg[1:2, :] * x1 + wg[2:3, :] * x0 + bg[...]
        sg = 0.5 * jnp.tanh(0.5 * gc) + 0.5
        o_ref[...] = (gc * sg * ac).astype(BF16)

    return pl.pallas_call(
        body, name=name, grid=(S // CONV_TR, CONV_NJ),
        out_shape=jax.ShapeDtypeStruct((S, D_FF), BF16),
        in_specs=_conv_specs(),
        out_specs=pl.BlockSpec((CONV_TR, CONV_TC), lambda i, j: (i, j)),
        compiler_params=_params("parallel", "parallel"),
    )(u_a, u_a, u_g, u_g, w, w, b, b)


def _anticausal_conv(d, nxt_ref, w_ref, last):
    n1 = jnp.where(last, 0.0, nxt_ref[0:1, :])
    n2 = jnp.where(last, 0.0, nxt_ref[1:2, :])
    r1, r2 = pltpu.roll(d, CONV_TR - 1, 0), pltpu.roll(d, CONV_TR - 2, 0)
    row = lax.broadcasted_iota(jnp.int32, (8, d.shape[1]), 0)
    cut = CONV_TR - 8
    dp1 = jnp.concatenate([r1[:cut], jnp.where(row == 7, n1, r1[cut:])], axis=0)
    dp2 = jnp.concatenate([r2[:cut], jnp.where(row == 7, n2, jnp.where(row == 6, n1, r2[cut:]))], axis=0)
    return w_ref[2:3, :] * d + w_ref[1:2, :] * dp1 + w_ref[0:1, :] * dp2


def _convgate_bwd(u_a, u_g, w, b, dact, name):
    n_i = S // CONV_TR

    def body(ua, pa, ug, pg, wa, wg, ba, bg, d_ref, dua_ref, dug_ref, dwa_ref, dwg_ref, dba_ref, dbg_ref,
             nxt_a, nxt_g):
        i = pl.program_id(1)
        last = i == 0
        first = i == n_i - 1
        a0, a1, a2 = _causal_taps(ua, pa, first)
        ac = wa[0:1, :] * a2 + wa[1:2, :] * a1 + wa[2:3, :] * a0 + ba[...]
        g0, g1, g2 = _causal_taps(ug, pg, first)
        gc = wg[0:1, :] * g2 + wg[1:2, :] * g1 + wg[2:3, :] * g0 + bg[...]
        sg = 0.5 * jnp.tanh(0.5 * gc) + 0.5
        dact_v = d_ref[...].astype(F32)
        da = dact_v * (gc * sg)
        dg = dact_v * ac * (sg * (1.0 + gc * (1.0 - sg)))
        dua_ref[...] = _anticausal_conv(da, nxt_a, wa, last).astype(BF16)
        dug_ref[...] = _anticausal_conv(dg, nxt_g, wg, last).astype(BF16)
        nxt_a[...] = da[0:8]
        nxt_g[...] = dg[0:8]

        def col(v):
            return jnp.sum(v, axis=0, keepdims=True)

        parts = [col(da * a2), col(da * a1), col(da * a0), col(dg * g2), col(dg * g1), col(dg * g0),
                 col(da), col(dg)]

        @pl.when(last)
        def _():
            for k in range(3):
                dwa_ref[k:k + 1, :] = parts[k]
                dwg_ref[k:k + 1, :] = parts[3 + k]
            dba_ref[...] = parts[6]
            dbg_ref[...] = parts[7]

        @pl.when(i > 0)
        def _():
            for k in range(3):
                dwa_ref[k:k + 1, :] += parts[k]
                dwg_ref[k:k + 1, :] += parts[3 + k]
            dba_ref[...] += parts[6]
            dbg_ref[...] += parts[7]

    def swap(spec):
        return pl.BlockSpec(spec.block_shape, lambda j, i, f=spec.index_map: f(n_i - 1 - i, j))

    blk = pl.BlockSpec((CONV_TR, CONV_TC), lambda j, i: (n_i - 1 - i, j))
    w3 = pl.BlockSpec((3, CONV_TC), lambda j, i: (0, j))
    b1 = pl.BlockSpec((1, CONV_TC), lambda j, i: (0, j))
    return pl.pallas_call(
        body, name=name, grid=(CONV_NJ, n_i),
        out_shape=[jax.ShapeDtypeStruct((S, D_FF), BF16), jax.ShapeDtypeStruct((S, D_FF), BF16),
                   jax.ShapeDtypeStruct((3, D_FF), F32), jax.ShapeDtypeStruct((3, D_FF), F32),
                   jax.ShapeDtypeStruct((1, D_FF), F32), jax.ShapeDtypeStruct((1, D_FF), F32)],
        in_specs=[swap(s) for s in _conv_specs()] + [blk],
        out_specs=[blk, blk, w3, w3, b1, b1],
        scratch_shapes=[pltpu.VMEM((8, CONV_TC), F32), pltpu.VMEM((8, CONV_TC), F32)],
        compiler_params=_params("arbitrary", "arbitrary"),
    )(u_a, u_a, u_g, u_g, w, w, b, b, dact)


FOX_T = 512
FOX_TQ, FOX_TK = 512, 512
FOX_FORWARD_AT = 4
N_PAIRS = A_HEADS // 2


def _lane_masks():
    lane = lax.broadcasted_iota(jnp.int32, (1, LANES), 1)
    return lane, (lane < HEAD_DIM, lane >= HEAD_DIM)


def _fox_prep_fwd(z_t, b, name):
    def body(z_ref, b_ref, c_ref):
        r = lax.broadcasted_iota(jnp.int32, (LANES, LANES), 0)
        cc = lax.broadcasted_iota(jnp.int32, (LANES, LANES), 1)
        upper = (r <= cc).astype(BF16)
        carry = jnp.zeros((A_HEADS, 1), F32)
        for blk in range(S // LANES):
            sl = slice(blk * LANES, (blk + 1) * LANES)
            z = z_ref[:, sl] + b_ref[...]
            lf = jnp.minimum(z, 0.0) - jnp.log(1.0 + jnp.exp(-jnp.abs(z)))
            cs = _split_dot(lf, upper, 3) + carry
            c_ref[:, sl] = cs
            carry = cs[:, LANES - 1:LANES]

    return pl.pallas_call(
        body, name=name, out_shape=jax.ShapeDtypeStruct((A_HEADS, S), F32),
        compiler_params=_params(),
    )(z_t, b)


def _fox_prep_bwd(drow_t, dcol_t, z_t, b, name):
    def body(dr_ref, dc_ref, z_ref, b_ref, dz_ref, db_ref):
        r = lax.broadcasted_iota(jnp.int32, (LANES, LANES), 0)
        cc = lax.broadcasted_iota(jnp.int32, (LANES, LANES), 1)
        lower = (r >= cc).astype(BF16)
        carry = jnp.zeros((A_HEADS, 1), F32)
        db = jnp.zeros((A_HEADS, 1), F32)
        for blk in reversed(range(S // LANES)):
            sl = slice(blk * LANES, (blk + 1) * LANES)
            rc = _split_dot(dr_ref[:, sl] - dc_ref[:, sl], lower, 3) + carry
            carry = rc[:, 0:1]
            z = z_ref[:, sl] + b_ref[...]
            dz = rc / (1.0 + jnp.exp(z))
            dz_ref[:, sl] = dz
            db = db + jnp.sum(dz, axis=1, keepdims=True)
        db_ref[...] = db

    return pl.pallas_call(
        body, name=name,
        out_shape=[jax.ShapeDtypeStruct((A_HEADS, S), F32), jax.ShapeDtypeStruct((A_HEADS, 1), F32)],
        compiler_params=_params(),
    )(drow_t, dcol_t, z_t, b)


def _fox_fwd(qkv, c_t2, name, gather):
    tq, tk = FOX_TQ, FOX_TK

    n = len(gather)

    def body(*refs):
        q_ref, k_ref, v_ref, ct_ref = refs[:4]
        o_ref, lse_ref = refs[4 + n:6 + n]
        s_scr, p_scr, acc_scr = refs[-5:-3], refs[-3:-1], refs[-1]
        qi = pl.program_id(1)

        gather_start, gather_forward, gather_finish = _gather_phases(
            refs[4:4 + n], refs[6 + n:6 + 2 * n], refs[6 + 2 * n:len(refs) - 5])

        @pl.when(jnp.logical_and(pl.program_id(0) == 0, qi == 0))
        def _():
            gather_start()

        @pl.when(jnp.logical_and(pl.program_id(0) == FOX_FORWARD_AT, qi == 0))
        def _():
            gather_forward()

        n_full = jnp.right_shift(qi, (tk // tq).bit_length() - 1)
        lane, masks = _lane_masks()
        q = q_ref[...] * SCALE
        qs = [jnp.where(masks[e], q, jnp.zeros_like(q)) for e in range(2)]

        def scores_into(j, slot):
            start = pl.multiple_of(j * tk, tk)
            kb = k_ref[pl.ds(start, tk), :]
            for e in range(2):
                s_scr[slot][e] = _dot(qs[e], kb, NT_DIMS) - ct_ref[e:e + 1, pl.ds(start, tk)]

        def softmax_of(slot, m, masked):
            m_new, alpha = [], []
            for e in range(2):
                s = s_scr[slot][e]
                if masked:
                    rows = lax.broadcasted_iota(jnp.int32, (tq, tk), 0) + (qi * tq - n_full * tk)
                    cols = lax.broadcasted_iota(jnp.int32, (tq, tk), 1)
                    s = jnp.where(cols <= rows, s, NEG)
                m_new.append(jnp.maximum(m[e], jnp.max(s, axis=1, keepdims=True)))
                p_scr[slot][e] = jnp.exp(s - m_new[e]).astype(BF16)
                alpha.append(jnp.exp(m[e] - m_new[e]))
            return tuple(m_new), tuple(alpha)

        def values_of(j, slot, alpha):
            start = pl.multiple_of(j * tk, tk)
            vb = v_ref[pl.ds(start, tk), :]
            for e in range(2):
                acc_scr[e] = (alpha[e] * acc_scr[e]
                              + _dot(p_scr[slot][e], jnp.where(masks[e], vb, jnp.ones_like(vb))))

        def stage(j, cur, nxt, carry):
            m, a_prev = carry
            scores_into(j + 1, nxt)
            values_of(jnp.maximum(j - 1, 0), nxt, a_prev)
            return softmax_of(cur, m, False)

        def finish(cur, nxt, carry):
            m, a_prev = carry
            values_of(jnp.maximum(n_full - 1, 0), nxt, a_prev)
            (m0, m1), alpha = softmax_of(cur, m, True)
            values_of(n_full, cur, alpha)
            l0 = acc_scr[0][:, HEAD_DIM:HEAD_DIM + 1]
            l1 = acc_scr[1][:, 0:1]
            o_ref[...] = jnp.where(masks[0], acc_scr[0] / l0, acc_scr[1] / l1).astype(BF16)
            lse_ref[...] = jnp.where(masks[0], m0 + jnp.log(l0), m1 + jnp.log(l1))

        scores_into(0, 0)
        for e in range(2):
            p_scr[1][e] = jnp.zeros((tq, tk), BF16)
            acc_scr[e] = jnp.zeros((tq, LANES), F32)
        two = lambda x: (x, x)
        init = (two(jnp.full((tq, 1), NEG, F32)), two(jnp.ones((tq, 1), F32)))

        def two_stages(jj, carry):
            return stage(2 * jj + 1, 1, 0, stage(2 * jj, 0, 1, carry))

        carry = lax.fori_loop(0, jnp.right_shift(n_full, 1), two_stages, init)
        odd = jnp.bitwise_and(n_full, 1) == 1

        @pl.when(odd)
        def _():
            finish(1, 0, stage(n_full - 1, 0, 1, carry))

        @pl.when(jnp.logical_not(odd))
        def _():
            finish(0, 1, carry)

        @pl.when(jnp.logical_and(pl.program_id(0) == N_PAIRS - 1, qi == S // tq - 1))
        def _():
            gather_finish()

    qspec = pl.BlockSpec((tq, LANES), lambda h, i: (i, h))
    return pl.pallas_call(
        body, name=name, grid=(N_PAIRS, S // tq),
        out_shape=[jax.ShapeDtypeStruct((S, D), BF16), jax.ShapeDtypeStruct((S, D), F32)]
        + _exchange_shapes(gather, False),
        in_specs=[qspec,
                  pl.BlockSpec((S, LANES), lambda h, i: (0, N_PAIRS + h)),
                  pl.BlockSpec((S, LANES), lambda h, i: (0, 2 * N_PAIRS + h)),
                  pl.BlockSpec((None, 2, S), lambda h, i: (h, 0, 0))] + [ANY] * n,
        out_specs=[qspec, qspec] + [ANY] * n,
        scratch_shapes=_exchange_scratch(n) + [
            pltpu.VMEM((2, tq, tk), F32), pltpu.VMEM((2, tq, tk), F32),
            pltpu.VMEM((2, tq, tk), BF16), pltpu.VMEM((2, tq, tk), BF16),
            pltpu.VMEM((2, tq, LANES), F32)],
        compiler_params=_params("arbitrary", "arbitrary"),
    )(qkv, qkv, qkv, c_t2, *gather)


def _head_rowsum(a, b, name, tr=256):
    C = a.shape[1]

    def body(a_ref, b_ref, o_ref):
        r = lax.broadcasted_iota(jnp.int32, (LANES, LANES), 0) < HEAD_DIM
        cc = lax.broadcasted_iota(jnp.int32, (LANES, LANES), 1) < HEAD_DIM
        same_head = (r == cc).astype(BF16)
        for blk in range(C // LANES):
            sl = slice(blk * LANES, (blk + 1) * LANES)
            prod = a_ref[:, sl].astype(F32) * b_ref[:, sl].astype(F32)
            o_ref[:, sl] = _split_dot(prod, same_head, 2)

    row = pl.BlockSpec((tr, C), lambda i: (i, 0))
    return pl.pallas_call(
        body, name=name, grid=(S // tr,), out_shape=jax.ShapeDtypeStruct((S, C), F32),
        in_specs=[row, row], out_specs=row, compiler_params=_params("parallel"),
    )(a, b)


def _fox_bwd(qkv, do, lse, delta, c_t2, name, scatter):
    t = FOX_T
    nq = S // t

    n = len(scatter)

    def body(*refs):
        q_ref, k_ref, v_ref, do_ref, lse_ref, dl_ref, ct_ref = refs[:7]
        dq_ref, dk_ref, dv_ref, dcol_ref, drow_ref = refs[7 + n:12 + n]
        exchange = (refs[7:7 + n], refs[12 + n:12 + 2 * n], refs[12 + 2 * n:len(refs) - 5], True)
        sd_scr, pd_scr, acc_scr = refs[-5:-3], refs[-3:-1], refs[-1]
        kj = pl.program_id(1)

        @pl.when(jnp.logical_and(pl.program_id(0) == 0, kj == 0))
        def _():
            for cp in _exchange_copies(*exchange):
                cp.start()

        @pl.when(kj == 0)
        def _():
            dq_ref[...] = jnp.zeros_like(dq_ref)
            drow_ref[...] = jnp.zeros_like(drow_ref)

        lane, masks = _lane_masks()
        k = k_ref[...]
        v = v_ref[...]
        k_aug = [jnp.where(masks[e], k * SCALE, jnp.ones_like(k)) for e in range(2)]
        cs = [ct_ref[e:e + 1, :] for e in range(2)]

        def rows_of(i):
            r0 = pl.multiple_of(i * t, t)
            return pl.ds(r0, t), q_ref[pl.ds(r0, t), :] * SCALE, do_ref[pl.ds(r0, t), :]

        def scores_into(i, slot):
            _, qb, dob = rows_of(i)
            for e in range(2):
                qe = jnp.where(masks[e], qb, jnp.zeros_like(qb))
                doe = jnp.where(masks[e], dob, jnp.zeros_like(dob))
                sd_scr[slot][2 * e] = _dot(qe, k, NT_DIMS) - cs[e]
                sd_scr[slot][2 * e + 1] = _dot(doe, v, NT_DIMS)

        def pointwise(i, slot, masked):
            rows, _, _ = rows_of(i)
            for e in range(2):
                lo = e * HEAD_DIM
                s = sd_scr[slot][2 * e]
                if masked:
                    r = lax.broadcasted_iota(jnp.int32, (t, t), 0)
                    c = lax.broadcasted_iota(jnp.int32, (t, t), 1)
                    s = jnp.where(c <= r, s, NEG)
                p = jnp.exp(s - lse_ref[rows, lo:lo + 1])
                pd_scr[slot][2 * e] = p.astype(BF16)
                pd_scr[slot][2 * e + 1] = (p * (sd_scr[slot][2 * e + 1] - dl_ref[rows, lo:lo + 1])).astype(BF16)

        def accumulate(i, slot):
            rows, qb, dob = rows_of(i)
            dq_parts = []
            for e in range(2):
                p, ds = pd_scr[slot][2 * e], pd_scr[slot][2 * e + 1]
                q_aug = jnp.where(masks[e], qb, jnp.ones_like(qb))
                doe = jnp.where(masks[e], dob, jnp.zeros_like(dob))
                acc_scr[2] += _dot(p, doe, TN_DIMS)
                acc_scr[e] += _dot(ds, q_aug, TN_DIMS)
                dq_parts.append(_dot(ds, k_aug[e]))
            dq_ref[rows, :] += jnp.where(masks[0], dq_parts[0], dq_parts[1])
            drow_ref[rows, :] += jnp.where(masks[0], dq_parts[1], dq_parts[0])

        def stage(i, cur, nxt):
            scores_into(jnp.minimum(i + 1, nq - 1), nxt)
            accumulate(i - 1, nxt)
            pointwise(i, cur, False)

        acc_scr[...] = jnp.zeros_like(acc_scr)
        scores_into(kj, 0)
        pointwise(kj, 0, True)
        scores_into(jnp.minimum(kj + 1, nq - 1), 1)
        rest = nq - 1 - kj

        def two_stages(jj, carry):
            stage(kj + 1 + 2 * jj, 1, 0)
            stage(kj + 2 + 2 * jj, 0, 1)
            return carry

        lax.fori_loop(0, jnp.right_shift(rest, 1), two_stages, 0)
        odd = jnp.bitwise_and(rest, 1) == 1

        @pl.when(odd)
        def _():
            stage(nq - 1, 1, 0)
            accumulate(nq - 1, 1)

        @pl.when(jnp.logical_not(odd))
        def _():
            accumulate(nq - 1, 0)

        dk0, dk1, dv = acc_scr[0], acc_scr[1], acc_scr[2]
        dk_ref[...] = jnp.where(masks[0], dk0, dk1).astype(BF16)
        dcol_ref[...] = jnp.where(masks[0], dk1, dk0)
        dv_ref[...] = dv.astype(BF16)

        @pl.when(jnp.logical_and(pl.program_id(0) == N_PAIRS - 1, kj == nq - 1))
        def _():
            for cp in _exchange_copies(*exchange):
                cp.wait()

    full = lambda off: pl.BlockSpec((S, LANES), lambda h, j, off=off: (0, off + h))
    kv = lambda off: pl.BlockSpec((t, LANES), lambda h, j, off=off: (j, off + h))
    return pl.pallas_call(
        body, name=name, grid=(N_PAIRS, nq),
        out_shape=[jax.ShapeDtypeStruct((S, D), F32), jax.ShapeDtypeStruct((S, D), BF16),
                   jax.ShapeDtypeStruct((S, D), BF16), jax.ShapeDtypeStruct((S, D), F32),
                   jax.ShapeDtypeStruct((S, D), F32)] + _exchange_shapes(scatter, True),
        in_specs=[full(0), kv(N_PAIRS), kv(2 * N_PAIRS), full(0), full(0), full(0),
                  pl.BlockSpec((None, 2, t), lambda h, j: (h, 0, j))] + [ANY] * n,
        out_specs=[full(0), kv(0), kv(0), kv(0), full(0)] + [ANY] * n,
        scratch_shapes=_exchange_scratch(n) + [
            pltpu.VMEM((4, t, t), F32), pltpu.VMEM((4, t, t), F32),
            pltpu.VMEM((4, t, t), BF16), pltpu.VMEM((4, t, t), BF16),
            pltpu.VMEM((3, t, LANES), F32)],
        compiler_params=_params("arbitrary", "arbitrary"),
    )(qkv, qkv, qkv, do, lse, delta, c_t2, *scatter)


B_PAIRS = 4
B_NB = S // B_W


def _group_consts(g):
    nbs = jnp.where(g == 0, B_NB // B_DILS[0], jnp.where(g == 1, B_NB // B_DILS[1], B_NB // B_DILS[2]))
    dil = jnp.where(g == 0, B_DILS[0], jnp.where(g == 1, B_DILS[1], B_DILS[2]))
    return nbs, dil


def _band(dil):
    qi = lax.broadcasted_iota(jnp.int32, (B_W, B_W), 0)
    kj = lax.broadcasted_iota(jnp.int32, (B_W, B_W), 1)
    dist_c = qi - kj
    dist_p = qi + B_W - kj
    return (dist_c * dil).astype(F32), dist_c >= 0, (dist_p * dil).astype(F32), dist_p <= B_W


def _dil_fwd(qp, kp, vp, slopes, name):
    def body(sl_ref, q_ref, kp_ref, kc_ref, vp_ref, vc_ref, o_ref, lse_ref):
        g, n = pl.program_id(0), pl.program_id(1)
        nbs, dil = _group_consts(g)
        has_prev = (n % nbs) != 0
        lane, masks = _lane_masks()
        bias_c, ok_c, bias_p, ok_p = _band(dil)
        ok_p = jnp.logical_and(ok_p, has_prev)
        heads = [(hp, e) for hp in range(B_PAIRS) for e in range(2)]
        col = lambda ref, hp: ref[:, hp * LANES:(hp + 1) * LANES]
        logits = []
        for hp, e in heads:
            q = col(q_ref, hp) * SCALE
            qe = jnp.where(masks[e], q, jnp.zeros_like(q))
            logits.append((_dot(qe, col(kc_ref, hp), NT_DIMS), _dot(qe, col(kp_ref, hp), NT_DIMS)))
        probs = []
        for (hp, e), (sc, sp) in zip(heads, logits):
            slope = sl_ref[g * 8 + 2 * hp + e]
            sc = jnp.where(ok_c, sc - slope * bias_c, NEG)
            sp = jnp.where(ok_p, sp - slope * bias_p, NEG)
            m = jnp.maximum(jnp.max(sc, axis=1, keepdims=True), jnp.max(sp, axis=1, keepdims=True))
            probs.append((jnp.exp(sc - m).astype(BF16), jnp.exp(sp - m).astype(BF16), m))
        outs, lses = [], []
        for (hp, e), (pc, pp, m) in zip(heads, probs):
            vc, vpv = col(vc_ref, hp), col(vp_ref, hp)
            acc = (_dot(pc, jnp.where(masks[e], vc, jnp.ones_like(vc)))
                   + _dot(pp, jnp.where(masks[e], vpv, jnp.ones_like(vpv))))
            l = acc[:, HEAD_DIM:HEAD_DIM + 1] if e == 0 else acc[:, 0:1]
            outs.append(acc / l)
            lses.append(m + jnp.log(l))
        o_ref[...] = jnp.concatenate(
            [jnp.where(masks[0], outs[2 * hp], outs[2 * hp + 1]) for hp in range(B_PAIRS)], axis=1).astype(BF16)
        lse = jnp.zeros((B_W, LANES), F32)
        for h in range(2 * B_PAIRS):
            lse = jnp.where(lane == h, lses[h], lse)
        lse_ref[...] = lse

    cur = pl.BlockSpec((None, B_W, B_OUT), lambda g, n, sl: (g, n, 0))
    prev = pl.BlockSpec((None, B_W, B_OUT), lambda g, n, sl: (g, jnp.maximum(n - 1, 0), 0))
    stat = pl.BlockSpec((None, B_W, LANES), lambda g, n, sl: (g, n, 0))
    return pl.pallas_call(
        body, name=name,
        grid_spec=pltpu.PrefetchScalarGridSpec(
            num_scalar_prefetch=1, grid=(3, B_NB),
            in_specs=[cur, prev, cur, prev, cur], out_specs=[cur, stat]),
        out_shape=[jax.ShapeDtypeStruct((3, S, B_OUT), BF16), jax.ShapeDtypeStruct((3, S, LANES), F32)],
        compiler_params=_params("parallel", "parallel"),
    )(slopes, qp, kp, kp, vp, vp)


def _head_expander():
    r = lax.broadcasted_iota(jnp.int32, (LANES, B_OUT), 0)
    c = lax.broadcasted_iota(jnp.int32, (LANES, B_OUT), 1)
    return jnp.logical_and(c >= r * HEAD_DIM, c < (r + 1) * HEAD_DIM).astype(BF16)


def _dil_merge(og, lseg, name, tr=256):
    def body(o_ref, l_ref, out_ref, lse_ref):
        l0, l1, l2 = l_ref[0], l_ref[1], l_ref[2]
        m = jnp.maximum(jnp.maximum(l0, l1), l2)
        w0, w1, w2 = jnp.exp(l0 - m), jnp.exp(l1 - m), jnp.exp(l2 - m)
        den = w0 + w1 + w2
        lse_ref[...] = m + jnp.log(den)
        expand = _head_expander()
        out = None
        for g, w in enumerate((w0, w1, w2)):
            part = _split_dot(w / den, expand, 3) * o_ref[g].astype(F32)
            out = part if out is None else out + part
        out_ref[...] = out.astype(BF16)

    blk3 = pl.BlockSpec((3, tr, B_OUT), lambda i: (0, i, 0))
    stat3 = pl.BlockSpec((3, tr, LANES), lambda i: (0, i, 0))
    blk = pl.BlockSpec((tr, B_OUT), lambda i: (i, 0))
    stat = pl.BlockSpec((tr, LANES), lambda i: (i, 0))
    return pl.pallas_call(
        body, name=name, grid=(S // tr,),
        out_shape=[jax.ShapeDtypeStruct((S, B_OUT), BF16), jax.ShapeDtypeStruct((S, LANES), F32)],
        in_specs=[blk3, stat3], out_specs=[blk, stat], compiler_params=_params("parallel"),
    )(og, lseg)


def _head_rowsum_compact(a, b, name, tr=256):
    def body(a_ref, b_ref, o_ref):
        r = lax.broadcasted_iota(jnp.int32, (B_OUT, LANES), 0)
        c = lax.broadcasted_iota(jnp.int32, (B_OUT, LANES), 1)
        collect = jnp.logical_and(r >= c * HEAD_DIM, r < (c + 1) * HEAD_DIM).astype(BF16)
        prod = a_ref[...].astype(F32) * b_ref[...].astype(F32)
        o_ref[...] = _split_dot(prod, collect, 2)

    row = pl.BlockSpec((tr, B_OUT), lambda i: (i, 0))
    return pl.pallas_call(
        body, name=name, grid=(S // tr,), out_shape=jax.ShapeDtypeStruct((S, LANES), F32),
        in_specs=[row, row], out_specs=pl.BlockSpec((tr, LANES), lambda i: (i, 0)),
        compiler_params=_params("parallel"),
    )(a, b)


def _dil_bwd(qp, kp, vp, dop, lsep, dlp, slopes, name, scatter):
    n_ex = len(scatter)

    def body(sl_ref, *refs):
        (qc_ref, qn_ref, kp_ref, kc_ref, vp_ref, vc_ref, doc_ref, don_ref,
         lc_ref, ln_ref, dc_ref, dn_ref) = refs[:12]
        dq_ref, dk_ref, dv_ref = refs[12 + n_ex:15 + n_ex]
        exchange = (refs[12:12 + n_ex], refs[15 + n_ex:15 + 2 * n_ex], refs[15 + 2 * n_ex:], True)
        g, n = pl.program_id(0), pl.program_id(1)

        @pl.when(jnp.logical_and(g == 0, n == 0))
        def _():
            for cp in _exchange_copies(*exchange):
                cp.start()

        nbs, dil = _group_consts(g)
        has_prev = (n % nbs) != 0
        has_next = jnp.logical_and(n + 1 < B_NB, ((n + 1) % nbs) != 0)
        lane, masks = _lane_masks()
        bias_c, ok_c, bias_p, ok_p = _band(dil)
        ok_pp = jnp.logical_and(ok_p, has_prev)
        ok_np = jnp.logical_and(ok_p, has_next)
        heads = [(hp, e) for hp in range(B_PAIRS) for e in range(2)]
        col = lambda ref, hp: ref[:, hp * LANES:(hp + 1) * LANES]
        mask = lambda t, e: jnp.where(masks[e], t, jnp.zeros_like(t))
        raw = []
        for hp, e in heads:
            qce, qne = mask(col(qc_ref, hp) * SCALE, e), mask(col(qn_ref, hp) * SCALE, e)
            doce, done = mask(col(doc_ref, hp), e), mask(col(don_ref, hp), e)
            kc, kpv, vc, vpv = col(kc_ref, hp), col(kp_ref, hp), col(vc_ref, hp), col(vp_ref, hp)
            raw.append(((_dot(qce, kc, NT_DIMS), _dot(doce, vc, NT_DIMS)),
                        (_dot(qce, kpv, NT_DIMS), _dot(doce, vpv, NT_DIMS)),
                        (_dot(qne, kc, NT_DIMS), _dot(done, vc, NT_DIMS))))
        pds = []
        for (hp, e), tiles in zip(heads, raw):
            lo = 2 * hp + e
            slope = sl_ref[g * 8 + 2 * hp + e]
            lse_c, dl_c = lc_ref[:, lo:lo + 1], dc_ref[:, lo:lo + 1]
            lse_n, dl_n = ln_ref[:, lo:lo + 1], dn_ref[:, lo:lo + 1]
            out = []
            for (s, dp), ok, bias, lse, dl in ((tiles[0], ok_c, bias_c, lse_c, dl_c),
                                               (tiles[1], ok_pp, bias_p, lse_c, dl_c),
                                               (tiles[2], ok_np, bias_p, lse_n, dl_n)):
                p = jnp.exp(jnp.where(ok, s - slope * bias, NEG) - lse)
                out.append((p.astype(BF16), (p * (dp - dl)).astype(BF16)))
            pds.append(out)
        dq_all, dk_all, dv_all = [], [], []
        for hp in range(B_PAIRS):
            dq = jnp.zeros((B_W, LANES), F32)
            dk = jnp.zeros((B_W, LANES), F32)
            dv = jnp.zeros((B_W, LANES), F32)
            for e in range(2):
                (p_c, ds_c), (_, ds_p), (p_n, ds_n) = pds[2 * hp + e]
                qce, qne = mask(col(qc_ref, hp) * SCALE, e), mask(col(qn_ref, hp) * SCALE, e)
                doce, done = mask(col(doc_ref, hp), e), mask(col(don_ref, hp), e)
                dq = dq + _dot(ds_c, mask(col(kc_ref, hp) * SCALE, e)) + _dot(ds_p, mask(col(kp_ref, hp) * SCALE, e))
                dk = dk + _dot(ds_c, qce, TN_DIMS) + _dot(ds_n, qne, TN_DIMS)
                dv = dv + _dot(p_c, doce, TN_DIMS) + _dot(p_n, done, TN_DIMS)
            dq_all.append(dq)
            dk_all.append(dk)
            dv_all.append(dv)
        dq_ref[...] = jnp.concatenate(dq_all, axis=1).astype(BF16)
        dk_ref[...] = jnp.concatenate(dk_all, axis=1).astype(BF16)
        dv_ref[...] = jnp.concatenate(dv_all, axis=1).astype(BF16)

        @pl.when(jnp.logical_and(g == 2, n == B_NB - 1))
        def _():
            for cp in _exchange_copies(*exchange):
                cp.wait()

    cur = pl.BlockSpec((None, B_W, B_OUT), lambda g, n, sl: (g, n, 0))
    prev = pl.BlockSpec((None, B_W, B_OUT), lambda g, n, sl: (g, jnp.maximum(n - 1, 0), 0))
    nxt = pl.BlockSpec((None, B_W, B_OUT), lambda g, n, sl: (g, jnp.minimum(n + 1, B_NB - 1), 0))
    stat_cur = pl.BlockSpec((None, B_W, LANES), lambda g, n, sl: (g, n, 0))
    stat_nxt = pl.BlockSpec((None, B_W, LANES), lambda g, n, sl: (g, jnp.minimum(n + 1, B_NB - 1), 0))
    return pl.pallas_call(
        body, name=name,
        grid_spec=pltpu.PrefetchScalarGridSpec(
            num_scalar_prefetch=1, grid=(3, B_NB),
            in_specs=[cur, nxt, prev, cur, prev, cur, cur, nxt, stat_cur, stat_nxt, stat_cur, stat_nxt]
            + [ANY] * n_ex,
            out_specs=[cur, cur, cur] + [ANY] * n_ex,
            scratch_shapes=_exchange_scratch(n_ex)),
        out_shape=[jax.ShapeDtypeStruct((3, S, B_OUT), BF16)] * 3 + _exchange_shapes(scatter, True),
        compiler_params=_params("arbitrary", "arbitrary"),
    )(slopes, qp, qp, kp, kp, vp, vp, dop, dop, lsep, lsep, dlp, dlp, *scatter)


def _rows_block(shape, max_bytes=2 * 1024 * 1024):
    rows, cols = shape
    padded_cols = -(-cols // LANES) * LANES
    for tr in (1024, 512, 256, 128, 64, 32, 16):
        if rows % tr == 0 and tr * padded_cols * 4 <= max_bytes:
            return tr
    return rows


def _adam_update(w, m, v, g):
    m_new = ADAM_B1 * m + (1.0 - ADAM_B1) * g
    v_new = ADAM_B2 * v + (1.0 - ADAM_B2) * (g * g)
    m_hat = m_new / (1.0 - ADAM_B1 ** ADAM_STEP)
    v_hat = v_new / (1.0 - ADAM_B2 ** ADAM_STEP)
    delta = -ADAM_LR * (m_hat / (jnp.sqrt(v_hat) + ADAM_EPS) + ADAM_WD * w)
    return delta, m_new, v_new


def _adamw_sharded(w, m, v, parts, name):
    R, C = w.shape
    tr = _rows_block((R, C), max_bytes=1024 * 1024)

    def body(w_ref, m_ref, v_ref, p_ref, g_ref, d_ref, mo_ref, vo_ref):
        g = p_ref[0].astype(F32)
        for dev in range(1, N_DEV):
            g = g + p_ref[dev].astype(F32)
        g_ref[...] = g
        d_ref[...], mo_ref[...], vo_ref[...] = _adam_update(w_ref[...], m_ref[...], v_ref[...], g)

    blk = pl.BlockSpec((tr, C), lambda i: (i, 0))
    out = jax.ShapeDtypeStruct((R, C), F32)
    return pl.pallas_call(
        body, name=name, grid=(R // tr,),
        in_specs=[blk, blk, blk, pl.BlockSpec((N_DEV, tr, C), lambda i: (0, i, 0))],
        out_specs=[blk, blk, blk, blk], out_shape=[out, out, out, out],
        compiler_params=_params("parallel"),
    )(w, m, v, parts)


def _adamw_replicated(w, m, v, parts, name):
    def body(w_ref, m_ref, v_ref, p_ref, g_ref, d_ref, mo_ref, vo_ref):
        g = p_ref[0]
        for dev in range(1, N_DEV):
            g = g + p_ref[dev]
        g_ref[...] = g
        d_ref[...], mo_ref[...], vo_ref[...] = _adam_update(w_ref[...], m_ref[...], v_ref[...], g)

    out = jax.ShapeDtypeStruct(w.shape, F32)
    return pl.pallas_call(body, name=name, out_shape=[out, out, out, out], compiler_params=_params())(w, m, v, parts)


def _cols_from_slots(g):
    return g.transpose(1, 0, 2).reshape(g.shape[1], N_DEV * g.shape[2])


def _cols_to_slots(w):
    k, n = w.shape
    return w.reshape(k, N_DEV, n // N_DEV).transpose(1, 0, 2)


def _permute(t, dil):
    c = t.shape[1]
    return t.reshape(S // dil, dil, c).transpose(1, 0, 2).reshape(S, c)


def _unpermute(t, dil):
    c = t.shape[1]
    return t.reshape(dil, S // dil, c).transpose(1, 0, 2).reshape(S, c)


def _group_permute(t):
    return jnp.stack([_permute(t[:, g * B_OUT:(g + 1) * B_OUT], B_DILS[g]) for g in range(3)])


def _same_permute(t):
    return jnp.stack([_permute(t, d) for d in B_DILS])


def _group_unpermute(t):
    return jnp.stack([_unpermute(t[g], B_DILS[g]) for g in range(3)])


SMALL_ROWS = 144


def _pack_small(a_b_f, kv_g, mix_g, ffn_g, conv_b, fin_g):
    flat = jnp.concatenate([a_b_f.reshape(-1), kv_g.reshape(-1), mix_g.reshape(-1), ffn_g.reshape(-1),
                            conv_b.reshape(-1), fin_g.reshape(-1)])
    return jnp.pad(flat, (0, SMALL_ROWS * LANES - flat.shape[0])).reshape(SMALL_ROWS, LANES)


def _unpack_small(p):
    flat = p.reshape(-1)
    out, off = [], 0
    for shape in ((1, A_HEADS), (D,), (2, D), (2, D), (2, 2 * D_FF), (D,)):
        size = math.prod(shape)
        out.append(flat[off:off + size].reshape(shape))
        off += size
    return out


def _unpack_late(g):
    half = N_DEV // 2
    up = g[4].reshape(N_DEV, 2, D, -1)
    w_up_a = [up[:half, l].transpose(1, 0, 2).reshape(D, D_FF) for l in range(2)]
    w_up_g = [up[half:, l].transpose(1, 0, 2).reshape(D, D_FF) for l in range(2)]
    w_down = [g[5].reshape(N_DEV, 2, -1, D)[:, l].reshape(D_FF, D) for l in range(2)]
    conv_w = [g[6].reshape(N_DEV, 2, 3, -1)[:, l].transpose(1, 0, 2).reshape(3, 2 * D_FF) for l in range(2)]
    return (g[0].reshape(D, D), _cols_from_slots(g[1]), _cols_from_slots(g[2]), _cols_from_slots(g[3]),
            w_up_a, w_up_g, w_down, conv_w)


def _ffn_slots(dw_up, dw_down_t, dconv_w):
    return [_cols_to_slots(dw_up), dw_down_t.reshape(D, N_DEV, -1).transpose(1, 2, 0), _cols_to_slots(dconv_w)]


def _local_step(x0, target, w_in_pad, late_shards,
                a_b_f, kv_norm_g, mix_norm_g, ffn_norm_g, ffn_conv_b, final_norm_g):
    w_qkv, w_f = w_in_pad[:, :A_QKV], w_in_pad[:, A_QKV:]
    conv_b = ffn_conv_b.reshape(2, 1, 2 * D_FF)
    slopes = jnp.exp2(-8.0 * jnp.arange(1, 25, dtype=F32) / 24)

    def gain(g):
        return g.reshape(1, D)

    (h1,) = _rmsnorm_fwd(x0, [gain(mix_norm_g[0])], "norm_mix0")
    qkv = _matmul(h1, w_qkv, mode="nn", out_dtype=BF16, name="proj_qkv", tm=512, tn=A_QKV)
    z = _matmul(h1, w_f, mode="nn", out_dtype=F32, name="proj_gate", tm=S, tn=LANES)
    z_t = z[:, :A_HEADS].T
    b_f = a_b_f.reshape(A_HEADS, 1)
    c_t = _fox_prep_fwd(z_t, b_f, "fox_prep")
    c_t2 = c_t.reshape(N_PAIRS, 2, S)
    o_a, lse_a, *late = _fox_fwd(qkv, c_t2, "fox_fwd", late_shards)
    w_out, w_q, w_bo, w_kvf, w_up_a, w_up_g, w_down, conv_w = _unpack_late(late)
    x1 = _matmul(o_a, w_out, mode="nn", out_dtype=F32, name="a_out", tm=512, tn=D, res=x0)

    def ffn_fwd(xin, layer):
        (h,) = _rmsnorm_fwd(xin, [gain(ffn_norm_g[layer])], f"norm_ffn{layer}")
        u = (_matmul(h, w_up_a[layer], mode="nn", out_dtype=BF16, name=f"ffn_up_a{layer}", tm=512, tn=D_FF),
             _matmul(h, w_up_g[layer], mode="nn", out_dtype=BF16, name=f"ffn_up_g{layer}", tm=512, tn=D_FF))
        act = _convgate_fwd(*u, conv_w[layer], conv_b[layer], f"convgate{layer}")
        xout = _matmul(act, w_down[layer], mode="nn", out_dtype=F32, name=f"ffn_down{layer}", tm=512, tn=D, res=xin)
        return h, u, act, xout

    h2, u0, act0, x2 = ffn_fwd(x1, 0)
    hk, h3 = _rmsnorm_fwd(x2, [gain(kv_norm_g), gain(mix_norm_g[1])], "norm_kv_mix1")
    kv = _matmul(hk, w_kvf, mode="nn", out_dtype=BF16, name="proj_kv", tm=512, tn=B_KV)
    qb = _matmul(h3, w_q, mode="nn", out_dtype=BF16, name="proj_qb", tm=512, tn=B_Q)
    qp, kp, vp = _group_permute(qb), _group_permute(kv[:, :B_Q]), _group_permute(kv[:, B_Q:])
    og_p, lseg_p = _dil_fwd(qp, kp, vp, slopes, "dil_fwd")
    o_b, lse_b = _dil_merge(_group_unpermute(og_p), _group_unpermute(lseg_p), "dil_merge")
    x3 = _matmul(o_b, w_bo, mode="nn", out_dtype=F32, name="b_out", tm=512, tn=D, res=x2)
    h4, u1, act1, x4 = ffn_fwd(x3, 1)
    loss_blk, dx4, dx4b, dg_final = _final_loss(x4, target, gain(final_norm_g), "final_loss")

    def ffn_bwd(dx, dxb, xin, h, u, act, layer):
        dact = _matmul(dxb, w_down[layer], mode="nt", out_dtype=BF16, name=f"d_act{layer}", tm=512, tn=D_FF)
        dw_down = _matmul_tn(dxb, act, out_dtype=BF16, name=f"dw_down_t{layer}")
        du_a, du_g, dwa, dwg, dba, dbg = _convgate_bwd(*u, conv_w[layer], conv_b[layer], dact, f"convgate_bwd{layer}")
        dw_up = jnp.concatenate(
            [_matmul_tn(h, du_a, out_dtype=BF16, name=f"dw_up_a{layer}"),
             _matmul_tn(h, du_g, out_dtype=BF16, name=f"dw_up_g{layer}")], axis=1)
        dxin, dxinb, dgain = _matmul_norm_bwd([(du_a, w_up_a[layer]), (du_g, w_up_g[layer])], xin,
                                              gain(ffn_norm_g[layer]), dx, f"dh_ffn_norm_bwd{layer}")
        dconv_w = jnp.concatenate([dwa, dwg], axis=1)
        dconv_b = jnp.concatenate([dba, dbg], axis=1)
        return dxin, dxinb, dgain, dw_up, dw_down, dconv_w, dconv_b

    dx3, dx3b, dg_ffn1, dw_up1, dw_down1, dconv_w1, dconv_b1 = ffn_bwd(dx4, dx4b, x3, h4, u1, act1, 1)

    do_b = _matmul(dx3b, w_bo, mode="nt", out_dtype=BF16, name="d_ob", tm=1024, tn=B_OUT)
    dw_bo = _matmul_tn(o_b, dx3b, out_dtype=BF16, name="dw_bo")
    dl_b = _head_rowsum_compact(do_b, o_b, "delta_b")
    slots_up1, slots_down1, slots_conv1 = _ffn_slots(dw_up1, dw_down1, dconv_w1)
    dqp, dkp, dvp, land_down1, land_conv1 = _dil_bwd(
        qp, kp, vp, _same_permute(do_b), _same_permute(lse_b), _same_permute(dl_b), slopes, "dil_bwd",
        [slots_down1, slots_conv1])

    def natural(tp):
        return jnp.concatenate([_unpermute(tp[g], B_DILS[g]) for g in range(3)], axis=1)

    dqb = natural(dqp)
    dkv = jnp.concatenate([natural(dkp), natural(dvp)], axis=1)
    dw_q = _matmul_tn(h3, dqb, out_dtype=BF16, name="dw_q")
    dw_kv = _matmul_tn(hk, dkv, out_dtype=BF16, name="dw_kv")
    dx2, _, dg_mix1 = _matmul_norm_bwd([(dqb, w_q)], x2, gain(mix_norm_g[1]), dx3, "dh_mix1_norm_bwd")
    dx2, dx2b, dg_kv = _matmul_norm_bwd([(dkv, w_kvf)], x2, gain(kv_norm_g), dx2, "dh_kv_norm_bwd")

    dx1, dx1b, dg_ffn0, dw_up0, dw_down0, dconv_w0, dconv_b0 = ffn_bwd(dx2, dx2b, x1, h2, u0, act0, 0)

    do_a = _matmul(dx1b, w_out, mode="nt", out_dtype=BF16, name="d_oa", tm=512, tn=D)
    dw_out = _matmul_tn(o_a, dx1b, out_dtype=BF16, name="dw_out")
    dl_a = _head_rowsum(do_a, o_a, "delta_a")
    dq_a, dk_a, dv_a, dcol, drow, *land = _fox_bwd(
        qkv, do_a, lse_a, dl_a, c_t2, "fox_bwd",
        [dw_out.reshape(N_DEV, D // N_DEV, D), _cols_to_slots(dw_q), _cols_to_slots(dw_bo), _cols_to_slots(dw_kv)]
        + _ffn_slots(dw_up0, dw_down0, dconv_w0) + [slots_up1])
    land_out, land_q, land_bo, land_kv, land_up0, land_down0, land_conv0, land_up1 = land

    def head_sums(t):
        return t.reshape(S, N_PAIRS, 2, HEAD_DIM)[:, :, ::-1, 0].reshape(S, A_HEADS).T

    dz_t, db_f = _fox_prep_bwd(head_sums(drow), head_sums(dcol), z_t, b_f, "fox_prep_bwd")
    dz = jnp.pad(dz_t.T, ((0, 0), (0, LANES - A_HEADS))).astype(BF16)
    dproj = jnp.concatenate([dq_a.astype(BF16), dk_a, dv_a, dz], axis=1)
    dw_in = _matmul_tn(h1, dproj, out_dtype=BF16, name="dw_in")
    grad_x, _, dg_mix0, land_in = _matmul_norm_bwd(
        [(dproj, w_in_pad)], x0, gain(mix_norm_g[0]), dx1, "dh_mix0_norm_bwd",
        scatter=[_cols_to_slots(dw_in[:, :A_QKV + A_HEADS])])

    dg_mix = jnp.concatenate([dg_mix0, dg_mix1], axis=0)
    dg_ffn = jnp.concatenate([dg_ffn0, dg_ffn1], axis=0)
    dconv_b = jnp.concatenate([dconv_b0, dconv_b1], axis=0)
    small_part = _pack_small(db_f, dg_kv, dg_mix, dg_ffn, dconv_b, dg_final)
    _, (small_parts,) = _final_exchange([], [small_part], "gather_small_grads")
    landed = [land_in, land_out, land_q, land_bo, land_kv, land_up0, land_up1, land_down0, land_down1,
              land_conv0, land_conv1]
    return loss_blk, grad_x, landed, small_parts


def kernel(x, a_w_in, a_b_f, a_w_out, b_w_q, b_w_out, kv_norm_g, w_kv, mix_norm_g, ffn_norm_g, ffn_w_up, ffn_conv_w, ffn_conv_b, ffn_w_down, final_norm_g, loss_target, m_a_w_in, m_a_b_f, m_a_w_out, m_b_w_q, m_b_w_out, m_kv_norm_g, m_w_kv, m_mix_norm_g, m_ffn_norm_g, m_ffn_w_up, m_ffn_conv_w, m_ffn_conv_b, m_ffn_w_down, m_final_norm_g, v_a_w_in, v_a_b_f, v_a_w_out, v_b_w_q, v_b_w_out, v_kv_norm_g, v_w_kv, v_mix_norm_g, v_ffn_norm_g, v_ffn_w_up, v_ffn_conv_w, v_ffn_conv_b, v_ffn_w_down, v_final_norm_g):
    def shards(a_w_in, a_w_out, b_w_q, b_w_out, w_kv, ffn_w_up, ffn_w_down, ffn_conv_w):
        return [a_w_in[0], a_w_out[0], b_w_q[0], b_w_out[0], w_kv, ffn_w_up[0], ffn_w_up[1],
                ffn_w_down[0], ffn_w_down[1], ffn_conv_w[0], ffn_conv_w[1]]

    w_loc = shards(a_w_in, a_w_out, b_w_q, b_w_out, w_kv, ffn_w_up, ffn_w_down, ffn_conv_w)
    m_loc = shards(m_a_w_in, m_a_w_out, m_b_w_q, m_b_w_out, m_w_kv, m_ffn_w_up, m_ffn_w_down, m_ffn_conv_w)
    v_loc = shards(v_a_w_in, v_a_w_out, v_b_w_q, v_b_w_out, v_w_kv, v_ffn_w_up, v_ffn_w_down, v_ffn_conv_w)

    (g_in,) = _all_gather([a_w_in[0].astype(BF16)], "gather_a_w_in")
    w_in = _cols_from_slots(g_in)
    w_in_pad = jnp.pad(w_in, ((0, 0), (0, A_PROJ_PAD - w_in.shape[1])))
    late_shards = [a_w_out[0].astype(BF16), b_w_q[0].astype(BF16), b_w_out[0].astype(BF16), w_kv.astype(BF16),
                   ffn_w_up.reshape(2 * D, -1).astype(BF16), ffn_w_down.reshape(-1, D).astype(BF16),
                   ffn_conv_w.reshape(6, -1)]

    loss_blk, grad_x, landed, small_parts = _local_step(
        x[0], loss_target[0], w_in_pad, late_shards,
        a_b_f, kv_norm_g, mix_norm_g, ffn_norm_g, ffn_conv_b, final_norm_g)

    big = [_adamw_sharded(w_loc[k], m_loc[k], v_loc[k], landed[k], f"adamw{k}") for k in range(11)]

    small = _adamw_replicated(
        _pack_small(a_b_f, kv_norm_g, mix_norm_g, ffn_norm_g, ffn_conv_b, final_norm_g),
        _pack_small(m_a_b_f, m_kv_norm_g, m_mix_norm_g, m_ffn_norm_g, m_ffn_conv_b, m_final_norm_g),
        _pack_small(v_a_b_f, v_kv_norm_g, v_mix_norm_g, v_ffn_norm_g, v_ffn_conv_b, v_final_norm_g),
        small_parts, "adamw_small")

    loss = lax.psum(loss_blk[0, 0], ("x", "y", "c"))

    def assemble(kind):
        b = [r[kind] for r in big]
        s_abf, s_kv, s_mix, s_ffn, s_cb, s_fin = _unpack_small(small[kind])
        return [b[0][None], s_abf, b[1][None], b[2][None], b[3][None], s_kv, b[4], s_mix, s_ffn,
                jnp.stack([b[5], b[6]]), jnp.stack([b[9], b[10]]), s_cb, jnp.stack([b[7], b[8]]), s_fin]

    return (loss, grad_x[None], *assemble(0), *assemble(1), *assemble(2), *assemble(3))
```

```python
import math

import jax
import jax.numpy as jnp
from jax import lax
from jax.experimental import pallas as pl
from jax.experimental.pallas import tpu as pltpu

F32 = jnp.float32
BF16 = jnp.bfloat16

S = 4096
D = 1024
N_DEV = 8
A_HEADS = 16
HEAD_DIM = 64
A_QKV = 3072
A_PROJ_PAD = 3200
B_Q = 1536
B_OUT = 512
B_KV = 3072
B_W = 128
B_DILS = (1, 4, 16)
D_FF = 2816
RMS_EPS = 1e-6
SCALE = HEAD_DIM ** -0.5
NEG = -1e30

ADAM_LR = 0.001
ADAM_B1 = 0.9
ADAM_B2 = 0.999
ADAM_EPS = 1e-08
ADAM_WD = 0.01
ADAM_STEP = 10

LANES = 128
VMEM_LIMIT = 56 * 1024 * 1024
MESH = pl.DeviceIdType.MESH
ANY = pl.BlockSpec(memory_space=pl.ANY)

NT_DIMS = (((1,), (1,)), ((), ()))
TN_DIMS = (((0,), (0,)), ((), ()))
NN_DIMS = (((1,), (0,)), ((), ()))


def _params(*sem):
    return pltpu.CompilerParams(dimension_semantics=sem if sem else None, vmem_limit_bytes=VMEM_LIMIT)


def _dot(a, b, dims=NN_DIMS):
    return lax.dot_general(a, b, dims, preferred_element_type=F32)


def _split_dot(x, mat, pieces):
    out = None
    rem = x
    for _ in range(pieces):
        part = rem.astype(BF16)
        rem = rem - part.astype(F32)
        d = _dot(part, mat)
        out = d if out is None else out + d
    return out


def _pick(n, prefs):
    for p in prefs:
        if n % p == 0:
            return p
    return n


def _gather_phases(ins, outs, sems):
    n = len(ins)
    if n == 0:
        return (lambda: None,) * 3
    send_sems, recv_sems, local_sems = sems
    x, y, c = lax.axis_index("x"), lax.axis_index("y"), lax.axis_index("c")
    me, sibling = (x, y, c), (x, y, 1 - c)
    chips = [(1 - x, y), (x, 1 - y), (1 - x, 1 - y)]

    def slot(a, px, py, pc):
        return outs[a].at[4 * px + 2 * py + pc]

    def copy(a, k, block, to, src=None):
        return pltpu.make_async_remote_copy(
            src_ref=slot(a, *block) if src is None else src, dst_ref=slot(a, *block),
            send_sem=send_sems.at[a, k], recv_sem=recv_sems.at[a, k],
            device_id=to, device_id_type=MESH)

    mine = [pltpu.make_async_copy(ins[a], slot(a, *me), local_sems.at[a]) for a in range(n)]
    first = []
    for a in range(n):
        first.append(copy(a, 0, me, sibling, src=ins[a]))
        first += [copy(a, 1 + j, me, (*chip, c), src=ins[a]) for j, chip in enumerate(chips)]
    passed = [copy(a, 4 + j, (*chip, c), sibling) for j, chip in enumerate(chips) for a in range(n)]

    def start():
        for cp in mine + first:
            cp.start()

    def forward():
        k = 0
        for j, chip in enumerate(chips):
            for a in range(n):
                copy(a, 1 + j, (*chip, c), me).wait_recv()
                passed[k].start()
                k += 1

    def finish():
        for a in range(n):
            copy(a, 0, sibling, me).wait_recv()
            for j, chip in enumerate(chips):
                copy(a, 4 + j, (*chip, 1 - c), me).wait_recv()
        for cp in first + passed:
            cp.wait_send()
        for cp in mine:
            cp.wait()

    return start, forward, finish


def _all_gather(arrays, name):
    n = len(arrays)

    def body(*refs):
        for phase in _gather_phases(refs[:n], refs[n:2 * n], refs[2 * n:]):
            phase()

    return pl.pallas_call(
        body, name=name,
        out_shape=[jax.ShapeDtypeStruct((N_DEV,) + a.shape, a.dtype) for a in arrays],
        in_specs=[ANY] * n, out_specs=[ANY] * n,
        scratch_shapes=[pltpu.SemaphoreType.DMA((n, 7)), pltpu.SemaphoreType.DMA((n, 7)),
                        pltpu.SemaphoreType.DMA((n,))],
    )(*arrays)


PEER_FLIPS = [(dx, dy, dc) for dx in (0, 1) for dy in (0, 1) for dc in (0, 1) if (dx, dy, dc) != (0, 0, 0)]


def _exchange_copies(ins, outs, sems, scatter):
    if not ins:
        return []
    send_sems, recv_sems, local_sems = sems
    x, y, c = lax.axis_index("x"), lax.axis_index("y"), lax.axis_index("c")
    me = 4 * x + 2 * y + c
    copies = []
    for a in range(len(ins)):
        copies.append(pltpu.make_async_copy(ins[a].at[me] if scatter else ins[a], outs[a].at[me], local_sems.at[a]))
        for k, (dx, dy, dc) in enumerate(PEER_FLIPS):
            px, py, pc = (1 - x if dx else x), (1 - y if dy else y), (1 - c if dc else c)
            copies.append(pltpu.make_async_remote_copy(
                src_ref=ins[a].at[4 * px + 2 * py + pc] if scatter else ins[a], dst_ref=outs[a].at[me],
                send_sem=send_sems.at[a, k], recv_sem=recv_sems.at[a, k],
                device_id=(px, py, pc), device_id_type=MESH))
    return copies


def _exchange_scratch(n):
    if n == 0:
        return []
    return [pltpu.SemaphoreType.DMA((n, 7)), pltpu.SemaphoreType.DMA((n, 7)), pltpu.SemaphoreType.DMA((n,))]


def _exchange_shapes(arrays, scatter):
    return [jax.ShapeDtypeStruct((N_DEV,) + (a.shape[1:] if scatter else a.shape), a.dtype) for a in arrays]


def _final_exchange(scatter, gather, name):
    ns, ng = len(scatter), len(gather)

    def body(*refs):
        ins, outs, sems = refs[:ns + ng], refs[ns + ng:2 * (ns + ng)], refs[2 * (ns + ng):]
        n_sems = len(_exchange_scratch(ns))
        copies = (_exchange_copies(ins[:ns], outs[:ns], sems[:n_sems], True)
                  + _exchange_copies(ins[ns:], outs[ns:], sems[n_sems:], False))
        for cp in copies:
            cp.start()
        for cp in copies:
            cp.wait()

    res = pl.pallas_call(
        body, name=name, out_shape=_exchange_shapes(scatter, True) + _exchange_shapes(gather, False),
        in_specs=[ANY] * (ns + ng), out_specs=[ANY] * (ns + ng),
        scratch_shapes=_exchange_scratch(ns) + _exchange_scratch(ng),
    )(*scatter, *gather)
    return res[:ns], res[ns:]


MM_ROWS = 512
MM_COLS = 1024


def _matmul(a, b, *, mode, out_dtype, name, tm, tn, res=None):
    if mode == "nn":
        (M, K), (K2, N) = a.shape, b.shape
    else:
        (M, K), (N, K2) = a.shape, b.shape
    assert K == K2, (a.shape, b.shape, mode)
    tm, tn = min(tm, M), min(tn, N)
    sm = min(tm, MM_ROWS)
    sn = tn if tn <= MM_COLS else _pick(tn, (512, 256, 128))
    assert M % tm == 0 and N % tn == 0 and tm % sm == 0, (M, N, K, tm, tn)
    dims = NN_DIMS if mode == "nn" else NT_DIMS
    a_spec = pl.BlockSpec((tm, K), lambda i, j: (i, 0))
    if mode == "nt":
        b_spec = pl.BlockSpec((tn, K), lambda i, j: (j, 0))
    else:
        b_spec = pl.BlockSpec((K, tn), lambda i, j: (0, j))
    o_spec = pl.BlockSpec((tm, tn), lambda i, j: (i, j))
    has_res = res is not None

    def body(*refs):
        a_ref, b_ref = refs[0], refs[1]
        r_ref = refs[2] if has_res else None
        o_ref = refs[2 + has_res]

        def chunk(r, carry):
            rows = pl.ds(pl.multiple_of(r * sm, sm), sm)
            av = a_ref[rows, :]
            for c0 in range(0, tn, sn):
                bv = b_ref[c0:c0 + sn, :] if mode == "nt" else b_ref[:, c0:c0 + sn]
                total = _dot(av, bv, dims)
                if has_res:
                    total = total + r_ref[rows, c0:c0 + sn]
                o_ref[rows, c0:c0 + sn] = total.astype(out_dtype)
            return carry

        lax.fori_loop(0, tm // sm, chunk, 0)

    return pl.pallas_call(
        body, name=name, grid=(M // tm, N // tn),
        out_shape=jax.ShapeDtypeStruct((M, N), out_dtype),
        in_specs=[a_spec, b_spec] + ([o_spec] if has_res else []),
        out_specs=o_spec,
        compiler_params=_params("parallel", "parallel"),
    )(*((a, b, res) if has_res else (a, b)))


def _matmul_tn(a, b, *, out_dtype, name, tk=512, sm=256):
    (K, M), (K2, N) = a.shape, b.shape
    assert K == K2 and K % tk == 0 and M % sm == 0, (a.shape, b.shape)
    nk = K // tk

    def body(a_ref, b_ref, o_ref, acc_ref):
        k = pl.program_id(0)

        @pl.when(k == 0)
        def _():
            acc_ref[...] = jnp.zeros_like(acc_ref)

        def chunk(mi, carry):
            cols = pl.ds(pl.multiple_of(mi * sm, sm), sm)
            acc_ref[cols, :] += _dot(a_ref[:, cols].T, b_ref[...])
            return carry

        lax.fori_loop(0, M // sm, chunk, 0)

        @pl.when(k == nk - 1)
        def _():
            def emit(mi, carry):
                rows = pl.ds(pl.multiple_of(mi * sm, sm), sm)
                o_ref[rows, :] = acc_ref[rows, :].astype(out_dtype)
                return carry
            lax.fori_loop(0, M // sm, emit, 0)

    return pl.pallas_call(
        body, name=name, grid=(nk,),
        out_shape=jax.ShapeDtypeStruct((M, N), out_dtype),
        in_specs=[pl.BlockSpec((tk, M), lambda k: (k, 0)), pl.BlockSpec((tk, N), lambda k: (k, 0))],
        out_specs=pl.BlockSpec((M, N), lambda k: (0, 0)),
        scratch_shapes=[pltpu.VMEM((M, N), F32)],
        compiler_params=_params("arbitrary"),
    )(a, b)


def _rmsnorm_fwd(x, gains, name, tr=256):
    n = len(gains)

    def body(*refs):
        x_ref = refs[0]
        xv = x_ref[...]
        r = lax.rsqrt(jnp.mean(xv * xv, axis=-1, keepdims=True) + RMS_EPS)
        y = xv * r
        for a in range(n):
            refs[1 + n + a][...] = (y * refs[1 + a][...]).astype(BF16)

    row = pl.BlockSpec((tr, D), lambda i: (i, 0))
    gain = pl.BlockSpec((1, D), lambda i: (0, 0))
    return pl.pallas_call(
        body, name=name, grid=(S // tr,),
        out_shape=[jax.ShapeDtypeStruct((S, D), BF16)] * n,
        in_specs=[row] + [gain] * n, out_specs=[row] * n,
        compiler_params=_params("parallel"),
    )(x, *gains)


def _matmul_norm_bwd(pairs, x, g, dres, name, scatter=(), tm=512):
    M = x.shape[0]
    n_p, n_ex = len(pairs), len(scatter)
    n_in = 2 * n_p + 3
    steps = M // tm

    def body(*refs):
        x_ref, g_ref, dres_ref = refs[2 * n_p:n_in]
        dx_ref, dxb_ref, dg_ref = refs[n_in + n_ex:n_in + n_ex + 3]
        exchange = (refs[n_in:n_in + n_ex], refs[n_in + n_ex + 3:n_in + 2 * n_ex + 3], refs[n_in + 2 * n_ex + 3:], True)

        @pl.when(pl.program_id(0) == 0)
        def _():
            for cp in _exchange_copies(*exchange):
                cp.start()

        dyv = _dot(refs[0][...], refs[1][...], NT_DIMS)
        for p in range(1, n_p):
            dyv = dyv + _dot(refs[2 * p][...], refs[2 * p + 1][...], NT_DIMS)
        xv = x_ref[...]
        r = lax.rsqrt(jnp.mean(xv * xv, axis=-1, keepdims=True) + RMS_EPS)
        xhat = xv * r
        dxhat = dyv * g_ref[...]
        mean_term = jnp.mean(dxhat * xhat, axis=-1, keepdims=True)
        dx = r * (dxhat - xhat * mean_term) + dres_ref[...]
        dx_ref[...] = dx
        dxb_ref[...] = dx.astype(BF16)
        part = jnp.sum(dyv * xhat, axis=0, keepdims=True)

        @pl.when(pl.program_id(0) == 0)
        def _():
            dg_ref[...] = part

        @pl.when(pl.program_id(0) > 0)
        def _():
            dg_ref[...] += part

        @pl.when(pl.program_id(0) == steps - 1)
        def _():
            for cp in _exchange_copies(*exchange):
                cp.wait()

    row = pl.BlockSpec((tm, D), lambda i: (i, 0))
    gain = pl.BlockSpec((1, D), lambda i: (0, 0))
    pair_specs, operands = [], []
    for a, b in pairs:
        assert a.shape == (M, b.shape[1]) and b.shape[0] == D, (a.shape, b.shape)
        pair_specs += [pl.BlockSpec((tm, a.shape[1]), lambda i: (i, 0)), pl.BlockSpec(b.shape, lambda i: (0, 0))]
        operands += [a, b]
    return pl.pallas_call(
        body, name=name, grid=(steps,),
        out_shape=[jax.ShapeDtypeStruct((M, D), F32), jax.ShapeDtypeStruct((M, D), BF16),
                   jax.ShapeDtypeStruct((1, D), F32)] + _exchange_shapes(scatter, True),
        in_specs=pair_specs + [row, gain, row] + [ANY] * n_ex,
        out_specs=[row, row, gain] + [ANY] * n_ex,
        scratch_shapes=_exchange_scratch(n_ex),
        compiler_params=_params("arbitrary"),
    )(*operands, x, g, dres, *scatter)


def _final_loss(x, target, g, name, tr=256):
    def body(x_ref, t_ref, g_ref, loss_ref, dx_ref, dxb_ref, dg_ref):
        xv = x_ref[...]
        gv = g_ref[...]
        r = lax.rsqrt(jnp.mean(xv * xv, axis=-1, keepdims=True) + RMS_EPS)
        xhat = xv * r
        err = xhat * gv - t_ref[...]
        row_loss = jnp.mean(err * err, axis=-1, keepdims=True)
        lpart = 0.5 * jnp.sum(row_loss, axis=0, keepdims=True)
        dyv = err / D
        dxhat = dyv * gv
        mean_term = jnp.mean(dxhat * xhat, axis=-1, keepdims=True)
        dx = r * (dxhat - xhat * mean_term)
        dx_ref[...] = dx
        dxb_ref[...] = dx.astype(BF16)
        gpart = jnp.sum(dyv * xhat, axis=0, keepdims=True)

        @pl.when(pl.program_id(0) == 0)
        def _():
            dg_ref[...] = gpart
            loss_ref[...] = jnp.broadcast_to(lpart, loss_ref.shape)

        @pl.when(pl.program_id(0) > 0)
        def _():
            dg_ref[...] += gpart
            loss_ref[...] += jnp.broadcast_to(lpart, loss_ref.shape)

    row = pl.BlockSpec((tr, D), lambda i: (i, 0))
    gain = pl.BlockSpec((1, D), lambda i: (0, 0))
    lspec = pl.BlockSpec((8, LANES), lambda i: (0, 0))
    return pl.pallas_call(
        body, name=name, grid=(S // tr,),
        out_shape=[jax.ShapeDtypeStruct((8, LANES), F32), jax.ShapeDtypeStruct((S, D), F32),
                   jax.ShapeDtypeStruct((S, D), BF16), jax.ShapeDtypeStruct((1, D), F32)],
        in_specs=[row, row, gain], out_specs=[lspec, row, row, gain],
        compiler_params=_params("arbitrary"),
    )(x, target, g)


CONV_TR = 128
CONV_TC = D_FF
CONV_NJ = D_FF // CONV_TC
HALO = 16


def _causal_taps(cur_ref, prev_ref, first):
    xv = cur_ref[...].astype(F32)
    pv = prev_ref[...].astype(F32)
    p1 = jnp.where(first, 0.0, pv[HALO - 1:HALO, :])
    p2 = jnp.where(first, 0.0, pv[HALO - 2:HALO - 1, :])
    r1, r2 = pltpu.roll(xv, 1, 0), pltpu.roll(xv, 2, 0)
    row = lax.broadcasted_iota(jnp.int32, (8, xv.shape[1]), 0)
    xm1 = jnp.concatenate([jnp.where(row == 0, p1, r1[0:8]), r1[8:]], axis=0)
    xm2 = jnp.concatenate([jnp.where(row == 0, p2, jnp.where(row == 1, p1, r2[0:8])), r2[8:]], axis=0)
    return xv, xm1, xm2


def _conv_specs():
    def prev_row(i):
        return jnp.maximum(i * (CONV_TR // HALO) - 1, 0)
    ua = pl.BlockSpec((CONV_TR, CONV_TC), lambda i, j: (i, j))
    ug = ua
    pa = pl.BlockSpec((HALO, CONV_TC), lambda i, j: (prev_row(i), j))
    pg = pa
    wa = pl.BlockSpec((3, CONV_TC), lambda i, j: (0, j))
    wg = pl.BlockSpec((3, CONV_TC), lambda i, j: (0, j + CONV_NJ))
    ba = pl.BlockSpec((1, CONV_TC), lambda i, j: (0, j))
    bg = pl.BlockSpec((1, CONV_TC), lambda i, j: (0, j + CONV_NJ))
    return [ua, pa, ug, pg, wa, wg, ba, bg]


def _convgate_fwd(u_a, u_g, w, b, name):
    def body(ua, pa, ug, pg, wa, wg, ba, bg, o_ref):
        first = pl.program_id(0) == 0
        x0, x1, x2 = _causal_taps(ua, pa, first)
        ac = wa[0:1, :] * x2 + wa[1:2, :] * x1 + wa[2:3, :] * x0 + ba[...]
        x0, x1, x2 = _causal_taps(ug, pg, first)
        gc = wg[0:1, :] * x2 + wg[1:2, :] * x1 + wg[2:3, :] * x0 + bg[...]
        sg = 0.5 * jnp.tanh(0.5 * gc) + 0.5
        o_ref[...] = (gc * sg * ac).astype(BF16)

    return pl.pallas_call(
        body, name=name, grid=(S // CONV_TR, CONV_NJ),
        out_shape=jax.ShapeDtypeStruct((S, D_FF), BF16),
        in_specs=_conv_specs(),
        out_specs=pl.BlockSpec((CONV_TR, CONV_TC), lambda i, j: (i, j)),
        compiler_params=_params("parallel", "parallel"),
    )(u_a, u_a, u_g, u_g, w, w, b, b)


def _anticausal_conv(d, nxt_ref, w_ref, last):
    n1 = jnp.where(last, 0.0, nxt_ref[0:1, :])
    n2 = jnp.where(last, 0.0, nxt_ref[1:2, :])
    r1, r2 = pltpu.roll(d, CONV_TR - 1, 0), pltpu.roll(d, CONV_TR - 2, 0)
    row = lax.broadcasted_iota(jnp.int32, (8, d.shape[1]), 0)
    cut = CONV_TR - 8
    dp1 = jnp.concatenate([r1[:cut], jnp.where(row == 7, n1, r1[cut:])], axis=0)
    dp2 = jnp.concatenate([r2[:cut], jnp.where(row == 7, n2, jnp.where(row == 6, n1, r2[cut:]))], axis=0)
    return w_ref[2:3, :] * d + w_ref[1:2, :] * dp1 + w_ref[0:1, :] * dp2


def _convgate_bwd(u_a, u_g, w, b, dact, name):
    n_i = S // CONV_TR

    def body(ua, pa, ug, pg, wa, wg, ba, bg, d_ref, dua_ref, dug_ref, dwa_ref, dwg_ref, dba_ref, dbg_ref,
             nxt_a, nxt_g):
        i = pl.program_id(1)
        last = i == 0
        first = i == n_i - 1
        a0, a1, a2 = _causal_taps(ua, pa, first)
        ac = wa[0:1, :] * a2 + wa[1:2, :] * a1 + wa[2:3, :] * a0 + ba[...]
        g0, g1, g2 = _causal_taps(ug, pg, first)
        gc = wg[0:1, :] * g2 + wg[1:2, :] * g1 + wg[2:3, :] * g0 + bg[...]
        sg = 0.5 * jnp.tanh(0.5 * gc) + 0.5
        dact_v = d_ref[...].astype(F32)
        da = dact_v * (gc * sg)
        dg = dact_v * ac * (sg * (1.0 + gc * (1.0 - sg)))
        dua_ref[...] = _anticausal_conv(da, nxt_a, wa, last).astype(BF16)
        dug_ref[...] = _anticausal_conv(dg, nxt_g, wg, last).astype(BF16)
        nxt_a[...] = da[0:8]
        nxt_g[...] = dg[0:8]

        def col(v):
            return jnp.sum(v, axis=0, keepdims=True)

        parts = [col(da * a2), col(da * a1), col(da * a0), col(dg * g2), col(dg * g1), col(dg * g0),
                 col(da), col(dg)]

        @pl.when(last)
        def _():
            for k in range(3):
                dwa_ref[k:k + 1, :] = parts[k]
                dwg_ref[k:k + 1, :] = parts[3 + k]
            dba_ref[...] = parts[6]
            dbg_ref[...] = parts[7]

        @pl.when(i > 0)
        def _():
            for k in range(3):
                dwa_ref[k:k + 1, :] += parts[k]
                dwg_ref[k:k + 1, :] += parts[3 + k]
            dba_ref[...] += parts[6]
            dbg_ref[...] += parts[7]

    def swap(spec):
        return pl.BlockSpec(spec.block_shape, lambda j, i, f=spec.index_map: f(n_i - 1 - i, j))

    blk = pl.BlockSpec((CONV_TR, CONV_TC), lambda j, i: (n_i - 1 - i, j))
    w3 = pl.BlockSpec((3, CONV_TC), lambda j, i: (0, j))
    b1 = pl.BlockSpec((1, CONV_TC), lambda j, i: (0, j))
    return pl.pallas_call(
        body, name=name, grid=(CONV_NJ, n_i),
        out_shape=[jax.ShapeDtypeStruct((S, D_FF), BF16), jax.ShapeDtypeStruct((S, D_FF), BF16),
                   jax.ShapeDtypeStruct((3, D_FF), F32), jax.ShapeDtypeStruct((3, D_FF), F32),
                   jax.ShapeDtypeStruct((1, D_FF), F32), jax.ShapeDtypeStruct((1, D_FF), F32)],
        in_specs=[swap(s) for s in _conv_specs()] + [blk],
        out_specs=[blk, blk, w3, w3, b1, b1],
        scratch_shapes=[pltpu.VMEM((8, CONV_TC), F32), pltpu.VMEM((8, CONV_TC), F32)],
        compiler_params=_params("arbitrary", "arbitrary"),
    )(u_a, u_a, u_g, u_g, w, w, b, b, dact)


FOX_T = 512
FOX_TQ, FOX_TK = 512, 512
FOX_FORWARD_AT = 4
N_PAIRS = A_HEADS // 2


def _lane_masks():
    lane = lax.broadcasted_iota(jnp.int32, (1, LANES), 1)
    return lane, (lane < HEAD_DIM, lane >= HEAD_DIM)


def _fox_prep_fwd(z_t, b, name):
    def body(z_ref, b_ref, c_ref):
        r = lax.broadcasted_iota(jnp.int32, (LANES, LANES), 0)
        cc = lax.broadcasted_iota(jnp.int32, (LANES, LANES), 1)
        upper = (r <= cc).astype(BF16)
        carry = jnp.zeros((A_HEADS, 1), F32)
        for blk in range(S // LANES):
            sl = slice(blk * LANES, (blk + 1) * LANES)
            z = z_ref[:, sl] + b_ref[...]
            lf = jnp.minimum(z, 0.0) - jnp.log(1.0 + jnp.exp(-jnp.abs(z)))
            cs = _split_dot(lf, upper, 3) + carry
            c_ref[:, sl] = cs
            carry = cs[:, LANES - 1:LANES]

    return pl.pallas_call(
        body, name=name, out_shape=jax.ShapeDtypeStruct((A_HEADS, S), F32),
        compiler_params=_params(),
    )(z_t, b)


def _fox_prep_bwd(drow_t, dcol_t, z_t, b, name):
    def body(dr_ref, dc_ref, z_ref, b_ref, dz_ref, db_ref):
        r = lax.broadcasted_iota(jnp.int32, (LANES, LANES), 0)
        cc = lax.broadcasted_iota(jnp.int32, (LANES, LANES), 1)
        lower = (r >= cc).astype(BF16)
        carry = jnp.zeros((A_HEADS, 1), F32)
        db = jnp.zeros((A_HEADS, 1), F32)
        for blk in reversed(range(S // LANES)):
            sl = slice(blk * LANES, (blk + 1) * LANES)
            rc = _split_dot(dr_ref[:, sl] - dc_ref[:, sl], lower, 3) + carry
            carry = rc[:, 0:1]
            z = z_ref[:, sl] + b_ref[...]
            dz = rc / (1.0 + jnp.exp(z))
            dz_ref[:, sl] = dz
            db = db + jnp.sum(dz, axis=1, keepdims=True)
        db_ref[...] = db

    return pl.pallas_call(
        body, name=name,
        out_shape=[jax.ShapeDtypeStruct((A_HEADS, S), F32), jax.ShapeDtypeStruct((A_HEADS, 1), F32)],
        compiler_params=_params(),
    )(drow_t, dcol_t, z_t, b)


def _fox_fwd(qkv, c_t2, name, gather):
    tq, tk = FOX_TQ, FOX_TK

    n = len(gather)

    def body(*refs):
        q_ref, k_ref, v_ref, ct_ref = refs[:4]
        o_ref, lse_ref = refs[4 + n:6 + n]
        s_scr, p_scr, acc_scr = refs[-5:-3], refs[-3:-1], refs[-1]
        qi = pl.program_id(1)

        gather_start, gather_forward, gather_finish = _gather_phases(
            refs[4:4 + n], refs[6 + n:6 + 2 * n], refs[6 + 2 * n:len(refs) - 5])

        @pl.when(jnp.logical_and(pl.program_id(0) == 0, qi == 0))
        def _():
            gather_start()

        @pl.when(jnp.logical_and(pl.program_id(0) == FOX_FORWARD_AT, qi == 0))
        def _():
            gather_forward()

        n_full = jnp.right_shift(qi, (tk // tq).bit_length() - 1)
        lane, masks = _lane_masks()
        q = q_ref[...] * SCALE
        qs = [jnp.where(masks[e], q, jnp.zeros_like(q)) for e in range(2)]

        def scores_into(j, slot):
            start = pl.multiple_of(j * tk, tk)
            kb = k_ref[pl.ds(start, tk), :]
            for e in range(2):
                s_scr[slot][e] = _dot(qs[e], kb, NT_DIMS) - ct_ref[e:e + 1, pl.ds(start, tk)]

        def softmax_of(slot, m, masked):
            m_new, alpha = [], []
            for e in range(2):
                s = s_scr[slot][e]
                if masked:
                    rows = lax.broadcasted_iota(jnp.int32, (tq, tk), 0) + (qi * tq - n_full * tk)
                    cols = lax.broadcasted_iota(jnp.int32, (tq, tk), 1)
                    s = jnp.where(cols <= rows, s, NEG)
                m_new.append(jnp.maximum(m[e], jnp.max(s, axis=1, keepdims=True)))
                p_scr[slot][e] = jnp.exp(s - m_new[e]).astype(BF16)
                alpha.append(jnp.exp(m[e] - m_new[e]))
            return tuple(m_new), tuple(alpha)

        def values_of(j, slot, alpha):
            start = pl.multiple_of(j * tk, tk)
            vb = v_ref[pl.ds(start, tk), :]
            for e in range(2):
                acc_scr[e] = (alpha[e] * acc_scr[e]
                              + _dot(p_scr[slot][e], jnp.where(masks[e], vb, jnp.ones_like(vb))))

        def stage(j, cur, nxt, carry):
            m, a_prev = carry
            scores_into(j + 1, nxt)
            values_of(jnp.maximum(j - 1, 0), nxt, a_prev)
            return softmax_of(cur, m, False)

        def finish(cur, nxt, carry):
            m, a_prev = carry
            values_of(jnp.maximum(n_full - 1, 0), nxt, a_prev)
            (m0, m1), alpha = softmax_of(cur, m, True)
            values_of(n_full, cur, alpha)
            l0 = acc_scr[0][:, HEAD_DIM:HEAD_DIM + 1]
            l1 = acc_scr[1][:, 0:1]
            o_ref[...] = jnp.where(masks[0], acc_scr[0] / l0, acc_scr[1] / l1).astype(BF16)
            lse_ref[...] = jnp.where(masks[0], m0 + jnp.log(l0), m1 + jnp.log(l1))

        scores_into(0, 0)
        for e in range(2):
            p_scr[1][e] = jnp.zeros((tq, tk), BF16)
            acc_scr[e] = jnp.zeros((tq, LANES), F32)
        two = lambda x: (x, x)
        init = (two(jnp.full((tq, 1), NEG, F32)), two(jnp.ones((tq, 1), F32)))

        def two_stages(jj, carry):
            return stage(2 * jj + 1, 1, 0, stage(2 * jj, 0, 1, carry))

        carry = lax.fori_loop(0, jnp.right_shift(n_full, 1), two_stages, init)
        odd = jnp.bitwise_and(n_full, 1) == 1

        @pl.when(odd)
        def _():
            finish(1, 0, stage(n_full - 1, 0, 1, carry))

        @pl.when(jnp.logical_not(odd))
        def _():
            finish(0, 1, carry)

        @pl.when(jnp.logical_and(pl.program_id(0) == N_PAIRS - 1, qi == S // tq - 1))
        def _():
            gather_finish()

    qspec = pl.BlockSpec((tq, LANES), lambda h, i: (i, h))
    return pl.pallas_call(
        body, name=name, grid=(N_PAIRS, S // tq),
        out_shape=[jax.ShapeDtypeStruct((S, D), BF16), jax.ShapeDtypeStruct((S, D), F32)]
        + _exchange_shapes(gather, False),
        in_specs=[qspec,
                  pl.BlockSpec((S, LANES), lambda h, i: (0, N_PAIRS + h)),
                  pl.BlockSpec((S, LANES), lambda h, i: (0, 2 * N_PAIRS + h)),
                  pl.BlockSpec((None, 2, S), lambda h, i: (h, 0, 0))] + [ANY] * n,
        out_specs=[qspec, qspec] + [ANY] * n,
        scratch_shapes=_exchange_scratch(n) + [
            pltpu.VMEM((2, tq, tk), F32), pltpu.VMEM((2, tq, tk), F32),
            pltpu.VMEM((2, tq, tk), BF16), pltpu.VMEM((2, tq, tk), BF16),
            pltpu.VMEM((2, tq, LANES), F32)],
        compiler_params=_params("arbitrary", "arbitrary"),
    )(qkv, qkv, qkv, c_t2, *gather)


def _head_rowsum(a, b, name, tr=256):
    C = a.shape[1]

    def body(a_ref, b_ref, o_ref):
        r = lax.broadcasted_iota(jnp.int32, (LANES, LANES), 0) < HEAD_DIM
        cc = lax.broadcasted_iota(jnp.int32, (LANES, LANES), 1) < HEAD_DIM
        same_head = (r == cc).astype(BF16)
        for blk in range(C // LANES):
            sl = slice(blk * LANES, (blk + 1) * LANES)
            prod = a_ref[:, sl].astype(F32) * b_ref[:, sl].astype(F32)
            o_ref[:, sl] = _split_dot(prod, same_head, 2)

    row = pl.BlockSpec((tr, C), lambda i: (i, 0))
    return pl.pallas_call(
        body, name=name, grid=(S // tr,), out_shape=jax.ShapeDtypeStruct((S, C), F32),
        in_specs=[row, row], out_specs=row, compiler_params=_params("parallel"),
    )(a, b)


def _fox_bwd(qkv, do, lse, delta, c_t2, name, scatter):
    t = FOX_T
    nq = S // t

    n = len(scatter)

    def body(*refs):
        q_ref, k_ref, v_ref, do_ref, lse_ref, dl_ref, ct_ref = refs[:7]
        dq_ref, dk_ref, dv_ref, dcol_ref, drow_ref = refs[7 + n:12 + n]
        exchange = (refs[7:7 + n], refs[12 + n:12 + 2 * n], refs[12 + 2 * n:len(refs) - 5], True)
        sd_scr, pd_scr, acc_scr = refs[-5:-3], refs[-3:-1], refs[-1]
        kj = pl.program_id(1)

        @pl.when(jnp.logical_and(pl.program_id(0) == 0, kj == 0))
        def _():
            for cp in _exchange_copies(*exchange):
                cp.start()

        @pl.when(kj == 0)
        def _():
            dq_ref[...] = jnp.zeros_like(dq_ref)
            drow_ref[...] = jnp.zeros_like(drow_ref)

        lane, masks = _lane_masks()
        k = k_ref[...]
        v = v_ref[...]
        k_aug = [jnp.where(masks[e], k * SCALE, jnp.ones_like(k)) for e in range(2)]
        cs = [ct_ref[e:e + 1, :] for e in range(2)]

        def rows_of(i):
            r0 = pl.multiple_of(i * t, t)
            return pl.ds(r0, t), q_ref[pl.ds(r0, t), :] * SCALE, do_ref[pl.ds(r0, t), :]

        def scores_into(i, slot):
            _, qb, dob = rows_of(i)
            for e in range(2):
                qe = jnp.where(masks[e], qb, jnp.zeros_like(qb))
                doe = jnp.where(masks[e], dob, jnp.zeros_like(dob))
                sd_scr[slot][2 * e] = _dot(qe, k, NT_DIMS) - cs[e]
                sd_scr[slot][2 * e + 1] = _dot(doe, v, NT_DIMS)

        def pointwise(i, slot, masked):
            rows, _, _ = rows_of(i)
            for e in range(2):
                lo = e * HEAD_DIM
                s = sd_scr[slot][2 * e]
                if masked:
                    r = lax.broadcasted_iota(jnp.int32, (t, t), 0)
                    c = lax.broadcasted_iota(jnp.int32, (t, t), 1)
                    s = jnp.where(c <= r, s, NEG)
                p = jnp.exp(s - lse_ref[rows, lo:lo + 1])
                pd_scr[slot][2 * e] = p.astype(BF16)
                pd_scr[slot][2 * e + 1] = (p * (sd_scr[slot][2 * e + 1] - dl_ref[rows, lo:lo + 1])).astype(BF16)

        def accumulate(i, slot):
            rows, qb, dob = rows_of(i)
            dq_parts = []
            for e in range(2):
                p, ds = pd_scr[slot][2 * e], pd_scr[slot][2 * e + 1]
                q_aug = jnp.where(masks[e], qb, jnp.ones_like(qb))
                doe = jnp.where(masks[e], dob, jnp.zeros_like(dob))
                acc_scr[2] += _dot(p, doe, TN_DIMS)
                acc_scr[e] += _dot(ds, q_aug, TN_DIMS)
                dq_parts.append(_dot(ds, k_aug[e]))
            dq_ref[rows, :] += jnp.where(masks[0], dq_parts[0], dq_parts[1])
            drow_ref[rows, :] += jnp.where(masks[0], dq_parts[1], dq_parts[0])

        def stage(i, cur, nxt):
            scores_into(jnp.minimum(i + 1, nq - 1), nxt)
            accumulate(i - 1, nxt)
            pointwise(i, cur, False)

        acc_scr[...] = jnp.zeros_like(acc_scr)
        scores_into(kj, 0)
        pointwise(kj, 0, True)
        scores_into(jnp.minimum(kj + 1, nq - 1), 1)
        rest = nq - 1 - kj

        def two_stages(jj, carry):
            stage(kj + 1 + 2 * jj, 1, 0)
            stage(kj + 2 + 2 * jj, 0, 1)
            return carry

        lax.fori_loop(0, jnp.right_shift(rest, 1), two_stages, 0)
        odd = jnp.bitwise_and(rest, 1) == 1

        @pl.when(odd)
        def _():
            stage(nq - 1, 1, 0)
            accumulate(nq - 1, 1)

        @pl.when(jnp.logical_not(odd))
        def _():
            accumulate(nq - 1, 0)

        dk0, dk1, dv = acc_scr[0], acc_scr[1], acc_scr[2]
        dk_ref[...] = jnp.where(masks[0], dk0, dk1).astype(BF16)
        dcol_ref[...] = jnp.where(masks[0], dk1, dk0)
        dv_ref[...] = dv.astype(BF16)

        @pl.when(jnp.logical_and(pl.program_id(0) == N_PAIRS - 1, kj == nq - 1))
        def _():
            for cp in _exchange_copies(*exchange):
                cp.wait()

    full = lambda off: pl.BlockSpec((S, LANES), lambda h, j, off=off: (0, off + h))
    kv = lambda off: pl.BlockSpec((t, LANES), lambda h, j, off=off: (j, off + h))
    return pl.pallas_call(
        body, name=name, grid=(N_PAIRS, nq),
        out_shape=[jax.ShapeDtypeStruct((S, D), F32), jax.ShapeDtypeStruct((S, D), BF16),
                   jax.ShapeDtypeStruct((S, D), BF16), jax.ShapeDtypeStruct((S, D), F32),
                   jax.ShapeDtypeStruct((S, D), F32)] + _exchange_shapes(scatter, True),
        in_specs=[full(0), kv(N_PAIRS), kv(2 * N_PAIRS), full(0), full(0), full(0),
                  pl.BlockSpec((None, 2, t), lambda h, j: (h, 0, j))] + [ANY] * n,
        out_specs=[full(0), kv(0), kv(0), kv(0), full(0)] + [ANY] * n,
        scratch_shapes=_exchange_scratch(n) + [
            pltpu.VMEM((4, t, t), F32), pltpu.VMEM((4, t, t), F32),
            pltpu.VMEM((4, t, t), BF16), pltpu.VMEM((4, t, t), BF16),
            pltpu.VMEM((3, t, LANES), F32)],
        compiler_params=_params("arbitrary", "arbitrary"),
    )(qkv, qkv, qkv, do, lse, delta, c_t2, *scatter)


B_PAIRS = 4
B_NB = S // B_W


def _group_consts(g):
    nbs = jnp.where(g == 0, B_NB // B_DILS[0], jnp.where(g == 1, B_NB // B_DILS[1], B_NB // B_DILS[2]))
    dil = jnp.where(g == 0, B_DILS[0], jnp.where(g == 1, B_DILS[1], B_DILS[2]))
    return nbs, dil


def _band(dil):
    qi = lax.broadcasted_iota(jnp.int32, (B_W, B_W), 0)
    kj = lax.broadcasted_iota(jnp.int32, (B_W, B_W), 1)
    dist_c = qi - kj
    dist_p = qi + B_W - kj
    return (dist_c * dil).astype(F32), dist_c >= 0, (dist_p * dil).astype(F32), dist_p <= B_W


def _dil_fwd(qp, kp, vp, slopes, name):
    def body(sl_ref, q_ref, kp_ref, kc_ref, vp_ref, vc_ref, o_ref, lse_ref):
        g, n = pl.program_id(0), pl.program_id(1)
        nbs, dil = _group_consts(g)
        has_prev = (n % nbs) != 0
        lane, masks = _lane_masks()
        bias_c, ok_c, bias_p, ok_p = _band(dil)
        ok_p = jnp.logical_and(ok_p, has_prev)
        heads = [(hp, e) for hp in range(B_PAIRS) for e in range(2)]
        col = lambda ref, hp: ref[:, hp * LANES:(hp + 1) * LANES]
        logits = []
        for hp, e in heads:
            q = col(q_ref, hp) * SCALE
            qe = jnp.where(masks[e], q, jnp.zeros_like(q))
            logits.append((_dot(qe, col(kc_ref, hp), NT_DIMS), _dot(qe, col(kp_ref, hp), NT_DIMS)))
        probs = []
        for (hp, e), (sc, sp) in zip(heads, logits):
            slope = sl_ref[g * 8 + 2 * hp + e]
            sc = jnp.where(ok_c, sc - slope * bias_c, NEG)
            sp = jnp.where(ok_p, sp - slope * bias_p, NEG)
            m = jnp.maximum(jnp.max(sc, axis=1, keepdims=True), jnp.max(sp, axis=1, keepdims=True))
            probs.append((jnp.exp(sc - m).astype(BF16), jnp.exp(sp - m).astype(BF16), m))
        outs, lses = [], []
        for (hp, e), (pc, pp, m) in zip(heads, probs):
            vc, vpv = col(vc_ref, hp), col(vp_ref, hp)
            acc = (_dot(pc, jnp.where(masks[e], vc, jnp.ones_like(vc)))
                   + _dot(pp, jnp.where(masks[e], vpv, jnp.ones_like(vpv))))
            l = acc[:, HEAD_DIM:HEAD_DIM + 1] if e == 0 else acc[:, 0:1]
            outs.append(acc / l)
            lses.append(m + jnp.log(l))
        o_ref[...] = jnp.concatenate(
            [jnp.where(masks[0], outs[2 * hp], outs[2 * hp + 1]) for hp in range(B_PAIRS)], axis=1).astype(BF16)
        lse = jnp.zeros((B_W, LANES), F32)
        for h in range(2 * B_PAIRS):
            lse = jnp.where(lane == h, lses[h], lse)
        lse_ref[...] = lse

    cur = pl.BlockSpec((None, B_W, B_OUT), lambda g, n, sl: (g, n, 0))
    prev = pl.BlockSpec((None, B_W, B_OUT), lambda g, n, sl: (g, jnp.maximum(n - 1, 0), 0))
    stat = pl.BlockSpec((None, B_W, LANES), lambda g, n, sl: (g, n, 0))
    return pl.pallas_call(
        body, name=name,
        grid_spec=pltpu.PrefetchScalarGridSpec(
            num_scalar_prefetch=1, grid=(3, B_NB),
            in_specs=[cur, prev, cur, prev, cur], out_specs=[cur, stat]),
        out_shape=[jax.ShapeDtypeStruct((3, S, B_OUT), BF16), jax.ShapeDtypeStruct((3, S, LANES), F32)],
        compiler_params=_params("parallel", "parallel"),
    )(slopes, qp, kp, kp, vp, vp)


def _head_expander():
    r = lax.broadcasted_iota(jnp.int32, (LANES, B_OUT), 0)
    c = lax.broadcasted_iota(jnp.int32, (LANES, B_OUT), 1)
    return jnp.logical_and(c >= r * HEAD_DIM, c < (r + 1) * HEAD_DIM).astype(BF16)


def _dil_merge(og, lseg, name, tr=256):
    def body(o_ref, l_ref, out_ref, lse_ref):
        l0, l1, l2 = l_ref[0], l_ref[1], l_ref[2]
        m = jnp.maximum(jnp.maximum(l0, l1), l2)
        w0, w1, w2 = jnp.exp(l0 - m), jnp.exp(l1 - m), jnp.exp(l2 - m)
        den = w0 + w1 + w2
        lse_ref[...] = m + jnp.log(den)
        expand = _head_expander()
        out = None
        for g, w in enumerate((w0, w1, w2)):
            part = _split_dot(w / den, expand, 3) * o_ref[g].astype(F32)
            out = part if out is None else out + part
        out_ref[...] = out.astype(BF16)

    blk3 = pl.BlockSpec((3, tr, B_OUT), lambda i: (0, i, 0))
    stat3 = pl.BlockSpec((3, tr, LANES), lambda i: (0, i, 0))
    blk = pl.BlockSpec((tr, B_OUT), lambda i: (i, 0))
    stat = pl.BlockSpec((tr, LANES), lambda i: (i, 0))
    return pl.pallas_call(
        body, name=name, grid=(S // tr,),
        out_shape=[jax.ShapeDtypeStruct((S, B_OUT), BF16), jax.ShapeDtypeStruct((S, LANES), F32)],
        in_specs=[blk3, stat3], out_specs=[blk, stat], compiler_params=_params("parallel"),
    )(og, lseg)


def _head_rowsum_compact(a, b, name, tr=256):
    def body(a_ref, b_ref, o_ref):
        r = lax.broadcasted_iota(jnp.int32, (B_OUT, LANES), 0)
        c = lax.broadcasted_iota(jnp.int32, (B_OUT, LANES), 1)
        collect = jnp.logical_and(r >= c * HEAD_DIM, r < (c + 1) * HEAD_DIM).astype(BF16)
        prod = a_ref[...].astype(F32) * b_ref[...].astype(F32)
        o_ref[...] = _split_dot(prod, collect, 2)

    row = pl.BlockSpec((tr, B_OUT), lambda i: (i, 0))
    return pl.pallas_call(
        body, name=name, grid=(S // tr,), out_shape=jax.ShapeDtypeStruct((S, LANES), F32),
        in_specs=[row, row], out_specs=pl.BlockSpec((tr, LANES), lambda i: (i, 0)),
        compiler_params=_params("parallel"),
    )(a, b)


def _dil_bwd(qp, kp, vp, dop, lsep, dlp, slopes, name, scatter):
    n_ex = len(scatter)

    def body(sl_ref, *refs):
        (qc_ref, qn_ref, kp_ref, kc_ref, vp_ref, vc_ref, doc_ref, don_ref,
         lc_ref, ln_ref, dc_ref, dn_ref) = refs[:12]
        dq_ref, dk_ref, dv_ref = refs[12 + n_ex:15 + n_ex]
        exchange = (refs[12:12 + n_ex], refs[15 + n_ex:15 + 2 * n_ex], refs[15 + 2 * n_ex:], True)
        g, n = pl.program_id(0), pl.program_id(1)

        @pl.when(jnp.logical_and(g == 0, n == 0))
        def _():
            for cp in _exchange_copies(*exchange):
                cp.start()

        nbs, dil = _group_consts(g)
        has_prev = (n % nbs) != 0
        has_next = jnp.logical_and(n + 1 < B_NB, ((n + 1) % nbs) != 0)
        lane, masks = _lane_masks()
        bias_c, ok_c, bias_p, ok_p = _band(dil)
        ok_pp = jnp.logical_and(ok_p, has_prev)
        ok_np = jnp.logical_and(ok_p, has_next)
        heads = [(hp, e) for hp in range(B_PAIRS) for e in range(2)]
        col = lambda ref, hp: ref[:, hp * LANES:(hp + 1) * LANES]
        mask = lambda t, e: jnp.where(masks[e], t, jnp.zeros_like(t))
        raw = []
        for hp, e in heads:
            qce, qne = mask(col(qc_ref, hp) * SCALE, e), mask(col(qn_ref, hp) * SCALE, e)
            doce, done = mask(col(doc_ref, hp), e), mask(col(don_ref, hp), e)
            kc, kpv, vc, vpv = col(kc_ref, hp), col(kp_ref, hp), col(vc_ref, hp), col(vp_ref, hp)
            raw.append(((_dot(qce, kc, NT_DIMS), _dot(doce, vc, NT_DIMS)),
                        (_dot(qce, kpv, NT_DIMS), _dot(doce, vpv, NT_DIMS)),
                        (_dot(qne, kc, NT_DIMS), _dot(done, vc, NT_DIMS))))
        pds = []
        for (hp, e), tiles in zip(heads, raw):
            lo = 2 * hp + e
            slope = sl_ref[g * 8 + 2 * hp + e]
            lse_c, dl_c = lc_ref[:, lo:lo + 1], dc_ref[:, lo:lo + 1]
            lse_n, dl_n = ln_ref[:, lo:lo + 1], dn_ref[:, lo:lo + 1]
            out = []
            for (s, dp), ok, bias, lse, dl in ((tiles[0], ok_c, bias_c, lse_c, dl_c),
                                               (tiles[1], ok_pp, bias_p, lse_c, dl_c),
                                               (tiles[2], ok_np, bias_p, lse_n, dl_n)):
                p = jnp.exp(jnp.where(ok, s - slope * bias, NEG) - lse)
                out.append((p.astype(BF16), (p * (dp - dl)).astype(BF16)))
            pds.append(out)
        dq_all, dk_all, dv_all = [], [], []
        for hp in range(B_PAIRS):
            dq = jnp.zeros((B_W, LANES), F32)
            dk = jnp.zeros((B_W, LANES), F32)
            dv = jnp.zeros((B_W, LANES), F32)
            for e in range(2):
                (p_c, ds_c), (_, ds_p), (p_n, ds_n) = pds[2 * hp + e]
                qce, qne = mask(col(qc_ref, hp) * SCALE, e), mask(col(qn_ref, hp) * SCALE, e)
                doce, done = mask(col(doc_ref, hp), e), mask(col(don_ref, hp), e)
                dq = dq + _dot(ds_c, mask(col(kc_ref, hp) * SCALE, e)) + _dot(ds_p, mask(col(kp_ref, hp) * SCALE, e))
                dk = dk + _dot(ds_c, qce, TN_DIMS) + _dot(ds_n, qne, TN_DIMS)
                dv = dv + _dot(p_c, doce, TN_DIMS) + _dot(p_n, done, TN_DIMS)
            dq_all.append(dq)
            dk_all.append(dk)
            dv_all.append(dv)
        dq_ref[...] = jnp.concatenate(dq_all, axis=1).astype(BF16)
        dk_ref[...] = jnp.concatenate(dk_all, axis=1).astype(BF16)
        dv_ref[...] = jnp.concatenate(dv_all, axis=1).astype(BF16)

        @pl.when(jnp.logical_and(g == 2, n == B_NB - 1))
        def _():
            for cp in _exchange_copies(*exchange):
                cp.wait()

    cur = pl.BlockSpec((None, B_W, B_OUT), lambda g, n, sl: (g, n, 0))
    prev = pl.BlockSpec((None, B_W, B_OUT), lambda g, n, sl: (g, jnp.maximum(n - 1, 0), 0))
    nxt = pl.BlockSpec((None, B_W, B_OUT), lambda g, n, sl: (g, jnp.minimum(n + 1, B_NB - 1), 0))
    stat_cur = pl.BlockSpec((None, B_W, LANES), lambda g, n, sl: (g, n, 0))
    stat_nxt = pl.BlockSpec((None, B_W, LANES), lambda g, n, sl: (g, jnp.minimum(n + 1, B_NB - 1), 0))
    return pl.pallas_call(
        body, name=name,
        grid_spec=pltpu.PrefetchScalarGridSpec(
            num_scalar_prefetch=1, grid=(3, B_NB),
            in_specs=[cur, nxt, prev, cur, prev, cur, cur, nxt, stat_cur, stat_nxt, stat_cur, stat_nxt]
            + [ANY] * n_ex,
            out_specs=[cur, cur, cur] + [ANY] * n_ex,
            scratch_shapes=_exchange_scratch(n_ex)),
        out_shape=[jax.ShapeDtypeStruct((3, S, B_OUT), BF16)] * 3 + _exchange_shapes(scatter, True),
        compiler_params=_params("arbitrary", "arbitrary"),
    )(slopes, qp, qp, kp, kp, vp, vp, dop, dop, lsep, lsep, dlp, dlp, *scatter)


def _rows_block(shape, max_bytes=2 * 1024 * 1024):
    rows, cols = shape
    padded_cols = -(-cols // LANES) * LANES
    for tr in (1024, 512, 256, 128, 64, 32, 16):
        if rows % tr == 0 and tr * padded_cols * 4 <= max_bytes:
            return tr
    return rows


def _adam_update(w, m, v, g):
    m_new = ADAM_B1 * m + (1.0 - ADAM_B1) * g
    v_new = ADAM_B2 * v + (1.0 - ADAM_B2) * (g * g)
    m_hat = m_new / (1.0 - ADAM_B1 ** ADAM_STEP)
    v_hat = v_new / (1.0 - ADAM_B2 ** ADAM_STEP)
    delta = -ADAM_LR * (m_hat / (jnp.sqrt(v_hat) + ADAM_EPS) + ADAM_WD * w)
    return delta, m_new, v_new


def _adamw_sharded(w, m, v, parts, name):
    R, C = w.shape
    tr = _rows_block((R, C), max_bytes=1024 * 1024)

    def body(w_ref, m_ref, v_ref, p_ref, g_ref, d_ref, mo_ref, vo_ref):
        g = p_ref[0].astype(F32)
        for dev in range(1, N_DEV):
            g = g + p_ref[dev].astype(F32)
        g_ref[...] = g
        d_ref[...], mo_ref[...], vo_ref[...] = _adam_update(w_ref[...], m_ref[...], v_ref[...], g)

    blk = pl.BlockSpec((tr, C), lambda i: (i, 0))
    out = jax.ShapeDtypeStruct((R, C), F32)
    return pl.pallas_call(
        body, name=name, grid=(R // tr,),
        in_specs=[blk, blk, blk, pl.BlockSpec((N_DEV, tr, C), lambda i: (0, i, 0))],
        out_specs=[blk, blk, blk, blk], out_shape=[out, out, out, out],
        compiler_params=_params("parallel"),
    )(w, m, v, parts)


def _adamw_replicated(w, m, v, parts, name):
    def body(w_ref, m_ref, v_ref, p_ref, g_ref, d_ref, mo_ref, vo_ref):
        g = p_ref[0]
        for dev in range(1, N_DEV):
            g = g + p_ref[dev]
        g_ref[...] = g
        d_ref[...], mo_ref[...], vo_ref[...] = _adam_update(w_ref[...], m_ref[...], v_ref[...], g)

    out = jax.ShapeDtypeStruct(w.shape, F32)
    return pl.pallas_call(body, name=name, out_shape=[out, out, out, out], compiler_params=_params())(w, m, v, parts)


def _cols_from_slots(g):
    return g.transpose(1, 0, 2).reshape(g.shape[1], N_DEV * g.shape[2])


def _cols_to_slots(w):
    k, n = w.shape
    return w.reshape(k, N_DEV, n // N_DEV).transpose(1, 0, 2)


def _permute(t, dil):
    c = t.shape[1]
    return t.reshape(S // dil, dil, c).transpose(1, 0, 2).reshape(S, c)


def _unpermute(t, dil):
    c = t.shape[1]
    return t.reshape(dil, S // dil, c).transpose(1, 0, 2).reshape(S, c)


def _group_permute(t):
    return jnp.stack([_permute(t[:, g * B_OUT:(g + 1) * B_OUT], B_DILS[g]) for g in range(3)])


def _same_permute(t):
    return jnp.stack([_permute(t, d) for d in B_DILS])


def _group_unpermute(t):
    return jnp.stack([_unpermute(t[g], B_DILS[g]) for g in range(3)])


SMALL_ROWS = 144


LOSS_SLOT = A_HEADS + 6 * D + 4 * D_FF


def _pack_small(a_b_f, kv_g, mix_g, ffn_g, conv_b, fin_g, loss=None):
    parts = [a_b_f.reshape(-1), kv_g.reshape(-1), mix_g.reshape(-1), ffn_g.reshape(-1),
             conv_b.reshape(-1), fin_g.reshape(-1)] + ([loss.reshape(-1)] if loss is not None else [])
    flat = jnp.concatenate(parts)
    return jnp.pad(flat, (0, SMALL_ROWS * LANES - flat.shape[0])).reshape(SMALL_ROWS, LANES)


def _unpack_small(p):
    flat = p.reshape(-1)
    out, off = [], 0
    for shape in ((1, A_HEADS), (D,), (2, D), (2, D), (2, 2 * D_FF), (D,)):
        size = math.prod(shape)
        out.append(flat[off:off + size].reshape(shape))
        off += size
    return out


def _unpack_late(g):
    half = N_DEV // 2
    up = g[4].reshape(N_DEV, 2, D, -1)
    w_up_a = [up[:half, l].transpose(1, 0, 2).reshape(D, D_FF) for l in range(2)]
    w_up_g = [up[half:, l].transpose(1, 0, 2).reshape(D, D_FF) for l in range(2)]
    w_down = [g[5].reshape(N_DEV, 2, -1, D)[:, l].reshape(D_FF, D) for l in range(2)]
    conv_w = [g[6].reshape(N_DEV, 2, 3, -1)[:, l].transpose(1, 0, 2).reshape(3, 2 * D_FF) for l in range(2)]
    return (g[0].reshape(D, D), _cols_from_slots(g[1]), _cols_from_slots(g[2]), _cols_from_slots(g[3]),
            w_up_a, w_up_g, w_down, conv_w)


def _ffn_slots(dw_up, dw_down_t, dconv_w):
    return [_cols_to_slots(dw_up), dw_down_t.reshape(D, N_DEV, -1).transpose(1, 2, 0), _cols_to_slots(dconv_w)]


def _local_step(x0, target, w_in_pad, late_shards,
                a_b_f, kv_norm_g, mix_norm_g, ffn_norm_g, ffn_conv_b, final_norm_g):
    w_qkv, w_f = w_in_pad[:, :A_QKV], w_in_pad[:, A_QKV:]
    conv_b = ffn_conv_b.reshape(2, 1, 2 * D_FF)
    slopes = jnp.exp2(-8.0 * jnp.arange(1, 25, dtype=F32) / 24)

    def gain(g):
        return g.reshape(1, D)

    (h1,) = _rmsnorm_fwd(x0, [gain(mix_norm_g[0])], "norm_mix0")
    qkv = _matmul(h1, w_qkv, mode="nn", out_dtype=BF16, name="proj_qkv", tm=512, tn=A_QKV)
    z = _matmul(h1, w_f, mode="nn", out_dtype=F32, name="proj_gate", tm=S, tn=LANES)
    z_t = z[:, :A_HEADS].T
    b_f = a_b_f.reshape(A_HEADS, 1)
    c_t = _fox_prep_fwd(z_t, b_f, "fox_prep")
    c_t2 = c_t.reshape(N_PAIRS, 2, S)
    o_a, lse_a, *late = _fox_fwd(qkv, c_t2, "fox_fwd", late_shards)
    w_out, w_q, w_bo, w_kvf, w_up_a, w_up_g, w_down, conv_w = _unpack_late(late)
    x1 = _matmul(o_a, w_out, mode="nn", out_dtype=F32, name="a_out", tm=512, tn=D, res=x0)

    def ffn_fwd(xin, layer):
        (h,) = _rmsnorm_fwd(xin, [gain(ffn_norm_g[layer])], f"norm_ffn{layer}")
        u = (_matmul(h, w_up_a[layer], mode="nn", out_dtype=BF16, name=f"ffn_up_a{layer}", tm=512, tn=D_FF),
             _matmul(h, w_up_g[layer], mode="nn", out_dtype=BF16, name=f"ffn_up_g{layer}", tm=512, tn=D_FF))
        act = _convgate_fwd(*u, conv_w[layer], conv_b[layer], f"convgate{layer}")
        xout = _matmul(act, w_down[layer], mode="nn", out_dtype=F32, name=f"ffn_down{layer}", tm=512, tn=D, res=xin)
        return h, u, act, xout

    h2, u0, act0, x2 = ffn_fwd(x1, 0)
    hk, h3 = _rmsnorm_fwd(x2, [gain(kv_norm_g), gain(mix_norm_g[1])], "norm_kv_mix1")
    kv = _matmul(hk, w_kvf, mode="nn", out_dtype=BF16, name="proj_kv", tm=512, tn=B_KV)
    qb = _matmul(h3, w_q, mode="nn", out_dtype=BF16, name="proj_qb", tm=512, tn=B_Q)
    qp, kp, vp = _group_permute(qb), _group_permute(kv[:, :B_Q]), _group_permute(kv[:, B_Q:])
    og_p, lseg_p = _dil_fwd(qp, kp, vp, slopes, "dil_fwd")
    o_b, lse_b = _dil_merge(_group_unpermute(og_p), _group_unpermute(lseg_p), "dil_merge")
    x3 = _matmul(o_b, w_bo, mode="nn", out_dtype=F32, name="b_out", tm=512, tn=D, res=x2)
    h4, u1, act1, x4 = ffn_fwd(x3, 1)
    loss_blk, dx4, dx4b, dg_final = _final_loss(x4, target, gain(final_norm_g), "final_loss")

    def ffn_bwd(dx, dxb, xin, h, u, act, layer):
        dact = _matmul(dxb, w_down[layer], mode="nt", out_dtype=BF16, name=f"d_act{layer}", tm=512, tn=D_FF)
        dw_down = _matmul_tn(dxb, act, out_dtype=BF16, name=f"dw_down_t{layer}")
        du_a, du_g, dwa, dwg, dba, dbg = _convgate_bwd(*u, conv_w[layer], conv_b[layer], dact, f"convgate_bwd{layer}")
        dw_up = jnp.concatenate(
            [_matmul_tn(h, du_a, out_dtype=BF16, name=f"dw_up_a{layer}"),
             _matmul_tn(h, du_g, out_dtype=BF16, name=f"dw_up_g{layer}")], axis=1)
        dxin, dxinb, dgain = _matmul_norm_bwd([(du_a, w_up_a[layer]), (du_g, w_up_g[layer])], xin,
                                              gain(ffn_norm_g[layer]), dx, f"dh_ffn_norm_bwd{layer}")
        dconv_w = jnp.concatenate([dwa, dwg], axis=1)
        dconv_b = jnp.concatenate([dba, dbg], axis=1)
        return dxin, dxinb, dgain, dw_up, dw_down, dconv_w, dconv_b

    dx3, dx3b, dg_ffn1, dw_up1, dw_down1, dconv_w1, dconv_b1 = ffn_bwd(dx4, dx4b, x3, h4, u1, act1, 1)

    do_b = _matmul(dx3b, w_bo, mode="nt", out_dtype=BF16, name="d_ob", tm=1024, tn=B_OUT)
    dw_bo = _matmul_tn(o_b, dx3b, out_dtype=BF16, name="dw_bo")
    dl_b = _head_rowsum_compact(do_b, o_b, "delta_b")
    slots_up1, slots_down1, slots_conv1 = _ffn_slots(dw_up1, dw_down1, dconv_w1)
    dqp, dkp, dvp, land_down1, land_conv1 = _dil_bwd(
        qp, kp, vp, _same_permute(do_b), _same_permute(lse_b), _same_permute(dl_b), slopes, "dil_bwd",
        [slots_down1, slots_conv1])

    def natural(tp):
        return jnp.concatenate([_unpermute(tp[g], B_DILS[g]) for g in range(3)], axis=1)

    dqb = natural(dqp)
    dkv = jnp.concatenate([natural(dkp), natural(dvp)], axis=1)
    dw_q = _matmul_tn(h3, dqb, out_dtype=BF16, name="dw_q")
    dw_kv = _matmul_tn(hk, dkv, out_dtype=BF16, name="dw_kv")
    dx2, _, dg_mix1 = _matmul_norm_bwd([(dqb, w_q)], x2, gain(mix_norm_g[1]), dx3, "dh_mix1_norm_bwd")
    dx2, dx2b, dg_kv = _matmul_norm_bwd([(dkv, w_kvf)], x2, gain(kv_norm_g), dx2, "dh_kv_norm_bwd")

    dx1, dx1b, dg_ffn0, dw_up0, dw_down0, dconv_w0, dconv_b0 = ffn_bwd(dx2, dx2b, x1, h2, u0, act0, 0)

    do_a = _matmul(dx1b, w_out, mode="nt", out_dtype=BF16, name="d_oa", tm=512, tn=D)
    dw_out = _matmul_tn(o_a, dx1b, out_dtype=BF16, name="dw_out")
    dl_a = _head_rowsum(do_a, o_a, "delta_a")
    dq_a, dk_a, dv_a, dcol, drow, *land = _fox_bwd(
        qkv, do_a, lse_a, dl_a, c_t2, "fox_bwd",
        [dw_out.reshape(N_DEV, D // N_DEV, D), _cols_to_slots(dw_q), _cols_to_slots(dw_bo), _cols_to_slots(dw_kv)]
        + _ffn_slots(dw_up0, dw_down0, dconv_w0) + [slots_up1])
    land_out, land_q, land_bo, land_kv, land_up0, land_down0, land_conv0, land_up1 = land

    def head_sums(t):
        return t.reshape(S, N_PAIRS, 2, HEAD_DIM)[:, :, ::-1, 0].reshape(S, A_HEADS).T

    dz_t, db_f = _fox_prep_bwd(head_sums(drow), head_sums(dcol), z_t, b_f, "fox_prep_bwd")
    dz = jnp.pad(dz_t.T, ((0, 0), (0, LANES - A_HEADS))).astype(BF16)
    dproj = jnp.concatenate([dq_a.astype(BF16), dk_a, dv_a, dz], axis=1)
    dw_in = _matmul_tn(h1, dproj, out_dtype=BF16, name="dw_in")
    grad_x, _, dg_mix0, land_in = _matmul_norm_bwd(
        [(dproj, w_in_pad)], x0, gain(mix_norm_g[0]), dx1, "dh_mix0_norm_bwd",
        scatter=[_cols_to_slots(dw_in[:, :A_QKV + A_HEADS])])

    dg_mix = jnp.concatenate([dg_mix0, dg_mix1], axis=0)
    dg_ffn = jnp.concatenate([dg_ffn0, dg_ffn1], axis=0)
    dconv_b = jnp.concatenate([dconv_b0, dconv_b1], axis=0)
    small_part = _pack_small(db_f, dg_kv, dg_mix, dg_ffn, dconv_b, dg_final, loss=loss_blk[0, 0])
    _, (small_parts,) = _final_exchange([], [small_part], "gather_small_grads")
    landed = [land_in, land_out, land_q, land_bo, land_kv, land_up0, land_up1, land_down0, land_down1,
              land_conv0, land_conv1]
    return loss_blk, grad_x, landed, small_parts


def kernel(x, a_w_in, a_b_f, a_w_out, b_w_q, b_w_out, kv_norm_g, w_kv, mix_norm_g, ffn_norm_g, ffn_w_up, ffn_conv_w, ffn_conv_b, ffn_w_down, final_norm_g, loss_target, m_a_w_in, m_a_b_f, m_a_w_out, m_b_w_q, m_b_w_out, m_kv_norm_g, m_w_kv, m_mix_norm_g, m_ffn_norm_g, m_ffn_w_up, m_ffn_conv_w, m_ffn_conv_b, m_ffn_w_down, m_final_norm_g, v_a_w_in, v_a_b_f, v_a_w_out, v_b_w_q, v_b_w_out, v_kv_norm_g, v_w_kv, v_mix_norm_g, v_ffn_norm_g, v_ffn_w_up, v_ffn_conv_w, v_ffn_conv_b, v_ffn_w_down, v_final_norm_g):
    def shards(a_w_in, a_w_out, b_w_q, b_w_out, w_kv, ffn_w_up, ffn_w_down, ffn_conv_w):
        return [a_w_in[0], a_w_out[0], b_w_q[0], b_w_out[0], w_kv, ffn_w_up[0], ffn_w_up[1],
                ffn_w_down[0], ffn_w_down[1], ffn_conv_w[0], ffn_conv_w[1]]

    w_loc = shards(a_w_in, a_w_out, b_w_q, b_w_out, w_kv, ffn_w_up, ffn_w_down, ffn_conv_w)
    m_loc = shards(m_a_w_in, m_a_w_out, m_b_w_q, m_b_w_out, m_w_kv, m_ffn_w_up, m_ffn_w_down, m_ffn_conv_w)
    v_loc = shards(v_a_w_in, v_a_w_out, v_b_w_q, v_b_w_out, v_w_kv, v_ffn_w_up, v_ffn_w_down, v_ffn_conv_w)

    (g_in,) = _all_gather([a_w_in[0].astype(BF16)], "gather_a_w_in")
    w_in = _cols_from_slots(g_in)
    w_in_pad = jnp.pad(w_in, ((0, 0), (0, A_PROJ_PAD - w_in.shape[1])))
    late_shards = [a_w_out[0].astype(BF16), b_w_q[0].astype(BF16), b_w_out[0].astype(BF16), w_kv.astype(BF16),
                   ffn_w_up.reshape(2 * D, -1).astype(BF16), ffn_w_down.reshape(-1, D).astype(BF16),
                   ffn_conv_w.reshape(6, -1)]

    loss_blk, grad_x, landed, small_parts = _local_step(
        x[0], loss_target[0], w_in_pad, late_shards,
        a_b_f, kv_norm_g, mix_norm_g, ffn_norm_g, ffn_conv_b, final_norm_g)

    big = [_adamw_sharded(w_loc[k], m_loc[k], v_loc[k], landed[k], f"adamw{k}") for k in range(11)]

    small = _adamw_replicated(
        _pack_small(a_b_f, kv_norm_g, mix_norm_g, ffn_norm_g, ffn_conv_b, final_norm_g),
        _pack_small(m_a_b_f, m_kv_norm_g, m_mix_norm_g, m_ffn_norm_g, m_ffn_conv_b, m_final_norm_g),
        _pack_small(v_a_b_f, v_kv_norm_g, v_mix_norm_g, v_ffn_norm_g, v_ffn_conv_b, v_final_norm_g),
        small_parts, "adamw_small")

    loss = small[0].reshape(-1)[LOSS_SLOT]

    def assemble(kind):
        b = [r[kind] for r in big]
        s_abf, s_kv, s_mix, s_ffn, s_cb, s_fin = _unpack_small(small[kind])
        return [b[0][None], s_abf, b[1][None], b[2][None], b[3][None], s_kv, b[4], s_mix, s_ffn,
                jnp.stack([b[5], b[6]]), jnp.stack([b[9], b[10]]), s_cb, jnp.stack([b[7], b[8]]), s_fin]

    return (loss, grad_x[None], *assemble(0), *assemble(1), *assemble(2), *assemble(3))
```

```python
import math

import jax
import jax.numpy as jnp
from jax import lax
from jax.experimental import pallas as pl
from jax.experimental.pallas import tpu as pltpu

F32 = jnp.float32
BF16 = jnp.bfloat16

S = 4096
D = 1024
N_DEV = 8
A_HEADS = 16
HEAD_DIM = 64
A_QKV = 3072
A_PROJ_PAD = 3200
B_Q = 1536
B_OUT = 512
B_KV = 3072
B_W = 128
B_DILS = (1, 4, 16)
D_FF = 2816
RMS_EPS = 1e-6
SCALE = HEAD_DIM ** -0.5
NEG = -1e30

ADAM_LR = 0.001
ADAM_B1 = 0.9
ADAM_B2 = 0.999
ADAM_EPS = 1e-08
ADAM_WD = 0.01
ADAM_STEP = 10

LANES = 128
VMEM_LIMIT = 56 * 1024 * 1024
MESH = pl.DeviceIdType.MESH
ANY = pl.BlockSpec(memory_space=pl.ANY)

NT_DIMS = (((1,), (1,)), ((), ()))
TN_DIMS = (((0,), (0,)), ((), ()))
NN_DIMS = (((1,), (0,)), ((), ()))


def _params(*sem):
    return pltpu.CompilerParams(dimension_semantics=sem if sem else None, vmem_limit_bytes=VMEM_LIMIT)


def _dot(a, b, dims=NN_DIMS):
    return lax.dot_general(a, b, dims, preferred_element_type=F32)


def _split_dot(x, mat, pieces):
    out = None
    rem = x
    for _ in range(pieces):
        part = rem.astype(BF16)
        rem = rem - part.astype(F32)
        d = _dot(part, mat)
        out = d if out is None else out + d
    return out


def _pick(n, prefs):
    for p in prefs:
        if n % p == 0:
            return p
    return n


def _gather_phases(ins, outs, sems):
    n = len(ins)
    if n == 0:
        return (lambda: None,) * 3
    send_sems, recv_sems, local_sems = sems
    x, y, c = lax.axis_index("x"), lax.axis_index("y"), lax.axis_index("c")
    me, sibling = (x, y, c), (x, y, 1 - c)
    chips = [(1 - x, y), (x, 1 - y), (1 - x, 1 - y)]

    def slot(a, px, py, pc):
        return outs[a].at[4 * px + 2 * py + pc]

    def copy(a, k, block, to, src=None):
        return pltpu.make_async_remote_copy(
            src_ref=slot(a, *block) if src is None else src, dst_ref=slot(a, *block),
            send_sem=send_sems.at[a, k], recv_sem=recv_sems.at[a, k],
            device_id=to, device_id_type=MESH)

    mine = [pltpu.make_async_copy(ins[a], slot(a, *me), local_sems.at[a]) for a in range(n)]
    first = []
    for a in range(n):
        first.append(copy(a, 0, me, sibling, src=ins[a]))
        first += [copy(a, 1 + j, me, (*chip, c), src=ins[a]) for j, chip in enumerate(chips)]
    passed = [copy(a, 4 + j, (*chip, c), sibling) for j, chip in enumerate(chips) for a in range(n)]

    def start():
        for cp in mine + first:
            cp.start()

    def forward():
        k = 0
        for j, chip in enumerate(chips):
            for a in range(n):
                copy(a, 1 + j, (*chip, c), me).wait_recv()
                passed[k].start()
                k += 1

    def finish():
        for a in range(n):
            copy(a, 0, sibling, me).wait_recv()
            for j, chip in enumerate(chips):
                copy(a, 4 + j, (*chip, 1 - c), me).wait_recv()
        for cp in first + passed:
            cp.wait_send()
        for cp in mine:
            cp.wait()

    return start, forward, finish


def _all_gather(arrays, name):
    n = len(arrays)

    def body(*refs):
        for phase in _gather_phases(refs[:n], refs[n:2 * n], refs[2 * n:]):
            phase()

    return pl.pallas_call(
        body, name=name,
        out_shape=[jax.ShapeDtypeStruct((N_DEV,) + a.shape, a.dtype) for a in arrays],
        in_specs=[ANY] * n, out_specs=[ANY] * n,
        scratch_shapes=[pltpu.SemaphoreType.DMA((n, 7)), pltpu.SemaphoreType.DMA((n, 7)),
                        pltpu.SemaphoreType.DMA((n,))],
    )(*arrays)


PEER_FLIPS = [(dx, dy, dc) for dx in (0, 1) for dy in (0, 1) for dc in (0, 1) if (dx, dy, dc) != (0, 0, 0)]


def _exchange_copies(ins, outs, sems, scatter):
    if not ins:
        return []
    send_sems, recv_sems, local_sems = sems
    x, y, c = lax.axis_index("x"), lax.axis_index("y"), lax.axis_index("c")
    me = 4 * x + 2 * y + c
    copies = []
    for a in range(len(ins)):
        copies.append(pltpu.make_async_copy(ins[a].at[me] if scatter else ins[a], outs[a].at[me], local_sems.at[a]))
        for k, (dx, dy, dc) in enumerate(PEER_FLIPS):
            px, py, pc = (1 - x if dx else x), (1 - y if dy else y), (1 - c if dc else c)
            copies.append(pltpu.make_async_remote_copy(
                src_ref=ins[a].at[4 * px + 2 * py + pc] if scatter else ins[a], dst_ref=outs[a].at[me],
                send_sem=send_sems.at[a, k], recv_sem=recv_sems.at[a, k],
                device_id=(px, py, pc), device_id_type=MESH))
    return copies


def _exchange_scratch(n):
    if n == 0:
        return []
    return [pltpu.SemaphoreType.DMA((n, 7)), pltpu.SemaphoreType.DMA((n, 7)), pltpu.SemaphoreType.DMA((n,))]


def _exchange_shapes(arrays, scatter):
    return [jax.ShapeDtypeStruct((N_DEV,) + (a.shape[1:] if scatter else a.shape), a.dtype) for a in arrays]


def _final_exchange(scatter, gather, name):
    ns, ng = len(scatter), len(gather)

    def body(*refs):
        ins, outs, sems = refs[:ns + ng], refs[ns + ng:2 * (ns + ng)], refs[2 * (ns + ng):]
        n_sems = len(_exchange_scratch(ns))
        copies = (_exchange_copies(ins[:ns], outs[:ns], sems[:n_sems], True)
                  + _exchange_copies(ins[ns:], outs[ns:], sems[n_sems:], False))
        for cp in copies:
            cp.start()
        for cp in copies:
            cp.wait()

    res = pl.pallas_call(
        body, name=name, out_shape=_exchange_shapes(scatter, True) + _exchange_shapes(gather, False),
        in_specs=[ANY] * (ns + ng), out_specs=[ANY] * (ns + ng),
        scratch_shapes=_exchange_scratch(ns) + _exchange_scratch(ng),
    )(*scatter, *gather)
    return res[:ns], res[ns:]


MM_ROWS = 512
MM_COLS = 1024


def _matmul(a, b, *, mode, out_dtype, name, tm, tn, res=None):
    if mode == "nn":
        (M, K), (K2, N) = a.shape, b.shape
    else:
        (M, K), (N, K2) = a.shape, b.shape
    assert K == K2, (a.shape, b.shape, mode)
    tm, tn = min(tm, M), min(tn, N)
    sm = min(tm, MM_ROWS)
    sn = tn if tn <= MM_COLS else _pick(tn, (512, 256, 128))
    assert M % tm == 0 and N % tn == 0 and tm % sm == 0, (M, N, K, tm, tn)
    dims = NN_DIMS if mode == "nn" else NT_DIMS
    a_spec = pl.BlockSpec((tm, K), lambda i, j: (i, 0))
    if mode == "nt":
        b_spec = pl.BlockSpec((tn, K), lambda i, j: (j, 0))
    else:
        b_spec = pl.BlockSpec((K, tn), lambda i, j: (0, j))
    o_spec = pl.BlockSpec((tm, tn), lambda i, j: (i, j))
    has_res = res is not None

    def body(*refs):
        a_ref, b_ref = refs[0], refs[1]
        r_ref = refs[2] if has_res else None
        o_ref = refs[2 + has_res]

        def chunk(r, carry):
            rows = pl.ds(pl.multiple_of(r * sm, sm), sm)
            av = a_ref[rows, :]
            for c0 in range(0, tn, sn):
                bv = b_ref[c0:c0 + sn, :] if mode == "nt" else b_ref[:, c0:c0 + sn]
                total = _dot(av, bv, dims)
                if has_res:
                    total = total + r_ref[rows, c0:c0 + sn]
                o_ref[rows, c0:c0 + sn] = total.astype(out_dtype)
            return carry

        lax.fori_loop(0, tm // sm, chunk, 0)

    return pl.pallas_call(
        body, name=name, grid=(M // tm, N // tn),
        out_shape=jax.ShapeDtypeStruct((M, N), out_dtype),
        in_specs=[a_spec, b_spec] + ([o_spec] if has_res else []),
        out_specs=o_spec,
        compiler_params=_params("parallel", "parallel"),
    )(*((a, b, res) if has_res else (a, b)))


def _matmul_tn(a, b, *, out_dtype, name, tk=512, sm=256):
    (K, M), (K2, N) = a.shape, b.shape
    assert K == K2 and K % tk == 0 and M % sm == 0, (a.shape, b.shape)
    nk = K // tk

    def body(a_ref, b_ref, o_ref, acc_ref):
        k = pl.program_id(0)

        @pl.when(k == 0)
        def _():
            acc_ref[...] = jnp.zeros_like(acc_ref)

        def chunk(mi, carry):
            cols = pl.ds(pl.multiple_of(mi * sm, sm), sm)
            acc_ref[cols, :] += _dot(a_ref[:, cols].T, b_ref[...])
            return carry

        lax.fori_loop(0, M // sm, chunk, 0)

        @pl.when(k == nk - 1)
        def _():
            def emit(mi, carry):
                rows = pl.ds(pl.multiple_of(mi * sm, sm), sm)
                o_ref[rows, :] = acc_ref[rows, :].astype(out_dtype)
                return carry
            lax.fori_loop(0, M // sm, emit, 0)

    return pl.pallas_call(
        body, name=name, grid=(nk,),
        out_shape=jax.ShapeDtypeStruct((M, N), out_dtype),
        in_specs=[pl.BlockSpec((tk, M), lambda k: (k, 0)), pl.BlockSpec((tk, N), lambda k: (k, 0))],
        out_specs=pl.BlockSpec((M, N), lambda k: (0, 0)),
        scratch_shapes=[pltpu.VMEM((M, N), F32)],
        compiler_params=_params("arbitrary"),
    )(a, b)


def _rmsnorm_fwd(x, gains, name, tr=512):
    n = len(gains)

    def body(*refs):
        x_ref = refs[0]
        xv = x_ref[...]
        r = lax.rsqrt(jnp.mean(xv * xv, axis=-1, keepdims=True) + RMS_EPS)
        y = xv * r
        for a in range(n):
            refs[1 + n + a][...] = (y * refs[1 + a][...]).astype(BF16)

    row = pl.BlockSpec((tr, D), lambda i: (i, 0))
    gain = pl.BlockSpec((1, D), lambda i: (0, 0))
    return pl.pallas_call(
        body, name=name, grid=(S // tr,),
        out_shape=[jax.ShapeDtypeStruct((S, D), BF16)] * n,
        in_specs=[row] + [gain] * n, out_specs=[row] * n,
        compiler_params=_params("parallel"),
    )(x, *gains)


def _matmul_norm_bwd(pairs, x, g, dres, name, scatter=(), tm=512):
    M = x.shape[0]
    n_p, n_ex = len(pairs), len(scatter)
    n_in = 2 * n_p + 3
    steps = M // tm

    def body(*refs):
        x_ref, g_ref, dres_ref = refs[2 * n_p:n_in]
        dx_ref, dxb_ref, dg_ref = refs[n_in + n_ex:n_in + n_ex + 3]
        exchange = (refs[n_in:n_in + n_ex], refs[n_in + n_ex + 3:n_in + 2 * n_ex + 3], refs[n_in + 2 * n_ex + 3:], True)

        @pl.when(pl.program_id(0) == 0)
        def _():
            for cp in _exchange_copies(*exchange):
                cp.start()

        dyv = _dot(refs[0][...], refs[1][...], NT_DIMS)
        for p in range(1, n_p):
            dyv = dyv + _dot(refs[2 * p][...], refs[2 * p + 1][...], NT_DIMS)
        xv = x_ref[...]
        r = lax.rsqrt(jnp.mean(xv * xv, axis=-1, keepdims=True) + RMS_EPS)
        xhat = xv * r
        dxhat = dyv * g_ref[...]
        mean_term = jnp.mean(dxhat * xhat, axis=-1, keepdims=True)
        dx = r * (dxhat - xhat * mean_term) + dres_ref[...]
        dx_ref[...] = dx
        dxb_ref[...] = dx.astype(BF16)
        part = jnp.sum(dyv * xhat, axis=0, keepdims=True)

        @pl.when(pl.program_id(0) == 0)
        def _():
            dg_ref[...] = part

        @pl.when(pl.program_id(0) > 0)
        def _():
            dg_ref[...] += part

        @pl.when(pl.program_id(0) == steps - 1)
        def _():
            for cp in _exchange_copies(*exchange):
                cp.wait()

    row = pl.BlockSpec((tm, D), lambda i: (i, 0))
    gain = pl.BlockSpec((1, D), lambda i: (0, 0))
    pair_specs, operands = [], []
    for a, b in pairs:
        assert a.shape == (M, b.shape[1]) and b.shape[0] == D, (a.shape, b.shape)
        pair_specs += [pl.BlockSpec((tm, a.shape[1]), lambda i: (i, 0)), pl.BlockSpec(b.shape, lambda i: (0, 0))]
        operands += [a, b]
    return pl.pallas_call(
        body, name=name, grid=(steps,),
        out_shape=[jax.ShapeDtypeStruct((M, D), F32), jax.ShapeDtypeStruct((M, D), BF16),
                   jax.ShapeDtypeStruct((1, D), F32)] + _exchange_shapes(scatter, True),
        in_specs=pair_specs + [row, gain, row] + [ANY] * n_ex,
        out_specs=[row, row, gain] + [ANY] * n_ex,
        scratch_shapes=_exchange_scratch(n_ex),
        compiler_params=_params("arbitrary"),
    )(*operands, x, g, dres, *scatter)


def _final_loss(x, target, g, name, tr=512):
    def body(x_ref, t_ref, g_ref, loss_ref, dx_ref, dxb_ref, dg_ref):
        xv = x_ref[...]
        gv = g_ref[...]
        r = lax.rsqrt(jnp.mean(xv * xv, axis=-1, keepdims=True) + RMS_EPS)
        xhat = xv * r
        err = xhat * gv - t_ref[...]
        row_loss = jnp.mean(err * err, axis=-1, keepdims=True)
        lpart = 0.5 * jnp.sum(row_loss, axis=0, keepdims=True)
        dyv = err / D
        dxhat = dyv * gv
        mean_term = jnp.mean(dxhat * xhat, axis=-1, keepdims=True)
        dx = r * (dxhat - xhat * mean_term)
        dx_ref[...] = dx
        dxb_ref[...] = dx.astype(BF16)
        gpart = jnp.sum(dyv * xhat, axis=0, keepdims=True)

        @pl.when(pl.program_id(0) == 0)
        def _():
            dg_ref[...] = gpart
            loss_ref[...] = jnp.broadcast_to(lpart, loss_ref.shape)

        @pl.when(pl.program_id(0) > 0)
        def _():
            dg_ref[...] += gpart
            loss_ref[...] += jnp.broadcast_to(lpart, loss_ref.shape)

    row = pl.BlockSpec((tr, D), lambda i: (i, 0))
    gain = pl.BlockSpec((1, D), lambda i: (0, 0))
    lspec = pl.BlockSpec((8, LANES), lambda i: (0, 0))
    return pl.pallas_call(
        body, name=name, grid=(S // tr,),
        out_shape=[jax.ShapeDtypeStruct((8, LANES), F32), jax.ShapeDtypeStruct((S, D), F32),
                   jax.ShapeDtypeStruct((S, D), BF16), jax.ShapeDtypeStruct((1, D), F32)],
        in_specs=[row, row, gain], out_specs=[lspec, row, row, gain],
        compiler_params=_params("arbitrary"),
    )(x, target, g)


CONV_TR = 128
CONV_TC = D_FF
CONV_NJ = D_FF // CONV_TC
HALO = 16


def _causal_taps(cur_ref, prev_ref, first):
    xv = cur_ref[...].astype(F32)
    pv = prev_ref[...].astype(F32)
    p1 = jnp.where(first, 0.0, pv[HALO - 1:HALO, :])
    p2 = jnp.where(first, 0.0, pv[HALO - 2:HALO - 1, :])
    r1, r2 = pltpu.roll(xv, 1, 0), pltpu.roll(xv, 2, 0)
    row = lax.broadcasted_iota(jnp.int32, (8, xv.shape[1]), 0)
    xm1 = jnp.concatenate([jnp.where(row == 0, p1, r1[0:8]), r1[8:]], axis=0)
    xm2 = jnp.concatenate([jnp.where(row == 0, p2, jnp.where(row == 1, p1, r2[0:8])), r2[8:]], axis=0)
    return xv, xm1, xm2


def _conv_specs():
    def prev_row(i):
        return jnp.maximum(i * (CONV_TR // HALO) - 1, 0)
    ua = pl.BlockSpec((CONV_TR, CONV_TC), lambda i, j: (i, j))
    ug = ua
    pa = pl.BlockSpec((HALO, CONV_TC), lambda i, j: (prev_row(i), j))
    pg = pa
    wa = pl.BlockSpec((3, CONV_TC), lambda i, j: (0, j))
    wg = pl.BlockSpec((3, CONV_TC), lambda i, j: (0, j + CONV_NJ))
    ba = pl.BlockSpec((1, CONV_TC), lambda i, j: (0, j))
    bg = pl.BlockSpec((1, CONV_TC), lambda i, j: (0, j + CONV_NJ))
    return [ua, pa, ug, pg, wa, wg, ba, bg]


def _convgate_fwd(u_a, u_g, w, b, name):
    def body(ua, pa, ug, pg, wa, wg, ba, bg, o_ref):
        first = pl.program_id(0) == 0
        x0, x1, x2 = _causal_taps(ua, pa, first)
        ac = wa[0:1, :] * x2 + wa[1:2, :] * x1 + wa[2:3, :] * x0 + ba[...]
        x0, x1, x2 = _causal_taps(ug, pg, first)
        gc = wg[0:1, :] * x2 + wg[1:2, :] * x1 + wg[2:3, :] * x0 + bg[...]
        sg = 0.5 * jnp.tanh(0.5 * gc) + 0.5
        o_ref[...] = (gc * sg * ac).astype(BF16)

    return pl.pallas_call(
        body, name=name, grid=(S // CONV_TR, CONV_NJ),
        out_shape=jax.ShapeDtypeStruct((S, D_FF), BF16),
        in_specs=_conv_specs(),
        out_specs=pl.BlockSpec((CONV_TR, CONV_TC), lambda i, j: (i, j)),
        compiler_params=_params("parallel", "parallel"),
    )(u_a, u_a, u_g, u_g, w, w, b, b)


def _anticausal_conv(d, nxt_ref, w_ref, last):
    n1 = jnp.where(last, 0.0, nxt_ref[0:1, :])
    n2 = jnp.where(last, 0.0, nxt_ref[1:2, :])
    r1, r2 = pltpu.roll(d, CONV_TR - 1, 0), pltpu.roll(d, CONV_TR - 2, 0)
    row = lax.broadcasted_iota(jnp.int32, (8, d.shape[1]), 0)
    cut = CONV_TR - 8
    dp1 = jnp.concatenate([r1[:cut], jnp.where(row == 7, n1, r1[cut:])], axis=0)
    dp2 = jnp.concatenate([r2[:cut], jnp.where(row == 7, n2, jnp.where(row == 6, n1, r2[cut:]))], axis=0)
    return w_ref[2:3, :] * d + w_ref[1:2, :] * dp1 + w_ref[0:1, :] * dp2


def _convgate_bwd(u_a, u_g, w, b, dact, name):
    n_i = S // CONV_TR

    def body(ua, pa, ug, pg, wa, wg, ba, bg, d_ref, dua_ref, dug_ref, dwa_ref, dwg_ref, dba_ref, dbg_ref,
             nxt_a, nxt_g):
        i = pl.program_id(1)
        last = i == 0
        first = i == n_i - 1
        a0, a1, a2 = _causal_taps(ua, pa, first)
        ac = wa[0:1, :] * a2 + wa[1:2, :] * a1 + wa[2:3, :] * a0 + ba[...]
        g0, g1, g2 = _causal_taps(ug, pg, first)
        gc = wg[0:1, :] * g2 + wg[1:2, :] * g1 + wg[2:3, :] * g0 + bg[...]
        sg = 0.5 * jnp.tanh(0.5 * gc) + 0.5
        dact_v = d_ref[...].astype(F32)
        da = dact_v * (gc * sg)
        dg = dact_v * ac * (sg * (1.0 + gc * (1.0 - sg)))
        dua_ref[...] = _anticausal_conv(da, nxt_a, wa, last).astype(BF16)
        dug_ref[...] = _anticausal_conv(dg, nxt_g, wg, last).astype(BF16)
        nxt_a[...] = da[0:8]
        nxt_g[...] = dg[0:8]

        def col(v):
            return jnp.sum(v, axis=0, keepdims=True)

        parts = [col(da * a2), col(da * a1), col(da * a0), col(dg * g2), col(dg * g1), col(dg * g0),
                 col(da), col(dg)]

        @pl.when(last)
        def _():
            for k in range(3):
                dwa_ref[k:k + 1, :] = parts[k]
                dwg_ref[k:k + 1, :] = parts[3 + k]
            dba_ref[...] = parts[6]
            dbg_ref[...] = parts[7]

        @pl.when(i > 0)
        def _():
            for k in range(3):
                dwa_ref[k:k + 1, :] += parts[k]
                dwg_ref[k:k + 1, :] += parts[3 + k]
            dba_ref[...] += parts[6]
            dbg_ref[...] += parts[7]

    def swap(spec):
        return pl.BlockSpec(spec.block_shape, lambda j, i, f=spec.index_map: f(n_i - 1 - i, j))

    blk = pl.BlockSpec((CONV_TR, CONV_TC), lambda j, i: (n_i - 1 - i, j))
    w3 = pl.BlockSpec((3, CONV_TC), lambda j, i: (0, j))
    b1 = pl.BlockSpec((1, CONV_TC), lambda j, i: (0, j))
    return pl.pallas_call(
        body, name=name, grid=(CONV_NJ, n_i),
        out_shape=[jax.ShapeDtypeStruct((S, D_FF), BF16), jax.ShapeDtypeStruct((S, D_FF), BF16),
                   jax.ShapeDtypeStruct((3, D_FF), F32), jax.ShapeDtypeStruct((3, D_FF), F32),
                   jax.ShapeDtypeStruct((1, D_FF), F32), jax.ShapeDtypeStruct((1, D_FF), F32)],
        in_specs=[swap(s) for s in _conv_specs()] + [blk],
        out_specs=[blk, blk, w3, w3, b1, b1],
        scratch_shapes=[pltpu.VMEM((8, CONV_TC), F32), pltpu.VMEM((8, CONV_TC), F32)],
        compiler_params=_params("arbitrary", "arbitrary"),
    )(u_a, u_a, u_g, u_g, w, w, b, b, dact)


FOX_T = 512
FOX_TQ, FOX_TK = 512, 512
FOX_FORWARD_AT = 4
N_PAIRS = A_HEADS // 2


def _lane_masks():
    lane = lax.broadcasted_iota(jnp.int32, (1, LANES), 1)
    return lane, (lane < HEAD_DIM, lane >= HEAD_DIM)


def _fox_prep_fwd(z_t, b, name):
    def body(z_ref, b_ref, c_ref):
        r = lax.broadcasted_iota(jnp.int32, (LANES, LANES), 0)
        cc = lax.broadcasted_iota(jnp.int32, (LANES, LANES), 1)
        upper = (r <= cc).astype(BF16)
        carry = jnp.zeros((A_HEADS, 1), F32)
        for blk in range(S // LANES):
            sl = slice(blk * LANES, (blk + 1) * LANES)
            z = z_ref[:, sl] + b_ref[...]
            lf = jnp.minimum(z, 0.0) - jnp.log(1.0 + jnp.exp(-jnp.abs(z)))
            cs = _split_dot(lf, upper, 3) + carry
            c_ref[:, sl] = cs
            carry = cs[:, LANES - 1:LANES]

    return pl.pallas_call(
        body, name=name, out_shape=jax.ShapeDtypeStruct((A_HEADS, S), F32),
        compiler_params=_params(),
    )(z_t, b)


def _fox_prep_bwd(drow_t, dcol_t, z_t, b, name):
    def body(dr_ref, dc_ref, z_ref, b_ref, dz_ref, db_ref):
        r = lax.broadcasted_iota(jnp.int32, (LANES, LANES), 0)
        cc = lax.broadcasted_iota(jnp.int32, (LANES, LANES), 1)
        lower = (r >= cc).astype(BF16)
        carry = jnp.zeros((A_HEADS, 1), F32)
        db = jnp.zeros((A_HEADS, 1), F32)
        for blk in reversed(range(S // LANES)):
            sl = slice(blk * LANES, (blk + 1) * LANES)
            rc = _split_dot(dr_ref[:, sl] - dc_ref[:, sl], lower, 3) + carry
            carry = rc[:, 0:1]
            z = z_ref[:, sl] + b_ref[...]
            dz = rc / (1.0 + jnp.exp(z))
            dz_ref[:, sl] = dz
            db = db + jnp.sum(dz, axis=1, keepdims=True)
        db_ref[...] = db

    return pl.pallas_call(
        body, name=name,
        out_shape=[jax.ShapeDtypeStruct((A_HEADS, S), F32), jax.ShapeDtypeStruct((A_HEADS, 1), F32)],
        compiler_params=_params(),
    )(drow_t, dcol_t, z_t, b)


def _fox_fwd(qkv, c_t2, name, gather):
    tq, tk = FOX_TQ, FOX_TK

    n = len(gather)

    def body(*refs):
        q_ref, k_ref, v_ref, ct_ref = refs[:4]
        o_ref, lse_ref = refs[4 + n:6 + n]
        s_scr, p_scr, acc_scr = refs[-5:-3], refs[-3:-1], refs[-1]
        qi = pl.program_id(1)

        gather_start, gather_forward, gather_finish = _gather_phases(
            refs[4:4 + n], refs[6 + n:6 + 2 * n], refs[6 + 2 * n:len(refs) - 5])

        @pl.when(jnp.logical_and(pl.program_id(0) == 0, qi == 0))
        def _():
            gather_start()

        @pl.when(jnp.logical_and(pl.program_id(0) == FOX_FORWARD_AT, qi == 0))
        def _():
            gather_forward()

        n_full = jnp.right_shift(qi, (tk // tq).bit_length() - 1)
        lane, masks = _lane_masks()
        q = q_ref[...] * SCALE
        qs = [jnp.where(masks[e], q, jnp.zeros_like(q)) for e in range(2)]

        def scores_into(j, slot):
            start = pl.multiple_of(j * tk, tk)
            kb = k_ref[pl.ds(start, tk), :]
            for e in range(2):
                s_scr[slot][e] = _dot(qs[e], kb, NT_DIMS) - ct_ref[e:e + 1, pl.ds(start, tk)]

        def softmax_of(slot, m, masked):
            m_new, alpha = [], []
            for e in range(2):
                s = s_scr[slot][e]
                if masked:
                    rows = lax.broadcasted_iota(jnp.int32, (tq, tk), 0) + (qi * tq - n_full * tk)
                    cols = lax.broadcasted_iota(jnp.int32, (tq, tk), 1)
                    s = jnp.where(cols <= rows, s, NEG)
                m_new.append(jnp.maximum(m[e], jnp.max(s, axis=1, keepdims=True)))
                p_scr[slot][e] = jnp.exp(s - m_new[e]).astype(BF16)
                alpha.append(jnp.exp(m[e] - m_new[e]))
            return tuple(m_new), tuple(alpha)

        def values_of(j, slot, alpha):
            start = pl.multiple_of(j * tk, tk)
            vb = v_ref[pl.ds(start, tk), :]
            for e in range(2):
                acc_scr[e] = (alpha[e] * acc_scr[e]
                              + _dot(p_scr[slot][e], jnp.where(masks[e], vb, jnp.ones_like(vb))))

        def stage(j, cur, nxt, carry):
            m, a_prev = carry
            scores_into(j + 1, nxt)
            values_of(jnp.maximum(j - 1, 0), nxt, a_prev)
            return softmax_of(cur, m, False)

        def finish(cur, nxt, carry):
            m, a_prev = carry
            values_of(jnp.maximum(n_full - 1, 0), nxt, a_prev)
            (m0, m1), alpha = softmax_of(cur, m, True)
            values_of(n_full, cur, alpha)
            l0 = acc_scr[0][:, HEAD_DIM:HEAD_DIM + 1]
            l1 = acc_scr[1][:, 0:1]
            o_ref[...] = jnp.where(masks[0], acc_scr[0] / l0, acc_scr[1] / l1).astype(BF16)
            lse_ref[...] = jnp.where(masks[0], m0 + jnp.log(l0), m1 + jnp.log(l1))

        scores_into(0, 0)
        for e in range(2):
            p_scr[1][e] = jnp.zeros((tq, tk), BF16)
            acc_scr[e] = jnp.zeros((tq, LANES), F32)
        two = lambda x: (x, x)
        init = (two(jnp.full((tq, 1), NEG, F32)), two(jnp.ones((tq, 1), F32)))

        def two_stages(jj, carry):
            return stage(2 * jj + 1, 1, 0, stage(2 * jj, 0, 1, carry))

        carry = lax.fori_loop(0, jnp.right_shift(n_full, 1), two_stages, init)
        odd = jnp.bitwise_and(n_full, 1) == 1

        @pl.when(odd)
        def _():
            finish(1, 0, stage(n_full - 1, 0, 1, carry))

        @pl.when(jnp.logical_not(odd))
        def _():
            finish(0, 1, carry)

        @pl.when(jnp.logical_and(pl.program_id(0) == N_PAIRS - 1, qi == S // tq - 1))
        def _():
            gather_finish()

    qspec = pl.BlockSpec((tq, LANES), lambda h, i: (i, h))
    return pl.pallas_call(
        body, name=name, grid=(N_PAIRS, S // tq),
        out_shape=[jax.ShapeDtypeStruct((S, D), BF16), jax.ShapeDtypeStruct((S, D), F32)]
        + _exchange_shapes(gather, False),
        in_specs=[qspec,
                  pl.BlockSpec((S, LANES), lambda h, i: (0, N_PAIRS + h)),
                  pl.BlockSpec((S, LANES), lambda h, i: (0, 2 * N_PAIRS + h)),
                  pl.BlockSpec((None, 2, S), lambda h, i: (h, 0, 0))] + [ANY] * n,
        out_specs=[qspec, qspec] + [ANY] * n,
        scratch_shapes=_exchange_scratch(n) + [
            pltpu.VMEM((2, tq, tk), F32), pltpu.VMEM((2, tq, tk), F32),
            pltpu.VMEM((2, tq, tk), BF16), pltpu.VMEM((2, tq, tk), BF16),
            pltpu.VMEM((2, tq, LANES), F32)],
        compiler_params=_params("arbitrary", "arbitrary"),
    )(qkv, qkv, qkv, c_t2, *gather)


def _head_rowsum(a, b, name, tr=512):
    C = a.shape[1]

    def body(a_ref, b_ref, o_ref):
        r = lax.broadcasted_iota(jnp.int32, (LANES, LANES), 0) < HEAD_DIM
        cc = lax.broadcasted_iota(jnp.int32, (LANES, LANES), 1) < HEAD_DIM
        same_head = (r == cc).astype(BF16)
        for blk in range(C // LANES):
            sl = slice(blk * LANES, (blk + 1) * LANES)
            prod = a_ref[:, sl].astype(F32) * b_ref[:, sl].astype(F32)
            o_ref[:, sl] = _split_dot(prod, same_head, 2)

    row = pl.BlockSpec((tr, C), lambda i: (i, 0))
    return pl.pallas_call(
        body, name=name, grid=(S // tr,), out_shape=jax.ShapeDtypeStruct((S, C), F32),
        in_specs=[row, row], out_specs=row, compiler_params=_params("parallel"),
    )(a, b)


def _fox_bwd(qkv, do, lse, delta, c_t2, name, scatter):
    t = FOX_T
    nq = S // t

    n = len(scatter)

    def body(*refs):
        q_ref, k_ref, v_ref, do_ref, lse_ref, dl_ref, ct_ref = refs[:7]
        dq_ref, dk_ref, dv_ref, dcol_ref, drow_ref = refs[7 + n:12 + n]
        exchange = (refs[7:7 + n], refs[12 + n:12 + 2 * n], refs[12 + 2 * n:len(refs) - 5], True)
        sd_scr, pd_scr, acc_scr = refs[-5:-3], refs[-3:-1], refs[-1]
        kj = pl.program_id(1)

        @pl.when(jnp.logical_and(pl.program_id(0) == 0, kj == 0))
        def _():
            for cp in _exchange_copies(*exchange):
                cp.start()

        @pl.when(kj == 0)
        def _():
            dq_ref[...] = jnp.zeros_like(dq_ref)
            drow_ref[...] = jnp.zeros_like(drow_ref)

        lane, masks = _lane_masks()
        k = k_ref[...]
        v = v_ref[...]
        k_aug = [jnp.where(masks[e], k * SCALE, jnp.ones_like(k)) for e in range(2)]
        cs = [ct_ref[e:e + 1, :] for e in range(2)]

        def rows_of(i):
            r0 = pl.multiple_of(i * t, t)
            return pl.ds(r0, t), q_ref[pl.ds(r0, t), :] * SCALE, do_ref[pl.ds(r0, t), :]

        def scores_into(i, slot):
            _, qb, dob = rows_of(i)
            for e in range(2):
                qe = jnp.where(masks[e], qb, jnp.zeros_like(qb))
                doe = jnp.where(masks[e], dob, jnp.zeros_like(dob))
                sd_scr[slot][2 * e] = _dot(qe, k, NT_DIMS) - cs[e]
                sd_scr[slot][2 * e + 1] = _dot(doe, v, NT_DIMS)

        def pointwise(i, slot, masked):
            rows, _, _ = rows_of(i)
            for e in range(2):
                lo = e * HEAD_DIM
                s = sd_scr[slot][2 * e]
                if masked:
                    r = lax.broadcasted_iota(jnp.int32, (t, t), 0)
                    c = lax.broadcasted_iota(jnp.int32, (t, t), 1)
                    s = jnp.where(c <= r, s, NEG)
                p = jnp.exp(s - lse_ref[rows, lo:lo + 1])
                pd_scr[slot][2 * e] = p.astype(BF16)
                pd_scr[slot][2 * e + 1] = (p * (sd_scr[slot][2 * e + 1] - dl_ref[rows, lo:lo + 1])).astype(BF16)

        def accumulate(i, slot):
            rows, qb, dob = rows_of(i)
            dq_parts = []
            for e in range(2):
                p, ds = pd_scr[slot][2 * e], pd_scr[slot][2 * e + 1]
                q_aug = jnp.where(masks[e], qb, jnp.ones_like(qb))
                doe = jnp.where(masks[e], dob, jnp.zeros_like(dob))
                acc_scr[2] += _dot(p, doe, TN_DIMS)
                acc_scr[e] += _dot(ds, q_aug, TN_DIMS)
                dq_parts.append(_dot(ds, k_aug[e]))
            dq_ref[rows, :] += jnp.where(masks[0], dq_parts[0], dq_parts[1])
            drow_ref[rows, :] += jnp.where(masks[0], dq_parts[1], dq_parts[0])

        def stage(i, cur, nxt):
            scores_into(jnp.minimum(i + 1, nq - 1), nxt)
            accumulate(i - 1, nxt)
            pointwise(i, cur, False)

        acc_scr[...] = jnp.zeros_like(acc_scr)
        scores_into(kj, 0)
        pointwise(kj, 0, True)
        scores_into(jnp.minimum(kj + 1, nq - 1), 1)
        rest = nq - 1 - kj

        def two_stages(jj, carry):
            stage(kj + 1 + 2 * jj, 1, 0)
            stage(kj + 2 + 2 * jj, 0, 1)
            return carry

        lax.fori_loop(0, jnp.right_shift(rest, 1), two_stages, 0)
        odd = jnp.bitwise_and(rest, 1) == 1

        @pl.when(odd)
        def _():
            stage(nq - 1, 1, 0)
            accumulate(nq - 1, 1)

        @pl.when(jnp.logical_not(odd))
        def _():
            accumulate(nq - 1, 0)

        dk0, dk1, dv = acc_scr[0], acc_scr[1], acc_scr[2]
        dk_ref[...] = jnp.where(masks[0], dk0, dk1).astype(BF16)
        dcol_ref[...] = jnp.where(masks[0], dk1, dk0)
        dv_ref[...] = dv.astype(BF16)

        @pl.when(jnp.logical_and(pl.program_id(0) == N_PAIRS - 1, kj == nq - 1))
        def _():
            for cp in _exchange_copies(*exchange):
                cp.wait()

    full = lambda off: pl.BlockSpec((S, LANES), lambda h, j, off=off: (0, off + h))
    kv = lambda off: pl.BlockSpec((t, LANES), lambda h, j, off=off: (j, off + h))
    return pl.pallas_call(
        body, name=name, grid=(N_PAIRS, nq),
        out_shape=[jax.ShapeDtypeStruct((S, D), F32), jax.ShapeDtypeStruct((S, D), BF16),
                   jax.ShapeDtypeStruct((S, D), BF16), jax.ShapeDtypeStruct((S, D), F32),
                   jax.ShapeDtypeStruct((S, D), F32)] + _exchange_shapes(scatter, True),
        in_specs=[full(0), kv(N_PAIRS), kv(2 * N_PAIRS), full(0), full(0), full(0),
                  pl.BlockSpec((None, 2, t), lambda h, j: (h, 0, j))] + [ANY] * n,
        out_specs=[full(0), kv(0), kv(0), kv(0), full(0)] + [ANY] * n,
        scratch_shapes=_exchange_scratch(n) + [
            pltpu.VMEM((4, t, t), F32), pltpu.VMEM((4, t, t), F32),
            pltpu.VMEM((4, t, t), BF16), pltpu.VMEM((4, t, t), BF16),
            pltpu.VMEM((3, t, LANES), F32)],
        compiler_params=_params("arbitrary", "arbitrary"),
    )(qkv, qkv, qkv, do, lse, delta, c_t2, *scatter)


B_PAIRS = 4
B_NB = S // B_W


def _group_consts(g):
    nbs = jnp.where(g == 0, B_NB // B_DILS[0], jnp.where(g == 1, B_NB // B_DILS[1], B_NB // B_DILS[2]))
    dil = jnp.where(g == 0, B_DILS[0], jnp.where(g == 1, B_DILS[1], B_DILS[2]))
    return nbs, dil


def _band(dil):
    qi = lax.broadcasted_iota(jnp.int32, (B_W, B_W), 0)
    kj = lax.broadcasted_iota(jnp.int32, (B_W, B_W), 1)
    dist_c = qi - kj
    dist_p = qi + B_W - kj
    return (dist_c * dil).astype(F32), dist_c >= 0, (dist_p * dil).astype(F32), dist_p <= B_W


def _dil_fwd(qp, kp, vp, slopes, name):
    def body(sl_ref, q_ref, kp_ref, kc_ref, vp_ref, vc_ref, o_ref, lse_ref):
        g, n = pl.program_id(0), pl.program_id(1)
        nbs, dil = _group_consts(g)
        has_prev = (n % nbs) != 0
        lane, masks = _lane_masks()
        bias_c, ok_c, bias_p, ok_p = _band(dil)
        ok_p = jnp.logical_and(ok_p, has_prev)
        heads = [(hp, e) for hp in range(B_PAIRS) for e in range(2)]
        col = lambda ref, hp: ref[:, hp * LANES:(hp + 1) * LANES]
        logits = []
        for hp, e in heads:
            q = col(q_ref, hp) * SCALE
            qe = jnp.where(masks[e], q, jnp.zeros_like(q))
            logits.append((_dot(qe, col(kc_ref, hp), NT_DIMS), _dot(qe, col(kp_ref, hp), NT_DIMS)))
        probs = []
        for (hp, e), (sc, sp) in zip(heads, logits):
            slope = sl_ref[g * 8 + 2 * hp + e]
            sc = jnp.where(ok_c, sc - slope * bias_c, NEG)
            sp = jnp.where(ok_p, sp - slope * bias_p, NEG)
            m = jnp.maximum(jnp.max(sc, axis=1, keepdims=True), jnp.max(sp, axis=1, keepdims=True))
            probs.append((jnp.exp(sc - m).astype(BF16), jnp.exp(sp - m).astype(BF16), m))
        outs, lses = [], []
        for (hp, e), (pc, pp, m) in zip(heads, probs):
            vc, vpv = col(vc_ref, hp), col(vp_ref, hp)
            acc = (_dot(pc, jnp.where(masks[e], vc, jnp.ones_like(vc)))
                   + _dot(pp, jnp.where(masks[e], vpv, jnp.ones_like(vpv))))
            l = acc[:, HEAD_DIM:HEAD_DIM + 1] if e == 0 else acc[:, 0:1]
            outs.append(acc / l)
            lses.append(m + jnp.log(l))
        o_ref[...] = jnp.concatenate(
            [jnp.where(masks[0], outs[2 * hp], outs[2 * hp + 1]) for hp in range(B_PAIRS)], axis=1).astype(BF16)
        lse = jnp.zeros((B_W, LANES), F32)
        for h in range(2 * B_PAIRS):
            lse = jnp.where(lane == h, lses[h], lse)
        lse_ref[...] = lse

    cur = pl.BlockSpec((None, B_W, B_OUT), lambda g, n, sl: (g, n, 0))
    prev = pl.BlockSpec((None, B_W, B_OUT), lambda g, n, sl: (g, jnp.maximum(n - 1, 0), 0))
    stat = pl.BlockSpec((None, B_W, LANES), lambda g, n, sl: (g, n, 0))
    return pl.pallas_call(
        body, name=name,
        grid_spec=pltpu.PrefetchScalarGridSpec(
            num_scalar_prefetch=1, grid=(3, B_NB),
            in_specs=[cur, prev, cur, prev, cur], out_specs=[cur, stat]),
        out_shape=[jax.ShapeDtypeStruct((3, S, B_OUT), BF16), jax.ShapeDtypeStruct((3, S, LANES), F32)],
        compiler_params=_params("parallel", "parallel"),
    )(slopes, qp, kp, kp, vp, vp)


def _head_expander():
    r = lax.broadcasted_iota(jnp.int32, (LANES, B_OUT), 0)
    c = lax.broadcasted_iota(jnp.int32, (LANES, B_OUT), 1)
    return jnp.logical_and(c >= r * HEAD_DIM, c < (r + 1) * HEAD_DIM).astype(BF16)


def _dil_merge(og, lseg, name, tr=256):
    def body(o_ref, l_ref, out_ref, lse_ref):
        l0, l1, l2 = l_ref[0], l_ref[1], l_ref[2]
        m = jnp.maximum(jnp.maximum(l0, l1), l2)
        w0, w1, w2 = jnp.exp(l0 - m), jnp.exp(l1 - m), jnp.exp(l2 - m)
        den = w0 + w1 + w2
        lse_ref[...] = m + jnp.log(den)
        expand = _head_expander()
        out = None
        for g, w in enumerate((w0, w1, w2)):
            part = _split_dot(w / den, expand, 3) * o_ref[g].astype(F32)
            out = part if out is None else out + part
        out_ref[...] = out.astype(BF16)

    blk3 = pl.BlockSpec((3, tr, B_OUT), lambda i: (0, i, 0))
    stat3 = pl.BlockSpec((3, tr, LANES), lambda i: (0, i, 0))
    blk = pl.BlockSpec((tr, B_OUT), lambda i: (i, 0))
    stat = pl.BlockSpec((tr, LANES), lambda i: (i, 0))
    return pl.pallas_call(
        body, name=name, grid=(S // tr,),
        out_shape=[jax.ShapeDtypeStruct((S, B_OUT), BF16), jax.ShapeDtypeStruct((S, LANES), F32)],
        in_specs=[blk3, stat3], out_specs=[blk, stat], compiler_params=_params("parallel"),
    )(og, lseg)


def _head_rowsum_compact(a, b, name, tr=256):
    def body(a_ref, b_ref, o_ref):
        r = lax.broadcasted_iota(jnp.int32, (B_OUT, LANES), 0)
        c = lax.broadcasted_iota(jnp.int32, (B_OUT, LANES), 1)
        collect = jnp.logical_and(r >= c * HEAD_DIM, r < (c + 1) * HEAD_DIM).astype(BF16)
        prod = a_ref[...].astype(F32) * b_ref[...].astype(F32)
        o_ref[...] = _split_dot(prod, collect, 2)

    row = pl.BlockSpec((tr, B_OUT), lambda i: (i, 0))
    return pl.pallas_call(
        body, name=name, grid=(S // tr,), out_shape=jax.ShapeDtypeStruct((S, LANES), F32),
        in_specs=[row, row], out_specs=pl.BlockSpec((tr, LANES), lambda i: (i, 0)),
        compiler_params=_params("parallel"),
    )(a, b)


def _dil_bwd(qp, kp, vp, dop, lsep, dlp, slopes, name, scatter):
    n_ex = len(scatter)

    def body(sl_ref, *refs):
        (qc_ref, qn_ref, kp_ref, kc_ref, vp_ref, vc_ref, doc_ref, don_ref,
         lc_ref, ln_ref, dc_ref, dn_ref) = refs[:12]
        dq_ref, dk_ref, dv_ref = refs[12 + n_ex:15 + n_ex]
        exchange = (refs[12:12 + n_ex], refs[15 + n_ex:15 + 2 * n_ex], refs[15 + 2 * n_ex:], True)
        g, n = pl.program_id(0), pl.program_id(1)

        @pl.when(jnp.logical_and(g == 0, n == 0))
        def _():
            for cp in _exchange_copies(*exchange):
                cp.start()

        nbs, dil = _group_consts(g)
        has_prev = (n % nbs) != 0
        has_next = jnp.logical_and(n + 1 < B_NB, ((n + 1) % nbs) != 0)
        lane, masks = _lane_masks()
        bias_c, ok_c, bias_p, ok_p = _band(dil)
        ok_pp = jnp.logical_and(ok_p, has_prev)
        ok_np = jnp.logical_and(ok_p, has_next)
        heads = [(hp, e) for hp in range(B_PAIRS) for e in range(2)]
        col = lambda ref, hp: ref[:, hp * LANES:(hp + 1) * LANES]
        mask = lambda t, e: jnp.where(masks[e], t, jnp.zeros_like(t))
        raw = []
        for hp, e in heads:
            qce, qne = mask(col(qc_ref, hp) * SCALE, e), mask(col(qn_ref, hp) * SCALE, e)
            doce, done = mask(col(doc_ref, hp), e), mask(col(don_ref, hp), e)
            kc, kpv, vc, vpv = col(kc_ref, hp), col(kp_ref, hp), col(vc_ref, hp), col(vp_ref, hp)
            raw.append(((_dot(qce, kc, NT_DIMS), _dot(doce, vc, NT_DIMS)),
                        (_dot(qce, kpv, NT_DIMS), _dot(doce, vpv, NT_DIMS)),
                        (_dot(qne, kc, NT_DIMS), _dot(done, vc, NT_DIMS))))
        pds = []
        for (hp, e), tiles in zip(heads, raw):
            lo = 2 * hp + e
            slope = sl_ref[g * 8 + 2 * hp + e]
            lse_c, dl_c = lc_ref[:, lo:lo + 1], dc_ref[:, lo:lo + 1]
            lse_n, dl_n = ln_ref[:, lo:lo + 1], dn_ref[:, lo:lo + 1]
            out = []
            for (s, dp), ok, bias, lse, dl in ((tiles[0], ok_c, bias_c, lse_c, dl_c),
                                               (tiles[1], ok_pp, bias_p, lse_c, dl_c),
                                               (tiles[2], ok_np, bias_p, lse_n, dl_n)):
                p = jnp.exp(jnp.where(ok, s - slope * bias, NEG) - lse)
                out.append((p.astype(BF16), (p * (dp - dl)).astype(BF16)))
            pds.append(out)
        dq_all, dk_all, dv_all = [], [], []
        for hp in range(B_PAIRS):
            dq = jnp.zeros((B_W, LANES), F32)
            dk = jnp.zeros((B_W, LANES), F32)
            dv = jnp.zeros((B_W, LANES), F32)
            for e in range(2):
                (p_c, ds_c), (_, ds_p), (p_n, ds_n) = pds[2 * hp + e]
                qce, qne = mask(col(qc_ref, hp) * SCALE, e), mask(col(qn_ref, hp) * SCALE, e)
                doce, done = mask(col(doc_ref, hp), e), mask(col(don_ref, hp), e)
                dq = dq + _dot(ds_c, mask(col(kc_ref, hp) * SCALE, e)) + _dot(ds_p, mask(col(kp_ref, hp) * SCALE, e))
                dk = dk + _dot(ds_c, qce, TN_DIMS) + _dot(ds_n, qne, TN_DIMS)
                dv = dv + _dot(p_c, doce, TN_DIMS) + _dot(p_n, done, TN_DIMS)
            dq_all.append(dq)
            dk_all.append(dk)
            dv_all.append(dv)
        dq_ref[...] = jnp.concatenate(dq_all, axis=1).astype(BF16)
        dk_ref[...] = jnp.concatenate(dk_all, axis=1).astype(BF16)
        dv_ref[...] = jnp.concatenate(dv_all, axis=1).astype(BF16)

        @pl.when(jnp.logical_and(g == 2, n == B_NB - 1))
        def _():
            for cp in _exchange_copies(*exchange):
                cp.wait()

    cur = pl.BlockSpec((None, B_W, B_OUT), lambda g, n, sl: (g, n, 0))
    prev = pl.BlockSpec((None, B_W, B_OUT), lambda g, n, sl: (g, jnp.maximum(n - 1, 0), 0))
    nxt = pl.BlockSpec((None, B_W, B_OUT), lambda g, n, sl: (g, jnp.minimum(n + 1, B_NB - 1), 0))
    stat_cur = pl.BlockSpec((None, B_W, LANES), lambda g, n, sl: (g, n, 0))
    stat_nxt = pl.BlockSpec((None, B_W, LANES), lambda g, n, sl: (g, jnp.minimum(n + 1, B_NB - 1), 0))
    return pl.pallas_call(
        body, name=name,
        grid_spec=pltpu.PrefetchScalarGridSpec(
            num_scalar_prefetch=1, grid=(3, B_NB),
            in_specs=[cur, nxt, prev, cur, prev, cur, cur, nxt, stat_cur, stat_nxt, stat_cur, stat_nxt]
            + [ANY] * n_ex,
            out_specs=[cur, cur, cur] + [ANY] * n_ex,
            scratch_shapes=_exchange_scratch(n_ex)),
        out_shape=[jax.ShapeDtypeStruct((3, S, B_OUT), BF16)] * 3 + _exchange_shapes(scatter, True),
        compiler_params=_params("arbitrary", "arbitrary"),
    )(slopes, qp, qp, kp, kp, vp, vp, dop, dop, lsep, lsep, dlp, dlp, *scatter)


def _rows_block(shape, max_bytes=2 * 1024 * 1024):
    rows, cols = shape
    padded_cols = -(-cols // LANES) * LANES
    for tr in (1024, 512, 256, 128, 64, 32, 16):
        if rows % tr == 0 and tr * padded_cols * 4 <= max_bytes:
            return tr
    return rows


def _adam_update(w, m, v, g):
    m_new = ADAM_B1 * m + (1.0 - ADAM_B1) * g
    v_new = ADAM_B2 * v + (1.0 - ADAM_B2) * (g * g)
    m_hat = m_new / (1.0 - ADAM_B1 ** ADAM_STEP)
    v_hat = v_new / (1.0 - ADAM_B2 ** ADAM_STEP)
    delta = -ADAM_LR * (m_hat / (jnp.sqrt(v_hat) + ADAM_EPS) + ADAM_WD * w)
    return delta, m_new, v_new


def _adamw_sharded(w, m, v, parts, name):
    R, C = w.shape
    tr = _rows_block((R, C), max_bytes=1024 * 1024)

    def body(w_ref, m_ref, v_ref, p_ref, g_ref, d_ref, mo_ref, vo_ref):
        g = p_ref[0].astype(F32)
        for dev in range(1, N_DEV):
            g = g + p_ref[dev].astype(F32)
        g_ref[...] = g
        d_ref[...], mo_ref[...], vo_ref[...] = _adam_update(w_ref[...], m_ref[...], v_ref[...], g)

    blk = pl.BlockSpec((tr, C), lambda i: (i, 0))
    out = jax.ShapeDtypeStruct((R, C), F32)
    return pl.pallas_call(
        body, name=name, grid=(R // tr,),
        in_specs=[blk, blk, blk, pl.BlockSpec((N_DEV, tr, C), lambda i: (0, i, 0))],
        out_specs=[blk, blk, blk, blk], out_shape=[out, out, out, out],
        compiler_params=_params("parallel"),
    )(w, m, v, parts)


def _adamw_replicated(w, m, v, parts, name):
    def body(w_ref, m_ref, v_ref, p_ref, g_ref, d_ref, mo_ref, vo_ref):
        g = p_ref[0]
        for dev in range(1, N_DEV):
            g = g + p_ref[dev]
        g_ref[...] = g
        d_ref[...], mo_ref[...], vo_ref[...] = _adam_update(w_ref[...], m_ref[...], v_ref[...], g)

    out = jax.ShapeDtypeStruct(w.shape, F32)
    return pl.pallas_call(body, name=name, out_shape=[out, out, out, out], compiler_params=_params())(w, m, v, parts)


def _cols_from_slots(g):
    return g.transpose(1, 0, 2).reshape(g.shape[1], N_DEV * g.shape[2])


def _cols_to_slots(w):
    k, n = w.shape
    return w.reshape(k, N_DEV, n // N_DEV).transpose(1, 0, 2)


def _permute(t, dil):
    c = t.shape[1]
    return t.reshape(S // dil, dil, c).transpose(1, 0, 2).reshape(S, c)


def _unpermute(t, dil):
    c = t.shape[1]
    return t.reshape(dil, S // dil, c).transpose(1, 0, 2).reshape(S, c)


def _group_permute(t):
    return jnp.stack([_permute(t[:, g * B_OUT:(g + 1) * B_OUT], B_DILS[g]) for g in range(3)])


def _same_permute(t):
    return jnp.stack([_permute(t, d) for d in B_DILS])


def _group_unpermute(t):
    return jnp.stack([_unpermute(t[g], B_DILS[g]) for g in range(3)])


SMALL_ROWS = 144


LOSS_SLOT = A_HEADS + 6 * D + 4 * D_FF


def _pack_small(a_b_f, kv_g, mix_g, ffn_g, conv_b, fin_g, loss=None):
    parts = [a_b_f.reshape(-1), kv_g.reshape(-1), mix_g.reshape(-1), ffn_g.reshape(-1),
             conv_b.reshape(-1), fin_g.reshape(-1)] + ([loss.reshape(-1)] if loss is not None else [])
    flat = jnp.concatenate(parts)
    return jnp.pad(flat, (0, SMALL_ROWS * LANES - flat.shape[0])).reshape(SMALL_ROWS, LANES)


def _unpack_small(p):
    flat = p.reshape(-1)
    out, off = [], 0
    for shape in ((1, A_HEADS), (D,), (2, D), (2, D), (2, 2 * D_FF), (D,)):
        size = math.prod(shape)
        out.append(flat[off:off + size].reshape(shape))
        off += size
    return out


def _unpack_late(g):
    half = N_DEV // 2
    up = g[4].reshape(N_DEV, 2, D, -1)
    w_up_a = [up[:half, l].transpose(1, 0, 2).reshape(D, D_FF) for l in range(2)]
    w_up_g = [up[half:, l].transpose(1, 0, 2).reshape(D, D_FF) for l in range(2)]
    w_down = [g[5].reshape(N_DEV, 2, -1, D)[:, l].reshape(D_FF, D) for l in range(2)]
    conv_w = [g[6].reshape(N_DEV, 2, 3, -1)[:, l].transpose(1, 0, 2).reshape(3, 2 * D_FF) for l in range(2)]
    return (g[0].reshape(D, D), _cols_from_slots(g[1]), _cols_from_slots(g[2]), _cols_from_slots(g[3]),
            w_up_a, w_up_g, w_down, conv_w)


def _ffn_slots(dw_up, dw_down_t, dconv_w):
    return [_cols_to_slots(dw_up), dw_down_t.reshape(D, N_DEV, -1).transpose(1, 2, 0), _cols_to_slots(dconv_w)]


def _local_step(x0, target, w_in_pad, late_shards,
                a_b_f, kv_norm_g, mix_norm_g, ffn_norm_g, ffn_conv_b, final_norm_g):
    w_qkv, w_f = w_in_pad[:, :A_QKV], w_in_pad[:, A_QKV:]
    conv_b = ffn_conv_b.reshape(2, 1, 2 * D_FF)
    slopes = jnp.exp2(-8.0 * jnp.arange(1, 25, dtype=F32) / 24)

    def gain(g):
        return g.reshape(1, D)

    (h1,) = _rmsnorm_fwd(x0, [gain(mix_norm_g[0])], "norm_mix0")
    qkv = _matmul(h1, w_qkv, mode="nn", out_dtype=BF16, name="proj_qkv", tm=512, tn=A_QKV)
    z = _matmul(h1, w_f, mode="nn", out_dtype=F32, name="proj_gate", tm=S, tn=LANES)
    z_t = z[:, :A_HEADS].T
    b_f = a_b_f.reshape(A_HEADS, 1)
    c_t = _fox_prep_fwd(z_t, b_f, "fox_prep")
    c_t2 = c_t.reshape(N_PAIRS, 2, S)
    o_a, lse_a, *late = _fox_fwd(qkv, c_t2, "fox_fwd", late_shards)
    w_out, w_q, w_bo, w_kvf, w_up_a, w_up_g, w_down, conv_w = _unpack_late(late)
    x1 = _matmul(o_a, w_out, mode="nn", out_dtype=F32, name="a_out", tm=512, tn=D, res=x0)

    def ffn_fwd(xin, layer):
        (h,) = _rmsnorm_fwd(xin, [gain(ffn_norm_g[layer])], f"norm_ffn{layer}")
        u = (_matmul(h, w_up_a[layer], mode="nn", out_dtype=BF16, name=f"ffn_up_a{layer}", tm=512, tn=D_FF),
             _matmul(h, w_up_g[layer], mode="nn", out_dtype=BF16, name=f"ffn_up_g{layer}", tm=512, tn=D_FF))
        act = _convgate_fwd(*u, conv_w[layer], conv_b[layer], f"convgate{layer}")
        xout = _matmul(act, w_down[layer], mode="nn", out_dtype=F32, name=f"ffn_down{layer}", tm=512, tn=D, res=xin)
        return h, u, act, xout

    h2, u0, act0, x2 = ffn_fwd(x1, 0)
    hk, h3 = _rmsnorm_fwd(x2, [gain(kv_norm_g), gain(mix_norm_g[1])], "norm_kv_mix1")
    kv = _matmul(hk, w_kvf, mode="nn", out_dtype=BF16, name="proj_kv", tm=512, tn=B_KV)
    qb = _matmul(h3, w_q, mode="nn", out_dtype=BF16, name="proj_qb", tm=512, tn=B_Q)
    qp, kp, vp = _group_permute(qb), _group_permute(kv[:, :B_Q]), _group_permute(kv[:, B_Q:])
    og_p, lseg_p = _dil_fwd(qp, kp, vp, slopes, "dil_fwd")
    o_b, lse_b = _dil_merge(_group_unpermute(og_p), _group_unpermute(lseg_p), "dil_merge")
    x3 = _matmul(o_b, w_bo, mode="nn", out_dtype=F32, name="b_out", tm=512, tn=D, res=x2)
    h4, u1, act1, x4 = ffn_fwd(x3, 1)
    loss_blk, dx4, dx4b, dg_final = _final_loss(x4, target, gain(final_norm_g), "final_loss")

    def ffn_bwd(dx, dxb, xin, h, u, act, layer):
        dact = _matmul(dxb, w_down[layer], mode="nt", out_dtype=BF16, name=f"d_act{layer}", tm=512, tn=D_FF)
        dw_down = _matmul_tn(dxb, act, out_dtype=BF16, name=f"dw_down_t{layer}")
        du_a, du_g, dwa, dwg, dba, dbg = _convgate_bwd(*u, conv_w[layer], conv_b[layer], dact, f"convgate_bwd{layer}")
        dw_up = jnp.concatenate(
            [_matmul_tn(h, du_a, out_dtype=BF16, name=f"dw_up_a{layer}"),
             _matmul_tn(h, du_g, out_dtype=BF16, name=f"dw_up_g{layer}")], axis=1)
        dxin, dxinb, dgain = _matmul_norm_bwd([(du_a, w_up_a[layer]), (du_g, w_up_g[layer])], xin,
                                              gain(ffn_norm_g[layer]), dx, f"dh_ffn_norm_bwd{layer}")
        dconv_w = jnp.concatenate([dwa, dwg], axis=1)
        dconv_b = jnp.concatenate([dba, dbg], axis=1)
        return dxin, dxinb, dgain, dw_up, dw_down, dconv_w, dconv_b

    dx3, dx3b, dg_ffn1, dw_up1, dw_down1, dconv_w1, dconv_b1 = ffn_bwd(dx4, dx4b, x3, h4, u1, act1, 1)

    do_b = _matmul(dx3b, w_bo, mode="nt", out_dtype=BF16, name="d_ob", tm=1024, tn=B_OUT)
    dw_bo = _matmul_tn(o_b, dx3b, out_dtype=BF16, name="dw_bo")
    dl_b = _head_rowsum_compact(do_b, o_b, "delta_b")
    slots_up1, slots_down1, slots_conv1 = _ffn_slots(dw_up1, dw_down1, dconv_w1)
    dqp, dkp, dvp, land_down1, land_conv1 = _dil_bwd(
        qp, kp, vp, _same_permute(do_b), _same_permute(lse_b), _same_permute(dl_b), slopes, "dil_bwd",
        [slots_down1, slots_conv1])

    def natural(tp):
        return jnp.concatenate([_unpermute(tp[g], B_DILS[g]) for g in range(3)], axis=1)

    dqb = natural(dqp)
    dkv = jnp.concatenate([natural(dkp), natural(dvp)], axis=1)
    dw_q = _matmul_tn(h3, dqb, out_dtype=BF16, name="dw_q")
    dw_kv = _matmul_tn(hk, dkv, out_dtype=BF16, name="dw_kv")
    dx2, _, dg_mix1 = _matmul_norm_bwd([(dqb, w_q)], x2, gain(mix_norm_g[1]), dx3, "dh_mix1_norm_bwd")
    dx2, dx2b, dg_kv = _matmul_norm_bwd([(dkv, w_kvf)], x2, gain(kv_norm_g), dx2, "dh_kv_norm_bwd")

    dx1, dx1b, dg_ffn0, dw_up0, dw_down0, dconv_w0, dconv_b0 = ffn_bwd(dx2, dx2b, x1, h2, u0, act0, 0)

    do_a = _matmul(dx1b, w_out, mode="nt", out_dtype=BF16, name="d_oa", tm=512, tn=D)
    dw_out = _matmul_tn(o_a, dx1b, out_dtype=BF16, name="dw_out")
    dl_a = _head_rowsum(do_a, o_a, "delta_a")
    dq_a, dk_a, dv_a, dcol, drow, *land = _fox_bwd(
        qkv, do_a, lse_a, dl_a, c_t2, "fox_bwd",
        [dw_out.reshape(N_DEV, D // N_DEV, D), _cols_to_slots(dw_q), _cols_to_slots(dw_bo), _cols_to_slots(dw_kv)]
        + _ffn_slots(dw_up0, dw_down0, dconv_w0) + [slots_up1])
    land_out, land_q, land_bo, land_kv, land_up0, land_down0, land_conv0, land_up1 = land

    def head_sums(t):
        return t.reshape(S, N_PAIRS, 2, HEAD_DIM)[:, :, ::-1, 0].reshape(S, A_HEADS).T

    dz_t, db_f = _fox_prep_bwd(head_sums(drow), head_sums(dcol), z_t, b_f, "fox_prep_bwd")
    dz = jnp.pad(dz_t.T, ((0, 0), (0, LANES - A_HEADS))).astype(BF16)
    dproj = jnp.concatenate([dq_a.astype(BF16), dk_a, dv_a, dz], axis=1)
    dw_in = _matmul_tn(h1, dproj, out_dtype=BF16, name="dw_in")
    grad_x, _, dg_mix0, land_in = _matmul_norm_bwd(
        [(dproj, w_in_pad)], x0, gain(mix_norm_g[0]), dx1, "dh_mix0_norm_bwd",
        scatter=[_cols_to_slots(dw_in[:, :A_QKV + A_HEADS])])

    dg_mix = jnp.concatenate([dg_mix0, dg_mix1], axis=0)
    dg_ffn = jnp.concatenate([dg_ffn0, dg_ffn1], axis=0)
    dconv_b = jnp.concatenate([dconv_b0, dconv_b1], axis=0)
    small_part = _pack_small(db_f, dg_kv, dg_mix, dg_ffn, dconv_b, dg_final, loss=loss_blk[0, 0])
    _, (small_parts,) = _final_exchange([], [small_part], "gather_small_grads")
    landed = [land_in, land_out, land_q, land_bo, land_kv, land_up0, land_up1, land_down0, land_down1,
              land_conv0, land_conv1]
    return loss_blk, grad_x, landed, small_parts


def kernel(x, a_w_in, a_b_f, a_w_out, b_w_q, b_w_out, kv_norm_g, w_kv, mix_norm_g, ffn_norm_g, ffn_w_up, ffn_conv_w, ffn_conv_b, ffn_w_down, final_norm_g, loss_target, m_a_w_in, m_a_b_f, m_a_w_out, m_b_w_q, m_b_w_out, m_kv_norm_g, m_w_kv, m_mix_norm_g, m_ffn_norm_g, m_ffn_w_up, m_ffn_conv_w, m_ffn_conv_b, m_ffn_w_down, m_final_norm_g, v_a_w_in, v_a_b_f, v_a_w_out, v_b_w_q, v_b_w_out, v_kv_norm_g, v_w_kv, v_mix_norm_g, v_ffn_norm_g, v_ffn_w_up, v_ffn_conv_w, v_ffn_conv_b, v_ffn_w_down, v_final_norm_g):
    def shards(a_w_in, a_w_out, b_w_q, b_w_out, w_kv, ffn_w_up, ffn_w_down, ffn_conv_w):
        return [a_w_in[0], a_w_out[0], b_w_q[0], b_w_out[0], w_kv, ffn_w_up[0], ffn_w_up[1],
                ffn_w_down[0], ffn_w_down[1], ffn_conv_w[0], ffn_conv_w[1]]

    w_loc = shards(a_w_in, a_w_out, b_w_q, b_w_out, w_kv, ffn_w_up, ffn_w_down, ffn_conv_w)
    m_loc = shards(m_a_w_in, m_a_w_out, m_b_w_q, m_b_w_out, m_w_kv, m_ffn_w_up, m_ffn_w_down, m_ffn_conv_w)
    v_loc = shards(v_a_w_in, v_a_w_out, v_b_w_q, v_b_w_out, v_w_kv, v_ffn_w_up, v_ffn_w_down, v_ffn_conv_w)

    (g_in,) = _all_gather([a_w_in[0].astype(BF16)], "gather_a_w_in")
    w_in = _cols_from_slots(g_in)
    w_in_pad = jnp.pad(w_in, ((0, 0), (0, A_PROJ_PAD - w_in.shape[1])))
    late_shards = [a_w_out[0].astype(BF16), b_w_q[0].astype(BF16), b_w_out[0].astype(BF16), w_kv.astype(BF16),
                   ffn_w_up.reshape(2 * D, -1).astype(BF16), ffn_w_down.reshape(-1, D).astype(BF16),
                   ffn_conv_w.reshape(6, -1)]

    loss_blk, grad_x, landed, small_parts = _local_step(
        x[0], loss_target[0], w_in_pad, late_shards,
        a_b_f, kv_norm_g, mix_norm_g, ffn_norm_g, ffn_conv_b, final_norm_g)

    big = [_adamw_sharded(w_loc[k], m_loc[k], v_loc[k], landed[k], f"adamw{k}") for k in range(11)]

    small = _adamw_replicated(
        _pack_small(a_b_f, kv_norm_g, mix_norm_g, ffn_norm_g, ffn_conv_b, final_norm_g),
        _pack_small(m_a_b_f, m_kv_norm_g, m_mix_norm_g, m_ffn_norm_g, m_ffn_conv_b, m_final_norm_g),
        _pack_small(v_a_b_f, v_kv_norm_g, v_mix_norm_g, v_ffn_norm_g, v_ffn_conv_b, v_final_norm_g),
        small_parts, "adamw_small")

    loss = small[0].reshape(-1)[LOSS_SLOT]

    def assemble(kind):
        b = [r[kind] for r in big]
        s_abf, s_kv, s_mix, s_ffn, s_cb, s_fin = _unpack_small(small[kind])
        return [b[0][None], s_abf, b[1][None], b[2][None], b[3][None], s_kv, b[4], s_mix, s_ffn,
                jnp.stack([b[5], b[6]]), jnp.stack([b[9], b[10]]), s_cb, jnp.stack([b[7], b[8]]), s_fin]

    return (loss, grad_x[None], *assemble(0), *assemble(1), *assemble(2), *assemble(3))
```

```python
import math

import jax
import jax.numpy as jnp
from jax import lax
from jax.experimental import pallas as pl
from jax.experimental.pallas import tpu as pltpu

F32 = jnp.float32
BF16 = jnp.bfloat16

S = 4096
D = 1024
N_DEV = 8
A_HEADS = 16
HEAD_DIM = 64
A_QKV = 3072
A_PROJ_PAD = 3200
B_Q = 1536
B_OUT = 512
B_KV = 3072
B_W = 128
B_DILS = (1, 4, 16)
D_FF = 2816
RMS_EPS = 1e-6
SCALE = HEAD_DIM ** -0.5
NEG = -1e30

ADAM_LR = 0.001
ADAM_B1 = 0.9
ADAM_B2 = 0.999
ADAM_EPS = 1e-08
ADAM_WD = 0.01
ADAM_STEP = 10

LANES = 128
VMEM_LIMIT = 56 * 1024 * 1024
MESH = pl.DeviceIdType.MESH
ANY = pl.BlockSpec(memory_space=pl.ANY)

NT_DIMS = (((1,), (1,)), ((), ()))
TN_DIMS = (((0,), (0,)), ((), ()))
NN_DIMS = (((1,), (0,)), ((), ()))


def _params(*sem):
    return pltpu.CompilerParams(dimension_semantics=sem if sem else None, vmem_limit_bytes=VMEM_LIMIT)


def _dot(a, b, dims=NN_DIMS):
    return lax.dot_general(a, b, dims, preferred_element_type=F32)


def _split_dot(x, mat, pieces):
    out = None
    rem = x
    for _ in range(pieces):
        part = rem.astype(BF16)
        rem = rem - part.astype(F32)
        d = _dot(part, mat)
        out = d if out is None else out + d
    return out


def _pick(n, prefs):
    for p in prefs:
        if n % p == 0:
            return p
    return n


def _gather_phases(ins, outs, sems):
    n = len(ins)
    if n == 0:
        return (lambda: None,) * 3
    send_sems, recv_sems, local_sems = sems
    x, y, c = lax.axis_index("x"), lax.axis_index("y"), lax.axis_index("c")
    me, sibling = (x, y, c), (x, y, 1 - c)
    chips = [(1 - x, y), (x, 1 - y), (1 - x, 1 - y)]

    def slot(a, px, py, pc):
        return outs[a].at[4 * px + 2 * py + pc]

    def copy(a, k, block, to, src=None):
        return pltpu.make_async_remote_copy(
            src_ref=slot(a, *block) if src is None else src, dst_ref=slot(a, *block),
            send_sem=send_sems.at[a, k], recv_sem=recv_sems.at[a, k],
            device_id=to, device_id_type=MESH)

    mine = [pltpu.make_async_copy(ins[a], slot(a, *me), local_sems.at[a]) for a in range(n)]
    first = []
    for a in range(n):
        first.append(copy(a, 0, me, sibling, src=ins[a]))
        first += [copy(a, 1 + j, me, (*chip, c), src=ins[a]) for j, chip in enumerate(chips)]
    passed = [copy(a, 4 + j, (*chip, c), sibling) for j, chip in enumerate(chips) for a in range(n)]

    def start():
        for cp in mine + first:
            cp.start()

    def forward():
        k = 0
        for j, chip in enumerate(chips):
            for a in range(n):
                copy(a, 1 + j, (*chip, c), me).wait_recv()
                passed[k].start()
                k += 1

    def finish():
        for a in range(n):
            copy(a, 0, sibling, me).wait_recv()
            for j, chip in enumerate(chips):
                copy(a, 4 + j, (*chip, 1 - c), me).wait_recv()
        for cp in first + passed:
            cp.wait_send()
        for cp in mine:
            cp.wait()

    return start, forward, finish


def _all_gather(arrays, name):
    n = len(arrays)

    def body(*refs):
        for phase in _gather_phases(refs[:n], refs[n:2 * n], refs[2 * n:]):
            phase()

    return pl.pallas_call(
        body, name=name,
        out_shape=[jax.ShapeDtypeStruct((N_DEV,) + a.shape, a.dtype) for a in arrays],
        in_specs=[ANY] * n, out_specs=[ANY] * n,
        scratch_shapes=[pltpu.SemaphoreType.DMA((n, 7)), pltpu.SemaphoreType.DMA((n, 7)),
                        pltpu.SemaphoreType.DMA((n,))],
    )(*arrays)


PEER_FLIPS = [(dx, dy, dc) for dx in (0, 1) for dy in (0, 1) for dc in (0, 1) if (dx, dy, dc) != (0, 0, 0)]


def _exchange_copies(ins, outs, sems, scatter):
    if not ins:
        return []
    send_sems, recv_sems, local_sems = sems
    x, y, c = lax.axis_index("x"), lax.axis_index("y"), lax.axis_index("c")
    me = 4 * x + 2 * y + c
    copies = []
    for a in range(len(ins)):
        copies.append(pltpu.make_async_copy(ins[a].at[me] if scatter else ins[a], outs[a].at[me], local_sems.at[a]))
        for k, (dx, dy, dc) in enumerate(PEER_FLIPS):
            px, py, pc = (1 - x if dx else x), (1 - y if dy else y), (1 - c if dc else c)
            copies.append(pltpu.make_async_remote_copy(
                src_ref=ins[a].at[4 * px + 2 * py + pc] if scatter else ins[a], dst_ref=outs[a].at[me],
                send_sem=send_sems.at[a, k], recv_sem=recv_sems.at[a, k],
                device_id=(px, py, pc), device_id_type=MESH))
    return copies


def _exchange_scratch(n):
    if n == 0:
        return []
    return [pltpu.SemaphoreType.DMA((n, 7)), pltpu.SemaphoreType.DMA((n, 7)), pltpu.SemaphoreType.DMA((n,))]


def _exchange_shapes(arrays, scatter):
    return [jax.ShapeDtypeStruct((N_DEV,) + (a.shape[1:] if scatter else a.shape), a.dtype) for a in arrays]


def _final_exchange(scatter, gather, name):
    ns, ng = len(scatter), len(gather)

    def body(*refs):
        ins, outs, sems = refs[:ns + ng], refs[ns + ng:2 * (ns + ng)], refs[2 * (ns + ng):]
        n_sems = len(_exchange_scratch(ns))
        copies = (_exchange_copies(ins[:ns], outs[:ns], sems[:n_sems], True)
                  + _exchange_copies(ins[ns:], outs[ns:], sems[n_sems:], False))
        for cp in copies:
            cp.start()
        for cp in copies:
            cp.wait()

    res = pl.pallas_call(
        body, name=name, out_shape=_exchange_shapes(scatter, True) + _exchange_shapes(gather, False),
        in_specs=[ANY] * (ns + ng), out_specs=[ANY] * (ns + ng),
        scratch_shapes=_exchange_scratch(ns) + _exchange_scratch(ng),
    )(*scatter, *gather)
    return res[:ns], res[ns:]


MM_ROWS = 512
MM_COLS = 1024


def _matmul(a, b, *, mode, out_dtype, name, tm, tn, res=None):
    if mode == "nn":
        (M, K), (K2, N) = a.shape, b.shape
    else:
        (M, K), (N, K2) = a.shape, b.shape
    assert K == K2, (a.shape, b.shape, mode)
    tm, tn = min(tm, M), min(tn, N)
    sm = min(tm, MM_ROWS)
    sn = tn if tn <= MM_COLS else _pick(tn, (512, 256, 128))
    assert M % tm == 0 and N % tn == 0 and tm % sm == 0, (M, N, K, tm, tn)
    dims = NN_DIMS if mode == "nn" else NT_DIMS
    a_spec = pl.BlockSpec((tm, K), lambda i, j: (i, 0))
    if mode == "nt":
        b_spec = pl.BlockSpec((tn, K), lambda i, j: (j, 0))
    else:
        b_spec = pl.BlockSpec((K, tn), lambda i, j: (0, j))
    o_spec = pl.BlockSpec((tm, tn), lambda i, j: (i, j))
    has_res = res is not None

    def body(*refs):
        a_ref, b_ref = refs[0], refs[1]
        r_ref = refs[2] if has_res else None
        o_ref = refs[2 + has_res]

        def chunk(r, carry):
            rows = pl.ds(pl.multiple_of(r * sm, sm), sm)
            av = a_ref[rows, :]
            for c0 in range(0, tn, sn):
                bv = b_ref[c0:c0 + sn, :] if mode == "nt" else b_ref[:, c0:c0 + sn]
                total = _dot(av, bv, dims)
                if has_res:
                    total = total + r_ref[rows, c0:c0 + sn]
                o_ref[rows, c0:c0 + sn] = total.astype(out_dtype)
            return carry

        lax.fori_loop(0, tm // sm, chunk, 0)

    return pl.pallas_call(
        body, name=name, grid=(M // tm, N // tn),
        out_shape=jax.ShapeDtypeStruct((M, N), out_dtype),
        in_specs=[a_spec, b_spec] + ([o_spec] if has_res else []),
        out_specs=o_spec,
        compiler_params=_params("parallel", "parallel"),
    )(*((a, b, res) if has_res else (a, b)))


def _matmul_tn(a, b, *, out_dtype, name, tk=512, sm=256):
    (K, M), (K2, N) = a.shape, b.shape
    assert K == K2 and K % tk == 0 and M % sm == 0, (a.shape, b.shape)
    nk = K // tk

    def body(a_ref, b_ref, o_ref, acc_ref):
        k = pl.program_id(0)

        @pl.when(k == 0)
        def _():
            acc_ref[...] = jnp.zeros_like(acc_ref)

        def chunk(mi, carry):
            cols = pl.ds(pl.multiple_of(mi * sm, sm), sm)
            acc_ref[cols, :] += _dot(a_ref[:, cols].T, b_ref[...])
            return carry

        lax.fori_loop(0, M // sm, chunk, 0)

        @pl.when(k == nk - 1)
        def _():
            def emit(mi, carry):
                rows = pl.ds(pl.multiple_of(mi * sm, sm), sm)
                o_ref[rows, :] = acc_ref[rows, :].astype(out_dtype)
                return carry
            lax.fori_loop(0, M // sm, emit, 0)

    return pl.pallas_call(
        body, name=name, grid=(nk,),
        out_shape=jax.ShapeDtypeStruct((M, N), out_dtype),
        in_specs=[pl.BlockSpec((tk, M), lambda k: (k, 0)), pl.BlockSpec((tk, N), lambda k: (k, 0))],
        out_specs=pl.BlockSpec((M, N), lambda k: (0, 0)),
        scratch_shapes=[pltpu.VMEM((M, N), F32)],
        compiler_params=_params("arbitrary"),
    )(a, b)


def _rmsnorm_fwd(x, gains, name, tr=512):
    n = len(gains)

    def body(*refs):
        x_ref = refs[0]
        xv = x_ref[...]
        r = lax.rsqrt(jnp.mean(xv * xv, axis=-1, keepdims=True) + RMS_EPS)
        y = xv * r
        for a in range(n):
            refs[1 + n + a][...] = (y * refs[1 + a][...]).astype(BF16)

    row = pl.BlockSpec((tr, D), lambda i: (i, 0))
    gain = pl.BlockSpec((1, D), lambda i: (0, 0))
    return pl.pallas_call(
        body, name=name, grid=(S // tr,),
        out_shape=[jax.ShapeDtypeStruct((S, D), BF16)] * n,
        in_specs=[row] + [gain] * n, out_specs=[row] * n,
        compiler_params=_params("parallel"),
    )(x, *gains)


def _matmul_norm_bwd(pairs, x, g, dres, name, scatter=(), tm=512):
    M = x.shape[0]
    n_p, n_ex = len(pairs), len(scatter)
    n_in = 2 * n_p + 3
    steps = M // tm

    def body(*refs):
        x_ref, g_ref, dres_ref = refs[2 * n_p:n_in]
        dx_ref, dxb_ref, dg_ref = refs[n_in + n_ex:n_in + n_ex + 3]
        exchange = (refs[n_in:n_in + n_ex], refs[n_in + n_ex + 3:n_in + 2 * n_ex + 3], refs[n_in + 2 * n_ex + 3:], True)

        @pl.when(pl.program_id(0) == 0)
        def _():
            for cp in _exchange_copies(*exchange):
                cp.start()

        dyv = _dot(refs[0][...], refs[1][...], NT_DIMS)
        for p in range(1, n_p):
            dyv = dyv + _dot(refs[2 * p][...], refs[2 * p + 1][...], NT_DIMS)
        xv = x_ref[...]
        r = lax.rsqrt(jnp.mean(xv * xv, axis=-1, keepdims=True) + RMS_EPS)
        xhat = xv * r
        dxhat = dyv * g_ref[...]
        mean_term = jnp.mean(dxhat * xhat, axis=-1, keepdims=True)
        dx = r * (dxhat - xhat * mean_term) + dres_ref[...]
        dx_ref[...] = dx
        dxb_ref[...] = dx.astype(BF16)
        part = jnp.sum(dyv * xhat, axis=0, keepdims=True)

        @pl.when(pl.program_id(0) == 0)
        def _():
            dg_ref[...] = part

        @pl.when(pl.program_id(0) > 0)
        def _():
            dg_ref[...] += part

        @pl.when(pl.program_id(0) == steps - 1)
        def _():
            for cp in _exchange_copies(*exchange):
                cp.wait()

    row = pl.BlockSpec((tm, D), lambda i: (i, 0))
    gain = pl.BlockSpec((1, D), lambda i: (0, 0))
    pair_specs, operands = [], []
    for a, b in pairs:
        assert a.shape == (M, b.shape[1]) and b.shape[0] == D, (a.shape, b.shape)
        pair_specs += [pl.BlockSpec((tm, a.shape[1]), lambda i: (i, 0)), pl.BlockSpec(b.shape, lambda i: (0, 0))]
        operands += [a, b]
    return pl.pallas_call(
        body, name=name, grid=(steps,),
        out_shape=[jax.ShapeDtypeStruct((M, D), F32), jax.ShapeDtypeStruct((M, D), BF16),
                   jax.ShapeDtypeStruct((1, D), F32)] + _exchange_shapes(scatter, True),
        in_specs=pair_specs + [row, gain, row] + [ANY] * n_ex,
        out_specs=[row, row, gain] + [ANY] * n_ex,
        scratch_shapes=_exchange_scratch(n_ex),
        compiler_params=_params("arbitrary"),
    )(*operands, x, g, dres, *scatter)


def _final_loss(x, target, g, name, tr=512):
    def body(x_ref, t_ref, g_ref, loss_ref, dx_ref, dxb_ref, dg_ref):
        xv = x_ref[...]
        gv = g_ref[...]
        r = lax.rsqrt(jnp.mean(xv * xv, axis=-1, keepdims=True) + RMS_EPS)
        xhat = xv * r
        err = xhat * gv - t_ref[...]
        row_loss = jnp.mean(err * err, axis=-1, keepdims=True)
        lpart = 0.5 * jnp.sum(row_loss, axis=0, keepdims=True)
        dyv = err / D
        dxhat = dyv * gv
        mean_term = jnp.mean(dxhat * xhat, axis=-1, keepdims=True)
        dx = r * (dxhat - xhat * mean_term)
        dx_ref[...] = dx
        dxb_ref[...] = dx.astype(BF16)
        gpart = jnp.sum(dyv * xhat, axis=0, keepdims=True)

        @pl.when(pl.program_id(0) == 0)
        def _():
            dg_ref[...] = gpart
            loss_ref[...] = jnp.broadcast_to(lpart, loss_ref.shape)

        @pl.when(pl.program_id(0) > 0)
        def _():
            dg_ref[...] += gpart
            loss_ref[...] += jnp.broadcast_to(lpart, loss_ref.shape)

    row = pl.BlockSpec((tr, D), lambda i: (i, 0))
    gain = pl.BlockSpec((1, D), lambda i: (0, 0))
    lspec = pl.BlockSpec((8, LANES), lambda i: (0, 0))
    return pl.pallas_call(
        body, name=name, grid=(S // tr,),
        out_shape=[jax.ShapeDtypeStruct((8, LANES), F32), jax.ShapeDtypeStruct((S, D), F32),
                   jax.ShapeDtypeStruct((S, D), BF16), jax.ShapeDtypeStruct((1, D), F32)],
        in_specs=[row, row, gain], out_specs=[lspec, row, row, gain],
        compiler_params=_params("arbitrary"),
    )(x, target, g)


CONV_TR = 256
CONV_TC = D_FF
CONV_NJ = D_FF // CONV_TC
HALO = 16


def _causal_taps(cur_ref, prev_ref, first):
    xv = cur_ref[...].astype(F32)
    pv = prev_ref[...].astype(F32)
    p1 = jnp.where(first, 0.0, pv[HALO - 1:HALO, :])
    p2 = jnp.where(first, 0.0, pv[HALO - 2:HALO - 1, :])
    r1, r2 = pltpu.roll(xv, 1, 0), pltpu.roll(xv, 2, 0)
    row = lax.broadcasted_iota(jnp.int32, (8, xv.shape[1]), 0)
    xm1 = jnp.concatenate([jnp.where(row == 0, p1, r1[0:8]), r1[8:]], axis=0)
    xm2 = jnp.concatenate([jnp.where(row == 0, p2, jnp.where(row == 1, p1, r2[0:8])), r2[8:]], axis=0)
    return xv, xm1, xm2


def _conv_specs():
    def prev_row(i):
        return jnp.maximum(i * (CONV_TR // HALO) - 1, 0)
    ua = pl.BlockSpec((CONV_TR, CONV_TC), lambda i, j: (i, j))
    ug = ua
    pa = pl.BlockSpec((HALO, CONV_TC), lambda i, j: (prev_row(i), j))
    pg = pa
    wa = pl.BlockSpec((3, CONV_TC), lambda i, j: (0, j))
    wg = pl.BlockSpec((3, CONV_TC), lambda i, j: (0, j + CONV_NJ))
    ba = pl.BlockSpec((1, CONV_TC), lambda i, j: (0, j))
    bg = pl.BlockSpec((1, CONV_TC), lambda i, j: (0, j + CONV_NJ))
    return [ua, pa, ug, pg, wa, wg, ba, bg]


def _convgate_fwd(u_a, u_g, w, b, name):
    def body(ua, pa, ug, pg, wa, wg, ba, bg, o_ref):
        first = pl.program_id(0) == 0
        x0, x1, x2 = _causal_taps(ua, pa, first)
        ac = wa[0:1, :] * x2 + wa[1:2, :] * x1 + wa[2:3, :] * x0 + ba[...]
        x0, x1, x2 = _causal_taps(ug, pg, first)
        gc = wg[0:1, :] * x2 + wg[1:2, :] * x1 + wg[2:3, :] * x0 + bg[...]
        sg = 0.5 * jnp.tanh(0.5 * gc) + 0.5
        o_ref[...] = (gc * sg * ac).astype(BF16)

    return pl.pallas_call(
        body, name=name, grid=(S // CONV_TR, CONV_NJ),
        out_shape=jax.ShapeDtypeStruct((S, D_FF), BF16),
        in_specs=_conv_specs(),
        out_specs=pl.BlockSpec((CONV_TR, CONV_TC), lambda i, j: (i, j)),
        compiler_params=_params("parallel", "parallel"),
    )(u_a, u_a, u_g, u_g, w, w, b, b)


def _anticausal_conv(d, nxt_ref, w_ref, last):
    n1 = jnp.where(last, 0.0, nxt_ref[0:1, :])
    n2 = jnp.where(last, 0.0, nxt_ref[1:2, :])
    r1, r2 = pltpu.roll(d, CONV_TR - 1, 0), pltpu.roll(d, CONV_TR - 2, 0)
    row = lax.broadcasted_iota(jnp.int32, (8, d.shape[1]), 0)
    cut = CONV_TR - 8
    dp1 = jnp.concatenate([r1[:cut], jnp.where(row == 7, n1, r1[cut:])], axis=0)
    dp2 = jnp.concatenate([r2[:cut], jnp.where(row == 7, n2, jnp.where(row == 6, n1, r2[cut:]))], axis=0)
    return w_ref[2:3, :] * d + w_ref[1:2, :] * dp1 + w_ref[0:1, :] * dp2


def _convgate_bwd(u_a, u_g, w, b, dact, name):
    n_i = S // CONV_TR

    def body(ua, pa, ug, pg, wa, wg, ba, bg, d_ref, dua_ref, dug_ref, dwa_ref, dwg_ref, dba_ref, dbg_ref,
             nxt_a, nxt_g):
        i = pl.program_id(1)
        last = i == 0
        first = i == n_i - 1
        a0, a1, a2 = _causal_taps(ua, pa, first)
        ac = wa[0:1, :] * a2 + wa[1:2, :] * a1 + wa[2:3, :] * a0 + ba[...]
        g0, g1, g2 = _causal_taps(ug, pg, first)
        gc = wg[0:1, :] * g2 + wg[1:2, :] * g1 + wg[2:3, :] * g0 + bg[...]
        sg = 0.5 * jnp.tanh(0.5 * gc) + 0.5
        dact_v = d_ref[...].astype(F32)
        da = dact_v * (gc * sg)
        dg = dact_v * ac * (sg * (1.0 + gc * (1.0 - sg)))
        dua_ref[...] = _anticausal_conv(da, nxt_a, wa, last).astype(BF16)
        dug_ref[...] = _anticausal_conv(dg, nxt_g, wg, last).astype(BF16)
        nxt_a[...] = da[0:8]
        nxt_g[...] = dg[0:8]

        def col(v):
            return jnp.sum(v, axis=0, keepdims=True)

        parts = [col(da * a2), col(da * a1), col(da * a0), col(dg * g2), col(dg * g1), col(dg * g0),
                 col(da), col(dg)]

        @pl.when(last)
        def _():
            for k in range(3):
                dwa_ref[k:k + 1, :] = parts[k]
                dwg_ref[k:k + 1, :] = parts[3 + k]
            dba_ref[...] = parts[6]
            dbg_ref[...] = parts[7]

        @pl.when(i > 0)
        def _():
            for k in range(3):
                dwa_ref[k:k + 1, :] += parts[k]
                dwg_ref[k:k + 1, :] += parts[3 + k]
            dba_ref[...] += parts[6]
            dbg_ref[...] += parts[7]

    def swap(spec):
        return pl.BlockSpec(spec.block_shape, lambda j, i, f=spec.index_map: f(n_i - 1 - i, j))

    blk = pl.BlockSpec((CONV_TR, CONV_TC), lambda j, i: (n_i - 1 - i, j))
    w3 = pl.BlockSpec((3, CONV_TC), lambda j, i: (0, j))
    b1 = pl.BlockSpec((1, CONV_TC), lambda j, i: (0, j))
    return pl.pallas_call(
        body, name=name, grid=(CONV_NJ, n_i),
        out_shape=[jax.ShapeDtypeStruct((S, D_FF), BF16), jax.ShapeDtypeStruct((S, D_FF), BF16),
                   jax.ShapeDtypeStruct((3, D_FF), F32), jax.ShapeDtypeStruct((3, D_FF), F32),
                   jax.ShapeDtypeStruct((1, D_FF), F32), jax.ShapeDtypeStruct((1, D_FF), F32)],
        in_specs=[swap(s) for s in _conv_specs()] + [blk],
        out_specs=[blk, blk, w3, w3, b1, b1],
        scratch_shapes=[pltpu.VMEM((8, CONV_TC), F32), pltpu.VMEM((8, CONV_TC), F32)],
        compiler_params=_params("arbitrary", "arbitrary"),
    )(u_a, u_a, u_g, u_g, w, w, b, b, dact)


FOX_T = 512
FOX_TQ, FOX_TK = 512, 512
FOX_FORWARD_AT = 4
N_PAIRS = A_HEADS // 2


def _lane_masks():
    lane = lax.broadcasted_iota(jnp.int32, (1, LANES), 1)
    return lane, (lane < HEAD_DIM, lane >= HEAD_DIM)


def _fox_prep_fwd(z_t, b, name):
    def body(z_ref, b_ref, c_ref):
        r = lax.broadcasted_iota(jnp.int32, (LANES, LANES), 0)
        cc = lax.broadcasted_iota(jnp.int32, (LANES, LANES), 1)
        upper = (r <= cc).astype(BF16)
        carry = jnp.zeros((A_HEADS, 1), F32)
        for blk in range(S // LANES):
            sl = slice(blk * LANES, (blk + 1) * LANES)
            z = z_ref[:, sl] + b_ref[...]
            lf = jnp.minimum(z, 0.0) - jnp.log(1.0 + jnp.exp(-jnp.abs(z)))
            cs = _split_dot(lf, upper, 3) + carry
            c_ref[:, sl] = cs
            carry = cs[:, LANES - 1:LANES]

    return pl.pallas_call(
        body, name=name, out_shape=jax.ShapeDtypeStruct((A_HEADS, S), F32),
        compiler_params=_params(),
    )(z_t, b)


def _fox_prep_bwd(drow_t, dcol_t, z_t, b, name):
    def body(dr_ref, dc_ref, z_ref, b_ref, dz_ref, db_ref):
        r = lax.broadcasted_iota(jnp.int32, (LANES, LANES), 0)
        cc = lax.broadcasted_iota(jnp.int32, (LANES, LANES), 1)
        lower = (r >= cc).astype(BF16)
        carry = jnp.zeros((A_HEADS, 1), F32)
        db = jnp.zeros((A_HEADS, 1), F32)
        for blk in reversed(range(S // LANES)):
            sl = slice(blk * LANES, (blk + 1) * LANES)
            rc = _split_dot(dr_ref[:, sl] - dc_ref[:, sl], lower, 3) + carry
            carry = rc[:, 0:1]
            z = z_ref[:, sl] + b_ref[...]
            dz = rc / (1.0 + jnp.exp(z))
            dz_ref[:, sl] = dz
            db = db + jnp.sum(dz, axis=1, keepdims=True)
        db_ref[...] = db

    return pl.pallas_call(
        body, name=name,
        out_shape=[jax.ShapeDtypeStruct((A_HEADS, S), F32), jax.ShapeDtypeStruct((A_HEADS, 1), F32)],
        compiler_params=_params(),
    )(drow_t, dcol_t, z_t, b)


def _fox_fwd(qkv, c_t2, name, gather):
    tq, tk = FOX_TQ, FOX_TK

    n = len(gather)

    def body(*refs):
        q_ref, k_ref, v_ref, ct_ref = refs[:4]
        o_ref, lse_ref = refs[4 + n:6 + n]
        s_scr, p_scr, acc_scr = refs[-5:-3], refs[-3:-1], refs[-1]
        qi = pl.program_id(1)

        gather_start, gather_forward, gather_finish = _gather_phases(
            refs[4:4 + n], refs[6 + n:6 + 2 * n], refs[6 + 2 * n:len(refs) - 5])

        @pl.when(jnp.logical_and(pl.program_id(0) == 0, qi == 0))
        def _():
            gather_start()

        @pl.when(jnp.logical_and(pl.program_id(0) == FOX_FORWARD_AT, qi == 0))
        def _():
            gather_forward()

        n_full = jnp.right_shift(qi, (tk // tq).bit_length() - 1)
        lane, masks = _lane_masks()
        q = q_ref[...] * SCALE
        qs = [jnp.where(masks[e], q, jnp.zeros_like(q)) for e in range(2)]

        def scores_into(j, slot):
            start = pl.multiple_of(j * tk, tk)
            kb = k_ref[pl.ds(start, tk), :]
            for e in range(2):
                s_scr[slot][e] = _dot(qs[e], kb, NT_DIMS) - ct_ref[e:e + 1, pl.ds(start, tk)]

        def softmax_of(slot, m, masked):
            m_new, alpha = [], []
            for e in range(2):
                s = s_scr[slot][e]
                if masked:
                    rows = lax.broadcasted_iota(jnp.int32, (tq, tk), 0) + (qi * tq - n_full * tk)
                    cols = lax.broadcasted_iota(jnp.int32, (tq, tk), 1)
                    s = jnp.where(cols <= rows, s, NEG)
                m_new.append(jnp.maximum(m[e], jnp.max(s, axis=1, keepdims=True)))
                p_scr[slot][e] = jnp.exp(s - m_new[e]).astype(BF16)
                alpha.append(jnp.exp(m[e] - m_new[e]))
            return tuple(m_new), tuple(alpha)

        def values_of(j, slot, alpha):
            start = pl.multiple_of(j * tk, tk)
            vb = v_ref[pl.ds(start, tk), :]
            for e in range(2):
                acc_scr[e] = (alpha[e] * acc_scr[e]
                              + _dot(p_scr[slot][e], jnp.where(masks[e], vb, jnp.ones_like(vb))))

        def stage(j, cur, nxt, carry):
            m, a_prev = carry
            scores_into(j + 1, nxt)
            values_of(jnp.maximum(j - 1, 0), nxt, a_prev)
            return softmax_of(cur, m, False)

        def finish(cur, nxt, carry):
            m, a_prev = carry
            values_of(jnp.maximum(n_full - 1, 0), nxt, a_prev)
            (m0, m1), alpha = softmax_of(cur, m, True)
            values_of(n_full, cur, alpha)
            l0 = acc_scr[0][:, HEAD_DIM:HEAD_DIM + 1]
            l1 = acc_scr[1][:, 0:1]
            o_ref[...] = jnp.where(masks[0], acc_scr[0] / l0, acc_scr[1] / l1).astype(BF16)
            lse_ref[...] = jnp.where(masks[0], m0 + jnp.log(l0), m1 + jnp.log(l1))

        scores_into(0, 0)
        for e in range(2):
            p_scr[1][e] = jnp.zeros((tq, tk), BF16)
            acc_scr[e] = jnp.zeros((tq, LANES), F32)
        two = lambda x: (x, x)
        init = (two(jnp.full((tq, 1), NEG, F32)), two(jnp.ones((tq, 1), F32)))

        def two_stages(jj, carry):
            return stage(2 * jj + 1, 1, 0, stage(2 * jj, 0, 1, carry))

        carry = lax.fori_loop(0, jnp.right_shift(n_full, 1), two_stages, init)
        odd = jnp.bitwise_and(n_full, 1) == 1

        @pl.when(odd)
        def _():
            finish(1, 0, stage(n_full - 1, 0, 1, carry))

        @pl.when(jnp.logical_not(odd))
        def _():
            finish(0, 1, carry)

        @pl.when(jnp.logical_and(pl.program_id(0) == N_PAIRS - 1, qi == S // tq - 1))
        def _():
            gather_finish()

    qspec = pl.BlockSpec((tq, LANES), lambda h, i: (i, h))
    return pl.pallas_call(
        body, name=name, grid=(N_PAIRS, S // tq),
        out_shape=[jax.ShapeDtypeStruct((S, D), BF16), jax.ShapeDtypeStruct((S, D), F32)]
        + _exchange_shapes(gather, False),
        in_specs=[qspec,
                  pl.BlockSpec((S, LANES), lambda h, i: (0, N_PAIRS + h)),
                  pl.BlockSpec((S, LANES), lambda h, i: (0, 2 * N_PAIRS + h)),
                  pl.BlockSpec((None, 2, S), lambda h, i: (h, 0, 0))] + [ANY] * n,
        out_specs=[qspec, qspec] + [ANY] * n,
        scratch_shapes=_exchange_scratch(n) + [
            pltpu.VMEM((2, tq, tk), F32), pltpu.VMEM((2, tq, tk), F32),
            pltpu.VMEM((2, tq, tk), BF16), pltpu.VMEM((2, tq, tk), BF16),
            pltpu.VMEM((2, tq, LANES), F32)],
        compiler_params=_params("arbitrary", "arbitrary"),
    )(qkv, qkv, qkv, c_t2, *gather)


def _head_rowsum(a, b, name, tr=512):
    C = a.shape[1]

    def body(a_ref, b_ref, o_ref):
        r = lax.broadcasted_iota(jnp.int32, (LANES, LANES), 0) < HEAD_DIM
        cc = lax.broadcasted_iota(jnp.int32, (LANES, LANES), 1) < HEAD_DIM
        same_head = (r == cc).astype(BF16)
        for blk in range(C // LANES):
            sl = slice(blk * LANES, (blk + 1) * LANES)
            prod = a_ref[:, sl].astype(F32) * b_ref[:, sl].astype(F32)
            o_ref[:, sl] = _split_dot(prod, same_head, 2)

    row = pl.BlockSpec((tr, C), lambda i: (i, 0))
    return pl.pallas_call(
        body, name=name, grid=(S // tr,), out_shape=jax.ShapeDtypeStruct((S, C), F32),
        in_specs=[row, row], out_specs=row, compiler_params=_params("parallel"),
    )(a, b)


def _fox_bwd(qkv, do, lse, delta, c_t2, name, scatter):
    t = FOX_T
    nq = S // t

    n = len(scatter)

    def body(*refs):
        q_ref, k_ref, v_ref, do_ref, lse_ref, dl_ref, ct_ref = refs[:7]
        dq_ref, dk_ref, dv_ref, dcol_ref, drow_ref = refs[7 + n:12 + n]
        exchange = (refs[7:7 + n], refs[12 + n:12 + 2 * n], refs[12 + 2 * n:len(refs) - 5], True)
        sd_scr, pd_scr, acc_scr = refs[-5:-3], refs[-3:-1], refs[-1]
        kj = pl.program_id(1)

        @pl.when(jnp.logical_and(pl.program_id(0) == 0, kj == 0))
        def _():
            for cp in _exchange_copies(*exchange):
                cp.start()

        @pl.when(kj == 0)
        def _():
            dq_ref[...] = jnp.zeros_like(dq_ref)
            drow_ref[...] = jnp.zeros_like(drow_ref)

        lane, masks = _lane_masks()
        k = k_ref[...]
        v = v_ref[...]
        k_aug = [jnp.where(masks[e], k * SCALE, jnp.ones_like(k)) for e in range(2)]
        cs = [ct_ref[e:e + 1, :] for e in range(2)]

        def rows_of(i):
            r0 = pl.multiple_of(i * t, t)
            return pl.ds(r0, t), q_ref[pl.ds(r0, t), :] * SCALE, do_ref[pl.ds(r0, t), :]

        def scores_into(i, slot):
            _, qb, dob = rows_of(i)
            for e in range(2):
                qe = jnp.where(masks[e], qb, jnp.zeros_like(qb))
                doe = jnp.where(masks[e], dob, jnp.zeros_like(dob))
                sd_scr[slot][2 * e] = _dot(qe, k, NT_DIMS) - cs[e]
                sd_scr[slot][2 * e + 1] = _dot(doe, v, NT_DIMS)

        def pointwise(i, slot, masked):
            rows, _, _ = rows_of(i)
            for e in range(2):
                lo = e * HEAD_DIM
                s = sd_scr[slot][2 * e]
                if masked:
                    r = lax.broadcasted_iota(jnp.int32, (t, t), 0)
                    c = lax.broadcasted_iota(jnp.int32, (t, t), 1)
                    s = jnp.where(c <= r, s, NEG)
                p = jnp.exp(s - lse_ref[rows, lo:lo + 1])
                pd_scr[slot][2 * e] = p.astype(BF16)
                pd_scr[slot][2 * e + 1] = (p * (sd_scr[slot][2 * e + 1] - dl_ref[rows, lo:lo + 1])).astype(BF16)

        def accumulate(i, slot):
            rows, qb, dob = rows_of(i)
            dq_parts = []
            for e in range(2):
                p, ds = pd_scr[slot][2 * e], pd_scr[slot][2 * e + 1]
                q_aug = jnp.where(masks[e], qb, jnp.ones_like(qb))
                doe = jnp.where(masks[e], dob, jnp.zeros_like(dob))
                acc_scr[2] += _dot(p, doe, TN_DIMS)
                acc_scr[e] += _dot(ds, q_aug, TN_DIMS)
                dq_parts.append(_dot(ds, k_aug[e]))
            dq_ref[rows, :] += jnp.where(masks[0], dq_parts[0], dq_parts[1])
            drow_ref[rows, :] += jnp.where(masks[0], dq_parts[1], dq_parts[0])

        def stage(i, cur, nxt):
            scores_into(jnp.minimum(i + 1, nq - 1), nxt)
            accumulate(i - 1, nxt)
            pointwise(i, cur, False)

        acc_scr[...] = jnp.zeros_like(acc_scr)
        scores_into(kj, 0)
        pointwise(kj, 0, True)
        scores_into(jnp.minimum(kj + 1, nq - 1), 1)
        rest = nq - 1 - kj

        def two_stages(jj, carry):
            stage(kj + 1 + 2 * jj, 1, 0)
            stage(kj + 2 + 2 * jj, 0, 1)
            return carry

        lax.fori_loop(0, jnp.right_shift(rest, 1), two_stages, 0)
        odd = jnp.bitwise_and(rest, 1) == 1

        @pl.when(odd)
        def _():
            stage(nq - 1, 1, 0)
            accumulate(nq - 1, 1)

        @pl.when(jnp.logical_not(odd))
        def _():
            accumulate(nq - 1, 0)

        dk0, dk1, dv = acc_scr[0], acc_scr[1], acc_scr[2]
        dk_ref[...] = jnp.where(masks[0], dk0, dk1).astype(BF16)
        dcol_ref[...] = jnp.where(masks[0], dk1, dk0)
        dv_ref[...] = dv.astype(BF16)

        @pl.when(jnp.logical_and(pl.program_id(0) == N_PAIRS - 1, kj == nq - 1))
        def _():
            for cp in _exchange_copies(*exchange):
                cp.wait()

    full = lambda off: pl.BlockSpec((S, LANES), lambda h, j, off=off: (0, off + h))
    kv = lambda off: pl.BlockSpec((t, LANES), lambda h, j, off=off: (j, off + h))
    return pl.pallas_call(
        body, name=name, grid=(N_PAIRS, nq),
        out_shape=[jax.ShapeDtypeStruct((S, D), F32), jax.ShapeDtypeStruct((S, D), BF16),
                   jax.ShapeDtypeStruct((S, D), BF16), jax.ShapeDtypeStruct((S, D), F32),
                   jax.ShapeDtypeStruct((S, D), F32)] + _exchange_shapes(scatter, True),
        in_specs=[full(0), kv(N_PAIRS), kv(2 * N_PAIRS), full(0), full(0), full(0),
                  pl.BlockSpec((None, 2, t), lambda h, j: (h, 0, j))] + [ANY] * n,
        out_specs=[full(0), kv(0), kv(0), kv(0), full(0)] + [ANY] * n,
        scratch_shapes=_exchange_scratch(n) + [
            pltpu.VMEM((4, t, t), F32), pltpu.VMEM((4, t, t), F32),
            pltpu.VMEM((4, t, t), BF16), pltpu.VMEM((4, t, t), BF16),
            pltpu.VMEM((3, t, LANES), F32)],
        compiler_params=_params("arbitrary", "arbitrary"),
    )(qkv, qkv, qkv, do, lse, delta, c_t2, *scatter)


B_PAIRS = 4
B_NB = S // B_W


def _group_consts(g):
    nbs = jnp.where(g == 0, B_NB // B_DILS[0], jnp.where(g == 1, B_NB // B_DILS[1], B_NB // B_DILS[2]))
    dil = jnp.where(g == 0, B_DILS[0], jnp.where(g == 1, B_DILS[1], B_DILS[2]))
    return nbs, dil


def _band(dil):
    qi = lax.broadcasted_iota(jnp.int32, (B_W, B_W), 0)
    kj = lax.broadcasted_iota(jnp.int32, (B_W, B_W), 1)
    dist_c = qi - kj
    dist_p = qi + B_W - kj
    return (dist_c * dil).astype(F32), dist_c >= 0, (dist_p * dil).astype(F32), dist_p <= B_W


def _dil_fwd(qp, kp, vp, slopes, name):
    def body(sl_ref, q_ref, kp_ref, kc_ref, vp_ref, vc_ref, o_ref, lse_ref):
        g, n = pl.program_id(0), pl.program_id(1)
        nbs, dil = _group_consts(g)
        has_prev = (n % nbs) != 0
        lane, masks = _lane_masks()
        bias_c, ok_c, bias_p, ok_p = _band(dil)
        ok_p = jnp.logical_and(ok_p, has_prev)
        heads = [(hp, e) for hp in range(B_PAIRS) for e in range(2)]
        col = lambda ref, hp: ref[:, hp * LANES:(hp + 1) * LANES]
        logits = []
        for hp, e in heads:
            q = col(q_ref, hp) * SCALE
            qe = jnp.where(masks[e], q, jnp.zeros_like(q))
            logits.append((_dot(qe, col(kc_ref, hp), NT_DIMS), _dot(qe, col(kp_ref, hp), NT_DIMS)))
        probs = []
        for (hp, e), (sc, sp) in zip(heads, logits):
            slope = sl_ref[g * 8 + 2 * hp + e]
            sc = jnp.where(ok_c, sc - slope * bias_c, NEG)
            sp = jnp.where(ok_p, sp - slope * bias_p, NEG)
            m = jnp.maximum(jnp.max(sc, axis=1, keepdims=True), jnp.max(sp, axis=1, keepdims=True))
            probs.append((jnp.exp(sc - m).astype(BF16), jnp.exp(sp - m).astype(BF16), m))
        outs, lses = [], []
        for (hp, e), (pc, pp, m) in zip(heads, probs):
            vc, vpv = col(vc_ref, hp), col(vp_ref, hp)
            acc = (_dot(pc, jnp.where(masks[e], vc, jnp.ones_like(vc)))
                   + _dot(pp, jnp.where(masks[e], vpv, jnp.ones_like(vpv))))
            l = acc[:, HEAD_DIM:HEAD_DIM + 1] if e == 0 else acc[:, 0:1]
            outs.append(acc / l)
            lses.append(m + jnp.log(l))
        o_ref[...] = jnp.concatenate(
            [jnp.where(masks[0], outs[2 * hp], outs[2 * hp + 1]) for hp in range(B_PAIRS)], axis=1).astype(BF16)
        lse = jnp.zeros((B_W, LANES), F32)
        for h in range(2 * B_PAIRS):
            lse = jnp.where(lane == h, lses[h], lse)
        lse_ref[...] = lse

    cur = pl.BlockSpec((None, B_W, B_OUT), lambda g, n, sl: (g, n, 0))
    prev = pl.BlockSpec((None, B_W, B_OUT), lambda g, n, sl: (g, jnp.maximum(n - 1, 0), 0))
    stat = pl.BlockSpec((None, B_W, LANES), lambda g, n, sl: (g, n, 0))
    return pl.pallas_call(
        body, name=name,
        grid_spec=pltpu.PrefetchScalarGridSpec(
            num_scalar_prefetch=1, grid=(3, B_NB),
            in_specs=[cur, prev, cur, prev, cur], out_specs=[cur, stat]),
        out_shape=[jax.ShapeDtypeStruct((3, S, B_OUT), BF16), jax.ShapeDtypeStruct((3, S, LANES), F32)],
        compiler_params=_params("parallel", "parallel"),
    )(slopes, qp, kp, kp, vp, vp)


def _head_expander():
    r = lax.broadcasted_iota(jnp.int32, (LANES, B_OUT), 0)
    c = lax.broadcasted_iota(jnp.int32, (LANES, B_OUT), 1)
    return jnp.logical_and(c >= r * HEAD_DIM, c < (r + 1) * HEAD_DIM).astype(BF16)


def _dil_merge(og, lseg, name, tr=256):
    def body(o_ref, l_ref, out_ref, lse_ref):
        l0, l1, l2 = l_ref[0], l_ref[1], l_ref[2]
        m = jnp.maximum(jnp.maximum(l0, l1), l2)
        w0, w1, w2 = jnp.exp(l0 - m), jnp.exp(l1 - m), jnp.exp(l2 - m)
        den = w0 + w1 + w2
        lse_ref[...] = m + jnp.log(den)
        expand = _head_expander()
        out = None
        for g, w in enumerate((w0, w1, w2)):
            part = _split_dot(w / den, expand, 3) * o_ref[g].astype(F32)
            out = part if out is None else out + part
        out_ref[...] = out.astype(BF16)

    blk3 = pl.BlockSpec((3, tr, B_OUT), lambda i: (0, i, 0))
    stat3 = pl.BlockSpec((3, tr, LANES), lambda i: (0, i, 0))
    blk = pl.BlockSpec((tr, B_OUT), lambda i: (i, 0))
    stat = pl.BlockSpec((tr, LANES), lambda i: (i, 0))
    return pl.pallas_call(
        body, name=name, grid=(S // tr,),
        out_shape=[jax.ShapeDtypeStruct((S, B_OUT), BF16), jax.ShapeDtypeStruct((S, LANES), F32)],
        in_specs=[blk3, stat3], out_specs=[blk, stat], compiler_params=_params("parallel"),
    )(og, lseg)


def _head_rowsum_compact(a, b, name, tr=256):
    def body(a_ref, b_ref, o_ref):
        r = lax.broadcasted_iota(jnp.int32, (B_OUT, LANES), 0)
        c = lax.broadcasted_iota(jnp.int32, (B_OUT, LANES), 1)
        collect = jnp.logical_and(r >= c * HEAD_DIM, r < (c + 1) * HEAD_DIM).astype(BF16)
        prod = a_ref[...].astype(F32) * b_ref[...].astype(F32)
        o_ref[...] = _split_dot(prod, collect, 2)

    row = pl.BlockSpec((tr, B_OUT), lambda i: (i, 0))
    return pl.pallas_call(
        body, name=name, grid=(S // tr,), out_shape=jax.ShapeDtypeStruct((S, LANES), F32),
        in_specs=[row, row], out_specs=pl.BlockSpec((tr, LANES), lambda i: (i, 0)),
        compiler_params=_params("parallel"),
    )(a, b)


def _dil_bwd(qp, kp, vp, dop, lsep, dlp, slopes, name, scatter):
    n_ex = len(scatter)

    def body(sl_ref, *refs):
        (qc_ref, qn_ref, kp_ref, kc_ref, vp_ref, vc_ref, doc_ref, don_ref,
         lc_ref, ln_ref, dc_ref, dn_ref) = refs[:12]
        dq_ref, dk_ref, dv_ref = refs[12 + n_ex:15 + n_ex]
        exchange = (refs[12:12 + n_ex], refs[15 + n_ex:15 + 2 * n_ex], refs[15 + 2 * n_ex:], True)
        g, n = pl.program_id(0), pl.program_id(1)

        @pl.when(jnp.logical_and(g == 0, n == 0))
        def _():
            for cp in _exchange_copies(*exchange):
                cp.start()

        nbs, dil = _group_consts(g)
        has_prev = (n % nbs) != 0
        has_next = jnp.logical_and(n + 1 < B_NB, ((n + 1) % nbs) != 0)
        lane, masks = _lane_masks()
        bias_c, ok_c, bias_p, ok_p = _band(dil)
        ok_pp = jnp.logical_and(ok_p, has_prev)
        ok_np = jnp.logical_and(ok_p, has_next)
        heads = [(hp, e) for hp in range(B_PAIRS) for e in range(2)]
        col = lambda ref, hp: ref[:, hp * LANES:(hp + 1) * LANES]
        mask = lambda t, e: jnp.where(masks[e], t, jnp.zeros_like(t))
        raw = []
        for hp, e in heads:
            qce, qne = mask(col(qc_ref, hp) * SCALE, e), mask(col(qn_ref, hp) * SCALE, e)
            doce, done = mask(col(doc_ref, hp), e), mask(col(don_ref, hp), e)
            kc, kpv, vc, vpv = col(kc_ref, hp), col(kp_ref, hp), col(vc_ref, hp), col(vp_ref, hp)
            raw.append(((_dot(qce, kc, NT_DIMS), _dot(doce, vc, NT_DIMS)),
                        (_dot(qce, kpv, NT_DIMS), _dot(doce, vpv, NT_DIMS)),
                        (_dot(qne, kc, NT_DIMS), _dot(done, vc, NT_DIMS))))
        pds = []
        for (hp, e), tiles in zip(heads, raw):
            lo = 2 * hp + e
            slope = sl_ref[g * 8 + 2 * hp + e]
            lse_c, dl_c = lc_ref[:, lo:lo + 1], dc_ref[:, lo:lo + 1]
            lse_n, dl_n = ln_ref[:, lo:lo + 1], dn_ref[:, lo:lo + 1]
            out = []
            for (s, dp), ok, bias, lse, dl in ((tiles[0], ok_c, bias_c, lse_c, dl_c),
                                               (tiles[1], ok_pp, bias_p, lse_c, dl_c),
                                               (tiles[2], ok_np, bias_p, lse_n, dl_n)):
                p = jnp.exp(jnp.where(ok, s - slope * bias, NEG) - lse)
                out.append((p.astype(BF16), (p * (dp - dl)).astype(BF16)))
            pds.append(out)
        dq_all, dk_all, dv_all = [], [], []
        for hp in range(B_PAIRS):
            dq = jnp.zeros((B_W, LANES), F32)
            dk = jnp.zeros((B_W, LANES), F32)
            dv = jnp.zeros((B_W, LANES), F32)
            for e in range(2):
                (p_c, ds_c), (_, ds_p), (p_n, ds_n) = pds[2 * hp + e]
                qce, qne = mask(col(qc_ref, hp) * SCALE, e), mask(col(qn_ref, hp) * SCALE, e)
                doce, done = mask(col(doc_ref, hp), e), mask(col(don_ref, hp), e)
                dq = dq + _dot(ds_c, mask(col(kc_ref, hp) * SCALE, e)) + _dot(ds_p, mask(col(kp_ref, hp) * SCALE, e))
                dk = dk + _dot(ds_c, qce, TN_DIMS) + _dot(ds_n, qne, TN_DIMS)
                dv = dv + _dot(p_c, doce, TN_DIMS) + _dot(p_n, done, TN_DIMS)
            dq_all.append(dq)
            dk_all.append(dk)
            dv_all.append(dv)
        dq_ref[...] = jnp.concatenate(dq_all, axis=1).astype(BF16)
        dk_ref[...] = jnp.concatenate(dk_all, axis=1).astype(BF16)
        dv_ref[...] = jnp.concatenate(dv_all, axis=1).astype(BF16)

        @pl.when(jnp.logical_and(g == 2, n == B_NB - 1))
        def _():
            for cp in _exchange_copies(*exchange):
                cp.wait()

    cur = pl.BlockSpec((None, B_W, B_OUT), lambda g, n, sl: (g, n, 0))
    prev = pl.BlockSpec((None, B_W, B_OUT), lambda g, n, sl: (g, jnp.maximum(n - 1, 0), 0))
    nxt = pl.BlockSpec((None, B_W, B_OUT), lambda g, n, sl: (g, jnp.minimum(n + 1, B_NB - 1), 0))
    stat_cur = pl.BlockSpec((None, B_W, LANES), lambda g, n, sl: (g, n, 0))
    stat_nxt = pl.BlockSpec((None, B_W, LANES), lambda g, n, sl: (g, jnp.minimum(n + 1, B_NB - 1), 0))
    return pl.pallas_call(
        body, name=name,
        grid_spec=pltpu.PrefetchScalarGridSpec(
            num_scalar_prefetch=1, grid=(3, B_NB),
            in_specs=[cur, nxt, prev, cur, prev, cur, cur, nxt, stat_cur, stat_nxt, stat_cur, stat_nxt]
            + [ANY] * n_ex,
            out_specs=[cur, cur, cur] + [ANY] * n_ex,
            scratch_shapes=_exchange_scratch(n_ex)),
        out_shape=[jax.ShapeDtypeStruct((3, S, B_OUT), BF16)] * 3 + _exchange_shapes(scatter, True),
        compiler_params=_params("arbitrary", "arbitrary"),
    )(slopes, qp, qp, kp, kp, vp, vp, dop, dop, lsep, lsep, dlp, dlp, *scatter)


def _rows_block(shape, max_bytes=2 * 1024 * 1024):
    rows, cols = shape
    padded_cols = -(-cols // LANES) * LANES
    for tr in (1024, 512, 256, 128, 64, 32, 16):
        if rows % tr == 0 and tr * padded_cols * 4 <= max_bytes:
            return tr
    return rows


def _adam_update(w, m, v, g):
    m_new = ADAM_B1 * m + (1.0 - ADAM_B1) * g
    v_new = ADAM_B2 * v + (1.0 - ADAM_B2) * (g * g)
    m_hat = m_new / (1.0 - ADAM_B1 ** ADAM_STEP)
    v_hat = v_new / (1.0 - ADAM_B2 ** ADAM_STEP)
    delta = -ADAM_LR * (m_hat / (jnp.sqrt(v_hat) + ADAM_EPS) + ADAM_WD * w)
    return delta, m_new, v_new


def _adamw_sharded(w, m, v, parts, name):
    R, C = w.shape
    tr = _rows_block((R, C), max_bytes=1024 * 1024)

    def body(w_ref, m_ref, v_ref, p_ref, g_ref, d_ref, mo_ref, vo_ref):
        g = p_ref[0].astype(F32)
        for dev in range(1, N_DEV):
            g = g + p_ref[dev].astype(F32)
        g_ref[...] = g
        d_ref[...], mo_ref[...], vo_ref[...] = _adam_update(w_ref[...], m_ref[...], v_ref[...], g)

    blk = pl.BlockSpec((tr, C), lambda i: (i, 0))
    out = jax.ShapeDtypeStruct((R, C), F32)
    return pl.pallas_call(
        body, name=name, grid=(R // tr,),
        in_specs=[blk, blk, blk, pl.BlockSpec((N_DEV, tr, C), lambda i: (0, i, 0))],
        out_specs=[blk, blk, blk, blk], out_shape=[out, out, out, out],
        compiler_params=_params("parallel"),
    )(w, m, v, parts)


def _adamw_replicated(w, m, v, parts, name):
    def body(w_ref, m_ref, v_ref, p_ref, g_ref, d_ref, mo_ref, vo_ref):
        g = p_ref[0]
        for dev in range(1, N_DEV):
            g = g + p_ref[dev]
        g_ref[...] = g
        d_ref[...], mo_ref[...], vo_ref[...] = _adam_update(w_ref[...], m_ref[...], v_ref[...], g)

    out = jax.ShapeDtypeStruct(w.shape, F32)
    return pl.pallas_call(body, name=name, out_shape=[out, out, out, out], compiler_params=_params())(w, m, v, parts)


def _cols_from_slots(g):
    return g.transpose(1, 0, 2).reshape(g.shape[1], N_DEV * g.shape[2])


def _cols_to_slots(w):
    k, n = w.shape
    return w.reshape(k, N_DEV, n // N_DEV).transpose(1, 0, 2)


def _permute(t, dil):
    c = t.shape[1]
    return t.reshape(S // dil, dil, c).transpose(1, 0, 2).reshape(S, c)


def _unpermute(t, dil):
    c = t.shape[1]
    return t.reshape(dil, S // dil, c).transpose(1, 0, 2).reshape(S, c)


def _group_permute(t):
    return jnp.stack([_permute(t[:, g * B_OUT:(g + 1) * B_OUT], B_DILS[g]) for g in range(3)])


def _same_permute(t):
    return jnp.stack([_permute(t, d) for d in B_DILS])


def _group_unpermute(t):
    return jnp.stack([_unpermute(t[g], B_DILS[g]) for g in range(3)])


SMALL_ROWS = 144


LOSS_SLOT = A_HEADS + 6 * D + 4 * D_FF


def _pack_small(a_b_f, kv_g, mix_g, ffn_g, conv_b, fin_g, loss=None):
    parts = [a_b_f.reshape(-1), kv_g.reshape(-1), mix_g.reshape(-1), ffn_g.reshape(-1),
             conv_b.reshape(-1), fin_g.reshape(-1)] + ([loss.reshape(-1)] if loss is not None else [])
    flat = jnp.concatenate(parts)
    return jnp.pad(flat, (0, SMALL_ROWS * LANES - flat.shape[0])).reshape(SMALL_ROWS, LANES)


def _unpack_small(p):
    flat = p.reshape(-1)
    out, off = [], 0
    for shape in ((1, A_HEADS), (D,), (2, D), (2, D), (2, 2 * D_FF), (D,)):
        size = math.prod(shape)
        out.append(flat[off:off + size].reshape(shape))
        off += size
    return out


def _unpack_late(g):
    half = N_DEV // 2
    up = g[4].reshape(N_DEV, 2, D, -1)
    w_up_a = [up[:half, l].transpose(1, 0, 2).reshape(D, D_FF) for l in range(2)]
    w_up_g = [up[half:, l].transpose(1, 0, 2).reshape(D, D_FF) for l in range(2)]
    w_down = [g[5].reshape(N_DEV, 2, -1, D)[:, l].reshape(D_FF, D) for l in range(2)]
    conv_w = [g[6].reshape(N_DEV, 2, 3, -1)[:, l].transpose(1, 0, 2).reshape(3, 2 * D_FF) for l in range(2)]
    return (g[0].reshape(D, D), _cols_from_slots(g[1]), _cols_from_slots(g[2]), _cols_from_slots(g[3]),
            w_up_a, w_up_g, w_down, conv_w)


def _ffn_slots(dw_up, dw_down_t, dconv_w):
    return [_cols_to_slots(dw_up), dw_down_t.reshape(D, N_DEV, -1).transpose(1, 2, 0), _cols_to_slots(dconv_w)]


def _local_step(x0, target, w_in_pad, late_shards,
                a_b_f, kv_norm_g, mix_norm_g, ffn_norm_g, ffn_conv_b, final_norm_g):
    w_qkv, w_f = w_in_pad[:, :A_QKV], w_in_pad[:, A_QKV:]
    conv_b = ffn_conv_b.reshape(2, 1, 2 * D_FF)
    slopes = jnp.exp2(-8.0 * jnp.arange(1, 25, dtype=F32) / 24)

    def gain(g):
        return g.reshape(1, D)

    (h1,) = _rmsnorm_fwd(x0, [gain(mix_norm_g[0])], "norm_mix0")
    qkv = _matmul(h1, w_qkv, mode="nn", out_dtype=BF16, name="proj_qkv", tm=512, tn=A_QKV)
    z = _matmul(h1, w_f, mode="nn", out_dtype=F32, name="proj_gate", tm=S, tn=LANES)
    z_t = z[:, :A_HEADS].T
    b_f = a_b_f.reshape(A_HEADS, 1)
    c_t = _fox_prep_fwd(z_t, b_f, "fox_prep")
    c_t2 = c_t.reshape(N_PAIRS, 2, S)
    o_a, lse_a, *late = _fox_fwd(qkv, c_t2, "fox_fwd", late_shards)
    w_out, w_q, w_bo, w_kvf, w_up_a, w_up_g, w_down, conv_w = _unpack_late(late)
    x1 = _matmul(o_a, w_out, mode="nn", out_dtype=F32, name="a_out", tm=512, tn=D, res=x0)

    def ffn_fwd(xin, layer):
        (h,) = _rmsnorm_fwd(xin, [gain(ffn_norm_g[layer])], f"norm_ffn{layer}")
        u = (_matmul(h, w_up_a[layer], mode="nn", out_dtype=BF16, name=f"ffn_up_a{layer}", tm=512, tn=D_FF),
             _matmul(h, w_up_g[layer], mode="nn", out_dtype=BF16, name=f"ffn_up_g{layer}", tm=512, tn=D_FF))
        act = _convgate_fwd(*u, conv_w[layer], conv_b[layer], f"convgate{layer}")
        xout = _matmul(act, w_down[layer], mode="nn", out_dtype=F32, name=f"ffn_down{layer}", tm=512, tn=D, res=xin)
        return h, u, act, xout

    h2, u0, act0, x2 = ffn_fwd(x1, 0)
    hk, h3 = _rmsnorm_fwd(x2, [gain(kv_norm_g), gain(mix_norm_g[1])], "norm_kv_mix1")
    kv = _matmul(hk, w_kvf, mode="nn", out_dtype=BF16, name="proj_kv", tm=512, tn=B_KV)
    qb = _matmul(h3, w_q, mode="nn", out_dtype=BF16, name="proj_qb", tm=512, tn=B_Q)
    qp, kp, vp = _group_permute(qb), _group_permute(kv[:, :B_Q]), _group_permute(kv[:, B_Q:])
    og_p, lseg_p = _dil_fwd(qp, kp, vp, slopes, "dil_fwd")
    o_b, lse_b = _dil_merge(_group_unpermute(og_p), _group_unpermute(lseg_p), "dil_merge")
    x3 = _matmul(o_b, w_bo, mode="nn", out_dtype=F32, name="b_out", tm=512, tn=D, res=x2)
    h4, u1, act1, x4 = ffn_fwd(x3, 1)
    loss_blk, dx4, dx4b, dg_final = _final_loss(x4, target, gain(final_norm_g), "final_loss")

    def ffn_bwd(dx, dxb, xin, h, u, act, layer):
        dact = _matmul(dxb, w_down[layer], mode="nt", out_dtype=BF16, name=f"d_act{layer}", tm=512, tn=D_FF)
        dw_down = _matmul_tn(dxb, act, out_dtype=BF16, name=f"dw_down_t{layer}")
        du_a, du_g, dwa, dwg, dba, dbg = _convgate_bwd(*u, conv_w[layer], conv_b[layer], dact, f"convgate_bwd{layer}")
        dw_up = jnp.concatenate(
            [_matmul_tn(h, du_a, out_dtype=BF16, name=f"dw_up_a{layer}"),
             _matmul_tn(h, du_g, out_dtype=BF16, name=f"dw_up_g{layer}")], axis=1)
        dxin, dxinb, dgain = _matmul_norm_bwd([(du_a, w_up_a[layer]), (du_g, w_up_g[layer])], xin,
                                              gain(ffn_norm_g[layer]), dx, f"dh_ffn_norm_bwd{layer}")
        dconv_w = jnp.concatenate([dwa, dwg], axis=1)
        dconv_b = jnp.concatenate([dba, dbg], axis=1)
        return dxin, dxinb, dgain, dw_up, dw_down, dconv_w, dconv_b

    dx3, dx3b, dg_ffn1, dw_up1, dw_down1, dconv_w1, dconv_b1 = ffn_bwd(dx4, dx4b, x3, h4, u1, act1, 1)

    do_b = _matmul(dx3b, w_bo, mode="nt", out_dtype=BF16, name="d_ob", tm=1024, tn=B_OUT)
    dw_bo = _matmul_tn(o_b, dx3b, out_dtype=BF16, name="dw_bo")
    dl_b = _head_rowsum_compact(do_b, o_b, "delta_b")
    slots_up1, slots_down1, slots_conv1 = _ffn_slots(dw_up1, dw_down1, dconv_w1)
    dqp, dkp, dvp, land_down1, land_conv1 = _dil_bwd(
        qp, kp, vp, _same_permute(do_b), _same_permute(lse_b), _same_permute(dl_b), slopes, "dil_bwd",
        [slots_down1, slots_conv1])

    def natural(tp):
        return jnp.concatenate([_unpermute(tp[g], B_DILS[g]) for g in range(3)], axis=1)

    dqb = natural(dqp)
    dkv = jnp.concatenate([natural(dkp), natural(dvp)], axis=1)
    dw_q = _matmul_tn(h3, dqb, out_dtype=BF16, name="dw_q")
    dw_kv = _matmul_tn(hk, dkv, out_dtype=BF16, name="dw_kv")
    dx2, _, dg_mix1 = _matmul_norm_bwd([(dqb, w_q)], x2, gain(mix_norm_g[1]), dx3, "dh_mix1_norm_bwd")
    dx2, dx2b, dg_kv = _matmul_norm_bwd([(dkv, w_kvf)], x2, gain(kv_norm_g), dx2, "dh_kv_norm_bwd")

    dx1, dx1b, dg_ffn0, dw_up0, dw_down0, dconv_w0, dconv_b0 = ffn_bwd(dx2, dx2b, x1, h2, u0, act0, 0)

    do_a = _matmul(dx1b, w_out, mode="nt", out_dtype=BF16, name="d_oa", tm=512, tn=D)
    dw_out = _matmul_tn(o_a, dx1b, out_dtype=BF16, name="dw_out")
    dl_a = _head_rowsum(do_a, o_a, "delta_a")
    dq_a, dk_a, dv_a, dcol, drow, *land = _fox_bwd(
        qkv, do_a, lse_a, dl_a, c_t2, "fox_bwd",
        [dw_out.reshape(N_DEV, D // N_DEV, D), _cols_to_slots(dw_q), _cols_to_slots(dw_bo), _cols_to_slots(dw_kv)]
        + _ffn_slots(dw_up0, dw_down0, dconv_w0) + [slots_up1])
    land_out, land_q, land_bo, land_kv, land_up0, land_down0, land_conv0, land_up1 = land

    def head_sums(t):
        return t.reshape(S, N_PAIRS, 2, HEAD_DIM)[:, :, ::-1, 0].reshape(S, A_HEADS).T

    dz_t, db_f = _fox_prep_bwd(head_sums(drow), head_sums(dcol), z_t, b_f, "fox_prep_bwd")
    dz = jnp.pad(dz_t.T, ((0, 0), (0, LANES - A_HEADS))).astype(BF16)
    dproj = jnp.concatenate([dq_a.astype(BF16), dk_a, dv_a, dz], axis=1)
    dw_in = _matmul_tn(h1, dproj, out_dtype=BF16, name="dw_in")
    grad_x, _, dg_mix0, land_in = _matmul_norm_bwd(
        [(dproj, w_in_pad)], x0, gain(mix_norm_g[0]), dx1, "dh_mix0_norm_bwd",
        scatter=[_cols_to_slots(dw_in[:, :A_QKV + A_HEADS])])

    dg_mix = jnp.concatenate([dg_mix0, dg_mix1], axis=0)
    dg_ffn = jnp.concatenate([dg_ffn0, dg_ffn1], axis=0)
    dconv_b = jnp.concatenate([dconv_b0, dconv_b1], axis=0)
    small_part = _pack_small(db_f, dg_kv, dg_mix, dg_ffn, dconv_b, dg_final, loss=loss_blk[0, 0])
    _, (small_parts,) = _final_exchange([], [small_part], "gather_small_grads")
    landed = [land_in, land_out, land_q, land_bo, land_kv, land_up0, land_up1, land_down0, land_down1,
              land_conv0, land_conv1]
    return loss_blk, grad_x, landed, small_parts


def kernel(x, a_w_in, a_b_f, a_w_out, b_w_q, b_w_out, kv_norm_g, w_kv, mix_norm_g, ffn_norm_g, ffn_w_up, ffn_conv_w, ffn_conv_b, ffn_w_down, final_norm_g, loss_target, m_a_w_in, m_a_b_f, m_a_w_out, m_b_w_q, m_b_w_out, m_kv_norm_g, m_w_kv, m_mix_norm_g, m_ffn_norm_g, m_ffn_w_up, m_ffn_conv_w, m_ffn_conv_b, m_ffn_w_down, m_final_norm_g, v_a_w_in, v_a_b_f, v_a_w_out, v_b_w_q, v_b_w_out, v_kv_norm_g, v_w_kv, v_mix_norm_g, v_ffn_norm_g, v_ffn_w_up, v_ffn_conv_w, v_ffn_conv_b, v_ffn_w_down, v_final_norm_g):
    def shards(a_w_in, a_w_out, b_w_q, b_w_out, w_kv, ffn_w_up, ffn_w_down, ffn_conv_w):
        return [a_w_in[0], a_w_out[0], b_w_q[0], b_w_out[0], w_kv, ffn_w_up[0], ffn_w_up[1],
                ffn_w_down[0], ffn_w_down[1], ffn_conv_w[0], ffn_conv_w[1]]

    w_loc = shards(a_w_in, a_w_out, b_w_q, b_w_out, w_kv, ffn_w_up, ffn_w_down, ffn_conv_w)
    m_loc = shards(m_a_w_in, m_a_w_out, m_b_w_q, m_b_w_out, m_w_kv, m_ffn_w_up, m_ffn_w_down, m_ffn_conv_w)
    v_loc = shards(v_a_w_in, v_a_w_out, v_b_w_q, v_b_w_out, v_w_kv, v_ffn_w_up, v_ffn_w_down, v_ffn_conv_w)

    (g_in,) = _all_gather([a_w_in[0].astype(BF16)], "gather_a_w_in")
    w_in = _cols_from_slots(g_in)
    w_in_pad = jnp.pad(w_in, ((0, 0), (0, A_PROJ_PAD - w_in.shape[1])))
    late_shards = [a_w_out[0].astype(BF16), b_w_q[0].astype(BF16), b_w_out[0].astype(BF16), w_kv.astype(BF16),
                   ffn_w_up.reshape(2 * D, -1).astype(BF16), ffn_w_down.reshape(-1, D).astype(BF16),
                   ffn_conv_w.reshape(6, -1)]

    loss_blk, grad_x, landed, small_parts = _local_step(
        x[0], loss_target[0], w_in_pad, late_shards,
        a_b_f, kv_norm_g, mix_norm_g, ffn_norm_g, ffn_conv_b, final_norm_g)

    big = [_adamw_sharded(w_loc[k], m_loc[k], v_loc[k], landed[k], f"adamw{k}") for k in range(11)]

    small = _adamw_replicated(
        _pack_small(a_b_f, kv_norm_g, mix_norm_g, ffn_norm_g, ffn_conv_b, final_norm_g),
        _pack_small(m_a_b_f, m_kv_norm_g, m_mix_norm_g, m_ffn_norm_g, m_ffn_conv_b, m_final_norm_g),
        _pack_small(v_a_b_f, v_kv_norm_g, v_mix_norm_g, v_ffn_norm_g, v_ffn_conv_b, v_final_norm_g),
        small_parts, "adamw_small")

    loss = small[0].reshape(-1)[LOSS_SLOT]

    def assemble(kind):
        b = [r[kind] for r in big]
        s_abf, s_kv, s_mix, s_ffn, s_cb, s_fin = _unpack_small(small[kind])
        return [b[0][None], s_abf, b[1][None], b[2][None], b[3][None], s_kv, b[4], s_mix, s_ffn,
                jnp.stack([b[5], b[6]]), jnp.stack([b[9], b[10]]), s_cb, jnp.stack([b[7], b[8]]), s_fin]

    return (loss, grad_x[None], *assemble(0), *assemble(1), *assemble(2), *assemble(3))
```
